```python
import math
import jax, jax.numpy as jnp
from jax import lax
import numpy as np

D_MODEL = 1024
BATCH = 8
SEQ = 4096
DEPTH = 1

SSM_WIDTH = D_MODEL // 2
SSM_GROUP = 16
SSM_GROUPS = SSM_WIDTH // SSM_GROUP
SSM_STATE = 64
DT_MIN = 1e-3
DT_MAX = 1e-1
ATTN_HEADS = 8
ATTN_KV_HEADS = 2
HEAD_DIM = 64
ATTN_WIDTH = ATTN_HEADS * HEAD_DIM
IDX_HEADS = 8
IDX_DIM = 32
TOPK_MAX = 256
Q_BLOCK = 128
ROPE_THETA = 10000.0
NEG_BIG = -1e30
PEER_HEADS = 8
PEER_KEYS = 128
PEER_EXPERTS = PEER_KEYS * PEER_KEYS
PEER_KEY_DIM = 128
PEER_TOPK = 16
PEER_CHUNK = 128
NORM_EPS = 1e-6

IN_SPLITS = (SSM_WIDTH, ATTN_WIDTH, ATTN_KV_HEADS * HEAD_DIM, ATTN_KV_HEADS * HEAD_DIM,
             IDX_HEADS * IDX_DIM, IDX_DIM, IDX_HEADS, D_MODEL, D_MODEL)
IN_WIDTH = sum(IN_SPLITS)

kernel_name = "hybrid_s5_dsa_peer_block"


def rmsnorm(x, g):
    xf = x.astype(jnp.float32)
    y = xf * lax.rsqrt(jnp.mean(xf * xf, axis=-1, keepdims=True) + NORM_EPS)
    return (y * g.astype(jnp.float32)).astype(x.dtype)


def rope_tables(seq, dim):
    pos = jnp.arange(seq, dtype=jnp.float32)
    inv = ROPE_THETA ** (-jnp.arange(0, dim, 2, dtype=jnp.float32) / dim)
    ang = pos[:, None] * inv[None, :]
    return jnp.cos(ang), jnp.sin(ang)


def apply_rope(x, cos, sin):
    half = x.shape[-1] // 2
    shape = (cos.shape[0],) + (1,) * (x.ndim - 3) + (half,)
    c = cos.reshape(shape)
    s = sin.reshape(shape)
    xf = x.astype(jnp.float32)
    x1, x2 = xf[..., :half], xf[..., half:]
    return jnp.concatenate([x1 * c - x2 * s, x2 * c + x1 * s], axis=-1).astype(x.dtype)


def s5_branch(u, a_re, a_im, log_dt, b_re, b_im, c_re, c_im, d_skip, w_glu):
    bsz, seq, _ = u.shape
    f32 = jnp.float32
    uf = u.astype(f32).reshape(bsz, seq, SSM_GROUPS, SSM_GROUP)
    lam = lax.complex(a_re.astype(f32), a_im.astype(f32))
    dt = jnp.exp(log_dt.astype(f32))[:, None]
    a_bar = jnp.exp(lam * dt)
    b = lax.complex(b_re.astype(f32), b_im.astype(f32))
    b_bar = ((a_bar - 1.0) / lam)[..., None] * b
    bu = jnp.einsum('gpc,bsgc->bsgp', b_bar, uf)
    a_seq = jnp.broadcast_to(a_bar, bu.shape)

    def combine(left, right):
        a1, s1 = left
        a2, s2 = right
        return a2 * a1, a2 * s1 + s2

    _, states = lax.associative_scan(combine, (a_seq, bu), axis=1)
    c = lax.complex(c_re.astype(f32), c_im.astype(f32))
    y = jnp.real(jnp.einsum('gcp,bsgp->bsgc', c, states))
    y = y + d_skip.astype(f32).reshape(SSM_GROUPS, SSM_GROUP) * uf
    y = jax.nn.gelu(y.reshape(bsz, seq, SSM_WIDTH))
    y = y * jax.nn.sigmoid(y @ w_glu.astype(f32))
    return y.astype(u.dtype)


def dsa_branch(q, k, v, qi, ki, wi):
    bsz, seq = q.shape[0], q.shape[1]
    topk = min(TOPK_MAX, seq // 4)
    nb = seq // Q_BLOCK
    grp = ATTN_HEADS // ATTN_KV_HEADS
    f32 = jnp.float32
    kf = ki.astype(f32)
    idx_scale = (IDX_HEADS ** -0.5) * (IDX_DIM ** -0.5)
    s_pos = jnp.arange(seq)

    def to_blocks(t):
        return t.reshape((bsz, nb, Q_BLOCK) + t.shape[2:]).swapaxes(0, 1)

    def one_block(args):
        qb, qib, wib, blk = args
        t_pos = blk * Q_BLOCK + jnp.arange(Q_BLOCK)
        rel = jax.nn.relu(jnp.einsum('bthd,bsd->bths', qib.astype(f32), kf))
        score = jnp.einsum('bths,bth->bts', rel, wib.astype(f32)) * idx_scale
        causal = s_pos[None, :] <= t_pos[:, None]
        score = jnp.where(causal[None], score, NEG_BIG)
        _, sel = lax.top_k(score, topk)
        valid = sel <= t_pos[None, :, None]
        kg = jax.vmap(lambda kk, ii: kk[ii])(k, sel)
        vg = jax.vmap(lambda vv, ii: vv[ii])(v, sel)
        qg = qb.reshape(bsz, Q_BLOCK, ATTN_KV_HEADS, grp, HEAD_DIM).astype(f32)
        logits = jnp.einsum('btngd,btknd->btngk', qg, kg.astype(f32)) * (HEAD_DIM ** -0.5)
        logits = jnp.where(valid[:, :, None, None, :], logits, NEG_BIG)
        p = jax.nn.softmax(logits, axis=-1)
        o = jnp.einsum('btngk,btknd->btngd', p, vg.astype(f32))
        return o.reshape(bsz, Q_BLOCK, ATTN_WIDTH).astype(qb.dtype)

    out = lax.map(one_block, (to_blocks(q), to_blocks(qi), to_blocks(wi), jnp.arange(nb)))
    return out.swapaxes(0, 1).reshape(bsz, seq, ATTN_WIDTH)


def peer_ffn(x, w_q, sub_k1, sub_k2, u_tab, v_tab):
    bsz, seq, d = x.shape
    f32 = jnp.float32
    q = (x @ w_q).astype(f32).reshape(bsz, seq, PEER_HEADS, 2, PEER_KEY_DIM)
    s1 = jnp.einsum('bshd,hnd->bshn', q[..., 0, :], sub_k1.astype(f32))
    s2 = jnp.einsum('bshd,hnd->bshn', q[..., 1, :], sub_k2.astype(f32))
    v1, i1 = lax.top_k(s1, PEER_TOPK)
    v2, i2 = lax.top_k(s2, PEER_TOPK)
    cand_s = (v1[..., :, None] + v2[..., None, :]).reshape(bsz, seq, PEER_HEADS, PEER_TOPK * PEER_TOPK)
    cand_i = (i1[..., :, None] * PEER_KEYS + i2[..., None, :]).reshape(bsz, seq, PEER_HEADS, PEER_TOPK * PEER_TOPK)
    top_s, pos = lax.top_k(cand_s, PEER_TOPK)
    experts = jnp.take_along_axis(cand_i, pos, axis=-1)
    gates = jax.nn.softmax(top_s, axis=-1)
    n_tok = bsz * seq
    n_chunks = n_tok // PEER_CHUNK
    n_sel = PEER_HEADS * PEER_TOPK
    xc_all = x.reshape(n_chunks, PEER_CHUNK, d)
    ec_all = experts.reshape(n_chunks, PEER_CHUNK, n_sel)
    gc_all = gates.reshape(n_chunks, PEER_CHUNK, n_sel)

    def chunk(args):
        xc, ec, gc = args
        u = u_tab[ec].astype(f32)
        act = jax.nn.gelu(jnp.einsum('ced,cd->ce', u, xc.astype(f32))) * gc
        return jnp.einsum('ce,ced->cd', act, v_tab[ec].astype(f32)).astype(x.dtype)

    out = lax.map(chunk, (xc_all, ec_all, gc_all))
    return out.reshape(bsz, seq, d)


def setup_inputs(seed: int = 0) -> dict:
    key = jax.random.key(seed)
    ks = jax.random.split(key, 24)
    f32 = jnp.float32
    L, D, G, P, N = DEPTH, D_MODEL, SSM_GROUPS, SSM_STATE, SSM_GROUP
    nrm = lambda k, shape, s: jax.random.normal(k, shape, f32) * s
    x = jax.random.normal(ks[0], (BATCH, SEQ, D), f32)
    norm1_g = 1.0 + nrm(ks[1], (L, D), 0.02)
    w_in = nrm(ks[2], (L, D, IN_WIDTH), D ** -0.5)
    a_re = -0.5 * (1.0 + nrm(ks[3], (L, G, P), 0.01))
    a_im = jnp.broadcast_to(math.pi * jnp.arange(P, dtype=f32), (L, G, P)) + nrm(ks[4], (L, G, P), 0.01)
    log_dt = jax.random.uniform(ks[5], (L, G), f32, math.log(DT_MIN), math.log(DT_MAX))
    b_re = nrm(ks[6], (L, G, P, N), (2.0 * N) ** -0.5)
    b_im = nrm(ks[7], (L, G, P, N), (2.0 * N) ** -0.5)
    c_re = nrm(ks[8], (L, G, N, P), P ** -0.5)
    c_im = nrm(ks[9], (L, G, N, P), P ** -0.5)
    d_skip = nrm(ks[10], (L, SSM_WIDTH), 1.0)
    w_glu = nrm(ks[11], (L, SSM_WIDTH, SSM_WIDTH), SSM_WIDTH ** -0.5)
    w_ssm_up = nrm(ks[12], (L, SSM_WIDTH, D), SSM_WIDTH ** -0.5)
    w_attn_up = nrm(ks[13], (L, ATTN_WIDTH, D), ATTN_WIDTH ** -0.5)
    w_out = nrm(ks[14], (L, D, D), D ** -0.5)
    norm2_g = 1.0 + nrm(ks[15], (L, D), 0.02)
    peer_wq = nrm(ks[16], (L, D, PEER_HEADS * 2 * PEER_KEY_DIM), D ** -0.5)
    peer_k1 = nrm(ks[17], (L, PEER_HEADS, PEER_KEYS, PEER_KEY_DIM), PEER_KEY_DIM ** -0.5)
    peer_k2 = nrm(ks[18], (L, PEER_HEADS, PEER_KEYS, PEER_KEY_DIM), PEER_KEY_DIM ** -0.5)
    peer_u = nrm(ks[19], (L, PEER_EXPERTS, D), D ** -0.5)
    peer_v = nrm(ks[20], (L, PEER_EXPERTS, D), PEER_HEADS ** -0.5)
    norm_f_g = 1.0 + nrm(ks[21], (D,), 0.02)
    return {"x": x, "norm1_g": norm1_g, "w_in": w_in, "a_re": a_re, "a_im": a_im,
            "log_dt": log_dt, "b_re": b_re, "b_im": b_im, "c_re": c_re, "c_im": c_im,
            "d_skip": d_skip, "w_glu": w_glu, "w_ssm_up": w_ssm_up, "w_attn_up": w_attn_up,
            "w_out": w_out, "norm2_g": norm2_g, "peer_wq": peer_wq, "peer_k1": peer_k1,
            "peer_k2": peer_k2, "peer_u": peer_u, "peer_v": peer_v, "norm_f_g": norm_f_g}


def reference(x, norm1_g, w_in, a_re, a_im, log_dt, b_re, b_im, c_re, c_im, d_skip, w_glu,
              w_ssm_up, w_attn_up, w_out, norm2_g, peer_wq, peer_k1, peer_k2, peer_u, peer_v,
              norm_f_g):
    bsz, seq, _ = x.shape
    cos_a, sin_a = rope_tables(seq, HEAD_DIM)
    cos_i, sin_i = rope_tables(seq, IDX_DIM)
    offsets = np.cumsum(IN_SPLITS)[:-1].tolist()
    h = x
    for layer in range(DEPTH):
        xn = rmsnorm(h, norm1_g[layer])
        proj = xn @ w_in[layer]
        u, q, k, v, qi, ki, wi, g_ssm, g_attn = jnp.split(proj, offsets, axis=-1)
        y_ssm = s5_branch(u, a_re[layer], a_im[layer], log_dt[layer], b_re[layer], b_im[layer],
                          c_re[layer], c_im[layer], d_skip[layer], w_glu[layer])
        q = apply_rope(q.reshape(bsz, seq, ATTN_HEADS, HEAD_DIM), cos_a, sin_a)
        k = apply_rope(k.reshape(bsz, seq, ATTN_KV_HEADS, HEAD_DIM), cos_a, sin_a)
        v = v.reshape(bsz, seq, ATTN_KV_HEADS, HEAD_DIM)
        qi = apply_rope(qi.reshape(bsz, seq, IDX_HEADS, IDX_DIM), cos_i, sin_i)
        ki = apply_rope(ki, cos_i, sin_i)
        y_attn = dsa_branch(q, k, v, qi, ki, wi)
        merged = (jax.nn.sigmoid(g_ssm) * (y_ssm @ w_ssm_up[layer])
                  + jax.nn.sigmoid(g_attn) * (y_attn @ w_attn_up[layer]))
        h = h + merged @ w_out[layer]
        hn = rmsnorm(h, norm2_g[layer])
        h = h + peer_ffn(hn, peer_wq[layer], peer_k1[layer], peer_k2[layer], peer_u[layer], peer_v[layer])
    return rmsnorm(h, norm_f_g)
```

```python
import functools
import math

import numpy as np
import jax
import jax.numpy as jnp
from jax import lax
from jax.experimental import pallas as pl
from jax.experimental.pallas import tpu as pltpu

F32 = jnp.float32
BF16 = jnp.bfloat16
I32 = jnp.int32

SSM_GROUP = 16
SSM_STATE = 64
ATTN_HEADS = 8
ATTN_KV_HEADS = 2
HEAD_DIM = 64
IDX_HEADS = 8
IDX_DIM = 32
TOPK_MAX = 256
ROPE_THETA = 10000.0
NEG_BIG = -1e30
PEER_HEADS = 8
PEER_KEYS = 128
PEER_KEY_DIM = 128
PEER_TOPK = 16
NORM_EPS = 1e-6

INT_MIN = -(2 ** 31)
VMEM_LIMIT = 56 * 1024 * 1024


def _cparams(sem):
    return pltpu.CompilerParams(dimension_semantics=sem, vmem_limit_bytes=VMEM_LIMIT)


def _gelu_tanh(x):
    return 0.5 * x * (1.0 + jnp.tanh(math.sqrt(2.0 / math.pi) * (x + 0.044715 * (x * x * x))))


def _sigmoid(x):
    return 1.0 / (1.0 + jnp.exp(-x))


def _rot_cols(w, hd):
    d, n = w.shape
    w3 = w.reshape(d, n // hd, hd)
    half = hd // 2
    return jnp.concatenate([-w3[..., half:], w3[..., :half]], axis=-1).reshape(d, n)


def _rope_full(seq, hd, heads):
    pos = jnp.arange(seq, dtype=F32)
    inv = ROPE_THETA ** (-jnp.arange(0, hd, 2, dtype=F32) / hd)
    ang = pos[:, None] * inv[None, :]
    c = jnp.concatenate([jnp.cos(ang), jnp.cos(ang)], axis=-1)
    s = jnp.concatenate([jnp.sin(ang), jnp.sin(ang)], axis=-1)
    return jnp.tile(c, (1, heads)), jnp.tile(s, (1, heads))


def _inproj_kernel(x_ref, g_ref, w_ref, wr_ref, cs_ref, sn_ref,
                   u_ref, q_ref, k_ref, v_ref, qi_ref, ki_ref, wi_ref, gs_ref, ga_ref,
                   *, d_ssm, d_q, d_kv, d_qi, d_ki, n_wi, d_model, q_scale, wi_scale):
    x = x_ref[0]
    xn = x * lax.rsqrt(jnp.mean(x * x, axis=-1, keepdims=True) + NORM_EPS) * g_ref[...]
    xb = xn.astype(BF16)

    def mm(ref, lo, n):
        return jnp.dot(xb, ref[:, lo:lo + n], preferred_element_type=F32)

    o = 0
    u_ref[0] = mm(w_ref, o, d_ssm).astype(BF16)
    o += d_ssm
    ro = 0
    q = mm(w_ref, o, d_q) * cs_ref[:, ro:ro + d_q] + mm(wr_ref, ro, d_q) * sn_ref[:, ro:ro + d_q]
    q_ref[0] = (q * q_scale).astype(BF16)
    o += d_q
    ro += d_q
    k = mm(w_ref, o, d_kv) * cs_ref[:, ro:ro + d_kv] + mm(wr_ref, ro, d_kv) * sn_ref[:, ro:ro + d_kv]
    k_ref[0] = k.astype(BF16)
    o += d_kv
    ro += d_kv
    v_ref[0] = mm(w_ref, o, d_kv).astype(BF16)
    o += d_kv
    qi = mm(w_ref, o, d_qi) * cs_ref[:, ro:ro + d_qi] + mm(wr_ref, ro, d_qi) * sn_ref[:, ro:ro + d_qi]
    qi_ref[0] = qi.astype(BF16)
    o += d_qi
    ro += d_qi
    seg = mm(w_ref, o, 128)
    segr = mm(wr_ref, ro, 128)
    kiw = seg * cs_ref[:, ro:ro + 128] + segr * sn_ref[:, ro:ro + 128]
    ki_ref[0] = kiw[:, :d_ki].astype(BF16)
    wi_ref[0] = seg[:, d_ki:d_ki + n_wi] * wi_scale
    o += 128
    gs_ref[0] = _sigmoid(mm(w_ref, o, d_model)).astype(BF16)
    o += d_model
    ga_ref[0] = _sigmoid(mm(w_ref, o, d_model)).astype(BF16)


def _inproj(x, norm_g, w_in, tm):
    bsz, seq, d = x.shape
    d_ssm = d // 2
    d_q = ATTN_HEADS * HEAD_DIM
    d_kv = ATTN_KV_HEADS * HEAD_DIM
    d_qi = IDX_HEADS * IDX_DIM
    d_ki = IDX_DIM
    n_wi = IDX_HEADS
    splits = (d_ssm, d_q, d_kv, d_kv, d_qi, d_ki, n_wi, d, d)
    offs = np.cumsum(splits)[:-1].tolist()
    wu, wq, wk, wv, wqi, wki, wwi, wgs, wga = jnp.split(w_in, offs, axis=1)
    pad = jnp.zeros((d, 128 - d_ki - n_wi), F32)
    w_main = jnp.concatenate([wu, wq, wk, wv, wqi, wki, wwi, pad, wgs, wga], axis=1).astype(BF16)
    rpad = jnp.zeros((d, 128 - d_ki), F32)
    w_rot = jnp.concatenate([_rot_cols(wq, HEAD_DIM), _rot_cols(wk, HEAD_DIM),
                             _rot_cols(wqi, IDX_DIM), _rot_cols(wki, IDX_DIM), rpad], axis=1).astype(BF16)
    cq, sq = _rope_full(seq, HEAD_DIM, ATTN_HEADS)
    ck, sk = _rope_full(seq, HEAD_DIM, ATTN_KV_HEADS)
    cqi, sqi = _rope_full(seq, IDX_DIM, IDX_HEADS)
    cki, ski = _rope_full(seq, IDX_DIM, 1)
    tpad = jnp.zeros((seq, 128 - d_ki), F32)
    cs = jnp.concatenate([cq, ck, cqi, cki, tpad], axis=1)
    sn = jnp.concatenate([sq, sk, sqi, ski, tpad], axis=1)
    n_main = w_main.shape[1]
    n_rot = w_rot.shape[1]

    kern = functools.partial(
        _inproj_kernel, d_ssm=d_ssm, d_q=d_q, d_kv=d_kv, d_qi=d_qi, d_ki=d_ki, n_wi=n_wi, d_model=d,
        q_scale=HEAD_DIM ** -0.5, wi_scale=(IDX_HEADS ** -0.5) * (IDX_DIM ** -0.5))
    tok = lambda n: pl.BlockSpec((1, tm, n), lambda s, b: (b, s, 0))
    full = lambda shape: pl.BlockSpec(shape, lambda s, b: (0,) * len(shape))
    outs = [(d_ssm, BF16), (d_q, BF16), (d_kv, BF16), (d_kv, BF16), (d_qi, BF16), (d_ki, BF16),
            (n_wi, F32), (d, BF16), (d, BF16)]
    return pl.pallas_call(
        kern,
        grid=(seq // tm, bsz),
        in_specs=[tok(d), full((1, d)), full((d, n_main)), full((d, n_rot)),
                  pl.BlockSpec((tm, n_rot), lambda s, b: (s, 0)),
                  pl.BlockSpec((tm, n_rot), lambda s, b: (s, 0))],
        out_specs=[tok(n) for n, _ in outs],
        out_shape=[jax.ShapeDtypeStruct((bsz, seq, n), dt) for n, dt in outs],
        compiler_params=_cparams(("arbitrary", "arbitrary")),
        name="inproj",
    )(x, norm_g.reshape(1, d), w_main, w_rot, cs, sn)


def _s5_kernel(u_ref, bre_ref, bim_ref, cre_ref, cim_ref, are_ref, aim_ref, dsk_ref, wglu_ref,
               y_ref, sre, sim, st_re, st_im, *, tc, nb, lane_chunk):
    @pl.when(pl.program_id(0) == 0)
    def _():
        st_re[...] = jnp.zeros_like(st_re)
        st_im[...] = jnp.zeros_like(st_im)

    u = u_ref[...]
    n_half = bre_ref.shape[0]
    hin = bre_ref.shape[1]
    hst = bre_ref.shape[2]
    for h in range(n_half):
        uh = u[:, h * hin:(h + 1) * hin]
        sre[:, h * hst:(h + 1) * hst] = jnp.dot(uh, bre_ref[h], preferred_element_type=F32)
        sim[:, h * hst:(h + 1) * hst] = jnp.dot(uh, bim_ref[h], preferred_element_type=F32)

    n_state = sre.shape[1]
    for c in range(n_state // lane_chunk):
        cols = slice(c * lane_chunk, (c + 1) * lane_chunk)
        ar = are_ref[:, cols]
        ai = aim_ref[:, cols]

        def step(t, carry, cols=cols, ar=ar, ai=ai):
            sr, si = carry
            r0 = pl.multiple_of(t * nb, nb)
            nr = ar * sr - ai * si + sre[pl.ds(r0, nb), cols]
            ni = ar * si + ai * sr + sim[pl.ds(r0, nb), cols]
            sre[pl.ds(r0, nb), cols] = nr
            sim[pl.ds(r0, nb), cols] = ni
            return nr, ni

        sr, si = lax.fori_loop(0, tc, step, (st_re[:, cols], st_im[:, cols]), unroll=4)
        st_re[:, cols] = sr
        st_im[:, cols] = si

    ys = []
    for h in range(n_half):
        srh = sre[:, h * hst:(h + 1) * hst].astype(BF16)
        sih = sim[:, h * hst:(h + 1) * hst].astype(BF16)
        ys.append(jnp.dot(srh, cre_ref[h], preferred_element_type=F32)
                  - jnp.dot(sih, cim_ref[h], preferred_element_type=F32))
    y = jnp.concatenate(ys, axis=-1) + dsk_ref[...] * u.astype(F32)
    y = _gelu_tanh(y)
    gate = jnp.dot(y.astype(BF16), wglu_ref[...], preferred_element_type=F32)
    y_ref[...] = (y * _sigmoid(gate)).astype(BF16)


def _s5(u_tb, a_re, a_im, log_dt, b_re, b_im, c_re, c_im, d_skip, w_glu, nb, tc):
    rows, d_ssm = u_tb.shape
    groups = d_ssm // SSM_GROUP
    n_state = groups * SSM_STATE
    lam = lax.complex(a_re, a_im)
    dt = jnp.exp(log_dt)[:, None]
    a_bar = jnp.exp(lam * dt)
    b_bar = ((a_bar - 1.0) / lam)[..., None] * lax.complex(b_re, b_im)
    gh = min(groups, 256 // SSM_GROUP)
    n_half = groups // gh
    eye = jnp.eye(gh, dtype=F32)

    def bmat(bb):
        b4 = bb.reshape(n_half, gh, SSM_STATE, SSM_GROUP)
        return jnp.einsum('hgpc,gk->hgckp', b4, eye).reshape(n_half, gh * SSM_GROUP, gh * SSM_STATE)

    def cmat(cc):
        c4 = cc.reshape(n_half, gh, SSM_GROUP, SSM_STATE)
        return jnp.einsum('hgcp,gk->hgpkc', c4, eye).reshape(n_half, gh * SSM_STATE, gh * SSM_GROUP)

    bre = bmat(jnp.real(b_bar)).astype(BF16)
    bim = bmat(jnp.imag(b_bar)).astype(BF16)
    cre = cmat(c_re).astype(BF16)
    cim = cmat(c_im).astype(BF16)
    are = jnp.broadcast_to(jnp.real(a_bar).reshape(1, n_state), (nb, n_state))
    aim = jnp.broadcast_to(jnp.imag(a_bar).reshape(1, n_state), (nb, n_state))
    blk = tc * nb
    full = lambda a: pl.BlockSpec(a.shape, lambda i: (0,) * a.ndim)
    dsk = d_skip.reshape(1, d_ssm)
    wg = w_glu.astype(BF16)
    kern = functools.partial(_s5_kernel, tc=tc, nb=nb, lane_chunk=512)
    return pl.pallas_call(
        kern,
        grid=(rows // blk,),
        in_specs=[pl.BlockSpec((blk, d_ssm), lambda i: (i, 0)),
                  full(bre), full(bim), full(cre), full(cim), full(are), full(aim), full(dsk), full(wg)],
        out_specs=pl.BlockSpec((blk, d_ssm), lambda i: (i, 0)),
        out_shape=jax.ShapeDtypeStruct((rows, d_ssm), BF16),
        scratch_shapes=[pltpu.VMEM((blk, n_state), F32), pltpu.VMEM((blk, n_state), F32),
                        pltpu.VMEM((nb, n_state), F32), pltpu.VMEM((nb, n_state), F32)],
        compiler_params=_cparams(("arbitrary",)),
        name="s5",
    )(u_tb, bre, bim, cre, cim, are, aim, dsk, wg)


def _dsa_kernel(qi_ref, wit_ref, q_ref, ki_ref, k_ref, vt_ref, o_ref, key_s,
                *, tq, kt, topk, seq_bits):
    qb = pl.program_id(1)
    nkt = (qb * tq + tq + kt - 1) // kt
    q_pos = qb * tq + lax.broadcasted_iota(I32, (1, tq), 1)
    k_eff = jnp.minimum(topk, q_pos + 1).astype(F32)

    qi = qi_ref[0]
    wit = wit_ref[0]
    qih = [qi[:, h * IDX_DIM:(h + 1) * IDX_DIM] for h in range(IDX_HEADS)]

    def key_pos(t):
        return t * kt + lax.broadcasted_iota(I32, (kt, tq), 0)

    def score_tile(t, _):
        r0 = pl.multiple_of(t * kt, kt)
        ki_t = ki_ref[0, pl.ds(r0, kt), :]
        sc = jnp.zeros((kt, tq), F32)
        for h in range(IDX_HEADS):
            rel = lax.dot_general(ki_t, qih[h], (((1,), (1,)), ((), ())), preferred_element_type=F32)
            sc = sc + jnp.maximum(rel, 0.0) * wit[h:h + 1, :]
        bits = lax.bitcast_convert_type(sc, I32)
        key = jnp.where(bits < 0, bits ^ jnp.int32(0x7FFFFFFF), bits)
        key = jnp.where(key_pos(t) <= q_pos, key, jnp.int32(INT_MIN))
        key_s[pl.ds(r0, kt), :] = key
        return 0

    lax.fori_loop(0, nkt, score_tile, 0)

    def count(pred_fn):
        def body(t, acc):
            r0 = pl.multiple_of(t * kt, kt)
            m = pred_fn(key_s[pl.ds(r0, kt), :], t)
            ones = jnp.where(m, 1.0, 0.0).reshape(kt // 8, 8, tq)
            return acc + jnp.sum(ones, axis=0)
        acc = lax.fori_loop(0, nkt, body, jnp.zeros((8, tq), F32))
        return jnp.sum(acc, axis=0, keepdims=True)

    def bit_step(i, u):
        bit = jnp.left_shift(jnp.int32(1), 31 - i)
        cand_u = u | bit
        cand_s = cand_u ^ jnp.int32(INT_MIN)
        cnt = count(lambda kk, t: kk >= cand_s)
        return jnp.where(cnt >= k_eff, cand_u, u)

    u_thr = lax.fori_loop(0, 32, bit_step, jnp.zeros((1, tq), I32))
    thr = u_thr ^ jnp.int32(INT_MIN)

    cnt_ge = count(lambda kk, t: kk >= thr)
    cnt_gt = count(lambda kk, t: kk > thr)
    need_eq = k_eff - cnt_gt
    has_tie = jnp.max(cnt_ge - k_eff) > 0.0

    def tie_cut():
        def pos_step(i, c):
            bit = jnp.left_shift(jnp.int32(1), seq_bits - 1 - i)
            cand = c | bit
            cnt = count(lambda kk, t: (kk == thr) & (key_pos(t) < cand))
            return jnp.where(cnt < need_eq, cand, c)
        return lax.fori_loop(0, seq_bits, pos_step, jnp.zeros((1, tq), I32))

    cut = lax.cond(has_tie, tie_cut, lambda: jnp.full((1, tq), 2 ** seq_bits, I32))

    q = q_ref[0]
    grp = ATTN_HEADS // ATTN_KV_HEADS
    for n in range(ATTN_KV_HEADS):
        qn = jnp.concatenate([q[:, (n * grp + g) * HEAD_DIM:(n * grp + g + 1) * HEAD_DIM]
                              for g in range(grp)], axis=0)

        def attn_tile(t, carry, n=n, qn=qn):
            m, l, acc = carry
            r0 = pl.multiple_of(t * kt, kt)
            key = key_s[pl.ds(r0, kt), :]
            sel = (key > thr) | ((key == thr) & (key_pos(t) <= cut))
            k_t = k_ref[0, n, pl.ds(r0, kt), :]
            lg = lax.dot_general(k_t, qn, (((1,), (1,)), ((), ())), preferred_element_type=F32)
            sel4 = jnp.concatenate([sel] * grp, axis=1)
            lg = jnp.where(sel4, lg, NEG_BIG)
            m_new = jnp.maximum(m, jnp.max(lg, axis=0, keepdims=True))
            p = jnp.where(sel4, jnp.exp(lg - m_new), 0.0)
            alpha = jnp.exp(m - m_new)
            l = alpha * l + jnp.sum(p, axis=0, keepdims=True)
            v_t = vt_ref[0, n, t]
            acc = alpha * acc + jnp.dot(v_t, p.astype(BF16), preferred_element_type=F32)
            return m_new, l, acc

        init = (jnp.full((1, grp * tq), NEG_BIG, F32), jnp.zeros((1, grp * tq), F32),
                jnp.zeros((HEAD_DIM, grp * tq), F32))
        m, l, acc = lax.fori_loop(0, nkt, attn_tile, init)
        o_ref[0, 0, n] = (acc / l).astype(BF16)


def _dsa(q, k, v, qi, ki, wi, tq, kt):
    bsz, seq, _ = q.shape
    topk = min(TOPK_MAX, seq // 4)
    nqb = seq // tq
    grp = ATTN_HEADS // ATTN_KV_HEADS
    seq_bits = int(math.log2(seq))
    assert 2 ** seq_bits == seq
    wit = wi.transpose(0, 2, 1)
    k4 = k.reshape(bsz, seq, ATTN_KV_HEADS, HEAD_DIM).transpose(0, 2, 1, 3)
    vt = v.reshape(bsz, seq // kt, kt, ATTN_KV_HEADS, HEAD_DIM).transpose(0, 3, 1, 4, 2)
    kern = functools.partial(_dsa_kernel, tq=tq, kt=kt, topk=topk, seq_bits=seq_bits)
    o_t = pl.pallas_call(
        kern,
        grid=(bsz, nqb),
        in_specs=[pl.BlockSpec((1, tq, IDX_HEADS * IDX_DIM), lambda b, j: (b, j, 0)),
                  pl.BlockSpec((1, IDX_HEADS, tq), lambda b, j: (b, 0, j)),
                  pl.BlockSpec((1, tq, ATTN_HEADS * HEAD_DIM), lambda b, j: (b, j, 0)),
                  pl.BlockSpec((1, seq, IDX_DIM), lambda b, j: (b, 0, 0)),
                  pl.BlockSpec((1, ATTN_KV_HEADS, seq, HEAD_DIM), lambda b, j: (b, 0, 0, 0)),
                  pl.BlockSpec((1, ATTN_KV_HEADS, seq // kt, HEAD_DIM, kt), lambda b, j: (b, 0, 0, 0, 0))],
        out_specs=pl.BlockSpec((1, 1, ATTN_KV_HEADS, HEAD_DIM, grp * tq), lambda b, j: (b, j, 0, 0, 0)),
        out_shape=jax.ShapeDtypeStruct((bsz, nqb, ATTN_KV_HEADS, HEAD_DIM, grp * tq), BF16),
        scratch_shapes=[pltpu.VMEM((seq, tq), I32)],
        compiler_params=_cparams(("arbitrary", "arbitrary")),
        name="dsa",
    )(qi, wit, q, ki, k4, vt)
    o = o_t.reshape(bsz, nqb, ATTN_KV_HEADS, HEAD_DIM, grp, tq).transpose(0, 1, 5, 2, 4, 3)
    return o.reshape(bsz, seq, ATTN_HEADS * HEAD_DIM)


def _merge_kernel(x_ref, ys_ref, ya_ref, gs_ref, ga_ref, wsu_ref, wau_ref, wout_ref, g2_ref, wq_ref,
                  h_ref, hn_ref, qp_ref):
    ms = jnp.dot(ys_ref[...], wsu_ref[...], preferred_element_type=F32)
    ma = jnp.dot(ya_ref[...], wau_ref[...], preferred_element_type=F32)
    merged = gs_ref[...].astype(F32) * ms + ga_ref[...].astype(F32) * ma
    h = x_ref[...] + jnp.dot(merged.astype(BF16), wout_ref[...], preferred_element_type=F32)
    h_ref[...] = h
    hn = h * lax.rsqrt(jnp.mean(h * h, axis=-1, keepdims=True) + NORM_EPS) * g2_ref[...]
    hn_ref[...] = hn
    qp_ref[...] = jnp.dot(hn.astype(BF16), wq_ref[...], preferred_element_type=F32).astype(BF16)


def _merge(x2, ys, ya, gs, ga, w_ssm_up, w_attn_up, w_out, norm2_g, peer_wq, tm):
    n, d = x2.shape
    row = lambda a: pl.BlockSpec((tm, a.shape[1]), lambda i: (i, 0))
    full = lambda a: pl.BlockSpec(a.shape, lambda i: (0,) * a.ndim)
    wsu = w_ssm_up.astype(BF16)
    wau = w_attn_up.astype(BF16)
    wo = w_out.astype(BF16)
    wq = peer_wq.astype(BF16)
    g2 = norm2_g.reshape(1, d)
    nq = wq.shape[1]
    return pl.pallas_call(
        _merge_kernel,
        grid=(n // tm,),
        in_specs=[row(x2), row(ys), row(ya), row(gs), row(ga), full(wsu), full(wau), full(wo), full(g2), full(wq)],
        out_specs=[pl.BlockSpec((tm, d), lambda i: (i, 0)), pl.BlockSpec((tm, d), lambda i: (i, 0)),
                   pl.BlockSpec((tm, nq), lambda i: (i, 0))],
        out_shape=[jax.ShapeDtypeStruct((n, d), F32), jax.ShapeDtypeStruct((n, d), F32),
                   jax.ShapeDtypeStruct((n, nq), BF16)],
        compiler_params=_cparams(("arbitrary",)),
        name="merge",
    )(x2, ys, ya, gs, ga, wsu, wau, wo, g2, wq)


def _cand_layout():
    blocks = []
    blocks.append((0, 16, 16))
    for i in range(1, 8):
        blocks.append((i, 8, PEER_TOPK // (i + 1)))
    blocks.append((None, 8, 8))
    return blocks


def _top_rows(s, order, payload, k):
    big = jnp.float32(3e38)
    vals, pays = [], []
    for _ in range(k):
        m = jnp.max(s, axis=0, keepdims=True)
        o = jnp.min(jnp.where(s == m, order, big), axis=0, keepdims=True)
        hit = order == o
        pays.append(jnp.min(jnp.where(hit, payload, big), axis=0, keepdims=True))
        vals.append(m)
        s = jnp.where(hit, -jnp.inf, s)
    return jnp.concatenate(vals, axis=0), jnp.concatenate(pays, axis=0)


def _route_kernel(qp_ref, k1_ref, k2_ref, e_ref, g_ref, *, tt):
    qp = qp_ref[...]
    kd = PEER_KEY_DIM
    rows_k = lax.broadcasted_iota(I32, (PEER_KEYS, tt), 0).astype(F32)
    for h in range(PEER_HEADS):
        q1 = qp[:, (2 * h) * kd:(2 * h + 1) * kd]
        q2 = qp[:, (2 * h + 1) * kd:(2 * h + 2) * kd]
        s1 = lax.dot_general(k1_ref[h], q1, (((1,), (1,)), ((), ())), preferred_element_type=F32)
        s2 = lax.dot_general(k2_ref[h], q2, (((1,), (1,)), ((), ())), preferred_element_type=F32)
        v1, i1 = _top_rows(s1, rows_k, rows_k, PEER_TOPK)
        v2, i2 = _top_rows(s2, rows_k, rows_k, PEER_TOPK)
        cs, ce, co = [], [], []
        for i, rows, valid in _cand_layout():
            r = lax.broadcasted_iota(I32, (rows, tt), 0).astype(F32)
            if i is None:
                val = v1[8:16] + v2[0:1]
                eid = i1[8:16] * PEER_KEYS + i2[0:1]
                flat = (r + 8.0) * PEER_TOPK
            else:
                val = v1[i:i + 1] + v2[0:rows]
                eid = i1[i:i + 1] * PEER_KEYS + i2[0:rows]
                flat = r + float(i * PEER_TOPK)
                if valid < rows:
                    val = jnp.where(r < float(valid), val, -jnp.inf)
            cs.append(val)
            ce.append(eid)
            co.append(flat)
        cand = jnp.concatenate(cs, axis=0)
        top_s, top_e = _top_rows(cand, jnp.concatenate(co, axis=0), jnp.concatenate(ce, axis=0), PEER_TOPK)
        p = jnp.exp(top_s - top_s[0:1])
        gates = p / jnp.sum(p, axis=0, keepdims=True)
        e_ref[h * PEER_TOPK:(h + 1) * PEER_TOPK, :] = top_e.astype(I32)
        g_ref[h * PEER_TOPK:(h + 1) * PEER_TOPK, :] = gates


def _route(qp, peer_k1, peer_k2, tt):
    n, nq = qp.shape
    k1 = peer_k1.astype(BF16)
    k2 = peer_k2.astype(BF16)
    n_sel = PEER_HEADS * PEER_TOPK
    full = lambda a: pl.BlockSpec(a.shape, lambda i: (0,) * a.ndim)
    return pl.pallas_call(
        functools.partial(_route_kernel, tt=tt),
        grid=(n // tt,),
        in_specs=[pl.BlockSpec((tt, nq), lambda i: (i, 0)), full(k1), full(k2)],
        out_specs=[pl.BlockSpec((n_sel, tt), lambda i: (0, i)), pl.BlockSpec((n_sel, tt), lambda i: (0, i))],
        out_shape=[jax.ShapeDtypeStruct((n_sel, n), I32), jax.ShapeDtypeStruct((n_sel, n), F32)],
        compiler_params=_cparams(("arbitrary",)),
        name="route",
    )(qp, k1, k2)


def _final_kernel(h_ref, p_ref, g_ref, o_ref):
    h = h_ref[...] + p_ref[...]
    o_ref[...] = h * lax.rsqrt(jnp.mean(h * h, axis=-1, keepdims=True) + NORM_EPS) * g_ref[...]


def _final(h, p, g, tm):
    n, d = h.shape
    row = pl.BlockSpec((tm, d), lambda i: (i, 0))
    return pl.pallas_call(
        _final_kernel,
        grid=(n // tm,),
        in_specs=[row, row, pl.BlockSpec((1, d), lambda i: (0, 0))],
        out_specs=row,
        out_shape=jax.ShapeDtypeStruct((n, d), F32),
        compiler_params=_cparams(("arbitrary",)),
        name="final",
    )(h, p, g.reshape(1, d))


def _peer_experts_xla(hn, experts, gates, u_tab, v_tab):
    def chunk(args):
        xc, ec, gc = args
        u = u_tab[ec]
        act = jax.nn.gelu(jnp.einsum('ced,cd->ce', u, xc)) * gc
        return jnp.einsum('ce,ced->cd', act, v_tab[ec])
    n, d = hn.shape
    c = 128
    out = lax.map(chunk, (hn.reshape(n // c, c, d), experts.reshape(n // c, c, -1), gates.reshape(n // c, c, -1)))
    return out.reshape(n, d)


def kernel(x, norm1_g, w_in, a_re, a_im, log_dt, b_re, b_im, c_re, c_im, d_skip, w_glu, w_ssm_up, w_attn_up,
           w_out, norm2_g, peer_wq, peer_k1, peer_k2, peer_u, peer_v, norm_f_g):
    bsz, seq, d = x.shape
    n = bsz * seq
    h = x
    depth = norm1_g.shape[0]
    for layer in range(depth):
        u, q, k, v, qi, ki, wi, gs, ga = _inproj(h, norm1_g[layer], w_in[layer], tm=min(512, seq))
        d_ssm = u.shape[-1]
        u_tb = u.transpose(1, 0, 2).reshape(n, d_ssm)
        y_tb = _s5(u_tb, a_re[layer], a_im[layer], log_dt[layer], b_re[layer], b_im[layer], c_re[layer],
                   c_im[layer], d_skip[layer], w_glu[layer], nb=bsz, tc=64)
        ys = y_tb.reshape(seq, bsz, d_ssm).transpose(1, 0, 2).reshape(n, d_ssm)
        ya = _dsa(q, k, v, qi, ki, wi, tq=128, kt=256).reshape(n, -1)
        hm, hn, qp = _merge(h.reshape(n, d), ys, ya, gs.reshape(n, d), ga.reshape(n, d), w_ssm_up[layer],
                            w_attn_up[layer], w_out[layer], norm2_g[layer], peer_wq[layer], tm=512)
        e_t, g_t = _route(qp, peer_k1[layer], peer_k2[layer], tt=256)
        po = _peer_experts_xla(hn, e_t.T, g_t.T, peer_u[layer], peer_v[layer])
        if layer + 1 < depth:
            h = (hm + po).reshape(bsz, seq, d)
    return _final(hm, po, norm_f_g, tm=512).reshape(bsz, seq, d)
```

```python
import functools
import math

import numpy as np
import jax
import jax.numpy as jnp
from jax import lax
from jax.experimental import pallas as pl
from jax.experimental.pallas import tpu as pltpu
from jax.experimental.pallas import tpu_sc as plsc

F32 = jnp.float32
BF16 = jnp.bfloat16
I32 = jnp.int32

SSM_GROUP = 16
SSM_STATE = 64
ATTN_HEADS = 8
ATTN_KV_HEADS = 2
HEAD_DIM = 64
IDX_HEADS = 8
IDX_DIM = 32
TOPK_MAX = 256
ROPE_THETA = 10000.0
NEG_BIG = -1e30
PEER_HEADS = 8
PEER_KEYS = 128
PEER_KEY_DIM = 128
PEER_TOPK = 16
NORM_EPS = 1e-6

INT_MIN = -(2 ** 31)
VMEM_LIMIT = 56 * 1024 * 1024


def _cparams(sem):
    return pltpu.CompilerParams(dimension_semantics=sem, vmem_limit_bytes=VMEM_LIMIT)


def _gelu_tanh(x):
    return 0.5 * x * (1.0 + jnp.tanh(math.sqrt(2.0 / math.pi) * (x + 0.044715 * (x * x * x))))


def _sigmoid(x):
    return 1.0 / (1.0 + jnp.exp(-x))


def _rot_cols(w, hd):
    d, n = w.shape
    w3 = w.reshape(d, n // hd, hd)
    half = hd // 2
    return jnp.concatenate([-w3[..., half:], w3[..., :half]], axis=-1).reshape(d, n)


def _rope_full(seq, hd, heads):
    pos = jnp.arange(seq, dtype=F32)
    inv = ROPE_THETA ** (-jnp.arange(0, hd, 2, dtype=F32) / hd)
    ang = pos[:, None] * inv[None, :]
    c = jnp.concatenate([jnp.cos(ang), jnp.cos(ang)], axis=-1)
    s = jnp.concatenate([jnp.sin(ang), jnp.sin(ang)], axis=-1)
    return jnp.tile(c, (1, heads)), jnp.tile(s, (1, heads))


def _inproj_kernel(x_ref, g_ref, w_ref, wr_ref, cs_ref, sn_ref,
                   u_ref, q_ref, k_ref, v_ref, qi_ref, ki_ref, wi_ref, gs_ref, ga_ref,
                   *, d_ssm, d_q, d_kv, d_qi, d_ki, n_wi, d_model, q_scale, wi_scale):
    x = x_ref[0]
    xn = x * lax.rsqrt(jnp.mean(x * x, axis=-1, keepdims=True) + NORM_EPS) * g_ref[...]
    xb = xn.astype(BF16)

    def mm(ref, lo, n):
        return jnp.dot(xb, ref[:, lo:lo + n], preferred_element_type=F32)

    o = 0
    u_ref[0] = mm(w_ref, o, d_ssm).astype(BF16)
    o += d_ssm
    ro = 0
    q = mm(w_ref, o, d_q) * cs_ref[:, ro:ro + d_q] + mm(wr_ref, ro, d_q) * sn_ref[:, ro:ro + d_q]
    q_ref[0] = (q * q_scale).astype(BF16)
    o += d_q
    ro += d_q
    k = mm(w_ref, o, d_kv) * cs_ref[:, ro:ro + d_kv] + mm(wr_ref, ro, d_kv) * sn_ref[:, ro:ro + d_kv]
    k_ref[0] = k.astype(BF16)
    o += d_kv
    ro += d_kv
    v_ref[0] = mm(w_ref, o, d_kv).astype(BF16)
    o += d_kv
    qi = mm(w_ref, o, d_qi) * cs_ref[:, ro:ro + d_qi] + mm(wr_ref, ro, d_qi) * sn_ref[:, ro:ro + d_qi]
    qi_ref[0] = qi.astype(BF16)
    o += d_qi
    ro += d_qi
    seg = mm(w_ref, o, 128)
    segr = mm(wr_ref, ro, 128)
    kiw = seg * cs_ref[:, ro:ro + 128] + segr * sn_ref[:, ro:ro + 128]
    ki_ref[0] = kiw[:, :d_ki].astype(BF16)
    wi_ref[0] = seg[:, d_ki:d_ki + n_wi] * wi_scale
    o += 128
    gs_ref[0] = _sigmoid(mm(w_ref, o, d_model)).astype(BF16)
    o += d_model
    ga_ref[0] = _sigmoid(mm(w_ref, o, d_model)).astype(BF16)


def _inproj(x, norm_g, w_in, tm):
    bsz, seq, d = x.shape
    d_ssm = d // 2
    d_q = ATTN_HEADS * HEAD_DIM
    d_kv = ATTN_KV_HEADS * HEAD_DIM
    d_qi = IDX_HEADS * IDX_DIM
    d_ki = IDX_DIM
    n_wi = IDX_HEADS
    splits = (d_ssm, d_q, d_kv, d_kv, d_qi, d_ki, n_wi, d, d)
    offs = np.cumsum(splits)[:-1].tolist()
    wu, wq, wk, wv, wqi, wki, wwi, wgs, wga = jnp.split(w_in, offs, axis=1)
    pad = jnp.zeros((d, 128 - d_ki - n_wi), F32)
    w_main = jnp.concatenate([wu, wq, wk, wv, wqi, wki, wwi, pad, wgs, wga], axis=1).astype(BF16)
    rpad = jnp.zeros((d, 128 - d_ki), F32)
    w_rot = jnp.concatenate([_rot_cols(wq, HEAD_DIM), _rot_cols(wk, HEAD_DIM),
                             _rot_cols(wqi, IDX_DIM), _rot_cols(wki, IDX_DIM), rpad], axis=1).astype(BF16)
    cq, sq = _rope_full(seq, HEAD_DIM, ATTN_HEADS)
    ck, sk = _rope_full(seq, HEAD_DIM, ATTN_KV_HEADS)
    cqi, sqi = _rope_full(seq, IDX_DIM, IDX_HEADS)
    cki, ski = _rope_full(seq, IDX_DIM, 1)
    tpad = jnp.zeros((seq, 128 - d_ki), F32)
    cs = jnp.concatenate([cq, ck, cqi, cki, tpad], axis=1)
    sn = jnp.concatenate([sq, sk, sqi, ski, tpad], axis=1)
    n_main = w_main.shape[1]
    n_rot = w_rot.shape[1]

    kern = functools.partial(
        _inproj_kernel, d_ssm=d_ssm, d_q=d_q, d_kv=d_kv, d_qi=d_qi, d_ki=d_ki, n_wi=n_wi, d_model=d,
        q_scale=HEAD_DIM ** -0.5, wi_scale=(IDX_HEADS ** -0.5) * (IDX_DIM ** -0.5))
    tok = lambda n: pl.BlockSpec((1, tm, n), lambda s, b: (b, s, 0))
    full = lambda shape: pl.BlockSpec(shape, lambda s, b: (0,) * len(shape))
    outs = [(d_ssm, BF16), (d_q, BF16), (d_kv, BF16), (d_kv, BF16), (d_qi, BF16), (d_ki, BF16),
            (n_wi, F32), (d, BF16), (d, BF16)]
    return pl.pallas_call(
        kern,
        grid=(seq // tm, bsz),
        in_specs=[tok(d), full((1, d)), full((d, n_main)), full((d, n_rot)),
                  pl.BlockSpec((tm, n_rot), lambda s, b: (s, 0)),
                  pl.BlockSpec((tm, n_rot), lambda s, b: (s, 0))],
        out_specs=[tok(n) for n, _ in outs],
        out_shape=[jax.ShapeDtypeStruct((bsz, seq, n), dt) for n, dt in outs],
        compiler_params=_cparams(("arbitrary", "arbitrary")),
        name="inproj",
    )(x, norm_g.reshape(1, d), w_main, w_rot, cs, sn)


def _s5_kernel(u_ref, bre_ref, bim_ref, cre_ref, cim_ref, are_ref, aim_ref, dsk_ref, wglu_ref,
               y_ref, sre, sim, st_re, st_im, *, tc, nb, lane_chunk):
    @pl.when(pl.program_id(0) == 0)
    def _():
        st_re[...] = jnp.zeros_like(st_re)
        st_im[...] = jnp.zeros_like(st_im)

    u = u_ref[...]
    n_half = bre_ref.shape[0]
    hin = bre_ref.shape[1]
    hst = bre_ref.shape[2]
    for h in range(n_half):
        uh = u[:, h * hin:(h + 1) * hin]
        sre[:, h * hst:(h + 1) * hst] = jnp.dot(uh, bre_ref[h], preferred_element_type=F32)
        sim[:, h * hst:(h + 1) * hst] = jnp.dot(uh, bim_ref[h], preferred_element_type=F32)

    n_state = sre.shape[1]
    for c in range(n_state // lane_chunk):
        cols = slice(c * lane_chunk, (c + 1) * lane_chunk)
        ar = are_ref[:, cols]
        ai = aim_ref[:, cols]

        def step(t, carry, cols=cols, ar=ar, ai=ai):
            sr, si = carry
            r0 = pl.multiple_of(t * nb, nb)
            nr = ar * sr - ai * si + sre[pl.ds(r0, nb), cols]
            ni = ar * si + ai * sr + sim[pl.ds(r0, nb), cols]
            sre[pl.ds(r0, nb), cols] = nr
            sim[pl.ds(r0, nb), cols] = ni
            return nr, ni

        sr, si = lax.fori_loop(0, tc, step, (st_re[:, cols], st_im[:, cols]), unroll=4)
        st_re[:, cols] = sr
        st_im[:, cols] = si

    ys = []
    for h in range(n_half):
        srh = sre[:, h * hst:(h + 1) * hst].astype(BF16)
        sih = sim[:, h * hst:(h + 1) * hst].astype(BF16)
        ys.append(jnp.dot(srh, cre_ref[h], preferred_element_type=F32)
                  - jnp.dot(sih, cim_ref[h], preferred_element_type=F32))
    y = jnp.concatenate(ys, axis=-1) + dsk_ref[...] * u.astype(F32)
    y = _gelu_tanh(y)
    gate = jnp.dot(y.astype(BF16), wglu_ref[...], preferred_element_type=F32)
    y_ref[...] = (y * _sigmoid(gate)).astype(BF16)


def _s5(u_tb, a_re, a_im, log_dt, b_re, b_im, c_re, c_im, d_skip, w_glu, nb, tc):
    rows, d_ssm = u_tb.shape
    groups = d_ssm // SSM_GROUP
    n_state = groups * SSM_STATE
    lam = lax.complex(a_re, a_im)
    dt = jnp.exp(log_dt)[:, None]
    a_bar = jnp.exp(lam * dt)
    b_bar = ((a_bar - 1.0) / lam)[..., None] * lax.complex(b_re, b_im)
    gh = min(groups, 256 // SSM_GROUP)
    n_half = groups // gh
    eye = jnp.eye(gh, dtype=F32)

    def bmat(bb):
        b4 = bb.reshape(n_half, gh, SSM_STATE, SSM_GROUP)
        return jnp.einsum('hgpc,gk->hgckp', b4, eye).reshape(n_half, gh * SSM_GROUP, gh * SSM_STATE)

    def cmat(cc):
        c4 = cc.reshape(n_half, gh, SSM_GROUP, SSM_STATE)
        return jnp.einsum('hgcp,gk->hgpkc', c4, eye).reshape(n_half, gh * SSM_STATE, gh * SSM_GROUP)

    bre = bmat(jnp.real(b_bar)).astype(BF16)
    bim = bmat(jnp.imag(b_bar)).astype(BF16)
    cre = cmat(c_re).astype(BF16)
    cim = cmat(c_im).astype(BF16)
    are = jnp.broadcast_to(jnp.real(a_bar).reshape(1, n_state), (nb, n_state))
    aim = jnp.broadcast_to(jnp.imag(a_bar).reshape(1, n_state), (nb, n_state))
    blk = tc * nb
    full = lambda a: pl.BlockSpec(a.shape, lambda i: (0,) * a.ndim)
    dsk = d_skip.reshape(1, d_ssm)
    wg = w_glu.astype(BF16)
    kern = functools.partial(_s5_kernel, tc=tc, nb=nb, lane_chunk=512)
    return pl.pallas_call(
        kern,
        grid=(rows // blk,),
        in_specs=[pl.BlockSpec((blk, d_ssm), lambda i: (i, 0)),
                  full(bre), full(bim), full(cre), full(cim), full(are), full(aim), full(dsk), full(wg)],
        out_specs=pl.BlockSpec((blk, d_ssm), lambda i: (i, 0)),
        out_shape=jax.ShapeDtypeStruct((rows, d_ssm), BF16),
        scratch_shapes=[pltpu.VMEM((blk, n_state), F32), pltpu.VMEM((blk, n_state), F32),
                        pltpu.VMEM((nb, n_state), F32), pltpu.VMEM((nb, n_state), F32)],
        compiler_params=_cparams(("arbitrary",)),
        name="s5",
    )(u_tb, bre, bim, cre, cim, are, aim, dsk, wg)


def _dsa_kernel(qi_ref, wit_ref, q_ref, ki_ref, k_ref, vt_ref, o_ref, key_s,
                *, tq, kt, topk, seq_bits):
    qb = pl.program_id(1)
    nkt = (qb * tq + tq + kt - 1) // kt
    q_pos = qb * tq + lax.broadcasted_iota(I32, (1, tq), 1)
    k_eff = jnp.minimum(topk, q_pos + 1).astype(F32)

    qi = qi_ref[0]
    wit = wit_ref[0]
    qih = [qi[:, h * IDX_DIM:(h + 1) * IDX_DIM] for h in range(IDX_HEADS)]

    def key_pos(t):
        return t * kt + lax.broadcasted_iota(I32, (kt, tq), 0)

    def score_tile(t, _):
        r0 = pl.multiple_of(t * kt, kt)
        ki_t = ki_ref[0, pl.ds(r0, kt), :]
        sc = jnp.zeros((kt, tq), F32)
        for h in range(IDX_HEADS):
            rel = lax.dot_general(ki_t, qih[h], (((1,), (1,)), ((), ())), preferred_element_type=F32)
            sc = sc + jnp.maximum(rel, 0.0) * wit[h:h + 1, :]
        bits = lax.bitcast_convert_type(sc, I32)
        key = jnp.where(bits < 0, bits ^ jnp.int32(0x7FFFFFFF), bits)
        key = jnp.where(key_pos(t) <= q_pos, key, jnp.int32(INT_MIN))
        key_s[pl.ds(r0, kt), :] = key
        return 0

    lax.fori_loop(0, nkt, score_tile, 0)

    def count(pred_fn):
        def body(t, acc):
            r0 = pl.multiple_of(t * kt, kt)
            m = pred_fn(key_s[pl.ds(r0, kt), :], t)
            ones = jnp.where(m, 1.0, 0.0).reshape(kt // 8, 8, tq)
            return acc + jnp.sum(ones, axis=0)
        acc = lax.fori_loop(0, nkt, body, jnp.zeros((8, tq), F32))
        return jnp.sum(acc, axis=0, keepdims=True)

    def bit_step(i, u):
        bit = jnp.left_shift(jnp.int32(1), 31 - i)
        cand_u = u | bit
        cand_s = cand_u ^ jnp.int32(INT_MIN)
        cnt = count(lambda kk, t: kk >= cand_s)
        return jnp.where(cnt >= k_eff, cand_u, u)

    u_thr = lax.fori_loop(0, 32, bit_step, jnp.zeros((1, tq), I32))
    thr = u_thr ^ jnp.int32(INT_MIN)

    cnt_ge = count(lambda kk, t: kk >= thr)
    cnt_gt = count(lambda kk, t: kk > thr)
    need_eq = k_eff - cnt_gt
    has_tie = jnp.max(cnt_ge - k_eff) > 0.0

    def tie_cut():
        def pos_step(i, c):
            bit = jnp.left_shift(jnp.int32(1), seq_bits - 1 - i)
            cand = c | bit
            cnt = count(lambda kk, t: (kk == thr) & (key_pos(t) < cand))
            return jnp.where(cnt < need_eq, cand, c)
        return lax.fori_loop(0, seq_bits, pos_step, jnp.zeros((1, tq), I32))

    cut = lax.cond(has_tie, tie_cut, lambda: jnp.full((1, tq), 2 ** seq_bits, I32))

    q = q_ref[0]
    grp = ATTN_HEADS // ATTN_KV_HEADS
    for n in range(ATTN_KV_HEADS):
        qn = jnp.concatenate([q[:, (n * grp + g) * HEAD_DIM:(n * grp + g + 1) * HEAD_DIM]
                              for g in range(grp)], axis=0)

        def attn_tile(t, carry, n=n, qn=qn):
            m, l, acc = carry
            r0 = pl.multiple_of(t * kt, kt)
            key = key_s[pl.ds(r0, kt), :]
            sel = (key > thr) | ((key == thr) & (key_pos(t) <= cut))
            k_t = k_ref[0, n, pl.ds(r0, kt), :]
            lg = lax.dot_general(k_t, qn, (((1,), (1,)), ((), ())), preferred_element_type=F32)
            sel4 = jnp.concatenate([sel] * grp, axis=1)
            lg = jnp.where(sel4, lg, NEG_BIG)
            m_new = jnp.maximum(m, jnp.max(lg, axis=0, keepdims=True))
            p = jnp.where(sel4, jnp.exp(lg - m_new), 0.0)
            alpha = jnp.exp(m - m_new)
            l = alpha * l + jnp.sum(p, axis=0, keepdims=True)
            v_t = vt_ref[0, n, t]
            acc = alpha * acc + jnp.dot(v_t, p.astype(BF16), preferred_element_type=F32)
            return m_new, l, acc

        init = (jnp.full((1, grp * tq), NEG_BIG, F32), jnp.zeros((1, grp * tq), F32),
                jnp.zeros((HEAD_DIM, grp * tq), F32))
        m, l, acc = lax.fori_loop(0, nkt, attn_tile, init)
        o_ref[0, 0, n] = (acc / l).astype(BF16)


def _dsa(q, k, v, qi, ki, wi, tq, kt):
    bsz, seq, _ = q.shape
    topk = min(TOPK_MAX, seq // 4)
    nqb = seq // tq
    grp = ATTN_HEADS // ATTN_KV_HEADS
    seq_bits = int(math.log2(seq))
    assert 2 ** seq_bits == seq
    wit = wi.transpose(0, 2, 1)
    k4 = k.reshape(bsz, seq, ATTN_KV_HEADS, HEAD_DIM).transpose(0, 2, 1, 3)
    vt = v.reshape(bsz, seq // kt, kt, ATTN_KV_HEADS, HEAD_DIM).transpose(0, 3, 1, 4, 2)
    kern = functools.partial(_dsa_kernel, tq=tq, kt=kt, topk=topk, seq_bits=seq_bits)
    o_t = pl.pallas_call(
        kern,
        grid=(bsz, nqb),
        in_specs=[pl.BlockSpec((1, tq, IDX_HEADS * IDX_DIM), lambda b, j: (b, j, 0)),
                  pl.BlockSpec((1, IDX_HEADS, tq), lambda b, j: (b, 0, j)),
                  pl.BlockSpec((1, tq, ATTN_HEADS * HEAD_DIM), lambda b, j: (b, j, 0)),
                  pl.BlockSpec((1, seq, IDX_DIM), lambda b, j: (b, 0, 0)),
                  pl.BlockSpec((1, ATTN_KV_HEADS, seq, HEAD_DIM), lambda b, j: (b, 0, 0, 0)),
                  pl.BlockSpec((1, ATTN_KV_HEADS, seq // kt, HEAD_DIM, kt), lambda b, j: (b, 0, 0, 0, 0))],
        out_specs=pl.BlockSpec((1, 1, ATTN_KV_HEADS, HEAD_DIM, grp * tq), lambda b, j: (b, j, 0, 0, 0)),
        out_shape=jax.ShapeDtypeStruct((bsz, nqb, ATTN_KV_HEADS, HEAD_DIM, grp * tq), BF16),
        scratch_shapes=[pltpu.VMEM((seq, tq), I32)],
        compiler_params=_cparams(("arbitrary", "arbitrary")),
        name="dsa",
    )(qi, wit, q, ki, k4, vt)
    o = o_t.reshape(bsz, nqb, ATTN_KV_HEADS, HEAD_DIM, grp, tq).transpose(0, 1, 5, 2, 4, 3)
    return o.reshape(bsz, seq, ATTN_HEADS * HEAD_DIM)


def _merge_kernel(x_ref, ys_ref, ya_ref, gs_ref, ga_ref, wsu_ref, wau_ref, wout_ref, g2_ref, wq_ref,
                  h_ref, hn_ref, qp_ref):
    ms = jnp.dot(ys_ref[...], wsu_ref[...], preferred_element_type=F32)
    ma = jnp.dot(ya_ref[...], wau_ref[...], preferred_element_type=F32)
    merged = gs_ref[...].astype(F32) * ms + ga_ref[...].astype(F32) * ma
    h = x_ref[...] + jnp.dot(merged.astype(BF16), wout_ref[...], preferred_element_type=F32)
    h_ref[...] = h
    hn = h * lax.rsqrt(jnp.mean(h * h, axis=-1, keepdims=True) + NORM_EPS) * g2_ref[...]
    hn_ref[...] = hn
    qp_ref[...] = jnp.dot(hn.astype(BF16), wq_ref[...], preferred_element_type=F32).astype(BF16)


def _merge(x2, ys, ya, gs, ga, w_ssm_up, w_attn_up, w_out, norm2_g, peer_wq, tm):
    n, d = x2.shape
    row = lambda a: pl.BlockSpec((tm, a.shape[1]), lambda i: (i, 0))
    full = lambda a: pl.BlockSpec(a.shape, lambda i: (0,) * a.ndim)
    wsu = w_ssm_up.astype(BF16)
    wau = w_attn_up.astype(BF16)
    wo = w_out.astype(BF16)
    wq = peer_wq.astype(BF16)
    g2 = norm2_g.reshape(1, d)
    nq = wq.shape[1]
    return pl.pallas_call(
        _merge_kernel,
        grid=(n // tm,),
        in_specs=[row(x2), row(ys), row(ya), row(gs), row(ga), full(wsu), full(wau), full(wo), full(g2), full(wq)],
        out_specs=[pl.BlockSpec((tm, d), lambda i: (i, 0)), pl.BlockSpec((tm, d), lambda i: (i, 0)),
                   pl.BlockSpec((tm, nq), lambda i: (i, 0))],
        out_shape=[jax.ShapeDtypeStruct((n, d), F32), jax.ShapeDtypeStruct((n, d), F32),
                   jax.ShapeDtypeStruct((n, nq), BF16)],
        compiler_params=_cparams(("arbitrary",)),
        name="merge",
    )(x2, ys, ya, gs, ga, wsu, wau, wo, g2, wq)


def _cand_layout():
    blocks = []
    blocks.append((0, 16, 16))
    for i in range(1, 8):
        blocks.append((i, 8, PEER_TOPK // (i + 1)))
    blocks.append((None, 8, 8))
    return blocks


def _top_rows(s, order, payload, k):
    big = jnp.float32(3e38)
    vals, pays = [], []
    for _ in range(k):
        m = jnp.max(s, axis=0, keepdims=True)
        o = jnp.min(jnp.where(s == m, order, big), axis=0, keepdims=True)
        hit = order == o
        pays.append(jnp.min(jnp.where(hit, payload, big), axis=0, keepdims=True))
        vals.append(m)
        s = jnp.where(hit, -jnp.inf, s)
    return jnp.concatenate(vals, axis=0), jnp.concatenate(pays, axis=0)


def _route_kernel(qp_ref, k1_ref, k2_ref, e_ref, g_ref, *, tt):
    qp = qp_ref[...]
    kd = PEER_KEY_DIM
    rows_k = lax.broadcasted_iota(I32, (PEER_KEYS, tt), 0).astype(F32)
    for h in range(PEER_HEADS):
        q1 = qp[:, (2 * h) * kd:(2 * h + 1) * kd]
        q2 = qp[:, (2 * h + 1) * kd:(2 * h + 2) * kd]
        s1 = lax.dot_general(k1_ref[h], q1, (((1,), (1,)), ((), ())), preferred_element_type=F32)
        s2 = lax.dot_general(k2_ref[h], q2, (((1,), (1,)), ((), ())), preferred_element_type=F32)
        v1, i1 = _top_rows(s1, rows_k, rows_k, PEER_TOPK)
        v2, i2 = _top_rows(s2, rows_k, rows_k, PEER_TOPK)
        cs, ce, co = [], [], []
        for i, rows, valid in _cand_layout():
            r = lax.broadcasted_iota(I32, (rows, tt), 0).astype(F32)
            if i is None:
                val = v1[8:16] + v2[0:1]
                eid = i1[8:16] * PEER_KEYS + i2[0:1]
                flat = (r + 8.0) * PEER_TOPK
            else:
                val = v1[i:i + 1] + v2[0:rows]
                eid = i1[i:i + 1] * PEER_KEYS + i2[0:rows]
                flat = r + float(i * PEER_TOPK)
                if valid < rows:
                    val = jnp.where(r < float(valid), val, -jnp.inf)
            cs.append(val)
            ce.append(eid)
            co.append(flat)
        cand = jnp.concatenate(cs, axis=0)
        top_s, top_e = _top_rows(cand, jnp.concatenate(co, axis=0), jnp.concatenate(ce, axis=0), PEER_TOPK)
        p = jnp.exp(top_s - top_s[0:1])
        gates = p / jnp.sum(p, axis=0, keepdims=True)
        e_ref[h * PEER_TOPK:(h + 1) * PEER_TOPK, :] = top_e.astype(I32)
        g_ref[h * PEER_TOPK:(h + 1) * PEER_TOPK, :] = gates


def _route(qp, peer_k1, peer_k2, tt):
    n, nq = qp.shape
    k1 = peer_k1.astype(BF16)
    k2 = peer_k2.astype(BF16)
    n_sel = PEER_HEADS * PEER_TOPK
    full = lambda a: pl.BlockSpec(a.shape, lambda i: (0,) * a.ndim)
    return pl.pallas_call(
        functools.partial(_route_kernel, tt=tt),
        grid=(n // tt,),
        in_specs=[pl.BlockSpec((tt, nq), lambda i: (i, 0)), full(k1), full(k2)],
        out_specs=[pl.BlockSpec((n_sel, tt), lambda i: (0, i)), pl.BlockSpec((n_sel, tt), lambda i: (0, i))],
        out_shape=[jax.ShapeDtypeStruct((n_sel, n), I32), jax.ShapeDtypeStruct((n_sel, n), F32)],
        compiler_params=_cparams(("arbitrary",)),
        name="route",
    )(qp, k1, k2)


def _final_kernel(h_ref, p_ref, g_ref, o_ref):
    h = h_ref[...] + p_ref[...]
    o_ref[...] = h * lax.rsqrt(jnp.mean(h * h, axis=-1, keepdims=True) + NORM_EPS) * g_ref[...]


def _final(h, p, g, tm):
    n, d = h.shape
    row = pl.BlockSpec((tm, d), lambda i: (i, 0))
    return pl.pallas_call(
        _final_kernel,
        grid=(n // tm,),
        in_specs=[row, row, pl.BlockSpec((1, d), lambda i: (0, 0))],
        out_specs=row,
        out_shape=jax.ShapeDtypeStruct((n, d), F32),
        compiler_params=_cparams(("arbitrary",)),
        name="final",
    )(h, p, g.reshape(1, d))


SC_CORES_V7X = 2
SC_SUBCORES_V7X = 16
SC_LANES_V7X = 16
PEER_TOK_BATCH = 8
PEER_ROW_CHUNK = 32


def _peer_sc_body(hn_hbm, e_hbm, g_hbm, u_hbm, v_hbm, out_hbm,
                  idx_v, gate_v, x_v, out_v, rows, p_v, act_v, sem, *, tpw, d, n_sel):
    nl = SC_LANES_V7X
    tb = PEER_TOK_BATCH
    rc = PEER_ROW_CHUNK
    n_chunk = n_sel // rc
    jobs_per_tok = 2 * n_chunk
    n_lane_blk = d // nl
    wid = lax.axis_index("s") * SC_CORES_V7X + lax.axis_index("c")
    base = wid * tpw
    lane = lax.iota(I32, nl)
    zero = jnp.zeros((nl,), F32)
    c_gelu = 2.0 * math.sqrt(2.0 / math.pi)

    def gather_copy(job):
        tok = job // jobs_per_tok
        j = job % jobs_per_tok
        c = j % n_chunk
        b = job % 2
        idx = idx_v.at[tok, pl.ds(c * rc, rc)]
        return j, (pltpu.make_async_copy(u_hbm.at[idx], rows.at[b], sem.at[b]),
                   pltpu.make_async_copy(v_hbm.at[idx], rows.at[b], sem.at[b]))

    def start(job):
        j, (cu, cv) = gather_copy(job)

        @pl.when(j < n_chunk)
        def _():
            cu.start()

        @pl.when(j >= n_chunk)
        def _():
            cv.start()

    def compute_u(tok, c, b):
        def rg_body(rg, _):
            r0 = rg * 8

            def jbody(j, accs):
                off = j * nl
                xv = x_v[tok, pl.ds(off, nl)]
                return tuple(accs[r] + rows[b, r0 + r, pl.ds(off, nl)] * xv for r in range(8))

            accs = lax.fori_loop(0, n_lane_blk, jbody, (zero,) * 8, unroll=2)
            for r in range(8):
                p_v[c * rc + r0 + r, :] = accs[r]
            return 0

        lax.fori_loop(0, rc // 8, rg_body, 0)

    def finish_act(tok):
        def eg_body(eg, _):
            e0 = eg * nl
            ridx = e0 + lane
            s = zero
            for l in range(nl):
                s = s + plsc.load_gather(p_v, [ridx, jnp.full((nl,), l, I32)])
            inner = c_gelu * (s + 0.044715 * (s * s * s))
            gl = s / (1.0 + jnp.exp(-inner))
            act_v[pl.ds(e0, nl)] = gl * gate_v[tok, pl.ds(e0, nl)]
            return 0

        lax.fori_loop(0, n_sel // nl, eg_body, 0)

        def zbody(j, _):
            out_v[tok, pl.ds(j * nl, nl)] = zero
            return 0

        lax.fori_loop(0, n_lane_blk, zbody, 0, unroll=4)

    def compute_v(tok, c, b):
        def rg_body(rg, _):
            r0 = rg * nl
            splat = [plsc.load_gather(act_v, [jnp.full((nl,), 0, I32) + (c * rc + r0 + r)]) for r in range(nl)]

            def jbody(j, _):
                off = j * nl
                parts = [splat[r] * rows[b, r0 + r, pl.ds(off, nl)] for r in range(nl)]
                while len(parts) > 1:
                    parts = [parts[i] + parts[i + 1] for i in range(0, len(parts), 2)]
                out_v[tok, pl.ds(off, nl)] = out_v[tok, pl.ds(off, nl)] + parts[0]
                return 0

            lax.fori_loop(0, n_lane_blk, jbody, 0, unroll=2)
            return 0

        lax.fori_loop(0, rc // nl, rg_body, 0)

    def batch_body(bi, _):
        t0 = base + bi * tb
        pltpu.sync_copy(e_hbm.at[pl.ds(t0, tb)], idx_v)
        pltpu.sync_copy(g_hbm.at[pl.ds(t0, tb)], gate_v)
        pltpu.sync_copy(hn_hbm.at[pl.ds(t0, tb)], x_v)
        start(0)

        def job_body(job, _):
            @pl.when(job + 1 < tb * jobs_per_tok)
            def _():
                start(job + 1)

            j, (cu, _cv) = gather_copy(job)
            cu.wait()
            tok = job // jobs_per_tok
            b = job % 2

            @pl.when(j < n_chunk)
            def _():
                compute_u(tok, j, b)

            @pl.when(j == n_chunk - 1)
            def _():
                finish_act(tok)

            @pl.when(j >= n_chunk)
            def _():
                compute_v(tok, j - n_chunk, b)

            return 0

        lax.fori_loop(0, tb * jobs_per_tok, job_body, 0)
        pltpu.sync_copy(out_v, out_hbm.at[pl.ds(t0, tb)])
        return 0

    lax.fori_loop(0, tpw // tb, batch_body, 0)


def _peer_sc(hn, experts, gates, u_tab, v_tab):
    n, d = hn.shape
    n_sel = experts.shape[1]
    nw = SC_CORES_V7X * SC_SUBCORES_V7X
    tpw = n // nw
    mesh = plsc.VectorSubcoreMesh(core_axis_name="c", subcore_axis_name="s",
                                  num_cores=SC_CORES_V7X, num_subcores=SC_SUBCORES_V7X)
    body = functools.partial(_peer_sc_body, tpw=tpw, d=d, n_sel=n_sel)
    call = pl.kernel(
        body,
        out_type=jax.ShapeDtypeStruct((n, d), F32),
        mesh=mesh,
        scratch_types=[pltpu.VMEM((PEER_TOK_BATCH, n_sel), I32),
                       pltpu.VMEM((PEER_TOK_BATCH, n_sel), F32),
                       pltpu.VMEM((PEER_TOK_BATCH, d), F32),
                       pltpu.VMEM((PEER_TOK_BATCH, d), F32),
                       pltpu.VMEM((2, PEER_ROW_CHUNK, d), F32),
                       pltpu.VMEM((n_sel, SC_LANES_V7X), F32),
                       pltpu.VMEM((n_sel,), F32),
                       pltpu.SemaphoreType.DMA((2,))],
        compiler_params=pltpu.CompilerParams(needs_layout_passes=False),
        name="peer_sc",
    )
    return call(hn, experts, gates, u_tab, v_tab)


def _peer_experts_xla(hn, experts, gates, u_tab, v_tab):
    def chunk(args):
        xc, ec, gc = args
        u = u_tab[ec]
        act = jax.nn.gelu(jnp.einsum('ced,cd->ce', u, xc)) * gc
        return jnp.einsum('ce,ced->cd', act, v_tab[ec])
    n, d = hn.shape
    c = 128
    out = lax.map(chunk, (hn.reshape(n // c, c, d), experts.reshape(n // c, c, -1), gates.reshape(n // c, c, -1)))
    return out.reshape(n, d)


def kernel(x, norm1_g, w_in, a_re, a_im, log_dt, b_re, b_im, c_re, c_im, d_skip, w_glu, w_ssm_up, w_attn_up,
           w_out, norm2_g, peer_wq, peer_k1, peer_k2, peer_u, peer_v, norm_f_g):
    bsz, seq, d = x.shape
    n = bsz * seq
    h = x
    depth = norm1_g.shape[0]
    for layer in range(depth):
        u, q, k, v, qi, ki, wi, gs, ga = _inproj(h, norm1_g[layer], w_in[layer], tm=min(512, seq))
        d_ssm = u.shape[-1]
        u_tb = u.transpose(1, 0, 2).reshape(n, d_ssm)
        y_tb = _s5(u_tb, a_re[layer], a_im[layer], log_dt[layer], b_re[layer], b_im[layer], c_re[layer],
                   c_im[layer], d_skip[layer], w_glu[layer], nb=bsz, tc=64)
        ys = y_tb.reshape(seq, bsz, d_ssm).transpose(1, 0, 2).reshape(n, d_ssm)
        ya = _dsa(q, k, v, qi, ki, wi, tq=128, kt=256).reshape(n, -1)
        hm, hn, qp = _merge(h.reshape(n, d), ys, ya, gs.reshape(n, d), ga.reshape(n, d), w_ssm_up[layer],
                            w_attn_up[layer], w_out[layer], norm2_g[layer], peer_wq[layer], tm=512)
        e_t, g_t = _route(qp, peer_k1[layer], peer_k2[layer], tt=256)
        po = _peer_sc(hn, e_t.T, g_t.T, peer_u[layer], peer_v[layer])
        if layer + 1 < depth:
            h = (hm + po).reshape(bsz, seq, d)
    return _final(hm, po, norm_f_g, tm=512).reshape(bsz, seq, d)
```

```python
import functools
import math

import numpy as np
import jax
import jax.numpy as jnp
from jax import lax
from jax.experimental import pallas as pl
from jax.experimental.pallas import tpu as pltpu
from jax.experimental.pallas import tpu_sc as plsc

F32 = jnp.float32
BF16 = jnp.bfloat16
I32 = jnp.int32

SSM_GROUP = 16
SSM_STATE = 64
ATTN_HEADS = 8
ATTN_KV_HEADS = 2
HEAD_DIM = 64
IDX_HEADS = 8
IDX_DIM = 32
TOPK_MAX = 256
ROPE_THETA = 10000.0
NEG_BIG = -1e30
PEER_HEADS = 8
PEER_KEYS = 128
PEER_KEY_DIM = 128
PEER_TOPK = 16
NORM_EPS = 1e-6

INT_MIN = -(2 ** 31)
VMEM_LIMIT = 56 * 1024 * 1024


def _cparams(sem):
    return pltpu.CompilerParams(dimension_semantics=sem, vmem_limit_bytes=VMEM_LIMIT)


def _gelu_tanh(x):
    return 0.5 * x * (1.0 + jnp.tanh(math.sqrt(2.0 / math.pi) * (x + 0.044715 * (x * x * x))))


def _sigmoid(x):
    return 1.0 / (1.0 + jnp.exp(-x))


def _rot_cols(w, hd):
    d, n = w.shape
    w3 = w.reshape(d, n // hd, hd)
    half = hd // 2
    return jnp.concatenate([-w3[..., half:], w3[..., :half]], axis=-1).reshape(d, n)


def _rope_full(seq, hd, heads):
    pos = jnp.arange(seq, dtype=F32)
    inv = ROPE_THETA ** (-jnp.arange(0, hd, 2, dtype=F32) / hd)
    ang = pos[:, None] * inv[None, :]
    c = jnp.concatenate([jnp.cos(ang), jnp.cos(ang)], axis=-1)
    s = jnp.concatenate([jnp.sin(ang), jnp.sin(ang)], axis=-1)
    return jnp.tile(c, (1, heads)), jnp.tile(s, (1, heads))


def _inproj_kernel(x_ref, g_ref, w_ref, wr_ref, cs_ref, sn_ref,
                   u_ref, q_ref, k_ref, v_ref, qi_ref, ki_ref, wi_ref, gs_ref, ga_ref,
                   *, d_ssm, d_q, d_kv, d_qi, d_ki, n_wi, d_model, q_scale, wi_scale):
    x = x_ref[0]
    xn = x * lax.rsqrt(jnp.mean(x * x, axis=-1, keepdims=True) + NORM_EPS) * g_ref[...]
    xb = xn.astype(BF16)

    def mm(ref, lo, n):
        return jnp.dot(xb, ref[:, lo:lo + n], preferred_element_type=F32)

    o = 0
    u_ref[0] = mm(w_ref, o, d_ssm).astype(BF16)
    o += d_ssm
    ro = 0
    q = mm(w_ref, o, d_q) * cs_ref[:, ro:ro + d_q] + mm(wr_ref, ro, d_q) * sn_ref[:, ro:ro + d_q]
    q_ref[0] = (q * q_scale).astype(BF16)
    o += d_q
    ro += d_q
    k = mm(w_ref, o, d_kv) * cs_ref[:, ro:ro + d_kv] + mm(wr_ref, ro, d_kv) * sn_ref[:, ro:ro + d_kv]
    k_ref[0] = k.astype(BF16)
    o += d_kv
    ro += d_kv
    v_ref[0] = mm(w_ref, o, d_kv).astype(BF16)
    o += d_kv
    qi = mm(w_ref, o, d_qi) * cs_ref[:, ro:ro + d_qi] + mm(wr_ref, ro, d_qi) * sn_ref[:, ro:ro + d_qi]
    qi_ref[0] = qi.astype(BF16)
    o += d_qi
    ro += d_qi
    seg = mm(w_ref, o, 128)
    segr = mm(wr_ref, ro, 128)
    kiw = seg * cs_ref[:, ro:ro + 128] + segr * sn_ref[:, ro:ro + 128]
    ki_ref[0] = kiw[:, :d_ki].astype(BF16)
    wi_ref[0] = seg[:, d_ki:d_ki + n_wi] * wi_scale
    o += 128
    gs_ref[0] = _sigmoid(mm(w_ref, o, d_model)).astype(BF16)
    o += d_model
    ga_ref[0] = _sigmoid(mm(w_ref, o, d_model)).astype(BF16)


def _inproj(x, norm_g, w_in, tm):
    bsz, seq, d = x.shape
    d_ssm = d // 2
    d_q = ATTN_HEADS * HEAD_DIM
    d_kv = ATTN_KV_HEADS * HEAD_DIM
    d_qi = IDX_HEADS * IDX_DIM
    d_ki = IDX_DIM
    n_wi = IDX_HEADS
    splits = (d_ssm, d_q, d_kv, d_kv, d_qi, d_ki, n_wi, d, d)
    offs = np.cumsum(splits)[:-1].tolist()
    wu, wq, wk, wv, wqi, wki, wwi, wgs, wga = jnp.split(w_in, offs, axis=1)
    pad = jnp.zeros((d, 128 - d_ki - n_wi), F32)
    w_main = jnp.concatenate([wu, wq, wk, wv, wqi, wki, wwi, pad, wgs, wga], axis=1).astype(BF16)
    rpad = jnp.zeros((d, 128 - d_ki), F32)
    w_rot = jnp.concatenate([_rot_cols(wq, HEAD_DIM), _rot_cols(wk, HEAD_DIM),
                             _rot_cols(wqi, IDX_DIM), _rot_cols(wki, IDX_DIM), rpad], axis=1).astype(BF16)
    cq, sq = _rope_full(seq, HEAD_DIM, ATTN_HEADS)
    ck, sk = _rope_full(seq, HEAD_DIM, ATTN_KV_HEADS)
    cqi, sqi = _rope_full(seq, IDX_DIM, IDX_HEADS)
    cki, ski = _rope_full(seq, IDX_DIM, 1)
    tpad = jnp.zeros((seq, 128 - d_ki), F32)
    cs = jnp.concatenate([cq, ck, cqi, cki, tpad], axis=1)
    sn = jnp.concatenate([sq, sk, sqi, ski, tpad], axis=1)
    n_main = w_main.shape[1]
    n_rot = w_rot.shape[1]

    kern = functools.partial(
        _inproj_kernel, d_ssm=d_ssm, d_q=d_q, d_kv=d_kv, d_qi=d_qi, d_ki=d_ki, n_wi=n_wi, d_model=d,
        q_scale=HEAD_DIM ** -0.5, wi_scale=(IDX_HEADS ** -0.5) * (IDX_DIM ** -0.5))
    tok = lambda n: pl.BlockSpec((1, tm, n), lambda s, b: (b, s, 0))
    full = lambda shape: pl.BlockSpec(shape, lambda s, b: (0,) * len(shape))
    outs = [(d_ssm, BF16), (d_q, BF16), (d_kv, BF16), (d_kv, BF16), (d_qi, BF16), (d_ki, BF16),
            (n_wi, F32), (d, BF16), (d, BF16)]
    return pl.pallas_call(
        kern,
        grid=(seq // tm, bsz),
        in_specs=[tok(d), full((1, d)), full((d, n_main)), full((d, n_rot)),
                  pl.BlockSpec((tm, n_rot), lambda s, b: (s, 0)),
                  pl.BlockSpec((tm, n_rot), lambda s, b: (s, 0))],
        out_specs=[tok(n) for n, _ in outs],
        out_shape=[jax.ShapeDtypeStruct((bsz, seq, n), dt) for n, dt in outs],
        compiler_params=_cparams(("arbitrary", "arbitrary")),
        name="inproj",
    )(x, norm_g.reshape(1, d), w_main, w_rot, cs, sn)


def _s5_kernel(u_ref, bre_ref, bim_ref, cre_ref, cim_ref, are_ref, aim_ref, dsk_ref, wglu_ref,
               y_ref, sre, sim, st_re, st_im, *, tc, nb, lane_chunk):
    @pl.when(pl.program_id(0) == 0)
    def _():
        st_re[...] = jnp.zeros_like(st_re)
        st_im[...] = jnp.zeros_like(st_im)

    u = u_ref[...]
    n_half = bre_ref.shape[0]
    hin = bre_ref.shape[1]
    hst = bre_ref.shape[2]
    for h in range(n_half):
        uh = u[:, h * hin:(h + 1) * hin]
        sre[:, h * hst:(h + 1) * hst] = jnp.dot(uh, bre_ref[h], preferred_element_type=F32)
        sim[:, h * hst:(h + 1) * hst] = jnp.dot(uh, bim_ref[h], preferred_element_type=F32)

    n_state = sre.shape[1]
    for c in range(n_state // lane_chunk):
        cols = slice(c * lane_chunk, (c + 1) * lane_chunk)
        ar = are_ref[:, cols]
        ai = aim_ref[:, cols]

        def step(t, carry, cols=cols, ar=ar, ai=ai):
            sr, si = carry
            r0 = pl.multiple_of(t * nb, nb)
            nr = ar * sr - ai * si + sre[pl.ds(r0, nb), cols]
            ni = ar * si + ai * sr + sim[pl.ds(r0, nb), cols]
            sre[pl.ds(r0, nb), cols] = nr
            sim[pl.ds(r0, nb), cols] = ni
            return nr, ni

        sr, si = lax.fori_loop(0, tc, step, (st_re[:, cols], st_im[:, cols]), unroll=4)
        st_re[:, cols] = sr
        st_im[:, cols] = si

    ys = []
    for h in range(n_half):
        srh = sre[:, h * hst:(h + 1) * hst].astype(BF16)
        sih = sim[:, h * hst:(h + 1) * hst].astype(BF16)
        ys.append(jnp.dot(srh, cre_ref[h], preferred_element_type=F32)
                  - jnp.dot(sih, cim_ref[h], preferred_element_type=F32))
    y = jnp.concatenate(ys, axis=-1) + dsk_ref[...] * u.astype(F32)
    y = _gelu_tanh(y)
    gate = jnp.dot(y.astype(BF16), wglu_ref[...], preferred_element_type=F32)
    y_ref[...] = (y * _sigmoid(gate)).astype(BF16)


def _s5(u_tb, a_re, a_im, log_dt, b_re, b_im, c_re, c_im, d_skip, w_glu, nb, tc):
    rows, d_ssm = u_tb.shape
    groups = d_ssm // SSM_GROUP
    n_state = groups * SSM_STATE
    lam = lax.complex(a_re, a_im)
    dt = jnp.exp(log_dt)[:, None]
    a_bar = jnp.exp(lam * dt)
    b_bar = ((a_bar - 1.0) / lam)[..., None] * lax.complex(b_re, b_im)
    gh = min(groups, 256 // SSM_GROUP)
    n_half = groups // gh
    eye = jnp.eye(gh, dtype=F32)

    def bmat(bb):
        b4 = bb.reshape(n_half, gh, SSM_STATE, SSM_GROUP)
        return jnp.einsum('hgpc,gk->hgckp', b4, eye).reshape(n_half, gh * SSM_GROUP, gh * SSM_STATE)

    def cmat(cc):
        c4 = cc.reshape(n_half, gh, SSM_GROUP, SSM_STATE)
        return jnp.einsum('hgcp,gk->hgpkc', c4, eye).reshape(n_half, gh * SSM_STATE, gh * SSM_GROUP)

    bre = bmat(jnp.real(b_bar)).astype(BF16)
    bim = bmat(jnp.imag(b_bar)).astype(BF16)
    cre = cmat(c_re).astype(BF16)
    cim = cmat(c_im).astype(BF16)
    are = jnp.broadcast_to(jnp.real(a_bar).reshape(1, n_state), (nb, n_state))
    aim = jnp.broadcast_to(jnp.imag(a_bar).reshape(1, n_state), (nb, n_state))
    blk = tc * nb
    full = lambda a: pl.BlockSpec(a.shape, lambda i: (0,) * a.ndim)
    dsk = d_skip.reshape(1, d_ssm)
    wg = w_glu.astype(BF16)
    kern = functools.partial(_s5_kernel, tc=tc, nb=nb, lane_chunk=512)
    return pl.pallas_call(
        kern,
        grid=(rows // blk,),
        in_specs=[pl.BlockSpec((blk, d_ssm), lambda i: (i, 0)),
                  full(bre), full(bim), full(cre), full(cim), full(are), full(aim), full(dsk), full(wg)],
        out_specs=pl.BlockSpec((blk, d_ssm), lambda i: (i, 0)),
        out_shape=jax.ShapeDtypeStruct((rows, d_ssm), BF16),
        scratch_shapes=[pltpu.VMEM((blk, n_state), F32), pltpu.VMEM((blk, n_state), F32),
                        pltpu.VMEM((nb, n_state), F32), pltpu.VMEM((nb, n_state), F32)],
        compiler_params=_cparams(("arbitrary",)),
        name="s5",
    )(u_tb, bre, bim, cre, cim, are, aim, dsk, wg)


def _dsa_kernel(qi_ref, wit_ref, q_ref, ki_ref, k_ref, vt_ref, o_ref, key_s,
                *, tq, kt, topk, seq_bits):
    qb = pl.program_id(1)
    nkt = (qb * tq + tq + kt - 1) // kt
    q_pos = qb * tq + lax.broadcasted_iota(I32, (1, tq), 1)
    k_eff = jnp.minimum(topk, q_pos + 1).astype(F32)

    qi = qi_ref[0]
    wit = wit_ref[0]
    qih = [qi[:, h * IDX_DIM:(h + 1) * IDX_DIM] for h in range(IDX_HEADS)]

    def key_pos(t):
        return t * kt + lax.broadcasted_iota(I32, (kt, tq), 0)

    def score_tile(t, _):
        r0 = pl.multiple_of(t * kt, kt)
        ki_t = ki_ref[0, pl.ds(r0, kt), :]
        sc = jnp.zeros((kt, tq), F32)
        for h in range(IDX_HEADS):
            rel = lax.dot_general(ki_t, qih[h], (((1,), (1,)), ((), ())), preferred_element_type=F32)
            sc = sc + jnp.maximum(rel, 0.0) * wit[h:h + 1, :]
        bits = lax.bitcast_convert_type(sc, I32)
        key = jnp.where(bits < 0, bits ^ jnp.int32(0x7FFFFFFF), bits)
        key = jnp.where(key_pos(t) <= q_pos, key, jnp.int32(INT_MIN))
        key_s[pl.ds(r0, kt), :] = key
        return 0

    lax.fori_loop(0, nkt, score_tile, 0)

    def count(pred_fn):
        def body(t, acc):
            r0 = pl.multiple_of(t * kt, kt)
            m = pred_fn(key_s[pl.ds(r0, kt), :], t)
            ones = jnp.where(m, 1.0, 0.0).reshape(kt // 8, 8, tq)
            return acc + jnp.sum(ones, axis=0)
        acc = lax.fori_loop(0, nkt, body, jnp.zeros((8, tq), F32))
        return jnp.sum(acc, axis=0, keepdims=True)

    def bit_step(i, u):
        bit = jnp.left_shift(jnp.int32(1), 31 - i)
        cand_u = u | bit
        cand_s = cand_u ^ jnp.int32(INT_MIN)
        cnt = count(lambda kk, t: kk >= cand_s)
        return jnp.where(cnt >= k_eff, cand_u, u)

    u_thr = lax.fori_loop(0, 32, bit_step, jnp.zeros((1, tq), I32))
    thr = u_thr ^ jnp.int32(INT_MIN)

    cnt_ge = count(lambda kk, t: kk >= thr)
    cnt_gt = count(lambda kk, t: kk > thr)
    need_eq = k_eff - cnt_gt
    has_tie = jnp.max(cnt_ge - k_eff) > 0.0

    def tie_cut():
        def pos_step(i, c):
            bit = jnp.left_shift(jnp.int32(1), seq_bits - 1 - i)
            cand = c | bit
            cnt = count(lambda kk, t: (kk == thr) & (key_pos(t) < cand))
            return jnp.where(cnt < need_eq, cand, c)
        return lax.fori_loop(0, seq_bits, pos_step, jnp.zeros((1, tq), I32))

    cut = lax.cond(has_tie, tie_cut, lambda: jnp.full((1, tq), 2 ** seq_bits, I32))

    q = q_ref[0]
    grp = ATTN_HEADS // ATTN_KV_HEADS
    for n in range(ATTN_KV_HEADS):
        qn = jnp.concatenate([q[:, (n * grp + g) * HEAD_DIM:(n * grp + g + 1) * HEAD_DIM]
                              for g in range(grp)], axis=0)

        def attn_tile(t, carry, n=n, qn=qn):
            m, l, acc = carry
            r0 = pl.multiple_of(t * kt, kt)
            key = key_s[pl.ds(r0, kt), :]
            sel = (key > thr) | ((key == thr) & (key_pos(t) <= cut))
            k_t = k_ref[0, n, pl.ds(r0, kt), :]
            lg = lax.dot_general(k_t, qn, (((1,), (1,)), ((), ())), preferred_element_type=F32)
            sel4 = jnp.concatenate([sel] * grp, axis=1)
            lg = jnp.where(sel4, lg, NEG_BIG)
            m_new = jnp.maximum(m, jnp.max(lg, axis=0, keepdims=True))
            p = jnp.where(sel4, jnp.exp(lg - m_new), 0.0)
            alpha = jnp.exp(m - m_new)
            l = alpha * l + jnp.sum(p, axis=0, keepdims=True)
            v_t = vt_ref[0, n, t]
            acc = alpha * acc + jnp.dot(v_t, p.astype(BF16), preferred_element_type=F32)
            return m_new, l, acc

        init = (jnp.full((1, grp * tq), NEG_BIG, F32), jnp.zeros((1, grp * tq), F32),
                jnp.zeros((HEAD_DIM, grp * tq), F32))
        m, l, acc = lax.fori_loop(0, nkt, attn_tile, init)
        o_ref[0, 0, n] = (acc / l).astype(BF16)


def _dsa(q, k, v, qi, ki, wi, tq, kt):
    bsz, seq, _ = q.shape
    topk = min(TOPK_MAX, seq // 4)
    nqb = seq // tq
    grp = ATTN_HEADS // ATTN_KV_HEADS
    seq_bits = int(math.log2(seq))
    assert 2 ** seq_bits == seq
    wit = wi.transpose(0, 2, 1)
    k4 = k.reshape(bsz, seq, ATTN_KV_HEADS, HEAD_DIM).transpose(0, 2, 1, 3)
    vt = v.reshape(bsz, seq // kt, kt, ATTN_KV_HEADS, HEAD_DIM).transpose(0, 3, 1, 4, 2)
    kern = functools.partial(_dsa_kernel, tq=tq, kt=kt, topk=topk, seq_bits=seq_bits)
    o_t = pl.pallas_call(
        kern,
        grid=(bsz, nqb),
        in_specs=[pl.BlockSpec((1, tq, IDX_HEADS * IDX_DIM), lambda b, j: (b, j, 0)),
                  pl.BlockSpec((1, IDX_HEADS, tq), lambda b, j: (b, 0, j)),
                  pl.BlockSpec((1, tq, ATTN_HEADS * HEAD_DIM), lambda b, j: (b, j, 0)),
                  pl.BlockSpec((1, seq, IDX_DIM), lambda b, j: (b, 0, 0)),
                  pl.BlockSpec((1, ATTN_KV_HEADS, seq, HEAD_DIM), lambda b, j: (b, 0, 0, 0)),
                  pl.BlockSpec((1, ATTN_KV_HEADS, seq // kt, HEAD_DIM, kt), lambda b, j: (b, 0, 0, 0, 0))],
        out_specs=pl.BlockSpec((1, 1, ATTN_KV_HEADS, HEAD_DIM, grp * tq), lambda b, j: (b, j, 0, 0, 0)),
        out_shape=jax.ShapeDtypeStruct((bsz, nqb, ATTN_KV_HEADS, HEAD_DIM, grp * tq), BF16),
        scratch_shapes=[pltpu.VMEM((seq, tq), I32)],
        compiler_params=_cparams(("arbitrary", "arbitrary")),
        name="dsa",
    )(qi, wit, q, ki, k4, vt)
    o = o_t.reshape(bsz, nqb, ATTN_KV_HEADS, HEAD_DIM, grp, tq).transpose(0, 1, 5, 2, 4, 3)
    return o.reshape(bsz, seq, ATTN_HEADS * HEAD_DIM)


def _merge_kernel(x_ref, ys_ref, ya_ref, gs_ref, ga_ref, wsu_ref, wau_ref, wout_ref, g2_ref, wq_ref,
                  h_ref, hn_ref, qp_ref):
    ms = jnp.dot(ys_ref[...], wsu_ref[...], preferred_element_type=F32)
    ma = jnp.dot(ya_ref[...], wau_ref[...], preferred_element_type=F32)
    merged = gs_ref[...].astype(F32) * ms + ga_ref[...].astype(F32) * ma
    h = x_ref[...] + jnp.dot(merged.astype(BF16), wout_ref[...], preferred_element_type=F32)
    h_ref[...] = h
    hn = h * lax.rsqrt(jnp.mean(h * h, axis=-1, keepdims=True) + NORM_EPS) * g2_ref[...]
    hn_ref[...] = hn
    qp_ref[...] = jnp.dot(hn.astype(BF16), wq_ref[...], preferred_element_type=F32).astype(BF16)


def _merge(x2, ys, ya, gs, ga, w_ssm_up, w_attn_up, w_out, norm2_g, peer_wq, tm):
    n, d = x2.shape
    row = lambda a: pl.BlockSpec((tm, a.shape[1]), lambda i: (i, 0))
    full = lambda a: pl.BlockSpec(a.shape, lambda i: (0,) * a.ndim)
    wsu = w_ssm_up.astype(BF16)
    wau = w_attn_up.astype(BF16)
    wo = w_out.astype(BF16)
    wq = peer_wq.astype(BF16)
    g2 = norm2_g.reshape(1, d)
    nq = wq.shape[1]
    return pl.pallas_call(
        _merge_kernel,
        grid=(n // tm,),
        in_specs=[row(x2), row(ys), row(ya), row(gs), row(ga), full(wsu), full(wau), full(wo), full(g2), full(wq)],
        out_specs=[pl.BlockSpec((tm, d), lambda i: (i, 0)), pl.BlockSpec((tm, d), lambda i: (i, 0)),
                   pl.BlockSpec((tm, nq), lambda i: (i, 0))],
        out_shape=[jax.ShapeDtypeStruct((n, d), F32), jax.ShapeDtypeStruct((n, d), F32),
                   jax.ShapeDtypeStruct((n, nq), BF16)],
        compiler_params=_cparams(("arbitrary",)),
        name="merge",
    )(x2, ys, ya, gs, ga, wsu, wau, wo, g2, wq)


def _cand_layout():
    blocks = []
    blocks.append((0, 16, 16))
    for i in range(1, 8):
        blocks.append((i, 8, PEER_TOPK // (i + 1)))
    blocks.append((None, 8, 8))
    return blocks


def _top_rows(s, order, payload, k):
    big = jnp.float32(3e38)
    vals, pays = [], []
    for _ in range(k):
        m = jnp.max(s, axis=0, keepdims=True)
        o = jnp.min(jnp.where(s == m, order, big), axis=0, keepdims=True)
        hit = order == o
        pays.append(jnp.min(jnp.where(hit, payload, big), axis=0, keepdims=True))
        vals.append(m)
        s = jnp.where(hit, -jnp.inf, s)
    return jnp.concatenate(vals, axis=0), jnp.concatenate(pays, axis=0)


def _route_kernel(qp_ref, k1_ref, k2_ref, e_ref, g_ref, *, tt):
    qp = qp_ref[...]
    kd = PEER_KEY_DIM
    rows_k = lax.broadcasted_iota(I32, (PEER_KEYS, tt), 0).astype(F32)
    for h in range(PEER_HEADS):
        q1 = qp[:, (2 * h) * kd:(2 * h + 1) * kd]
        q2 = qp[:, (2 * h + 1) * kd:(2 * h + 2) * kd]
        s1 = lax.dot_general(k1_ref[h], q1, (((1,), (1,)), ((), ())), preferred_element_type=F32)
        s2 = lax.dot_general(k2_ref[h], q2, (((1,), (1,)), ((), ())), preferred_element_type=F32)
        v1, i1 = _top_rows(s1, rows_k, rows_k, PEER_TOPK)
        v2, i2 = _top_rows(s2, rows_k, rows_k, PEER_TOPK)
        cs, ce, co = [], [], []
        for i, rows, valid in _cand_layout():
            r = lax.broadcasted_iota(I32, (rows, tt), 0).astype(F32)
            if i is None:
                val = v1[8:16] + v2[0:1]
                eid = i1[8:16] * PEER_KEYS + i2[0:1]
                flat = (r + 8.0) * PEER_TOPK
            else:
                val = v1[i:i + 1] + v2[0:rows]
                eid = i1[i:i + 1] * PEER_KEYS + i2[0:rows]
                flat = r + float(i * PEER_TOPK)
                if valid < rows:
                    val = jnp.where(r < float(valid), val, -jnp.inf)
            cs.append(val)
            ce.append(eid)
            co.append(flat)
        cand = jnp.concatenate(cs, axis=0)
        top_s, top_e = _top_rows(cand, jnp.concatenate(co, axis=0), jnp.concatenate(ce, axis=0), PEER_TOPK)
        p = jnp.exp(top_s - top_s[0:1])
        gates = p / jnp.sum(p, axis=0, keepdims=True)
        e_ref[h * PEER_TOPK:(h + 1) * PEER_TOPK, :] = top_e.astype(I32)
        g_ref[h * PEER_TOPK:(h + 1) * PEER_TOPK, :] = gates


def _route(qp, peer_k1, peer_k2, tt):
    n, nq = qp.shape
    k1 = peer_k1.astype(BF16)
    k2 = peer_k2.astype(BF16)
    n_sel = PEER_HEADS * PEER_TOPK
    full = lambda a: pl.BlockSpec(a.shape, lambda i: (0,) * a.ndim)
    return pl.pallas_call(
        functools.partial(_route_kernel, tt=tt),
        grid=(n // tt,),
        in_specs=[pl.BlockSpec((tt, nq), lambda i: (i, 0)), full(k1), full(k2)],
        out_specs=[pl.BlockSpec((n_sel, tt), lambda i: (0, i)), pl.BlockSpec((n_sel, tt), lambda i: (0, i))],
        out_shape=[jax.ShapeDtypeStruct((n_sel, n), I32), jax.ShapeDtypeStruct((n_sel, n), F32)],
        compiler_params=_cparams(("arbitrary",)),
        name="route",
    )(qp, k1, k2)


def _final_kernel(h_ref, p_ref, g_ref, o_ref):
    h = h_ref[...] + p_ref[...]
    o_ref[...] = h * lax.rsqrt(jnp.mean(h * h, axis=-1, keepdims=True) + NORM_EPS) * g_ref[...]


def _final(h, p, g, tm):
    n, d = h.shape
    row = pl.BlockSpec((tm, d), lambda i: (i, 0))
    return pl.pallas_call(
        _final_kernel,
        grid=(n // tm,),
        in_specs=[row, row, pl.BlockSpec((1, d), lambda i: (0, 0))],
        out_specs=row,
        out_shape=jax.ShapeDtypeStruct((n, d), F32),
        compiler_params=_cparams(("arbitrary",)),
        name="final",
    )(h, p, g.reshape(1, d))


SC_CORES_V7X = 2
SC_SUBCORES_V7X = 16
SC_LANES_V7X = 16
PEER_TOK_BATCH = 16
PEER_ROW_CHUNK = 64


def _pack_bf16_pairs(t):
    half = t.shape[1] // 2
    tb = t.astype(BF16)
    lo = lax.bitcast_convert_type(tb[:, :half], jnp.uint16).astype(jnp.uint32)
    hi = lax.bitcast_convert_type(tb[:, half:], jnp.uint16).astype(jnp.uint32)
    return lax.bitcast_convert_type(lo | (hi << 16), I32)


def _unpack_pair(w):
    lo = lax.bitcast_convert_type(jnp.left_shift(w, 16), F32)
    hi = lax.bitcast_convert_type(w & jnp.int32(-65536), F32)
    return lo, hi


def _peer_sc_body(hn_hbm, e_hbm, g_hbm, u_hbm, v_hbm, out_hbm,
                  idx_v, gate_v, x_v, out_v, rows, p_v, act_v, sem, *, tpw, d, n_sel):
    nl = SC_LANES_V7X
    tb = PEER_TOK_BATCH
    rc = PEER_ROW_CHUNK
    n_chunk = n_sel // rc
    jobs_per_tok = 2 * n_chunk
    half = d // 2
    n_lane_blk = half // nl
    wid = lax.axis_index("s") * SC_CORES_V7X + lax.axis_index("c")
    base = wid * tpw
    lane = lax.iota(I32, nl)
    zero = jnp.zeros((nl,), F32)
    c_gelu = 2.0 * math.sqrt(2.0 / math.pi)

    def gather_copy(tab_hbm, job):
        tok = job // jobs_per_tok
        c = (job % jobs_per_tok) % n_chunk
        b = job % 2
        return pltpu.make_async_copy(tab_hbm.at[idx_v.at[tok, pl.ds(c * rc, rc)]], rows.at[b], sem.at[b])

    def start(job):
        j = job % jobs_per_tok

        @pl.when(j < n_chunk)
        def _():
            gather_copy(u_hbm, job).start()

        @pl.when(j >= n_chunk)
        def _():
            gather_copy(v_hbm, job).start()

    def compute_u(tok, c, b):
        def rg_body(rg, _):
            r0 = rg * 8

            def jbody(j, accs):
                off = j * nl
                xlo = x_v[tok, pl.ds(off, nl)]
                xhi = x_v[tok, pl.ds(half + off, nl)]
                new = []
                for r in range(8):
                    lo, hi = _unpack_pair(rows[b, r0 + r, pl.ds(off, nl)])
                    new.append(accs[r] + (lo * xlo + hi * xhi))
                return tuple(new)

            accs = lax.fori_loop(0, n_lane_blk, jbody, (zero,) * 8, unroll=2)
            for r in range(8):
                p_v[c * rc + r0 + r, :] = accs[r]
            return 0

        lax.fori_loop(0, rc // 8, rg_body, 0)

    def finish_act(tok):
        def eg_body(eg, _):
            e0 = eg * nl
            ridx = e0 + lane
            s = zero
            for l in range(nl):
                s = s + plsc.load_gather(p_v, [ridx, jnp.full((nl,), l, I32)])
            inner = c_gelu * (s + 0.044715 * (s * s * s))
            gl = s / (1.0 + jnp.exp(-inner))
            act_v[pl.ds(e0, nl)] = gl * gate_v[tok, pl.ds(e0, nl)]
            return 0

        lax.fori_loop(0, n_sel // nl, eg_body, 0)

        def zbody(j, _):
            out_v[tok, pl.ds(j * nl, nl)] = zero
            return 0

        lax.fori_loop(0, d // nl, zbody, 0, unroll=4)

    def compute_v(tok, c, b):
        def rg_body(rg, _):
            r0 = rg * nl
            splat = [plsc.load_gather(act_v, [jnp.full((nl,), 0, I32) + (c * rc + r0 + r)]) for r in range(nl)]

            def tree(parts):
                while len(parts) > 1:
                    parts = [parts[i] + parts[i + 1] for i in range(0, len(parts), 2)]
                return parts[0]

            def jbody(j, _):
                off = j * nl
                los, his = [], []
                for r in range(nl):
                    lo, hi = _unpack_pair(rows[b, r0 + r, pl.ds(off, nl)])
                    los.append(splat[r] * lo)
                    his.append(splat[r] * hi)
                out_v[tok, pl.ds(off, nl)] = out_v[tok, pl.ds(off, nl)] + tree(los)
                out_v[tok, pl.ds(half + off, nl)] = out_v[tok, pl.ds(half + off, nl)] + tree(his)
                return 0

            lax.fori_loop(0, n_lane_blk, jbody, 0)
            return 0

        lax.fori_loop(0, rc // nl, rg_body, 0)

    def batch_body(bi, _):
        t0 = base + bi * tb
        pltpu.sync_copy(e_hbm.at[pl.ds(t0, tb)], idx_v)
        pltpu.sync_copy(g_hbm.at[pl.ds(t0, tb)], gate_v)
        pltpu.sync_copy(hn_hbm.at[pl.ds(t0, tb)], x_v)
        start(0)

        def job_body(job, _):
            @pl.when(job + 1 < tb * jobs_per_tok)
            def _():
                start(job + 1)

            j = job % jobs_per_tok
            gather_copy(u_hbm, job).wait()
            tok = job // jobs_per_tok
            b = job % 2

            @pl.when(j < n_chunk)
            def _():
                compute_u(tok, j, b)

            @pl.when(j == n_chunk - 1)
            def _():
                finish_act(tok)

            @pl.when(j >= n_chunk)
            def _():
                compute_v(tok, j - n_chunk, b)

            return 0

        lax.fori_loop(0, tb * jobs_per_tok, job_body, 0)
        pltpu.sync_copy(out_v, out_hbm.at[pl.ds(t0, tb)])
        return 0

    lax.fori_loop(0, tpw // tb, batch_body, 0)


def _peer_sc(hn, experts, gates, u_tab, v_tab):
    n, d = hn.shape
    n_sel = experts.shape[1]
    nw = SC_CORES_V7X * SC_SUBCORES_V7X
    tpw = n // nw
    mesh = plsc.VectorSubcoreMesh(core_axis_name="c", subcore_axis_name="s",
                                  num_cores=SC_CORES_V7X, num_subcores=SC_SUBCORES_V7X)
    body = functools.partial(_peer_sc_body, tpw=tpw, d=d, n_sel=n_sel)
    call = pl.kernel(
        body,
        out_type=jax.ShapeDtypeStruct((n, d), F32),
        mesh=mesh,
        scratch_types=[pltpu.VMEM((PEER_TOK_BATCH, n_sel), I32),
                       pltpu.VMEM((PEER_TOK_BATCH, n_sel), F32),
                       pltpu.VMEM((PEER_TOK_BATCH, d), F32),
                       pltpu.VMEM((PEER_TOK_BATCH, d), F32),
                       pltpu.VMEM((2, PEER_ROW_CHUNK, d // 2), I32),
                       pltpu.VMEM((n_sel, SC_LANES_V7X), F32),
                       pltpu.VMEM((n_sel,), F32),
                       pltpu.SemaphoreType.DMA((2,))],
        compiler_params=pltpu.CompilerParams(needs_layout_passes=False),
        name="peer_sc",
    )
    return call(hn, experts, gates, _pack_bf16_pairs(u_tab), _pack_bf16_pairs(v_tab))


def kernel(x, norm1_g, w_in, a_re, a_im, log_dt, b_re, b_im, c_re, c_im, d_skip, w_glu, w_ssm_up, w_attn_up,
           w_out, norm2_g, peer_wq, peer_k1, peer_k2, peer_u, peer_v, norm_f_g):
    bsz, seq, d = x.shape
    n = bsz * seq
    h = x
    depth = norm1_g.shape[0]
    for layer in range(depth):
        u, q, k, v, qi, ki, wi, gs, ga = _inproj(h, norm1_g[layer], w_in[layer], tm=min(512, seq))
        d_ssm = u.shape[-1]
        u_tb = u.transpose(1, 0, 2).reshape(n, d_ssm)
        y_tb = _s5(u_tb, a_re[layer], a_im[layer], log_dt[layer], b_re[layer], b_im[layer], c_re[layer],
                   c_im[layer], d_skip[layer], w_glu[layer], nb=bsz, tc=64)
        ys = y_tb.reshape(seq, bsz, d_ssm).transpose(1, 0, 2).reshape(n, d_ssm)
        ya = _dsa(q, k, v, qi, ki, wi, tq=128, kt=256).reshape(n, -1)
        hm, hn, qp = _merge(h.reshape(n, d), ys, ya, gs.reshape(n, d), ga.reshape(n, d), w_ssm_up[layer],
                            w_attn_up[layer], w_out[layer], norm2_g[layer], peer_wq[layer], tm=512)
        e_t, g_t = _route(qp, peer_k1[layer], peer_k2[layer], tt=256)
        po = _peer_sc(hn, e_t.T, g_t.T, peer_u[layer], peer_v[layer])
        if layer + 1 < depth:
            h = (hm + po).reshape(bsz, seq, d)
    return _final(hm, po, norm_f_g, tm=512).reshape(bsz, seq, d)
```

```python
import functools
import math

import numpy as np
import jax
import jax.numpy as jnp
from jax import lax
from jax.experimental import pallas as pl
from jax.experimental.pallas import tpu as pltpu
from jax.experimental.pallas import tpu_sc as plsc

F32 = jnp.float32
BF16 = jnp.bfloat16
I32 = jnp.int32

SSM_GROUP = 16
SSM_STATE = 64
ATTN_HEADS = 8
ATTN_KV_HEADS = 2
HEAD_DIM = 64
IDX_HEADS = 8
IDX_DIM = 32
TOPK_MAX = 256
ROPE_THETA = 10000.0
NEG_BIG = -1e30
PEER_HEADS = 8
PEER_KEYS = 128
PEER_KEY_DIM = 128
PEER_TOPK = 16
NORM_EPS = 1e-6

BATCH_CHUNKS = 4
INT_MIN = -(2 ** 31)
VMEM_LIMIT = 56 * 1024 * 1024


def _cparams(sem):
    return pltpu.CompilerParams(dimension_semantics=sem, vmem_limit_bytes=VMEM_LIMIT)


def _gelu_tanh(x):
    return 0.5 * x * (1.0 + jnp.tanh(math.sqrt(2.0 / math.pi) * (x + 0.044715 * (x * x * x))))


def _sigmoid(x):
    return 1.0 / (1.0 + jnp.exp(-x))


def _rot_cols(w, hd):
    d, n = w.shape
    w3 = w.reshape(d, n // hd, hd)
    half = hd // 2
    return jnp.concatenate([-w3[..., half:], w3[..., :half]], axis=-1).reshape(d, n)


def _rope_full(seq, hd, heads):
    pos = jnp.arange(seq, dtype=F32)
    inv = ROPE_THETA ** (-jnp.arange(0, hd, 2, dtype=F32) / hd)
    ang = pos[:, None] * inv[None, :]
    c = jnp.concatenate([jnp.cos(ang), jnp.cos(ang)], axis=-1)
    s = jnp.concatenate([jnp.sin(ang), jnp.sin(ang)], axis=-1)
    return jnp.tile(c, (1, heads)), jnp.tile(s, (1, heads))


def _inproj_kernel(x_ref, g_ref, w_ref, wr_ref, cs_ref, sn_ref,
                   u_ref, q_ref, k_ref, v_ref, qi_ref, ki_ref, wi_ref, gs_ref, ga_ref,
                   *, d_ssm, d_q, d_kv, d_qi, d_ki, n_wi, d_model, q_scale, wi_scale):
    x = x_ref[0]
    xn = x * lax.rsqrt(jnp.mean(x * x, axis=-1, keepdims=True) + NORM_EPS) * g_ref[...]
    xb = xn.astype(BF16)

    def mm(ref, lo, n):
        return jnp.dot(xb, ref[:, lo:lo + n], preferred_element_type=F32)

    o = 0
    u_ref[0] = mm(w_ref, o, d_ssm).astype(BF16)
    o += d_ssm
    ro = 0
    q = mm(w_ref, o, d_q) * cs_ref[:, ro:ro + d_q] + mm(wr_ref, ro, d_q) * sn_ref[:, ro:ro + d_q]
    q_ref[0] = (q * q_scale).astype(BF16)
    o += d_q
    ro += d_q
    k = mm(w_ref, o, d_kv) * cs_ref[:, ro:ro + d_kv] + mm(wr_ref, ro, d_kv) * sn_ref[:, ro:ro + d_kv]
    k_ref[0] = k.astype(BF16)
    o += d_kv
    ro += d_kv
    v_ref[0] = mm(w_ref, o, d_kv).astype(BF16)
    o += d_kv
    qi = mm(w_ref, o, d_qi) * cs_ref[:, ro:ro + d_qi] + mm(wr_ref, ro, d_qi) * sn_ref[:, ro:ro + d_qi]
    qi_ref[0] = qi.astype(BF16)
    o += d_qi
    ro += d_qi
    seg = mm(w_ref, o, 128)
    segr = mm(wr_ref, ro, 128)
    kiw = seg * cs_ref[:, ro:ro + 128] + segr * sn_ref[:, ro:ro + 128]
    ki_ref[0] = kiw[:, :d_ki].astype(BF16)
    wi_ref[0] = seg[:, d_ki:d_ki + n_wi] * wi_scale
    o += 128
    gs_ref[0] = _sigmoid(mm(w_ref, o, d_model)).astype(BF16)
    o += d_model
    ga_ref[0] = _sigmoid(mm(w_ref, o, d_model)).astype(BF16)


def _inproj(x, norm_g, w_in, tm):
    bsz, seq, d = x.shape
    d_ssm = d // 2
    d_q = ATTN_HEADS * HEAD_DIM
    d_kv = ATTN_KV_HEADS * HEAD_DIM
    d_qi = IDX_HEADS * IDX_DIM
    d_ki = IDX_DIM
    n_wi = IDX_HEADS
    splits = (d_ssm, d_q, d_kv, d_kv, d_qi, d_ki, n_wi, d, d)
    offs = np.cumsum(splits)[:-1].tolist()
    wu, wq, wk, wv, wqi, wki, wwi, wgs, wga = jnp.split(w_in, offs, axis=1)
    pad = jnp.zeros((d, 128 - d_ki - n_wi), F32)
    w_main = jnp.concatenate([wu, wq, wk, wv, wqi, wki, wwi, pad, wgs, wga], axis=1).astype(BF16)
    rpad = jnp.zeros((d, 128 - d_ki), F32)
    w_rot = jnp.concatenate([_rot_cols(wq, HEAD_DIM), _rot_cols(wk, HEAD_DIM),
                             _rot_cols(wqi, IDX_DIM), _rot_cols(wki, IDX_DIM), rpad], axis=1).astype(BF16)
    cq, sq = _rope_full(seq, HEAD_DIM, ATTN_HEADS)
    ck, sk = _rope_full(seq, HEAD_DIM, ATTN_KV_HEADS)
    cqi, sqi = _rope_full(seq, IDX_DIM, IDX_HEADS)
    cki, ski = _rope_full(seq, IDX_DIM, 1)
    tpad = jnp.zeros((seq, 128 - d_ki), F32)
    cs = jnp.concatenate([cq, ck, cqi, cki, tpad], axis=1)
    sn = jnp.concatenate([sq, sk, sqi, ski, tpad], axis=1)
    n_main = w_main.shape[1]
    n_rot = w_rot.shape[1]

    kern = functools.partial(
        _inproj_kernel, d_ssm=d_ssm, d_q=d_q, d_kv=d_kv, d_qi=d_qi, d_ki=d_ki, n_wi=n_wi, d_model=d,
        q_scale=HEAD_DIM ** -0.5, wi_scale=(IDX_HEADS ** -0.5) * (IDX_DIM ** -0.5))
    tok = lambda n: pl.BlockSpec((1, tm, n), lambda s, b: (b, s, 0))
    full = lambda shape: pl.BlockSpec(shape, lambda s, b: (0,) * len(shape))
    outs = [(d_ssm, BF16), (d_q, BF16), (d_kv, BF16), (d_kv, BF16), (d_qi, BF16), (d_ki, BF16),
            (n_wi, F32), (d, BF16), (d, BF16)]
    return pl.pallas_call(
        kern,
        grid=(seq // tm, bsz),
        in_specs=[tok(d), full((1, d)), full((d, n_main)), full((d, n_rot)),
                  pl.BlockSpec((tm, n_rot), lambda s, b: (s, 0)),
                  pl.BlockSpec((tm, n_rot), lambda s, b: (s, 0))],
        out_specs=[tok(n) for n, _ in outs],
        out_shape=[jax.ShapeDtypeStruct((bsz, seq, n), dt) for n, dt in outs],
        compiler_params=_cparams(("arbitrary", "arbitrary")),
        name="inproj",
    )(x, norm_g.reshape(1, d), w_main, w_rot, cs, sn)


def _s5_kernel(u_ref, bre_ref, bim_ref, cre_ref, cim_ref, are_ref, aim_ref, dsk_ref, wglu_ref,
               y_ref, sre, sim, st_re, st_im, *, tc, nb, lane_chunk):
    @pl.when(pl.program_id(0) == 0)
    def _():
        st_re[...] = jnp.zeros_like(st_re)
        st_im[...] = jnp.zeros_like(st_im)

    u = u_ref[...]
    n_half = bre_ref.shape[0]
    hin = bre_ref.shape[1]
    hst = bre_ref.shape[2]
    for h in range(n_half):
        uh = u[:, h * hin:(h + 1) * hin]
        sre[:, h * hst:(h + 1) * hst] = jnp.dot(uh, bre_ref[h], preferred_element_type=F32)
        sim[:, h * hst:(h + 1) * hst] = jnp.dot(uh, bim_ref[h], preferred_element_type=F32)

    n_state = sre.shape[1]
    for c in range(n_state // lane_chunk):
        cols = slice(c * lane_chunk, (c + 1) * lane_chunk)
        ar = are_ref[:, cols]
        ai = aim_ref[:, cols]

        def step(t, carry, cols=cols, ar=ar, ai=ai):
            sr, si = carry
            r0 = pl.multiple_of(t * nb, nb)
            nr = ar * sr - ai * si + sre[pl.ds(r0, nb), cols]
            ni = ar * si + ai * sr + sim[pl.ds(r0, nb), cols]
            sre[pl.ds(r0, nb), cols] = nr
            sim[pl.ds(r0, nb), cols] = ni
            return nr, ni

        sr, si = lax.fori_loop(0, tc, step, (st_re[:, cols], st_im[:, cols]), unroll=4)
        st_re[:, cols] = sr
        st_im[:, cols] = si

    ys = []
    for h in range(n_half):
        srh = sre[:, h * hst:(h + 1) * hst].astype(BF16)
        sih = sim[:, h * hst:(h + 1) * hst].astype(BF16)
        ys.append(jnp.dot(srh, cre_ref[h], preferred_element_type=F32)
                  - jnp.dot(sih, cim_ref[h], preferred_element_type=F32))
    y = jnp.concatenate(ys, axis=-1) + dsk_ref[...] * u.astype(F32)
    y = _gelu_tanh(y)
    gate = jnp.dot(y.astype(BF16), wglu_ref[...], preferred_element_type=F32)
    y_ref[...] = (y * _sigmoid(gate)).astype(BF16)


def _s5(u_tb, a_re, a_im, log_dt, b_re, b_im, c_re, c_im, d_skip, w_glu, nb, tc):
    rows, d_ssm = u_tb.shape
    groups = d_ssm // SSM_GROUP
    n_state = groups * SSM_STATE
    lam = lax.complex(a_re, a_im)
    dt = jnp.exp(log_dt)[:, None]
    a_bar = jnp.exp(lam * dt)
    b_bar = ((a_bar - 1.0) / lam)[..., None] * lax.complex(b_re, b_im)
    gh = min(groups, 256 // SSM_GROUP)
    n_half = groups // gh
    eye = jnp.eye(gh, dtype=F32)

    def bmat(bb):
        b4 = bb.reshape(n_half, gh, SSM_STATE, SSM_GROUP)
        return jnp.einsum('hgpc,gk->hgckp', b4, eye).reshape(n_half, gh * SSM_GROUP, gh * SSM_STATE)

    def cmat(cc):
        c4 = cc.reshape(n_half, gh, SSM_GROUP, SSM_STATE)
        return jnp.einsum('hgcp,gk->hgpkc', c4, eye).reshape(n_half, gh * SSM_STATE, gh * SSM_GROUP)

    bre = bmat(jnp.real(b_bar)).astype(BF16)
    bim = bmat(jnp.imag(b_bar)).astype(BF16)
    cre = cmat(c_re).astype(BF16)
    cim = cmat(c_im).astype(BF16)
    are = jnp.broadcast_to(jnp.real(a_bar).reshape(1, n_state), (nb, n_state))
    aim = jnp.broadcast_to(jnp.imag(a_bar).reshape(1, n_state), (nb, n_state))
    blk = tc * nb
    full = lambda a: pl.BlockSpec(a.shape, lambda i: (0,) * a.ndim)
    dsk = d_skip.reshape(1, d_ssm)
    wg = w_glu.astype(BF16)
    kern = functools.partial(_s5_kernel, tc=tc, nb=nb, lane_chunk=512)
    return pl.pallas_call(
        kern,
        grid=(rows // blk,),
        in_specs=[pl.BlockSpec((blk, d_ssm), lambda i: (i, 0)),
                  full(bre), full(bim), full(cre), full(cim), full(are), full(aim), full(dsk), full(wg)],
        out_specs=pl.BlockSpec((blk, d_ssm), lambda i: (i, 0)),
        out_shape=jax.ShapeDtypeStruct((rows, d_ssm), BF16),
        scratch_shapes=[pltpu.VMEM((blk, n_state), F32), pltpu.VMEM((blk, n_state), F32),
                        pltpu.VMEM((nb, n_state), F32), pltpu.VMEM((nb, n_state), F32)],
        compiler_params=_cparams(("arbitrary",)),
        name="s5",
    )(u_tb, bre, bim, cre, cim, are, aim, dsk, wg)


def _dsa_kernel(qi_ref, wit_ref, q_ref, ki_ref, k_ref, vt_ref, o_ref, key_s,
                *, tq, kt, topk, seq_bits):
    qb = pl.program_id(1)
    nkt = (qb * tq + tq + kt - 1) // kt
    q_pos = qb * tq + lax.broadcasted_iota(I32, (1, tq), 1)
    k_eff = jnp.minimum(topk, q_pos + 1).astype(F32)

    qi = qi_ref[0]
    wit = wit_ref[0]
    qih = [qi[:, h * IDX_DIM:(h + 1) * IDX_DIM] for h in range(IDX_HEADS)]

    def key_pos(t):
        return t * kt + lax.broadcasted_iota(I32, (kt, tq), 0)

    def score_tile(t, _):
        r0 = pl.multiple_of(t * kt, kt)
        ki_t = ki_ref[0, pl.ds(r0, kt), :]
        sc = jnp.zeros((kt, tq), F32)
        for h in range(IDX_HEADS):
            rel = lax.dot_general(ki_t, qih[h], (((1,), (1,)), ((), ())), preferred_element_type=F32)
            sc = sc + jnp.maximum(rel, 0.0) * wit[h:h + 1, :]
        bits = lax.bitcast_convert_type(sc, I32)
        key = jnp.where(bits < 0, bits ^ jnp.int32(0x7FFFFFFF), bits)
        key = jnp.where(key_pos(t) <= q_pos, key, jnp.int32(INT_MIN))
        key_s[pl.ds(r0, kt), :] = key
        return 0

    lax.fori_loop(0, nkt, score_tile, 0)

    def count(pred_fn):
        def body(t, acc):
            r0 = pl.multiple_of(t * kt, kt)
            m = pred_fn(key_s[pl.ds(r0, kt), :], t)
            ones = jnp.where(m, 1.0, 0.0).reshape(kt // 8, 8, tq)
            return acc + jnp.sum(ones, axis=0)
        acc = lax.fori_loop(0, nkt, body, jnp.zeros((8, tq), F32))
        return jnp.sum(acc, axis=0, keepdims=True)

    def bit_step(i, u):
        bit = jnp.left_shift(jnp.int32(1), 31 - i)
        cand_u = u | bit
        cand_s = cand_u ^ jnp.int32(INT_MIN)
        cnt = count(lambda kk, t: kk >= cand_s)
        return jnp.where(cnt >= k_eff, cand_u, u)

    u_thr = lax.fori_loop(0, 32, bit_step, jnp.zeros((1, tq), I32))
    thr = u_thr ^ jnp.int32(INT_MIN)

    cnt_ge = count(lambda kk, t: kk >= thr)
    cnt_gt = count(lambda kk, t: kk > thr)
    need_eq = k_eff - cnt_gt
    has_tie = jnp.max(cnt_ge - k_eff) > 0.0

    def tie_cut():
        def pos_step(i, c):
            bit = jnp.left_shift(jnp.int32(1), seq_bits - 1 - i)
            cand = c | bit
            cnt = count(lambda kk, t: (kk == thr) & (key_pos(t) < cand))
            return jnp.where(cnt < need_eq, cand, c)
        return lax.fori_loop(0, seq_bits, pos_step, jnp.zeros((1, tq), I32))

    cut = lax.cond(has_tie, tie_cut, lambda: jnp.full((1, tq), 2 ** seq_bits, I32))

    q = q_ref[0]
    grp = ATTN_HEADS // ATTN_KV_HEADS
    for n in range(ATTN_KV_HEADS):
        qn = jnp.concatenate([q[:, (n * grp + g) * HEAD_DIM:(n * grp + g + 1) * HEAD_DIM]
                              for g in range(grp)], axis=0)

        def attn_tile(t, carry, n=n, qn=qn):
            m, l, acc = carry
            r0 = pl.multiple_of(t * kt, kt)
            key = key_s[pl.ds(r0, kt), :]
            sel = (key > thr) | ((key == thr) & (key_pos(t) <= cut))
            k_t = k_ref[0, n, pl.ds(r0, kt), :]
            lg = lax.dot_general(k_t, qn, (((1,), (1,)), ((), ())), preferred_element_type=F32)
            sel4 = jnp.concatenate([sel] * grp, axis=1)
            lg = jnp.where(sel4, lg, NEG_BIG)
            m_new = jnp.maximum(m, jnp.max(lg, axis=0, keepdims=True))
            p = jnp.where(sel4, jnp.exp(lg - m_new), 0.0)
            alpha = jnp.exp(m - m_new)
            l = alpha * l + jnp.sum(p, axis=0, keepdims=True)
            v_t = vt_ref[0, n, t]
            acc = alpha * acc + jnp.dot(v_t, p.astype(BF16), preferred_element_type=F32)
            return m_new, l, acc

        init = (jnp.full((1, grp * tq), NEG_BIG, F32), jnp.zeros((1, grp * tq), F32),
                jnp.zeros((HEAD_DIM, grp * tq), F32))
        m, l, acc = lax.fori_loop(0, nkt, attn_tile, init)
        o_ref[0, 0, n] = (acc / l).astype(BF16)


def _dsa_prep(k, v, wi, kt):
    bsz, seq, _ = k.shape
    wit = wi.transpose(0, 2, 1)
    k4 = k.reshape(bsz, seq, ATTN_KV_HEADS, HEAD_DIM).transpose(0, 2, 1, 3)
    vt = v.reshape(bsz, seq // kt, kt, ATTN_KV_HEADS, HEAD_DIM).transpose(0, 3, 1, 4, 2)
    return wit, k4, vt


def _dsa(q, qi, ki, wit, k4, vt, b0, nb, tq, kt):
    _, seq, _ = q.shape
    topk = min(TOPK_MAX, seq // 4)
    nqb = seq // tq
    grp = ATTN_HEADS // ATTN_KV_HEADS
    seq_bits = int(math.log2(seq))
    assert 2 ** seq_bits == seq
    kern = functools.partial(_dsa_kernel, tq=tq, kt=kt, topk=topk, seq_bits=seq_bits)
    o_t = pl.pallas_call(
        kern,
        grid=(nb, nqb),
        in_specs=[pl.BlockSpec((1, tq, IDX_HEADS * IDX_DIM), lambda b, j: (b + b0, j, 0)),
                  pl.BlockSpec((1, IDX_HEADS, tq), lambda b, j: (b + b0, 0, j)),
                  pl.BlockSpec((1, tq, ATTN_HEADS * HEAD_DIM), lambda b, j: (b + b0, j, 0)),
                  pl.BlockSpec((1, seq, IDX_DIM), lambda b, j: (b + b0, 0, 0)),
                  pl.BlockSpec((1, ATTN_KV_HEADS, seq, HEAD_DIM), lambda b, j: (b + b0, 0, 0, 0)),
                  pl.BlockSpec((1, ATTN_KV_HEADS, seq // kt, HEAD_DIM, kt), lambda b, j: (b + b0, 0, 0, 0, 0))],
        out_specs=pl.BlockSpec((1, 1, ATTN_KV_HEADS, HEAD_DIM, grp * tq), lambda b, j: (b, j, 0, 0, 0)),
        out_shape=jax.ShapeDtypeStruct((nb, nqb, ATTN_KV_HEADS, HEAD_DIM, grp * tq), BF16),
        scratch_shapes=[pltpu.VMEM((seq, tq), I32)],
        compiler_params=_cparams(("arbitrary", "arbitrary")),
        name="dsa",
    )(qi, wit, q, ki, k4, vt)
    o = o_t.reshape(nb, nqb, ATTN_KV_HEADS, HEAD_DIM, grp, tq).transpose(0, 1, 5, 2, 4, 3)
    return o.reshape(nb * seq, ATTN_HEADS * HEAD_DIM)


def _merge_kernel(x_ref, ys_ref, ya_ref, gs_ref, ga_ref, wsu_ref, wau_ref, wout_ref, g2_ref, wq_ref,
                  h_ref, hn_ref, qp_ref):
    ms = jnp.dot(ys_ref[...], wsu_ref[...], preferred_element_type=F32)
    ma = jnp.dot(ya_ref[...], wau_ref[...], preferred_element_type=F32)
    merged = gs_ref[...].astype(F32) * ms + ga_ref[...].astype(F32) * ma
    h = x_ref[...] + jnp.dot(merged.astype(BF16), wout_ref[...], preferred_element_type=F32)
    h_ref[...] = h
    hn = h * lax.rsqrt(jnp.mean(h * h, axis=-1, keepdims=True) + NORM_EPS) * g2_ref[...]
    hn_ref[...] = hn
    qp_ref[...] = jnp.dot(hn.astype(BF16), wq_ref[...], preferred_element_type=F32).astype(BF16)


def _merge(x2, ys, ya, gs, ga, wsu, wau, wo, norm2_g, wq, r0, tm):
    n = ya.shape[0]
    d = x2.shape[1]
    i0 = r0 // tm
    row = lambda a: pl.BlockSpec((tm, a.shape[1]), lambda i: (i + i0, 0))
    loc = lambda a: pl.BlockSpec((tm, a.shape[1]), lambda i: (i, 0))
    full = lambda a: pl.BlockSpec(a.shape, lambda i: (0,) * a.ndim)
    g2 = norm2_g.reshape(1, d)
    nq = wq.shape[1]
    return pl.pallas_call(
        _merge_kernel,
        grid=(n // tm,),
        in_specs=[row(x2), row(ys), loc(ya), row(gs), row(ga), full(wsu), full(wau), full(wo), full(g2), full(wq)],
        out_specs=[pl.BlockSpec((tm, d), lambda i: (i, 0)), pl.BlockSpec((tm, d), lambda i: (i, 0)),
                   pl.BlockSpec((tm, nq), lambda i: (i, 0))],
        out_shape=[jax.ShapeDtypeStruct((n, d), F32), jax.ShapeDtypeStruct((n, d), F32),
                   jax.ShapeDtypeStruct((n, nq), BF16)],
        compiler_params=_cparams(("arbitrary",)),
        name="merge",
    )(x2, ys, ya, gs, ga, wsu, wau, wo, g2, wq)


def _cand_layout():
    blocks = []
    blocks.append((0, 16, 16))
    for i in range(1, 8):
        blocks.append((i, 8, PEER_TOPK // (i + 1)))
    blocks.append((None, 8, 8))
    return blocks


def _top_rows(s, order, payload, k):
    big = jnp.float32(3e38)
    vals, pays = [], []
    for _ in range(k):
        m = jnp.max(s, axis=0, keepdims=True)
        o = jnp.min(jnp.where(s == m, order, big), axis=0, keepdims=True)
        hit = order == o
        pays.append(jnp.min(jnp.where(hit, payload, big), axis=0, keepdims=True))
        vals.append(m)
        s = jnp.where(hit, -jnp.inf, s)
    return jnp.concatenate(vals, axis=0), jnp.concatenate(pays, axis=0)


def _route_kernel(qp_ref, k1_ref, k2_ref, e_ref, g_ref, *, tt):
    qp = qp_ref[...]
    kd = PEER_KEY_DIM
    rows_k = lax.broadcasted_iota(I32, (PEER_KEYS, tt), 0).astype(F32)
    for h in range(PEER_HEADS):
        q1 = qp[:, (2 * h) * kd:(2 * h + 1) * kd]
        q2 = qp[:, (2 * h + 1) * kd:(2 * h + 2) * kd]
        s1 = lax.dot_general(k1_ref[h], q1, (((1,), (1,)), ((), ())), preferred_element_type=F32)
        s2 = lax.dot_general(k2_ref[h], q2, (((1,), (1,)), ((), ())), preferred_element_type=F32)
        v1, i1 = _top_rows(s1, rows_k, rows_k, PEER_TOPK)
        v2, i2 = _top_rows(s2, rows_k, rows_k, PEER_TOPK)
        cs, ce, co = [], [], []
        for i, rows, valid in _cand_layout():
            r = lax.broadcasted_iota(I32, (rows, tt), 0).astype(F32)
            if i is None:
                val = v1[8:16] + v2[0:1]
                eid = i1[8:16] * PEER_KEYS + i2[0:1]
                flat = (r + 8.0) * PEER_TOPK
            else:
                val = v1[i:i + 1] + v2[0:rows]
                eid = i1[i:i + 1] * PEER_KEYS + i2[0:rows]
                flat = r + float(i * PEER_TOPK)
                if valid < rows:
                    val = jnp.where(r < float(valid), val, -jnp.inf)
            cs.append(val)
            ce.append(eid)
            co.append(flat)
        cand = jnp.concatenate(cs, axis=0)
        top_s, top_e = _top_rows(cand, jnp.concatenate(co, axis=0), jnp.concatenate(ce, axis=0), PEER_TOPK)
        p = jnp.exp(top_s - top_s[0:1])
        gates = p / jnp.sum(p, axis=0, keepdims=True)
        e_ref[h * PEER_TOPK:(h + 1) * PEER_TOPK, :] = top_e.astype(I32)
        g_ref[h * PEER_TOPK:(h + 1) * PEER_TOPK, :] = gates


def _route(qp, peer_k1, peer_k2, tt):
    n, nq = qp.shape
    k1 = peer_k1.astype(BF16)
    k2 = peer_k2.astype(BF16)
    n_sel = PEER_HEADS * PEER_TOPK
    full = lambda a: pl.BlockSpec(a.shape, lambda i: (0,) * a.ndim)
    return pl.pallas_call(
        functools.partial(_route_kernel, tt=tt),
        grid=(n // tt,),
        in_specs=[pl.BlockSpec((tt, nq), lambda i: (i, 0)), full(k1), full(k2)],
        out_specs=[pl.BlockSpec((n_sel, tt), lambda i: (0, i)), pl.BlockSpec((n_sel, tt), lambda i: (0, i))],
        out_shape=[jax.ShapeDtypeStruct((n_sel, n), I32), jax.ShapeDtypeStruct((n_sel, n), F32)],
        compiler_params=_cparams(("arbitrary",)),
        name="route",
    )(qp, k1, k2)


def _final_kernel(h_ref, p_ref, g_ref, o_ref):
    h = h_ref[...] + p_ref[...]
    o_ref[...] = h * lax.rsqrt(jnp.mean(h * h, axis=-1, keepdims=True) + NORM_EPS) * g_ref[...]


def _final(h, p, g, tm):
    n, d = h.shape
    row = pl.BlockSpec((tm, d), lambda i: (i, 0))
    return pl.pallas_call(
        _final_kernel,
        grid=(n // tm,),
        in_specs=[row, row, pl.BlockSpec((1, d), lambda i: (0, 0))],
        out_specs=row,
        out_shape=jax.ShapeDtypeStruct((n, d), F32),
        compiler_params=_cparams(("arbitrary",)),
        name="final",
    )(h, p, g.reshape(1, d))


SC_CORES_V7X = 2
SC_SUBCORES_V7X = 16
SC_LANES_V7X = 16
PEER_TOK_BATCH = 16
PEER_ROW_CHUNK = 64


def _pack_bf16_pairs(t):
    half = t.shape[1] // 2
    tb = t.astype(BF16)
    lo = lax.bitcast_convert_type(tb[:, :half], jnp.uint16).astype(jnp.uint32)
    hi = lax.bitcast_convert_type(tb[:, half:], jnp.uint16).astype(jnp.uint32)
    return lax.bitcast_convert_type(lo | (hi << 16), I32)


def _unpack_pair(w):
    lo = lax.bitcast_convert_type(jnp.left_shift(w, 16), F32)
    hi = lax.bitcast_convert_type(w & jnp.int32(-65536), F32)
    return lo, hi


def _peer_sc_body(hn_hbm, e_hbm, g_hbm, u_hbm, v_hbm, out_hbm,
                  idx_v, gate_v, x_v, out_v, rows, p_v, act_v, sem, *, tpw, d, n_sel):
    nl = SC_LANES_V7X
    tb = PEER_TOK_BATCH
    rc = PEER_ROW_CHUNK
    n_chunk = n_sel // rc
    jobs_per_tok = 2 * n_chunk
    half = d // 2
    n_lane_blk = half // nl
    wid = lax.axis_index("s") * SC_CORES_V7X + lax.axis_index("c")
    base = wid * tpw
    lane = lax.iota(I32, nl)
    zero = jnp.zeros((nl,), F32)
    c_gelu = 2.0 * math.sqrt(2.0 / math.pi)

    def gather_copy(tab_hbm, job):
        tok = job // jobs_per_tok
        c = (job % jobs_per_tok) % n_chunk
        b = job % 2
        return pltpu.make_async_copy(tab_hbm.at[idx_v.at[tok, pl.ds(c * rc, rc)]], rows.at[b], sem.at[b])

    def start(job):
        j = job % jobs_per_tok

        @pl.when(j < n_chunk)
        def _():
            gather_copy(u_hbm, job).start()

        @pl.when(j >= n_chunk)
        def _():
            gather_copy(v_hbm, job).start()

    def compute_u(tok, c, b):
        def rg_body(rg, _):
            r0 = rg * 8

            def jbody(j, accs):
                off = j * nl
                xlo = x_v[tok, pl.ds(off, nl)]
                xhi = x_v[tok, pl.ds(half + off, nl)]
                new = []
                for r in range(8):
                    lo, hi = _unpack_pair(rows[b, r0 + r, pl.ds(off, nl)])
                    new.append(accs[r] + (lo * xlo + hi * xhi))
                return tuple(new)

            accs = lax.fori_loop(0, n_lane_blk, jbody, (zero,) * 8, unroll=2)
            for r in range(8):
                p_v[c * rc + r0 + r, :] = accs[r]
            return 0

        lax.fori_loop(0, rc // 8, rg_body, 0)

    def finish_act(tok):
        def eg_body(eg, _):
            e0 = eg * nl
            ridx = e0 + lane
            s = zero
            for l in range(nl):
                s = s + plsc.load_gather(p_v, [ridx, jnp.full((nl,), l, I32)])
            inner = c_gelu * (s + 0.044715 * (s * s * s))
            gl = s / (1.0 + jnp.exp(-inner))
            act_v[pl.ds(e0, nl)] = gl * gate_v[tok, pl.ds(e0, nl)]
            return 0

        lax.fori_loop(0, n_sel // nl, eg_body, 0)

        def zbody(j, _):
            out_v[tok, pl.ds(j * nl, nl)] = zero
            return 0

        lax.fori_loop(0, d // nl, zbody, 0, unroll=4)

    def compute_v(tok, c, b):
        def rg_body(rg, _):
            r0 = rg * nl
            splat = [plsc.load_gather(act_v, [jnp.full((nl,), 0, I32) + (c * rc + r0 + r)]) for r in range(nl)]

            def tree(parts):
                while len(parts) > 1:
                    parts = [parts[i] + parts[i + 1] for i in range(0, len(parts), 2)]
                return parts[0]

            def jbody(j, _):
                off = j * nl
                los, his = [], []
                for r in range(nl):
                    lo, hi = _unpack_pair(rows[b, r0 + r, pl.ds(off, nl)])
                    los.append(splat[r] * lo)
                    his.append(splat[r] * hi)
                out_v[tok, pl.ds(off, nl)] = out_v[tok, pl.ds(off, nl)] + tree(los)
                out_v[tok, pl.ds(half + off, nl)] = out_v[tok, pl.ds(half + off, nl)] + tree(his)
                return 0

            lax.fori_loop(0, n_lane_blk, jbody, 0)
            return 0

        lax.fori_loop(0, rc // nl, rg_body, 0)

    def batch_body(bi, _):
        t0 = base + bi * tb
        pltpu.sync_copy(e_hbm.at[pl.ds(t0, tb)], idx_v)
        pltpu.sync_copy(g_hbm.at[pl.ds(t0, tb)], gate_v)
        pltpu.sync_copy(hn_hbm.at[pl.ds(t0, tb)], x_v)
        start(0)

        def job_body(job, _):
            @pl.when(job + 1 < tb * jobs_per_tok)
            def _():
                start(job + 1)

            j = job % jobs_per_tok
            gather_copy(u_hbm, job).wait()
            tok = job // jobs_per_tok
            b = job % 2

            @pl.when(j < n_chunk)
            def _():
                compute_u(tok, j, b)

            @pl.when(j == n_chunk - 1)
            def _():
                finish_act(tok)

            @pl.when(j >= n_chunk)
            def _():
                compute_v(tok, j - n_chunk, b)

            return 0

        lax.fori_loop(0, tb * jobs_per_tok, job_body, 0)
        pltpu.sync_copy(out_v, out_hbm.at[pl.ds(t0, tb)])
        return 0

    lax.fori_loop(0, tpw // tb, batch_body, 0)


def _peer_sc(hn, experts, gates, u_tab, v_tab):
    n, d = hn.shape
    n_sel = experts.shape[1]
    nw = SC_CORES_V7X * SC_SUBCORES_V7X
    tpw = n // nw
    mesh = plsc.VectorSubcoreMesh(core_axis_name="c", subcore_axis_name="s",
                                  num_cores=SC_CORES_V7X, num_subcores=SC_SUBCORES_V7X)
    body = functools.partial(_peer_sc_body, tpw=tpw, d=d, n_sel=n_sel)
    call = pl.kernel(
        body,
        out_type=jax.ShapeDtypeStruct((n, d), F32),
        mesh=mesh,
        scratch_types=[pltpu.VMEM((PEER_TOK_BATCH, n_sel), I32),
                       pltpu.VMEM((PEER_TOK_BATCH, n_sel), F32),
                       pltpu.VMEM((PEER_TOK_BATCH, d), F32),
                       pltpu.VMEM((PEER_TOK_BATCH, d), F32),
                       pltpu.VMEM((2, PEER_ROW_CHUNK, d // 2), I32),
                       pltpu.VMEM((n_sel, SC_LANES_V7X), F32),
                       pltpu.VMEM((n_sel,), F32),
                       pltpu.SemaphoreType.DMA((2,))],
        compiler_params=pltpu.CompilerParams(needs_layout_passes=False),
        name="peer_sc",
    )
    return call(hn, experts, gates, u_tab, v_tab)


def kernel(x, norm1_g, w_in, a_re, a_im, log_dt, b_re, b_im, c_re, c_im, d_skip, w_glu, w_ssm_up, w_attn_up,
           w_out, norm2_g, peer_wq, peer_k1, peer_k2, peer_u, peer_v, norm_f_g):
    bsz, seq, d = x.shape
    n = bsz * seq
    h = x
    depth = norm1_g.shape[0]
    n_chunks = BATCH_CHUNKS if bsz % BATCH_CHUNKS == 0 else 1
    nb = bsz // n_chunks
    for layer in range(depth):
        last = layer + 1 == depth
        u, q, k, v, qi, ki, wi, gs, ga = _inproj(h, norm1_g[layer], w_in[layer], tm=min(512, seq))
        d_ssm = u.shape[-1]
        u_tb = u.transpose(1, 0, 2).reshape(n, d_ssm)
        y_tb = _s5(u_tb, a_re[layer], a_im[layer], log_dt[layer], b_re[layer], b_im[layer], c_re[layer],
                   c_im[layer], d_skip[layer], w_glu[layer], nb=bsz, tc=64)
        ys = y_tb.reshape(seq, bsz, d_ssm).transpose(1, 0, 2).reshape(n, d_ssm)
        wit, k4, vt = _dsa_prep(k, v, wi, kt=256)
        wsu = w_ssm_up[layer].astype(BF16)
        wau = w_attn_up[layer].astype(BF16)
        wo = w_out[layer].astype(BF16)
        wq = peer_wq[layer].astype(BF16)
        u_pk = _pack_bf16_pairs(peer_u[layer])
        v_pk = _pack_bf16_pairs(peer_v[layer])
        h2 = h.reshape(n, d)
        outs = []
        for c in range(n_chunks):
            ya = _dsa(q, qi, ki, wit, k4, vt, b0=c * nb, nb=nb, tq=128, kt=256)
            hm, hn, qp = _merge(h2, ys, ya, gs.reshape(n, d), ga.reshape(n, d), wsu, wau, wo, norm2_g[layer], wq,
                                r0=c * nb * seq, tm=512)
            e_t, g_t = _route(qp, peer_k1[layer], peer_k2[layer], tt=256)
            po = _peer_sc(hn, e_t.T, g_t.T, u_pk, v_pk)
            outs.append(_final(hm, po, norm_f_g, tm=512) if last else hm + po)
        h = jnp.concatenate(outs, axis=0).reshape(bsz, seq, d)
    return h
```

```python
import functools
import math

import numpy as np
import jax
import jax.numpy as jnp
from jax import lax
from jax.experimental import pallas as pl
from jax.experimental.pallas import tpu as pltpu
from jax.experimental.pallas import tpu_sc as plsc

F32 = jnp.float32
BF16 = jnp.bfloat16
I32 = jnp.int32

SSM_GROUP = 16
SSM_STATE = 64
ATTN_HEADS = 8
ATTN_KV_HEADS = 2
HEAD_DIM = 64
IDX_HEADS = 8
IDX_DIM = 32
TOPK_MAX = 256
ROPE_THETA = 10000.0
NEG_BIG = -1e30
PEER_HEADS = 8
PEER_KEYS = 128
PEER_KEY_DIM = 128
PEER_TOPK = 16
NORM_EPS = 1e-6

BATCH_CHUNKS = 4
INT_MIN = -(2 ** 31)
VMEM_LIMIT = 56 * 1024 * 1024


def _cparams(sem):
    return pltpu.CompilerParams(dimension_semantics=sem, vmem_limit_bytes=VMEM_LIMIT)


def _gelu_tanh(x):
    return 0.5 * x * (1.0 + jnp.tanh(math.sqrt(2.0 / math.pi) * (x + 0.044715 * (x * x * x))))


def _sigmoid(x):
    return 1.0 / (1.0 + jnp.exp(-x))


def _rot_cols(w, hd):
    d, n = w.shape
    w3 = w.reshape(d, n // hd, hd)
    half = hd // 2
    return jnp.concatenate([-w3[..., half:], w3[..., :half]], axis=-1).reshape(d, n)


def _rope_full(seq, hd, heads):
    pos = jnp.arange(seq, dtype=F32)
    inv = ROPE_THETA ** (-jnp.arange(0, hd, 2, dtype=F32) / hd)
    ang = pos[:, None] * inv[None, :]
    c = jnp.concatenate([jnp.cos(ang), jnp.cos(ang)], axis=-1)
    s = jnp.concatenate([jnp.sin(ang), jnp.sin(ang)], axis=-1)
    return jnp.tile(c, (1, heads)), jnp.tile(s, (1, heads))


def _inproj_kernel(x_ref, g_ref, w_ref, wr_ref, cs_ref, sn_ref,
                   u_ref, q_ref, k_ref, v_ref, qi_ref, ki_ref, wi_ref, gs_ref, ga_ref,
                   *, d_ssm, d_q, d_kv, d_qi, d_ki, n_wi, d_model, q_scale, wi_scale):
    x = x_ref[0]
    xn = x * lax.rsqrt(jnp.mean(x * x, axis=-1, keepdims=True) + NORM_EPS) * g_ref[...]
    xb = xn.astype(BF16)

    def mm(ref, lo, n):
        return jnp.dot(xb, ref[:, lo:lo + n], preferred_element_type=F32)

    o = 0
    u_ref[0] = mm(w_ref, o, d_ssm).astype(BF16)
    o += d_ssm
    ro = 0
    q = mm(w_ref, o, d_q) * cs_ref[:, ro:ro + d_q] + mm(wr_ref, ro, d_q) * sn_ref[:, ro:ro + d_q]
    q_ref[0] = (q * q_scale).astype(BF16)
    o += d_q
    ro += d_q
    k = mm(w_ref, o, d_kv) * cs_ref[:, ro:ro + d_kv] + mm(wr_ref, ro, d_kv) * sn_ref[:, ro:ro + d_kv]
    k_ref[0] = k.astype(BF16)
    o += d_kv
    ro += d_kv
    v_ref[0] = mm(w_ref, o, d_kv).astype(BF16)
    o += d_kv
    qi = mm(w_ref, o, d_qi) * cs_ref[:, ro:ro + d_qi] + mm(wr_ref, ro, d_qi) * sn_ref[:, ro:ro + d_qi]
    qi_ref[0] = qi.astype(BF16)
    o += d_qi
    ro += d_qi
    seg = mm(w_ref, o, 128)
    segr = mm(wr_ref, ro, 128)
    kiw = seg * cs_ref[:, ro:ro + 128] + segr * sn_ref[:, ro:ro + 128]
    ki_ref[0] = kiw[:, :d_ki].astype(BF16)
    wi_ref[0] = seg[:, d_ki:d_ki + n_wi] * wi_scale
    o += 128
    gs_ref[0] = _sigmoid(mm(w_ref, o, d_model)).astype(BF16)
    o += d_model
    ga_ref[0] = _sigmoid(mm(w_ref, o, d_model)).astype(BF16)


def _inproj(x, norm_g, w_in, tm):
    bsz, seq, d = x.shape
    d_ssm = d // 2
    d_q = ATTN_HEADS * HEAD_DIM
    d_kv = ATTN_KV_HEADS * HEAD_DIM
    d_qi = IDX_HEADS * IDX_DIM
    d_ki = IDX_DIM
    n_wi = IDX_HEADS
    splits = (d_ssm, d_q, d_kv, d_kv, d_qi, d_ki, n_wi, d, d)
    offs = np.cumsum(splits)[:-1].tolist()
    wu, wq, wk, wv, wqi, wki, wwi, wgs, wga = jnp.split(w_in, offs, axis=1)
    pad = jnp.zeros((d, 128 - d_ki - n_wi), F32)
    w_main = jnp.concatenate([wu, wq, wk, wv, wqi, wki, wwi, pad, wgs, wga], axis=1).astype(BF16)
    rpad = jnp.zeros((d, 128 - d_ki), F32)
    w_rot = jnp.concatenate([_rot_cols(wq, HEAD_DIM), _rot_cols(wk, HEAD_DIM),
                             _rot_cols(wqi, IDX_DIM), _rot_cols(wki, IDX_DIM), rpad], axis=1).astype(BF16)
    cq, sq = _rope_full(seq, HEAD_DIM, ATTN_HEADS)
    ck, sk = _rope_full(seq, HEAD_DIM, ATTN_KV_HEADS)
    cqi, sqi = _rope_full(seq, IDX_DIM, IDX_HEADS)
    cki, ski = _rope_full(seq, IDX_DIM, 1)
    tpad = jnp.zeros((seq, 128 - d_ki), F32)
    cs = jnp.concatenate([cq, ck, cqi, cki, tpad], axis=1)
    sn = jnp.concatenate([sq, sk, sqi, ski, tpad], axis=1)
    n_main = w_main.shape[1]
    n_rot = w_rot.shape[1]

    kern = functools.partial(
        _inproj_kernel, d_ssm=d_ssm, d_q=d_q, d_kv=d_kv, d_qi=d_qi, d_ki=d_ki, n_wi=n_wi, d_model=d,
        q_scale=HEAD_DIM ** -0.5, wi_scale=(IDX_HEADS ** -0.5) * (IDX_DIM ** -0.5))
    tok = lambda n: pl.BlockSpec((1, tm, n), lambda s, b: (b, s, 0))
    full = lambda shape: pl.BlockSpec(shape, lambda s, b: (0,) * len(shape))
    outs = [(d_ssm, BF16), (d_q, BF16), (d_kv, BF16), (d_kv, BF16), (d_qi, BF16), (d_ki, BF16),
            (n_wi, F32), (d, BF16), (d, BF16)]
    return pl.pallas_call(
        kern,
        grid=(seq // tm, bsz),
        in_specs=[tok(d), full((1, d)), full((d, n_main)), full((d, n_rot)),
                  pl.BlockSpec((tm, n_rot), lambda s, b: (s, 0)),
                  pl.BlockSpec((tm, n_rot), lambda s, b: (s, 0))],
        out_specs=[tok(n) for n, _ in outs],
        out_shape=[jax.ShapeDtypeStruct((bsz, seq, n), dt) for n, dt in outs],
        compiler_params=_cparams(("arbitrary", "arbitrary")),
        name="inproj",
    )(x, norm_g.reshape(1, d), w_main, w_rot, cs, sn)


def _s5_kernel(u_ref, bre_ref, bim_ref, cre_ref, cim_ref, are_ref, aim_ref, dsk_ref, wglu_ref,
               y_ref, sre, sim, st_re, st_im, *, tc, nb, lane_chunk):
    @pl.when(pl.program_id(0) == 0)
    def _():
        st_re[...] = jnp.zeros_like(st_re)
        st_im[...] = jnp.zeros_like(st_im)

    u = u_ref[...]
    n_half = bre_ref.shape[0]
    hin = bre_ref.shape[1]
    hst = bre_ref.shape[2]
    for h in range(n_half):
        uh = u[:, h * hin:(h + 1) * hin]
        sre[:, h * hst:(h + 1) * hst] = jnp.dot(uh, bre_ref[h], preferred_element_type=F32)
        sim[:, h * hst:(h + 1) * hst] = jnp.dot(uh, bim_ref[h], preferred_element_type=F32)

    n_state = sre.shape[1]
    for c in range(n_state // lane_chunk):
        cols = slice(c * lane_chunk, (c + 1) * lane_chunk)
        ar = are_ref[:, cols]
        ai = aim_ref[:, cols]

        def step(t, carry, cols=cols, ar=ar, ai=ai):
            sr, si = carry
            r0 = pl.multiple_of(t * nb, nb)
            nr = ar * sr - ai * si + sre[pl.ds(r0, nb), cols]
            ni = ar * si + ai * sr + sim[pl.ds(r0, nb), cols]
            sre[pl.ds(r0, nb), cols] = nr
            sim[pl.ds(r0, nb), cols] = ni
            return nr, ni

        sr, si = lax.fori_loop(0, tc, step, (st_re[:, cols], st_im[:, cols]), unroll=4)
        st_re[:, cols] = sr
        st_im[:, cols] = si

    ys = []
    for h in range(n_half):
        srh = sre[:, h * hst:(h + 1) * hst].astype(BF16)
        sih = sim[:, h * hst:(h + 1) * hst].astype(BF16)
        ys.append(jnp.dot(srh, cre_ref[h], preferred_element_type=F32)
                  - jnp.dot(sih, cim_ref[h], preferred_element_type=F32))
    y = jnp.concatenate(ys, axis=-1) + dsk_ref[...] * u.astype(F32)
    y = _gelu_tanh(y)
    gate = jnp.dot(y.astype(BF16), wglu_ref[...], preferred_element_type=F32)
    y_ref[...] = (y * _sigmoid(gate)).astype(BF16)


def _s5(u_tb, a_re, a_im, log_dt, b_re, b_im, c_re, c_im, d_skip, w_glu, nb, tc):
    rows, d_ssm = u_tb.shape
    groups = d_ssm // SSM_GROUP
    n_state = groups * SSM_STATE
    lam = lax.complex(a_re, a_im)
    dt = jnp.exp(log_dt)[:, None]
    a_bar = jnp.exp(lam * dt)
    b_bar = ((a_bar - 1.0) / lam)[..., None] * lax.complex(b_re, b_im)
    gh = min(groups, 256 // SSM_GROUP)
    n_half = groups // gh
    eye = jnp.eye(gh, dtype=F32)

    def bmat(bb):
        b4 = bb.reshape(n_half, gh, SSM_STATE, SSM_GROUP)
        return jnp.einsum('hgpc,gk->hgckp', b4, eye).reshape(n_half, gh * SSM_GROUP, gh * SSM_STATE)

    def cmat(cc):
        c4 = cc.reshape(n_half, gh, SSM_GROUP, SSM_STATE)
        return jnp.einsum('hgcp,gk->hgpkc', c4, eye).reshape(n_half, gh * SSM_STATE, gh * SSM_GROUP)

    bre = bmat(jnp.real(b_bar)).astype(BF16)
    bim = bmat(jnp.imag(b_bar)).astype(BF16)
    cre = cmat(c_re).astype(BF16)
    cim = cmat(c_im).astype(BF16)
    are = jnp.broadcast_to(jnp.real(a_bar).reshape(1, n_state), (nb, n_state))
    aim = jnp.broadcast_to(jnp.imag(a_bar).reshape(1, n_state), (nb, n_state))
    blk = tc * nb
    full = lambda a: pl.BlockSpec(a.shape, lambda i: (0,) * a.ndim)
    dsk = d_skip.reshape(1, d_ssm)
    wg = w_glu.astype(BF16)
    kern = functools.partial(_s5_kernel, tc=tc, nb=nb, lane_chunk=512)
    return pl.pallas_call(
        kern,
        grid=(rows // blk,),
        in_specs=[pl.BlockSpec((blk, d_ssm), lambda i: (i, 0)),
                  full(bre), full(bim), full(cre), full(cim), full(are), full(aim), full(dsk), full(wg)],
        out_specs=pl.BlockSpec((blk, d_ssm), lambda i: (i, 0)),
        out_shape=jax.ShapeDtypeStruct((rows, d_ssm), BF16),
        scratch_shapes=[pltpu.VMEM((blk, n_state), F32), pltpu.VMEM((blk, n_state), F32),
                        pltpu.VMEM((nb, n_state), F32), pltpu.VMEM((nb, n_state), F32)],
        compiler_params=_cparams(("arbitrary",)),
        name="s5",
    )(u_tb, bre, bim, cre, cim, are, aim, dsk, wg)


def _dsa_kernel(qi_ref, wit_ref, q_ref, ki_ref, k_ref, vt_ref, o_ref, key_s,
                *, tq, kt, topk, seq_bits):
    qb = pl.program_id(1)
    nkt = (qb * tq + tq + kt - 1) // kt
    q_pos = qb * tq + lax.broadcasted_iota(I32, (1, tq), 1)
    k_eff = jnp.minimum(topk, q_pos + 1).astype(F32)

    qi = qi_ref[0]
    wit = wit_ref[0]
    qih = [qi[:, h * IDX_DIM:(h + 1) * IDX_DIM] for h in range(IDX_HEADS)]

    def key_pos(t):
        return t * kt + lax.broadcasted_iota(I32, (kt, tq), 0)

    def score_tile(t, _):
        r0 = pl.multiple_of(t * kt, kt)
        ki_t = ki_ref[0, pl.ds(r0, kt), :]
        sc = jnp.zeros((kt, tq), F32)
        for h in range(IDX_HEADS):
            rel = lax.dot_general(ki_t, qih[h], (((1,), (1,)), ((), ())), preferred_element_type=F32)
            sc = sc + jnp.maximum(rel, 0.0) * wit[h:h + 1, :]
        bits = lax.bitcast_convert_type(sc, I32)
        key = jnp.where(bits < 0, bits ^ jnp.int32(0x7FFFFFFF), bits)
        key = jnp.where(key_pos(t) <= q_pos, key, jnp.int32(INT_MIN))
        key_s[pl.ds(r0, kt), :] = key
        return 0

    lax.fori_loop(0, nkt, score_tile, 0)

    def count(pred_fn):
        def body(t, acc):
            r0 = pl.multiple_of(t * kt, kt)
            m = pred_fn(key_s[pl.ds(r0, kt), :], t)
            ones = jnp.where(m, 1.0, 0.0).reshape(kt // 8, 8, tq)
            return acc + jnp.sum(ones, axis=0)
        acc = lax.fori_loop(0, nkt, body, jnp.zeros((8, tq), F32))
        return jnp.sum(acc, axis=0, keepdims=True)

    def bit_step(i, u):
        bit = jnp.left_shift(jnp.int32(1), 31 - i)
        cand_u = u | bit
        cand_s = cand_u ^ jnp.int32(INT_MIN)
        cnt = count(lambda kk, t: kk >= cand_s)
        return jnp.where(cnt >= k_eff, cand_u, u)

    u_thr = lax.fori_loop(0, 32, bit_step, jnp.zeros((1, tq), I32))
    thr = u_thr ^ jnp.int32(INT_MIN)

    cnt_ge = count(lambda kk, t: kk >= thr)
    cnt_gt = count(lambda kk, t: kk > thr)
    need_eq = k_eff - cnt_gt
    has_tie = jnp.max(cnt_ge - k_eff) > 0.0

    def tie_cut():
        def pos_step(i, c):
            bit = jnp.left_shift(jnp.int32(1), seq_bits - 1 - i)
            cand = c | bit
            cnt = count(lambda kk, t: (kk == thr) & (key_pos(t) < cand))
            return jnp.where(cnt < need_eq, cand, c)
        return lax.fori_loop(0, seq_bits, pos_step, jnp.zeros((1, tq), I32))

    cut = lax.cond(has_tie, tie_cut, lambda: jnp.full((1, tq), 2 ** seq_bits, I32))

    q = q_ref[0]
    grp = ATTN_HEADS // ATTN_KV_HEADS
    for n in range(ATTN_KV_HEADS):
        qn = jnp.concatenate([q[:, (n * grp + g) * HEAD_DIM:(n * grp + g + 1) * HEAD_DIM]
                              for g in range(grp)], axis=0)

        def attn_tile(t, carry, n=n, qn=qn):
            m, l, acc = carry
            r0 = pl.multiple_of(t * kt, kt)
            key = key_s[pl.ds(r0, kt), :]
            sel = (key > thr) | ((key == thr) & (key_pos(t) <= cut))
            k_t = k_ref[0, n, pl.ds(r0, kt), :]
            lg = lax.dot_general(k_t, qn, (((1,), (1,)), ((), ())), preferred_element_type=F32)
            sel4 = jnp.concatenate([sel] * grp, axis=1)
            lg = jnp.where(sel4, lg, NEG_BIG)
            m_new = jnp.maximum(m, jnp.max(lg, axis=0, keepdims=True))
            p = jnp.where(sel4, jnp.exp(lg - m_new), 0.0)
            alpha = jnp.exp(m - m_new)
            l = alpha * l + jnp.sum(p, axis=0, keepdims=True)
            v_t = vt_ref[0, n, t]
            acc = alpha * acc + jnp.dot(v_t, p.astype(BF16), preferred_element_type=F32)
            return m_new, l, acc

        init = (jnp.full((1, grp * tq), NEG_BIG, F32), jnp.zeros((1, grp * tq), F32),
                jnp.zeros((HEAD_DIM, grp * tq), F32))
        m, l, acc = lax.fori_loop(0, nkt, attn_tile, init)
        o_ref[0, 0, n] = (acc / l).astype(BF16)


def _dsa_prep(k, v, wi, kt):
    bsz, seq, _ = k.shape
    wit = wi.transpose(0, 2, 1)
    k4 = k.reshape(bsz, seq, ATTN_KV_HEADS, HEAD_DIM).transpose(0, 2, 1, 3)
    vt = v.reshape(bsz, seq // kt, kt, ATTN_KV_HEADS, HEAD_DIM).transpose(0, 3, 1, 4, 2)
    return wit, k4, vt


def _dsa(q, qi, ki, wit, k4, vt, b0, nb, tq, kt):
    _, seq, _ = q.shape
    topk = min(TOPK_MAX, seq // 4)
    nqb = seq // tq
    grp = ATTN_HEADS // ATTN_KV_HEADS
    seq_bits = int(math.log2(seq))
    assert 2 ** seq_bits == seq
    kern = functools.partial(_dsa_kernel, tq=tq, kt=kt, topk=topk, seq_bits=seq_bits)
    o_t = pl.pallas_call(
        kern,
        grid=(nb, nqb),
        in_specs=[pl.BlockSpec((1, tq, IDX_HEADS * IDX_DIM), lambda b, j: (b + b0, j, 0)),
                  pl.BlockSpec((1, IDX_HEADS, tq), lambda b, j: (b + b0, 0, j)),
                  pl.BlockSpec((1, tq, ATTN_HEADS * HEAD_DIM), lambda b, j: (b + b0, j, 0)),
                  pl.BlockSpec((1, seq, IDX_DIM), lambda b, j: (b + b0, 0, 0)),
                  pl.BlockSpec((1, ATTN_KV_HEADS, seq, HEAD_DIM), lambda b, j: (b + b0, 0, 0, 0)),
                  pl.BlockSpec((1, ATTN_KV_HEADS, seq // kt, HEAD_DIM, kt), lambda b, j: (b + b0, 0, 0, 0, 0))],
        out_specs=pl.BlockSpec((1, 1, ATTN_KV_HEADS, HEAD_DIM, grp * tq), lambda b, j: (b, j, 0, 0, 0)),
        out_shape=jax.ShapeDtypeStruct((nb, nqb, ATTN_KV_HEADS, HEAD_DIM, grp * tq), BF16),
        scratch_shapes=[pltpu.VMEM((seq, tq), I32)],
        compiler_params=_cparams(("arbitrary", "arbitrary")),
        name="dsa",
    )(qi, wit, q, ki, k4, vt)
    o = o_t.reshape(nb, nqb, ATTN_KV_HEADS, HEAD_DIM, grp, tq).transpose(0, 1, 5, 2, 4, 3)
    return o.reshape(nb * seq, ATTN_HEADS * HEAD_DIM)


def _merge_kernel(x_ref, ys_ref, ya_ref, gs_ref, ga_ref, wsu_ref, wau_ref, wout_ref, g2_ref, wq_ref,
                  h_ref, hn_ref, qp_ref):
    ms = jnp.dot(ys_ref[...], wsu_ref[...], preferred_element_type=F32)
    ma = jnp.dot(ya_ref[...], wau_ref[...], preferred_element_type=F32)
    merged = gs_ref[...].astype(F32) * ms + ga_ref[...].astype(F32) * ma
    h = x_ref[...] + jnp.dot(merged.astype(BF16), wout_ref[...], preferred_element_type=F32)
    h_ref[...] = h
    hn = h * lax.rsqrt(jnp.mean(h * h, axis=-1, keepdims=True) + NORM_EPS) * g2_ref[...]
    hb = hn.astype(BF16)
    bits = lax.bitcast_convert_type(hb.astype(F32), I32)
    half = bits.shape[1] // 2
    hn_ref[...] = (bits[:, half:] & jnp.int32(-65536)) | lax.shift_right_logical(bits[:, :half], 16)
    qp_ref[...] = jnp.dot(hb, wq_ref[...], preferred_element_type=F32).astype(BF16)


def _merge(x2, ys, ya, gs, ga, wsu, wau, wo, norm2_g, wq, r0, tm):
    n = ya.shape[0]
    d = x2.shape[1]
    i0 = r0 // tm
    row = lambda a: pl.BlockSpec((tm, a.shape[1]), lambda i: (i + i0, 0))
    loc = lambda a: pl.BlockSpec((tm, a.shape[1]), lambda i: (i, 0))
    full = lambda a: pl.BlockSpec(a.shape, lambda i: (0,) * a.ndim)
    g2 = norm2_g.reshape(1, d)
    nq = wq.shape[1]
    return pl.pallas_call(
        _merge_kernel,
        grid=(n // tm,),
        in_specs=[row(x2), row(ys), loc(ya), row(gs), row(ga), full(wsu), full(wau), full(wo), full(g2), full(wq)],
        out_specs=[pl.BlockSpec((tm, d), lambda i: (i, 0)), pl.BlockSpec((tm, d // 2), lambda i: (i, 0)),
                   pl.BlockSpec((tm, nq), lambda i: (i, 0))],
        out_shape=[jax.ShapeDtypeStruct((n, d), F32), jax.ShapeDtypeStruct((n, d // 2), I32),
                   jax.ShapeDtypeStruct((n, nq), BF16)],
        compiler_params=_cparams(("arbitrary",)),
        name="merge",
    )(x2, ys, ya, gs, ga, wsu, wau, wo, g2, wq)


def _cand_layout():
    blocks = []
    blocks.append((0, 16, 16))
    for i in range(1, 8):
        blocks.append((i, 8, PEER_TOPK // (i + 1)))
    blocks.append((None, 8, 8))
    return blocks


def _top_rows(s, order, payload, k):
    big = jnp.float32(3e38)
    vals, pays = [], []
    for _ in range(k):
        m = jnp.max(s, axis=0, keepdims=True)
        o = jnp.min(jnp.where(s == m, order, big), axis=0, keepdims=True)
        hit = order == o
        pays.append(jnp.min(jnp.where(hit, payload, big), axis=0, keepdims=True))
        vals.append(m)
        s = jnp.where(hit, -jnp.inf, s)
    return jnp.concatenate(vals, axis=0), jnp.concatenate(pays, axis=0)


def _route_kernel(qp_ref, k1_ref, k2_ref, e_ref, g_ref, *, tt):
    qp = qp_ref[...]
    kd = PEER_KEY_DIM
    rows_k = lax.broadcasted_iota(I32, (PEER_KEYS, tt), 0).astype(F32)
    for h in range(PEER_HEADS):
        q1 = qp[:, (2 * h) * kd:(2 * h + 1) * kd]
        q2 = qp[:, (2 * h + 1) * kd:(2 * h + 2) * kd]
        s1 = lax.dot_general(k1_ref[h], q1, (((1,), (1,)), ((), ())), preferred_element_type=F32)
        s2 = lax.dot_general(k2_ref[h], q2, (((1,), (1,)), ((), ())), preferred_element_type=F32)
        v1, i1 = _top_rows(s1, rows_k, rows_k, PEER_TOPK)
        v2, i2 = _top_rows(s2, rows_k, rows_k, PEER_TOPK)
        cs, ce, co = [], [], []
        for i, rows, valid in _cand_layout():
            r = lax.broadcasted_iota(I32, (rows, tt), 0).astype(F32)
            if i is None:
                val = v1[8:16] + v2[0:1]
                eid = i1[8:16] * PEER_KEYS + i2[0:1]
                flat = (r + 8.0) * PEER_TOPK
            else:
                val = v1[i:i + 1] + v2[0:rows]
                eid = i1[i:i + 1] * PEER_KEYS + i2[0:rows]
                flat = r + float(i * PEER_TOPK)
                if valid < rows:
                    val = jnp.where(r < float(valid), val, -jnp.inf)
            cs.append(val)
            ce.append(eid)
            co.append(flat)
        cand = jnp.concatenate(cs, axis=0)
        top_s, top_e = _top_rows(cand, jnp.concatenate(co, axis=0), jnp.concatenate(ce, axis=0), PEER_TOPK)
        p = jnp.exp(top_s - top_s[0:1])
        gates = p / jnp.sum(p, axis=0, keepdims=True)
        e_ref[h * PEER_TOPK:(h + 1) * PEER_TOPK, :] = top_e.astype(I32)
        g_ref[h * PEER_TOPK:(h + 1) * PEER_TOPK, :] = gates


def _route(qp, peer_k1, peer_k2, tt):
    n, nq = qp.shape
    k1 = peer_k1.astype(BF16)
    k2 = peer_k2.astype(BF16)
    n_sel = PEER_HEADS * PEER_TOPK
    full = lambda a: pl.BlockSpec(a.shape, lambda i: (0,) * a.ndim)
    return pl.pallas_call(
        functools.partial(_route_kernel, tt=tt),
        grid=(n // tt,),
        in_specs=[pl.BlockSpec((tt, nq), lambda i: (i, 0)), full(k1), full(k2)],
        out_specs=[pl.BlockSpec((n_sel, tt), lambda i: (0, i)), pl.BlockSpec((n_sel, tt), lambda i: (0, i))],
        out_shape=[jax.ShapeDtypeStruct((n_sel, n), I32), jax.ShapeDtypeStruct((n_sel, n), F32)],
        compiler_params=_cparams(("arbitrary",)),
        name="route",
    )(qp, k1, k2)


def _final_kernel(h_ref, p_ref, g_ref, o_ref):
    h = h_ref[...] + p_ref[...]
    o_ref[...] = h * lax.rsqrt(jnp.mean(h * h, axis=-1, keepdims=True) + NORM_EPS) * g_ref[...]


def _final(h, p, g, tm):
    n, d = h.shape
    row = pl.BlockSpec((tm, d), lambda i: (i, 0))
    return pl.pallas_call(
        _final_kernel,
        grid=(n // tm,),
        in_specs=[row, row, pl.BlockSpec((1, d), lambda i: (0, 0))],
        out_specs=row,
        out_shape=jax.ShapeDtypeStruct((n, d), F32),
        compiler_params=_cparams(("arbitrary",)),
        name="final",
    )(h, p, g.reshape(1, d))


SC_CORES_V7X = 2
SC_SUBCORES_V7X = 16
SC_LANES_V7X = 16
PEER_TOK_BATCH = 16
PEER_ROW_CHUNK = 64


def _pack_bf16_pairs(t):
    half = t.shape[1] // 2
    tb = t.astype(BF16)
    lo = lax.bitcast_convert_type(tb[:, :half], jnp.uint16).astype(jnp.uint32)
    hi = lax.bitcast_convert_type(tb[:, half:], jnp.uint16).astype(jnp.uint32)
    return lax.bitcast_convert_type(lo | (hi << 16), I32)


def _unpack_pair(w):
    lo = lax.bitcast_convert_type(jnp.left_shift(w, 16), F32)
    hi = lax.bitcast_convert_type(w & jnp.int32(-65536), F32)
    return lo, hi


def _peer_sc_body(hn_hbm, e_hbm, g_hbm, u_hbm, v_hbm, out_hbm,
                  idx_v, gate_v, x_v, out_v, rows, p_v, act_v, sem, *, tpw, d, n_sel):
    nl = SC_LANES_V7X
    tb = PEER_TOK_BATCH
    rc = PEER_ROW_CHUNK
    n_chunk = n_sel // rc
    jobs_per_tok = 2 * n_chunk
    half = d // 2
    n_lane_blk = half // nl
    wid = lax.axis_index("s") * SC_CORES_V7X + lax.axis_index("c")
    base = wid * tpw
    lane = lax.iota(I32, nl)
    zero = jnp.zeros((nl,), F32)
    c_gelu = 2.0 * math.sqrt(2.0 / math.pi)

    def gather_copy(tab_hbm, job):
        tok = job // jobs_per_tok
        c = (job % jobs_per_tok) % n_chunk
        b = job % 2
        return pltpu.make_async_copy(tab_hbm.at[idx_v.at[tok, pl.ds(c * rc, rc)]], rows.at[b], sem.at[b])

    def start(job):
        j = job % jobs_per_tok

        @pl.when(j < n_chunk)
        def _():
            gather_copy(u_hbm, job).start()

        @pl.when(j >= n_chunk)
        def _():
            gather_copy(v_hbm, job).start()

    def compute_u(tok, c, b):
        def rg_body(rg, _):
            r0 = rg * 8

            def jbody(j2, accs):
                off0 = j2 * (2 * nl)
                off1 = off0 + nl
                x0 = plsc.bitcast(x_v[tok, pl.ds(off0, nl)], BF16)
                x1 = plsc.bitcast(x_v[tok, pl.ds(off1, nl)], BF16)
                new = []
                for r in range(8):
                    w0 = plsc.bitcast(rows[b, r0 + r, pl.ds(off0, nl)], BF16)
                    w1 = plsc.bitcast(rows[b, r0 + r, pl.ds(off1, nl)], BF16)
                    lo, hi = _unpack_pair(plsc.bitcast(w0 * x0 + w1 * x1, I32))
                    new.append(accs[r] + (lo + hi))
                return tuple(new)

            accs = lax.fori_loop(0, n_lane_blk // 2, jbody, (zero,) * 8)
            for r in range(8):
                p_v[c * rc + r0 + r, :] = accs[r]
            return 0

        lax.fori_loop(0, rc // 8, rg_body, 0)

    def finish_act(tok):
        def eg_body(eg, _):
            e0 = eg * nl
            ridx = e0 + lane
            s = zero
            for l in range(nl):
                s = s + plsc.load_gather(p_v, [ridx, jnp.full((nl,), l, I32)])
            inner = c_gelu * (s + 0.044715 * (s * s * s))
            gl = s / (1.0 + jnp.exp(-inner))
            a = gl * gate_v[tok, pl.ds(e0, nl)]
            bits = lax.bitcast_convert_type(a, I32)
            rnd = bits + jnp.int32(0x7FFF) + (lax.shift_right_logical(bits, 16) & 1)
            hi16 = rnd & jnp.int32(-65536)
            act_v[pl.ds(e0, nl)] = hi16 | lax.shift_right_logical(hi16, 16)
            return 0

        lax.fori_loop(0, n_sel // nl, eg_body, 0)

        def zbody(j, _):
            out_v[tok, pl.ds(j * nl, nl)] = zero
            return 0

        lax.fori_loop(0, d // nl, zbody, 0, unroll=4)

    def compute_v(tok, c, b):
        def rg_body(rg, _):
            r0 = rg * nl
            splat = [plsc.bitcast(plsc.load_gather(act_v, [jnp.full((nl,), 0, I32) + (c * rc + r0 + r)]), BF16)
                     for r in range(nl)]

            def tree(parts):
                while len(parts) > 1:
                    parts = [parts[i] + parts[i + 1] for i in range(0, len(parts), 2)]
                return parts[0]

            @plsc.parallel_loop(0, n_lane_blk, unroll=2)
            def _(j):
                off = j * nl
                los, his = [], []
                for r in range(0, nl, 2):
                    w0 = plsc.bitcast(rows[b, r0 + r, pl.ds(off, nl)], BF16)
                    w1 = plsc.bitcast(rows[b, r0 + r + 1, pl.ds(off, nl)], BF16)
                    lo, hi = _unpack_pair(plsc.bitcast(w0 * splat[r] + w1 * splat[r + 1], I32))
                    los.append(lo)
                    his.append(hi)
                out_v[tok, pl.ds(off, nl)] = out_v[tok, pl.ds(off, nl)] + tree(los)
                out_v[tok, pl.ds(half + off, nl)] = out_v[tok, pl.ds(half + off, nl)] + tree(his)

            return 0

        lax.fori_loop(0, rc // nl, rg_body, 0)

    def batch_body(bi, _):
        t0 = base + bi * tb
        pltpu.sync_copy(e_hbm.at[pl.ds(t0, tb)], idx_v)
        pltpu.sync_copy(g_hbm.at[pl.ds(t0, tb)], gate_v)
        pltpu.sync_copy(hn_hbm.at[pl.ds(t0, tb)], x_v)
        start(0)

        def job_body(job, _):
            @pl.when(job + 1 < tb * jobs_per_tok)
            def _():
                start(job + 1)

            j = job % jobs_per_tok
            gather_copy(u_hbm, job).wait()
            tok = job // jobs_per_tok
            b = job % 2

            @pl.when(j < n_chunk)
            def _():
                compute_u(tok, j, b)

            @pl.when(j == n_chunk - 1)
            def _():
                finish_act(tok)

            @pl.when(j >= n_chunk)
            def _():
                compute_v(tok, j - n_chunk, b)

            return 0

        lax.fori_loop(0, tb * jobs_per_tok, job_body, 0)
        pltpu.sync_copy(out_v, out_hbm.at[pl.ds(t0, tb)])
        return 0

    lax.fori_loop(0, tpw // tb, batch_body, 0)


def _peer_sc(x_pk, experts, gates, u_tab, v_tab):
    n = x_pk.shape[0]
    d = 2 * x_pk.shape[1]
    n_sel = experts.shape[1]
    nw = SC_CORES_V7X * SC_SUBCORES_V7X
    tpw = n // nw
    mesh = plsc.VectorSubcoreMesh(core_axis_name="c", subcore_axis_name="s",
                                  num_cores=SC_CORES_V7X, num_subcores=SC_SUBCORES_V7X)
    body = functools.partial(_peer_sc_body, tpw=tpw, d=d, n_sel=n_sel)
    call = pl.kernel(
        body,
        out_type=jax.ShapeDtypeStruct((n, d), F32),
        mesh=mesh,
        scratch_types=[pltpu.VMEM((PEER_TOK_BATCH, n_sel), I32),
                       pltpu.VMEM((PEER_TOK_BATCH, n_sel), F32),
                       pltpu.VMEM((PEER_TOK_BATCH, d // 2), I32),
                       pltpu.VMEM((PEER_TOK_BATCH, d), F32),
                       pltpu.VMEM((2, PEER_ROW_CHUNK, d // 2), I32),
                       pltpu.VMEM((n_sel, SC_LANES_V7X), F32),
                       pltpu.VMEM((n_sel,), I32),
                       pltpu.SemaphoreType.DMA((2,))],
        compiler_params=pltpu.CompilerParams(needs_layout_passes=False),
        name="peer_sc",
    )
    return call(x_pk, experts, gates, u_tab, v_tab)


def kernel(x, norm1_g, w_in, a_re, a_im, log_dt, b_re, b_im, c_re, c_im, d_skip, w_glu, w_ssm_up, w_attn_up,
           w_out, norm2_g, peer_wq, peer_k1, peer_k2, peer_u, peer_v, norm_f_g):
    bsz, seq, d = x.shape
    n = bsz * seq
    h = x
    depth = norm1_g.shape[0]
    n_chunks = BATCH_CHUNKS if bsz % BATCH_CHUNKS == 0 else 1
    nb = bsz // n_chunks
    for layer in range(depth):
        last = layer + 1 == depth
        u, q, k, v, qi, ki, wi, gs, ga = _inproj(h, norm1_g[layer], w_in[layer], tm=min(512, seq))
        d_ssm = u.shape[-1]
        u_tb = u.transpose(1, 0, 2).reshape(n, d_ssm)
        y_tb = _s5(u_tb, a_re[layer], a_im[layer], log_dt[layer], b_re[layer], b_im[layer], c_re[layer],
                   c_im[layer], d_skip[layer], w_glu[layer], nb=bsz, tc=64)
        ys = y_tb.reshape(seq, bsz, d_ssm).transpose(1, 0, 2).reshape(n, d_ssm)
        wit, k4, vt = _dsa_prep(k, v, wi, kt=256)
        wsu = w_ssm_up[layer].astype(BF16)
        wau = w_attn_up[layer].astype(BF16)
        wo = w_out[layer].astype(BF16)
        wq = peer_wq[layer].astype(BF16)
        u_pk = _pack_bf16_pairs(peer_u[layer])
        v_pk = _pack_bf16_pairs(peer_v[layer])
        h2 = h.reshape(n, d)
        outs = []
        for c in range(n_chunks):
            ya = _dsa(q, qi, ki, wit, k4, vt, b0=c * nb, nb=nb, tq=128, kt=256)
            hm, hn, qp = _merge(h2, ys, ya, gs.reshape(n, d), ga.reshape(n, d), wsu, wau, wo, norm2_g[layer], wq,
                                r0=c * nb * seq, tm=512)
            e_t, g_t = _route(qp, peer_k1[layer], peer_k2[layer], tt=256)
            po = _peer_sc(hn, e_t.T, g_t.T, u_pk, v_pk)
            outs.append(_final(hm, po, norm_f_g, tm=512) if last else hm + po)
        h = jnp.concatenate(outs, axis=0).reshape(bsz, seq, d)
    return h
```

```python
import functools
import math

import numpy as np
import jax
import jax.numpy as jnp
from jax import lax
from jax.experimental import pallas as pl
from jax.experimental.pallas import tpu as pltpu
from jax.experimental.pallas import tpu_sc as plsc

F32 = jnp.float32
BF16 = jnp.bfloat16
I32 = jnp.int32

SSM_GROUP = 16
SSM_STATE = 64
ATTN_HEADS = 8
ATTN_KV_HEADS = 2
HEAD_DIM = 64
IDX_HEADS = 8
IDX_DIM = 32
TOPK_MAX = 256
ROPE_THETA = 10000.0
NEG_BIG = -1e30
PEER_HEADS = 8
PEER_KEYS = 128
PEER_KEY_DIM = 128
PEER_TOPK = 16
NORM_EPS = 1e-6

BATCH_CHUNKS = 8
INT_MIN = -(2 ** 31)
VMEM_LIMIT = 56 * 1024 * 1024


def _cparams(sem):
    return pltpu.CompilerParams(dimension_semantics=sem, vmem_limit_bytes=VMEM_LIMIT)


def _gelu_tanh(x):
    return 0.5 * x * (1.0 + jnp.tanh(math.sqrt(2.0 / math.pi) * (x + 0.044715 * (x * x * x))))


def _sigmoid(x):
    return 1.0 / (1.0 + jnp.exp(-x))


def _rot_cols(w, hd):
    d, n = w.shape
    w3 = w.reshape(d, n // hd, hd)
    half = hd // 2
    return jnp.concatenate([-w3[..., half:], w3[..., :half]], axis=-1).reshape(d, n)


def _rope_full(seq, hd, heads):
    pos = jnp.arange(seq, dtype=F32)
    inv = ROPE_THETA ** (-jnp.arange(0, hd, 2, dtype=F32) / hd)
    ang = pos[:, None] * inv[None, :]
    c = jnp.concatenate([jnp.cos(ang), jnp.cos(ang)], axis=-1)
    s = jnp.concatenate([jnp.sin(ang), jnp.sin(ang)], axis=-1)
    return jnp.tile(c, (1, heads)), jnp.tile(s, (1, heads))


def _inproj_kernel(x_ref, g_ref, w_ref, wr_ref, cs_ref, sn_ref,
                   u_ref, q_ref, k_ref, v_ref, qi_ref, ki_ref, wi_ref, gs_ref, ga_ref,
                   *, d_ssm, d_q, d_kv, d_qi, d_ki, n_wi, d_model, q_scale, wi_scale):
    x = x_ref[0]
    xn = x * lax.rsqrt(jnp.mean(x * x, axis=-1, keepdims=True) + NORM_EPS) * g_ref[...]
    xb = xn.astype(BF16)

    def mm(ref, lo, n):
        return jnp.dot(xb, ref[:, lo:lo + n], preferred_element_type=F32)

    o = 0
    u_ref[0] = mm(w_ref, o, d_ssm).astype(BF16)
    o += d_ssm
    ro = 0
    q = mm(w_ref, o, d_q) * cs_ref[:, ro:ro + d_q] + mm(wr_ref, ro, d_q) * sn_ref[:, ro:ro + d_q]
    q_ref[0] = (q * q_scale).astype(BF16)
    o += d_q
    ro += d_q
    k = mm(w_ref, o, d_kv) * cs_ref[:, ro:ro + d_kv] + mm(wr_ref, ro, d_kv) * sn_ref[:, ro:ro + d_kv]
    k_ref[0] = k.astype(BF16)
    o += d_kv
    ro += d_kv
    v_ref[0] = mm(w_ref, o, d_kv).astype(BF16)
    o += d_kv
    qi = mm(w_ref, o, d_qi) * cs_ref[:, ro:ro + d_qi] + mm(wr_ref, ro, d_qi) * sn_ref[:, ro:ro + d_qi]
    qi_ref[0] = qi.astype(BF16)
    o += d_qi
    ro += d_qi
    seg = mm(w_ref, o, 128)
    segr = mm(wr_ref, ro, 128)
    kiw = seg * cs_ref[:, ro:ro + 128] + segr * sn_ref[:, ro:ro + 128]
    ki_ref[0] = kiw[:, :d_ki].astype(BF16)
    wi_ref[0] = seg[:, d_ki:d_ki + n_wi] * wi_scale
    o += 128
    gs_ref[0] = _sigmoid(mm(w_ref, o, d_model)).astype(BF16)
    o += d_model
    ga_ref[0] = _sigmoid(mm(w_ref, o, d_model)).astype(BF16)


def _inproj(x, norm_g, w_in, tm):
    bsz, seq, d = x.shape
    d_ssm = d // 2
    d_q = ATTN_HEADS * HEAD_DIM
    d_kv = ATTN_KV_HEADS * HEAD_DIM
    d_qi = IDX_HEADS * IDX_DIM
    d_ki = IDX_DIM
    n_wi = IDX_HEADS
    splits = (d_ssm, d_q, d_kv, d_kv, d_qi, d_ki, n_wi, d, d)
    offs = np.cumsum(splits)[:-1].tolist()
    wu, wq, wk, wv, wqi, wki, wwi, wgs, wga = jnp.split(w_in, offs, axis=1)
    pad = jnp.zeros((d, 128 - d_ki - n_wi), F32)
    w_main = jnp.concatenate([wu, wq, wk, wv, wqi, wki, wwi, pad, wgs, wga], axis=1).astype(BF16)
    rpad = jnp.zeros((d, 128 - d_ki), F32)
    w_rot = jnp.concatenate([_rot_cols(wq, HEAD_DIM), _rot_cols(wk, HEAD_DIM),
                             _rot_cols(wqi, IDX_DIM), _rot_cols(wki, IDX_DIM), rpad], axis=1).astype(BF16)
    cq, sq = _rope_full(seq, HEAD_DIM, ATTN_HEADS)
    ck, sk = _rope_full(seq, HEAD_DIM, ATTN_KV_HEADS)
    cqi, sqi = _rope_full(seq, IDX_DIM, IDX_HEADS)
    cki, ski = _rope_full(seq, IDX_DIM, 1)
    tpad = jnp.zeros((seq, 128 - d_ki), F32)
    cs = jnp.concatenate([cq, ck, cqi, cki, tpad], axis=1)
    sn = jnp.concatenate([sq, sk, sqi, ski, tpad], axis=1)
    n_main = w_main.shape[1]
    n_rot = w_rot.shape[1]

    kern = functools.partial(
        _inproj_kernel, d_ssm=d_ssm, d_q=d_q, d_kv=d_kv, d_qi=d_qi, d_ki=d_ki, n_wi=n_wi, d_model=d,
        q_scale=HEAD_DIM ** -0.5 * math.log2(math.e), wi_scale=(IDX_HEADS ** -0.5) * (IDX_DIM ** -0.5))
    tok = lambda n: pl.BlockSpec((1, tm, n), lambda s, b: (b, s, 0))
    full = lambda shape: pl.BlockSpec(shape, lambda s, b: (0,) * len(shape))
    outs = [(d_ssm, BF16), (d_q, BF16), (d_kv, BF16), (d_kv, BF16), (d_qi, BF16), (d_ki, BF16),
            (n_wi, F32), (d, BF16), (d, BF16)]
    return pl.pallas_call(
        kern,
        grid=(seq // tm, bsz),
        in_specs=[tok(d), full((1, d)), full((d, n_main)), full((d, n_rot)),
                  pl.BlockSpec((tm, n_rot), lambda s, b: (s, 0)),
                  pl.BlockSpec((tm, n_rot), lambda s, b: (s, 0))],
        out_specs=[tok(n) for n, _ in outs],
        out_shape=[jax.ShapeDtypeStruct((bsz, seq, n), dt) for n, dt in outs],
        compiler_params=_cparams(("arbitrary", "arbitrary")),
        name="inproj",
    )(x, norm_g.reshape(1, d), w_main, w_rot, cs, sn)


def _s5_kernel(u_ref, bre_ref, bim_ref, cre_ref, cim_ref, are_ref, aim_ref, dsk_ref, wglu_ref,
               y_ref, sre, sim, st_re, st_im, *, tc, nb, lane_chunk):
    @pl.when(pl.program_id(0) == 0)
    def _():
        st_re[...] = jnp.zeros_like(st_re)
        st_im[...] = jnp.zeros_like(st_im)

    u = u_ref[...]
    n_half = bre_ref.shape[0]
    hin = bre_ref.shape[1]
    hst = bre_ref.shape[2]
    for h in range(n_half):
        uh = u[:, h * hin:(h + 1) * hin]
        sre[:, h * hst:(h + 1) * hst] = jnp.dot(uh, bre_ref[h], preferred_element_type=F32)
        sim[:, h * hst:(h + 1) * hst] = jnp.dot(uh, bim_ref[h], preferred_element_type=F32)

    n_state = sre.shape[1]
    for c in range(n_state // lane_chunk):
        cols = slice(c * lane_chunk, (c + 1) * lane_chunk)
        ar = are_ref[:, cols]
        ai = aim_ref[:, cols]

        def step(t, carry, cols=cols, ar=ar, ai=ai):
            sr, si = carry
            r0 = pl.multiple_of(t * nb, nb)
            nr = ar * sr - ai * si + sre[pl.ds(r0, nb), cols]
            ni = ar * si + ai * sr + sim[pl.ds(r0, nb), cols]
            sre[pl.ds(r0, nb), cols] = nr
            sim[pl.ds(r0, nb), cols] = ni
            return nr, ni

        sr, si = lax.fori_loop(0, tc, step, (st_re[:, cols], st_im[:, cols]), unroll=4)
        st_re[:, cols] = sr
        st_im[:, cols] = si

    ys = []
    for h in range(n_half):
        srh = sre[:, h * hst:(h + 1) * hst].astype(BF16)
        sih = sim[:, h * hst:(h + 1) * hst].astype(BF16)
        ys.append(jnp.dot(srh, cre_ref[h], preferred_element_type=F32)
                  - jnp.dot(sih, cim_ref[h], preferred_element_type=F32))
    y = jnp.concatenate(ys, axis=-1) + dsk_ref[...] * u.astype(F32)
    y = _gelu_tanh(y)
    gate = jnp.dot(y.astype(BF16), wglu_ref[...], preferred_element_type=F32)
    y_ref[...] = (y * _sigmoid(gate)).astype(BF16)


def _s5(u_tb, a_re, a_im, log_dt, b_re, b_im, c_re, c_im, d_skip, w_glu, nb, tc):
    rows, d_ssm = u_tb.shape
    groups = d_ssm // SSM_GROUP
    n_state = groups * SSM_STATE
    lam = lax.complex(a_re, a_im)
    dt = jnp.exp(log_dt)[:, None]
    a_bar = jnp.exp(lam * dt)
    b_bar = ((a_bar - 1.0) / lam)[..., None] * lax.complex(b_re, b_im)
    gh = min(groups, 256 // SSM_GROUP)
    n_half = groups // gh
    eye = jnp.eye(gh, dtype=F32)

    def bmat(bb):
        b4 = bb.reshape(n_half, gh, SSM_STATE, SSM_GROUP)
        return jnp.einsum('hgpc,gk->hgckp', b4, eye).reshape(n_half, gh * SSM_GROUP, gh * SSM_STATE)

    def cmat(cc):
        c4 = cc.reshape(n_half, gh, SSM_GROUP, SSM_STATE)
        return jnp.einsum('hgcp,gk->hgpkc', c4, eye).reshape(n_half, gh * SSM_STATE, gh * SSM_GROUP)

    bre = bmat(jnp.real(b_bar)).astype(BF16)
    bim = bmat(jnp.imag(b_bar)).astype(BF16)
    cre = cmat(c_re).astype(BF16)
    cim = cmat(c_im).astype(BF16)
    are = jnp.broadcast_to(jnp.real(a_bar).reshape(1, n_state), (nb, n_state))
    aim = jnp.broadcast_to(jnp.imag(a_bar).reshape(1, n_state), (nb, n_state))
    blk = tc * nb
    full = lambda a: pl.BlockSpec(a.shape, lambda i: (0,) * a.ndim)
    dsk = d_skip.reshape(1, d_ssm)
    wg = w_glu.astype(BF16)
    kern = functools.partial(_s5_kernel, tc=tc, nb=nb, lane_chunk=512)
    return pl.pallas_call(
        kern,
        grid=(rows // blk,),
        in_specs=[pl.BlockSpec((blk, d_ssm), lambda i: (i, 0)),
                  full(bre), full(bim), full(cre), full(cim), full(are), full(aim), full(dsk), full(wg)],
        out_specs=pl.BlockSpec((blk, d_ssm), lambda i: (i, 0)),
        out_shape=jax.ShapeDtypeStruct((rows, d_ssm), BF16),
        scratch_shapes=[pltpu.VMEM((blk, n_state), F32), pltpu.VMEM((blk, n_state), F32),
                        pltpu.VMEM((nb, n_state), F32), pltpu.VMEM((nb, n_state), F32)],
        compiler_params=_cparams(("arbitrary",)),
        name="s5",
    )(u_tb, bre, bim, cre, cim, are, aim, dsk, wg)


PART_ROWS = 32


def _dsa_kernel(qi_ref, wit_ref, q_ref, ki_ref, k_ref, vt_ref, o_ref, key_s, bias_s, lg_s,
                *, tq, kt, sub, topk, seq_bits):
    qb = pl.program_id(1)
    nkt = ((qb * tq + tq + sub * kt - 1) // (sub * kt)) * sub
    q_pos = qb * tq + lax.broadcasted_iota(I32, (1, tq), 1)
    k_eff = jnp.minimum(topk, q_pos + 1).astype(F32)

    qi = qi_ref[0]
    wit = wit_ref[0]
    qih = [qi[:, h * IDX_DIM:(h + 1) * IDX_DIM] for h in range(IDX_HEADS)]

    def key_pos(t):
        return t * kt + lax.broadcasted_iota(I32, (kt, tq), 0)

    def score_tile(t, _):
        r0 = pl.multiple_of(t * kt, kt)
        ki_t = ki_ref[0, pl.ds(r0, kt), :]
        sc = jnp.zeros((kt, tq), F32)
        for h in range(IDX_HEADS):
            rel = lax.dot_general(ki_t, qih[h], (((1,), (1,)), ((), ())), preferred_element_type=F32)
            sc = sc + jnp.maximum(rel, 0.0) * wit[h:h + 1, :]
        bits = lax.bitcast_convert_type(sc, I32)
        key = jnp.where(bits < 0, bits ^ jnp.int32(0x7FFFFFFF), bits)
        key = jnp.where(key_pos(t) <= q_pos, key, jnp.int32(INT_MIN))
        key_s[pl.ds(r0, kt), :] = key
        return 0

    lax.fori_loop(0, nkt, score_tile, 0)

    def count(pred_fn):
        def body(t, acc):
            r0 = pl.multiple_of(t * kt, kt)
            m = pred_fn(key_s[pl.ds(r0, kt), :], t)
            ones = jnp.where(m, 1.0, 0.0).reshape(kt // PART_ROWS, PART_ROWS, tq)
            return acc + jnp.sum(ones, axis=0)
        acc = lax.fori_loop(0, nkt, body, jnp.zeros((PART_ROWS, tq), F32))
        return jnp.sum(acc, axis=0, keepdims=True)

    def bit_step(i, u):
        bit = jnp.left_shift(jnp.int32(1), 31 - i)
        cand_u = u | bit
        cand_s = cand_u ^ jnp.int32(INT_MIN)
        cnt = count(lambda kk, t: kk >= cand_s)
        return jnp.where(cnt >= k_eff, cand_u, u)

    u_thr = lax.fori_loop(0, 32, bit_step, jnp.zeros((1, tq), I32))
    thr = u_thr ^ jnp.int32(INT_MIN)

    cnt_ge = count(lambda kk, t: kk >= thr)
    cnt_gt = count(lambda kk, t: kk > thr)
    need_eq = k_eff - cnt_gt
    has_tie = jnp.max(cnt_ge - k_eff) > 0.0

    def tie_cut():
        def pos_step(i, c):
            bit = jnp.left_shift(jnp.int32(1), seq_bits - 1 - i)
            cand = c | bit
            cnt = count(lambda kk, t: (kk == thr) & (key_pos(t) < cand))
            return jnp.where(cnt < need_eq, cand, c)
        return lax.fori_loop(0, seq_bits, pos_step, jnp.zeros((1, tq), I32))

    cut = lax.cond(has_tie, tie_cut, lambda: jnp.full((1, tq), 2 ** seq_bits, I32))

    def bias_tile(t, _):
        r0 = pl.multiple_of(t * kt, kt)
        key = key_s[pl.ds(r0, kt), :]
        sel = (key > thr) | ((key == thr) & (key_pos(t) <= cut))
        bias_s[pl.ds(r0, kt), :] = jnp.where(sel, 0.0, NEG_BIG)
        return 0

    lax.fori_loop(0, nkt, bias_tile, 0)

    q = q_ref[0]
    grp = ATTN_HEADS // ATTN_KV_HEADS
    pairs_per_kv = grp // 2
    n_unit = ATTN_KV_HEADS * pairs_per_kv
    wq = 2 * tq
    qpair = [jnp.concatenate([q[:, (2 * u) * HEAD_DIM:(2 * u + 1) * HEAD_DIM],
                              q[:, (2 * u + 1) * HEAD_DIM:(2 * u + 2) * HEAD_DIM]], axis=0)
             for u in range(n_unit)]

    def col_reduce(x, op):
        part = op(x.reshape(kt // PART_ROWS, PART_ROWS, wq), axis=0)
        return op(part, axis=0, keepdims=True)

    def attn_tile(t, carry):
        ms, ls, accs = list(carry[0]), list(carry[1]), list(carry[2])
        for hf in range(sub):
            r0 = pl.multiple_of((t * sub + hf) * kt, kt)
            bias = bias_s[pl.ds(r0, kt), :]
            bias2 = jnp.concatenate([bias, bias], axis=1)
            for u in range(n_unit):
                k_t = k_ref[0, u // pairs_per_kv, pl.ds(r0, kt), :]
                lg_s[hf, u] = lax.dot_general(k_t, qpair[u], (((1,), (1,)), ((), ())),
                                              preferred_element_type=F32) + bias2
        for hf in range(sub):
            for u in range(n_unit):
                lg = lg_s[hf, u]
                m_new = jnp.maximum(ms[u], col_reduce(lg, jnp.max))
                p = jnp.exp2(lg - m_new)
                alpha = jnp.exp2(ms[u] - m_new)
                v_t = vt_ref[0, u // pairs_per_kv, t * sub + hf]
                ls[u] = alpha * ls[u] + col_reduce(p, jnp.sum)
                accs[u] = alpha * accs[u] + jnp.dot(v_t, p.astype(BF16), preferred_element_type=F32)
                ms[u] = m_new
        return tuple(ms), tuple(ls), tuple(accs)

    init = (tuple(jnp.full((1, wq), NEG_BIG, F32) for _ in range(n_unit)),
            tuple(jnp.zeros((1, wq), F32) for _ in range(n_unit)),
            tuple(jnp.zeros((HEAD_DIM, wq), F32) for _ in range(n_unit)))
    _, ls, accs = lax.fori_loop(0, nkt // sub, attn_tile, init)
    for n in range(ATTN_KV_HEADS):
        o_ref[0, 0, n] = jnp.concatenate([accs[n * pairs_per_kv + pg] / ls[n * pairs_per_kv + pg]
                                          for pg in range(pairs_per_kv)], axis=1).astype(BF16)


def _dsa_prep(k, v, wi, kt):
    bsz, seq, _ = k.shape
    wit = wi.transpose(0, 2, 1)
    k4 = k.reshape(bsz, seq, ATTN_KV_HEADS, HEAD_DIM).transpose(0, 2, 1, 3)
    vt = v.reshape(bsz, seq // kt, kt, ATTN_KV_HEADS, HEAD_DIM).transpose(0, 3, 1, 4, 2)
    return wit, k4, vt


def _dsa(q, qi, ki, wit, k4, vt, b0, nb, tq, kt):
    _, seq, _ = q.shape
    topk = min(TOPK_MAX, seq // 4)
    nqb = seq // tq
    grp = ATTN_HEADS // ATTN_KV_HEADS
    seq_bits = int(math.log2(seq))
    assert 2 ** seq_bits == seq
    sub = 2 if seq % (2 * kt) == 0 else 1
    n_unit = ATTN_HEADS // 2
    kern = functools.partial(_dsa_kernel, tq=tq, kt=kt, sub=sub, topk=topk, seq_bits=seq_bits)
    o_t = pl.pallas_call(
        kern,
        grid=(nb, nqb),
        in_specs=[pl.BlockSpec((1, tq, IDX_HEADS * IDX_DIM), lambda b, j: (b + b0, j, 0)),
                  pl.BlockSpec((1, IDX_HEADS, tq), lambda b, j: (b + b0, 0, j)),
                  pl.BlockSpec((1, tq, ATTN_HEADS * HEAD_DIM), lambda b, j: (b + b0, j, 0)),
                  pl.BlockSpec((1, seq, IDX_DIM), lambda b, j: (b + b0, 0, 0)),
                  pl.BlockSpec((1, ATTN_KV_HEADS, seq, HEAD_DIM), lambda b, j: (b + b0, 0, 0, 0)),
                  pl.BlockSpec((1, ATTN_KV_HEADS, seq // kt, HEAD_DIM, kt), lambda b, j: (b + b0, 0, 0, 0, 0))],
        out_specs=pl.BlockSpec((1, 1, ATTN_KV_HEADS, HEAD_DIM, grp * tq), lambda b, j: (b, j, 0, 0, 0)),
        out_shape=jax.ShapeDtypeStruct((nb, nqb, ATTN_KV_HEADS, HEAD_DIM, grp * tq), BF16),
        scratch_shapes=[pltpu.VMEM((seq, tq), I32), pltpu.VMEM((seq, tq), F32),
                        pltpu.VMEM((sub, n_unit, kt, 2 * tq), F32)],
        compiler_params=_cparams(("arbitrary", "arbitrary")),
        name="dsa",
    )(qi, wit, q, ki, k4, vt)
    o = o_t.reshape(nb, nqb, ATTN_KV_HEADS, HEAD_DIM, grp, tq).transpose(0, 1, 5, 2, 4, 3)
    return o.reshape(nb * seq, ATTN_HEADS * HEAD_DIM)


def _merge_kernel(x_ref, ys_ref, ya_ref, gs_ref, ga_ref, wsu_ref, wau_ref, wout_ref, g2_ref, wq_ref,
                  h_ref, hn_ref, qp_ref):
    ms = jnp.dot(ys_ref[...], wsu_ref[...], preferred_element_type=F32)
    ma = jnp.dot(ya_ref[...], wau_ref[...], preferred_element_type=F32)
    merged = gs_ref[...].astype(F32) * ms + ga_ref[...].astype(F32) * ma
    h = x_ref[...] + jnp.dot(merged.astype(BF16), wout_ref[...], preferred_element_type=F32)
    h_ref[...] = h
    hn = h * lax.rsqrt(jnp.mean(h * h, axis=-1, keepdims=True) + NORM_EPS) * g2_ref[...]
    hb = hn.astype(BF16)
    bits = lax.bitcast_convert_type(hb.astype(F32), I32)
    half = bits.shape[1] // 2
    hn_ref[...] = (bits[:, half:] & jnp.int32(-65536)) | lax.shift_right_logical(bits[:, :half], 16)
    qp_ref[...] = jnp.dot(hb, wq_ref[...], preferred_element_type=F32).astype(BF16)


def _merge(x2, ys, ya, gs, ga, wsu, wau, wo, norm2_g, wq, r0, tm):
    n = ya.shape[0]
    d = x2.shape[1]
    i0 = r0 // tm
    row = lambda a: pl.BlockSpec((tm, a.shape[1]), lambda i: (i + i0, 0))
    loc = lambda a: pl.BlockSpec((tm, a.shape[1]), lambda i: (i, 0))
    full = lambda a: pl.BlockSpec(a.shape, lambda i: (0,) * a.ndim)
    g2 = norm2_g.reshape(1, d)
    nq = wq.shape[1]
    return pl.pallas_call(
        _merge_kernel,
        grid=(n // tm,),
        in_specs=[row(x2), row(ys), loc(ya), row(gs), row(ga), full(wsu), full(wau), full(wo), full(g2), full(wq)],
        out_specs=[pl.BlockSpec((tm, d), lambda i: (i, 0)), pl.BlockSpec((tm, d // 2), lambda i: (i, 0)),
                   pl.BlockSpec((tm, nq), lambda i: (i, 0))],
        out_shape=[jax.ShapeDtypeStruct((n, d), F32), jax.ShapeDtypeStruct((n, d // 2), I32),
                   jax.ShapeDtypeStruct((n, nq), BF16)],
        compiler_params=_cparams(("arbitrary",)),
        name="merge",
    )(x2, ys, ya, gs, ga, wsu, wau, wo, g2, wq)


def _cand_layout():
    blocks = []
    blocks.append((0, 16, 16))
    for i in range(1, 8):
        blocks.append((i, 8, PEER_TOPK // (i + 1)))
    blocks.append((None, 8, 8))
    return blocks


def _top_rows(s, order, payload, k):
    big = jnp.float32(3e38)
    vals, pays = [], []
    for _ in range(k):
        m = jnp.max(s, axis=0, keepdims=True)
        o = jnp.min(jnp.where(s == m, order, big), axis=0, keepdims=True)
        hit = order == o
        pays.append(jnp.min(jnp.where(hit, payload, big), axis=0, keepdims=True))
        vals.append(m)
        s = jnp.where(hit, -jnp.inf, s)
    return jnp.concatenate(vals, axis=0), jnp.concatenate(pays, axis=0)


def _route_kernel(qp_ref, k1_ref, k2_ref, e_ref, g_ref, *, tt):
    qp = qp_ref[...]
    kd = PEER_KEY_DIM
    rows_k = lax.broadcasted_iota(I32, (PEER_KEYS, tt), 0).astype(F32)
    for h in range(PEER_HEADS):
        q1 = qp[:, (2 * h) * kd:(2 * h + 1) * kd]
        q2 = qp[:, (2 * h + 1) * kd:(2 * h + 2) * kd]
        s1 = lax.dot_general(k1_ref[h], q1, (((1,), (1,)), ((), ())), preferred_element_type=F32)
        s2 = lax.dot_general(k2_ref[h], q2, (((1,), (1,)), ((), ())), preferred_element_type=F32)
        v1, i1 = _top_rows(s1, rows_k, rows_k, PEER_TOPK)
        v2, i2 = _top_rows(s2, rows_k, rows_k, PEER_TOPK)
        cs, ce, co = [], [], []
        for i, rows, valid in _cand_layout():
            r = lax.broadcasted_iota(I32, (rows, tt), 0).astype(F32)
            if i is None:
                val = v1[8:16] + v2[0:1]
                eid = i1[8:16] * PEER_KEYS + i2[0:1]
                flat = (r + 8.0) * PEER_TOPK
            else:
                val = v1[i:i + 1] + v2[0:rows]
                eid = i1[i:i + 1] * PEER_KEYS + i2[0:rows]
                flat = r + float(i * PEER_TOPK)
                if valid < rows:
                    val = jnp.where(r < float(valid), val, -jnp.inf)
            cs.append(val)
            ce.append(eid)
            co.append(flat)
        cand = jnp.concatenate(cs, axis=0)
        top_s, top_e = _top_rows(cand, jnp.concatenate(co, axis=0), jnp.concatenate(ce, axis=0), PEER_TOPK)
        p = jnp.exp(top_s - top_s[0:1])
        gates = p / jnp.sum(p, axis=0, keepdims=True)
        e_ref[h * PEER_TOPK:(h + 1) * PEER_TOPK, :] = top_e.astype(I32)
        g_ref[h * PEER_TOPK:(h + 1) * PEER_TOPK, :] = gates


def _route(qp, peer_k1, peer_k2, tt):
    n, nq = qp.shape
    k1 = peer_k1.astype(BF16)
    k2 = peer_k2.astype(BF16)
    n_sel = PEER_HEADS * PEER_TOPK
    full = lambda a: pl.BlockSpec(a.shape, lambda i: (0,) * a.ndim)
    return pl.pallas_call(
        functools.partial(_route_kernel, tt=tt),
        grid=(n // tt,),
        in_specs=[pl.BlockSpec((tt, nq), lambda i: (i, 0)), full(k1), full(k2)],
        out_specs=[pl.BlockSpec((n_sel, tt), lambda i: (0, i)), pl.BlockSpec((n_sel, tt), lambda i: (0, i))],
        out_shape=[jax.ShapeDtypeStruct((n_sel, n), I32), jax.ShapeDtypeStruct((n_sel, n), F32)],
        compiler_params=_cparams(("arbitrary",)),
        name="route",
    )(qp, k1, k2)


def _final_kernel(h_ref, p_ref, g_ref, o_ref):
    h = h_ref[...] + p_ref[...]
    o_ref[...] = h * lax.rsqrt(jnp.mean(h * h, axis=-1, keepdims=True) + NORM_EPS) * g_ref[...]


def _final(h, p, g, tm):
    n, d = h.shape
    row = pl.BlockSpec((tm, d), lambda i: (i, 0))
    return pl.pallas_call(
        _final_kernel,
        grid=(n // tm,),
        in_specs=[row, row, pl.BlockSpec((1, d), lambda i: (0, 0))],
        out_specs=row,
        out_shape=jax.ShapeDtypeStruct((n, d), F32),
        compiler_params=_cparams(("arbitrary",)),
        name="final",
    )(h, p, g.reshape(1, d))


SC_CORES_V7X = 2
SC_SUBCORES_V7X = 16
SC_LANES_V7X = 16
PEER_TOK_BATCH = 16
PEER_ROW_CHUNK = 64


def _pack_bf16_pairs(t):
    half = t.shape[1] // 2
    tb = t.astype(BF16)
    lo = lax.bitcast_convert_type(tb[:, :half], jnp.uint16).astype(jnp.uint32)
    hi = lax.bitcast_convert_type(tb[:, half:], jnp.uint16).astype(jnp.uint32)
    return lax.bitcast_convert_type(lo | (hi << 16), I32)


def _unpack_pair(w):
    lo = lax.bitcast_convert_type(jnp.left_shift(w, 16), F32)
    hi = lax.bitcast_convert_type(w & jnp.int32(-65536), F32)
    return lo, hi


def _peer_sc_body(hn_hbm, e_hbm, g_hbm, u_hbm, v_hbm, out_hbm,
                  idx_v, gate_v, x_v, out_v, rows, p_v, act_v, sem, *, tpw, d, n_sel):
    nl = SC_LANES_V7X
    tb = PEER_TOK_BATCH
    rc = PEER_ROW_CHUNK
    n_chunk = n_sel // rc
    jobs_per_tok = 2 * n_chunk
    half = d // 2
    n_lane_blk = half // nl
    wid = lax.axis_index("s") * SC_CORES_V7X + lax.axis_index("c")
    base = wid * tpw
    lane = lax.iota(I32, nl)
    zero = jnp.zeros((nl,), F32)
    c_gelu = 2.0 * math.sqrt(2.0 / math.pi)

    def gather_copy(tab_hbm, job):
        tok = job // jobs_per_tok
        c = (job % jobs_per_tok) % n_chunk
        b = job % 2
        return pltpu.make_async_copy(tab_hbm.at[idx_v.at[tok, pl.ds(c * rc, rc)]], rows.at[b], sem.at[b])

    def start(job):
        j = job % jobs_per_tok

        @pl.when(j < n_chunk)
        def _():
            gather_copy(u_hbm, job).start()

        @pl.when(j >= n_chunk)
        def _():
            gather_copy(v_hbm, job).start()

    def compute_u(tok, c, b):
        def rg_body(rg, _):
            r0 = rg * 8

            def jbody(j2, accs):
                off0 = j2 * (2 * nl)
                off1 = off0 + nl
                x0 = plsc.bitcast(x_v[tok, pl.ds(off0, nl)], BF16)
                x1 = plsc.bitcast(x_v[tok, pl.ds(off1, nl)], BF16)
                new = []
                for r in range(8):
                    w0 = plsc.bitcast(rows[b, r0 + r, pl.ds(off0, nl)], BF16)
                    w1 = plsc.bitcast(rows[b, r0 + r, pl.ds(off1, nl)], BF16)
                    lo, hi = _unpack_pair(plsc.bitcast(w0 * x0 + w1 * x1, I32))
                    new.append(accs[r] + (lo + hi))
                return tuple(new)

            accs = lax.fori_loop(0, n_lane_blk // 2, jbody, (zero,) * 8)
            for r in range(8):
                p_v[c * rc + r0 + r, :] = accs[r]
            return 0

        lax.fori_loop(0, rc // 8, rg_body, 0)

    def finish_act(tok):
        def eg_body(eg, _):
            e0 = eg * nl
            ridx = e0 + lane
            s = zero
            for l in range(nl):
                s = s + plsc.load_gather(p_v, [ridx, jnp.full((nl,), l, I32)])
            inner = c_gelu * (s + 0.044715 * (s * s * s))
            gl = s / (1.0 + jnp.exp(-inner))
            a = gl * gate_v[tok, pl.ds(e0, nl)]
            bits = lax.bitcast_convert_type(a, I32)
            rnd = bits + jnp.int32(0x7FFF) + (lax.shift_right_logical(bits, 16) & 1)
            hi16 = rnd & jnp.int32(-65536)
            act_v[pl.ds(e0, nl)] = hi16 | lax.shift_right_logical(hi16, 16)
            return 0

        lax.fori_loop(0, n_sel // nl, eg_body, 0)

        def zbody(j, _):
            out_v[tok, pl.ds(j * nl, nl)] = zero
            return 0

        lax.fori_loop(0, d // nl, zbody, 0, unroll=4)

    def compute_v(tok, c, b):
        def rg_body(rg, _):
            r0 = rg * nl
            splat = [plsc.bitcast(plsc.load_gather(act_v, [jnp.full((nl,), 0, I32) + (c * rc + r0 + r)]), BF16)
                     for r in range(nl)]

            def tree(parts):
                while len(parts) > 1:
                    parts = [parts[i] + parts[i + 1] for i in range(0, len(parts), 2)]
                return parts[0]

            @plsc.parallel_loop(0, n_lane_blk, unroll=2)
            def _(j):
                off = j * nl
                los, his = [], []
                for r in range(0, nl, 2):
                    w0 = plsc.bitcast(rows[b, r0 + r, pl.ds(off, nl)], BF16)
                    w1 = plsc.bitcast(rows[b, r0 + r + 1, pl.ds(off, nl)], BF16)
                    lo, hi = _unpack_pair(plsc.bitcast(w0 * splat[r] + w1 * splat[r + 1], I32))
                    los.append(lo)
                    his.append(hi)
                out_v[tok, pl.ds(off, nl)] = out_v[tok, pl.ds(off, nl)] + tree(los)
                out_v[tok, pl.ds(half + off, nl)] = out_v[tok, pl.ds(half + off, nl)] + tree(his)

            return 0

        lax.fori_loop(0, rc // nl, rg_body, 0)

    def batch_body(bi, _):
        t0 = base + bi * tb
        pltpu.sync_copy(e_hbm.at[pl.ds(t0, tb)], idx_v)
        pltpu.sync_copy(g_hbm.at[pl.ds(t0, tb)], gate_v)
        pltpu.sync_copy(hn_hbm.at[pl.ds(t0, tb)], x_v)
        start(0)

        def job_body(job, _):
            @pl.when(job + 1 < tb * jobs_per_tok)
            def _():
                start(job + 1)

            j = job % jobs_per_tok
            gather_copy(u_hbm, job).wait()
            tok = job // jobs_per_tok
            b = job % 2

            @pl.when(j < n_chunk)
            def _():
                compute_u(tok, j, b)

            @pl.when(j == n_chunk - 1)
            def _():
                finish_act(tok)

            @pl.when(j >= n_chunk)
            def _():
                compute_v(tok, j - n_chunk, b)

            return 0

        lax.fori_loop(0, tb * jobs_per_tok, job_body, 0)
        pltpu.sync_copy(out_v, out_hbm.at[pl.ds(t0, tb)])
        return 0

    lax.fori_loop(0, tpw // tb, batch_body, 0)


def _peer_sc(x_pk, experts, gates, u_tab, v_tab):
    n = x_pk.shape[0]
    d = 2 * x_pk.shape[1]
    n_sel = experts.shape[1]
    nw = SC_CORES_V7X * SC_SUBCORES_V7X
    tpw = n // nw
    mesh = plsc.VectorSubcoreMesh(core_axis_name="c", subcore_axis_name="s",
                                  num_cores=SC_CORES_V7X, num_subcores=SC_SUBCORES_V7X)
    body = functools.partial(_peer_sc_body, tpw=tpw, d=d, n_sel=n_sel)
    call = pl.kernel(
        body,
        out_type=jax.ShapeDtypeStruct((n, d), F32),
        mesh=mesh,
        scratch_types=[pltpu.VMEM((PEER_TOK_BATCH, n_sel), I32),
                       pltpu.VMEM((PEER_TOK_BATCH, n_sel), F32),
                       pltpu.VMEM((PEER_TOK_BATCH, d // 2), I32),
                       pltpu.VMEM((PEER_TOK_BATCH, d), F32),
                       pltpu.VMEM((2, PEER_ROW_CHUNK, d // 2), I32),
                       pltpu.VMEM((n_sel, SC_LANES_V7X), F32),
                       pltpu.VMEM((n_sel,), I32),
                       pltpu.SemaphoreType.DMA((2,))],
        compiler_params=pltpu.CompilerParams(needs_layout_passes=False),
        name="peer_sc",
    )
    return call(x_pk, experts, gates, u_tab, v_tab)


def kernel(x, norm1_g, w_in, a_re, a_im, log_dt, b_re, b_im, c_re, c_im, d_skip, w_glu, w_ssm_up, w_attn_up,
           w_out, norm2_g, peer_wq, peer_k1, peer_k2, peer_u, peer_v, norm_f_g):
    bsz, seq, d = x.shape
    n = bsz * seq
    h = x
    depth = norm1_g.shape[0]
    n_chunks = BATCH_CHUNKS if bsz % BATCH_CHUNKS == 0 else 1
    nb = bsz // n_chunks
    for layer in range(depth):
        last = layer + 1 == depth
        u, q, k, v, qi, ki, wi, gs, ga = _inproj(h, norm1_g[layer], w_in[layer], tm=min(512, seq))
        d_ssm = u.shape[-1]
        u_tb = u.transpose(1, 0, 2).reshape(n, d_ssm)
        y_tb = _s5(u_tb, a_re[layer], a_im[layer], log_dt[layer], b_re[layer], b_im[layer], c_re[layer],
                   c_im[layer], d_skip[layer], w_glu[layer], nb=bsz, tc=64)
        ys = y_tb.reshape(seq, bsz, d_ssm).transpose(1, 0, 2).reshape(n, d_ssm)
        wit, k4, vt = _dsa_prep(k, v, wi, kt=256)
        wsu = w_ssm_up[layer].astype(BF16)
        wau = w_attn_up[layer].astype(BF16)
        wo = w_out[layer].astype(BF16)
        wq = peer_wq[layer].astype(BF16)
        u_pk = _pack_bf16_pairs(peer_u[layer])
        v_pk = _pack_bf16_pairs(peer_v[layer])
        h2 = h.reshape(n, d)
        outs = []
        for c in range(n_chunks):
            ya = _dsa(q, qi, ki, wit, k4, vt, b0=c * nb, nb=nb, tq=128, kt=256)
            hm, hn, qp = _merge(h2, ys, ya, gs.reshape(n, d), ga.reshape(n, d), wsu, wau, wo, norm2_g[layer], wq,
                                r0=c * nb * seq, tm=512)
            e_t, g_t = _route(qp, peer_k1[layer], peer_k2[layer], tt=256)
            po = _peer_sc(hn, e_t.T, g_t.T, u_pk, v_pk)
            outs.append(_final(hm, po, norm_f_g, tm=512) if last else hm + po)
        h = jnp.concatenate(outs, axis=0).reshape(bsz, seq, d)
    return h
```

```python
import functools
import math

import numpy as np
import jax
import jax.numpy as jnp
from jax import lax
from jax.experimental import pallas as pl
from jax.experimental.pallas import tpu as pltpu
from jax.experimental.pallas import tpu_sc as plsc

F32 = jnp.float32
BF16 = jnp.bfloat16
I32 = jnp.int32

SSM_GROUP = 16
SSM_STATE = 64
ATTN_HEADS = 8
ATTN_KV_HEADS = 2
HEAD_DIM = 64
IDX_HEADS = 8
IDX_DIM = 32
TOPK_MAX = 256
ROPE_THETA = 10000.0
NEG_BIG = -1e30
PEER_HEADS = 8
PEER_KEYS = 128
PEER_KEY_DIM = 128
PEER_TOPK = 16
NORM_EPS = 1e-6

BATCH_CHUNKS = 8
INT_MIN = -(2 ** 31)
VMEM_LIMIT = 56 * 1024 * 1024


def _cparams(sem):
    return pltpu.CompilerParams(dimension_semantics=sem, vmem_limit_bytes=VMEM_LIMIT)


def _gelu_tanh(x):
    return 0.5 * x * (1.0 + jnp.tanh(math.sqrt(2.0 / math.pi) * (x + 0.044715 * (x * x * x))))


def _sigmoid(x):
    return 1.0 / (1.0 + jnp.exp(-x))


def _rot_cols(w, hd):
    d, n = w.shape
    w3 = w.reshape(d, n // hd, hd)
    half = hd // 2
    return jnp.concatenate([-w3[..., half:], w3[..., :half]], axis=-1).reshape(d, n)


def _rope_full(seq, hd, heads):
    pos = jnp.arange(seq, dtype=F32)
    inv = ROPE_THETA ** (-jnp.arange(0, hd, 2, dtype=F32) / hd)
    ang = pos[:, None] * inv[None, :]
    c = jnp.concatenate([jnp.cos(ang), jnp.cos(ang)], axis=-1)
    s = jnp.concatenate([jnp.sin(ang), jnp.sin(ang)], axis=-1)
    return jnp.tile(c, (1, heads)), jnp.tile(s, (1, heads))


def _inproj_kernel(x_ref, g_ref, w_ref, wr_ref, cs_ref, sn_ref,
                   u_ref, q_ref, k_ref, v_ref, qi_ref, ki_ref, wi_ref, gs_ref, ga_ref,
                   *, d_ssm, d_q, d_kv, d_qi, d_ki, n_wi, d_model, q_scale, wi_scale):
    x = x_ref[0]
    xn = x * lax.rsqrt(jnp.mean(x * x, axis=-1, keepdims=True) + NORM_EPS) * g_ref[...]
    xb = xn.astype(BF16)

    def mm(ref, lo, n):
        return jnp.dot(xb, ref[:, lo:lo + n], preferred_element_type=F32)

    o = 0
    u_ref[0] = mm(w_ref, o, d_ssm).astype(BF16)
    o += d_ssm
    ro = 0
    q = mm(w_ref, o, d_q) * cs_ref[:, ro:ro + d_q] + mm(wr_ref, ro, d_q) * sn_ref[:, ro:ro + d_q]
    q_ref[0] = (q * q_scale).astype(BF16)
    o += d_q
    ro += d_q
    k = mm(w_ref, o, d_kv) * cs_ref[:, ro:ro + d_kv] + mm(wr_ref, ro, d_kv) * sn_ref[:, ro:ro + d_kv]
    k_ref[0] = k.astype(BF16)
    o += d_kv
    ro += d_kv
    v_ref[0] = mm(w_ref, o, d_kv).astype(BF16)
    o += d_kv
    qi = mm(w_ref, o, d_qi) * cs_ref[:, ro:ro + d_qi] + mm(wr_ref, ro, d_qi) * sn_ref[:, ro:ro + d_qi]
    qi_ref[0] = qi.astype(BF16)
    o += d_qi
    ro += d_qi
    seg = mm(w_ref, o, 128)
    segr = mm(wr_ref, ro, 128)
    kiw = seg * cs_ref[:, ro:ro + 128] + segr * sn_ref[:, ro:ro + 128]
    ki_ref[0] = kiw[:, :d_ki].astype(BF16)
    wi_ref[0] = seg[:, d_ki:d_ki + n_wi] * wi_scale
    o += 128
    gs_ref[0] = _sigmoid(mm(w_ref, o, d_model)).astype(BF16)
    o += d_model
    ga_ref[0] = _sigmoid(mm(w_ref, o, d_model)).astype(BF16)


def _inproj(x, norm_g, w_in, tm):
    bsz, seq, d = x.shape
    d_ssm = d // 2
    d_q = ATTN_HEADS * HEAD_DIM
    d_kv = ATTN_KV_HEADS * HEAD_DIM
    d_qi = IDX_HEADS * IDX_DIM
    d_ki = IDX_DIM
    n_wi = IDX_HEADS
    splits = (d_ssm, d_q, d_kv, d_kv, d_qi, d_ki, n_wi, d, d)
    offs = np.cumsum(splits)[:-1].tolist()
    wu, wq, wk, wv, wqi, wki, wwi, wgs, wga = jnp.split(w_in, offs, axis=1)
    pad = jnp.zeros((d, 128 - d_ki - n_wi), F32)
    w_main = jnp.concatenate([wu, wq, wk, wv, wqi, wki, wwi, pad, wgs, wga], axis=1).astype(BF16)
    rpad = jnp.zeros((d, 128 - d_ki), F32)
    w_rot = jnp.concatenate([_rot_cols(wq, HEAD_DIM), _rot_cols(wk, HEAD_DIM),
                             _rot_cols(wqi, IDX_DIM), _rot_cols(wki, IDX_DIM), rpad], axis=1).astype(BF16)
    cq, sq = _rope_full(seq, HEAD_DIM, ATTN_HEADS)
    ck, sk = _rope_full(seq, HEAD_DIM, ATTN_KV_HEADS)
    cqi, sqi = _rope_full(seq, IDX_DIM, IDX_HEADS)
    cki, ski = _rope_full(seq, IDX_DIM, 1)
    tpad = jnp.zeros((seq, 128 - d_ki), F32)
    cs = jnp.concatenate([cq, ck, cqi, cki, tpad], axis=1)
    sn = jnp.concatenate([sq, sk, sqi, ski, tpad], axis=1)
    n_main = w_main.shape[1]
    n_rot = w_rot.shape[1]

    kern = functools.partial(
        _inproj_kernel, d_ssm=d_ssm, d_q=d_q, d_kv=d_kv, d_qi=d_qi, d_ki=d_ki, n_wi=n_wi, d_model=d,
        q_scale=HEAD_DIM ** -0.5 * math.log2(math.e), wi_scale=(IDX_HEADS ** -0.5) * (IDX_DIM ** -0.5))
    tok = lambda n: pl.BlockSpec((1, tm, n), lambda s, b: (b, s, 0))
    full = lambda shape: pl.BlockSpec(shape, lambda s, b: (0,) * len(shape))
    outs = [(d_ssm, BF16), (d_q, BF16), (d_kv, BF16), (d_kv, BF16), (d_qi, BF16), (d_ki, BF16),
            (n_wi, F32), (d, BF16), (d, BF16)]
    return pl.pallas_call(
        kern,
        grid=(seq // tm, bsz),
        in_specs=[tok(d), full((1, d)), full((d, n_main)), full((d, n_rot)),
                  pl.BlockSpec((tm, n_rot), lambda s, b: (s, 0)),
                  pl.BlockSpec((tm, n_rot), lambda s, b: (s, 0))],
        out_specs=[tok(n) for n, _ in outs],
        out_shape=[jax.ShapeDtypeStruct((bsz, seq, n), dt) for n, dt in outs],
        compiler_params=_cparams(("arbitrary", "arbitrary")),
        name="inproj",
    )(x, norm_g.reshape(1, d), w_main, w_rot, cs, sn)


def _s5_kernel(u_ref, bre_ref, bim_ref, cre_ref, cim_ref, are_ref, aim_ref, dsk_ref, wglu_ref,
               y_ref, sre, sim, st_re, st_im, *, tc, nb, lane_chunk):
    @pl.when(pl.program_id(0) == 0)
    def _():
        st_re[...] = jnp.zeros_like(st_re)
        st_im[...] = jnp.zeros_like(st_im)

    u = u_ref[...]
    n_half = bre_ref.shape[0]
    hin = bre_ref.shape[1]
    hst = bre_ref.shape[2]
    for h in range(n_half):
        uh = u[:, h * hin:(h + 1) * hin]
        sre[:, h * hst:(h + 1) * hst] = jnp.dot(uh, bre_ref[h], preferred_element_type=F32)
        sim[:, h * hst:(h + 1) * hst] = jnp.dot(uh, bim_ref[h], preferred_element_type=F32)

    n_state = sre.shape[1]
    for c in range(n_state // lane_chunk):
        cols = slice(c * lane_chunk, (c + 1) * lane_chunk)
        ar = are_ref[:, cols]
        ai = aim_ref[:, cols]

        def step(t, carry, cols=cols, ar=ar, ai=ai):
            sr, si = carry
            r0 = pl.multiple_of(t * nb, nb)
            nr = ar * sr - ai * si + sre[pl.ds(r0, nb), cols]
            ni = ar * si + ai * sr + sim[pl.ds(r0, nb), cols]
            sre[pl.ds(r0, nb), cols] = nr
            sim[pl.ds(r0, nb), cols] = ni
            return nr, ni

        sr, si = lax.fori_loop(0, tc, step, (st_re[:, cols], st_im[:, cols]), unroll=4)
        st_re[:, cols] = sr
        st_im[:, cols] = si

    ys = []
    for h in range(n_half):
        srh = sre[:, h * hst:(h + 1) * hst].astype(BF16)
        sih = sim[:, h * hst:(h + 1) * hst].astype(BF16)
        ys.append(jnp.dot(srh, cre_ref[h], preferred_element_type=F32)
                  - jnp.dot(sih, cim_ref[h], preferred_element_type=F32))
    y = jnp.concatenate(ys, axis=-1) + dsk_ref[...] * u.astype(F32)
    y = _gelu_tanh(y)
    gate = jnp.dot(y.astype(BF16), wglu_ref[...], preferred_element_type=F32)
    y_ref[...] = (y * _sigmoid(gate)).astype(BF16)


def _s5(u_tb, a_re, a_im, log_dt, b_re, b_im, c_re, c_im, d_skip, w_glu, nb, tc):
    rows, d_ssm = u_tb.shape
    groups = d_ssm // SSM_GROUP
    n_state = groups * SSM_STATE
    lam = lax.complex(a_re, a_im)
    dt = jnp.exp(log_dt)[:, None]
    a_bar = jnp.exp(lam * dt)
    b_bar = ((a_bar - 1.0) / lam)[..., None] * lax.complex(b_re, b_im)
    gh = min(groups, 256 // SSM_GROUP)
    n_half = groups // gh
    eye = jnp.eye(gh, dtype=F32)

    def bmat(bb):
        b4 = bb.reshape(n_half, gh, SSM_STATE, SSM_GROUP)
        return jnp.einsum('hgpc,gk->hgckp', b4, eye).reshape(n_half, gh * SSM_GROUP, gh * SSM_STATE)

    def cmat(cc):
        c4 = cc.reshape(n_half, gh, SSM_GROUP, SSM_STATE)
        return jnp.einsum('hgcp,gk->hgpkc', c4, eye).reshape(n_half, gh * SSM_STATE, gh * SSM_GROUP)

    bre = bmat(jnp.real(b_bar)).astype(BF16)
    bim = bmat(jnp.imag(b_bar)).astype(BF16)
    cre = cmat(c_re).astype(BF16)
    cim = cmat(c_im).astype(BF16)
    are = jnp.broadcast_to(jnp.real(a_bar).reshape(1, n_state), (nb, n_state))
    aim = jnp.broadcast_to(jnp.imag(a_bar).reshape(1, n_state), (nb, n_state))
    blk = tc * nb
    full = lambda a: pl.BlockSpec(a.shape, lambda i: (0,) * a.ndim)
    dsk = d_skip.reshape(1, d_ssm)
    wg = w_glu.astype(BF16)
    kern = functools.partial(_s5_kernel, tc=tc, nb=nb, lane_chunk=512)
    return pl.pallas_call(
        kern,
        grid=(rows // blk,),
        in_specs=[pl.BlockSpec((blk, d_ssm), lambda i: (i, 0)),
                  full(bre), full(bim), full(cre), full(cim), full(are), full(aim), full(dsk), full(wg)],
        out_specs=pl.BlockSpec((blk, d_ssm), lambda i: (i, 0)),
        out_shape=jax.ShapeDtypeStruct((rows, d_ssm), BF16),
        scratch_shapes=[pltpu.VMEM((blk, n_state), F32), pltpu.VMEM((blk, n_state), F32),
                        pltpu.VMEM((nb, n_state), F32), pltpu.VMEM((nb, n_state), F32)],
        compiler_params=_cparams(("arbitrary",)),
        name="s5",
    )(u_tb, bre, bim, cre, cim, are, aim, dsk, wg)


PART_ROWS = 32


def _dsa_kernel(qi_ref, wit_ref, q_ref, ki_ref, k_ref, vt_ref, o_ref, key_s, bias_s, lg_s,
                *, tq, kt, sub, topk, seq_bits):
    qb = pl.program_id(1)
    nkt = ((qb * tq + tq + sub * kt - 1) // (sub * kt)) * sub
    q_pos = qb * tq + lax.broadcasted_iota(I32, (1, tq), 1)
    k_eff = jnp.minimum(topk, q_pos + 1).astype(F32)

    qi = qi_ref[0]
    wit = wit_ref[0]
    qih = [qi[:, h * IDX_DIM:(h + 1) * IDX_DIM] for h in range(IDX_HEADS)]

    def key_pos(t):
        return t * kt + lax.broadcasted_iota(I32, (kt, tq), 0)

    def score_tile(t, _):
        r0 = pl.multiple_of(t * kt, kt)
        ki_t = ki_ref[0, pl.ds(r0, kt), :]
        sc = jnp.zeros((kt, tq), F32)
        for h in range(IDX_HEADS):
            rel = lax.dot_general(ki_t, qih[h], (((1,), (1,)), ((), ())), preferred_element_type=F32)
            sc = sc + jnp.maximum(rel, 0.0) * wit[h:h + 1, :]
        bits = lax.bitcast_convert_type(sc, I32)
        key = jnp.where(bits < 0, bits ^ jnp.int32(0x7FFFFFFF), bits)
        key = jnp.where(key_pos(t) <= q_pos, key, jnp.int32(INT_MIN))
        key_s[pl.ds(r0, kt), :] = key
        return 0

    lax.fori_loop(0, nkt, score_tile, 0)

    def count(pred_fn):
        def body(t, acc):
            r0 = pl.multiple_of(t * kt, kt)
            m = pred_fn(key_s[pl.ds(r0, kt), :], t)
            ones = jnp.where(m, 1.0, 0.0).reshape(kt // PART_ROWS, PART_ROWS, tq)
            return acc + jnp.sum(ones, axis=0)
        acc = lax.fori_loop(0, nkt, body, jnp.zeros((PART_ROWS, tq), F32))
        return jnp.sum(acc, axis=0, keepdims=True)

    def bit_step(i, u):
        bit = jnp.left_shift(jnp.int32(1), 31 - i)
        cand_u = u | bit
        cand_s = cand_u ^ jnp.int32(INT_MIN)
        cnt = count(lambda kk, t: kk >= cand_s)
        return jnp.where(cnt >= k_eff, cand_u, u)

    u_thr = lax.fori_loop(0, 32, bit_step, jnp.zeros((1, tq), I32))
    thr = u_thr ^ jnp.int32(INT_MIN)

    cnt_ge = count(lambda kk, t: kk >= thr)
    cnt_gt = count(lambda kk, t: kk > thr)
    need_eq = k_eff - cnt_gt
    has_tie = jnp.max(cnt_ge - k_eff) > 0.0

    def tie_cut():
        def pos_step(i, c):
            bit = jnp.left_shift(jnp.int32(1), seq_bits - 1 - i)
            cand = c | bit
            cnt = count(lambda kk, t: (kk == thr) & (key_pos(t) < cand))
            return jnp.where(cnt < need_eq, cand, c)
        return lax.fori_loop(0, seq_bits, pos_step, jnp.zeros((1, tq), I32))

    cut = lax.cond(has_tie, tie_cut, lambda: jnp.full((1, tq), 2 ** seq_bits, I32))

    def bias_tile(t, _):
        r0 = pl.multiple_of(t * kt, kt)
        key = key_s[pl.ds(r0, kt), :]
        sel = (key > thr) | ((key == thr) & (key_pos(t) <= cut))
        bias_s[pl.ds(r0, kt), :] = jnp.where(sel, 0.0, NEG_BIG)
        return 0

    lax.fori_loop(0, nkt, bias_tile, 0)

    q = q_ref[0]
    grp = ATTN_HEADS // ATTN_KV_HEADS
    pairs_per_kv = grp // 2
    n_unit = ATTN_KV_HEADS * pairs_per_kv
    wq = 2 * tq
    qpair = [jnp.concatenate([q[:, (2 * u) * HEAD_DIM:(2 * u + 1) * HEAD_DIM],
                              q[:, (2 * u + 1) * HEAD_DIM:(2 * u + 2) * HEAD_DIM]], axis=0)
             for u in range(n_unit)]

    def col_reduce(x, op):
        part = op(x.reshape(kt // PART_ROWS, PART_ROWS, wq), axis=0)
        return op(part, axis=0, keepdims=True)

    def attn_tile(t, carry):
        ms, ls, accs = list(carry[0]), list(carry[1]), list(carry[2])
        for hf in range(sub):
            r0 = pl.multiple_of((t * sub + hf) * kt, kt)
            bias = bias_s[pl.ds(r0, kt), :]
            bias2 = jnp.concatenate([bias, bias], axis=1)
            for u in range(n_unit):
                k_t = k_ref[0, u // pairs_per_kv, pl.ds(r0, kt), :]
                lg_s[hf, u] = lax.dot_general(k_t, qpair[u], (((1,), (1,)), ((), ())),
                                              preferred_element_type=F32) + bias2
        for hf in range(sub):
            for u in range(n_unit):
                lg = lg_s[hf, u]
                m_new = jnp.maximum(ms[u], col_reduce(lg, jnp.max))
                p = jnp.exp2(lg - m_new)
                alpha = jnp.exp2(ms[u] - m_new)
                v_t = vt_ref[0, u // pairs_per_kv, t * sub + hf]
                ls[u] = alpha * ls[u] + col_reduce(p, jnp.sum)
                accs[u] = alpha * accs[u] + jnp.dot(v_t, p.astype(BF16), preferred_element_type=F32)
                ms[u] = m_new
        return tuple(ms), tuple(ls), tuple(accs)

    init = (tuple(jnp.full((1, wq), NEG_BIG, F32) for _ in range(n_unit)),
            tuple(jnp.zeros((1, wq), F32) for _ in range(n_unit)),
            tuple(jnp.zeros((HEAD_DIM, wq), F32) for _ in range(n_unit)))
    _, ls, accs = lax.fori_loop(0, nkt // sub, attn_tile, init)
    for n in range(ATTN_KV_HEADS):
        o_ref[0, 0, n] = jnp.concatenate([accs[n * pairs_per_kv + pg] / ls[n * pairs_per_kv + pg]
                                          for pg in range(pairs_per_kv)], axis=1).astype(BF16)


def _dsa_prep(k, v, wi, kt):
    bsz, seq, _ = k.shape
    wit = wi.transpose(0, 2, 1)
    k4 = k.reshape(bsz, seq, ATTN_KV_HEADS, HEAD_DIM).transpose(0, 2, 1, 3)
    vt = v.reshape(bsz, seq // kt, kt, ATTN_KV_HEADS, HEAD_DIM).transpose(0, 3, 1, 4, 2)
    return wit, k4, vt


def _dsa(q, qi, ki, wit, k4, vt, b0, nb, tq, kt):
    _, seq, _ = q.shape
    topk = min(TOPK_MAX, seq // 4)
    nqb = seq // tq
    grp = ATTN_HEADS // ATTN_KV_HEADS
    seq_bits = int(math.log2(seq))
    assert 2 ** seq_bits == seq
    sub = 2 if seq % (2 * kt) == 0 else 1
    n_unit = ATTN_HEADS // 2
    kern = functools.partial(_dsa_kernel, tq=tq, kt=kt, sub=sub, topk=topk, seq_bits=seq_bits)
    o_t = pl.pallas_call(
        kern,
        grid=(nb, nqb),
        in_specs=[pl.BlockSpec((1, tq, IDX_HEADS * IDX_DIM), lambda b, j: (b + b0, j, 0)),
                  pl.BlockSpec((1, IDX_HEADS, tq), lambda b, j: (b + b0, 0, j)),
                  pl.BlockSpec((1, tq, ATTN_HEADS * HEAD_DIM), lambda b, j: (b + b0, j, 0)),
                  pl.BlockSpec((1, seq, IDX_DIM), lambda b, j: (b + b0, 0, 0)),
                  pl.BlockSpec((1, ATTN_KV_HEADS, seq, HEAD_DIM), lambda b, j: (b + b0, 0, 0, 0)),
                  pl.BlockSpec((1, ATTN_KV_HEADS, seq // kt, HEAD_DIM, kt), lambda b, j: (b + b0, 0, 0, 0, 0))],
        out_specs=pl.BlockSpec((1, 1, ATTN_KV_HEADS, HEAD_DIM, grp * tq), lambda b, j: (b, j, 0, 0, 0)),
        out_shape=jax.ShapeDtypeStruct((nb, nqb, ATTN_KV_HEADS, HEAD_DIM, grp * tq), BF16),
        scratch_shapes=[pltpu.VMEM((seq, tq), I32), pltpu.VMEM((seq, tq), F32),
                        pltpu.VMEM((sub, n_unit, kt, 2 * tq), F32)],
        compiler_params=_cparams(("arbitrary", "arbitrary")),
        name="dsa",
    )(qi, wit, q, ki, k4, vt)
    o = o_t.reshape(nb, nqb, ATTN_KV_HEADS, HEAD_DIM, grp, tq).transpose(0, 1, 5, 2, 4, 3)
    return o.reshape(nb * seq, ATTN_HEADS * HEAD_DIM)


def _merge_kernel(x_ref, ys_ref, ya_ref, gs_ref, ga_ref, wsu_ref, wau_ref, wout_ref, g2_ref, wq_ref,
                  h_ref, hn_ref, qp_ref):
    ms = jnp.dot(ys_ref[...], wsu_ref[...], preferred_element_type=F32)
    ma = jnp.dot(ya_ref[...], wau_ref[...], preferred_element_type=F32)
    merged = gs_ref[...].astype(F32) * ms + ga_ref[...].astype(F32) * ma
    h = x_ref[...] + jnp.dot(merged.astype(BF16), wout_ref[...], preferred_element_type=F32)
    h_ref[...] = h
    hn = h * lax.rsqrt(jnp.mean(h * h, axis=-1, keepdims=True) + NORM_EPS) * g2_ref[...]
    hb = hn.astype(BF16)
    bits = lax.bitcast_convert_type(hb.astype(F32), I32)
    half = bits.shape[1] // 2
    hn_ref[...] = (bits[:, half:] & jnp.int32(-65536)) | lax.shift_right_logical(bits[:, :half], 16)
    qp_ref[...] = jnp.dot(hb, wq_ref[...], preferred_element_type=F32).astype(BF16)


def _merge(x2, ys, ya, gs, ga, wsu, wau, wo, norm2_g, wq, r0, tm):
    n = ya.shape[0]
    d = x2.shape[1]
    i0 = r0 // tm
    row = lambda a: pl.BlockSpec((tm, a.shape[1]), lambda i: (i + i0, 0))
    loc = lambda a: pl.BlockSpec((tm, a.shape[1]), lambda i: (i, 0))
    full = lambda a: pl.BlockSpec(a.shape, lambda i: (0,) * a.ndim)
    g2 = norm2_g.reshape(1, d)
    nq = wq.shape[1]
    return pl.pallas_call(
        _merge_kernel,
        grid=(n // tm,),
        in_specs=[row(x2), row(ys), loc(ya), row(gs), row(ga), full(wsu), full(wau), full(wo), full(g2), full(wq)],
        out_specs=[pl.BlockSpec((tm, d), lambda i: (i, 0)), pl.BlockSpec((tm, d // 2), lambda i: (i, 0)),
                   pl.BlockSpec((tm, nq), lambda i: (i, 0))],
        out_shape=[jax.ShapeDtypeStruct((n, d), F32), jax.ShapeDtypeStruct((n, d // 2), I32),
                   jax.ShapeDtypeStruct((n, nq), BF16)],
        compiler_params=_cparams(("arbitrary",)),
        name="merge",
    )(x2, ys, ya, gs, ga, wsu, wau, wo, g2, wq)


def _cand_layout():
    blocks = []
    blocks.append((0, 16, 16))
    for i in range(1, 8):
        blocks.append((i, 8, PEER_TOPK // (i + 1)))
    blocks.append((None, 8, 8))
    return blocks


def _top_rows(s, order, payload, k):
    big = jnp.float32(3e38)
    vals, pays = [], []
    for _ in range(k):
        m = jnp.max(s, axis=0, keepdims=True)
        o = jnp.min(jnp.where(s == m, order, big), axis=0, keepdims=True)
        hit = order == o
        pays.append(jnp.min(jnp.where(hit, payload, big), axis=0, keepdims=True))
        vals.append(m)
        s = jnp.where(hit, -jnp.inf, s)
    return jnp.concatenate(vals, axis=0), jnp.concatenate(pays, axis=0)


def _route_kernel(qp_ref, k1_ref, k2_ref, e_ref, g_ref, *, tt):
    qp = qp_ref[...]
    kd = PEER_KEY_DIM
    rows_k = lax.broadcasted_iota(I32, (PEER_KEYS, tt), 0).astype(F32)
    for h in range(PEER_HEADS):
        q1 = qp[:, (2 * h) * kd:(2 * h + 1) * kd]
        q2 = qp[:, (2 * h + 1) * kd:(2 * h + 2) * kd]
        s1 = lax.dot_general(k1_ref[h], q1, (((1,), (1,)), ((), ())), preferred_element_type=F32)
        s2 = lax.dot_general(k2_ref[h], q2, (((1,), (1,)), ((), ())), preferred_element_type=F32)
        v1, i1 = _top_rows(s1, rows_k, rows_k, PEER_TOPK)
        v2, i2 = _top_rows(s2, rows_k, rows_k, PEER_TOPK)
        cs, ce, co = [], [], []
        for i, rows, valid in _cand_layout():
            r = lax.broadcasted_iota(I32, (rows, tt), 0).astype(F32)
            if i is None:
                val = v1[8:16] + v2[0:1]
                eid = i1[8:16] * PEER_KEYS + i2[0:1]
                flat = (r + 8.0) * PEER_TOPK
            else:
                val = v1[i:i + 1] + v2[0:rows]
                eid = i1[i:i + 1] * PEER_KEYS + i2[0:rows]
                flat = r + float(i * PEER_TOPK)
                if valid < rows:
                    val = jnp.where(r < float(valid), val, -jnp.inf)
            cs.append(val)
            ce.append(eid)
            co.append(flat)
        cand = jnp.concatenate(cs, axis=0)
        top_s, top_e = _top_rows(cand, jnp.concatenate(co, axis=0), jnp.concatenate(ce, axis=0), PEER_TOPK)
        p = jnp.exp(top_s - top_s[0:1])
        gates = p / jnp.sum(p, axis=0, keepdims=True)
        e_ref[h * PEER_TOPK:(h + 1) * PEER_TOPK, :] = top_e.astype(I32)
        g_ref[h * PEER_TOPK:(h + 1) * PEER_TOPK, :] = gates


def _route(qp, peer_k1, peer_k2, tt):
    n, nq = qp.shape
    k1 = peer_k1.astype(BF16)
    k2 = peer_k2.astype(BF16)
    n_sel = PEER_HEADS * PEER_TOPK
    full = lambda a: pl.BlockSpec(a.shape, lambda i: (0,) * a.ndim)
    return pl.pallas_call(
        functools.partial(_route_kernel, tt=tt),
        grid=(n // tt,),
        in_specs=[pl.BlockSpec((tt, nq), lambda i: (i, 0)), full(k1), full(k2)],
        out_specs=[pl.BlockSpec((n_sel, tt), lambda i: (0, i)), pl.BlockSpec((n_sel, tt), lambda i: (0, i))],
        out_shape=[jax.ShapeDtypeStruct((n_sel, n), I32), jax.ShapeDtypeStruct((n_sel, n), F32)],
        compiler_params=_cparams(("arbitrary",)),
        name="route",
    )(qp, k1, k2)


def _final_kernel(h_ref, p_ref, g_ref, o_ref):
    h = h_ref[...] + p_ref[...]
    o_ref[...] = h * lax.rsqrt(jnp.mean(h * h, axis=-1, keepdims=True) + NORM_EPS) * g_ref[...]


def _final(h, p, g, tm):
    n, d = h.shape
    row = pl.BlockSpec((tm, d), lambda i: (i, 0))
    return pl.pallas_call(
        _final_kernel,
        grid=(n // tm,),
        in_specs=[row, row, pl.BlockSpec((1, d), lambda i: (0, 0))],
        out_specs=row,
        out_shape=jax.ShapeDtypeStruct((n, d), F32),
        compiler_params=_cparams(("arbitrary",)),
        name="final",
    )(h, p, g.reshape(1, d))


SC_CORES_V7X = 2
SC_SUBCORES_V7X = 16
SC_LANES_V7X = 16
PEER_TOK_BATCH = 16
PEER_ROW_CHUNK = 32
PEER_RING = 4


def _pack_bf16_pairs(t):
    half = t.shape[1] // 2
    tb = t.astype(BF16)
    lo = lax.bitcast_convert_type(tb[:, :half], jnp.uint16).astype(jnp.uint32)
    hi = lax.bitcast_convert_type(tb[:, half:], jnp.uint16).astype(jnp.uint32)
    return lax.bitcast_convert_type(lo | (hi << 16), I32)


def _unpack_pair(w):
    lo = lax.bitcast_convert_type(jnp.left_shift(w, 16), F32)
    hi = lax.bitcast_convert_type(w & jnp.int32(-65536), F32)
    return lo, hi


def _peer_sc_body(hn_hbm, e_hbm, g_hbm, u_hbm, v_hbm, out_hbm,
                  idx_v, gate_v, x_v, out_v, rows, p_v, act_v, sem, *, tpw, d, n_sel):
    nl = SC_LANES_V7X
    tb = PEER_TOK_BATCH
    rc = PEER_ROW_CHUNK
    n_chunk = n_sel // rc
    jobs_per_tok = 2 * n_chunk
    half = d // 2
    n_lane_blk = half // nl
    wid = lax.axis_index("s") * SC_CORES_V7X + lax.axis_index("c")
    base = wid * tpw
    lane = lax.iota(I32, nl)
    zero = jnp.zeros((nl,), F32)
    c_gelu = 2.0 * math.sqrt(2.0 / math.pi)

    def gather_copy(tab_hbm, job):
        tok = job // jobs_per_tok
        c = (job % jobs_per_tok) % n_chunk
        b = job % PEER_RING
        return pltpu.make_async_copy(tab_hbm.at[idx_v.at[tok, pl.ds(c * rc, rc)]], rows.at[b], sem.at[b])

    def start(job):
        j = job % jobs_per_tok

        @pl.when(j < n_chunk)
        def _():
            gather_copy(u_hbm, job).start()

        @pl.when(j >= n_chunk)
        def _():
            gather_copy(v_hbm, job).start()

    def compute_u(tok, c, b):
        def rg_body(rg, _):
            r0 = rg * 8

            def jbody(j2, accs):
                off0 = j2 * (2 * nl)
                off1 = off0 + nl
                x0 = plsc.bitcast(x_v[tok, pl.ds(off0, nl)], BF16)
                x1 = plsc.bitcast(x_v[tok, pl.ds(off1, nl)], BF16)
                new = []
                for r in range(8):
                    w0 = plsc.bitcast(rows[b, r0 + r, pl.ds(off0, nl)], BF16)
                    w1 = plsc.bitcast(rows[b, r0 + r, pl.ds(off1, nl)], BF16)
                    lo, hi = _unpack_pair(plsc.bitcast(w0 * x0 + w1 * x1, I32))
                    new.append(accs[r] + (lo + hi))
                return tuple(new)

            accs = lax.fori_loop(0, n_lane_blk // 2, jbody, (zero,) * 8)
            for r in range(8):
                p_v[c * rc + r0 + r, :] = accs[r]
            return 0

        lax.fori_loop(0, rc // 8, rg_body, 0)

    def finish_act(tok):
        def eg_body(eg, _):
            e0 = eg * nl
            ridx = e0 + lane
            s = zero
            for l in range(nl):
                s = s + plsc.load_gather(p_v, [ridx, jnp.full((nl,), l, I32)])
            inner = c_gelu * (s + 0.044715 * (s * s * s))
            gl = s / (1.0 + jnp.exp(-inner))
            a = gl * gate_v[tok, pl.ds(e0, nl)]
            bits = lax.bitcast_convert_type(a, I32)
            rnd = bits + jnp.int32(0x7FFF) + (lax.shift_right_logical(bits, 16) & 1)
            hi16 = rnd & jnp.int32(-65536)
            act_v[pl.ds(e0, nl)] = hi16 | lax.shift_right_logical(hi16, 16)
            return 0

        lax.fori_loop(0, n_sel // nl, eg_body, 0)

        def zbody(j, _):
            out_v[tok, pl.ds(j * nl, nl)] = zero
            return 0

        lax.fori_loop(0, d // nl, zbody, 0, unroll=4)

    def compute_v(tok, c, b):
        def rg_body(rg, _):
            r0 = rg * nl
            splat = [plsc.bitcast(plsc.load_gather(act_v, [jnp.full((nl,), 0, I32) + (c * rc + r0 + r)]), BF16)
                     for r in range(nl)]

            def tree(parts):
                while len(parts) > 1:
                    parts = [parts[i] + parts[i + 1] for i in range(0, len(parts), 2)]
                return parts[0]

            @plsc.parallel_loop(0, n_lane_blk, unroll=2)
            def _(j):
                off = j * nl
                los, his = [], []
                for r in range(0, nl, 2):
                    w0 = plsc.bitcast(rows[b, r0 + r, pl.ds(off, nl)], BF16)
                    w1 = plsc.bitcast(rows[b, r0 + r + 1, pl.ds(off, nl)], BF16)
                    lo, hi = _unpack_pair(plsc.bitcast(w0 * splat[r] + w1 * splat[r + 1], I32))
                    los.append(lo)
                    his.append(hi)
                out_v[tok, pl.ds(off, nl)] = out_v[tok, pl.ds(off, nl)] + tree(los)
                out_v[tok, pl.ds(half + off, nl)] = out_v[tok, pl.ds(half + off, nl)] + tree(his)

            return 0

        lax.fori_loop(0, rc // nl, rg_body, 0)

    def batch_body(bi, _):
        t0 = base + bi * tb
        pltpu.sync_copy(e_hbm.at[pl.ds(t0, tb)], idx_v)
        pltpu.sync_copy(g_hbm.at[pl.ds(t0, tb)], gate_v)
        pltpu.sync_copy(hn_hbm.at[pl.ds(t0, tb)], x_v)
        for pre in range(PEER_RING - 1):
            start(pre)

        def job_body(job, _):
            @pl.when(job + (PEER_RING - 1) < tb * jobs_per_tok)
            def _():
                start(job + (PEER_RING - 1))

            j = job % jobs_per_tok
            gather_copy(u_hbm, job).wait()
            tok = job // jobs_per_tok
            b = job % PEER_RING

            @pl.when(j < n_chunk)
            def _():
                compute_u(tok, j, b)

            @pl.when(j == n_chunk - 1)
            def _():
                finish_act(tok)

            @pl.when(j >= n_chunk)
            def _():
                compute_v(tok, j - n_chunk, b)

            return 0

        lax.fori_loop(0, tb * jobs_per_tok, job_body, 0)
        pltpu.sync_copy(out_v, out_hbm.at[pl.ds(t0, tb)])
        return 0

    lax.fori_loop(0, tpw // tb, batch_body, 0)


def _peer_sc(x_pk, experts, gates, u_tab, v_tab):
    n = x_pk.shape[0]
    d = 2 * x_pk.shape[1]
    n_sel = experts.shape[1]
    nw = SC_CORES_V7X * SC_SUBCORES_V7X
    tpw = n // nw
    mesh = plsc.VectorSubcoreMesh(core_axis_name="c", subcore_axis_name="s",
                                  num_cores=SC_CORES_V7X, num_subcores=SC_SUBCORES_V7X)
    body = functools.partial(_peer_sc_body, tpw=tpw, d=d, n_sel=n_sel)
    call = pl.kernel(
        body,
        out_type=jax.ShapeDtypeStruct((n, d), F32),
        mesh=mesh,
        scratch_types=[pltpu.VMEM((PEER_TOK_BATCH, n_sel), I32),
                       pltpu.VMEM((PEER_TOK_BATCH, n_sel), F32),
                       pltpu.VMEM((PEER_TOK_BATCH, d // 2), I32),
                       pltpu.VMEM((PEER_TOK_BATCH, d), F32),
                       pltpu.VMEM((PEER_RING, PEER_ROW_CHUNK, d // 2), I32),
                       pltpu.VMEM((n_sel, SC_LANES_V7X), F32),
                       pltpu.VMEM((n_sel,), I32),
                       pltpu.SemaphoreType.DMA((PEER_RING,))],
        compiler_params=pltpu.CompilerParams(needs_layout_passes=False),
        name="peer_sc",
    )
    return call(x_pk, experts, gates, u_tab, v_tab)


def kernel(x, norm1_g, w_in, a_re, a_im, log_dt, b_re, b_im, c_re, c_im, d_skip, w_glu, w_ssm_up, w_attn_up,
           w_out, norm2_g, peer_wq, peer_k1, peer_k2, peer_u, peer_v, norm_f_g):
    bsz, seq, d = x.shape
    n = bsz * seq
    h = x
    depth = norm1_g.shape[0]
    n_chunks = BATCH_CHUNKS if bsz % BATCH_CHUNKS == 0 else 1
    nb = bsz // n_chunks
    for layer in range(depth):
        last = layer + 1 == depth
        u, q, k, v, qi, ki, wi, gs, ga = _inproj(h, norm1_g[layer], w_in[layer], tm=min(512, seq))
        d_ssm = u.shape[-1]
        u_tb = u.transpose(1, 0, 2).reshape(n, d_ssm)
        y_tb = _s5(u_tb, a_re[layer], a_im[layer], log_dt[layer], b_re[layer], b_im[layer], c_re[layer],
                   c_im[layer], d_skip[layer], w_glu[layer], nb=bsz, tc=64)
        ys = y_tb.reshape(seq, bsz, d_ssm).transpose(1, 0, 2).reshape(n, d_ssm)
        wit, k4, vt = _dsa_prep(k, v, wi, kt=256)
        wsu = w_ssm_up[layer].astype(BF16)
        wau = w_attn_up[layer].astype(BF16)
        wo = w_out[layer].astype(BF16)
        wq = peer_wq[layer].astype(BF16)
        u_pk = _pack_bf16_pairs(peer_u[layer])
        v_pk = _pack_bf16_pairs(peer_v[layer])
        h2 = h.reshape(n, d)
        outs = []
        for c in range(n_chunks):
            ya = _dsa(q, qi, ki, wit, k4, vt, b0=c * nb, nb=nb, tq=128, kt=256)
            hm, hn, qp = _merge(h2, ys, ya, gs.reshape(n, d), ga.reshape(n, d), wsu, wau, wo, norm2_g[layer], wq,
                                r0=c * nb * seq, tm=512)
            e_t, g_t = _route(qp, peer_k1[layer], peer_k2[layer], tt=256)
            po = _peer_sc(hn, e_t.T, g_t.T, u_pk, v_pk)
            outs.append(_final(hm, po, norm_f_g, tm=512) if last else hm + po)
        h = jnp.concatenate(outs, axis=0).reshape(bsz, seq, d)
    return h
```

```python
import functools
import math

import numpy as np
import jax
import jax.numpy as jnp
from jax import lax
from jax.experimental import pallas as pl
from jax.experimental.pallas import tpu as pltpu
from jax.experimental.pallas import tpu_sc as plsc

F32 = jnp.float32
BF16 = jnp.bfloat16
I32 = jnp.int32

SSM_GROUP = 16
SSM_STATE = 64
ATTN_HEADS = 8
ATTN_KV_HEADS = 2
HEAD_DIM = 64
IDX_HEADS = 8
IDX_DIM = 32
TOPK_MAX = 256
ROPE_THETA = 10000.0
NEG_BIG = -1e30
PEER_HEADS = 8
PEER_KEYS = 128
PEER_KEY_DIM = 128
PEER_TOPK = 16
NORM_EPS = 1e-6

TIME_CHUNKS = 8
ROW_TILE = 512
DSA_TQ = 128
DSA_KT = 256
INT_MIN = -(2 ** 31)
VMEM_LIMIT = 56 * 1024 * 1024


def _cparams(sem):
    return pltpu.CompilerParams(dimension_semantics=sem, vmem_limit_bytes=VMEM_LIMIT)


def _gelu_tanh(x):
    return 0.5 * x * (1.0 + jnp.tanh(math.sqrt(2.0 / math.pi) * (x + 0.044715 * (x * x * x))))


def _sigmoid(x):
    return 1.0 / (1.0 + jnp.exp(-x))


def _rms(x, g):
    return x * lax.rsqrt(jnp.mean(x * x, axis=-1, keepdims=True) + NORM_EPS) * g


def _rot_cols(w, hd):
    d, n = w.shape
    w3 = w.reshape(d, n // hd, hd)
    half = hd // 2
    return jnp.concatenate([-w3[..., half:], w3[..., :half]], axis=-1).reshape(d, n)


def _rope_full(seq, hd, heads):
    pos = jnp.arange(seq, dtype=F32)
    inv = ROPE_THETA ** (-jnp.arange(0, hd, 2, dtype=F32) / hd)
    ang = pos[:, None] * inv[None, :]
    c = jnp.concatenate([jnp.cos(ang), jnp.cos(ang)], axis=-1)
    s = jnp.concatenate([jnp.sin(ang), jnp.sin(ang)], axis=-1)
    return jnp.tile(c, (1, heads)), jnp.tile(s, (1, heads))


def _in_weights(w_in, seq):
    d = w_in.shape[0]
    d_ssm = d // 2
    d_q = ATTN_HEADS * HEAD_DIM
    d_kv = ATTN_KV_HEADS * HEAD_DIM
    d_qi = IDX_HEADS * IDX_DIM
    splits = (d_ssm, d_q, d_kv, d_kv, d_qi, IDX_DIM, IDX_HEADS, d, d)
    offs = np.cumsum(splits)[:-1].tolist()
    wu, wq, wk, wv, wqi, wki, wwi, wgs, wga = jnp.split(w_in, offs, axis=1)
    pad = jnp.zeros((d, 128 - IDX_DIM), F32)
    cq, sq = _rope_full(seq, HEAD_DIM, ATTN_HEADS)
    ck, sk = _rope_full(seq, HEAD_DIM, ATTN_KV_HEADS)
    cqi, sqi = _rope_full(seq, IDX_DIM, IDX_HEADS)
    cki, ski = _rope_full(seq, IDX_DIM, 1)
    tpad = jnp.zeros((seq, 128 - IDX_DIM), F32)
    kv = dict(
        w=jnp.concatenate([wk, wki, pad], axis=1).astype(BF16),
        wvt=wv.T.astype(BF16),
        wr=jnp.concatenate([_rot_cols(wk, HEAD_DIM), _rot_cols(wki, IDX_DIM), pad], axis=1).astype(BF16),
        cs=jnp.concatenate([ck, cki, tpad], axis=1), sn=jnp.concatenate([sk, ski, tpad], axis=1))
    main = dict(
        w=jnp.concatenate([wu, wq, wqi, wgs, wga], axis=1).astype(BF16),
        wr=jnp.concatenate([_rot_cols(wq, HEAD_DIM), _rot_cols(wqi, IDX_DIM)], axis=1).astype(BF16),
        cs=jnp.concatenate([cq, cqi], axis=1), sn=jnp.concatenate([sq, sqi], axis=1),
        wwit=wwi.T.astype(BF16))
    return kv, main


def _kvproj_kernel(x_ref, g_ref, w_ref, wr_ref, wvt_ref, cs_ref, sn_ref, k_ref, vt_ref, ki_ref, *, d_kv, kt):
    xb = _rms(x_ref[0], g_ref[...]).astype(BF16)

    def mm(ref, lo, n):
        return jnp.dot(xb, ref[:, lo:lo + n], preferred_element_type=F32)

    k = mm(w_ref, 0, d_kv) * cs_ref[:, :d_kv] + mm(wr_ref, 0, d_kv) * sn_ref[:, :d_kv]
    for n in range(ATTN_KV_HEADS):
        k_ref[0, n] = k[:, n * HEAD_DIM:(n + 1) * HEAD_DIM].astype(BF16)
    vt = lax.dot_general(wvt_ref[...], xb, (((1,), (1,)), ((), ())), preferred_element_type=F32)
    for n in range(ATTN_KV_HEADS):
        for j in range(vt.shape[1] // kt):
            vt_ref[0, n, j] = vt[n * HEAD_DIM:(n + 1) * HEAD_DIM, j * kt:(j + 1) * kt].astype(BF16)
    kiw = (mm(w_ref, d_kv, 128) * cs_ref[:, d_kv:d_kv + 128]
           + mm(wr_ref, d_kv, 128) * sn_ref[:, d_kv:d_kv + 128])
    ki_ref[0] = kiw[:, :IDX_DIM].astype(BF16)


def _kvproj(x, norm_g, kv, tm, kt):
    bsz, seq, d = x.shape
    d_kv = ATTN_KV_HEADS * HEAD_DIM
    full = lambda a: pl.BlockSpec(a.shape, lambda s, b: (0,) * a.ndim)
    g = norm_g.reshape(1, d)
    ncs = kv["cs"].shape[1]
    return pl.pallas_call(
        functools.partial(_kvproj_kernel, d_kv=d_kv, kt=kt),
        grid=(seq // tm, bsz),
        in_specs=[pl.BlockSpec((1, tm, d), lambda s, b: (b, s, 0)), full(g), full(kv["w"]), full(kv["wr"]),
                  full(kv["wvt"]),
                  pl.BlockSpec((tm, ncs), lambda s, b: (s, 0)), pl.BlockSpec((tm, ncs), lambda s, b: (s, 0))],
        out_specs=[pl.BlockSpec((1, ATTN_KV_HEADS, tm, HEAD_DIM), lambda s, b: (b, 0, s, 0)),
                   pl.BlockSpec((1, ATTN_KV_HEADS, tm // kt, HEAD_DIM, kt), lambda s, b: (b, 0, s, 0, 0)),
                   pl.BlockSpec((1, tm, IDX_DIM), lambda s, b: (b, s, 0))],
        out_shape=[jax.ShapeDtypeStruct((bsz, ATTN_KV_HEADS, seq, HEAD_DIM), BF16),
                   jax.ShapeDtypeStruct((bsz, ATTN_KV_HEADS, seq // kt, HEAD_DIM, kt), BF16),
                   jax.ShapeDtypeStruct((bsz, seq, IDX_DIM), BF16)],
        compiler_params=_cparams(("arbitrary", "arbitrary")),
        name="kvproj",
    )(x, g, kv["w"], kv["wr"], kv["wvt"], kv["cs"], kv["sn"])


def _inproj_kernel(x_ref, g_ref, w_ref, wr_ref, cs_ref, sn_ref, wwit_ref,
                   u_ref, q_ref, qi_ref, wit_ref, gs_ref, ga_ref, *, d_ssm, d_q, d_qi, d_model, q_scale, wi_scale):
    xb = _rms(x_ref[0], g_ref[...]).astype(BF16)

    def mm(ref, lo, n):
        return jnp.dot(xb, ref[:, lo:lo + n], preferred_element_type=F32)

    o = 0
    u_ref[0] = mm(w_ref, o, d_ssm).astype(BF16)
    o += d_ssm
    q = mm(w_ref, o, d_q) * cs_ref[:, :d_q] + mm(wr_ref, 0, d_q) * sn_ref[:, :d_q]
    q_ref[0] = (q * q_scale).astype(BF16)
    o += d_q
    qi = mm(w_ref, o, d_qi) * cs_ref[:, d_q:d_q + d_qi] + mm(wr_ref, d_q, d_qi) * sn_ref[:, d_q:d_q + d_qi]
    qi_ref[0] = qi.astype(BF16)
    o += d_qi
    gs_ref[0] = _sigmoid(mm(w_ref, o, d_model)).astype(BF16)
    o += d_model
    ga_ref[0] = _sigmoid(mm(w_ref, o, d_model)).astype(BF16)
    wit_ref[0] = lax.dot_general(wwit_ref[...], xb, (((1,), (1,)), ((), ())),
                                 preferred_element_type=F32) * wi_scale


def _inproj(x, norm_g, main, s0, sc, tm):
    bsz, _, d = x.shape
    d_ssm = d // 2
    d_q = ATTN_HEADS * HEAD_DIM
    d_qi = IDX_HEADS * IDX_DIM
    i0 = s0 // tm
    kern = functools.partial(
        _inproj_kernel, d_ssm=d_ssm, d_q=d_q, d_qi=d_qi, d_model=d,
        q_scale=HEAD_DIM ** -0.5 * math.log2(math.e), wi_scale=(IDX_HEADS ** -0.5) * (IDX_DIM ** -0.5))
    tok = lambda n: pl.BlockSpec((1, tm, n), lambda s, b: (b, s, 0))
    full = lambda a: pl.BlockSpec(a.shape, lambda s, b: (0,) * a.ndim)
    g = norm_g.reshape(1, d)
    ncs = main["cs"].shape[1]
    outs = [(d_ssm, BF16), (d_q, BF16), (d_qi, BF16)]
    return pl.pallas_call(
        kern,
        grid=(sc // tm, bsz),
        in_specs=[pl.BlockSpec((1, tm, d), lambda s, b: (b, s + i0, 0)), full(g), full(main["w"]), full(main["wr"]),
                  pl.BlockSpec((tm, ncs), lambda s, b: (s + i0, 0)),
                  pl.BlockSpec((tm, ncs), lambda s, b: (s + i0, 0)), full(main["wwit"])],
        out_specs=[tok(n) for n, _ in outs] + [pl.BlockSpec((1, IDX_HEADS, tm), lambda s, b: (b, 0, s)),
                                                tok(d), tok(d)],
        out_shape=[jax.ShapeDtypeStruct((bsz, sc, n), dt) for n, dt in outs]
        + [jax.ShapeDtypeStruct((bsz, IDX_HEADS, sc), F32),
           jax.ShapeDtypeStruct((bsz, sc, d), BF16), jax.ShapeDtypeStruct((bsz, sc, d), BF16)],
        compiler_params=_cparams(("arbitrary", "arbitrary")),
        name="inproj",
    )(x, g, main["w"], main["wr"], main["cs"], main["sn"], main["wwit"])


def _s5_kernel(u_ref, sre_in, sim_in, bre_ref, bim_ref, cre_ref, cim_ref, are_ref, aim_ref, dsk_ref, wglu_ref,
               y_ref, st_re, st_im, sre, sim, *, tc, nb, lane_chunk):
    @pl.when(pl.program_id(0) == 0)
    def _():
        st_re[...] = sre_in[...]
        st_im[...] = sim_in[...]

    u = u_ref[...]
    n_half = bre_ref.shape[0]
    hin = bre_ref.shape[1]
    hst = bre_ref.shape[2]
    for h in range(n_half):
        uh = u[:, h * hin:(h + 1) * hin]
        sre[:, h * hst:(h + 1) * hst] = jnp.dot(uh, bre_ref[h], preferred_element_type=F32)
        sim[:, h * hst:(h + 1) * hst] = jnp.dot(uh, bim_ref[h], preferred_element_type=F32)

    n_state = sre.shape[1]
    for c in range(n_state // lane_chunk):
        cols = slice(c * lane_chunk, (c + 1) * lane_chunk)
        ar = are_ref[:, cols]
        ai = aim_ref[:, cols]

        def step(t, carry, cols=cols, ar=ar, ai=ai):
            sr, si = carry
            r0 = pl.multiple_of(t * nb, nb)
            nr = ar * sr - ai * si + sre[pl.ds(r0, nb), cols]
            ni = ar * si + ai * sr + sim[pl.ds(r0, nb), cols]
            sre[pl.ds(r0, nb), cols] = nr
            sim[pl.ds(r0, nb), cols] = ni
            return nr, ni

        sr, si = lax.fori_loop(0, tc, step, (st_re[:, cols], st_im[:, cols]), unroll=4)
        st_re[:, cols] = sr
        st_im[:, cols] = si

    ys = []
    for h in range(n_half):
        srh = sre[:, h * hst:(h + 1) * hst].astype(BF16)
        sih = sim[:, h * hst:(h + 1) * hst].astype(BF16)
        ys.append(jnp.dot(srh, cre_ref[h], preferred_element_type=F32)
                  - jnp.dot(sih, cim_ref[h], preferred_element_type=F32))
    y = jnp.concatenate(ys, axis=-1) + dsk_ref[...] * u.astype(F32)
    y = _gelu_tanh(y)
    gate = jnp.dot(y.astype(BF16), wglu_ref[...], preferred_element_type=F32)
    y_ref[...] = (y * _sigmoid(gate)).astype(BF16)


def _s5_params(a_re, a_im, log_dt, b_re, b_im, c_re, c_im, d_skip, w_glu, nb):
    groups = a_re.shape[0]
    d_ssm = groups * SSM_GROUP
    n_state = groups * SSM_STATE
    lam = lax.complex(a_re, a_im)
    dt = jnp.exp(log_dt)[:, None]
    a_bar = jnp.exp(lam * dt)
    b_bar = ((a_bar - 1.0) / lam)[..., None] * lax.complex(b_re, b_im)
    gh = min(groups, 256 // SSM_GROUP)
    n_half = groups // gh
    eye = jnp.eye(gh, dtype=F32)

    def bmat(bb):
        b4 = bb.reshape(n_half, gh, SSM_STATE, SSM_GROUP)
        return jnp.einsum('hgpc,gk->hgckp', b4, eye).reshape(n_half, gh * SSM_GROUP, gh * SSM_STATE)

    def cmat(cc):
        c4 = cc.reshape(n_half, gh, SSM_GROUP, SSM_STATE)
        return jnp.einsum('hgcp,gk->hgpkc', c4, eye).reshape(n_half, gh * SSM_STATE, gh * SSM_GROUP)

    return (bmat(jnp.real(b_bar)).astype(BF16), bmat(jnp.imag(b_bar)).astype(BF16),
            cmat(c_re).astype(BF16), cmat(c_im).astype(BF16),
            jnp.broadcast_to(jnp.real(a_bar).reshape(1, n_state), (nb, n_state)),
            jnp.broadcast_to(jnp.imag(a_bar).reshape(1, n_state), (nb, n_state)),
            d_skip.reshape(1, d_ssm), w_glu.astype(BF16))


def _s5(u_tb, st_re, st_im, params, nb, tc):
    rows, d_ssm = u_tb.shape
    n_state = st_re.shape[1]
    blk = tc * nb
    full = lambda a: pl.BlockSpec(a.shape, lambda i: (0,) * a.ndim)
    st_spec = pl.BlockSpec((nb, n_state), lambda i: (0, 0))
    kern = functools.partial(_s5_kernel, tc=tc, nb=nb, lane_chunk=512)
    return pl.pallas_call(
        kern,
        grid=(rows // blk,),
        in_specs=[pl.BlockSpec((blk, d_ssm), lambda i: (i, 0)), st_spec, st_spec] + [full(p) for p in params],
        out_specs=[pl.BlockSpec((blk, d_ssm), lambda i: (i, 0)), st_spec, st_spec],
        out_shape=[jax.ShapeDtypeStruct((rows, d_ssm), BF16),
                   jax.ShapeDtypeStruct((nb, n_state), F32), jax.ShapeDtypeStruct((nb, n_state), F32)],
        scratch_shapes=[pltpu.VMEM((blk, n_state), F32), pltpu.VMEM((blk, n_state), F32)],
        compiler_params=_cparams(("arbitrary",)),
        name="s5",
    )(u_tb, st_re, st_im, *params)


PART_ROWS = 32


def _dsa_kernel(qi_ref, wit_ref, q_ref, ki_ref, k_ref, vt_ref, o_ref, key_s, bias_s, lg_s,
                *, qb0, tq, kt, sub, topk, seq_bits):
    qb = pl.program_id(1) + qb0
    nkt = ((qb * tq + tq + sub * kt - 1) // (sub * kt)) * sub
    q_pos = qb * tq + lax.broadcasted_iota(I32, (1, tq), 1)
    k_eff = jnp.minimum(topk, q_pos + 1).astype(F32)

    qi = qi_ref[0]
    wit = wit_ref[0]
    qih = [qi[:, h * IDX_DIM:(h + 1) * IDX_DIM] for h in range(IDX_HEADS)]

    def key_pos(t):
        return t * kt + lax.broadcasted_iota(I32, (kt, tq), 0)

    def score_tile(t, _):
        r0 = pl.multiple_of(t * kt, kt)
        ki_t = ki_ref[0, pl.ds(r0, kt), :]
        sc = jnp.zeros((kt, tq), F32)
        for h in range(IDX_HEADS):
            rel = lax.dot_general(ki_t, qih[h], (((1,), (1,)), ((), ())), preferred_element_type=F32)
            sc = sc + jnp.maximum(rel, 0.0) * wit[h:h + 1, :]
        bits = lax.bitcast_convert_type(sc, I32)
        key = jnp.where(bits < 0, bits ^ jnp.int32(0x7FFFFFFF), bits)
        key = jnp.where(key_pos(t) <= q_pos, key, jnp.int32(INT_MIN))
        key_s[pl.ds(r0, kt), :] = key
        return 0

    lax.fori_loop(0, nkt, score_tile, 0)

    def count(pred_fn):
        def body(t, acc):
            r0 = pl.multiple_of(t * kt, kt)
            m = pred_fn(key_s[pl.ds(r0, kt), :], t)
            ones = jnp.where(m, 1.0, 0.0).reshape(kt // PART_ROWS, PART_ROWS, tq)
            return acc + jnp.sum(ones, axis=0)
        acc = lax.fori_loop(0, nkt, body, jnp.zeros((PART_ROWS, tq), F32))
        return jnp.sum(acc, axis=0, keepdims=True)

    def bit_step(i, u):
        bit = jnp.left_shift(jnp.int32(1), 31 - i)
        cand_u = u | bit
        cand_s = cand_u ^ jnp.int32(INT_MIN)
        cnt = count(lambda kk, t: kk >= cand_s)
        return jnp.where(cnt >= k_eff, cand_u, u)

    u_thr = lax.fori_loop(0, 32, bit_step, jnp.zeros((1, tq), I32))
    thr = u_thr ^ jnp.int32(INT_MIN)

    cnt_ge = count(lambda kk, t: kk >= thr)
    cnt_gt = count(lambda kk, t: kk > thr)
    need_eq = k_eff - cnt_gt
    has_tie = jnp.max(cnt_ge - k_eff) > 0.0

    def tie_cut():
        def pos_step(i, c):
            bit = jnp.left_shift(jnp.int32(1), seq_bits - 1 - i)
            cand = c | bit
            cnt = count(lambda kk, t: (kk == thr) & (key_pos(t) < cand))
            return jnp.where(cnt < need_eq, cand, c)
        return lax.fori_loop(0, seq_bits, pos_step, jnp.zeros((1, tq), I32))

    cut = lax.cond(has_tie, tie_cut, lambda: jnp.full((1, tq), 2 ** seq_bits, I32))

    def bias_tile(t, _):
        r0 = pl.multiple_of(t * kt, kt)
        key = key_s[pl.ds(r0, kt), :]
        sel = (key > thr) | ((key == thr) & (key_pos(t) <= cut))
        bias_s[pl.ds(r0, kt), :] = jnp.where(sel, 0.0, NEG_BIG)
        return 0

    lax.fori_loop(0, nkt, bias_tile, 0)

    q = q_ref[0]
    grp = ATTN_HEADS // ATTN_KV_HEADS
    pairs_per_kv = grp // 2
    n_unit = ATTN_KV_HEADS * pairs_per_kv
    wq = 2 * tq
    qpair = [jnp.concatenate([q[:, (2 * u) * HEAD_DIM:(2 * u + 1) * HEAD_DIM],
                              q[:, (2 * u + 1) * HEAD_DIM:(2 * u + 2) * HEAD_DIM]], axis=0)
             for u in range(n_unit)]

    def col_reduce(x, op):
        part = op(x.reshape(kt // PART_ROWS, PART_ROWS, wq), axis=0)
        return op(part, axis=0, keepdims=True)

    def attn_tile(t, carry):
        ms, ls, accs = list(carry[0]), list(carry[1]), list(carry[2])
        for hf in range(sub):
            r0 = pl.multiple_of((t * sub + hf) * kt, kt)
            bias = bias_s[pl.ds(r0, kt), :]
            bias2 = jnp.concatenate([bias, bias], axis=1)
            for u in range(n_unit):
                k_t = k_ref[0, u // pairs_per_kv, pl.ds(r0, kt), :]
                lg_s[hf, u] = lax.dot_general(k_t, qpair[u], (((1,), (1,)), ((), ())),
                                              preferred_element_type=F32) + bias2
        for hf in range(sub):
            for u in range(n_unit):
                lg = lg_s[hf, u]
                m_new = jnp.maximum(ms[u], col_reduce(lg, jnp.max))
                p = jnp.exp2(lg - m_new)
                alpha = jnp.exp2(ms[u] - m_new)
                v_t = vt_ref[0, u // pairs_per_kv, t * sub + hf]
                ls[u] = alpha * ls[u] + col_reduce(p, jnp.sum)
                accs[u] = alpha * accs[u] + jnp.dot(v_t, p.astype(BF16), preferred_element_type=F32)
                ms[u] = m_new
        return tuple(ms), tuple(ls), tuple(accs)

    init = (tuple(jnp.full((1, wq), NEG_BIG, F32) for _ in range(n_unit)),
            tuple(jnp.zeros((1, wq), F32) for _ in range(n_unit)),
            tuple(jnp.zeros((HEAD_DIM, wq), F32) for _ in range(n_unit)))
    _, ls, accs = lax.fori_loop(0, nkt // sub, attn_tile, init)
    for n in range(ATTN_KV_HEADS):
        o_ref[0, 0, n] = jnp.concatenate([accs[n * pairs_per_kv + pg] / ls[n * pairs_per_kv + pg]
                                          for pg in range(pairs_per_kv)], axis=1).astype(BF16)


def _dsa(q, qi, wit, ki, k4, vt, s0, tq, kt):
    bsz, sc, _ = q.shape
    seq = ki.shape[1]
    topk = min(TOPK_MAX, seq // 4)
    nqb = sc // tq
    grp = ATTN_HEADS // ATTN_KV_HEADS
    seq_bits = int(math.log2(seq))
    assert 2 ** seq_bits == seq
    sub = 2 if seq % (2 * kt) == 0 else 1
    n_unit = ATTN_HEADS // 2
    kern = functools.partial(_dsa_kernel, qb0=s0 // tq, tq=tq, kt=kt, sub=sub, topk=topk, seq_bits=seq_bits)
    o_t = pl.pallas_call(
        kern,
        grid=(bsz, nqb),
        in_specs=[pl.BlockSpec((1, tq, IDX_HEADS * IDX_DIM), lambda b, j: (b, j, 0)),
                  pl.BlockSpec((1, IDX_HEADS, tq), lambda b, j: (b, 0, j)),
                  pl.BlockSpec((1, tq, ATTN_HEADS * HEAD_DIM), lambda b, j: (b, j, 0)),
                  pl.BlockSpec((1, seq, IDX_DIM), lambda b, j: (b, 0, 0)),
                  pl.BlockSpec((1, ATTN_KV_HEADS, seq, HEAD_DIM), lambda b, j: (b, 0, 0, 0)),
                  pl.BlockSpec((1, ATTN_KV_HEADS, seq // kt, HEAD_DIM, kt), lambda b, j: (b, 0, 0, 0, 0))],
        out_specs=pl.BlockSpec((1, 1, ATTN_KV_HEADS, HEAD_DIM, grp * tq), lambda b, j: (b, j, 0, 0, 0)),
        out_shape=jax.ShapeDtypeStruct((bsz, nqb, ATTN_KV_HEADS, HEAD_DIM, grp * tq), BF16),
        scratch_shapes=[pltpu.VMEM((seq, tq), I32), pltpu.VMEM((seq, tq), F32),
                        pltpu.VMEM((sub, n_unit, kt, 2 * tq), F32)],
        compiler_params=_cparams(("arbitrary", "arbitrary")),
        name="dsa",
    )(qi, wit, q, ki, k4, vt)
    o = o_t.reshape(bsz, nqb, ATTN_KV_HEADS, HEAD_DIM, grp, tq).transpose(0, 1, 5, 2, 4, 3)
    return o.reshape(bsz, sc, ATTN_HEADS * HEAD_DIM)


def _merge_kernel(x_ref, ys_ref, ya_ref, gs_ref, ga_ref, wsu_ref, wau_ref, wout_ref, g2_ref, wq_ref,
                  h_ref, hn_ref, qp_ref):
    ms = jnp.dot(ys_ref[0], wsu_ref[...], preferred_element_type=F32)
    ma = jnp.dot(ya_ref[0], wau_ref[...], preferred_element_type=F32)
    merged = gs_ref[0].astype(F32) * ms + ga_ref[0].astype(F32) * ma
    h = x_ref[0] + jnp.dot(merged.astype(BF16), wout_ref[...], preferred_element_type=F32)
    h_ref[0] = h
    hb = _rms(h, g2_ref[...]).astype(BF16)
    bits = lax.bitcast_convert_type(hb.astype(F32), I32)
    half = bits.shape[1] // 2
    hn_ref[0] = (bits[:, half:] & jnp.int32(-65536)) | lax.shift_right_logical(bits[:, :half], 16)
    qp_ref[0] = jnp.dot(hb, wq_ref[...], preferred_element_type=F32).astype(BF16)


def _merge(x, ys, ya, gs, ga, wsu, wau, wo, norm2_g, wq, s0, tm):
    bsz, sc, _ = ya.shape
    d = x.shape[2]
    i0 = s0 // tm
    loc = lambda a: pl.BlockSpec((1, tm, a.shape[2]), lambda b, i: (b, i, 0))
    full = lambda a: pl.BlockSpec(a.shape, lambda b, i: (0,) * a.ndim)
    g2 = norm2_g.reshape(1, d)
    nq = wq.shape[1]
    out = lambda n: pl.BlockSpec((1, tm, n), lambda b, i: (b, i, 0))
    return pl.pallas_call(
        _merge_kernel,
        grid=(bsz, sc // tm),
        in_specs=[pl.BlockSpec((1, tm, d), lambda b, i: (b, i + i0, 0)), loc(ys), loc(ya), loc(gs), loc(ga),
                  full(wsu), full(wau), full(wo), full(g2), full(wq)],
        out_specs=[out(d), out(d // 2), out(nq)],
        out_shape=[jax.ShapeDtypeStruct((bsz, sc, d), F32), jax.ShapeDtypeStruct((bsz, sc, d // 2), I32),
                   jax.ShapeDtypeStruct((bsz, sc, nq), BF16)],
        compiler_params=_cparams(("arbitrary", "arbitrary")),
        name="merge",
    )(x, ys, ya, gs, ga, wsu, wau, wo, g2, wq)


def _cand_layout():
    blocks = []
    blocks.append((0, 16, 16))
    for i in range(1, 8):
        blocks.append((i, 8, PEER_TOPK // (i + 1)))
    blocks.append((None, 8, 8))
    return blocks


def _top_rows(s, order, payload, k):
    big = jnp.float32(3e38)
    vals, pays = [], []
    for _ in range(k):
        m = jnp.max(s, axis=0, keepdims=True)
        o = jnp.min(jnp.where(s == m, order, big), axis=0, keepdims=True)
        hit = order == o
        pays.append(jnp.min(jnp.where(hit, payload, big), axis=0, keepdims=True))
        vals.append(m)
        s = jnp.where(hit, -jnp.inf, s)
    return jnp.concatenate(vals, axis=0), jnp.concatenate(pays, axis=0)


def _route_kernel(qp_ref, k1_ref, k2_ref, e_ref, g_ref, *, tt):
    qp = qp_ref[...]
    kd = PEER_KEY_DIM
    rows_k = lax.broadcasted_iota(I32, (PEER_KEYS, tt), 0).astype(F32)
    for h in range(PEER_HEADS):
        q1 = qp[:, (2 * h) * kd:(2 * h + 1) * kd]
        q2 = qp[:, (2 * h + 1) * kd:(2 * h + 2) * kd]
        s1 = lax.dot_general(k1_ref[h], q1, (((1,), (1,)), ((), ())), preferred_element_type=F32)
        s2 = lax.dot_general(k2_ref[h], q2, (((1,), (1,)), ((), ())), preferred_element_type=F32)
        v1, i1 = _top_rows(s1, rows_k, rows_k, PEER_TOPK)
        v2, i2 = _top_rows(s2, rows_k, rows_k, PEER_TOPK)
        cs, ce, co = [], [], []
        for i, rows, valid in _cand_layout():
            r = lax.broadcasted_iota(I32, (rows, tt), 0).astype(F32)
            if i is None:
                val = v1[8:16] + v2[0:1]
                eid = i1[8:16] * PEER_KEYS + i2[0:1]
                flat = (r + 8.0) * PEER_TOPK
            else:
                val = v1[i:i + 1] + v2[0:rows]
                eid = i1[i:i + 1] * PEER_KEYS + i2[0:rows]
                flat = r + float(i * PEER_TOPK)
                if valid < rows:
                    val = jnp.where(r < float(valid), val, -jnp.inf)
            cs.append(val)
            ce.append(eid)
            co.append(flat)
        cand = jnp.concatenate(cs, axis=0)
        top_s, top_e = _top_rows(cand, jnp.concatenate(co, axis=0), jnp.concatenate(ce, axis=0), PEER_TOPK)
        p = jnp.exp(top_s - top_s[0:1])
        gates = p / jnp.sum(p, axis=0, keepdims=True)
        e_ref[h * PEER_TOPK:(h + 1) * PEER_TOPK, :] = top_e.astype(I32)
        g_ref[h * PEER_TOPK:(h + 1) * PEER_TOPK, :] = gates


def _route(qp, k1, k2, tt):
    n, nq = qp.shape
    n_sel = PEER_HEADS * PEER_TOPK
    full = lambda a: pl.BlockSpec(a.shape, lambda i: (0,) * a.ndim)
    return pl.pallas_call(
        functools.partial(_route_kernel, tt=tt),
        grid=(n // tt,),
        in_specs=[pl.BlockSpec((tt, nq), lambda i: (i, 0)), full(k1), full(k2)],
        out_specs=[pl.BlockSpec((n_sel, tt), lambda i: (0, i)), pl.BlockSpec((n_sel, tt), lambda i: (0, i))],
        out_shape=[jax.ShapeDtypeStruct((n_sel, n), I32), jax.ShapeDtypeStruct((n_sel, n), F32)],
        compiler_params=_cparams(("arbitrary",)),
        name="route",
    )(qp, k1, k2)


def _final_kernel(h_ref, p_ref, g_ref, o_ref):
    o_ref[...] = _rms(h_ref[...] + p_ref[...], g_ref[...])


def _final(h, p, g, tm):
    n, d = h.shape
    row = pl.BlockSpec((tm, d), lambda i: (i, 0))
    return pl.pallas_call(
        _final_kernel,
        grid=(n // tm,),
        in_specs=[row, row, pl.BlockSpec((1, d), lambda i: (0, 0))],
        out_specs=row,
        out_shape=jax.ShapeDtypeStruct((n, d), F32),
        compiler_params=_cparams(("arbitrary",)),
        name="final",
    )(h, p, g.reshape(1, d))


SC_CORES_V7X = 2
SC_SUBCORES_V7X = 16
SC_LANES_V7X = 16
PEER_TOK_BATCH = 16
PEER_ROW_CHUNK = 32
PEER_RING = 4


def _pack_bf16_pairs(t):
    half = t.shape[1] // 2
    tb = t.astype(BF16)
    lo = lax.bitcast_convert_type(tb[:, :half], jnp.uint16).astype(jnp.uint32)
    hi = lax.bitcast_convert_type(tb[:, half:], jnp.uint16).astype(jnp.uint32)
    return lax.bitcast_convert_type(lo | (hi << 16), I32)


def _unpack_pair(w):
    lo = lax.bitcast_convert_type(jnp.left_shift(w, 16), F32)
    hi = lax.bitcast_convert_type(w & jnp.int32(-65536), F32)
    return lo, hi


def _peer_sc_body(hn_hbm, e_hbm, g_hbm, u_hbm, v_hbm, out_hbm,
                  idx_v, gate_v, x_v, out_v, rows, p_v, act_v, sem, *, tpw, d, n_sel):
    nl = SC_LANES_V7X
    tb = PEER_TOK_BATCH
    rc = PEER_ROW_CHUNK
    n_chunk = n_sel // rc
    jobs_per_tok = 2 * n_chunk
    half = d // 2
    n_lane_blk = half // nl
    wid = lax.axis_index("s") * SC_CORES_V7X + lax.axis_index("c")
    base = wid * tpw
    lane = lax.iota(I32, nl)
    zero = jnp.zeros((nl,), F32)
    c_gelu = 2.0 * math.sqrt(2.0 / math.pi)

    def gather_copy(tab_hbm, job):
        tok = job // jobs_per_tok
        c = (job % jobs_per_tok) % n_chunk
        b = job % PEER_RING
        return pltpu.make_async_copy(tab_hbm.at[idx_v.at[tok, pl.ds(c * rc, rc)]], rows.at[b], sem.at[b])

    def start(job):
        j = job % jobs_per_tok

        @pl.when(j < n_chunk)
        def _():
            gather_copy(u_hbm, job).start()

        @pl.when(j >= n_chunk)
        def _():
            gather_copy(v_hbm, job).start()

    def compute_u(tok, c, b):
        def rg_body(rg, _):
            r0 = rg * 8

            def jbody(j2, accs):
                off0 = j2 * (2 * nl)
                off1 = off0 + nl
                x0 = plsc.bitcast(x_v[tok, pl.ds(off0, nl)], BF16)
                x1 = plsc.bitcast(x_v[tok, pl.ds(off1, nl)], BF16)
                new = []
                for r in range(8):
                    w0 = plsc.bitcast(rows[b, r0 + r, pl.ds(off0, nl)], BF16)
                    w1 = plsc.bitcast(rows[b, r0 + r, pl.ds(off1, nl)], BF16)
                    lo, hi = _unpack_pair(plsc.bitcast(w0 * x0 + w1 * x1, I32))
                    new.append(accs[r] + (lo + hi))
                return tuple(new)

            accs = lax.fori_loop(0, n_lane_blk // 2, jbody, (zero,) * 8)
            for r in range(8):
                p_v[c * rc + r0 + r, :] = accs[r]
            return 0

        lax.fori_loop(0, rc // 8, rg_body, 0)

    def finish_act(tok):
        def eg_body(eg, _):
            e0 = eg * nl
            ridx = e0 + lane
            s = zero
            for l in range(nl):
                s = s + plsc.load_gather(p_v, [ridx, jnp.full((nl,), l, I32)])
            inner = c_gelu * (s + 0.044715 * (s * s * s))
            gl = s / (1.0 + jnp.exp(-inner))
            a = gl * gate_v[tok, pl.ds(e0, nl)]
            bits = lax.bitcast_convert_type(a, I32)
            rnd = bits + jnp.int32(0x7FFF) + (lax.shift_right_logical(bits, 16) & 1)
            hi16 = rnd & jnp.int32(-65536)
            act_v[pl.ds(e0, nl)] = hi16 | lax.shift_right_logical(hi16, 16)
            return 0

        lax.fori_loop(0, n_sel // nl, eg_body, 0)

        def zbody(j, _):
            out_v[tok, pl.ds(j * nl, nl)] = zero
            return 0

        lax.fori_loop(0, d // nl, zbody, 0, unroll=4)

    def compute_v(tok, c, b):
        def rg_body(rg, _):
            r0 = rg * nl
            splat = [plsc.bitcast(plsc.load_gather(act_v, [jnp.full((nl,), 0, I32) + (c * rc + r0 + r)]), BF16)
                     for r in range(nl)]

            def tree(parts):
                while len(parts) > 1:
                    parts = [parts[i] + parts[i + 1] for i in range(0, len(parts), 2)]
                return parts[0]

            @plsc.parallel_loop(0, n_lane_blk, unroll=2)
            def _(j):
                off = j * nl
                los, his = [], []
                for r in range(0, nl, 2):
                    w0 = plsc.bitcast(rows[b, r0 + r, pl.ds(off, nl)], BF16)
                    w1 = plsc.bitcast(rows[b, r0 + r + 1, pl.ds(off, nl)], BF16)
                    lo, hi = _unpack_pair(plsc.bitcast(w0 * splat[r] + w1 * splat[r + 1], I32))
                    los.append(lo)
                    his.append(hi)
                out_v[tok, pl.ds(off, nl)] = out_v[tok, pl.ds(off, nl)] + tree(los)
                out_v[tok, pl.ds(half + off, nl)] = out_v[tok, pl.ds(half + off, nl)] + tree(his)

            return 0

        lax.fori_loop(0, rc // nl, rg_body, 0)

    def batch_body(bi, _):
        t0 = base + bi * tb
        pltpu.sync_copy(e_hbm.at[pl.ds(t0, tb)], idx_v)
        pltpu.sync_copy(g_hbm.at[pl.ds(t0, tb)], gate_v)
        pltpu.sync_copy(hn_hbm.at[pl.ds(t0, tb)], x_v)
        for pre in range(PEER_RING - 1):
            start(pre)

        def job_body(job, _):
            @pl.when(job + (PEER_RING - 1) < tb * jobs_per_tok)
            def _():
                start(job + (PEER_RING - 1))

            j = job % jobs_per_tok
            gather_copy(u_hbm, job).wait()
            tok = job // jobs_per_tok
            b = job % PEER_RING

            @pl.when(j < n_chunk)
            def _():
                compute_u(tok, j, b)

            @pl.when(j == n_chunk - 1)
            def _():
                finish_act(tok)

            @pl.when(j >= n_chunk)
            def _():
                compute_v(tok, j - n_chunk, b)

            return 0

        lax.fori_loop(0, tb * jobs_per_tok, job_body, 0)
        pltpu.sync_copy(out_v, out_hbm.at[pl.ds(t0, tb)])
        return 0

    lax.fori_loop(0, tpw // tb, batch_body, 0)


def _peer_sc(x_pk, experts, gates, u_tab, v_tab):
    n = x_pk.shape[0]
    d = 2 * x_pk.shape[1]
    n_sel = experts.shape[1]
    nw = SC_CORES_V7X * SC_SUBCORES_V7X
    tpw = n // nw
    mesh = plsc.VectorSubcoreMesh(core_axis_name="c", subcore_axis_name="s",
                                  num_cores=SC_CORES_V7X, num_subcores=SC_SUBCORES_V7X)
    body = functools.partial(_peer_sc_body, tpw=tpw, d=d, n_sel=n_sel)
    call = pl.kernel(
        body,
        out_type=jax.ShapeDtypeStruct((n, d), F32),
        mesh=mesh,
        scratch_types=[pltpu.VMEM((PEER_TOK_BATCH, n_sel), I32),
                       pltpu.VMEM((PEER_TOK_BATCH, n_sel), F32),
                       pltpu.VMEM((PEER_TOK_BATCH, d // 2), I32),
                       pltpu.VMEM((PEER_TOK_BATCH, d), F32),
                       pltpu.VMEM((PEER_RING, PEER_ROW_CHUNK, d // 2), I32),
                       pltpu.VMEM((n_sel, SC_LANES_V7X), F32),
                       pltpu.VMEM((n_sel,), I32),
                       pltpu.SemaphoreType.DMA((PEER_RING,))],
        compiler_params=pltpu.CompilerParams(needs_layout_passes=False),
        name="peer_sc",
    )
    return call(x_pk, experts, gates, u_tab, v_tab)


def kernel(x, norm1_g, w_in, a_re, a_im, log_dt, b_re, b_im, c_re, c_im, d_skip, w_glu, w_ssm_up, w_attn_up,
           w_out, norm2_g, peer_wq, peer_k1, peer_k2, peer_u, peer_v, norm_f_g):
    bsz, seq, d = x.shape
    depth = norm1_g.shape[0]
    tm = min(ROW_TILE, seq)
    n_chunks = TIME_CHUNKS if seq % (TIME_CHUNKS * tm) == 0 else max(1, seq // tm)
    sc = seq // n_chunks
    h = x
    for layer in range(depth):
        last = layer + 1 == depth
        kv_w, main_w = _in_weights(w_in[layer], seq)
        s5p = _s5_params(a_re[layer], a_im[layer], log_dt[layer], b_re[layer], b_im[layer], c_re[layer],
                         c_im[layer], d_skip[layer], w_glu[layer], nb=bsz)
        n_state = s5p[4].shape[1]
        wsu = w_ssm_up[layer].astype(BF16)
        wau = w_attn_up[layer].astype(BF16)
        wo = w_out[layer].astype(BF16)
        wq = peer_wq[layer].astype(BF16)
        k1 = peer_k1[layer].astype(BF16)
        k2 = peer_k2[layer].astype(BF16)
        u_pk = _pack_bf16_pairs(peer_u[layer])
        v_pk = _pack_bf16_pairs(peer_v[layer])
        k4, vt, ki = _kvproj(h, norm1_g[layer], kv_w, tm=tm, kt=DSA_KT)
        st_re = jnp.zeros((bsz, n_state), F32)
        st_im = jnp.zeros((bsz, n_state), F32)
        outs = []
        for c in range(n_chunks):
            s0 = c * sc
            u, q, qi, wit, gs, ga = _inproj(h, norm1_g[layer], main_w, s0=s0, sc=sc, tm=tm)
            d_ssm = u.shape[-1]
            u_tb = u.transpose(1, 0, 2).reshape(sc * bsz, d_ssm)
            y_tb, st_re, st_im = _s5(u_tb, st_re, st_im, s5p, nb=bsz, tc=64)
            ys = y_tb.reshape(sc, bsz, d_ssm).transpose(1, 0, 2)
            ya = _dsa(q, qi, wit, ki, k4, vt, s0=s0, tq=DSA_TQ, kt=DSA_KT)
            hm, x_pk, qp = _merge(h, ys, ya, gs, ga, wsu, wau, wo, norm2_g[layer], wq, s0=s0, tm=tm)
            nt = bsz * sc
            e_t, g_t = _route(qp.reshape(nt, -1), k1, k2, tt=256)
            po = _peer_sc(x_pk.reshape(nt, -1), e_t.T, g_t.T, u_pk, v_pk)
            hm2 = hm.reshape(nt, d)
            o = _final(hm2, po, norm_f_g, tm=tm) if last else hm2 + po
            outs.append(o.reshape(bsz, sc, d))
        h = jnp.concatenate(outs, axis=1)
    return h
```

```python
import functools
import math

import numpy as np
import jax
import jax.numpy as jnp
from jax import lax
from jax.experimental import pallas as pl
from jax.experimental.pallas import tpu as pltpu
from jax.experimental.pallas import tpu_sc as plsc

F32 = jnp.float32
BF16 = jnp.bfloat16
I32 = jnp.int32

SSM_GROUP = 16
SSM_STATE = 64
ATTN_HEADS = 8
ATTN_KV_HEADS = 2
HEAD_DIM = 64
IDX_HEADS = 8
IDX_DIM = 32
TOPK_MAX = 256
ROPE_THETA = 10000.0
NEG_BIG = -1e30
PEER_HEADS = 8
PEER_KEYS = 128
PEER_KEY_DIM = 128
PEER_TOPK = 16
NORM_EPS = 1e-6

TIME_CHUNKS = 8
ROW_TILE = 512
DSA_TQ = 128
DSA_KT = 256
INT_MIN = -(2 ** 31)
VMEM_LIMIT = 56 * 1024 * 1024


def _cparams(sem):
    return pltpu.CompilerParams(dimension_semantics=sem, vmem_limit_bytes=VMEM_LIMIT)


def _gelu_tanh(x):
    return 0.5 * x * (1.0 + jnp.tanh(math.sqrt(2.0 / math.pi) * (x + 0.044715 * (x * x * x))))


def _sigmoid(x):
    return 1.0 / (1.0 + jnp.exp(-x))


def _rms(x, g):
    return x * lax.rsqrt(jnp.mean(x * x, axis=-1, keepdims=True) + NORM_EPS) * g


def _rot_cols(w, hd):
    d, n = w.shape
    w3 = w.reshape(d, n // hd, hd)
    half = hd // 2
    return jnp.concatenate([-w3[..., half:], w3[..., :half]], axis=-1).reshape(d, n)


def _rope_full(seq, hd, heads):
    pos = jnp.arange(seq, dtype=F32)
    inv = ROPE_THETA ** (-jnp.arange(0, hd, 2, dtype=F32) / hd)
    ang = pos[:, None] * inv[None, :]
    c = jnp.concatenate([jnp.cos(ang), jnp.cos(ang)], axis=-1)
    s = jnp.concatenate([jnp.sin(ang), jnp.sin(ang)], axis=-1)
    return jnp.tile(c, (1, heads)), jnp.tile(s, (1, heads))


def _in_weights(w_in, seq):
    d = w_in.shape[0]
    d_ssm = d // 2
    d_q = ATTN_HEADS * HEAD_DIM
    d_kv = ATTN_KV_HEADS * HEAD_DIM
    d_qi = IDX_HEADS * IDX_DIM
    splits = (d_ssm, d_q, d_kv, d_kv, d_qi, IDX_DIM, IDX_HEADS, d, d)
    offs = np.cumsum(splits)[:-1].tolist()
    wu, wq, wk, wv, wqi, wki, wwi, wgs, wga = jnp.split(w_in, offs, axis=1)
    pad = jnp.zeros((d, 128 - IDX_DIM), F32)
    cq, sq = _rope_full(seq, HEAD_DIM, ATTN_HEADS)
    ck, sk = _rope_full(seq, HEAD_DIM, ATTN_KV_HEADS)
    cqi, sqi = _rope_full(seq, IDX_DIM, IDX_HEADS)
    cki, ski = _rope_full(seq, IDX_DIM, 1)
    tpad = jnp.zeros((seq, 128 - IDX_DIM), F32)
    kv = dict(
        w=jnp.concatenate([wk, wki, pad], axis=1).astype(BF16),
        wvt=wv.T.astype(BF16),
        wr=jnp.concatenate([_rot_cols(wk, HEAD_DIM), _rot_cols(wki, IDX_DIM), pad], axis=1).astype(BF16),
        cs=jnp.concatenate([ck, cki, tpad], axis=1), sn=jnp.concatenate([sk, ski, tpad], axis=1))
    main = dict(
        w=jnp.concatenate([wu, wq, wqi, wgs, wga], axis=1).astype(BF16),
        wr=jnp.concatenate([_rot_cols(wq, HEAD_DIM), _rot_cols(wqi, IDX_DIM)], axis=1).astype(BF16),
        cs=jnp.concatenate([cq, cqi], axis=1), sn=jnp.concatenate([sq, sqi], axis=1),
        wwit=wwi.T.astype(BF16))
    return kv, main


def _kvproj_kernel(x_ref, g_ref, w_ref, wr_ref, wvt_ref, cs_ref, sn_ref, k_ref, vt_ref, ki_ref, *, d_kv, kt):
    xb = _rms(x_ref[0], g_ref[...]).astype(BF16)

    def mm(ref, lo, n):
        return jnp.dot(xb, ref[:, lo:lo + n], preferred_element_type=F32)

    k = mm(w_ref, 0, d_kv) * cs_ref[:, :d_kv] + mm(wr_ref, 0, d_kv) * sn_ref[:, :d_kv]
    for n in range(ATTN_KV_HEADS):
        k_ref[0, n] = k[:, n * HEAD_DIM:(n + 1) * HEAD_DIM].astype(BF16)
    vt = lax.dot_general(wvt_ref[...], xb, (((1,), (1,)), ((), ())), preferred_element_type=F32)
    for n in range(ATTN_KV_HEADS):
        for j in range(vt.shape[1] // kt):
            vt_ref[0, n, j] = vt[n * HEAD_DIM:(n + 1) * HEAD_DIM, j * kt:(j + 1) * kt].astype(BF16)
    kiw = (mm(w_ref, d_kv, 128) * cs_ref[:, d_kv:d_kv + 128]
           + mm(wr_ref, d_kv, 128) * sn_ref[:, d_kv:d_kv + 128])
    ki_ref[0] = kiw[:, :IDX_DIM].astype(BF16)


def _kvproj(x, norm_g, kv, tm, kt):
    bsz, seq, d = x.shape
    d_kv = ATTN_KV_HEADS * HEAD_DIM
    full = lambda a: pl.BlockSpec(a.shape, lambda s, b: (0,) * a.ndim)
    g = norm_g.reshape(1, d)
    ncs = kv["cs"].shape[1]
    return pl.pallas_call(
        functools.partial(_kvproj_kernel, d_kv=d_kv, kt=kt),
        grid=(seq // tm, bsz),
        in_specs=[pl.BlockSpec((1, tm, d), lambda s, b: (b, s, 0)), full(g), full(kv["w"]), full(kv["wr"]),
                  full(kv["wvt"]),
                  pl.BlockSpec((tm, ncs), lambda s, b: (s, 0)), pl.BlockSpec((tm, ncs), lambda s, b: (s, 0))],
        out_specs=[pl.BlockSpec((1, ATTN_KV_HEADS, tm, HEAD_DIM), lambda s, b: (b, 0, s, 0)),
                   pl.BlockSpec((1, ATTN_KV_HEADS, tm // kt, HEAD_DIM, kt), lambda s, b: (b, 0, s, 0, 0)),
                   pl.BlockSpec((1, tm, IDX_DIM), lambda s, b: (b, s, 0))],
        out_shape=[jax.ShapeDtypeStruct((bsz, ATTN_KV_HEADS, seq, HEAD_DIM), BF16),
                   jax.ShapeDtypeStruct((bsz, ATTN_KV_HEADS, seq // kt, HEAD_DIM, kt), BF16),
                   jax.ShapeDtypeStruct((bsz, seq, IDX_DIM), BF16)],
        compiler_params=_cparams(("arbitrary", "arbitrary")),
        name="kvproj",
    )(x, g, kv["w"], kv["wr"], kv["wvt"], kv["cs"], kv["sn"])


def _inproj_kernel(x_ref, g_ref, w_ref, wr_ref, cs_ref, sn_ref, wwit_ref, after_ref,
                   u_ref, q_ref, qi_ref, wit_ref, gs_ref, ga_ref, *, d_ssm, d_q, d_qi, d_model, q_scale, wi_scale):
    del after_ref
    xb = _rms(x_ref[0], g_ref[...]).astype(BF16)

    def mm(ref, lo, n):
        return jnp.dot(xb, ref[:, lo:lo + n], preferred_element_type=F32)

    o = 0
    u_ref[0] = mm(w_ref, o, d_ssm).astype(BF16)
    o += d_ssm
    q = mm(w_ref, o, d_q) * cs_ref[:, :d_q] + mm(wr_ref, 0, d_q) * sn_ref[:, :d_q]
    q_ref[0] = (q * q_scale).astype(BF16)
    o += d_q
    qi = mm(w_ref, o, d_qi) * cs_ref[:, d_q:d_q + d_qi] + mm(wr_ref, d_q, d_qi) * sn_ref[:, d_q:d_q + d_qi]
    qi_ref[0] = qi.astype(BF16)
    o += d_qi
    gs_ref[0] = _sigmoid(mm(w_ref, o, d_model)).astype(BF16)
    o += d_model
    ga_ref[0] = _sigmoid(mm(w_ref, o, d_model)).astype(BF16)
    wit_ref[0] = lax.dot_general(wwit_ref[...], xb, (((1,), (1,)), ((), ())),
                                 preferred_element_type=F32) * wi_scale


def _inproj(x, norm_g, main, after, s0, sc, tm):
    bsz, _, d = x.shape
    d_ssm = d // 2
    d_q = ATTN_HEADS * HEAD_DIM
    d_qi = IDX_HEADS * IDX_DIM
    i0 = s0 // tm
    kern = functools.partial(
        _inproj_kernel, d_ssm=d_ssm, d_q=d_q, d_qi=d_qi, d_model=d,
        q_scale=HEAD_DIM ** -0.5 * math.log2(math.e), wi_scale=(IDX_HEADS ** -0.5) * (IDX_DIM ** -0.5))
    tok = lambda n: pl.BlockSpec((1, tm, n), lambda s, b: (b, s, 0))
    full = lambda a: pl.BlockSpec(a.shape, lambda s, b: (0,) * a.ndim)
    g = norm_g.reshape(1, d)
    ncs = main["cs"].shape[1]
    outs = [(d_ssm, BF16), (d_q, BF16), (d_qi, BF16)]
    return pl.pallas_call(
        kern,
        grid=(sc // tm, bsz),
        in_specs=[pl.BlockSpec((1, tm, d), lambda s, b: (b, s + i0, 0)), full(g), full(main["w"]), full(main["wr"]),
                  pl.BlockSpec((tm, ncs), lambda s, b: (s + i0, 0)),
                  pl.BlockSpec((tm, ncs), lambda s, b: (s + i0, 0)), full(main["wwit"]),
                  pl.BlockSpec(memory_space=pl.ANY)],
        out_specs=[tok(n) for n, _ in outs] + [pl.BlockSpec((1, IDX_HEADS, tm), lambda s, b: (b, 0, s)),
                                                tok(d), tok(d)],
        out_shape=[jax.ShapeDtypeStruct((bsz, sc, n), dt) for n, dt in outs]
        + [jax.ShapeDtypeStruct((bsz, IDX_HEADS, sc), F32),
           jax.ShapeDtypeStruct((bsz, sc, d), BF16), jax.ShapeDtypeStruct((bsz, sc, d), BF16)],
        compiler_params=_cparams(("arbitrary", "arbitrary")),
        name="inproj",
    )(x, g, main["w"], main["wr"], main["cs"], main["sn"], main["wwit"], after)


def _s5_kernel(u_ref, sre_in, sim_in, bre_ref, bim_ref, cre_ref, cim_ref, are_ref, aim_ref, dsk_ref, wglu_ref,
               y_ref, st_re, st_im, sre, sim, *, tc, nb, lane_chunk):
    @pl.when(pl.program_id(0) == 0)
    def _():
        st_re[...] = sre_in[...]
        st_im[...] = sim_in[...]

    u = u_ref[...]
    n_half = bre_ref.shape[0]
    hin = bre_ref.shape[1]
    hst = bre_ref.shape[2]
    for h in range(n_half):
        uh = u[:, h * hin:(h + 1) * hin]
        sre[:, h * hst:(h + 1) * hst] = jnp.dot(uh, bre_ref[h], preferred_element_type=F32)
        sim[:, h * hst:(h + 1) * hst] = jnp.dot(uh, bim_ref[h], preferred_element_type=F32)

    n_state = sre.shape[1]
    for c in range(n_state // lane_chunk):
        cols = slice(c * lane_chunk, (c + 1) * lane_chunk)
        ar = are_ref[:, cols]
        ai = aim_ref[:, cols]

        def step(t, carry, cols=cols, ar=ar, ai=ai):
            sr, si = carry
            r0 = pl.multiple_of(t * nb, nb)
            nr = ar * sr - ai * si + sre[pl.ds(r0, nb), cols]
            ni = ar * si + ai * sr + sim[pl.ds(r0, nb), cols]
            sre[pl.ds(r0, nb), cols] = nr
            sim[pl.ds(r0, nb), cols] = ni
            return nr, ni

        sr, si = lax.fori_loop(0, tc, step, (st_re[:, cols], st_im[:, cols]), unroll=4)
        st_re[:, cols] = sr
        st_im[:, cols] = si

    ys = []
    for h in range(n_half):
        srh = sre[:, h * hst:(h + 1) * hst].astype(BF16)
        sih = sim[:, h * hst:(h + 1) * hst].astype(BF16)
        ys.append(jnp.dot(srh, cre_ref[h], preferred_element_type=F32)
                  - jnp.dot(sih, cim_ref[h], preferred_element_type=F32))
    y = jnp.concatenate(ys, axis=-1) + dsk_ref[...] * u.astype(F32)
    y = _gelu_tanh(y)
    gate = jnp.dot(y.astype(BF16), wglu_ref[...], preferred_element_type=F32)
    y_ref[...] = (y * _sigmoid(gate)).astype(BF16)


def _s5_params(a_re, a_im, log_dt, b_re, b_im, c_re, c_im, d_skip, w_glu, nb):
    groups = a_re.shape[0]
    d_ssm = groups * SSM_GROUP
    n_state = groups * SSM_STATE
    lam = lax.complex(a_re, a_im)
    dt = jnp.exp(log_dt)[:, None]
    a_bar = jnp.exp(lam * dt)
    b_bar = ((a_bar - 1.0) / lam)[..., None] * lax.complex(b_re, b_im)
    gh = min(groups, 256 // SSM_GROUP)
    n_half = groups // gh
    eye = jnp.eye(gh, dtype=F32)

    def bmat(bb):
        b4 = bb.reshape(n_half, gh, SSM_STATE, SSM_GROUP)
        return jnp.einsum('hgpc,gk->hgckp', b4, eye).reshape(n_half, gh * SSM_GROUP, gh * SSM_STATE)

    def cmat(cc):
        c4 = cc.reshape(n_half, gh, SSM_GROUP, SSM_STATE)
        return jnp.einsum('hgcp,gk->hgpkc', c4, eye).reshape(n_half, gh * SSM_STATE, gh * SSM_GROUP)

    return (bmat(jnp.real(b_bar)).astype(BF16), bmat(jnp.imag(b_bar)).astype(BF16),
            cmat(c_re).astype(BF16), cmat(c_im).astype(BF16),
            jnp.broadcast_to(jnp.real(a_bar).reshape(1, n_state), (nb, n_state)),
            jnp.broadcast_to(jnp.imag(a_bar).reshape(1, n_state), (nb, n_state)),
            d_skip.reshape(1, d_ssm), w_glu.astype(BF16))


def _s5(u_tb, st_re, st_im, params, nb, tc):
    rows, d_ssm = u_tb.shape
    n_state = st_re.shape[1]
    blk = tc * nb
    full = lambda a: pl.BlockSpec(a.shape, lambda i: (0,) * a.ndim)
    st_spec = pl.BlockSpec((nb, n_state), lambda i: (0, 0))
    kern = functools.partial(_s5_kernel, tc=tc, nb=nb, lane_chunk=512)
    return pl.pallas_call(
        kern,
        grid=(rows // blk,),
        in_specs=[pl.BlockSpec((blk, d_ssm), lambda i: (i, 0)), st_spec, st_spec] + [full(p) for p in params],
        out_specs=[pl.BlockSpec((blk, d_ssm), lambda i: (i, 0)), st_spec, st_spec],
        out_shape=[jax.ShapeDtypeStruct((rows, d_ssm), BF16),
                   jax.ShapeDtypeStruct((nb, n_state), F32), jax.ShapeDtypeStruct((nb, n_state), F32)],
        scratch_shapes=[pltpu.VMEM((blk, n_state), F32), pltpu.VMEM((blk, n_state), F32)],
        compiler_params=_cparams(("arbitrary",)),
        name="s5",
    )(u_tb, st_re, st_im, *params)


PART_ROWS = 32


def _dsa_kernel(qi_ref, wit_ref, q_ref, ki_ref, k_ref, vt_ref, o_ref, key_s, bias_s, lg_s,
                *, qb0, tq, kt, sub, topk, seq_bits):
    qb = pl.program_id(1) + qb0
    nkt = ((qb * tq + tq + sub * kt - 1) // (sub * kt)) * sub
    q_pos = qb * tq + lax.broadcasted_iota(I32, (1, tq), 1)
    k_eff = jnp.minimum(topk, q_pos + 1).astype(F32)

    qi = qi_ref[0]
    wit = wit_ref[0]
    qih = [qi[:, h * IDX_DIM:(h + 1) * IDX_DIM] for h in range(IDX_HEADS)]

    def key_pos(t):
        return t * kt + lax.broadcasted_iota(I32, (kt, tq), 0)

    def score_tile(t, _):
        r0 = pl.multiple_of(t * kt, kt)
        ki_t = ki_ref[0, pl.ds(r0, kt), :]
        sc = jnp.zeros((kt, tq), F32)
        for h in range(IDX_HEADS):
            rel = lax.dot_general(ki_t, qih[h], (((1,), (1,)), ((), ())), preferred_element_type=F32)
            sc = sc + jnp.maximum(rel, 0.0) * wit[h:h + 1, :]
        bits = lax.bitcast_convert_type(sc, I32)
        key = jnp.where(bits < 0, bits ^ jnp.int32(0x7FFFFFFF), bits)
        key = jnp.where(key_pos(t) <= q_pos, key, jnp.int32(INT_MIN))
        key_s[pl.ds(r0, kt), :] = key
        return 0

    lax.fori_loop(0, nkt, score_tile, 0)

    def count(pred_fn):
        def body(t, acc):
            r0 = pl.multiple_of(t * kt, kt)
            m = pred_fn(key_s[pl.ds(r0, kt), :], t)
            ones = jnp.where(m, 1.0, 0.0).reshape(kt // PART_ROWS, PART_ROWS, tq)
            return acc + jnp.sum(ones, axis=0)
        acc = lax.fori_loop(0, nkt, body, jnp.zeros((PART_ROWS, tq), F32))
        return jnp.sum(acc, axis=0, keepdims=True)

    def bit_step(i, u):
        bit = jnp.left_shift(jnp.int32(1), 31 - i)
        cand_u = u | bit
        cand_s = cand_u ^ jnp.int32(INT_MIN)
        cnt = count(lambda kk, t: kk >= cand_s)
        return jnp.where(cnt >= k_eff, cand_u, u)

    u_thr = lax.fori_loop(0, 32, bit_step, jnp.zeros((1, tq), I32))
    thr = u_thr ^ jnp.int32(INT_MIN)

    cnt_ge = count(lambda kk, t: kk >= thr)
    cnt_gt = count(lambda kk, t: kk > thr)
    need_eq = k_eff - cnt_gt
    has_tie = jnp.max(cnt_ge - k_eff) > 0.0

    def tie_cut():
        def pos_step(i, c):
            bit = jnp.left_shift(jnp.int32(1), seq_bits - 1 - i)
            cand = c | bit
            cnt = count(lambda kk, t: (kk == thr) & (key_pos(t) < cand))
            return jnp.where(cnt < need_eq, cand, c)
        return lax.fori_loop(0, seq_bits, pos_step, jnp.zeros((1, tq), I32))

    cut = lax.cond(has_tie, tie_cut, lambda: jnp.full((1, tq), 2 ** seq_bits, I32))

    def bias_tile(t, _):
        r0 = pl.multiple_of(t * kt, kt)
        key = key_s[pl.ds(r0, kt), :]
        sel = (key > thr) | ((key == thr) & (key_pos(t) <= cut))
        bias_s[pl.ds(r0, kt), :] = jnp.where(sel, 0.0, NEG_BIG)
        return 0

    lax.fori_loop(0, nkt, bias_tile, 0)

    q = q_ref[0]
    grp = ATTN_HEADS // ATTN_KV_HEADS
    pairs_per_kv = grp // 2
    n_unit = ATTN_KV_HEADS * pairs_per_kv
    wq = 2 * tq
    qpair = [jnp.concatenate([q[:, (2 * u) * HEAD_DIM:(2 * u + 1) * HEAD_DIM],
                              q[:, (2 * u + 1) * HEAD_DIM:(2 * u + 2) * HEAD_DIM]], axis=0)
             for u in range(n_unit)]

    def col_reduce(x, op):
        part = op(x.reshape(kt // PART_ROWS, PART_ROWS, wq), axis=0)
        return op(part, axis=0, keepdims=True)

    def attn_tile(t, carry):
        ms, ls, accs = list(carry[0]), list(carry[1]), list(carry[2])
        for hf in range(sub):
            r0 = pl.multiple_of((t * sub + hf) * kt, kt)
            bias = bias_s[pl.ds(r0, kt), :]
            bias2 = jnp.concatenate([bias, bias], axis=1)
            for u in range(n_unit):
                k_t = k_ref[0, u // pairs_per_kv, pl.ds(r0, kt), :]
                lg_s[hf, u] = lax.dot_general(k_t, qpair[u], (((1,), (1,)), ((), ())),
                                              preferred_element_type=F32) + bias2
        for hf in range(sub):
            for u in range(n_unit):
                lg = lg_s[hf, u]
                m_new = jnp.maximum(ms[u], col_reduce(lg, jnp.max))
                p = jnp.exp2(lg - m_new)
                alpha = jnp.exp2(ms[u] - m_new)
                v_t = vt_ref[0, u // pairs_per_kv, t * sub + hf]
                ls[u] = alpha * ls[u] + col_reduce(p, jnp.sum)
                accs[u] = alpha * accs[u] + jnp.dot(v_t, p.astype(BF16), preferred_element_type=F32)
                ms[u] = m_new
        return tuple(ms), tuple(ls), tuple(accs)

    init = (tuple(jnp.full((1, wq), NEG_BIG, F32) for _ in range(n_unit)),
            tuple(jnp.zeros((1, wq), F32) for _ in range(n_unit)),
            tuple(jnp.zeros((HEAD_DIM, wq), F32) for _ in range(n_unit)))
    _, ls, accs = lax.fori_loop(0, nkt // sub, attn_tile, init)
    for n in range(ATTN_KV_HEADS):
        o_ref[0, 0, n] = jnp.concatenate([accs[n * pairs_per_kv + pg] / ls[n * pairs_per_kv + pg]
                                          for pg in range(pairs_per_kv)], axis=1).astype(BF16)


def _dsa(q, qi, wit, ki, k4, vt, s0, tq, kt):
    bsz, sc, _ = q.shape
    seq = ki.shape[1]
    topk = min(TOPK_MAX, seq // 4)
    nqb = sc // tq
    grp = ATTN_HEADS // ATTN_KV_HEADS
    seq_bits = int(math.log2(seq))
    assert 2 ** seq_bits == seq
    sub = 2 if seq % (2 * kt) == 0 else 1
    n_unit = ATTN_HEADS // 2
    kern = functools.partial(_dsa_kernel, qb0=s0 // tq, tq=tq, kt=kt, sub=sub, topk=topk, seq_bits=seq_bits)
    o_t = pl.pallas_call(
        kern,
        grid=(bsz, nqb),
        in_specs=[pl.BlockSpec((1, tq, IDX_HEADS * IDX_DIM), lambda b, j: (b, j, 0)),
                  pl.BlockSpec((1, IDX_HEADS, tq), lambda b, j: (b, 0, j)),
                  pl.BlockSpec((1, tq, ATTN_HEADS * HEAD_DIM), lambda b, j: (b, j, 0)),
                  pl.BlockSpec((1, seq, IDX_DIM), lambda b, j: (b, 0, 0)),
                  pl.BlockSpec((1, ATTN_KV_HEADS, seq, HEAD_DIM), lambda b, j: (b, 0, 0, 0)),
                  pl.BlockSpec((1, ATTN_KV_HEADS, seq // kt, HEAD_DIM, kt), lambda b, j: (b, 0, 0, 0, 0))],
        out_specs=pl.BlockSpec((1, 1, ATTN_KV_HEADS, HEAD_DIM, grp * tq), lambda b, j: (b, j, 0, 0, 0)),
        out_shape=jax.ShapeDtypeStruct((bsz, nqb, ATTN_KV_HEADS, HEAD_DIM, grp * tq), BF16),
        scratch_shapes=[pltpu.VMEM((seq, tq), I32), pltpu.VMEM((seq, tq), F32),
                        pltpu.VMEM((sub, n_unit, kt, 2 * tq), F32)],
        compiler_params=_cparams(("arbitrary", "arbitrary")),
        name="dsa",
    )(qi, wit, q, ki, k4, vt)
    o = o_t.reshape(bsz, nqb, ATTN_KV_HEADS, HEAD_DIM, grp, tq).transpose(0, 1, 5, 2, 4, 3)
    return o.reshape(bsz, sc, ATTN_HEADS * HEAD_DIM)


def _merge_kernel(x_ref, ys_ref, ya_ref, gs_ref, ga_ref, wsu_ref, wau_ref, wout_ref, g2_ref, wq_ref,
                  h_ref, hn_ref, qp_ref):
    ms = jnp.dot(ys_ref[0], wsu_ref[...], preferred_element_type=F32)
    ma = jnp.dot(ya_ref[0], wau_ref[...], preferred_element_type=F32)
    merged = gs_ref[0].astype(F32) * ms + ga_ref[0].astype(F32) * ma
    h = x_ref[0] + jnp.dot(merged.astype(BF16), wout_ref[...], preferred_element_type=F32)
    h_ref[0] = h
    hb = _rms(h, g2_ref[...]).astype(BF16)
    bits = lax.bitcast_convert_type(hb.astype(F32), I32)
    half = bits.shape[1] // 2
    hn_ref[0] = (bits[:, half:] & jnp.int32(-65536)) | lax.shift_right_logical(bits[:, :half], 16)
    qp_ref[0] = jnp.dot(hb, wq_ref[...], preferred_element_type=F32).astype(BF16)


def _merge(x, ys, ya, gs, ga, wsu, wau, wo, norm2_g, wq, s0, tm):
    bsz, sc, _ = ya.shape
    d = x.shape[2]
    i0 = s0 // tm
    loc = lambda a: pl.BlockSpec((1, tm, a.shape[2]), lambda b, i: (b, i, 0))
    full = lambda a: pl.BlockSpec(a.shape, lambda b, i: (0,) * a.ndim)
    g2 = norm2_g.reshape(1, d)
    nq = wq.shape[1]
    out = lambda n: pl.BlockSpec((1, tm, n), lambda b, i: (b, i, 0))
    return pl.pallas_call(
        _merge_kernel,
        grid=(bsz, sc // tm),
        in_specs=[pl.BlockSpec((1, tm, d), lambda b, i: (b, i + i0, 0)), loc(ys), loc(ya), loc(gs), loc(ga),
                  full(wsu), full(wau), full(wo), full(g2), full(wq)],
        out_specs=[out(d), out(d // 2), out(nq)],
        out_shape=[jax.ShapeDtypeStruct((bsz, sc, d), F32), jax.ShapeDtypeStruct((bsz, sc, d // 2), I32),
                   jax.ShapeDtypeStruct((bsz, sc, nq), BF16)],
        compiler_params=_cparams(("arbitrary", "arbitrary")),
        name="merge",
    )(x, ys, ya, gs, ga, wsu, wau, wo, g2, wq)


def _cand_layout():
    blocks = []
    blocks.append((0, 16, 16))
    for i in range(1, 8):
        blocks.append((i, 8, PEER_TOPK // (i + 1)))
    blocks.append((None, 8, 8))
    return blocks


def _top_rows(s, order, payload, k):
    big = jnp.float32(3e38)
    vals, pays = [], []
    for _ in range(k):
        m = jnp.max(s, axis=0, keepdims=True)
        o = jnp.min(jnp.where(s == m, order, big), axis=0, keepdims=True)
        hit = order == o
        pays.append(jnp.min(jnp.where(hit, payload, big), axis=0, keepdims=True))
        vals.append(m)
        s = jnp.where(hit, -jnp.inf, s)
    return jnp.concatenate(vals, axis=0), jnp.concatenate(pays, axis=0)


def _route_kernel(qp_ref, k1_ref, k2_ref, e_ref, g_ref, *, tt):
    qp = qp_ref[...]
    kd = PEER_KEY_DIM
    rows_k = lax.broadcasted_iota(I32, (PEER_KEYS, tt), 0).astype(F32)
    for h in range(PEER_HEADS):
        q1 = qp[:, (2 * h) * kd:(2 * h + 1) * kd]
        q2 = qp[:, (2 * h + 1) * kd:(2 * h + 2) * kd]
        s1 = lax.dot_general(k1_ref[h], q1, (((1,), (1,)), ((), ())), preferred_element_type=F32)
        s2 = lax.dot_general(k2_ref[h], q2, (((1,), (1,)), ((), ())), preferred_element_type=F32)
        v1, i1 = _top_rows(s1, rows_k, rows_k, PEER_TOPK)
        v2, i2 = _top_rows(s2, rows_k, rows_k, PEER_TOPK)
        cs, ce, co = [], [], []
        for i, rows, valid in _cand_layout():
            r = lax.broadcasted_iota(I32, (rows, tt), 0).astype(F32)
            if i is None:
                val = v1[8:16] + v2[0:1]
                eid = i1[8:16] * PEER_KEYS + i2[0:1]
                flat = (r + 8.0) * PEER_TOPK
            else:
                val = v1[i:i + 1] + v2[0:rows]
                eid = i1[i:i + 1] * PEER_KEYS + i2[0:rows]
                flat = r + float(i * PEER_TOPK)
                if valid < rows:
                    val = jnp.where(r < float(valid), val, -jnp.inf)
            cs.append(val)
            ce.append(eid)
            co.append(flat)
        cand = jnp.concatenate(cs, axis=0)
        top_s, top_e = _top_rows(cand, jnp.concatenate(co, axis=0), jnp.concatenate(ce, axis=0), PEER_TOPK)
        p = jnp.exp(top_s - top_s[0:1])
        gates = p / jnp.sum(p, axis=0, keepdims=True)
        e_ref[h * PEER_TOPK:(h + 1) * PEER_TOPK, :] = top_e.astype(I32)
        g_ref[h * PEER_TOPK:(h + 1) * PEER_TOPK, :] = gates


def _route(qp, k1, k2, tt):
    n, nq = qp.shape
    n_sel = PEER_HEADS * PEER_TOPK
    full = lambda a: pl.BlockSpec(a.shape, lambda i: (0,) * a.ndim)
    return pl.pallas_call(
        functools.partial(_route_kernel, tt=tt),
        grid=(n // tt,),
        in_specs=[pl.BlockSpec((tt, nq), lambda i: (i, 0)), full(k1), full(k2)],
        out_specs=[pl.BlockSpec((n_sel, tt), lambda i: (0, i)), pl.BlockSpec((n_sel, tt), lambda i: (0, i))],
        out_shape=[jax.ShapeDtypeStruct((n_sel, n), I32), jax.ShapeDtypeStruct((n_sel, n), F32)],
        compiler_params=_cparams(("arbitrary",)),
        name="route",
    )(qp, k1, k2)


def _final_kernel(h_ref, p_ref, g_ref, o_ref):
    o_ref[...] = _rms(h_ref[...] + p_ref[...], g_ref[...])


def _final(h, p, g, tm):
    n, d = h.shape
    row = pl.BlockSpec((tm, d), lambda i: (i, 0))
    return pl.pallas_call(
        _final_kernel,
        grid=(n // tm,),
        in_specs=[row, row, pl.BlockSpec((1, d), lambda i: (0, 0))],
        out_specs=row,
        out_shape=jax.ShapeDtypeStruct((n, d), F32),
        compiler_params=_cparams(("arbitrary",)),
        name="final",
    )(h, p, g.reshape(1, d))


SC_CORES_V7X = 2
SC_SUBCORES_V7X = 16
SC_LANES_V7X = 16
PEER_TOK_BATCH = 16
PEER_ROW_CHUNK = 32
PEER_RING = 4


def _pack_bf16_pairs(t):
    half = t.shape[1] // 2
    tb = t.astype(BF16)
    lo = lax.bitcast_convert_type(tb[:, :half], jnp.uint16).astype(jnp.uint32)
    hi = lax.bitcast_convert_type(tb[:, half:], jnp.uint16).astype(jnp.uint32)
    return lax.bitcast_convert_type(lo | (hi << 16), I32)


def _unpack_pair(w):
    lo = lax.bitcast_convert_type(jnp.left_shift(w, 16), F32)
    hi = lax.bitcast_convert_type(w & jnp.int32(-65536), F32)
    return lo, hi


def _peer_sc_body(hn_hbm, e_hbm, g_hbm, u_hbm, v_hbm, out_hbm,
                  idx_v, gate_v, x_v, out_v, rows, p_v, act_v, sem, *, tpw, d, n_sel):
    nl = SC_LANES_V7X
    tb = PEER_TOK_BATCH
    rc = PEER_ROW_CHUNK
    n_chunk = n_sel // rc
    jobs_per_tok = 2 * n_chunk
    half = d // 2
    n_lane_blk = half // nl
    wid = lax.axis_index("s") * SC_CORES_V7X + lax.axis_index("c")
    base = wid * tpw
    lane = lax.iota(I32, nl)
    zero = jnp.zeros((nl,), F32)
    c_gelu = 2.0 * math.sqrt(2.0 / math.pi)

    def gather_copy(tab_hbm, job):
        tok = job // jobs_per_tok
        c = (job % jobs_per_tok) % n_chunk
        b = job % PEER_RING
        return pltpu.make_async_copy(tab_hbm.at[idx_v.at[tok, pl.ds(c * rc, rc)]], rows.at[b], sem.at[b])

    def start(job):
        j = job % jobs_per_tok

        @pl.when(j < n_chunk)
        def _():
            gather_copy(u_hbm, job).start()

        @pl.when(j >= n_chunk)
        def _():
            gather_copy(v_hbm, job).start()

    def compute_u(tok, c, b):
        def rg_body(rg, _):
            r0 = rg * 8

            def jbody(j2, accs):
                off0 = j2 * (2 * nl)
                off1 = off0 + nl
                x0 = plsc.bitcast(x_v[tok, pl.ds(off0, nl)], BF16)
                x1 = plsc.bitcast(x_v[tok, pl.ds(off1, nl)], BF16)
                new = []
                for r in range(8):
                    w0 = plsc.bitcast(rows[b, r0 + r, pl.ds(off0, nl)], BF16)
                    w1 = plsc.bitcast(rows[b, r0 + r, pl.ds(off1, nl)], BF16)
                    lo, hi = _unpack_pair(plsc.bitcast(w0 * x0 + w1 * x1, I32))
                    new.append(accs[r] + (lo + hi))
                return tuple(new)

            accs = lax.fori_loop(0, n_lane_blk // 2, jbody, (zero,) * 8)
            for r in range(8):
                p_v[c * rc + r0 + r, :] = accs[r]
            return 0

        lax.fori_loop(0, rc // 8, rg_body, 0)

    def finish_act(tok):
        def eg_body(eg, _):
            e0 = eg * nl
            ridx = e0 + lane
            s = zero
            for l in range(nl):
                s = s + plsc.load_gather(p_v, [ridx, jnp.full((nl,), l, I32)])
            inner = c_gelu * (s + 0.044715 * (s * s * s))
            gl = s / (1.0 + jnp.exp(-inner))
            a = gl * gate_v[tok, pl.ds(e0, nl)]
            bits = lax.bitcast_convert_type(a, I32)
            rnd = bits + jnp.int32(0x7FFF) + (lax.shift_right_logical(bits, 16) & 1)
            hi16 = rnd & jnp.int32(-65536)
            act_v[pl.ds(e0, nl)] = hi16 | lax.shift_right_logical(hi16, 16)
            return 0

        lax.fori_loop(0, n_sel // nl, eg_body, 0)

        def zbody(j, _):
            out_v[tok, pl.ds(j * nl, nl)] = zero
            return 0

        lax.fori_loop(0, d // nl, zbody, 0, unroll=4)

    def compute_v(tok, c, b):
        def rg_body(rg, _):
            r0 = rg * nl
            splat = [plsc.bitcast(plsc.load_gather(act_v, [jnp.full((nl,), 0, I32) + (c * rc + r0 + r)]), BF16)
                     for r in range(nl)]

            def tree(parts):
                while len(parts) > 1:
                    parts = [parts[i] + parts[i + 1] for i in range(0, len(parts), 2)]
                return parts[0]

            @plsc.parallel_loop(0, n_lane_blk, unroll=2)
            def _(j):
                off = j * nl
                los, his = [], []
                for r in range(0, nl, 2):
                    w0 = plsc.bitcast(rows[b, r0 + r, pl.ds(off, nl)], BF16)
                    w1 = plsc.bitcast(rows[b, r0 + r + 1, pl.ds(off, nl)], BF16)
                    lo, hi = _unpack_pair(plsc.bitcast(w0 * splat[r] + w1 * splat[r + 1], I32))
                    los.append(lo)
                    his.append(hi)
                out_v[tok, pl.ds(off, nl)] = out_v[tok, pl.ds(off, nl)] + tree(los)
                out_v[tok, pl.ds(half + off, nl)] = out_v[tok, pl.ds(half + off, nl)] + tree(his)

            return 0

        lax.fori_loop(0, rc // nl, rg_body, 0)

    def batch_body(bi, _):
        t0 = base + bi * tb
        pltpu.sync_copy(e_hbm.at[pl.ds(t0, tb)], idx_v)
        pltpu.sync_copy(g_hbm.at[pl.ds(t0, tb)], gate_v)
        pltpu.sync_copy(hn_hbm.at[pl.ds(t0, tb)], x_v)
        for pre in range(PEER_RING - 1):
            start(pre)

        def job_body(job, _):
            @pl.when(job + (PEER_RING - 1) < tb * jobs_per_tok)
            def _():
                start(job + (PEER_RING - 1))

            j = job % jobs_per_tok
            gather_copy(u_hbm, job).wait()
            tok = job // jobs_per_tok
            b = job % PEER_RING

            @pl.when(j < n_chunk)
            def _():
                compute_u(tok, j, b)

            @pl.when(j == n_chunk - 1)
            def _():
                finish_act(tok)

            @pl.when(j >= n_chunk)
            def _():
                compute_v(tok, j - n_chunk, b)

            return 0

        lax.fori_loop(0, tb * jobs_per_tok, job_body, 0)
        pltpu.sync_copy(out_v, out_hbm.at[pl.ds(t0, tb)])
        return 0

    lax.fori_loop(0, tpw // tb, batch_body, 0)


def _peer_sc(x_pk, experts, gates, u_tab, v_tab):
    n = x_pk.shape[0]
    d = 2 * x_pk.shape[1]
    n_sel = experts.shape[1]
    nw = SC_CORES_V7X * SC_SUBCORES_V7X
    tpw = n // nw
    mesh = plsc.VectorSubcoreMesh(core_axis_name="c", subcore_axis_name="s",
                                  num_cores=SC_CORES_V7X, num_subcores=SC_SUBCORES_V7X)
    body = functools.partial(_peer_sc_body, tpw=tpw, d=d, n_sel=n_sel)
    call = pl.kernel(
        body,
        out_type=jax.ShapeDtypeStruct((n, d), F32),
        mesh=mesh,
        scratch_types=[pltpu.VMEM((PEER_TOK_BATCH, n_sel), I32),
                       pltpu.VMEM((PEER_TOK_BATCH, n_sel), F32),
                       pltpu.VMEM((PEER_TOK_BATCH, d // 2), I32),
                       pltpu.VMEM((PEER_TOK_BATCH, d), F32),
                       pltpu.VMEM((PEER_RING, PEER_ROW_CHUNK, d // 2), I32),
                       pltpu.VMEM((n_sel, SC_LANES_V7X), F32),
                       pltpu.VMEM((n_sel,), I32),
                       pltpu.SemaphoreType.DMA((PEER_RING,))],
        compiler_params=pltpu.CompilerParams(needs_layout_passes=False),
        name="peer_sc",
    )
    return call(x_pk, experts, gates, u_tab, v_tab)


def kernel(x, norm1_g, w_in, a_re, a_im, log_dt, b_re, b_im, c_re, c_im, d_skip, w_glu, w_ssm_up, w_attn_up,
           w_out, norm2_g, peer_wq, peer_k1, peer_k2, peer_u, peer_v, norm_f_g):
    bsz, seq, d = x.shape
    depth = norm1_g.shape[0]
    tm = min(ROW_TILE, seq)
    n_chunks = TIME_CHUNKS if seq % (TIME_CHUNKS * tm) == 0 else max(1, seq // tm)
    sc = seq // n_chunks
    h = x
    for layer in range(depth):
        last = layer + 1 == depth
        kv_w, main_w = _in_weights(w_in[layer], seq)
        s5p = _s5_params(a_re[layer], a_im[layer], log_dt[layer], b_re[layer], b_im[layer], c_re[layer],
                         c_im[layer], d_skip[layer], w_glu[layer], nb=bsz)
        n_state = s5p[4].shape[1]
        wsu = w_ssm_up[layer].astype(BF16)
        wau = w_attn_up[layer].astype(BF16)
        wo = w_out[layer].astype(BF16)
        wq = peer_wq[layer].astype(BF16)
        k1 = peer_k1[layer].astype(BF16)
        k2 = peer_k2[layer].astype(BF16)
        u_pk = _pack_bf16_pairs(peer_u[layer])
        v_pk = _pack_bf16_pairs(peer_v[layer])
        k4, vt, ki = _kvproj(h, norm1_g[layer], kv_w, tm=tm, kt=DSA_KT)
        st_re = jnp.zeros((bsz, n_state), F32)
        st_im = jnp.zeros((bsz, n_state), F32)
        outs = []
        after = ki
        for c in range(n_chunks):
            s0 = c * sc
            u, q, qi, wit, gs, ga = _inproj(h, norm1_g[layer], main_w, after, s0=s0, sc=sc, tm=tm)
            d_ssm = u.shape[-1]
            u_tb = u.transpose(1, 0, 2).reshape(sc * bsz, d_ssm)
            y_tb, st_re, st_im = _s5(u_tb, st_re, st_im, s5p, nb=bsz, tc=64)
            ys = y_tb.reshape(sc, bsz, d_ssm).transpose(1, 0, 2)
            ya = _dsa(q, qi, wit, ki, k4, vt, s0=s0, tq=DSA_TQ, kt=DSA_KT)
            hm, x_pk, qp = _merge(h, ys, ya, gs, ga, wsu, wau, wo, norm2_g[layer], wq, s0=s0, tm=tm)
            nt = bsz * sc
            e_t, g_t = _route(qp.reshape(nt, -1), k1, k2, tt=256)
            after = e_t
            po =_peer_sc(x_pk.reshape(nt, -1), e_t.T, g_t.T, u_pk, v_pk)
            hm2 = hm.reshape(nt, d)
            o = _final(hm2, po, norm_f_g, tm=tm) if last else hm2 + po
            outs.append(o.reshape(bsz, sc, d))
        h = jnp.concatenate(outs, axis=1)
    return h
```

```python
import functools
import math

import numpy as np
import jax
import jax.numpy as jnp
from jax import lax
from jax.experimental import pallas as pl
from jax.experimental.pallas import tpu as pltpu
from jax.experimental.pallas import tpu_sc as plsc

F32 = jnp.float32
BF16 = jnp.bfloat16
I32 = jnp.int32

SSM_GROUP = 16
SSM_STATE = 64
ATTN_HEADS = 8
ATTN_KV_HEADS = 2
HEAD_DIM = 64
IDX_HEADS = 8
IDX_DIM = 32
TOPK_MAX = 256
ROPE_THETA = 10000.0
NEG_BIG = -1e30
PEER_HEADS = 8
PEER_KEYS = 128
PEER_KEY_DIM = 128
PEER_TOPK = 16
NORM_EPS = 1e-6

TIME_CHUNKS = 8
SC_LAG = 2
ROW_TILE = 512
DSA_TQ = 128
DSA_KT = 256
INT_MIN = -(2 ** 31)
VMEM_LIMIT = 56 * 1024 * 1024


def _cparams(sem):
    return pltpu.CompilerParams(dimension_semantics=sem, vmem_limit_bytes=VMEM_LIMIT)


def _gelu_tanh(x):
    return 0.5 * x * (1.0 + jnp.tanh(math.sqrt(2.0 / math.pi) * (x + 0.044715 * (x * x * x))))


def _sigmoid(x):
    return 1.0 / (1.0 + jnp.exp(-x))


def _rms(x, g):
    return x * lax.rsqrt(jnp.mean(x * x, axis=-1, keepdims=True) + NORM_EPS) * g


def _rot_cols(w, hd):
    d, n = w.shape
    w3 = w.reshape(d, n // hd, hd)
    half = hd // 2
    return jnp.concatenate([-w3[..., half:], w3[..., :half]], axis=-1).reshape(d, n)


def _rope_full(seq, hd, heads):
    pos = jnp.arange(seq, dtype=F32)
    inv = ROPE_THETA ** (-jnp.arange(0, hd, 2, dtype=F32) / hd)
    ang = pos[:, None] * inv[None, :]
    c = jnp.concatenate([jnp.cos(ang), jnp.cos(ang)], axis=-1)
    s = jnp.concatenate([jnp.sin(ang), jnp.sin(ang)], axis=-1)
    return jnp.tile(c, (1, heads)), jnp.tile(s, (1, heads))


def _in_weights(w_in, seq):
    d = w_in.shape[0]
    d_ssm = d // 2
    d_q = ATTN_HEADS * HEAD_DIM
    d_kv = ATTN_KV_HEADS * HEAD_DIM
    d_qi = IDX_HEADS * IDX_DIM
    splits = (d_ssm, d_q, d_kv, d_kv, d_qi, IDX_DIM, IDX_HEADS, d, d)
    offs = np.cumsum(splits)[:-1].tolist()
    wu, wq, wk, wv, wqi, wki, wwi, wgs, wga = jnp.split(w_in, offs, axis=1)
    pad = jnp.zeros((d, 128 - IDX_DIM), F32)
    cq, sq = _rope_full(seq, HEAD_DIM, ATTN_HEADS)
    ck, sk = _rope_full(seq, HEAD_DIM, ATTN_KV_HEADS)
    cqi, sqi = _rope_full(seq, IDX_DIM, IDX_HEADS)
    cki, ski = _rope_full(seq, IDX_DIM, 1)
    tpad = jnp.zeros((seq, 128 - IDX_DIM), F32)
    kv = dict(
        w=jnp.concatenate([wk, wki, pad], axis=1).astype(BF16),
        wvt=wv.T.astype(BF16),
        wr=jnp.concatenate([_rot_cols(wk, HEAD_DIM), _rot_cols(wki, IDX_DIM), pad], axis=1).astype(BF16),
        cs=jnp.concatenate([ck, cki, tpad], axis=1), sn=jnp.concatenate([sk, ski, tpad], axis=1))
    main = dict(
        w=jnp.concatenate([wu, wq, wqi, wgs, wga], axis=1).astype(BF16),
        wr=jnp.concatenate([_rot_cols(wq, HEAD_DIM), _rot_cols(wqi, IDX_DIM)], axis=1).astype(BF16),
        cs=jnp.concatenate([cq, cqi], axis=1), sn=jnp.concatenate([sq, sqi], axis=1),
        wwit=wwi.T.astype(BF16))
    return kv, main


def _kvproj_kernel(x_ref, g_ref, w_ref, wr_ref, wvt_ref, cs_ref, sn_ref, k_ref, vt_ref, ki_ref, *, d_kv, kt):
    xb = _rms(x_ref[0], g_ref[...]).astype(BF16)

    def mm(ref, lo, n):
        return jnp.dot(xb, ref[:, lo:lo + n], preferred_element_type=F32)

    k = mm(w_ref, 0, d_kv) * cs_ref[:, :d_kv] + mm(wr_ref, 0, d_kv) * sn_ref[:, :d_kv]
    for n in range(ATTN_KV_HEADS):
        k_ref[0, n] = k[:, n * HEAD_DIM:(n + 1) * HEAD_DIM].astype(BF16)
    vt = lax.dot_general(wvt_ref[...], xb, (((1,), (1,)), ((), ())), preferred_element_type=F32)
    for n in range(ATTN_KV_HEADS):
        for j in range(vt.shape[1] // kt):
            vt_ref[0, n, j] = vt[n * HEAD_DIM:(n + 1) * HEAD_DIM, j * kt:(j + 1) * kt].astype(BF16)
    kiw = (mm(w_ref, d_kv, 128) * cs_ref[:, d_kv:d_kv + 128]
           + mm(wr_ref, d_kv, 128) * sn_ref[:, d_kv:d_kv + 128])
    ki_ref[0] = kiw[:, :IDX_DIM].astype(BF16)


def _kvproj(x, norm_g, kv, tm, kt):
    bsz, seq, d = x.shape
    d_kv = ATTN_KV_HEADS * HEAD_DIM
    full = lambda a: pl.BlockSpec(a.shape, lambda s, b: (0,) * a.ndim)
    g = norm_g.reshape(1, d)
    ncs = kv["cs"].shape[1]
    return pl.pallas_call(
        functools.partial(_kvproj_kernel, d_kv=d_kv, kt=kt),
        grid=(seq // tm, bsz),
        in_specs=[pl.BlockSpec((1, tm, d), lambda s, b: (b, s, 0)), full(g), full(kv["w"]), full(kv["wr"]),
                  full(kv["wvt"]),
                  pl.BlockSpec((tm, ncs), lambda s, b: (s, 0)), pl.BlockSpec((tm, ncs), lambda s, b: (s, 0))],
        out_specs=[pl.BlockSpec((1, ATTN_KV_HEADS, tm, HEAD_DIM), lambda s, b: (b, 0, s, 0)),
                   pl.BlockSpec((1, ATTN_KV_HEADS, tm // kt, HEAD_DIM, kt), lambda s, b: (b, 0, s, 0, 0)),
                   pl.BlockSpec((1, tm, IDX_DIM), lambda s, b: (b, s, 0))],
        out_shape=[jax.ShapeDtypeStruct((bsz, ATTN_KV_HEADS, seq, HEAD_DIM), BF16),
                   jax.ShapeDtypeStruct((bsz, ATTN_KV_HEADS, seq // kt, HEAD_DIM, kt), BF16),
                   jax.ShapeDtypeStruct((bsz, seq, IDX_DIM), BF16)],
        compiler_params=_cparams(("arbitrary", "arbitrary")),
        name="kvproj",
    )(x, g, kv["w"], kv["wr"], kv["wvt"], kv["cs"], kv["sn"])


def _inproj_kernel(x_ref, g_ref, w_ref, wr_ref, cs_ref, sn_ref, wwit_ref, after_tc_ref, after_sc_ref,
                   u_ref, q_ref, qi_ref, wit_ref, gs_ref, ga_ref, *, d_ssm, d_q, d_qi, d_model, q_scale, wi_scale):
    del after_tc_ref, after_sc_ref
    xb = _rms(x_ref[0], g_ref[...]).astype(BF16)

    def mm(ref, lo, n):
        return jnp.dot(xb, ref[:, lo:lo + n], preferred_element_type=F32)

    o = 0
    u_ref[0] = mm(w_ref, o, d_ssm).astype(BF16)
    o += d_ssm
    q = mm(w_ref, o, d_q) * cs_ref[:, :d_q] + mm(wr_ref, 0, d_q) * sn_ref[:, :d_q]
    q_ref[0] = (q * q_scale).astype(BF16)
    o += d_q
    qi = mm(w_ref, o, d_qi) * cs_ref[:, d_q:d_q + d_qi] + mm(wr_ref, d_q, d_qi) * sn_ref[:, d_q:d_q + d_qi]
    qi_ref[0] = qi.astype(BF16)
    o += d_qi
    gs_ref[0] = _sigmoid(mm(w_ref, o, d_model)).astype(BF16)
    o += d_model
    ga_ref[0] = _sigmoid(mm(w_ref, o, d_model)).astype(BF16)
    wit_ref[0] = lax.dot_general(wwit_ref[...], xb, (((1,), (1,)), ((), ())),
                                 preferred_element_type=F32) * wi_scale


def _inproj(x, norm_g, main, after, s0, sc, tm):
    bsz, _, d = x.shape
    d_ssm = d // 2
    d_q = ATTN_HEADS * HEAD_DIM
    d_qi = IDX_HEADS * IDX_DIM
    i0 = s0 // tm
    kern = functools.partial(
        _inproj_kernel, d_ssm=d_ssm, d_q=d_q, d_qi=d_qi, d_model=d,
        q_scale=HEAD_DIM ** -0.5 * math.log2(math.e), wi_scale=(IDX_HEADS ** -0.5) * (IDX_DIM ** -0.5))
    tok = lambda n: pl.BlockSpec((1, tm, n), lambda s, b: (b, s, 0))
    full = lambda a: pl.BlockSpec(a.shape, lambda s, b: (0,) * a.ndim)
    g = norm_g.reshape(1, d)
    ncs = main["cs"].shape[1]
    outs = [(d_ssm, BF16), (d_q, BF16), (d_qi, BF16)]
    return pl.pallas_call(
        kern,
        grid=(sc // tm, bsz),
        in_specs=[pl.BlockSpec((1, tm, d), lambda s, b: (b, s + i0, 0)), full(g), full(main["w"]), full(main["wr"]),
                  pl.BlockSpec((tm, ncs), lambda s, b: (s + i0, 0)),
                  pl.BlockSpec((tm, ncs), lambda s, b: (s + i0, 0)), full(main["wwit"]),
                  pl.BlockSpec(memory_space=pl.ANY), pl.BlockSpec(memory_space=pl.ANY)],
        out_specs=[tok(n) for n, _ in outs] + [pl.BlockSpec((1, IDX_HEADS, tm), lambda s, b: (b, 0, s)),
                                                tok(d), tok(d)],
        out_shape=[jax.ShapeDtypeStruct((bsz, sc, n), dt) for n, dt in outs]
        + [jax.ShapeDtypeStruct((bsz, IDX_HEADS, sc), F32),
           jax.ShapeDtypeStruct((bsz, sc, d), BF16), jax.ShapeDtypeStruct((bsz, sc, d), BF16)],
        compiler_params=_cparams(("arbitrary", "arbitrary")),
        name="inproj",
    )(x, g, main["w"], main["wr"], main["cs"], main["sn"], main["wwit"], *after)


def _s5_kernel(u_ref, sre_in, sim_in, bre_ref, bim_ref, cre_ref, cim_ref, are_ref, aim_ref, dsk_ref, wglu_ref,
               y_ref, st_re, st_im, sre, sim, *, tc, nb, lane_chunk):
    @pl.when(pl.program_id(0) == 0)
    def _():
        st_re[...] = sre_in[...]
        st_im[...] = sim_in[...]

    u = u_ref[...]
    n_half = bre_ref.shape[0]
    hin = bre_ref.shape[1]
    hst = bre_ref.shape[2]
    for h in range(n_half):
        uh = u[:, h * hin:(h + 1) * hin]
        sre[:, h * hst:(h + 1) * hst] = jnp.dot(uh, bre_ref[h], preferred_element_type=F32)
        sim[:, h * hst:(h + 1) * hst] = jnp.dot(uh, bim_ref[h], preferred_element_type=F32)

    n_state = sre.shape[1]
    for c in range(n_state // lane_chunk):
        cols = slice(c * lane_chunk, (c + 1) * lane_chunk)
        ar = are_ref[:, cols]
        ai = aim_ref[:, cols]

        def step(t, carry, cols=cols, ar=ar, ai=ai):
            sr, si = carry
            r0 = pl.multiple_of(t * nb, nb)
            nr = ar * sr - ai * si + sre[pl.ds(r0, nb), cols]
            ni = ar * si + ai * sr + sim[pl.ds(r0, nb), cols]
            sre[pl.ds(r0, nb), cols] = nr
            sim[pl.ds(r0, nb), cols] = ni
            return nr, ni

        sr, si = lax.fori_loop(0, tc, step, (st_re[:, cols], st_im[:, cols]), unroll=4)
        st_re[:, cols] = sr
        st_im[:, cols] = si

    ys = []
    for h in range(n_half):
        srh = sre[:, h * hst:(h + 1) * hst].astype(BF16)
        sih = sim[:, h * hst:(h + 1) * hst].astype(BF16)
        ys.append(jnp.dot(srh, cre_ref[h], preferred_element_type=F32)
                  - jnp.dot(sih, cim_ref[h], preferred_element_type=F32))
    y = jnp.concatenate(ys, axis=-1) + dsk_ref[...] * u.astype(F32)
    y = _gelu_tanh(y)
    gate = jnp.dot(y.astype(BF16), wglu_ref[...], preferred_element_type=F32)
    y_ref[...] = (y * _sigmoid(gate)).astype(BF16)


def _s5_params(a_re, a_im, log_dt, b_re, b_im, c_re, c_im, d_skip, w_glu, nb):
    groups = a_re.shape[0]
    d_ssm = groups * SSM_GROUP
    n_state = groups * SSM_STATE
    lam = lax.complex(a_re, a_im)
    dt = jnp.exp(log_dt)[:, None]
    a_bar = jnp.exp(lam * dt)
    b_bar = ((a_bar - 1.0) / lam)[..., None] * lax.complex(b_re, b_im)
    gh = min(groups, 256 // SSM_GROUP)
    n_half = groups // gh
    eye = jnp.eye(gh, dtype=F32)

    def bmat(bb):
        b4 = bb.reshape(n_half, gh, SSM_STATE, SSM_GROUP)
        return jnp.einsum('hgpc,gk->hgckp', b4, eye).reshape(n_half, gh * SSM_GROUP, gh * SSM_STATE)

    def cmat(cc):
        c4 = cc.reshape(n_half, gh, SSM_GROUP, SSM_STATE)
        return jnp.einsum('hgcp,gk->hgpkc', c4, eye).reshape(n_half, gh * SSM_STATE, gh * SSM_GROUP)

    return (bmat(jnp.real(b_bar)).astype(BF16), bmat(jnp.imag(b_bar)).astype(BF16),
            cmat(c_re).astype(BF16), cmat(c_im).astype(BF16),
            jnp.broadcast_to(jnp.real(a_bar).reshape(1, n_state), (nb, n_state)),
            jnp.broadcast_to(jnp.imag(a_bar).reshape(1, n_state), (nb, n_state)),
            d_skip.reshape(1, d_ssm), w_glu.astype(BF16))


def _s5(u_tb, st_re, st_im, params, nb, tc):
    rows, d_ssm = u_tb.shape
    n_state = st_re.shape[1]
    blk = tc * nb
    full = lambda a: pl.BlockSpec(a.shape, lambda i: (0,) * a.ndim)
    st_spec = pl.BlockSpec((nb, n_state), lambda i: (0, 0))
    kern = functools.partial(_s5_kernel, tc=tc, nb=nb, lane_chunk=512)
    return pl.pallas_call(
        kern,
        grid=(rows // blk,),
        in_specs=[pl.BlockSpec((blk, d_ssm), lambda i: (i, 0)), st_spec, st_spec] + [full(p) for p in params],
        out_specs=[pl.BlockSpec((blk, d_ssm), lambda i: (i, 0)), st_spec, st_spec],
        out_shape=[jax.ShapeDtypeStruct((rows, d_ssm), BF16),
                   jax.ShapeDtypeStruct((nb, n_state), F32), jax.ShapeDtypeStruct((nb, n_state), F32)],
        scratch_shapes=[pltpu.VMEM((blk, n_state), F32), pltpu.VMEM((blk, n_state), F32)],
        compiler_params=_cparams(("arbitrary",)),
        name="s5",
    )(u_tb, st_re, st_im, *params)


PART_ROWS = 32


def _dsa_kernel(qi_ref, wit_ref, q_ref, ki_ref, k_ref, vt_ref, o_ref, key_s, bias_s, lg_s,
                *, qb0, tq, kt, sub, topk, seq_bits):
    qb = pl.program_id(1) + qb0
    nkt = ((qb * tq + tq + sub * kt - 1) // (sub * kt)) * sub
    q_pos = qb * tq + lax.broadcasted_iota(I32, (1, tq), 1)
    k_eff = jnp.minimum(topk, q_pos + 1).astype(F32)

    qi = qi_ref[0]
    wit = wit_ref[0]
    qih = [qi[:, h * IDX_DIM:(h + 1) * IDX_DIM] for h in range(IDX_HEADS)]

    def key_pos(t):
        return t * kt + lax.broadcasted_iota(I32, (kt, tq), 0)

    def score_tile(t, _):
        r0 = pl.multiple_of(t * kt, kt)
        ki_t = ki_ref[0, pl.ds(r0, kt), :]
        sc = jnp.zeros((kt, tq), F32)
        for h in range(IDX_HEADS):
            rel = lax.dot_general(ki_t, qih[h], (((1,), (1,)), ((), ())), preferred_element_type=F32)
            sc = sc + jnp.maximum(rel, 0.0) * wit[h:h + 1, :]
        bits = lax.bitcast_convert_type(sc, I32)
        key = jnp.where(bits < 0, bits ^ jnp.int32(0x7FFFFFFF), bits)
        key = jnp.where(key_pos(t) <= q_pos, key, jnp.int32(INT_MIN))
        key_s[pl.ds(r0, kt), :] = key
        return 0

    lax.fori_loop(0, nkt, score_tile, 0)

    def count(pred_fn):
        def body(t, acc):
            r0 = pl.multiple_of(t * kt, kt)
            m = pred_fn(key_s[pl.ds(r0, kt), :], t)
            ones = jnp.where(m, 1.0, 0.0).reshape(kt // PART_ROWS, PART_ROWS, tq)
            return acc + jnp.sum(ones, axis=0)
        acc = lax.fori_loop(0, nkt, body, jnp.zeros((PART_ROWS, tq), F32))
        return jnp.sum(acc, axis=0, keepdims=True)

    def bit_step(i, u):
        bit = jnp.left_shift(jnp.int32(1), 31 - i)
        cand_u = u | bit
        cand_s = cand_u ^ jnp.int32(INT_MIN)
        cnt = count(lambda kk, t: kk >= cand_s)
        return jnp.where(cnt >= k_eff, cand_u, u)

    u_thr = lax.fori_loop(0, 32, bit_step, jnp.zeros((1, tq), I32))
    thr = u_thr ^ jnp.int32(INT_MIN)

    cnt_ge = count(lambda kk, t: kk >= thr)
    cnt_gt = count(lambda kk, t: kk > thr)
    need_eq = k_eff - cnt_gt
    has_tie = jnp.max(cnt_ge - k_eff) > 0.0

    def tie_cut():
        def pos_step(i, c):
            bit = jnp.left_shift(jnp.int32(1), seq_bits - 1 - i)
            cand = c | bit
            cnt = count(lambda kk, t: (kk == thr) & (key_pos(t) < cand))
            return jnp.where(cnt < need_eq, cand, c)
        return lax.fori_loop(0, seq_bits, pos_step, jnp.zeros((1, tq), I32))

    cut = lax.cond(has_tie, tie_cut, lambda: jnp.full((1, tq), 2 ** seq_bits, I32))

    def bias_tile(t, _):
        r0 = pl.multiple_of(t * kt, kt)
        key = key_s[pl.ds(r0, kt), :]
        sel = (key > thr) | ((key == thr) & (key_pos(t) <= cut))
        bias_s[pl.ds(r0, kt), :] = jnp.where(sel, 0.0, NEG_BIG)
        return 0

    lax.fori_loop(0, nkt, bias_tile, 0)

    q = q_ref[0]
    grp = ATTN_HEADS // ATTN_KV_HEADS
    pairs_per_kv = grp // 2
    n_unit = ATTN_KV_HEADS * pairs_per_kv
    wq = 2 * tq
    qpair = [jnp.concatenate([q[:, (2 * u) * HEAD_DIM:(2 * u + 1) * HEAD_DIM],
                              q[:, (2 * u + 1) * HEAD_DIM:(2 * u + 2) * HEAD_DIM]], axis=0)
             for u in range(n_unit)]

    def col_reduce(x, op):
        part = op(x.reshape(kt // PART_ROWS, PART_ROWS, wq), axis=0)
        return op(part, axis=0, keepdims=True)

    def attn_tile(t, carry):
        ms, ls, accs = list(carry[0]), list(carry[1]), list(carry[2])
        for hf in range(sub):
            r0 = pl.multiple_of((t * sub + hf) * kt, kt)
            bias = bias_s[pl.ds(r0, kt), :]
            bias2 = jnp.concatenate([bias, bias], axis=1)
            for u in range(n_unit):
                k_t = k_ref[0, u // pairs_per_kv, pl.ds(r0, kt), :]
                lg_s[hf, u] = lax.dot_general(k_t, qpair[u], (((1,), (1,)), ((), ())),
                                              preferred_element_type=F32) + bias2
        for hf in range(sub):
            for u in range(n_unit):
                lg = lg_s[hf, u]
                m_new = jnp.maximum(ms[u], col_reduce(lg, jnp.max))
                p = jnp.exp2(lg - m_new)
                alpha = jnp.exp2(ms[u] - m_new)
                v_t = vt_ref[0, u // pairs_per_kv, t * sub + hf]
                ls[u] = alpha * ls[u] + col_reduce(p, jnp.sum)
                accs[u] = alpha * accs[u] + jnp.dot(v_t, p.astype(BF16), preferred_element_type=F32)
                ms[u] = m_new
        return tuple(ms), tuple(ls), tuple(accs)

    init = (tuple(jnp.full((1, wq), NEG_BIG, F32) for _ in range(n_unit)),
            tuple(jnp.zeros((1, wq), F32) for _ in range(n_unit)),
            tuple(jnp.zeros((HEAD_DIM, wq), F32) for _ in range(n_unit)))
    _, ls, accs = lax.fori_loop(0, nkt // sub, attn_tile, init)
    for n in range(ATTN_KV_HEADS):
        o_ref[0, 0, n] = jnp.concatenate([accs[n * pairs_per_kv + pg] / ls[n * pairs_per_kv + pg]
                                          for pg in range(pairs_per_kv)], axis=1).astype(BF16)


def _dsa(q, qi, wit, ki, k4, vt, s0, tq, kt):
    bsz, sc, _ = q.shape
    seq = ki.shape[1]
    topk = min(TOPK_MAX, seq // 4)
    nqb = sc // tq
    grp = ATTN_HEADS // ATTN_KV_HEADS
    seq_bits = int(math.log2(seq))
    assert 2 ** seq_bits == seq
    sub = 2 if seq % (2 * kt) == 0 else 1
    n_unit = ATTN_HEADS // 2
    kern = functools.partial(_dsa_kernel, qb0=s0 // tq, tq=tq, kt=kt, sub=sub, topk=topk, seq_bits=seq_bits)
    o_t = pl.pallas_call(
        kern,
        grid=(bsz, nqb),
        in_specs=[pl.BlockSpec((1, tq, IDX_HEADS * IDX_DIM), lambda b, j: (b, j, 0)),
                  pl.BlockSpec((1, IDX_HEADS, tq), lambda b, j: (b, 0, j)),
                  pl.BlockSpec((1, tq, ATTN_HEADS * HEAD_DIM), lambda b, j: (b, j, 0)),
                  pl.BlockSpec((1, seq, IDX_DIM), lambda b, j: (b, 0, 0)),
                  pl.BlockSpec((1, ATTN_KV_HEADS, seq, HEAD_DIM), lambda b, j: (b, 0, 0, 0)),
                  pl.BlockSpec((1, ATTN_KV_HEADS, seq // kt, HEAD_DIM, kt), lambda b, j: (b, 0, 0, 0, 0))],
        out_specs=pl.BlockSpec((1, 1, ATTN_KV_HEADS, HEAD_DIM, grp * tq), lambda b, j: (b, j, 0, 0, 0)),
        out_shape=jax.ShapeDtypeStruct((bsz, nqb, ATTN_KV_HEADS, HEAD_DIM, grp * tq), BF16),
        scratch_shapes=[pltpu.VMEM((seq, tq), I32), pltpu.VMEM((seq, tq), F32),
                        pltpu.VMEM((sub, n_unit, kt, 2 * tq), F32)],
        compiler_params=_cparams(("arbitrary", "arbitrary")),
        name="dsa",
    )(qi, wit, q, ki, k4, vt)
    o = o_t.reshape(bsz, nqb, ATTN_KV_HEADS, HEAD_DIM, grp, tq).transpose(0, 1, 5, 2, 4, 3)
    return o.reshape(bsz, sc, ATTN_HEADS * HEAD_DIM)


def _merge_kernel(x_ref, ys_ref, ya_ref, gs_ref, ga_ref, wsu_ref, wau_ref, wout_ref, g2_ref, wq_ref,
                  h_ref, hn_ref, qp_ref):
    ms = jnp.dot(ys_ref[0], wsu_ref[...], preferred_element_type=F32)
    ma = jnp.dot(ya_ref[0], wau_ref[...], preferred_element_type=F32)
    merged = gs_ref[0].astype(F32) * ms + ga_ref[0].astype(F32) * ma
    h = x_ref[0] + jnp.dot(merged.astype(BF16), wout_ref[...], preferred_element_type=F32)
    h_ref[0] = h
    hb = _rms(h, g2_ref[...]).astype(BF16)
    bits = lax.bitcast_convert_type(hb.astype(F32), I32)
    half = bits.shape[1] // 2
    hn_ref[0] = (bits[:, half:] & jnp.int32(-65536)) | lax.shift_right_logical(bits[:, :half], 16)
    qp_ref[0] = jnp.dot(hb, wq_ref[...], preferred_element_type=F32).astype(BF16)


def _merge(x, ys, ya, gs, ga, wsu, wau, wo, norm2_g, wq, s0, tm):
    bsz, sc, _ = ya.shape
    d = x.shape[2]
    i0 = s0 // tm
    loc = lambda a: pl.BlockSpec((1, tm, a.shape[2]), lambda b, i: (b, i, 0))
    full = lambda a: pl.BlockSpec(a.shape, lambda b, i: (0,) * a.ndim)
    g2 = norm2_g.reshape(1, d)
    nq = wq.shape[1]
    out = lambda n: pl.BlockSpec((1, tm, n), lambda b, i: (b, i, 0))
    return pl.pallas_call(
        _merge_kernel,
        grid=(bsz, sc // tm),
        in_specs=[pl.BlockSpec((1, tm, d), lambda b, i: (b, i + i0, 0)), loc(ys), loc(ya), loc(gs), loc(ga),
                  full(wsu), full(wau), full(wo), full(g2), full(wq)],
        out_specs=[out(d), out(d // 2), out(nq)],
        out_shape=[jax.ShapeDtypeStruct((bsz, sc, d), F32), jax.ShapeDtypeStruct((bsz, sc, d // 2), I32),
                   jax.ShapeDtypeStruct((bsz, sc, nq), BF16)],
        compiler_params=_cparams(("arbitrary", "arbitrary")),
        name="merge",
    )(x, ys, ya, gs, ga, wsu, wau, wo, g2, wq)


def _cand_layout():
    blocks = []
    blocks.append((0, 16, 16))
    for i in range(1, 8):
        blocks.append((i, 8, PEER_TOPK // (i + 1)))
    blocks.append((None, 8, 8))
    return blocks


def _top_rows(s, order, payload, k):
    big = jnp.float32(3e38)
    vals, pays = [], []
    for _ in range(k):
        m = jnp.max(s, axis=0, keepdims=True)
        o = jnp.min(jnp.where(s == m, order, big), axis=0, keepdims=True)
        hit = order == o
        pays.append(jnp.min(jnp.where(hit, payload, big), axis=0, keepdims=True))
        vals.append(m)
        s = jnp.where(hit, -jnp.inf, s)
    return jnp.concatenate(vals, axis=0), jnp.concatenate(pays, axis=0)


def _route_kernel(qp_ref, k1_ref, k2_ref, e_ref, g_ref, *, tt):
    qp = qp_ref[...]
    kd = PEER_KEY_DIM
    rows_k = lax.broadcasted_iota(I32, (PEER_KEYS, tt), 0).astype(F32)
    for h in range(PEER_HEADS):
        q1 = qp[:, (2 * h) * kd:(2 * h + 1) * kd]
        q2 = qp[:, (2 * h + 1) * kd:(2 * h + 2) * kd]
        s1 = lax.dot_general(k1_ref[h], q1, (((1,), (1,)), ((), ())), preferred_element_type=F32)
        s2 = lax.dot_general(k2_ref[h], q2, (((1,), (1,)), ((), ())), preferred_element_type=F32)
        v1, i1 = _top_rows(s1, rows_k, rows_k, PEER_TOPK)
        v2, i2 = _top_rows(s2, rows_k, rows_k, PEER_TOPK)
        cs, ce, co = [], [], []
        for i, rows, valid in _cand_layout():
            r = lax.broadcasted_iota(I32, (rows, tt), 0).astype(F32)
            if i is None:
                val = v1[8:16] + v2[0:1]
                eid = i1[8:16] * PEER_KEYS + i2[0:1]
                flat = (r + 8.0) * PEER_TOPK
            else:
                val = v1[i:i + 1] + v2[0:rows]
                eid = i1[i:i + 1] * PEER_KEYS + i2[0:rows]
                flat = r + float(i * PEER_TOPK)
                if valid < rows:
                    val = jnp.where(r < float(valid), val, -jnp.inf)
            cs.append(val)
            ce.append(eid)
            co.append(flat)
        cand = jnp.concatenate(cs, axis=0)
        top_s, top_e = _top_rows(cand, jnp.concatenate(co, axis=0), jnp.concatenate(ce, axis=0), PEER_TOPK)
        p = jnp.exp(top_s - top_s[0:1])
        gates = p / jnp.sum(p, axis=0, keepdims=True)
        e_ref[h * PEER_TOPK:(h + 1) * PEER_TOPK, :] = top_e.astype(I32)
        g_ref[h * PEER_TOPK:(h + 1) * PEER_TOPK, :] = gates


def _route(qp, k1, k2, tt):
    n, nq = qp.shape
    n_sel = PEER_HEADS * PEER_TOPK
    full = lambda a: pl.BlockSpec(a.shape, lambda i: (0,) * a.ndim)
    return pl.pallas_call(
        functools.partial(_route_kernel, tt=tt),
        grid=(n // tt,),
        in_specs=[pl.BlockSpec((tt, nq), lambda i: (i, 0)), full(k1), full(k2)],
        out_specs=[pl.BlockSpec((n_sel, tt), lambda i: (0, i)), pl.BlockSpec((n_sel, tt), lambda i: (0, i))],
        out_shape=[jax.ShapeDtypeStruct((n_sel, n), I32), jax.ShapeDtypeStruct((n_sel, n), F32)],
        compiler_params=_cparams(("arbitrary",)),
        name="route",
    )(qp, k1, k2)


def _final_kernel(h_ref, p_ref, g_ref, o_ref):
    o_ref[...] = _rms(h_ref[...] + p_ref[...], g_ref[...])


def _final(h, p, g, tm):
    n, d = h.shape
    row = pl.BlockSpec((tm, d), lambda i: (i, 0))
    return pl.pallas_call(
        _final_kernel,
        grid=(n // tm,),
        in_specs=[row, row, pl.BlockSpec((1, d), lambda i: (0, 0))],
        out_specs=row,
        out_shape=jax.ShapeDtypeStruct((n, d), F32),
        compiler_params=_cparams(("arbitrary",)),
        name="final",
    )(h, p, g.reshape(1, d))


SC_CORES_V7X = 2
SC_SUBCORES_V7X = 16
SC_LANES_V7X = 16
PEER_TOK_BATCH = 16
PEER_ROW_CHUNK = 32
PEER_RING = 4


def _pack_bf16_pairs(t):
    half = t.shape[1] // 2
    tb = t.astype(BF16)
    lo = lax.bitcast_convert_type(tb[:, :half], jnp.uint16).astype(jnp.uint32)
    hi = lax.bitcast_convert_type(tb[:, half:], jnp.uint16).astype(jnp.uint32)
    return lax.bitcast_convert_type(lo | (hi << 16), I32)


def _unpack_pair(w):
    lo = lax.bitcast_convert_type(jnp.left_shift(w, 16), F32)
    hi = lax.bitcast_convert_type(w & jnp.int32(-65536), F32)
    return lo, hi


def _peer_sc_body(hn_hbm, e_hbm, g_hbm, u_hbm, v_hbm, out_hbm,
                  idx_v, gate_v, x_v, out_v, rows, p_v, act_v, sem, *, tpw, d, n_sel):
    nl = SC_LANES_V7X
    tb = PEER_TOK_BATCH
    rc = PEER_ROW_CHUNK
    n_chunk = n_sel // rc
    jobs_per_tok = 2 * n_chunk
    half = d // 2
    n_lane_blk = half // nl
    wid = lax.axis_index("s") * SC_CORES_V7X + lax.axis_index("c")
    base = wid * tpw
    lane = lax.iota(I32, nl)
    zero = jnp.zeros((nl,), F32)
    c_gelu = 2.0 * math.sqrt(2.0 / math.pi)

    def gather_copy(tab_hbm, job):
        tok = job // jobs_per_tok
        c = (job % jobs_per_tok) % n_chunk
        b = job % PEER_RING
        return pltpu.make_async_copy(tab_hbm.at[idx_v.at[tok, pl.ds(c * rc, rc)]], rows.at[b], sem.at[b])

    def start(job):
        j = job % jobs_per_tok

        @pl.when(j < n_chunk)
        def _():
            gather_copy(u_hbm, job).start()

        @pl.when(j >= n_chunk)
        def _():
            gather_copy(v_hbm, job).start()

    def compute_u(tok, c, b):
        def rg_body(rg, _):
            r0 = rg * 8

            def jbody(j2, accs):
                off0 = j2 * (2 * nl)
                off1 = off0 + nl
                x0 = plsc.bitcast(x_v[tok, pl.ds(off0, nl)], BF16)
                x1 = plsc.bitcast(x_v[tok, pl.ds(off1, nl)], BF16)
                new = []
                for r in range(8):
                    w0 = plsc.bitcast(rows[b, r0 + r, pl.ds(off0, nl)], BF16)
                    w1 = plsc.bitcast(rows[b, r0 + r, pl.ds(off1, nl)], BF16)
                    lo, hi = _unpack_pair(plsc.bitcast(w0 * x0 + w1 * x1, I32))
                    new.append(accs[r] + (lo + hi))
                return tuple(new)

            accs = lax.fori_loop(0, n_lane_blk // 2, jbody, (zero,) * 8)
            for r in range(8):
                p_v[c * rc + r0 + r, :] = accs[r]
            return 0

        lax.fori_loop(0, rc // 8, rg_body, 0)

    def finish_act(tok):
        def eg_body(eg, _):
            e0 = eg * nl
            ridx = e0 + lane
            s = zero
            for l in range(nl):
                s = s + plsc.load_gather(p_v, [ridx, jnp.full((nl,), l, I32)])
            inner = c_gelu * (s + 0.044715 * (s * s * s))
            gl = s / (1.0 + jnp.exp(-inner))
            a = gl * gate_v[tok, pl.ds(e0, nl)]
            bits = lax.bitcast_convert_type(a, I32)
            rnd = bits + jnp.int32(0x7FFF) + (lax.shift_right_logical(bits, 16) & 1)
            hi16 = rnd & jnp.int32(-65536)
            act_v[pl.ds(e0, nl)] = hi16 | lax.shift_right_logical(hi16, 16)
            return 0

        lax.fori_loop(0, n_sel // nl, eg_body, 0)

        def zbody(j, _):
            out_v[tok, pl.ds(j * nl, nl)] = zero
            return 0

        lax.fori_loop(0, d // nl, zbody, 0, unroll=4)

    def compute_v(tok, c, b):
        def rg_body(rg, _):
            r0 = rg * nl
            splat = [plsc.bitcast(plsc.load_gather(act_v, [jnp.full((nl,), 0, I32) + (c * rc + r0 + r)]), BF16)
                     for r in range(nl)]

            def tree(parts):
                while len(parts) > 1:
                    parts = [parts[i] + parts[i + 1] for i in range(0, len(parts), 2)]
                return parts[0]

            @plsc.parallel_loop(0, n_lane_blk, unroll=2)
            def _(j):
                off = j * nl
                los, his = [], []
                for r in range(0, nl, 2):
                    w0 = plsc.bitcast(rows[b, r0 + r, pl.ds(off, nl)], BF16)
                    w1 = plsc.bitcast(rows[b, r0 + r + 1, pl.ds(off, nl)], BF16)
                    lo, hi = _unpack_pair(plsc.bitcast(w0 * splat[r] + w1 * splat[r + 1], I32))
                    los.append(lo)
                    his.append(hi)
                out_v[tok, pl.ds(off, nl)] = out_v[tok, pl.ds(off, nl)] + tree(los)
                out_v[tok, pl.ds(half + off, nl)] = out_v[tok, pl.ds(half + off, nl)] + tree(his)

            return 0

        lax.fori_loop(0, rc // nl, rg_body, 0)

    def batch_body(bi, _):
        t0 = base + bi * tb
        pltpu.sync_copy(e_hbm.at[pl.ds(t0, tb)], idx_v)
        pltpu.sync_copy(g_hbm.at[pl.ds(t0, tb)], gate_v)
        pltpu.sync_copy(hn_hbm.at[pl.ds(t0, tb)], x_v)
        for pre in range(PEER_RING - 1):
            start(pre)

        def job_body(job, _):
            @pl.when(job + (PEER_RING - 1) < tb * jobs_per_tok)
            def _():
                start(job + (PEER_RING - 1))

            j = job % jobs_per_tok
            gather_copy(u_hbm, job).wait()
            tok = job // jobs_per_tok
            b = job % PEER_RING

            @pl.when(j < n_chunk)
            def _():
                compute_u(tok, j, b)

            @pl.when(j == n_chunk - 1)
            def _():
                finish_act(tok)

            @pl.when(j >= n_chunk)
            def _():
                compute_v(tok, j - n_chunk, b)

            return 0

        lax.fori_loop(0, tb * jobs_per_tok, job_body, 0)
        pltpu.sync_copy(out_v, out_hbm.at[pl.ds(t0, tb)])
        return 0

    lax.fori_loop(0, tpw // tb, batch_body, 0)


def _peer_sc(x_pk, experts, gates, u_tab, v_tab):
    n = x_pk.shape[0]
    d = 2 * x_pk.shape[1]
    n_sel = experts.shape[1]
    nw = SC_CORES_V7X * SC_SUBCORES_V7X
    tpw = n // nw
    mesh = plsc.VectorSubcoreMesh(core_axis_name="c", subcore_axis_name="s",
                                  num_cores=SC_CORES_V7X, num_subcores=SC_SUBCORES_V7X)
    body = functools.partial(_peer_sc_body, tpw=tpw, d=d, n_sel=n_sel)
    call = pl.kernel(
        body,
        out_type=jax.ShapeDtypeStruct((n, d), F32),
        mesh=mesh,
        scratch_types=[pltpu.VMEM((PEER_TOK_BATCH, n_sel), I32),
                       pltpu.VMEM((PEER_TOK_BATCH, n_sel), F32),
                       pltpu.VMEM((PEER_TOK_BATCH, d // 2), I32),
                       pltpu.VMEM((PEER_TOK_BATCH, d), F32),
                       pltpu.VMEM((PEER_RING, PEER_ROW_CHUNK, d // 2), I32),
                       pltpu.VMEM((n_sel, SC_LANES_V7X), F32),
                       pltpu.VMEM((n_sel,), I32),
                       pltpu.SemaphoreType.DMA((PEER_RING,))],
        compiler_params=pltpu.CompilerParams(needs_layout_passes=False),
        name="peer_sc",
    )
    return call(x_pk, experts, gates, u_tab, v_tab)


def kernel(x, norm1_g, w_in, a_re, a_im, log_dt, b_re, b_im, c_re, c_im, d_skip, w_glu, w_ssm_up, w_attn_up,
           w_out, norm2_g, peer_wq, peer_k1, peer_k2, peer_u, peer_v, norm_f_g):
    bsz, seq, d = x.shape
    depth = norm1_g.shape[0]
    tm = min(ROW_TILE, seq)
    n_chunks = TIME_CHUNKS if seq % (TIME_CHUNKS * tm) == 0 else max(1, seq // tm)
    sc = seq // n_chunks
    h = x
    for layer in range(depth):
        last = layer + 1 == depth
        kv_w, main_w = _in_weights(w_in[layer], seq)
        s5p = _s5_params(a_re[layer], a_im[layer], log_dt[layer], b_re[layer], b_im[layer], c_re[layer],
                         c_im[layer], d_skip[layer], w_glu[layer], nb=bsz)
        n_state = s5p[4].shape[1]
        wsu = w_ssm_up[layer].astype(BF16)
        wau = w_attn_up[layer].astype(BF16)
        wo = w_out[layer].astype(BF16)
        wq = peer_wq[layer].astype(BF16)
        k1 = peer_k1[layer].astype(BF16)
        k2 = peer_k2[layer].astype(BF16)
        u_pk = _pack_bf16_pairs(peer_u[layer])
        v_pk = _pack_bf16_pairs(peer_v[layer])
        k4, vt, ki = _kvproj(h, norm1_g[layer], kv_w, tm=tm, kt=DSA_KT)
        st_re = jnp.zeros((bsz, n_state), F32)
        st_im = jnp.zeros((bsz, n_state), F32)
        outs = []
        routed = ki
        peer_outs = []
        for c in range(n_chunks):
            s0 = c * sc
            after = (routed, peer_outs[c - SC_LAG] if c >= SC_LAG else ki)
            u, q, qi, wit, gs, ga = _inproj(h, norm1_g[layer], main_w, after, s0=s0, sc=sc, tm=tm)
            d_ssm = u.shape[-1]
            u_tb = u.transpose(1, 0, 2).reshape(sc * bsz, d_ssm)
            y_tb, st_re, st_im = _s5(u_tb, st_re, st_im, s5p, nb=bsz, tc=64)
            ys = y_tb.reshape(sc, bsz, d_ssm).transpose(1, 0, 2)
            ya = _dsa(q, qi, wit, ki, k4, vt, s0=s0, tq=DSA_TQ, kt=DSA_KT)
            hm, x_pk, qp = _merge(h, ys, ya, gs, ga, wsu, wau, wo, norm2_g[layer], wq, s0=s0, tm=tm)
            nt = bsz * sc
            e_t, g_t = _route(qp.reshape(nt, -1), k1, k2, tt=256)
            routed = e_t
            po = _peer_sc(x_pk.reshape(nt, -1), e_t.T, g_t.T, u_pk, v_pk)
            peer_outs.append(po)
            hm2 = hm.reshape(nt, d)
            o = _final(hm2, po, norm_f_g, tm=tm) if last else hm2 + po
            outs.append(o.reshape(bsz, sc, d))
        h = jnp.concatenate(outs, axis=1)
    return h
```

```python
import functools
import math

import numpy as np
import jax
import jax.numpy as jnp
from jax import lax
from jax.experimental import pallas as pl
from jax.experimental.pallas import tpu as pltpu
from jax.experimental.pallas import tpu_sc as plsc

F32 = jnp.float32
BF16 = jnp.bfloat16
I32 = jnp.int32

SSM_GROUP = 16
SSM_STATE = 64
ATTN_HEADS = 8
ATTN_KV_HEADS = 2
HEAD_DIM = 64
IDX_HEADS = 8
IDX_DIM = 32
TOPK_MAX = 256
ROPE_THETA = 10000.0
NEG_BIG = -1e30
PEER_HEADS = 8
PEER_KEYS = 128
PEER_KEY_DIM = 128
PEER_TOPK = 16
NORM_EPS = 1e-6

TIME_SPLIT_16THS = (1, 2, 2, 2, 2, 2, 2, 1, 1, 1)
SC_LAG = 3
ROW_TILE = 512
DSA_TQ = 128
DSA_KT = 256
INT_MIN = -(2 ** 31)
VMEM_LIMIT = 56 * 1024 * 1024


def _time_chunks(seq):
    unit = seq // 16
    if seq % 16 == 0 and unit % DSA_KT == 0:
        sizes = [f * unit for f in TIME_SPLIT_16THS]
    else:
        step = min(ROW_TILE, seq)
        sizes = [step] * (seq // step)
    assert sum(sizes) == seq
    starts = np.cumsum([0] + sizes[:-1]).tolist()
    return list(zip(starts, sizes))


def _cparams(sem):
    return pltpu.CompilerParams(dimension_semantics=sem, vmem_limit_bytes=VMEM_LIMIT)


def _gelu_tanh(x):
    return 0.5 * x * (1.0 + jnp.tanh(math.sqrt(2.0 / math.pi) * (x + 0.044715 * (x * x * x))))


def _sigmoid(x):
    return 1.0 / (1.0 + jnp.exp(-x))


def _rms(x, g):
    return x * lax.rsqrt(jnp.mean(x * x, axis=-1, keepdims=True) + NORM_EPS) * g


def _rot_cols(w, hd):
    d, n = w.shape
    w3 = w.reshape(d, n // hd, hd)
    half = hd // 2
    return jnp.concatenate([-w3[..., half:], w3[..., :half]], axis=-1).reshape(d, n)


def _rope_full(seq, hd, heads):
    pos = jnp.arange(seq, dtype=F32)
    inv = ROPE_THETA ** (-jnp.arange(0, hd, 2, dtype=F32) / hd)
    ang = pos[:, None] * inv[None, :]
    c = jnp.concatenate([jnp.cos(ang), jnp.cos(ang)], axis=-1)
    s = jnp.concatenate([jnp.sin(ang), jnp.sin(ang)], axis=-1)
    return jnp.tile(c, (1, heads)), jnp.tile(s, (1, heads))


def _in_weights(w_in, seq):
    d = w_in.shape[0]
    d_ssm = d // 2
    d_q = ATTN_HEADS * HEAD_DIM
    d_kv = ATTN_KV_HEADS * HEAD_DIM
    d_qi = IDX_HEADS * IDX_DIM
    splits = (d_ssm, d_q, d_kv, d_kv, d_qi, IDX_DIM, IDX_HEADS, d, d)
    offs = np.cumsum(splits)[:-1].tolist()
    wu, wq, wk, wv, wqi, wki, wwi, wgs, wga = jnp.split(w_in, offs, axis=1)
    pad = jnp.zeros((d, 128 - IDX_DIM), F32)
    cq, sq = _rope_full(seq, HEAD_DIM, ATTN_HEADS)
    ck, sk = _rope_full(seq, HEAD_DIM, ATTN_KV_HEADS)
    cqi, sqi = _rope_full(seq, IDX_DIM, IDX_HEADS)
    cki, ski = _rope_full(seq, IDX_DIM, 1)
    tpad = jnp.zeros((seq, 128 - IDX_DIM), F32)
    kv = dict(
        w=jnp.concatenate([wk, wki, pad], axis=1).astype(BF16),
        wvt=wv.T.astype(BF16),
        wr=jnp.concatenate([_rot_cols(wk, HEAD_DIM), _rot_cols(wki, IDX_DIM), pad], axis=1).astype(BF16),
        cs=jnp.concatenate([ck, cki, tpad], axis=1), sn=jnp.concatenate([sk, ski, tpad], axis=1))
    main = dict(
        w=jnp.concatenate([wu, wq, wqi, wgs, wga], axis=1).astype(BF16),
        wr=jnp.concatenate([_rot_cols(wq, HEAD_DIM), _rot_cols(wqi, IDX_DIM)], axis=1).astype(BF16),
        cs=jnp.concatenate([cq, cqi], axis=1), sn=jnp.concatenate([sq, sqi], axis=1),
        wwit=wwi.T.astype(BF16))
    return kv, main


def _kvproj_kernel(x_ref, g_ref, w_ref, wr_ref, wvt_ref, cs_ref, sn_ref, k_ref, vt_ref, ki_ref, *, d_kv, kt):
    xb = _rms(x_ref[0], g_ref[...]).astype(BF16)

    def mm(ref, lo, n):
        return jnp.dot(xb, ref[:, lo:lo + n], preferred_element_type=F32)

    k = mm(w_ref, 0, d_kv) * cs_ref[:, :d_kv] + mm(wr_ref, 0, d_kv) * sn_ref[:, :d_kv]
    for n in range(ATTN_KV_HEADS):
        k_ref[0, n] = k[:, n * HEAD_DIM:(n + 1) * HEAD_DIM].astype(BF16)
    vt = lax.dot_general(wvt_ref[...], xb, (((1,), (1,)), ((), ())), preferred_element_type=F32)
    for n in range(ATTN_KV_HEADS):
        for j in range(vt.shape[1] // kt):
            vt_ref[0, n, j] = vt[n * HEAD_DIM:(n + 1) * HEAD_DIM, j * kt:(j + 1) * kt].astype(BF16)
    kiw = (mm(w_ref, d_kv, 128) * cs_ref[:, d_kv:d_kv + 128]
           + mm(wr_ref, d_kv, 128) * sn_ref[:, d_kv:d_kv + 128])
    ki_ref[0] = kiw[:, :IDX_DIM].astype(BF16)


def _kvproj(x, norm_g, kv, tm, kt):
    bsz, seq, d = x.shape
    d_kv = ATTN_KV_HEADS * HEAD_DIM
    full = lambda a: pl.BlockSpec(a.shape, lambda s, b: (0,) * a.ndim)
    g = norm_g.reshape(1, d)
    ncs = kv["cs"].shape[1]
    return pl.pallas_call(
        functools.partial(_kvproj_kernel, d_kv=d_kv, kt=kt),
        grid=(seq // tm, bsz),
        in_specs=[pl.BlockSpec((1, tm, d), lambda s, b: (b, s, 0)), full(g), full(kv["w"]), full(kv["wr"]),
                  full(kv["wvt"]),
                  pl.BlockSpec((tm, ncs), lambda s, b: (s, 0)), pl.BlockSpec((tm, ncs), lambda s, b: (s, 0))],
        out_specs=[pl.BlockSpec((1, ATTN_KV_HEADS, tm, HEAD_DIM), lambda s, b: (b, 0, s, 0)),
                   pl.BlockSpec((1, ATTN_KV_HEADS, tm // kt, HEAD_DIM, kt), lambda s, b: (b, 0, s, 0, 0)),
                   pl.BlockSpec((1, tm, IDX_DIM), lambda s, b: (b, s, 0))],
        out_shape=[jax.ShapeDtypeStruct((bsz, ATTN_KV_HEADS, seq, HEAD_DIM), BF16),
                   jax.ShapeDtypeStruct((bsz, ATTN_KV_HEADS, seq // kt, HEAD_DIM, kt), BF16),
                   jax.ShapeDtypeStruct((bsz, seq, IDX_DIM), BF16)],
        compiler_params=_cparams(("arbitrary", "arbitrary")),
        name="kvproj",
    )(x, g, kv["w"], kv["wr"], kv["wvt"], kv["cs"], kv["sn"])


def _inproj_kernel(x_ref, g_ref, w_ref, wr_ref, cs_ref, sn_ref, wwit_ref, after_tc_ref, after_sc_ref,
                   u_ref, q_ref, qi_ref, wit_ref, gs_ref, ga_ref, *, d_ssm, d_q, d_qi, d_model, q_scale, wi_scale):
    del after_tc_ref, after_sc_ref
    xb = _rms(x_ref[0], g_ref[...]).astype(BF16)

    def mm(ref, lo, n):
        return jnp.dot(xb, ref[:, lo:lo + n], preferred_element_type=F32)

    o = 0
    u_ref[0] = mm(w_ref, o, d_ssm).astype(BF16)
    o += d_ssm
    q = mm(w_ref, o, d_q) * cs_ref[:, :d_q] + mm(wr_ref, 0, d_q) * sn_ref[:, :d_q]
    q_ref[0] = (q * q_scale).astype(BF16)
    o += d_q
    qi = mm(w_ref, o, d_qi) * cs_ref[:, d_q:d_q + d_qi] + mm(wr_ref, d_q, d_qi) * sn_ref[:, d_q:d_q + d_qi]
    qi_ref[0] = qi.astype(BF16)
    o += d_qi
    gs_ref[0] = _sigmoid(mm(w_ref, o, d_model)).astype(BF16)
    o += d_model
    ga_ref[0] = _sigmoid(mm(w_ref, o, d_model)).astype(BF16)
    wit_ref[0] = lax.dot_general(wwit_ref[...], xb, (((1,), (1,)), ((), ())),
                                 preferred_element_type=F32) * wi_scale


def _inproj(x, norm_g, main, after, s0, sc, tm):
    bsz, _, d = x.shape
    d_ssm = d // 2
    d_q = ATTN_HEADS * HEAD_DIM
    d_qi = IDX_HEADS * IDX_DIM
    i0 = s0 // tm
    kern = functools.partial(
        _inproj_kernel, d_ssm=d_ssm, d_q=d_q, d_qi=d_qi, d_model=d,
        q_scale=HEAD_DIM ** -0.5 * math.log2(math.e), wi_scale=(IDX_HEADS ** -0.5) * (IDX_DIM ** -0.5))
    tok = lambda n: pl.BlockSpec((1, tm, n), lambda s, b: (b, s, 0))
    full = lambda a: pl.BlockSpec(a.shape, lambda s, b: (0,) * a.ndim)
    g = norm_g.reshape(1, d)
    ncs = main["cs"].shape[1]
    outs = [(d_ssm, BF16), (d_q, BF16), (d_qi, BF16)]
    return pl.pallas_call(
        kern,
        grid=(sc // tm, bsz),
        in_specs=[pl.BlockSpec((1, tm, d), lambda s, b: (b, s + i0, 0)), full(g), full(main["w"]), full(main["wr"]),
                  pl.BlockSpec((tm, ncs), lambda s, b: (s + i0, 0)),
                  pl.BlockSpec((tm, ncs), lambda s, b: (s + i0, 0)), full(main["wwit"]),
                  pl.BlockSpec(memory_space=pl.ANY), pl.BlockSpec(memory_space=pl.ANY)],
        out_specs=[tok(n) for n, _ in outs] + [pl.BlockSpec((1, IDX_HEADS, tm), lambda s, b: (b, 0, s)),
                                                tok(d), tok(d)],
        out_shape=[jax.ShapeDtypeStruct((bsz, sc, n), dt) for n, dt in outs]
        + [jax.ShapeDtypeStruct((bsz, IDX_HEADS, sc), F32),
           jax.ShapeDtypeStruct((bsz, sc, d), BF16), jax.ShapeDtypeStruct((bsz, sc, d), BF16)],
        compiler_params=_cparams(("arbitrary", "arbitrary")),
        name="inproj",
    )(x, g, main["w"], main["wr"], main["cs"], main["sn"], main["wwit"], *after)


def _s5_kernel(u_ref, sre_in, sim_in, bre_ref, bim_ref, cre_ref, cim_ref, are_ref, aim_ref, dsk_ref, wglu_ref,
               y_ref, st_re, st_im, sre, sim, *, tc, nb, lane_chunk):
    @pl.when(pl.program_id(0) == 0)
    def _():
        st_re[...] = sre_in[...]
        st_im[...] = sim_in[...]

    u = u_ref[...]
    n_half = bre_ref.shape[0]
    hin = bre_ref.shape[1]
    hst = bre_ref.shape[2]
    for h in range(n_half):
        uh = u[:, h * hin:(h + 1) * hin]
        sre[:, h * hst:(h + 1) * hst] = jnp.dot(uh, bre_ref[h], preferred_element_type=F32)
        sim[:, h * hst:(h + 1) * hst] = jnp.dot(uh, bim_ref[h], preferred_element_type=F32)

    n_state = sre.shape[1]
    for c in range(n_state // lane_chunk):
        cols = slice(c * lane_chunk, (c + 1) * lane_chunk)
        ar = are_ref[:, cols]
        ai = aim_ref[:, cols]

        def step(t, carry, cols=cols, ar=ar, ai=ai):
            sr, si = carry
            r0 = pl.multiple_of(t * nb, nb)
            nr = ar * sr - ai * si + sre[pl.ds(r0, nb), cols]
            ni = ar * si + ai * sr + sim[pl.ds(r0, nb), cols]
            sre[pl.ds(r0, nb), cols] = nr
            sim[pl.ds(r0, nb), cols] = ni
            return nr, ni

        sr, si = lax.fori_loop(0, tc, step, (st_re[:, cols], st_im[:, cols]), unroll=4)
        st_re[:, cols] = sr
        st_im[:, cols] = si

    ys = []
    for h in range(n_half):
        srh = sre[:, h * hst:(h + 1) * hst].astype(BF16)
        sih = sim[:, h * hst:(h + 1) * hst].astype(BF16)
        ys.append(jnp.dot(srh, cre_ref[h], preferred_element_type=F32)
                  - jnp.dot(sih, cim_ref[h], preferred_element_type=F32))
    y = jnp.concatenate(ys, axis=-1) + dsk_ref[...] * u.astype(F32)
    y = _gelu_tanh(y)
    gate = jnp.dot(y.astype(BF16), wglu_ref[...], preferred_element_type=F32)
    y_ref[...] = (y * _sigmoid(gate)).astype(BF16)


def _s5_params(a_re, a_im, log_dt, b_re, b_im, c_re, c_im, d_skip, w_glu, nb):
    groups = a_re.shape[0]
    d_ssm = groups * SSM_GROUP
    n_state = groups * SSM_STATE
    lam = lax.complex(a_re, a_im)
    dt = jnp.exp(log_dt)[:, None]
    a_bar = jnp.exp(lam * dt)
    b_bar = ((a_bar - 1.0) / lam)[..., None] * lax.complex(b_re, b_im)
    gh = min(groups, 256 // SSM_GROUP)
    n_half = groups // gh
    eye = jnp.eye(gh, dtype=F32)

    def bmat(bb):
        b4 = bb.reshape(n_half, gh, SSM_STATE, SSM_GROUP)
        return jnp.einsum('hgpc,gk->hgckp', b4, eye).reshape(n_half, gh * SSM_GROUP, gh * SSM_STATE)

    def cmat(cc):
        c4 = cc.reshape(n_half, gh, SSM_GROUP, SSM_STATE)
        return jnp.einsum('hgcp,gk->hgpkc', c4, eye).reshape(n_half, gh * SSM_STATE, gh * SSM_GROUP)

    return (bmat(jnp.real(b_bar)).astype(BF16), bmat(jnp.imag(b_bar)).astype(BF16),
            cmat(c_re).astype(BF16), cmat(c_im).astype(BF16),
            jnp.broadcast_to(jnp.real(a_bar).reshape(1, n_state), (nb, n_state)),
            jnp.broadcast_to(jnp.imag(a_bar).reshape(1, n_state), (nb, n_state)),
            d_skip.reshape(1, d_ssm), w_glu.astype(BF16))


def _s5(u_tb, st_re, st_im, params, nb, tc):
    rows, d_ssm = u_tb.shape
    n_state = st_re.shape[1]
    blk = tc * nb
    full = lambda a: pl.BlockSpec(a.shape, lambda i: (0,) * a.ndim)
    st_spec = pl.BlockSpec((nb, n_state), lambda i: (0, 0))
    kern = functools.partial(_s5_kernel, tc=tc, nb=nb, lane_chunk=512)
    return pl.pallas_call(
        kern,
        grid=(rows // blk,),
        in_specs=[pl.BlockSpec((blk, d_ssm), lambda i: (i, 0)), st_spec, st_spec] + [full(p) for p in params],
        out_specs=[pl.BlockSpec((blk, d_ssm), lambda i: (i, 0)), st_spec, st_spec],
        out_shape=[jax.ShapeDtypeStruct((rows, d_ssm), BF16),
                   jax.ShapeDtypeStruct((nb, n_state), F32), jax.ShapeDtypeStruct((nb, n_state), F32)],
        scratch_shapes=[pltpu.VMEM((blk, n_state), F32), pltpu.VMEM((blk, n_state), F32)],
        compiler_params=_cparams(("arbitrary",)),
        name="s5",
    )(u_tb, st_re, st_im, *params)


PART_ROWS = 32


def _dsa_kernel(qi_ref, wit_ref, q_ref, ki_ref, k_ref, vt_ref, o_ref, key_s, bias_s, lg_s,
                *, qb0, tq, kt, sub, topk, seq_bits):
    qb = pl.program_id(1) + qb0
    nkt = ((qb * tq + tq + sub * kt - 1) // (sub * kt)) * sub
    q_pos = qb * tq + lax.broadcasted_iota(I32, (1, tq), 1)
    k_eff = jnp.minimum(topk, q_pos + 1).astype(F32)

    qi = qi_ref[0]
    wit = wit_ref[0]
    qih = [qi[:, h * IDX_DIM:(h + 1) * IDX_DIM] for h in range(IDX_HEADS)]

    def key_pos(t):
        return t * kt + lax.broadcasted_iota(I32, (kt, tq), 0)

    def score_tile(t, _):
        r0 = pl.multiple_of(t * kt, kt)
        ki_t = ki_ref[0, pl.ds(r0, kt), :]
        sc = jnp.zeros((kt, tq), F32)
        for h in range(IDX_HEADS):
            rel = lax.dot_general(ki_t, qih[h], (((1,), (1,)), ((), ())), preferred_element_type=F32)
            sc = sc + jnp.maximum(rel, 0.0) * wit[h:h + 1, :]
        bits = lax.bitcast_convert_type(sc, I32)
        key = jnp.where(bits < 0, bits ^ jnp.int32(0x7FFFFFFF), bits)
        key = jnp.where(key_pos(t) <= q_pos, key, jnp.int32(INT_MIN))
        key_s[pl.ds(r0, kt), :] = key
        return 0

    lax.fori_loop(0, nkt, score_tile, 0)

    def count(pred_fn):
        def body(t, acc):
            r0 = pl.multiple_of(t * kt, kt)
            m = pred_fn(key_s[pl.ds(r0, kt), :], t)
            ones = jnp.where(m, 1.0, 0.0).reshape(kt // PART_ROWS, PART_ROWS, tq)
            return acc + jnp.sum(ones, axis=0)
        acc = lax.fori_loop(0, nkt, body, jnp.zeros((PART_ROWS, tq), F32))
        return jnp.sum(acc, axis=0, keepdims=True)

    def bit_step(i, u):
        bit = jnp.left_shift(jnp.int32(1), 31 - i)
        cand_u = u | bit
        cand_s = cand_u ^ jnp.int32(INT_MIN)
        cnt = count(lambda kk, t: kk >= cand_s)
        return jnp.where(cnt >= k_eff, cand_u, u)

    u_thr = lax.fori_loop(0, 32, bit_step, jnp.zeros((1, tq), I32))
    thr = u_thr ^ jnp.int32(INT_MIN)

    cnt_ge = count(lambda kk, t: kk >= thr)
    cnt_gt = count(lambda kk, t: kk > thr)
    need_eq = k_eff - cnt_gt
    has_tie = jnp.max(cnt_ge - k_eff) > 0.0

    def tie_cut():
        def pos_step(i, c):
            bit = jnp.left_shift(jnp.int32(1), seq_bits - 1 - i)
            cand = c | bit
            cnt = count(lambda kk, t: (kk == thr) & (key_pos(t) < cand))
            return jnp.where(cnt < need_eq, cand, c)
        return lax.fori_loop(0, seq_bits, pos_step, jnp.zeros((1, tq), I32))

    cut = lax.cond(has_tie, tie_cut, lambda: jnp.full((1, tq), 2 ** seq_bits, I32))

    def bias_tile(t, _):
        r0 = pl.multiple_of(t * kt, kt)
        key = key_s[pl.ds(r0, kt), :]
        sel = (key > thr) | ((key == thr) & (key_pos(t) <= cut))
        bias_s[pl.ds(r0, kt), :] = jnp.where(sel, 0.0, NEG_BIG)
        return 0

    lax.fori_loop(0, nkt, bias_tile, 0)

    q = q_ref[0]
    grp = ATTN_HEADS // ATTN_KV_HEADS
    pairs_per_kv = grp // 2
    n_unit = ATTN_KV_HEADS * pairs_per_kv
    wq = 2 * tq
    qpair = [jnp.concatenate([q[:, (2 * u) * HEAD_DIM:(2 * u + 1) * HEAD_DIM],
                              q[:, (2 * u + 1) * HEAD_DIM:(2 * u + 2) * HEAD_DIM]], axis=0)
             for u in range(n_unit)]

    def col_reduce(x, op):
        part = op(x.reshape(kt // PART_ROWS, PART_ROWS, wq), axis=0)
        return op(part, axis=0, keepdims=True)

    def attn_tile(t, carry):
        ms, ls, accs = list(carry[0]), list(carry[1]), list(carry[2])
        for hf in range(sub):
            r0 = pl.multiple_of((t * sub + hf) * kt, kt)
            bias = bias_s[pl.ds(r0, kt), :]
            bias2 = jnp.concatenate([bias, bias], axis=1)
            for u in range(n_unit):
                k_t = k_ref[0, u // pairs_per_kv, pl.ds(r0, kt), :]
                lg_s[hf, u] = lax.dot_general(k_t, qpair[u], (((1,), (1,)), ((), ())),
                                              preferred_element_type=F32) + bias2
        for hf in range(sub):
            for u in range(n_unit):
                lg = lg_s[hf, u]
                m_new = jnp.maximum(ms[u], col_reduce(lg, jnp.max))
                p = jnp.exp2(lg - m_new)
                alpha = jnp.exp2(ms[u] - m_new)
                v_t = vt_ref[0, u // pairs_per_kv, t * sub + hf]
                ls[u] = alpha * ls[u] + col_reduce(p, jnp.sum)
                accs[u] = alpha * accs[u] + jnp.dot(v_t, p.astype(BF16), preferred_element_type=F32)
                ms[u] = m_new
        return tuple(ms), tuple(ls), tuple(accs)

    init = (tuple(jnp.full((1, wq), NEG_BIG, F32) for _ in range(n_unit)),
            tuple(jnp.zeros((1, wq), F32) for _ in range(n_unit)),
            tuple(jnp.zeros((HEAD_DIM, wq), F32) for _ in range(n_unit)))
    _, ls, accs = lax.fori_loop(0, nkt // sub, attn_tile, init)
    for n in range(ATTN_KV_HEADS):
        o_ref[0, 0, n] = jnp.concatenate([accs[n * pairs_per_kv + pg] / ls[n * pairs_per_kv + pg]
                                          for pg in range(pairs_per_kv)], axis=1).astype(BF16)


def _dsa(q, qi, wit, ki, k4, vt, s0, tq, kt):
    bsz, sc, _ = q.shape
    seq = ki.shape[1]
    topk = min(TOPK_MAX, seq // 4)
    nqb = sc // tq
    grp = ATTN_HEADS // ATTN_KV_HEADS
    seq_bits = int(math.log2(seq))
    assert 2 ** seq_bits == seq
    sub = 2 if seq % (2 * kt) == 0 else 1
    n_unit = ATTN_HEADS // 2
    kern = functools.partial(_dsa_kernel, qb0=s0 // tq, tq=tq, kt=kt, sub=sub, topk=topk, seq_bits=seq_bits)
    o_t = pl.pallas_call(
        kern,
        grid=(bsz, nqb),
        in_specs=[pl.BlockSpec((1, tq, IDX_HEADS * IDX_DIM), lambda b, j: (b, j, 0)),
                  pl.BlockSpec((1, IDX_HEADS, tq), lambda b, j: (b, 0, j)),
                  pl.BlockSpec((1, tq, ATTN_HEADS * HEAD_DIM), lambda b, j: (b, j, 0)),
                  pl.BlockSpec((1, seq, IDX_DIM), lambda b, j: (b, 0, 0)),
                  pl.BlockSpec((1, ATTN_KV_HEADS, seq, HEAD_DIM), lambda b, j: (b, 0, 0, 0)),
                  pl.BlockSpec((1, ATTN_KV_HEADS, seq // kt, HEAD_DIM, kt), lambda b, j: (b, 0, 0, 0, 0))],
        out_specs=pl.BlockSpec((1, 1, ATTN_KV_HEADS, HEAD_DIM, grp * tq), lambda b, j: (b, j, 0, 0, 0)),
        out_shape=jax.ShapeDtypeStruct((bsz, nqb, ATTN_KV_HEADS, HEAD_DIM, grp * tq), BF16),
        scratch_shapes=[pltpu.VMEM((seq, tq), I32), pltpu.VMEM((seq, tq), F32),
                        pltpu.VMEM((sub, n_unit, kt, 2 * tq), F32)],
        compiler_params=_cparams(("arbitrary", "arbitrary")),
        name="dsa",
    )(qi, wit, q, ki, k4, vt)
    o = o_t.reshape(bsz, nqb, ATTN_KV_HEADS, HEAD_DIM, grp, tq).transpose(0, 1, 5, 2, 4, 3)
    return o.reshape(bsz, sc, ATTN_HEADS * HEAD_DIM)


def _merge_kernel(x_ref, ys_ref, ya_ref, gs_ref, ga_ref, wsu_ref, wau_ref, wout_ref, g2_ref, wq_ref,
                  h_ref, hn_ref, qp_ref):
    ms = jnp.dot(ys_ref[0], wsu_ref[...], preferred_element_type=F32)
    ma = jnp.dot(ya_ref[0], wau_ref[...], preferred_element_type=F32)
    merged = gs_ref[0].astype(F32) * ms + ga_ref[0].astype(F32) * ma
    h = x_ref[0] + jnp.dot(merged.astype(BF16), wout_ref[...], preferred_element_type=F32)
    h_ref[0] = h
    hb = _rms(h, g2_ref[...]).astype(BF16)
    bits = lax.bitcast_convert_type(hb.astype(F32), I32)
    half = bits.shape[1] // 2
    hn_ref[0] = (bits[:, half:] & jnp.int32(-65536)) | lax.shift_right_logical(bits[:, :half], 16)
    qp_ref[0] = jnp.dot(hb, wq_ref[...], preferred_element_type=F32).astype(BF16)


def _merge(x, ys, ya, gs, ga, wsu, wau, wo, norm2_g, wq, s0, tm):
    bsz, sc, _ = ya.shape
    d = x.shape[2]
    i0 = s0 // tm
    loc = lambda a: pl.BlockSpec((1, tm, a.shape[2]), lambda b, i: (b, i, 0))
    full = lambda a: pl.BlockSpec(a.shape, lambda b, i: (0,) * a.ndim)
    g2 = norm2_g.reshape(1, d)
    nq = wq.shape[1]
    out = lambda n: pl.BlockSpec((1, tm, n), lambda b, i: (b, i, 0))
    return pl.pallas_call(
        _merge_kernel,
        grid=(bsz, sc // tm),
        in_specs=[pl.BlockSpec((1, tm, d), lambda b, i: (b, i + i0, 0)), loc(ys), loc(ya), loc(gs), loc(ga),
                  full(wsu), full(wau), full(wo), full(g2), full(wq)],
        out_specs=[out(d), out(d // 2), out(nq)],
        out_shape=[jax.ShapeDtypeStruct((bsz, sc, d), F32), jax.ShapeDtypeStruct((bsz, sc, d // 2), I32),
                   jax.ShapeDtypeStruct((bsz, sc, nq), BF16)],
        compiler_params=_cparams(("arbitrary", "arbitrary")),
        name="merge",
    )(x, ys, ya, gs, ga, wsu, wau, wo, g2, wq)


def _cand_layout():
    blocks = []
    blocks.append((0, 16, 16))
    for i in range(1, 8):
        blocks.append((i, 8, PEER_TOPK // (i + 1)))
    blocks.append((None, 8, 8))
    return blocks


def _top_rows(s, order, payload, k):
    big = jnp.float32(3e38)
    vals, pays = [], []
    for _ in range(k):
        m = jnp.max(s, axis=0, keepdims=True)
        o = jnp.min(jnp.where(s == m, order, big), axis=0, keepdims=True)
        hit = order == o
        pays.append(jnp.min(jnp.where(hit, payload, big), axis=0, keepdims=True))
        vals.append(m)
        s = jnp.where(hit, -jnp.inf, s)
    return jnp.concatenate(vals, axis=0), jnp.concatenate(pays, axis=0)


def _route_kernel(qp_ref, k1_ref, k2_ref, e_ref, g_ref, *, tt):
    qp = qp_ref[...]
    kd = PEER_KEY_DIM
    rows_k = lax.broadcasted_iota(I32, (PEER_KEYS, tt), 0).astype(F32)
    for h in range(PEER_HEADS):
        q1 = qp[:, (2 * h) * kd:(2 * h + 1) * kd]
        q2 = qp[:, (2 * h + 1) * kd:(2 * h + 2) * kd]
        s1 = lax.dot_general(k1_ref[h], q1, (((1,), (1,)), ((), ())), preferred_element_type=F32)
        s2 = lax.dot_general(k2_ref[h], q2, (((1,), (1,)), ((), ())), preferred_element_type=F32)
        v1, i1 = _top_rows(s1, rows_k, rows_k, PEER_TOPK)
        v2, i2 = _top_rows(s2, rows_k, rows_k, PEER_TOPK)
        cs, ce, co = [], [], []
        for i, rows, valid in _cand_layout():
            r = lax.broadcasted_iota(I32, (rows, tt), 0).astype(F32)
            if i is None:
                val = v1[8:16] + v2[0:1]
                eid = i1[8:16] * PEER_KEYS + i2[0:1]
                flat = (r + 8.0) * PEER_TOPK
            else:
                val = v1[i:i + 1] + v2[0:rows]
                eid = i1[i:i + 1] * PEER_KEYS + i2[0:rows]
                flat = r + float(i * PEER_TOPK)
                if valid < rows:
                    val = jnp.where(r < float(valid), val, -jnp.inf)
            cs.append(val)
            ce.append(eid)
            co.append(flat)
        cand = jnp.concatenate(cs, axis=0)
        top_s, top_e = _top_rows(cand, jnp.concatenate(co, axis=0), jnp.concatenate(ce, axis=0), PEER_TOPK)
        p = jnp.exp(top_s - top_s[0:1])
        gates = p / jnp.sum(p, axis=0, keepdims=True)
        e_ref[h * PEER_TOPK:(h + 1) * PEER_TOPK, :] = top_e.astype(I32)
        g_ref[h * PEER_TOPK:(h + 1) * PEER_TOPK, :] = gates


def _route(qp, k1, k2, tt):
    n, nq = qp.shape
    n_sel = PEER_HEADS * PEER_TOPK
    full = lambda a: pl.BlockSpec(a.shape, lambda i: (0,) * a.ndim)
    return pl.pallas_call(
        functools.partial(_route_kernel, tt=tt),
        grid=(n // tt,),
        in_specs=[pl.BlockSpec((tt, nq), lambda i: (i, 0)), full(k1), full(k2)],
        out_specs=[pl.BlockSpec((n_sel, tt), lambda i: (0, i)), pl.BlockSpec((n_sel, tt), lambda i: (0, i))],
        out_shape=[jax.ShapeDtypeStruct((n_sel, n), I32), jax.ShapeDtypeStruct((n_sel, n), F32)],
        compiler_params=_cparams(("arbitrary",)),
        name="route",
    )(qp, k1, k2)


def _final_kernel(h_ref, p_ref, g_ref, o_ref):
    o_ref[...] = _rms(h_ref[...] + p_ref[...], g_ref[...])


def _final(h, p, g, tm):
    n, d = h.shape
    row = pl.BlockSpec((tm, d), lambda i: (i, 0))
    return pl.pallas_call(
        _final_kernel,
        grid=(n // tm,),
        in_specs=[row, row, pl.BlockSpec((1, d), lambda i: (0, 0))],
        out_specs=row,
        out_shape=jax.ShapeDtypeStruct((n, d), F32),
        compiler_params=_cparams(("arbitrary",)),
        name="final",
    )(h, p, g.reshape(1, d))


SC_CORES_V7X = 2
SC_SUBCORES_V7X = 16
SC_LANES_V7X = 16
PEER_TOK_BATCH = 16
PEER_ROW_CHUNK = 32
PEER_RING = 4


def _pack_bf16_pairs(t):
    half = t.shape[1] // 2
    tb = t.astype(BF16)
    lo = lax.bitcast_convert_type(tb[:, :half], jnp.uint16).astype(jnp.uint32)
    hi = lax.bitcast_convert_type(tb[:, half:], jnp.uint16).astype(jnp.uint32)
    return lax.bitcast_convert_type(lo | (hi << 16), I32)


def _unpack_pair(w):
    lo = lax.bitcast_convert_type(jnp.left_shift(w, 16), F32)
    hi = lax.bitcast_convert_type(w & jnp.int32(-65536), F32)
    return lo, hi


def _peer_sc_body(hn_hbm, e_hbm, g_hbm, u_hbm, v_hbm, out_hbm,
                  idx_v, gate_v, x_v, out_v, rows, p_v, act_v, sem, *, tpw, d, n_sel):
    nl = SC_LANES_V7X
    tb = PEER_TOK_BATCH
    rc = PEER_ROW_CHUNK
    n_chunk = n_sel // rc
    jobs_per_tok = 2 * n_chunk
    half = d // 2
    n_lane_blk = half // nl
    wid = lax.axis_index("s") * SC_CORES_V7X + lax.axis_index("c")
    base = wid * tpw
    lane = lax.iota(I32, nl)
    zero = jnp.zeros((nl,), F32)
    c_gelu = 2.0 * math.sqrt(2.0 / math.pi)

    def gather_copy(tab_hbm, job):
        tok = job // jobs_per_tok
        c = (job % jobs_per_tok) % n_chunk
        b = job % PEER_RING
        return pltpu.make_async_copy(tab_hbm.at[idx_v.at[tok, pl.ds(c * rc, rc)]], rows.at[b], sem.at[b])

    def start(job):
        j = job % jobs_per_tok

        @pl.when(j < n_chunk)
        def _():
            gather_copy(u_hbm, job).start()

        @pl.when(j >= n_chunk)
        def _():
            gather_copy(v_hbm, job).start()

    def compute_u(tok, c, b):
        def rg_body(rg, _):
            r0 = rg * 8

            def jbody(j2, accs):
                off0 = j2 * (2 * nl)
                off1 = off0 + nl
                x0 = plsc.bitcast(x_v[tok, pl.ds(off0, nl)], BF16)
                x1 = plsc.bitcast(x_v[tok, pl.ds(off1, nl)], BF16)
                new = []
                for r in range(8):
                    w0 = plsc.bitcast(rows[b, r0 + r, pl.ds(off0, nl)], BF16)
                    w1 = plsc.bitcast(rows[b, r0 + r, pl.ds(off1, nl)], BF16)
                    lo, hi = _unpack_pair(plsc.bitcast(w0 * x0 + w1 * x1, I32))
                    new.append(accs[r] + (lo + hi))
                return tuple(new)

            accs = lax.fori_loop(0, n_lane_blk // 2, jbody, (zero,) * 8)
            for r in range(8):
                p_v[c * rc + r0 + r, :] = accs[r]
            return 0

        lax.fori_loop(0, rc // 8, rg_body, 0)

    def finish_act(tok):
        def eg_body(eg, _):
            e0 = eg * nl
            ridx = e0 + lane
            s = zero
            for l in range(nl):
                s = s + plsc.load_gather(p_v, [ridx, jnp.full((nl,), l, I32)])
            inner = c_gelu * (s + 0.044715 * (s * s * s))
            gl = s / (1.0 + jnp.exp(-inner))
            a = gl * gate_v[tok, pl.ds(e0, nl)]
            bits = lax.bitcast_convert_type(a, I32)
            rnd = bits + jnp.int32(0x7FFF) + (lax.shift_right_logical(bits, 16) & 1)
            hi16 = rnd & jnp.int32(-65536)
            act_v[pl.ds(e0, nl)] = hi16 | lax.shift_right_logical(hi16, 16)
            return 0

        lax.fori_loop(0, n_sel // nl, eg_body, 0)

        def zbody(j, _):
            out_v[tok, pl.ds(j * nl, nl)] = zero
            return 0

        lax.fori_loop(0, d // nl, zbody, 0, unroll=4)

    def compute_v(tok, c, b):
        def rg_body(rg, _):
            r0 = rg * nl
            splat = [plsc.bitcast(plsc.load_gather(act_v, [jnp.full((nl,), 0, I32) + (c * rc + r0 + r)]), BF16)
                     for r in range(nl)]

            def tree(parts):
                while len(parts) > 1:
                    parts = [parts[i] + parts[i + 1] for i in range(0, len(parts), 2)]
                return parts[0]

            @plsc.parallel_loop(0, n_lane_blk, unroll=2)
            def _(j):
                off = j * nl
                los, his = [], []
                for r in range(0, nl, 2):
                    w0 = plsc.bitcast(rows[b, r0 + r, pl.ds(off, nl)], BF16)
                    w1 = plsc.bitcast(rows[b, r0 + r + 1, pl.ds(off, nl)], BF16)
                    lo, hi = _unpack_pair(plsc.bitcast(w0 * splat[r] + w1 * splat[r + 1], I32))
                    los.append(lo)
                    his.append(hi)
                out_v[tok, pl.ds(off, nl)] = out_v[tok, pl.ds(off, nl)] + tree(los)
                out_v[tok, pl.ds(half + off, nl)] = out_v[tok, pl.ds(half + off, nl)] + tree(his)

            return 0

        lax.fori_loop(0, rc // nl, rg_body, 0)

    def batch_body(bi, _):
        t0 = base + bi * tb
        pltpu.sync_copy(e_hbm.at[pl.ds(t0, tb)], idx_v)
        pltpu.sync_copy(g_hbm.at[pl.ds(t0, tb)], gate_v)
        pltpu.sync_copy(hn_hbm.at[pl.ds(t0, tb)], x_v)
        for pre in range(PEER_RING - 1):
            start(pre)

        def job_body(job, _):
            @pl.when(job + (PEER_RING - 1) < tb * jobs_per_tok)
            def _():
                start(job + (PEER_RING - 1))

            j = job % jobs_per_tok
            gather_copy(u_hbm, job).wait()
            tok = job // jobs_per_tok
            b = job % PEER_RING

            @pl.when(j < n_chunk)
            def _():
                compute_u(tok, j, b)

            @pl.when(j == n_chunk - 1)
            def _():
                finish_act(tok)

            @pl.when(j >= n_chunk)
            def _():
                compute_v(tok, j - n_chunk, b)

            return 0

        lax.fori_loop(0, tb * jobs_per_tok, job_body, 0)
        pltpu.sync_copy(out_v, out_hbm.at[pl.ds(t0, tb)])
        return 0

    lax.fori_loop(0, tpw // tb, batch_body, 0)


def _peer_sc(x_pk, experts, gates, u_tab, v_tab):
    n = x_pk.shape[0]
    d = 2 * x_pk.shape[1]
    n_sel = experts.shape[1]
    nw = SC_CORES_V7X * SC_SUBCORES_V7X
    tpw = n // nw
    mesh = plsc.VectorSubcoreMesh(core_axis_name="c", subcore_axis_name="s",
                                  num_cores=SC_CORES_V7X, num_subcores=SC_SUBCORES_V7X)
    body = functools.partial(_peer_sc_body, tpw=tpw, d=d, n_sel=n_sel)
    call = pl.kernel(
        body,
        out_type=jax.ShapeDtypeStruct((n, d), F32),
        mesh=mesh,
        scratch_types=[pltpu.VMEM((PEER_TOK_BATCH, n_sel), I32),
                       pltpu.VMEM((PEER_TOK_BATCH, n_sel), F32),
                       pltpu.VMEM((PEER_TOK_BATCH, d // 2), I32),
                       pltpu.VMEM((PEER_TOK_BATCH, d), F32),
                       pltpu.VMEM((PEER_RING, PEER_ROW_CHUNK, d // 2), I32),
                       pltpu.VMEM((n_sel, SC_LANES_V7X), F32),
                       pltpu.VMEM((n_sel,), I32),
                       pltpu.SemaphoreType.DMA((PEER_RING,))],
        compiler_params=pltpu.CompilerParams(needs_layout_passes=False),
        name="peer_sc",
    )
    return call(x_pk, experts, gates, u_tab, v_tab)


def kernel(x, norm1_g, w_in, a_re, a_im, log_dt, b_re, b_im, c_re, c_im, d_skip, w_glu, w_ssm_up, w_attn_up,
           w_out, norm2_g, peer_wq, peer_k1, peer_k2, peer_u, peer_v, norm_f_g):
    bsz, seq, d = x.shape
    depth = norm1_g.shape[0]
    chunks = _time_chunks(seq)
    h = x
    for layer in range(depth):
        last = layer + 1 == depth
        kv_w, main_w = _in_weights(w_in[layer], seq)
        s5p = _s5_params(a_re[layer], a_im[layer], log_dt[layer], b_re[layer], b_im[layer], c_re[layer],
                         c_im[layer], d_skip[layer], w_glu[layer], nb=bsz)
        n_state = s5p[4].shape[1]
        wsu = w_ssm_up[layer].astype(BF16)
        wau = w_attn_up[layer].astype(BF16)
        wo = w_out[layer].astype(BF16)
        wq = peer_wq[layer].astype(BF16)
        k1 = peer_k1[layer].astype(BF16)
        k2 = peer_k2[layer].astype(BF16)
        u_pk = _pack_bf16_pairs(peer_u[layer])
        v_pk = _pack_bf16_pairs(peer_v[layer])
        k4, vt, ki = _kvproj(h, norm1_g[layer], kv_w, tm=min(ROW_TILE, seq), kt=DSA_KT)
        st_re = jnp.zeros((bsz, n_state), F32)
        st_im = jnp.zeros((bsz, n_state), F32)
        outs = []
        routed = ki
        peer_outs = []
        for c, (s0, sc) in enumerate(chunks):
            tm = math.gcd(math.gcd(s0, sc), ROW_TILE)
            after =(routed, peer_outs[c - SC_LAG] if c >= SC_LAG else ki)
            u, q, qi, wit, gs, ga = _inproj(h, norm1_g[layer], main_w, after, s0=s0, sc=sc, tm=tm)
            d_ssm = u.shape[-1]
            u_tb = u.transpose(1, 0, 2).reshape(sc * bsz, d_ssm)
            y_tb, st_re, st_im = _s5(u_tb, st_re, st_im, s5p, nb=bsz, tc=64)
            ys = y_tb.reshape(sc, bsz, d_ssm).transpose(1, 0, 2)
            ya = _dsa(q, qi, wit, ki, k4, vt, s0=s0, tq=DSA_TQ, kt=DSA_KT)
            hm, x_pk, qp = _merge(h, ys, ya, gs, ga, wsu, wau, wo, norm2_g[layer], wq, s0=s0, tm=tm)
            nt = bsz * sc
            e_t, g_t = _route(qp.reshape(nt, -1), k1, k2, tt=256)
            routed = e_t
            po = _peer_sc(x_pk.reshape(nt, -1), e_t.T, g_t.T, u_pk, v_pk)
            peer_outs.append(po)
            hm2 = hm.reshape(nt, d)
            o = _final(hm2, po, norm_f_g, tm=tm) if last else hm2 + po
            outs.append(o.reshape(bsz, sc, d))
        h = jnp.concatenate(outs, axis=1)
    return h
```

```python
import functools
import math

import numpy as np
import jax
import jax.numpy as jnp
from jax import lax
from jax.experimental import pallas as pl
from jax.experimental.pallas import tpu as pltpu
from jax.experimental.pallas import tpu_sc as plsc

F32 = jnp.float32
BF16 = jnp.bfloat16
I32 = jnp.int32

SSM_GROUP = 16
SSM_STATE = 64
ATTN_HEADS = 8
ATTN_KV_HEADS = 2
HEAD_DIM = 64
IDX_HEADS = 8
IDX_DIM = 32
TOPK_MAX = 256
ROPE_THETA = 10000.0
NEG_BIG = -1e30
PEER_HEADS = 8
PEER_KEYS = 128
PEER_KEY_DIM = 128
PEER_TOPK = 16
NORM_EPS = 1e-6

TIME_SPLIT_16THS = (1, 2, 2, 2, 2, 2, 2, 1, 1, 1)
SC_LAG = 3
ROW_TILE = 512
DSA_TQ = 128
DSA_KT = 256
INT_MIN = -(2 ** 31)
VMEM_LIMIT = 56 * 1024 * 1024


def _time_chunks(seq):
    unit = seq // 16
    if seq % 16 == 0 and unit % DSA_KT == 0:
        sizes = [f * unit for f in TIME_SPLIT_16THS]
    else:
        step = min(ROW_TILE, seq)
        sizes = [step] * (seq // step)
    assert sum(sizes) == seq
    starts = np.cumsum([0] + sizes[:-1]).tolist()
    return list(zip(starts, sizes))


def _cparams(sem):
    return pltpu.CompilerParams(dimension_semantics=sem, vmem_limit_bytes=VMEM_LIMIT)


def _gelu_tanh(x):
    return 0.5 * x * (1.0 + jnp.tanh(math.sqrt(2.0 / math.pi) * (x + 0.044715 * (x * x * x))))


def _sigmoid(x):
    return 1.0 / (1.0 + jnp.exp(-x))


def _rms(x, g):
    return x * lax.rsqrt(jnp.mean(x * x, axis=-1, keepdims=True) + NORM_EPS) * g


def _rot_cols(w, hd):
    d, n = w.shape
    w3 = w.reshape(d, n // hd, hd)
    half = hd // 2
    return jnp.concatenate([-w3[..., half:], w3[..., :half]], axis=-1).reshape(d, n)


def _rope_full(seq, hd, heads):
    pos = jnp.arange(seq, dtype=F32)
    inv = ROPE_THETA ** (-jnp.arange(0, hd, 2, dtype=F32) / hd)
    ang = pos[:, None] * inv[None, :]
    c = jnp.concatenate([jnp.cos(ang), jnp.cos(ang)], axis=-1)
    s = jnp.concatenate([jnp.sin(ang), jnp.sin(ang)], axis=-1)
    return jnp.tile(c, (1, heads)), jnp.tile(s, (1, heads))


def _in_weights(w_in, seq):
    d = w_in.shape[0]
    d_ssm = d // 2
    d_q = ATTN_HEADS * HEAD_DIM
    d_kv = ATTN_KV_HEADS * HEAD_DIM
    d_qi = IDX_HEADS * IDX_DIM
    splits = (d_ssm, d_q, d_kv, d_kv, d_qi, IDX_DIM, IDX_HEADS, d, d)
    offs = np.cumsum(splits)[:-1].tolist()
    wu, wq, wk, wv, wqi, wki, wwi, wgs, wga = jnp.split(w_in, offs, axis=1)
    pad = jnp.zeros((d, 128 - IDX_DIM), F32)
    cq, sq = _rope_full(seq, HEAD_DIM, ATTN_HEADS)
    ck, sk = _rope_full(seq, HEAD_DIM, ATTN_KV_HEADS)
    cqi, sqi = _rope_full(seq, IDX_DIM, IDX_HEADS)
    cki, ski = _rope_full(seq, IDX_DIM, 1)
    tpad = jnp.zeros((seq, 128 - IDX_DIM), F32)
    kv = dict(
        w=jnp.concatenate([wk, wki, pad], axis=1).astype(BF16),
        wvt=wv.T.astype(BF16),
        wr=jnp.concatenate([_rot_cols(wk, HEAD_DIM), _rot_cols(wki, IDX_DIM), pad], axis=1).astype(BF16),
        cs=jnp.concatenate([ck, cki, tpad], axis=1), sn=jnp.concatenate([sk, ski, tpad], axis=1))
    main = dict(
        w=jnp.concatenate([wu, wq, wqi, wgs, wga], axis=1).astype(BF16),
        wr=jnp.concatenate([_rot_cols(wq, HEAD_DIM), _rot_cols(wqi, IDX_DIM)], axis=1).astype(BF16),
        cs=jnp.concatenate([cq, cqi], axis=1), sn=jnp.concatenate([sq, sqi], axis=1),
        wwit=wwi.T.astype(BF16))
    return kv, main


def _kvproj_kernel(x_ref, g_ref, w_ref, wr_ref, wvt_ref, cs_ref, sn_ref, k_ref, vt_ref, ki_ref, *, d_kv, kt):
    xb = _rms(x_ref[0], g_ref[...]).astype(BF16)

    def mm(ref, lo, n):
        return jnp.dot(xb, ref[:, lo:lo + n], preferred_element_type=F32)

    k = mm(w_ref, 0, d_kv) * cs_ref[:, :d_kv] + mm(wr_ref, 0, d_kv) * sn_ref[:, :d_kv]
    for n in range(ATTN_KV_HEADS):
        k_ref[0, n] = k[:, n * HEAD_DIM:(n + 1) * HEAD_DIM].astype(BF16)
    vt = lax.dot_general(wvt_ref[...], xb, (((1,), (1,)), ((), ())), preferred_element_type=F32)
    for n in range(ATTN_KV_HEADS):
        for j in range(vt.shape[1] // kt):
            vt_ref[0, n, j] = vt[n * HEAD_DIM:(n + 1) * HEAD_DIM, j * kt:(j + 1) * kt].astype(BF16)
    kiw = (mm(w_ref, d_kv, 128) * cs_ref[:, d_kv:d_kv + 128]
           + mm(wr_ref, d_kv, 128) * sn_ref[:, d_kv:d_kv + 128])
    ki_ref[0] = kiw[:, :IDX_DIM].astype(BF16)


def _kvproj(x, norm_g, kv, tm, kt):
    bsz, seq, d = x.shape
    d_kv = ATTN_KV_HEADS * HEAD_DIM
    full = lambda a: pl.BlockSpec(a.shape, lambda s, b: (0,) * a.ndim)
    g = norm_g.reshape(1, d)
    ncs = kv["cs"].shape[1]
    return pl.pallas_call(
        functools.partial(_kvproj_kernel, d_kv=d_kv, kt=kt),
        grid=(seq // tm, bsz),
        in_specs=[pl.BlockSpec((1, tm, d), lambda s, b: (b, s, 0)), full(g), full(kv["w"]), full(kv["wr"]),
                  full(kv["wvt"]),
                  pl.BlockSpec((tm, ncs), lambda s, b: (s, 0)), pl.BlockSpec((tm, ncs), lambda s, b: (s, 0))],
        out_specs=[pl.BlockSpec((1, ATTN_KV_HEADS, tm, HEAD_DIM), lambda s, b: (b, 0, s, 0)),
                   pl.BlockSpec((1, ATTN_KV_HEADS, tm // kt, HEAD_DIM, kt), lambda s, b: (b, 0, s, 0, 0)),
                   pl.BlockSpec((1, tm, IDX_DIM), lambda s, b: (b, s, 0))],
        out_shape=[jax.ShapeDtypeStruct((bsz, ATTN_KV_HEADS, seq, HEAD_DIM), BF16),
                   jax.ShapeDtypeStruct((bsz, ATTN_KV_HEADS, seq // kt, HEAD_DIM, kt), BF16),
                   jax.ShapeDtypeStruct((bsz, seq, IDX_DIM), BF16)],
        compiler_params=_cparams(("arbitrary", "arbitrary")),
        name="kvproj",
    )(x, g, kv["w"], kv["wr"], kv["wvt"], kv["cs"], kv["sn"])


def _inproj_kernel(x_ref, g_ref, w_ref, wr_ref, cs_ref, sn_ref, wwit_ref, after_tc_ref, after_sc_ref,
                   u_ref, q_ref, qi_ref, wit_ref, gs_ref, ga_ref, *, d_ssm, d_q, d_qi, d_model, q_scale, wi_scale):
    del after_tc_ref, after_sc_ref
    xb = _rms(x_ref[0], g_ref[...]).astype(BF16)

    def mm(ref, lo, n):
        return jnp.dot(xb, ref[:, lo:lo + n], preferred_element_type=F32)

    o = 0
    u_ref[0] = mm(w_ref, o, d_ssm).astype(BF16)
    o += d_ssm
    q = mm(w_ref, o, d_q) * cs_ref[:, :d_q] + mm(wr_ref, 0, d_q) * sn_ref[:, :d_q]
    q_ref[0] = (q * q_scale).astype(BF16)
    o += d_q
    qi = mm(w_ref, o, d_qi) * cs_ref[:, d_q:d_q + d_qi] + mm(wr_ref, d_q, d_qi) * sn_ref[:, d_q:d_q + d_qi]
    qi_ref[0] = qi.astype(BF16)
    o += d_qi
    gs_ref[0] = _sigmoid(mm(w_ref, o, d_model)).astype(BF16)
    o += d_model
    ga_ref[0] = _sigmoid(mm(w_ref, o, d_model)).astype(BF16)
    wit_ref[0] = lax.dot_general(wwit_ref[...], xb, (((1,), (1,)), ((), ())),
                                 preferred_element_type=F32) * wi_scale


def _inproj(x, norm_g, main, after, s0, sc, tm):
    bsz, _, d = x.shape
    d_ssm = d // 2
    d_q = ATTN_HEADS * HEAD_DIM
    d_qi = IDX_HEADS * IDX_DIM
    i0 = s0 // tm
    kern = functools.partial(
        _inproj_kernel, d_ssm=d_ssm, d_q=d_q, d_qi=d_qi, d_model=d,
        q_scale=HEAD_DIM ** -0.5 * math.log2(math.e), wi_scale=(IDX_HEADS ** -0.5) * (IDX_DIM ** -0.5))
    tok = lambda n: pl.BlockSpec((1, tm, n), lambda s, b: (b, s, 0))
    full = lambda a: pl.BlockSpec(a.shape, lambda s, b: (0,) * a.ndim)
    g = norm_g.reshape(1, d)
    ncs = main["cs"].shape[1]
    outs = [(d_ssm, BF16), (d_q, BF16), (d_qi, BF16)]
    return pl.pallas_call(
        kern,
        grid=(sc // tm, bsz),
        in_specs=[pl.BlockSpec((1, tm, d), lambda s, b: (b, s + i0, 0)), full(g), full(main["w"]), full(main["wr"]),
                  pl.BlockSpec((tm, ncs), lambda s, b: (s + i0, 0)),
                  pl.BlockSpec((tm, ncs), lambda s, b: (s + i0, 0)), full(main["wwit"]),
                  pl.BlockSpec(memory_space=pl.ANY), pl.BlockSpec(memory_space=pl.ANY)],
        out_specs=[tok(n) for n, _ in outs] + [pl.BlockSpec((1, IDX_HEADS, tm), lambda s, b: (b, 0, s)),
                                                tok(d), tok(d)],
        out_shape=[jax.ShapeDtypeStruct((bsz, sc, n), dt) for n, dt in outs]
        + [jax.ShapeDtypeStruct((bsz, IDX_HEADS, sc), F32),
           jax.ShapeDtypeStruct((bsz, sc, d), BF16), jax.ShapeDtypeStruct((bsz, sc, d), BF16)],
        compiler_params=_cparams(("arbitrary", "arbitrary")),
        name="inproj",
    )(x, g, main["w"], main["wr"], main["cs"], main["sn"], main["wwit"], *after)


def _s5_kernel(u_ref, sre_in, sim_in, bre_ref, bim_ref, cre_ref, cim_ref, are_ref, aim_ref, dsk_ref, wglu_ref,
               y_ref, st_re, st_im, sre, sim, *, tc, nb, lane_chunk):
    @pl.when(pl.program_id(0) == 0)
    def _():
        st_re[...] = sre_in[...]
        st_im[...] = sim_in[...]

    u = u_ref[...]
    n_half = bre_ref.shape[0]
    hin = bre_ref.shape[1]
    hst = bre_ref.shape[2]
    for h in range(n_half):
        uh = u[:, h * hin:(h + 1) * hin]
        sre[:, h * hst:(h + 1) * hst] = jnp.dot(uh, bre_ref[h], preferred_element_type=F32)
        sim[:, h * hst:(h + 1) * hst] = jnp.dot(uh, bim_ref[h], preferred_element_type=F32)

    n_state = sre.shape[1]
    for c in range(n_state // lane_chunk):
        cols = slice(c * lane_chunk, (c + 1) * lane_chunk)
        ar = are_ref[:, cols]
        ai = aim_ref[:, cols]

        def step(t, carry, cols=cols, ar=ar, ai=ai):
            sr, si = carry
            r0 = pl.multiple_of(t * nb, nb)
            nr = ar * sr - ai * si + sre[pl.ds(r0, nb), cols]
            ni = ar * si + ai * sr + sim[pl.ds(r0, nb), cols]
            sre[pl.ds(r0, nb), cols] = nr
            sim[pl.ds(r0, nb), cols] = ni
            return nr, ni

        sr, si = lax.fori_loop(0, tc, step, (st_re[:, cols], st_im[:, cols]), unroll=4)
        st_re[:, cols] = sr
        st_im[:, cols] = si

    ys = []
    for h in range(n_half):
        srh = sre[:, h * hst:(h + 1) * hst].astype(BF16)
        sih = sim[:, h * hst:(h + 1) * hst].astype(BF16)
        ys.append(jnp.dot(srh, cre_ref[h], preferred_element_type=F32)
                  - jnp.dot(sih, cim_ref[h], preferred_element_type=F32))
    y = jnp.concatenate(ys, axis=-1) + dsk_ref[...] * u.astype(F32)
    y = _gelu_tanh(y)
    gate = jnp.dot(y.astype(BF16), wglu_ref[...], preferred_element_type=F32)
    y_ref[...] = (y * _sigmoid(gate)).astype(BF16)


def _s5_params(a_re, a_im, log_dt, b_re, b_im, c_re, c_im, d_skip, w_glu, nb):
    groups = a_re.shape[0]
    d_ssm = groups * SSM_GROUP
    n_state = groups * SSM_STATE
    lam = lax.complex(a_re, a_im)
    dt = jnp.exp(log_dt)[:, None]
    a_bar = jnp.exp(lam * dt)
    b_bar = ((a_bar - 1.0) / lam)[..., None] * lax.complex(b_re, b_im)
    gh = min(groups, 256 // SSM_GROUP)
    n_half = groups // gh
    eye = jnp.eye(gh, dtype=F32)

    def bmat(bb):
        b4 = bb.reshape(n_half, gh, SSM_STATE, SSM_GROUP)
        return jnp.einsum('hgpc,gk->hgckp', b4, eye).reshape(n_half, gh * SSM_GROUP, gh * SSM_STATE)

    def cmat(cc):
        c4 = cc.reshape(n_half, gh, SSM_GROUP, SSM_STATE)
        return jnp.einsum('hgcp,gk->hgpkc', c4, eye).reshape(n_half, gh * SSM_STATE, gh * SSM_GROUP)

    return (bmat(jnp.real(b_bar)).astype(BF16), bmat(jnp.imag(b_bar)).astype(BF16),
            cmat(c_re).astype(BF16), cmat(c_im).astype(BF16),
            jnp.broadcast_to(jnp.real(a_bar).reshape(1, n_state), (nb, n_state)),
            jnp.broadcast_to(jnp.imag(a_bar).reshape(1, n_state), (nb, n_state)),
            d_skip.reshape(1, d_ssm), w_glu.astype(BF16))


def _s5(u_tb, st_re, st_im, params, nb, tc):
    rows, d_ssm = u_tb.shape
    n_state = st_re.shape[1]
    blk = tc * nb
    full = lambda a: pl.BlockSpec(a.shape, lambda i: (0,) * a.ndim)
    st_spec = pl.BlockSpec((nb, n_state), lambda i: (0, 0))
    kern = functools.partial(_s5_kernel, tc=tc, nb=nb, lane_chunk=512)
    return pl.pallas_call(
        kern,
        grid=(rows // blk,),
        in_specs=[pl.BlockSpec((blk, d_ssm), lambda i: (i, 0)), st_spec, st_spec] + [full(p) for p in params],
        out_specs=[pl.BlockSpec((blk, d_ssm), lambda i: (i, 0)), st_spec, st_spec],
        out_shape=[jax.ShapeDtypeStruct((rows, d_ssm), BF16),
                   jax.ShapeDtypeStruct((nb, n_state), F32), jax.ShapeDtypeStruct((nb, n_state), F32)],
        scratch_shapes=[pltpu.VMEM((blk, n_state), F32), pltpu.VMEM((blk, n_state), F32)],
        compiler_params=_cparams(("arbitrary",)),
        name="s5",
    )(u_tb, st_re, st_im, *params)


PART_ROWS = 32


def _dsa_kernel(qi_ref, wit_ref, q_ref, ki_ref, k_ref, vt_ref, o_ref, key_s, bias_s, lg_s,
                *, qb0, tq, kt, sub, topk, seq_bits):
    qb = pl.program_id(1) + qb0
    nkt = ((qb * tq + tq + sub * kt - 1) // (sub * kt)) * sub
    q_pos = qb * tq + lax.broadcasted_iota(I32, (1, tq), 1)
    k_eff = jnp.minimum(topk, q_pos + 1).astype(F32)

    qi = qi_ref[0]
    wit = wit_ref[0]
    qih = [qi[:, h * IDX_DIM:(h + 1) * IDX_DIM] for h in range(IDX_HEADS)]

    def key_pos(t):
        return t * kt + lax.broadcasted_iota(I32, (kt, tq), 0)

    def score_tile(t, _):
        r0 = pl.multiple_of(t * kt, kt)
        ki_t = ki_ref[0, pl.ds(r0, kt), :]
        sc = jnp.zeros((kt, tq), F32)
        for h in range(IDX_HEADS):
            rel = lax.dot_general(ki_t, qih[h], (((1,), (1,)), ((), ())), preferred_element_type=F32)
            sc = sc + jnp.maximum(rel, 0.0) * wit[h:h + 1, :]
        bits = lax.bitcast_convert_type(sc, I32)
        key = jnp.where(bits < 0, bits ^ jnp.int32(0x7FFFFFFF), bits)
        key = jnp.where(key_pos(t) <= q_pos, key, jnp.int32(INT_MIN))
        key_s[pl.ds(r0, kt), :] = key
        return 0

    lax.fori_loop(0, nkt, score_tile, 0)

    def count(pred_fn):
        def body(t, acc):
            r0 = pl.multiple_of(t * kt, kt)
            m = pred_fn(key_s[pl.ds(r0, kt), :], t)
            ones = jnp.where(m, 1.0, 0.0).reshape(kt // PART_ROWS, PART_ROWS, tq)
            return acc + jnp.sum(ones, axis=0)
        acc = lax.fori_loop(0, nkt, body, jnp.zeros((PART_ROWS, tq), F32))
        return jnp.sum(acc, axis=0, keepdims=True)

    def bit_step(i, u):
        bit = jnp.left_shift(jnp.int32(1), 31 - i)
        cand_u = u | bit
        cand_s = cand_u ^ jnp.int32(INT_MIN)
        cnt = count(lambda kk, t: kk >= cand_s)
        return jnp.where(cnt >= k_eff, cand_u, u)

    u_thr = lax.fori_loop(0, 32, bit_step, jnp.zeros((1, tq), I32))
    thr = u_thr ^ jnp.int32(INT_MIN)

    cnt_ge = count(lambda kk, t: kk >= thr)
    cnt_gt = count(lambda kk, t: kk > thr)
    need_eq = k_eff - cnt_gt
    has_tie = jnp.max(cnt_ge - k_eff) > 0.0

    def tie_cut():
        def pos_step(i, c):
            bit = jnp.left_shift(jnp.int32(1), seq_bits - 1 - i)
            cand = c | bit
            cnt = count(lambda kk, t: (kk == thr) & (key_pos(t) < cand))
            return jnp.where(cnt < need_eq, cand, c)
        return lax.fori_loop(0, seq_bits, pos_step, jnp.zeros((1, tq), I32))

    cut = lax.cond(has_tie, tie_cut, lambda: jnp.full((1, tq), 2 ** seq_bits, I32))

    def bias_tile(t, _):
        r0 = pl.multiple_of(t * kt, kt)
        key = key_s[pl.ds(r0, kt), :]
        sel = (key > thr) | ((key == thr) & (key_pos(t) <= cut))
        bias_s[pl.ds(r0, kt), :] = jnp.where(sel, 0.0, NEG_BIG)
        return 0

    lax.fori_loop(0, nkt, bias_tile, 0)

    q = q_ref[0]
    grp = ATTN_HEADS // ATTN_KV_HEADS
    pairs_per_kv = grp // 2
    n_unit = ATTN_KV_HEADS * pairs_per_kv
    wq = 2 * tq
    qpair = [jnp.concatenate([q[:, (2 * u) * HEAD_DIM:(2 * u + 1) * HEAD_DIM],
                              q[:, (2 * u + 1) * HEAD_DIM:(2 * u + 2) * HEAD_DIM]], axis=0)
             for u in range(n_unit)]

    def col_reduce(x, op):
        part = op(x.reshape(kt // PART_ROWS, PART_ROWS, wq), axis=0)
        return op(part, axis=0, keepdims=True)

    def attn_tile(t, carry):
        ms, ls, accs = list(carry[0]), list(carry[1]), list(carry[2])
        for hf in range(sub):
            r0 = pl.multiple_of((t * sub + hf) * kt, kt)
            bias = bias_s[pl.ds(r0, kt), :]
            bias2 = jnp.concatenate([bias, bias], axis=1)
            for u in range(n_unit):
                k_t = k_ref[0, u // pairs_per_kv, pl.ds(r0, kt), :]
                lg_s[hf, u] = lax.dot_general(k_t, qpair[u], (((1,), (1,)), ((), ())),
                                              preferred_element_type=F32) + bias2
        for hf in range(sub):
            for u in range(n_unit):
                lg = lg_s[hf, u]
                m_new = jnp.maximum(ms[u], col_reduce(lg, jnp.max))
                p = jnp.exp2(lg - m_new)
                alpha = jnp.exp2(ms[u] - m_new)
                v_t = vt_ref[0, u // pairs_per_kv, t * sub + hf]
                ls[u] = alpha * ls[u] + col_reduce(p, jnp.sum)
                accs[u] = alpha * accs[u] + jnp.dot(v_t, p.astype(BF16), preferred_element_type=F32)
                ms[u] = m_new
        return tuple(ms), tuple(ls), tuple(accs)

    init = (tuple(jnp.full((1, wq), NEG_BIG, F32) for _ in range(n_unit)),
            tuple(jnp.zeros((1, wq), F32) for _ in range(n_unit)),
            tuple(jnp.zeros((HEAD_DIM, wq), F32) for _ in range(n_unit)))
    _, ls, accs = lax.fori_loop(0, nkt // sub, attn_tile, init)
    for n in range(ATTN_KV_HEADS):
        o_ref[0, 0, n] = jnp.concatenate([accs[n * pairs_per_kv + pg] / ls[n * pairs_per_kv + pg]
                                          for pg in range(pairs_per_kv)], axis=1).astype(BF16)


def _dsa(q, qi, wit, ki, k4, vt, s0, tq, kt):
    bsz, sc, _ = q.shape
    seq = ki.shape[1]
    topk = min(TOPK_MAX, seq // 4)
    nqb = sc // tq
    grp = ATTN_HEADS // ATTN_KV_HEADS
    seq_bits = int(math.log2(seq))
    assert 2 ** seq_bits == seq
    sub = 2 if seq % (2 * kt) == 0 else 1
    n_unit = ATTN_HEADS // 2
    kern = functools.partial(_dsa_kernel, qb0=s0 // tq, tq=tq, kt=kt, sub=sub, topk=topk, seq_bits=seq_bits)
    o_t = pl.pallas_call(
        kern,
        grid=(bsz, nqb),
        in_specs=[pl.BlockSpec((1, tq, IDX_HEADS * IDX_DIM), lambda b, j: (b, j, 0)),
                  pl.BlockSpec((1, IDX_HEADS, tq), lambda b, j: (b, 0, j)),
                  pl.BlockSpec((1, tq, ATTN_HEADS * HEAD_DIM), lambda b, j: (b, j, 0)),
                  pl.BlockSpec((1, seq, IDX_DIM), lambda b, j: (b, 0, 0)),
                  pl.BlockSpec((1, ATTN_KV_HEADS, seq, HEAD_DIM), lambda b, j: (b, 0, 0, 0)),
                  pl.BlockSpec((1, ATTN_KV_HEADS, seq // kt, HEAD_DIM, kt), lambda b, j: (b, 0, 0, 0, 0))],
        out_specs=pl.BlockSpec((1, 1, ATTN_KV_HEADS, HEAD_DIM, grp * tq), lambda b, j: (b, j, 0, 0, 0)),
        out_shape=jax.ShapeDtypeStruct((bsz, nqb, ATTN_KV_HEADS, HEAD_DIM, grp * tq), BF16),
        scratch_shapes=[pltpu.VMEM((seq, tq), I32), pltpu.VMEM((seq, tq), F32),
                        pltpu.VMEM((sub, n_unit, kt, 2 * tq), F32)],
        compiler_params=_cparams(("arbitrary", "arbitrary")),
        name="dsa",
    )(qi, wit, q, ki, k4, vt)
    o = o_t.reshape(bsz, nqb, ATTN_KV_HEADS, HEAD_DIM, grp, tq).transpose(0, 1, 5, 2, 4, 3)
    return o.reshape(bsz, sc, ATTN_HEADS * HEAD_DIM)


def _merge_kernel(x_ref, ys_ref, ya_ref, gs_ref, ga_ref, wsu_ref, wau_ref, wout_ref, g2_ref, wq_ref,
                  h_ref, hn_ref, qp_ref):
    ms = jnp.dot(ys_ref[0], wsu_ref[...], preferred_element_type=F32)
    ma = jnp.dot(ya_ref[0], wau_ref[...], preferred_element_type=F32)
    merged = gs_ref[0].astype(F32) * ms + ga_ref[0].astype(F32) * ma
    h = x_ref[0] + jnp.dot(merged.astype(BF16), wout_ref[...], preferred_element_type=F32)
    h_ref[0] = h
    hb = _rms(h, g2_ref[...]).astype(BF16)
    bits = lax.bitcast_convert_type(hb.astype(F32), I32)
    half = bits.shape[1] // 2
    hn_ref[0] = (bits[:, half:] & jnp.int32(-65536)) | lax.shift_right_logical(bits[:, :half], 16)
    qp_ref[0] = jnp.dot(hb, wq_ref[...], preferred_element_type=F32).astype(BF16)


def _merge(x, ys, ya, gs, ga, wsu, wau, wo, norm2_g, wq, s0, tm):
    bsz, sc, _ = ya.shape
    d = x.shape[2]
    i0 = s0 // tm
    loc = lambda a: pl.BlockSpec((1, tm, a.shape[2]), lambda b, i: (b, i, 0))
    full = lambda a: pl.BlockSpec(a.shape, lambda b, i: (0,) * a.ndim)
    g2 = norm2_g.reshape(1, d)
    nq = wq.shape[1]
    out = lambda n: pl.BlockSpec((1, tm, n), lambda b, i: (b, i, 0))
    return pl.pallas_call(
        _merge_kernel,
        grid=(bsz, sc // tm),
        in_specs=[pl.BlockSpec((1, tm, d), lambda b, i: (b, i + i0, 0)), loc(ys), loc(ya), loc(gs), loc(ga),
                  full(wsu), full(wau), full(wo), full(g2), full(wq)],
        out_specs=[out(d), out(d // 2), out(nq)],
        out_shape=[jax.ShapeDtypeStruct((bsz, sc, d), F32), jax.ShapeDtypeStruct((bsz, sc, d // 2), I32),
                   jax.ShapeDtypeStruct((bsz, sc, nq), BF16)],
        compiler_params=_cparams(("arbitrary", "arbitrary")),
        name="merge",
    )(x, ys, ya, gs, ga, wsu, wau, wo, g2, wq)


def _cand_layout():
    blocks = []
    blocks.append((0, 16, 16))
    for i in range(1, 8):
        blocks.append((i, 8, PEER_TOPK // (i + 1)))
    blocks.append((None, 8, 8))
    return blocks


def _top_rows(s, order, payload, k):
    big = jnp.float32(3e38)
    vals, pays = [], []
    for _ in range(k):
        m = jnp.max(s, axis=0, keepdims=True)
        o = jnp.min(jnp.where(s == m, order, big), axis=0, keepdims=True)
        hit = order == o
        pays.append(jnp.min(jnp.where(hit, payload, big), axis=0, keepdims=True))
        vals.append(m)
        s = jnp.where(hit, -jnp.inf, s)
    return jnp.concatenate(vals, axis=0), jnp.concatenate(pays, axis=0)


def _route_kernel(qp_ref, k1_ref, k2_ref, e_ref, g_ref, *, tt):
    qp = qp_ref[...]
    kd = PEER_KEY_DIM
    rows_k = lax.broadcasted_iota(I32, (PEER_KEYS, tt), 0).astype(F32)
    for h in range(PEER_HEADS):
        q1 = qp[:, (2 * h) * kd:(2 * h + 1) * kd]
        q2 = qp[:, (2 * h + 1) * kd:(2 * h + 2) * kd]
        s1 = lax.dot_general(k1_ref[h], q1, (((1,), (1,)), ((), ())), preferred_element_type=F32)
        s2 = lax.dot_general(k2_ref[h], q2, (((1,), (1,)), ((), ())), preferred_element_type=F32)
        v1, i1 = _top_rows(s1, rows_k, rows_k, PEER_TOPK)
        v2, i2 = _top_rows(s2, rows_k, rows_k, PEER_TOPK)
        cs, ce, co = [], [], []
        for i, rows, valid in _cand_layout():
            r = lax.broadcasted_iota(I32, (rows, tt), 0).astype(F32)
            if i is None:
                val = v1[8:16] + v2[0:1]
                eid = i1[8:16] * PEER_KEYS + i2[0:1]
                flat = (r + 8.0) * PEER_TOPK
            else:
                val = v1[i:i + 1] + v2[0:rows]
                eid = i1[i:i + 1] * PEER_KEYS + i2[0:rows]
                flat = r + float(i * PEER_TOPK)
                if valid < rows:
                    val = jnp.where(r < float(valid), val, -jnp.inf)
            cs.append(val)
            ce.append(eid)
            co.append(flat)
        cand = jnp.concatenate(cs, axis=0)
        top_s, top_e = _top_rows(cand, jnp.concatenate(co, axis=0), jnp.concatenate(ce, axis=0), PEER_TOPK)
        p = jnp.exp(top_s - top_s[0:1])
        gates = p / jnp.sum(p, axis=0, keepdims=True)
        e_ref[h * PEER_TOPK:(h + 1) * PEER_TOPK, :] = top_e.astype(I32)
        g_ref[h * PEER_TOPK:(h + 1) * PEER_TOPK, :] = gates


def _route(qp, k1, k2, tt):
    n, nq = qp.shape
    n_sel = PEER_HEADS * PEER_TOPK
    full = lambda a: pl.BlockSpec(a.shape, lambda i: (0,) * a.ndim)
    return pl.pallas_call(
        functools.partial(_route_kernel, tt=tt),
        grid=(n // tt,),
        in_specs=[pl.BlockSpec((tt, nq), lambda i: (i, 0)), full(k1), full(k2)],
        out_specs=[pl.BlockSpec((n_sel, tt), lambda i: (0, i)), pl.BlockSpec((n_sel, tt), lambda i: (0, i))],
        out_shape=[jax.ShapeDtypeStruct((n_sel, n), I32), jax.ShapeDtypeStruct((n_sel, n), F32)],
        compiler_params=_cparams(("arbitrary",)),
        name="route",
    )(qp, k1, k2)


def _final_kernel(h_ref, p_ref, g_ref, o_ref):
    o_ref[...] = _rms(h_ref[...] + p_ref[...], g_ref[...])


def _final(h, p, g, tm):
    n, d = h.shape
    row = pl.BlockSpec((tm, d), lambda i: (i, 0))
    return pl.pallas_call(
        _final_kernel,
        grid=(n // tm,),
        in_specs=[row, row, pl.BlockSpec((1, d), lambda i: (0, 0))],
        out_specs=row,
        out_shape=jax.ShapeDtypeStruct((n, d), F32),
        compiler_params=_cparams(("arbitrary",)),
        name="final",
    )(h, p, g.reshape(1, d))


SC_CORES_V7X = 2
SC_SUBCORES_V7X = 16
SC_LANES_V7X = 16
PEER_TOK_BATCH = 16
PEER_ROW_CHUNK = 32
PEER_RING = 4
TAB_SEG = 128


def _pack_bf16_pairs(t):
    half = t.shape[1] // 2
    tb = t.astype(BF16)
    lo = lax.bitcast_convert_type(tb[:, :half], jnp.uint16).astype(jnp.uint32)
    hi = lax.bitcast_convert_type(tb[:, half:], jnp.uint16).astype(jnp.uint32)
    return lax.bitcast_convert_type(lo | (hi << 16), I32)


def _unpack_pair(w):
    lo = lax.bitcast_convert_type(jnp.left_shift(w, 16), F32)
    hi = lax.bitcast_convert_type(w & jnp.int32(-65536), F32)
    return lo, hi


def _peer_sc_body(hn_hbm, e_hbm, g_hbm, u_hbm, v_hbm, out_hbm,
                  idx_v, gate_v, x_v, out_v, rows, p_v, act_v, sem, *, tpw, d, n_sel):
    nl = SC_LANES_V7X
    tb = PEER_TOK_BATCH
    rc = PEER_ROW_CHUNK
    n_chunk = n_sel // rc
    jobs_per_tok = 2 * n_chunk
    half = d // 2
    n_lane_blk = half // nl
    wid = lax.axis_index("s") * SC_CORES_V7X + lax.axis_index("c")
    base = wid * tpw
    lane = lax.iota(I32, nl)
    zero = jnp.zeros((nl,), F32)
    c_gelu = 2.0 * math.sqrt(2.0 / math.pi)

    def gather_copy(tab_hbm, job):
        tok = job // jobs_per_tok
        c = (job % jobs_per_tok) % n_chunk
        b = job % PEER_RING
        return pltpu.make_async_copy(tab_hbm.at[idx_v.at[tok, pl.ds(c * rc, rc)]], rows.at[b], sem.at[b])

    def start(job):
        j = job % jobs_per_tok

        @pl.when(j < n_chunk)
        def _():
            gather_copy(u_hbm, job).start()

        @pl.when(j >= n_chunk)
        def _():
            gather_copy(v_hbm, job).start()

    def compute_u(tok, c, b):
        def rg_body(rg, _):
            r0 = rg * 8

            def jbody(j2, accs):
                off0 = j2 * (2 * nl)
                off1 = off0 + nl
                x0 = plsc.bitcast(x_v[tok, pl.ds(off0, nl)], BF16)
                x1 = plsc.bitcast(x_v[tok, pl.ds(off1, nl)], BF16)
                new = []
                seg = off0 // TAB_SEG
                l0 = off0 % TAB_SEG
                for r in range(8):
                    w0 = plsc.bitcast(rows[b, r0 + r, seg, pl.ds(l0, nl)], BF16)
                    w1 = plsc.bitcast(rows[b, r0 + r, seg, pl.ds(l0 + nl, nl)], BF16)
                    lo, hi = _unpack_pair(plsc.bitcast(w0 * x0 + w1 * x1, I32))
                    new.append(accs[r] + (lo + hi))
                return tuple(new)

            accs = lax.fori_loop(0, n_lane_blk // 2, jbody, (zero,) * 8)
            for r in range(8):
                p_v[c * rc + r0 + r, :] = accs[r]
            return 0

        lax.fori_loop(0, rc // 8, rg_body, 0)

    def finish_act(tok):
        def eg_body(eg, _):
            e0 = eg * nl
            ridx = e0 + lane
            s = zero
            for l in range(nl):
                s = s + plsc.load_gather(p_v, [ridx, jnp.full((nl,), l, I32)])
            inner = c_gelu * (s + 0.044715 * (s * s * s))
            gl = s / (1.0 + jnp.exp(-inner))
            a = gl * gate_v[tok, pl.ds(e0, nl)]
            bits = lax.bitcast_convert_type(a, I32)
            rnd = bits + jnp.int32(0x7FFF) + (lax.shift_right_logical(bits, 16) & 1)
            hi16 = rnd & jnp.int32(-65536)
            act_v[pl.ds(e0, nl)] = hi16 | lax.shift_right_logical(hi16, 16)
            return 0

        lax.fori_loop(0, n_sel // nl, eg_body, 0)

        def zbody(j, _):
            out_v[tok, pl.ds(j * nl, nl)] = zero
            return 0

        lax.fori_loop(0, d // nl, zbody, 0, unroll=4)

    def compute_v(tok, c, b):
        def rg_body(rg, _):
            r0 = rg * nl
            splat = [plsc.bitcast(plsc.load_gather(act_v, [jnp.full((nl,), 0, I32) + (c * rc + r0 + r)]), BF16)
                     for r in range(nl)]

            def tree(parts):
                while len(parts) > 1:
                    parts = [parts[i] + parts[i + 1] for i in range(0, len(parts), 2)]
                return parts[0]

            @plsc.parallel_loop(0, n_lane_blk, unroll=2)
            def _(j):
                off = j * nl
                seg = off // TAB_SEG
                l0 = off % TAB_SEG
                los, his = [], []
                for r in range(0, nl, 2):
                    w0 = plsc.bitcast(rows[b, r0 + r, seg, pl.ds(l0, nl)], BF16)
                    w1 = plsc.bitcast(rows[b, r0 + r + 1, seg, pl.ds(l0, nl)], BF16)
                    lo, hi = _unpack_pair(plsc.bitcast(w0 * splat[r] + w1 * splat[r + 1], I32))
                    los.append(lo)
                    his.append(hi)
                out_v[tok, pl.ds(off, nl)] = out_v[tok, pl.ds(off, nl)] + tree(los)
                out_v[tok, pl.ds(half + off, nl)] = out_v[tok, pl.ds(half + off, nl)] + tree(his)

            return 0

        lax.fori_loop(0, rc // nl, rg_body, 0)

    def batch_body(bi, _):
        t0 = base + bi * tb
        pltpu.sync_copy(e_hbm.at[pl.ds(t0, tb)], idx_v)
        pltpu.sync_copy(g_hbm.at[pl.ds(t0, tb)], gate_v)
        pltpu.sync_copy(hn_hbm.at[pl.ds(t0, tb)], x_v)
        for pre in range(PEER_RING - 1):
            start(pre)

        def job_body(job, _):
            @pl.when(job + (PEER_RING - 1) < tb * jobs_per_tok)
            def _():
                start(job + (PEER_RING - 1))

            j = job % jobs_per_tok
            gather_copy(u_hbm, job).wait()
            tok = job // jobs_per_tok
            b = job % PEER_RING

            @pl.when(j < n_chunk)
            def _():
                compute_u(tok, j, b)

            @pl.when(j == n_chunk - 1)
            def _():
                finish_act(tok)

            @pl.when(j >= n_chunk)
            def _():
                compute_v(tok, j - n_chunk, b)

            return 0

        lax.fori_loop(0, tb * jobs_per_tok, job_body, 0)
        pltpu.sync_copy(out_v, out_hbm.at[pl.ds(t0, tb)])
        return 0

    lax.fori_loop(0, tpw // tb, batch_body, 0)


def _peer_sc(x_pk, experts, gates, u_tab, v_tab):
    n = x_pk.shape[0]
    d = 2 * x_pk.shape[1]
    n_sel = experts.shape[1]
    nw = SC_CORES_V7X * SC_SUBCORES_V7X
    tpw = n // nw
    mesh = plsc.VectorSubcoreMesh(core_axis_name="c", subcore_axis_name="s",
                                  num_cores=SC_CORES_V7X, num_subcores=SC_SUBCORES_V7X)
    body = functools.partial(_peer_sc_body, tpw=tpw, d=d, n_sel=n_sel)
    call = pl.kernel(
        body,
        out_type=jax.ShapeDtypeStruct((n, d), F32),
        mesh=mesh,
        scratch_types=[pltpu.VMEM((PEER_TOK_BATCH, n_sel), I32),
                       pltpu.VMEM((PEER_TOK_BATCH, n_sel), F32),
                       pltpu.VMEM((PEER_TOK_BATCH, d // 2), I32),
                       pltpu.VMEM((PEER_TOK_BATCH, d), F32),
                       pltpu.VMEM((PEER_RING, PEER_ROW_CHUNK, d // 2 // TAB_SEG, TAB_SEG), I32),
                       pltpu.VMEM((n_sel, SC_LANES_V7X), F32),
                       pltpu.VMEM((n_sel,), I32),
                       pltpu.SemaphoreType.DMA((PEER_RING,))],
        compiler_params=pltpu.CompilerParams(needs_layout_passes=False),
        name="peer_sc",
    )
    return call(x_pk, experts, gates, u_tab, v_tab)


def kernel(x, norm1_g, w_in, a_re, a_im, log_dt, b_re, b_im, c_re, c_im, d_skip, w_glu, w_ssm_up, w_attn_up,
           w_out, norm2_g, peer_wq, peer_k1, peer_k2, peer_u, peer_v, norm_f_g):
    bsz, seq, d = x.shape
    depth = norm1_g.shape[0]
    chunks = _time_chunks(seq)
    h = x
    for layer in range(depth):
        last = layer + 1 == depth
        kv_w, main_w = _in_weights(w_in[layer], seq)
        s5p = _s5_params(a_re[layer], a_im[layer], log_dt[layer], b_re[layer], b_im[layer], c_re[layer],
                         c_im[layer], d_skip[layer], w_glu[layer], nb=bsz)
        n_state = s5p[4].shape[1]
        wsu = w_ssm_up[layer].astype(BF16)
        wau = w_attn_up[layer].astype(BF16)
        wo = w_out[layer].astype(BF16)
        wq = peer_wq[layer].astype(BF16)
        k1 = peer_k1[layer].astype(BF16)
        k2 = peer_k2[layer].astype(BF16)
        u_pk = _pack_bf16_pairs(peer_u[layer]).reshape(-1, d // 2 // TAB_SEG, TAB_SEG)
        v_pk = _pack_bf16_pairs(peer_v[layer]).reshape(-1, d // 2 // TAB_SEG, TAB_SEG)
        k4, vt, ki = _kvproj(h, norm1_g[layer], kv_w, tm=min(ROW_TILE, seq), kt=DSA_KT)
        st_re = jnp.zeros((bsz, n_state), F32)
        st_im = jnp.zeros((bsz, n_state), F32)
        outs = []
        routed = ki
        peer_outs = []
        for c, (s0, sc) in enumerate(chunks):
            tm = math.gcd(math.gcd(s0, sc), ROW_TILE)
            after =(routed, peer_outs[c - SC_LAG] if c >= SC_LAG else ki)
            u, q, qi, wit, gs, ga = _inproj(h, norm1_g[layer], main_w, after, s0=s0, sc=sc, tm=tm)
            d_ssm = u.shape[-1]
            u_tb = u.transpose(1, 0, 2).reshape(sc * bsz, d_ssm)
            y_tb, st_re, st_im = _s5(u_tb, st_re, st_im, s5p, nb=bsz, tc=64)
            ys = y_tb.reshape(sc, bsz, d_ssm).transpose(1, 0, 2)
            ya = _dsa(q, qi, wit, ki, k4, vt, s0=s0, tq=DSA_TQ, kt=DSA_KT)
            hm, x_pk, qp = _merge(h, ys, ya, gs, ga, wsu, wau, wo, norm2_g[layer], wq, s0=s0, tm=tm)
            nt = bsz * sc
            e_t, g_t = _route(qp.reshape(nt, -1), k1, k2, tt=256)
            routed = e_t
            po = _peer_sc(x_pk.reshape(nt, -1), e_t.T, g_t.T, u_pk, v_pk)
            peer_outs.append(po)
            hm2 = hm.reshape(nt, d)
            o = _final(hm2, po, norm_f_g, tm=tm) if last else hm2 + po
            outs.append(o.reshape(bsz, sc, d))
        h = jnp.concatenate(outs, axis=1)
    return h
```

```python
import functools
import math

import numpy as np
import jax
import jax.numpy as jnp
from jax import lax
from jax.experimental import pallas as pl
from jax.experimental.pallas import tpu as pltpu
from jax.experimental.pallas import tpu_sc as plsc

F32 = jnp.float32
BF16 = jnp.bfloat16
I32 = jnp.int32

SSM_GROUP = 16
SSM_STATE = 64
ATTN_HEADS = 8
ATTN_KV_HEADS = 2
HEAD_DIM = 64
IDX_HEADS = 8
IDX_DIM = 32
TOPK_MAX = 256
ROPE_THETA = 10000.0
NEG_BIG = -1e30
PEER_HEADS = 8
PEER_KEYS = 128
PEER_KEY_DIM = 128
PEER_TOPK = 16
NORM_EPS = 1e-6

TIME_SPLIT_16THS = (1, 2, 2, 2, 2, 2, 2, 1, 1, 1)
SC_LAG = 3
ROW_TILE = 512
DSA_TQ = 128
DSA_KT = 256
LANES = 128
INT_MIN = -(2 ** 31)
VMEM_LIMIT = 56 * 1024 * 1024


def _time_chunks(seq):
    unit = seq // 16
    if seq % 16 == 0 and unit % DSA_KT == 0:
        sizes = [f * unit for f in TIME_SPLIT_16THS]
    else:
        step = min(ROW_TILE, seq)
        sizes = [step] * (seq // step)
    assert sum(sizes) == seq
    starts = np.cumsum([0] + sizes[:-1]).tolist()
    return list(zip(starts, sizes))


def _cparams(sem):
    return pltpu.CompilerParams(dimension_semantics=sem, vmem_limit_bytes=VMEM_LIMIT)


def _gelu_tanh(x):
    return 0.5 * x * (1.0 + jnp.tanh(math.sqrt(2.0 / math.pi) * (x + 0.044715 * (x * x * x))))


def _sigmoid(x):
    return 1.0 / (1.0 + jnp.exp(-x))


def _rms(x, g):
    return x * lax.rsqrt(jnp.mean(x * x, axis=-1, keepdims=True) + NORM_EPS) * g


def _rot_cols(w, hd):
    d, n = w.shape
    w3 = w.reshape(d, n // hd, hd)
    half = hd // 2
    return jnp.concatenate([-w3[..., half:], w3[..., :half]], axis=-1).reshape(d, n)


def _rope_full(seq, hd, heads):
    pos = jnp.arange(seq, dtype=F32)
    inv = ROPE_THETA ** (-jnp.arange(0, hd, 2, dtype=F32) / hd)
    ang = pos[:, None] * inv[None, :]
    c = jnp.concatenate([jnp.cos(ang), jnp.cos(ang)], axis=-1)
    s = jnp.concatenate([jnp.sin(ang), jnp.sin(ang)], axis=-1)
    return jnp.tile(c, (1, heads)), jnp.tile(s, (1, heads))


def _in_weights(w_in, seq):
    d = w_in.shape[0]
    d_ssm = d // 2
    d_q = ATTN_HEADS * HEAD_DIM
    d_kv = ATTN_KV_HEADS * HEAD_DIM
    d_qi = IDX_HEADS * IDX_DIM
    splits = (d_ssm, d_q, d_kv, d_kv, d_qi, IDX_DIM, IDX_HEADS, d, d)
    offs = np.cumsum(splits)[:-1].tolist()
    wu, wq, wk, wv, wqi, wki, wwi, wgs, wga = jnp.split(w_in, offs, axis=1)
    pad = jnp.zeros((d, 128 - IDX_DIM), F32)
    cq, sq = _rope_full(seq, HEAD_DIM, ATTN_HEADS)
    ck, sk = _rope_full(seq, HEAD_DIM, ATTN_KV_HEADS)
    cqi, sqi = _rope_full(seq, IDX_DIM, IDX_HEADS)
    cki, ski = _rope_full(seq, IDX_DIM, 1)
    tpad = jnp.zeros((seq, 128 - IDX_DIM), F32)
    kv = dict(
        w=jnp.concatenate([wk, wki, pad], axis=1).astype(BF16),
        wvt=wv.T.astype(BF16),
        wr=jnp.concatenate([_rot_cols(wk, HEAD_DIM), _rot_cols(wki, IDX_DIM), pad], axis=1).astype(BF16),
        cs=jnp.concatenate([ck, cki, tpad], axis=1), sn=jnp.concatenate([sk, ski, tpad], axis=1))
    main = dict(
        w=jnp.concatenate([wu, wq, wqi, wgs, wga], axis=1).astype(BF16),
        wr=jnp.concatenate([_rot_cols(wq, HEAD_DIM), _rot_cols(wqi, IDX_DIM)], axis=1).astype(BF16),
        cs=jnp.concatenate([cq, cqi], axis=1), sn=jnp.concatenate([sq, sqi], axis=1),
        wwit=wwi.T.astype(BF16))
    return kv, main


def _kvproj_kernel(x_ref, g_ref, w_ref, wr_ref, wvt_ref, cs_ref, sn_ref, k_ref, vt_ref, ki_ref, *, d_kv, kt):
    xb = _rms(x_ref[0], g_ref[...]).astype(BF16)

    def mm(ref, lo, n):
        return jnp.dot(xb, ref[:, lo:lo + n], preferred_element_type=F32)

    k = mm(w_ref, 0, d_kv) * cs_ref[:, :d_kv] + mm(wr_ref, 0, d_kv) * sn_ref[:, :d_kv]
    for n in range(ATTN_KV_HEADS):
        k_ref[0, n] = k[:, n * HEAD_DIM:(n + 1) * HEAD_DIM].astype(BF16)
    vt = lax.dot_general(wvt_ref[...], xb, (((1,), (1,)), ((), ())), preferred_element_type=F32)
    for n in range(ATTN_KV_HEADS):
        for j in range(vt.shape[1] // kt):
            vt_ref[0, n, j] = vt[n * HEAD_DIM:(n + 1) * HEAD_DIM, j * kt:(j + 1) * kt].astype(BF16)
    kiw = (mm(w_ref, d_kv, 128) * cs_ref[:, d_kv:d_kv + 128]
           + mm(wr_ref, d_kv, 128) * sn_ref[:, d_kv:d_kv + 128])
    ki_ref[0] = kiw[:, :IDX_DIM].astype(BF16)


def _kvproj(x, norm_g, kv, tm, kt):
    bsz, seq, d = x.shape
    d_kv = ATTN_KV_HEADS * HEAD_DIM
    full = lambda a: pl.BlockSpec(a.shape, lambda s, b: (0,) * a.ndim)
    g = norm_g.reshape(1, d)
    ncs = kv["cs"].shape[1]
    return pl.pallas_call(
        functools.partial(_kvproj_kernel, d_kv=d_kv, kt=kt),
        grid=(seq // tm, bsz),
        in_specs=[pl.BlockSpec((1, tm, d), lambda s, b: (b, s, 0)), full(g), full(kv["w"]), full(kv["wr"]),
                  full(kv["wvt"]),
                  pl.BlockSpec((tm, ncs), lambda s, b: (s, 0)), pl.BlockSpec((tm, ncs), lambda s, b: (s, 0))],
        out_specs=[pl.BlockSpec((1, ATTN_KV_HEADS, tm, HEAD_DIM), lambda s, b: (b, 0, s, 0)),
                   pl.BlockSpec((1, ATTN_KV_HEADS, tm // kt, HEAD_DIM, kt), lambda s, b: (b, 0, s, 0, 0)),
                   pl.BlockSpec((1, tm, IDX_DIM), lambda s, b: (b, s, 0))],
        out_shape=[jax.ShapeDtypeStruct((bsz, ATTN_KV_HEADS, seq, HEAD_DIM), BF16),
                   jax.ShapeDtypeStruct((bsz, ATTN_KV_HEADS, seq // kt, HEAD_DIM, kt), BF16),
                   jax.ShapeDtypeStruct((bsz, seq, IDX_DIM), BF16)],
        compiler_params=_cparams(("arbitrary", "arbitrary")),
        name="kvproj",
    )(x, g, kv["w"], kv["wr"], kv["wvt"], kv["cs"], kv["sn"])


def _inproj_kernel(x_ref, g_ref, w_ref, wr_ref, cs_ref, sn_ref, wwit_ref, after_tc_ref, after_sc_ref,
                   u_ref, q_ref, qi_ref, wit_ref, gs_ref, ga_ref, *, d_ssm, d_q, d_qi, d_model, q_scale, wi_scale):
    del after_tc_ref, after_sc_ref
    xb = _rms(x_ref[0], g_ref[...]).astype(BF16)

    def mm(ref, lo, n):
        return jnp.dot(xb, ref[:, lo:lo + n], preferred_element_type=F32)

    o = 0
    u_ref[0] = mm(w_ref, o, d_ssm).astype(BF16)
    o += d_ssm
    q = mm(w_ref, o, d_q) * cs_ref[:, :d_q] + mm(wr_ref, 0, d_q) * sn_ref[:, :d_q]
    q_ref[0] = (q * q_scale).astype(BF16)
    o += d_q
    qi = mm(w_ref, o, d_qi) * cs_ref[:, d_q:d_q + d_qi] + mm(wr_ref, d_q, d_qi) * sn_ref[:, d_q:d_q + d_qi]
    qi_ref[0] = qi.astype(BF16)
    o += d_qi
    gs_ref[0] = _sigmoid(mm(w_ref, o, d_model)).astype(BF16)
    o += d_model
    ga_ref[0] = _sigmoid(mm(w_ref, o, d_model)).astype(BF16)
    wit_ref[0] = lax.dot_general(wwit_ref[...], xb, (((1,), (1,)), ((), ())),
                                 preferred_element_type=F32) * wi_scale


def _inproj(x, norm_g, main, after, s0, sc, tm):
    bsz, _, d = x.shape
    d_ssm = d // 2
    d_q = ATTN_HEADS * HEAD_DIM
    d_qi = IDX_HEADS * IDX_DIM
    i0 = s0 // tm
    kern = functools.partial(
        _inproj_kernel, d_ssm=d_ssm, d_q=d_q, d_qi=d_qi, d_model=d,
        q_scale=HEAD_DIM ** -0.5 * math.log2(math.e), wi_scale=(IDX_HEADS ** -0.5) * (IDX_DIM ** -0.5))
    tok = lambda n: pl.BlockSpec((1, tm, n), lambda s, b: (b, s, 0))
    full = lambda a: pl.BlockSpec(a.shape, lambda s, b: (0,) * a.ndim)
    g = norm_g.reshape(1, d)
    ncs = main["cs"].shape[1]
    outs = [(d_ssm, BF16), (d_q, BF16), (d_qi, BF16)]
    return pl.pallas_call(
        kern,
        grid=(sc // tm, bsz),
        in_specs=[pl.BlockSpec((1, tm, d), lambda s, b: (b, s + i0, 0)), full(g), full(main["w"]), full(main["wr"]),
                  pl.BlockSpec((tm, ncs), lambda s, b: (s + i0, 0)),
                  pl.BlockSpec((tm, ncs), lambda s, b: (s + i0, 0)), full(main["wwit"]),
                  pl.BlockSpec(memory_space=pl.ANY), pl.BlockSpec(memory_space=pl.ANY)],
        out_specs=[tok(n) for n, _ in outs] + [pl.BlockSpec((1, IDX_HEADS, tm), lambda s, b: (b, 0, s)),
                                                tok(d), tok(d)],
        out_shape=[jax.ShapeDtypeStruct((bsz, sc, n), dt) for n, dt in outs]
        + [jax.ShapeDtypeStruct((bsz, IDX_HEADS, sc), F32),
           jax.ShapeDtypeStruct((bsz, sc, d), BF16), jax.ShapeDtypeStruct((bsz, sc, d), BF16)],
        compiler_params=_cparams(("arbitrary", "arbitrary")),
        name="inproj",
    )(x, g, main["w"], main["wr"], main["cs"], main["sn"], main["wwit"], *after)


def _s5_kernel(u_ref, sre_in, sim_in, bre_ref, bim_ref, cre_ref, cim_ref, are_ref, aim_ref, dsk_ref, wglu_ref,
               y_ref, st_re, st_im, sre, sim, *, tc, nb, lane_chunk):
    @pl.when(pl.program_id(0) == 0)
    def _():
        st_re[...] = sre_in[...]
        st_im[...] = sim_in[...]

    u = u_ref[...]
    n_half = bre_ref.shape[0]
    hin = bre_ref.shape[1]
    hst = bre_ref.shape[2]
    for h in range(n_half):
        uh = u[:, h * hin:(h + 1) * hin]
        sre[:, h * hst:(h + 1) * hst] = jnp.dot(uh, bre_ref[h], preferred_element_type=F32)
        sim[:, h * hst:(h + 1) * hst] = jnp.dot(uh, bim_ref[h], preferred_element_type=F32)

    n_state = sre.shape[1]
    for c in range(n_state // lane_chunk):
        cols = slice(c * lane_chunk, (c + 1) * lane_chunk)
        ar = are_ref[:, cols]
        ai = aim_ref[:, cols]

        def step(t, carry, cols=cols, ar=ar, ai=ai):
            sr, si = carry
            r0 = pl.multiple_of(t * nb, nb)
            nr = ar * sr - ai * si + sre[pl.ds(r0, nb), cols]
            ni = ar * si + ai * sr + sim[pl.ds(r0, nb), cols]
            sre[pl.ds(r0, nb), cols] = nr
            sim[pl.ds(r0, nb), cols] = ni
            return nr, ni

        sr, si = lax.fori_loop(0, tc, step, (st_re[:, cols], st_im[:, cols]), unroll=4)
        st_re[:, cols] = sr
        st_im[:, cols] = si

    ys = []
    for h in range(n_half):
        srh = sre[:, h * hst:(h + 1) * hst].astype(BF16)
        sih = sim[:, h * hst:(h + 1) * hst].astype(BF16)
        ys.append(jnp.dot(srh, cre_ref[h], preferred_element_type=F32)
                  - jnp.dot(sih, cim_ref[h], preferred_element_type=F32))
    y = jnp.concatenate(ys, axis=-1) + dsk_ref[...] * u.astype(F32)
    y = _gelu_tanh(y)
    gate = jnp.dot(y.astype(BF16), wglu_ref[...], preferred_element_type=F32)
    y_ref[...] = (y * _sigmoid(gate)).astype(BF16)


def _s5_params(a_re, a_im, log_dt, b_re, b_im, c_re, c_im, d_skip, w_glu, nb):
    groups = a_re.shape[0]
    d_ssm = groups * SSM_GROUP
    n_state = groups * SSM_STATE
    lam = lax.complex(a_re, a_im)
    dt = jnp.exp(log_dt)[:, None]
    a_bar = jnp.exp(lam * dt)
    b_bar = ((a_bar - 1.0) / lam)[..., None] * lax.complex(b_re, b_im)
    gh = min(groups, 256 // SSM_GROUP)
    n_half = groups // gh
    eye = jnp.eye(gh, dtype=F32)

    def bmat(bb):
        b4 = bb.reshape(n_half, gh, SSM_STATE, SSM_GROUP)
        return jnp.einsum('hgpc,gk->hgckp', b4, eye).reshape(n_half, gh * SSM_GROUP, gh * SSM_STATE)

    def cmat(cc):
        c4 = cc.reshape(n_half, gh, SSM_GROUP, SSM_STATE)
        return jnp.einsum('hgcp,gk->hgpkc', c4, eye).reshape(n_half, gh * SSM_STATE, gh * SSM_GROUP)

    return (bmat(jnp.real(b_bar)).astype(BF16), bmat(jnp.imag(b_bar)).astype(BF16),
            cmat(c_re).astype(BF16), cmat(c_im).astype(BF16),
            jnp.broadcast_to(jnp.real(a_bar).reshape(1, n_state), (nb, n_state)),
            jnp.broadcast_to(jnp.imag(a_bar).reshape(1, n_state), (nb, n_state)),
            d_skip.reshape(1, d_ssm), w_glu.astype(BF16))


def _s5(u_tb, st_re, st_im, params, nb, tc):
    rows, d_ssm = u_tb.shape
    n_state = st_re.shape[1]
    blk = tc * nb
    full = lambda a: pl.BlockSpec(a.shape, lambda i: (0,) * a.ndim)
    st_spec = pl.BlockSpec((nb, n_state), lambda i: (0, 0))
    kern = functools.partial(_s5_kernel, tc=tc, nb=nb, lane_chunk=512)
    return pl.pallas_call(
        kern,
        grid=(rows // blk,),
        in_specs=[pl.BlockSpec((blk, d_ssm), lambda i: (i, 0)), st_spec, st_spec] + [full(p) for p in params],
        out_specs=[pl.BlockSpec((blk, d_ssm), lambda i: (i, 0)), st_spec, st_spec],
        out_shape=[jax.ShapeDtypeStruct((rows, d_ssm), BF16),
                   jax.ShapeDtypeStruct((nb, n_state), F32), jax.ShapeDtypeStruct((nb, n_state), F32)],
        scratch_shapes=[pltpu.VMEM((blk, n_state), F32), pltpu.VMEM((blk, n_state), F32)],
        compiler_params=_cparams(("arbitrary",)),
        name="s5",
    )(u_tb, st_re, st_im, *params)


PART_ROWS = 32


def _dsa_kernel(qi_ref, wit_ref, q_ref, ki_ref, k_ref, vt_ref, o_ref, key_s, bias_s, lg_s,
                *, qb0, tq, kt, sub, topk, seq_bits):
    qb = pl.program_id(1) + qb0
    nkt = ((qb * tq + tq + sub * kt - 1) // (sub * kt)) * sub
    q_pos = qb * tq + lax.broadcasted_iota(I32, (1, tq), 1)
    k_eff = jnp.minimum(topk, q_pos + 1).astype(F32)

    qi = qi_ref[0]
    wit = wit_ref[0]
    qih = [qi[:, h * IDX_DIM:(h + 1) * IDX_DIM] for h in range(IDX_HEADS)]

    def key_pos(t):
        return t * kt + lax.broadcasted_iota(I32, (kt, tq), 0)

    def score_tile(t, _):
        r0 = pl.multiple_of(t * kt, kt)
        ki_t = ki_ref[0, pl.ds(r0, kt), :]
        sc = jnp.zeros((kt, tq), F32)
        for h in range(IDX_HEADS):
            rel = lax.dot_general(ki_t, qih[h], (((1,), (1,)), ((), ())), preferred_element_type=F32)
            sc = sc + jnp.maximum(rel, 0.0) * wit[h:h + 1, :]
        bits = lax.bitcast_convert_type(sc, I32)
        key = jnp.where(bits < 0, bits ^ jnp.int32(0x7FFFFFFF), bits)
        key = jnp.where(key_pos(t) <= q_pos, key, jnp.int32(INT_MIN))
        key_s[pl.ds(r0, kt), :] = key
        return 0

    lax.fori_loop(0, nkt, score_tile, 0)

    def count(pred_fn):
        def body(t, acc):
            r0 = pl.multiple_of(t * kt, kt)
            m = pred_fn(key_s[pl.ds(r0, kt), :], t)
            ones = jnp.where(m, 1.0, 0.0).reshape(kt // PART_ROWS, PART_ROWS, tq)
            return acc + jnp.sum(ones, axis=0)
        acc = lax.fori_loop(0, nkt, body, jnp.zeros((PART_ROWS, tq), F32))
        return jnp.sum(acc, axis=0, keepdims=True)

    def bit_step(i, u):
        bit = jnp.left_shift(jnp.int32(1), 31 - i)
        cand_u = u | bit
        cand_s = cand_u ^ jnp.int32(INT_MIN)
        cnt = count(lambda kk, t: kk >= cand_s)
        return jnp.where(cnt >= k_eff, cand_u, u)

    u_thr = lax.fori_loop(0, 32, bit_step, jnp.zeros((1, tq), I32))
    thr = u_thr ^ jnp.int32(INT_MIN)

    cnt_ge = count(lambda kk, t: kk >= thr)
    cnt_gt = count(lambda kk, t: kk > thr)
    need_eq = k_eff - cnt_gt
    has_tie = jnp.max(cnt_ge - k_eff) > 0.0

    def tie_cut():
        def pos_step(i, c):
            bit = jnp.left_shift(jnp.int32(1), seq_bits - 1 - i)
            cand = c | bit
            cnt = count(lambda kk, t: (kk == thr) & (key_pos(t) < cand))
            return jnp.where(cnt < need_eq, cand, c)
        return lax.fori_loop(0, seq_bits, pos_step, jnp.zeros((1, tq), I32))

    cut = lax.cond(has_tie, tie_cut, lambda: jnp.full((1, tq), 2 ** seq_bits, I32))

    def bias_tile(t, _):
        r0 = pl.multiple_of(t * kt, kt)
        key = key_s[pl.ds(r0, kt), :]
        sel = (key > thr) | ((key == thr) & (key_pos(t) <= cut))
        bias_s[pl.ds(r0, kt), :] = jnp.where(sel, 0.0, NEG_BIG)
        return 0

    lax.fori_loop(0, nkt, bias_tile, 0)

    q = q_ref[0]
    grp = ATTN_HEADS // ATTN_KV_HEADS
    pairs_per_kv = grp // 2
    n_unit = ATTN_KV_HEADS * pairs_per_kv
    wq = 2 * tq
    qpair = [jnp.concatenate([q[:, (2 * u) * HEAD_DIM:(2 * u + 1) * HEAD_DIM],
                              q[:, (2 * u + 1) * HEAD_DIM:(2 * u + 2) * HEAD_DIM]], axis=0)
             for u in range(n_unit)]

    def col_reduce(x, op):
        part = op(x.reshape(kt // PART_ROWS, PART_ROWS, wq), axis=0)
        return op(part, axis=0, keepdims=True)

    def attn_tile(t, carry):
        ms, ls, accs = list(carry[0]), list(carry[1]), list(carry[2])
        for hf in range(sub):
            r0 = pl.multiple_of((t * sub + hf) * kt, kt)
            bias = bias_s[pl.ds(r0, kt), :]
            bias2 = jnp.concatenate([bias, bias], axis=1)
            for u in range(n_unit):
                k_t = k_ref[0, u // pairs_per_kv, pl.ds(r0, kt), :]
                lg_s[hf, u] = lax.dot_general(k_t, qpair[u], (((1,), (1,)), ((), ())),
                                              preferred_element_type=F32) + bias2
        for hf in range(sub):
            for u in range(n_unit):
                lg = lg_s[hf, u]
                m_new = jnp.maximum(ms[u], col_reduce(lg, jnp.max))
                p = jnp.exp2(lg - m_new)
                alpha = jnp.exp2(ms[u] - m_new)
                v_t = vt_ref[0, u // pairs_per_kv, t * sub + hf]
                ls[u] = alpha * ls[u] + col_reduce(p, jnp.sum)
                accs[u] = alpha * accs[u] + jnp.dot(v_t, p.astype(BF16), preferred_element_type=F32)
                ms[u] = m_new
        return tuple(ms), tuple(ls), tuple(accs)

    init = (tuple(jnp.full((1, wq), NEG_BIG, F32) for _ in range(n_unit)),
            tuple(jnp.zeros((1, wq), F32) for _ in range(n_unit)),
            tuple(jnp.zeros((HEAD_DIM, wq), F32) for _ in range(n_unit)))
    _, ls, accs = lax.fori_loop(0, nkt // sub, attn_tile, init)
    for n in range(ATTN_KV_HEADS):
        o_ref[0, 0, n] = jnp.concatenate([accs[n * pairs_per_kv + pg] / ls[n * pairs_per_kv + pg]
                                          for pg in range(pairs_per_kv)], axis=1).astype(BF16)


def _dsa(q, qi, wit, ki, k4, vt, s0, tq, kt):
    bsz, sc, _ = q.shape
    seq = ki.shape[1]
    topk = min(TOPK_MAX, seq // 4)
    nqb = sc // tq
    grp = ATTN_HEADS // ATTN_KV_HEADS
    seq_bits = int(math.log2(seq))
    assert 2 ** seq_bits == seq
    sub = 2 if seq % (2 * kt) == 0 else 1
    n_unit = ATTN_HEADS // 2
    kern = functools.partial(_dsa_kernel, qb0=s0 // tq, tq=tq, kt=kt, sub=sub, topk=topk, seq_bits=seq_bits)
    o_t = pl.pallas_call(
        kern,
        grid=(bsz, nqb),
        in_specs=[pl.BlockSpec((1, tq, IDX_HEADS * IDX_DIM), lambda b, j: (b, j, 0)),
                  pl.BlockSpec((1, IDX_HEADS, tq), lambda b, j: (b, 0, j)),
                  pl.BlockSpec((1, tq, ATTN_HEADS * HEAD_DIM), lambda b, j: (b, j, 0)),
                  pl.BlockSpec((1, seq, IDX_DIM), lambda b, j: (b, 0, 0)),
                  pl.BlockSpec((1, ATTN_KV_HEADS, seq, HEAD_DIM), lambda b, j: (b, 0, 0, 0)),
                  pl.BlockSpec((1, ATTN_KV_HEADS, seq // kt, HEAD_DIM, kt), lambda b, j: (b, 0, 0, 0, 0))],
        out_specs=pl.BlockSpec((1, 1, ATTN_KV_HEADS, HEAD_DIM, grp * tq), lambda b, j: (b, j, 0, 0, 0)),
        out_shape=jax.ShapeDtypeStruct((bsz, nqb, ATTN_KV_HEADS, HEAD_DIM, grp * tq), BF16),
        scratch_shapes=[pltpu.VMEM((seq, tq), I32), pltpu.VMEM((seq, tq), F32),
                        pltpu.VMEM((sub, n_unit, kt, 2 * tq), F32)],
        compiler_params=_cparams(("arbitrary", "arbitrary")),
        name="dsa",
    )(qi, wit, q, ki, k4, vt)
    o = o_t.reshape(bsz, nqb, ATTN_KV_HEADS, HEAD_DIM, grp, tq).transpose(0, 1, 5, 2, 4, 3)
    return o.reshape(bsz, sc, ATTN_HEADS * HEAD_DIM)


def _merge_kernel(x_ref, ys_ref, ya_ref, gs_ref, ga_ref, wsu_ref, wau_ref, wout_ref, g2_ref, wq_ref,
                  h_ref, hn_ref, qp_ref):
    ms = jnp.dot(ys_ref[0], wsu_ref[...], preferred_element_type=F32)
    ma = jnp.dot(ya_ref[0], wau_ref[...], preferred_element_type=F32)
    merged = gs_ref[0].astype(F32) * ms + ga_ref[0].astype(F32) * ma
    h = x_ref[0] + jnp.dot(merged.astype(BF16), wout_ref[...], preferred_element_type=F32)
    h_ref[0] = h
    hb = _rms(h, g2_ref[...]).astype(BF16)
    bits = lax.bitcast_convert_type(hb.astype(F32), I32)
    half = bits.shape[1] // 2
    words = (bits[:, half:] & jnp.int32(-65536)) | lax.shift_right_logical(bits[:, :half], 16)
    n_seg = half // LANES
    tm = words.shape[0]
    for p in range(n_seg):
        hn_ref[0, pl.ds(p, tm, stride=n_seg), :] = words[:, p * LANES:(p + 1) * LANES]
    qp_ref[0] = jnp.dot(hb, wq_ref[...], preferred_element_type=F32).astype(BF16)


def _merge(x, ys, ya, gs, ga, wsu, wau, wo, norm2_g, wq, s0, tm):
    bsz, sc, _ = ya.shape
    d = x.shape[2]
    i0 = s0 // tm
    loc = lambda a: pl.BlockSpec((1, tm, a.shape[2]), lambda b, i: (b, i, 0))
    full = lambda a: pl.BlockSpec(a.shape, lambda b, i: (0,) * a.ndim)
    g2 = norm2_g.reshape(1, d)
    nq = wq.shape[1]
    n_seg = d // 2 // LANES
    out = lambda n: pl.BlockSpec((1, tm, n), lambda b, i: (b, i, 0))
    return pl.pallas_call(
        _merge_kernel,
        grid=(bsz, sc // tm),
        in_specs=[pl.BlockSpec((1, tm, d), lambda b, i: (b, i + i0, 0)), loc(ys), loc(ya), loc(gs), loc(ga),
                  full(wsu), full(wau), full(wo), full(g2), full(wq)],
        out_specs=[out(d), pl.BlockSpec((1, tm * n_seg, LANES), lambda b, i: (b, i, 0)), out(nq)],
        out_shape=[jax.ShapeDtypeStruct((bsz, sc, d), F32), jax.ShapeDtypeStruct((bsz, sc * n_seg, LANES), I32),
                   jax.ShapeDtypeStruct((bsz, sc, nq), BF16)],
        compiler_params=_cparams(("arbitrary", "arbitrary")),
        name="merge",
    )(x, ys, ya, gs, ga, wsu, wau, wo, g2, wq)


def _cand_layout():
    blocks = []
    blocks.append((0, 16, 16))
    for i in range(1, 8):
        blocks.append((i, 8, PEER_TOPK // (i + 1)))
    blocks.append((None, 8, 8))
    return blocks


def _top_rows(s, order, payload, k):
    big = jnp.float32(3e38)
    vals, pays = [], []
    for _ in range(k):
        m = jnp.max(s, axis=0, keepdims=True)
        o = jnp.min(jnp.where(s == m, order, big), axis=0, keepdims=True)
        hit = order == o
        pays.append(jnp.min(jnp.where(hit, payload, big), axis=0, keepdims=True))
        vals.append(m)
        s = jnp.where(hit, -jnp.inf, s)
    return jnp.concatenate(vals, axis=0), jnp.concatenate(pays, axis=0)


def _route_kernel(qp_ref, k1_ref, k2_ref, e_ref, g_ref, *, tt):
    qp = qp_ref[...]
    kd = PEER_KEY_DIM
    rows_k = lax.broadcasted_iota(I32, (PEER_KEYS, tt), 0).astype(F32)
    for h in range(PEER_HEADS):
        q1 = qp[:, (2 * h) * kd:(2 * h + 1) * kd]
        q2 = qp[:, (2 * h + 1) * kd:(2 * h + 2) * kd]
        s1 = lax.dot_general(k1_ref[h], q1, (((1,), (1,)), ((), ())), preferred_element_type=F32)
        s2 = lax.dot_general(k2_ref[h], q2, (((1,), (1,)), ((), ())), preferred_element_type=F32)
        v1, i1 = _top_rows(s1, rows_k, rows_k, PEER_TOPK)
        v2, i2 = _top_rows(s2, rows_k, rows_k, PEER_TOPK)
        cs, ce, co = [], [], []
        for i, rows, valid in _cand_layout():
            r = lax.broadcasted_iota(I32, (rows, tt), 0).astype(F32)
            if i is None:
                val = v1[8:16] + v2[0:1]
                eid = i1[8:16] * PEER_KEYS + i2[0:1]
                flat = (r + 8.0) * PEER_TOPK
            else:
                val = v1[i:i + 1] + v2[0:rows]
                eid = i1[i:i + 1] * PEER_KEYS + i2[0:rows]
                flat = r + float(i * PEER_TOPK)
                if valid < rows:
                    val = jnp.where(r < float(valid), val, -jnp.inf)
            cs.append(val)
            ce.append(eid)
            co.append(flat)
        cand = jnp.concatenate(cs, axis=0)
        top_s, top_e = _top_rows(cand, jnp.concatenate(co, axis=0), jnp.concatenate(ce, axis=0), PEER_TOPK)
        p = jnp.exp(top_s - top_s[0:1])
        gates = p / jnp.sum(p, axis=0, keepdims=True)
        e_ref[h * PEER_TOPK:(h + 1) * PEER_TOPK, :] = top_e.astype(I32)
        g_ref[h * PEER_TOPK:(h + 1) * PEER_TOPK, :] = gates


def _route(qp, k1, k2, tt):
    n, nq = qp.shape
    n_sel = PEER_HEADS * PEER_TOPK
    full = lambda a: pl.BlockSpec(a.shape, lambda i: (0,) * a.ndim)
    return pl.pallas_call(
        functools.partial(_route_kernel, tt=tt),
        grid=(n // tt,),
        in_specs=[pl.BlockSpec((tt, nq), lambda i: (i, 0)), full(k1), full(k2)],
        out_specs=[pl.BlockSpec((n_sel, tt), lambda i: (0, i)), pl.BlockSpec((n_sel, tt), lambda i: (0, i))],
        out_shape=[jax.ShapeDtypeStruct((n_sel, n), I32), jax.ShapeDtypeStruct((n_sel, n), F32)],
        compiler_params=_cparams(("arbitrary",)),
        name="route",
    )(qp, k1, k2)


def _final_kernel(h_ref, p_ref, g_ref, o_ref):
    tm, d = h_ref.shape
    n_seg = d // LANES
    p = jnp.concatenate([p_ref[pl.ds(k, tm, stride=n_seg), :] for k in range(n_seg)], axis=1)
    o_ref[...] = _rms(h_ref[...] + p, g_ref[...])


def _final(h, p, g, tm):
    n, d = h.shape
    row = pl.BlockSpec((tm, d), lambda i: (i, 0))
    return pl.pallas_call(
        _final_kernel,
        grid=(n // tm,),
        in_specs=[row, pl.BlockSpec((tm * (d // LANES), LANES), lambda i: (i, 0)),
                  pl.BlockSpec((1, d), lambda i: (0, 0))],
        out_specs=row,
        out_shape=jax.ShapeDtypeStruct((n, d), F32),
        compiler_params=_cparams(("arbitrary",)),
        name="final",
    )(h, p, g.reshape(1, d))


SC_CORES_V7X = 2
SC_SUBCORES_V7X = 16
SC_LANES_V7X = 16
PEER_TOK_BATCH = 16
PEER_ROW_CHUNK = 32
PEER_RING = 4


def _pack_bf16_pairs(t):
    half = t.shape[1] // 2
    tb = t.astype(BF16)
    lo = lax.bitcast_convert_type(tb[:, :half], jnp.uint16).astype(jnp.uint32)
    hi = lax.bitcast_convert_type(tb[:, half:], jnp.uint16).astype(jnp.uint32)
    return lax.bitcast_convert_type(lo | (hi << 16), I32)


def _unpack_pair(w):
    lo = lax.bitcast_convert_type(jnp.left_shift(w, 16), F32)
    hi = lax.bitcast_convert_type(w & jnp.int32(-65536), F32)
    return lo, hi


def _peer_sc_body(hn_hbm, e_hbm, g_hbm, u_hbm, v_hbm, out_hbm,
                  idx_v, gate_v, x_v, out_v, rows, p_v, act_v, sem, *, tpw, d, n_sel):
    nl = SC_LANES_V7X
    tb = PEER_TOK_BATCH
    rc = PEER_ROW_CHUNK
    n_chunk = n_sel // rc
    jobs_per_tok = 2 * n_chunk
    half = d // 2
    n_lane_blk = half // nl
    xs = half // LANES
    os_ = d // LANES
    wid =lax.axis_index("s") * SC_CORES_V7X + lax.axis_index("c")
    base = wid * tpw
    lane = lax.iota(I32, nl)
    zero = jnp.zeros((nl,), F32)
    c_gelu = 2.0 * math.sqrt(2.0 / math.pi)

    def gather_copy(tab_hbm, job):
        tok = job // jobs_per_tok
        c = (job % jobs_per_tok) % n_chunk
        b = job % PEER_RING
        return pltpu.make_async_copy(tab_hbm.at[idx_v.at[tok, pl.ds(c * rc, rc)]], rows.at[b], sem.at[b])

    def start(job):
        j = job % jobs_per_tok

        @pl.when(j < n_chunk)
        def _():
            gather_copy(u_hbm, job).start()

        @pl.when(j >= n_chunk)
        def _():
            gather_copy(v_hbm, job).start()

    def compute_u(tok, c, b):
        def rg_body(rg, _):
            r0 = rg * 8

            def jbody(j2, accs):
                off0 = j2 * (2 * nl)
                off1 = off0 + nl
                xrow = tok * xs + off0 // LANES
                xl = off0 % LANES
                x0 = plsc.bitcast(x_v[xrow, pl.ds(xl, nl)], BF16)
                x1 = plsc.bitcast(x_v[xrow, pl.ds(xl + nl, nl)], BF16)
                new = []
                for r in range(8):
                    w0 = plsc.bitcast(rows[b, r0 + r, pl.ds(off0, nl)], BF16)
                    w1 = plsc.bitcast(rows[b, r0 + r, pl.ds(off1, nl)], BF16)
                    lo, hi = _unpack_pair(plsc.bitcast(w0 * x0 + w1 * x1, I32))
                    new.append(accs[r] + (lo + hi))
                return tuple(new)

            accs = lax.fori_loop(0, n_lane_blk // 2, jbody, (zero,) * 8)
            for r in range(8):
                p_v[c * rc + r0 + r, :] = accs[r]
            return 0

        lax.fori_loop(0, rc // 8, rg_body, 0)

    def finish_act(tok):
        def eg_body(eg, _):
            e0 = eg * nl
            ridx = e0 + lane
            s = zero
            for l in range(nl):
                s = s + plsc.load_gather(p_v, [ridx, jnp.full((nl,), l, I32)])
            inner = c_gelu * (s + 0.044715 * (s * s * s))
            gl = s / (1.0 + jnp.exp(-inner))
            a = gl * gate_v[tok, pl.ds(e0, nl)]
            bits = lax.bitcast_convert_type(a, I32)
            rnd = bits + jnp.int32(0x7FFF) + (lax.shift_right_logical(bits, 16) & 1)
            hi16 = rnd & jnp.int32(-65536)
            act_v[pl.ds(e0, nl)] = hi16 | lax.shift_right_logical(hi16, 16)
            return 0

        lax.fori_loop(0, n_sel // nl, eg_body, 0)

        def zbody(j, _):
            off = j * nl
            out_v[tok * os_ + off // LANES, pl.ds(off % LANES, nl)] = zero
            return 0

        lax.fori_loop(0, d // nl, zbody, 0, unroll=4)

    def compute_v(tok, c, b):
        def rg_body(rg, _):
            r0 = rg * nl
            splat = [plsc.bitcast(plsc.load_gather(act_v, [jnp.full((nl,), 0, I32) + (c * rc + r0 + r)]), BF16)
                     for r in range(nl)]

            def tree(parts):
                while len(parts) > 1:
                    parts = [parts[i] + parts[i + 1] for i in range(0, len(parts), 2)]
                return parts[0]

            @plsc.parallel_loop(0, n_lane_blk, unroll=2)
            def _(j):
                off = j * nl
                los, his = [], []
                for r in range(0, nl, 2):
                    w0 = plsc.bitcast(rows[b, r0 + r, pl.ds(off, nl)], BF16)
                    w1 = plsc.bitcast(rows[b, r0 + r + 1, pl.ds(off, nl)], BF16)
                    lo, hi = _unpack_pair(plsc.bitcast(w0 * splat[r] + w1 * splat[r + 1], I32))
                    los.append(lo)
                    his.append(hi)
                orow = tok * os_ + off // LANES
                ol = off % LANES
                out_v[orow, pl.ds(ol, nl)] = out_v[orow, pl.ds(ol, nl)] + tree(los)
                out_v[orow + xs, pl.ds(ol, nl)] = out_v[orow + xs, pl.ds(ol, nl)] + tree(his)

            return 0

        lax.fori_loop(0, rc // nl, rg_body, 0)

    def batch_body(bi, _):
        t0 = base + bi * tb
        pltpu.sync_copy(e_hbm.at[pl.ds(t0, tb)], idx_v)
        pltpu.sync_copy(g_hbm.at[pl.ds(t0, tb)], gate_v)
        pltpu.sync_copy(hn_hbm.at[pl.ds(t0 * xs, tb * xs)], x_v)
        for pre in range(PEER_RING - 1):
            start(pre)

        def job_body(job, _):
            @pl.when(job + (PEER_RING - 1) < tb * jobs_per_tok)
            def _():
                start(job + (PEER_RING - 1))

            j = job % jobs_per_tok
            gather_copy(u_hbm, job).wait()
            tok = job // jobs_per_tok
            b = job % PEER_RING

            @pl.when(j < n_chunk)
            def _():
                compute_u(tok, j, b)

            @pl.when(j == n_chunk - 1)
            def _():
                finish_act(tok)

            @pl.when(j >= n_chunk)
            def _():
                compute_v(tok, j - n_chunk, b)

            return 0

        lax.fori_loop(0, tb * jobs_per_tok, job_body, 0)
        pltpu.sync_copy(out_v, out_hbm.at[pl.ds(t0 * os_, tb * os_)])
        return 0

    lax.fori_loop(0, tpw // tb, batch_body, 0)


def _peer_sc(x_pk, experts, gates, u_tab, v_tab):
    d = 2 * u_tab.shape[1]
    n = x_pk.shape[0] * LANES // (d // 2)
    n_sel = experts.shape[1]
    nw = SC_CORES_V7X * SC_SUBCORES_V7X
    tpw = n // nw
    mesh = plsc.VectorSubcoreMesh(core_axis_name="c", subcore_axis_name="s",
                                  num_cores=SC_CORES_V7X, num_subcores=SC_SUBCORES_V7X)
    body = functools.partial(_peer_sc_body, tpw=tpw, d=d, n_sel=n_sel)
    call = pl.kernel(
        body,
        out_type=jax.ShapeDtypeStruct((n * d // LANES, LANES), F32),
        mesh=mesh,
        scratch_types=[pltpu.VMEM((PEER_TOK_BATCH, n_sel), I32),
                       pltpu.VMEM((PEER_TOK_BATCH, n_sel), F32),
                       pltpu.VMEM((PEER_TOK_BATCH * d // 2 // LANES, LANES), I32),
                       pltpu.VMEM((PEER_TOK_BATCH * d // LANES, LANES), F32),
                       pltpu.VMEM((PEER_RING, PEER_ROW_CHUNK, d // 2), I32),
                       pltpu.VMEM((n_sel, SC_LANES_V7X), F32),
                       pltpu.VMEM((n_sel,), I32),
                       pltpu.SemaphoreType.DMA((PEER_RING,))],
        compiler_params=pltpu.CompilerParams(needs_layout_passes=False, use_tc_tiling_on_sc=False),
        name="peer_sc",
    )
    return call(x_pk, experts, gates, u_tab, v_tab)


def kernel(x, norm1_g, w_in, a_re, a_im, log_dt, b_re, b_im, c_re, c_im, d_skip, w_glu, w_ssm_up, w_attn_up,
           w_out, norm2_g, peer_wq, peer_k1, peer_k2, peer_u, peer_v, norm_f_g):
    bsz, seq, d = x.shape
    depth = norm1_g.shape[0]
    chunks = _time_chunks(seq)
    h = x
    for layer in range(depth):
        last = layer + 1 == depth
        kv_w, main_w = _in_weights(w_in[layer], seq)
        s5p = _s5_params(a_re[layer], a_im[layer], log_dt[layer], b_re[layer], b_im[layer], c_re[layer],
                         c_im[layer], d_skip[layer], w_glu[layer], nb=bsz)
        n_state = s5p[4].shape[1]
        wsu = w_ssm_up[layer].astype(BF16)
        wau = w_attn_up[layer].astype(BF16)
        wo = w_out[layer].astype(BF16)
        wq = peer_wq[layer].astype(BF16)
        k1 = peer_k1[layer].astype(BF16)
        k2 = peer_k2[layer].astype(BF16)
        u_pk = _pack_bf16_pairs(peer_u[layer])
        v_pk = _pack_bf16_pairs(peer_v[layer])
        k4, vt, ki = _kvproj(h, norm1_g[layer], kv_w, tm=min(ROW_TILE, seq), kt=DSA_KT)
        st_re = jnp.zeros((bsz, n_state), F32)
        st_im = jnp.zeros((bsz, n_state), F32)
        outs = []
        routed = ki
        peer_outs = []
        for c, (s0, sc) in enumerate(chunks):
            tm = math.gcd(math.gcd(s0, sc), ROW_TILE)
            after =(routed, peer_outs[c - SC_LAG] if c >= SC_LAG else ki)
            u, q, qi, wit, gs, ga = _inproj(h, norm1_g[layer], main_w, after, s0=s0, sc=sc, tm=tm)
            d_ssm = u.shape[-1]
            u_tb = u.transpose(1, 0, 2).reshape(sc * bsz, d_ssm)
            y_tb, st_re, st_im = _s5(u_tb, st_re, st_im, s5p, nb=bsz, tc=64)
            ys = y_tb.reshape(sc, bsz, d_ssm).transpose(1, 0, 2)
            ya = _dsa(q, qi, wit, ki, k4, vt, s0=s0, tq=DSA_TQ, kt=DSA_KT)
            hm, x_pk, qp = _merge(h, ys, ya, gs, ga, wsu, wau, wo, norm2_g[layer], wq, s0=s0, tm=tm)
            nt = bsz * sc
            e_t, g_t = _route(qp.reshape(nt, -1), k1, k2, tt=256)
            routed = e_t
            po = _peer_sc(x_pk.reshape(-1, LANES), e_t.T, g_t.T, u_pk, v_pk)
            peer_outs.append(po)
            hm2 = hm.reshape(nt, d)
            o = _final(hm2, po, norm_f_g, tm=tm) if last else hm2 + po.reshape(nt, d)
            outs.append(o.reshape(bsz, sc, d))
        h = jnp.concatenate(outs, axis=1)
    return h
```

```python
import functools
import math

import numpy as np
import jax
import jax.numpy as jnp
from jax import lax
from jax.experimental import pallas as pl
from jax.experimental.pallas import tpu as pltpu
from jax.experimental.pallas import tpu_sc as plsc

F32 = jnp.float32
BF16 = jnp.bfloat16
I32 = jnp.int32

SSM_GROUP = 16
SSM_STATE = 64
ATTN_HEADS = 8
ATTN_KV_HEADS = 2
HEAD_DIM = 64
IDX_HEADS = 8
IDX_DIM = 32
TOPK_MAX = 256
ROPE_THETA = 10000.0
NEG_BIG = -1e30
PEER_HEADS = 8
PEER_KEYS = 128
PEER_KEY_DIM = 128
PEER_TOPK = 16
NORM_EPS = 1e-6

TIME_SPLIT_16THS = (1, 2, 2, 2, 2, 2, 2, 1, 1, 1)
SC_LAG = 3
ROW_TILE = 512
DSA_TQ = 128
DSA_KT = 256
LANES = 128
INT_MIN = -(2 ** 31)
VMEM_LIMIT = 56 * 1024 * 1024


def _time_chunks(seq):
    unit = seq // 16
    if seq % 16 == 0 and unit % DSA_KT == 0:
        sizes = [f * unit for f in TIME_SPLIT_16THS]
    else:
        step = min(ROW_TILE, seq)
        sizes = [step] * (seq // step)
    assert sum(sizes) == seq
    starts = np.cumsum([0] + sizes[:-1]).tolist()
    return list(zip(starts, sizes))


def _cparams(sem):
    return pltpu.CompilerParams(dimension_semantics=sem, vmem_limit_bytes=VMEM_LIMIT)


def _gelu_tanh(x):
    return 0.5 * x * (1.0 + jnp.tanh(math.sqrt(2.0 / math.pi) * (x + 0.044715 * (x * x * x))))


def _sigmoid(x):
    return 1.0 / (1.0 + jnp.exp(-x))


def _rms(x, g):
    return x * lax.rsqrt(jnp.mean(x * x, axis=-1, keepdims=True) + NORM_EPS) * g


def _rot_cols(w, hd):
    d, n = w.shape
    w3 = w.reshape(d, n // hd, hd)
    half = hd // 2
    return jnp.concatenate([-w3[..., half:], w3[..., :half]], axis=-1).reshape(d, n)


def _rope_full(seq, hd, heads):
    pos = jnp.arange(seq, dtype=F32)
    inv = ROPE_THETA ** (-jnp.arange(0, hd, 2, dtype=F32) / hd)
    ang = pos[:, None] * inv[None, :]
    c = jnp.concatenate([jnp.cos(ang), jnp.cos(ang)], axis=-1)
    s = jnp.concatenate([jnp.sin(ang), jnp.sin(ang)], axis=-1)
    return jnp.tile(c, (1, heads)), jnp.tile(s, (1, heads))


def _in_weights(w_in, seq):
    d = w_in.shape[0]
    d_ssm = d // 2
    d_q = ATTN_HEADS * HEAD_DIM
    d_kv = ATTN_KV_HEADS * HEAD_DIM
    d_qi = IDX_HEADS * IDX_DIM
    splits = (d_ssm, d_q, d_kv, d_kv, d_qi, IDX_DIM, IDX_HEADS, d, d)
    offs = np.cumsum(splits)[:-1].tolist()
    wu, wq, wk, wv, wqi, wki, wwi, wgs, wga = jnp.split(w_in, offs, axis=1)
    pad = jnp.zeros((d, 128 - IDX_DIM), F32)
    cq, sq = _rope_full(seq, HEAD_DIM, ATTN_HEADS)
    ck, sk = _rope_full(seq, HEAD_DIM, ATTN_KV_HEADS)
    cqi, sqi = _rope_full(seq, IDX_DIM, IDX_HEADS)
    cki, ski = _rope_full(seq, IDX_DIM, 1)
    tpad = jnp.zeros((seq, 128 - IDX_DIM), F32)
    kv = dict(
        w=jnp.concatenate([wk, wki, pad], axis=1).astype(BF16),
        wvt=wv.T.astype(BF16),
        wr=jnp.concatenate([_rot_cols(wk, HEAD_DIM), _rot_cols(wki, IDX_DIM), pad], axis=1).astype(BF16),
        cs=jnp.concatenate([ck, cki, tpad], axis=1), sn=jnp.concatenate([sk, ski, tpad], axis=1))
    main = dict(
        w=jnp.concatenate([wu, wq, wqi, wgs, wga], axis=1).astype(BF16),
        wr=jnp.concatenate([_rot_cols(wq, HEAD_DIM), _rot_cols(wqi, IDX_DIM)], axis=1).astype(BF16),
        cs=jnp.concatenate([cq, cqi], axis=1), sn=jnp.concatenate([sq, sqi], axis=1),
        wwit=wwi.T.astype(BF16))
    return kv, main


def _kvproj_kernel(x_ref, g_ref, w_ref, wr_ref, wvt_ref, cs_ref, sn_ref, k_ref, vt_ref, ki_ref, *, d_kv, kt):
    xb = _rms(x_ref[0], g_ref[...]).astype(BF16)

    def mm(ref, lo, n):
        return jnp.dot(xb, ref[:, lo:lo + n], preferred_element_type=F32)

    k = mm(w_ref, 0, d_kv) * cs_ref[:, :d_kv] + mm(wr_ref, 0, d_kv) * sn_ref[:, :d_kv]
    for n in range(ATTN_KV_HEADS):
        k_ref[0, n] = k[:, n * HEAD_DIM:(n + 1) * HEAD_DIM].astype(BF16)
    vt = lax.dot_general(wvt_ref[...], xb, (((1,), (1,)), ((), ())), preferred_element_type=F32)
    for n in range(ATTN_KV_HEADS):
        for j in range(vt.shape[1] // kt):
            vt_ref[0, n, j] = vt[n * HEAD_DIM:(n + 1) * HEAD_DIM, j * kt:(j + 1) * kt].astype(BF16)
    kiw = (mm(w_ref, d_kv, 128) * cs_ref[:, d_kv:d_kv + 128]
           + mm(wr_ref, d_kv, 128) * sn_ref[:, d_kv:d_kv + 128])
    ki_ref[0] = kiw[:, :IDX_DIM].astype(BF16)


def _kvproj(x, norm_g, kv, tm, kt):
    bsz, seq, d = x.shape
    d_kv = ATTN_KV_HEADS * HEAD_DIM
    full = lambda a: pl.BlockSpec(a.shape, lambda s, b: (0,) * a.ndim)
    g = norm_g.reshape(1, d)
    ncs = kv["cs"].shape[1]
    return pl.pallas_call(
        functools.partial(_kvproj_kernel, d_kv=d_kv, kt=kt),
        grid=(seq // tm, bsz),
        in_specs=[pl.BlockSpec((1, tm, d), lambda s, b: (b, s, 0)), full(g), full(kv["w"]), full(kv["wr"]),
                  full(kv["wvt"]),
                  pl.BlockSpec((tm, ncs), lambda s, b: (s, 0)), pl.BlockSpec((tm, ncs), lambda s, b: (s, 0))],
        out_specs=[pl.BlockSpec((1, ATTN_KV_HEADS, tm, HEAD_DIM), lambda s, b: (b, 0, s, 0)),
                   pl.BlockSpec((1, ATTN_KV_HEADS, tm // kt, HEAD_DIM, kt), lambda s, b: (b, 0, s, 0, 0)),
                   pl.BlockSpec((1, tm, IDX_DIM), lambda s, b: (b, s, 0))],
        out_shape=[jax.ShapeDtypeStruct((bsz, ATTN_KV_HEADS, seq, HEAD_DIM), BF16),
                   jax.ShapeDtypeStruct((bsz, ATTN_KV_HEADS, seq // kt, HEAD_DIM, kt), BF16),
                   jax.ShapeDtypeStruct((bsz, seq, IDX_DIM), BF16)],
        compiler_params=_cparams(("arbitrary", "arbitrary")),
        name="kvproj",
    )(x, g, kv["w"], kv["wr"], kv["wvt"], kv["cs"], kv["sn"])


def _inproj_kernel(x_ref, g_ref, w_ref, wr_ref, cs_ref, sn_ref, wwit_ref, after_tc_ref, after_sc_ref,
                   u_ref, q_ref, qi_ref, wit_ref, gs_ref, ga_ref, *, d_ssm, d_q, d_qi, d_model, q_scale, wi_scale):
    del after_tc_ref, after_sc_ref
    xb = _rms(x_ref[0], g_ref[...]).astype(BF16)

    def mm(ref, lo, n):
        return jnp.dot(xb, ref[:, lo:lo + n], preferred_element_type=F32)

    o = 0
    u_ref[0] = mm(w_ref, o, d_ssm).astype(BF16)
    o += d_ssm
    q = mm(w_ref, o, d_q) * cs_ref[:, :d_q] + mm(wr_ref, 0, d_q) * sn_ref[:, :d_q]
    q_ref[0] = (q * q_scale).astype(BF16)
    o += d_q
    qi = mm(w_ref, o, d_qi) * cs_ref[:, d_q:d_q + d_qi] + mm(wr_ref, d_q, d_qi) * sn_ref[:, d_q:d_q + d_qi]
    qi_ref[0] = qi.astype(BF16)
    o += d_qi
    gs_ref[0] = _sigmoid(mm(w_ref, o, d_model)).astype(BF16)
    o += d_model
    ga_ref[0] = _sigmoid(mm(w_ref, o, d_model)).astype(BF16)
    wit_ref[0] = lax.dot_general(wwit_ref[...], xb, (((1,), (1,)), ((), ())),
                                 preferred_element_type=F32) * wi_scale


def _inproj(x, norm_g, main, after, s0, sc, tm):
    bsz, _, d = x.shape
    d_ssm = d // 2
    d_q = ATTN_HEADS * HEAD_DIM
    d_qi = IDX_HEADS * IDX_DIM
    i0 = s0 // tm
    kern = functools.partial(
        _inproj_kernel, d_ssm=d_ssm, d_q=d_q, d_qi=d_qi, d_model=d,
        q_scale=HEAD_DIM ** -0.5 * math.log2(math.e), wi_scale=(IDX_HEADS ** -0.5) * (IDX_DIM ** -0.5))
    tok = lambda n: pl.BlockSpec((1, tm, n), lambda s, b: (b, s, 0))
    full = lambda a: pl.BlockSpec(a.shape, lambda s, b: (0,) * a.ndim)
    g = norm_g.reshape(1, d)
    ncs = main["cs"].shape[1]
    outs = [(d_ssm, BF16), (d_q, BF16), (d_qi, BF16)]
    return pl.pallas_call(
        kern,
        grid=(sc // tm, bsz),
        in_specs=[pl.BlockSpec((1, tm, d), lambda s, b: (b, s + i0, 0)), full(g), full(main["w"]), full(main["wr"]),
                  pl.BlockSpec((tm, ncs), lambda s, b: (s + i0, 0)),
                  pl.BlockSpec((tm, ncs), lambda s, b: (s + i0, 0)), full(main["wwit"]),
                  pl.BlockSpec(memory_space=pl.ANY), pl.BlockSpec(memory_space=pl.ANY)],
        out_specs=[tok(n) for n, _ in outs] + [pl.BlockSpec((1, IDX_HEADS, tm), lambda s, b: (b, 0, s)),
                                                tok(d), tok(d)],
        out_shape=[jax.ShapeDtypeStruct((bsz, sc, n), dt) for n, dt in outs]
        + [jax.ShapeDtypeStruct((bsz, IDX_HEADS, sc), F32),
           jax.ShapeDtypeStruct((bsz, sc, d), BF16), jax.ShapeDtypeStruct((bsz, sc, d), BF16)],
        compiler_params=_cparams(("arbitrary", "arbitrary")),
        name="inproj",
    )(x, g, main["w"], main["wr"], main["cs"], main["sn"], main["wwit"], *after)


def _s5_kernel(u_ref, sre_in, sim_in, bre_ref, bim_ref, cre_ref, cim_ref, are_ref, aim_ref, dsk_ref, wglu_ref,
               y_ref, st_re, st_im, sre, sim, *, tc, nb, lane_chunk):
    @pl.when(pl.program_id(0) == 0)
    def _():
        st_re[...] = sre_in[...]
        st_im[...] = sim_in[...]

    u = u_ref[...]
    n_half = bre_ref.shape[0]
    hin = bre_ref.shape[1]
    hst = bre_ref.shape[2]
    for h in range(n_half):
        uh = u[:, h * hin:(h + 1) * hin]
        sre[:, h * hst:(h + 1) * hst] = jnp.dot(uh, bre_ref[h], preferred_element_type=F32)
        sim[:, h * hst:(h + 1) * hst] = jnp.dot(uh, bim_ref[h], preferred_element_type=F32)

    n_state = sre.shape[1]
    for c in range(n_state // lane_chunk):
        cols = slice(c * lane_chunk, (c + 1) * lane_chunk)
        ar = are_ref[:, cols]
        ai = aim_ref[:, cols]

        def step(t, carry, cols=cols, ar=ar, ai=ai):
            sr, si = carry
            r0 = pl.multiple_of(t * nb, nb)
            nr = ar * sr - ai * si + sre[pl.ds(r0, nb), cols]
            ni = ar * si + ai * sr + sim[pl.ds(r0, nb), cols]
            sre[pl.ds(r0, nb), cols] = nr
            sim[pl.ds(r0, nb), cols] = ni
            return nr, ni

        sr, si = lax.fori_loop(0, tc, step, (st_re[:, cols], st_im[:, cols]), unroll=4)
        st_re[:, cols] = sr
        st_im[:, cols] = si

    ys = []
    for h in range(n_half):
        srh = sre[:, h * hst:(h + 1) * hst].astype(BF16)
        sih = sim[:, h * hst:(h + 1) * hst].astype(BF16)
        ys.append(jnp.dot(srh, cre_ref[h], preferred_element_type=F32)
                  - jnp.dot(sih, cim_ref[h], preferred_element_type=F32))
    y = jnp.concatenate(ys, axis=-1) + dsk_ref[...] * u.astype(F32)
    y = _gelu_tanh(y)
    gate = jnp.dot(y.astype(BF16), wglu_ref[...], preferred_element_type=F32)
    y_ref[...] = (y * _sigmoid(gate)).astype(BF16)


def _s5_params(a_re, a_im, log_dt, b_re, b_im, c_re, c_im, d_skip, w_glu, nb):
    groups = a_re.shape[0]
    d_ssm = groups * SSM_GROUP
    n_state = groups * SSM_STATE
    lam = lax.complex(a_re, a_im)
    dt = jnp.exp(log_dt)[:, None]
    a_bar = jnp.exp(lam * dt)
    b_bar = ((a_bar - 1.0) / lam)[..., None] * lax.complex(b_re, b_im)
    gh = min(groups, 256 // SSM_GROUP)
    n_half = groups // gh
    eye = jnp.eye(gh, dtype=F32)

    def bmat(bb):
        b4 = bb.reshape(n_half, gh, SSM_STATE, SSM_GROUP)
        return jnp.einsum('hgpc,gk->hgckp', b4, eye).reshape(n_half, gh * SSM_GROUP, gh * SSM_STATE)

    def cmat(cc):
        c4 = cc.reshape(n_half, gh, SSM_GROUP, SSM_STATE)
        return jnp.einsum('hgcp,gk->hgpkc', c4, eye).reshape(n_half, gh * SSM_STATE, gh * SSM_GROUP)

    return (bmat(jnp.real(b_bar)).astype(BF16), bmat(jnp.imag(b_bar)).astype(BF16),
            cmat(c_re).astype(BF16), cmat(c_im).astype(BF16),
            jnp.broadcast_to(jnp.real(a_bar).reshape(1, n_state), (nb, n_state)),
            jnp.broadcast_to(jnp.imag(a_bar).reshape(1, n_state), (nb, n_state)),
            d_skip.reshape(1, d_ssm), w_glu.astype(BF16))


def _s5(u_tb, st_re, st_im, params, nb, tc):
    rows, d_ssm = u_tb.shape
    n_state = st_re.shape[1]
    blk = tc * nb
    full = lambda a: pl.BlockSpec(a.shape, lambda i: (0,) * a.ndim)
    st_spec = pl.BlockSpec((nb, n_state), lambda i: (0, 0))
    kern = functools.partial(_s5_kernel, tc=tc, nb=nb, lane_chunk=512)
    return pl.pallas_call(
        kern,
        grid=(rows // blk,),
        in_specs=[pl.BlockSpec((blk, d_ssm), lambda i: (i, 0)), st_spec, st_spec] + [full(p) for p in params],
        out_specs=[pl.BlockSpec((blk, d_ssm), lambda i: (i, 0)), st_spec, st_spec],
        out_shape=[jax.ShapeDtypeStruct((rows, d_ssm), BF16),
                   jax.ShapeDtypeStruct((nb, n_state), F32), jax.ShapeDtypeStruct((nb, n_state), F32)],
        scratch_shapes=[pltpu.VMEM((blk, n_state), F32), pltpu.VMEM((blk, n_state), F32)],
        compiler_params=_cparams(("arbitrary",)),
        name="s5",
    )(u_tb, st_re, st_im, *params)


PART_ROWS = 32


def _dsa_kernel(qi_ref, wit_ref, q_ref, ki_ref, k_ref, vt_ref, o_ref, key_s, bias_s, lg_s,
                *, qb0, tq, kt, sub, topk, seq_bits):
    qb = pl.program_id(1) + qb0
    nkt = ((qb * tq + tq + sub * kt - 1) // (sub * kt)) * sub
    q_pos = qb * tq + lax.broadcasted_iota(I32, (1, tq), 1)
    k_eff = jnp.minimum(topk, q_pos + 1).astype(F32)

    qi = qi_ref[0]
    wit = wit_ref[0]
    qipair = [jnp.concatenate([qi[:, (2 * hp) * IDX_DIM:(2 * hp + 1) * IDX_DIM],
                               qi[:, (2 * hp + 1) * IDX_DIM:(2 * hp + 2) * IDX_DIM]], axis=0)
              for hp in range(IDX_HEADS // 2)]

    def key_pos(t):
        return t * kt + lax.broadcasted_iota(I32, (kt, tq), 0)

    def score_tile(t2, _):
        for hf in range(sub):
            r0 = pl.multiple_of((t2 * sub + hf) * kt, kt)
            ki_t = ki_ref[0, pl.ds(r0, kt), :]
            for hp in range(IDX_HEADS // 2):
                lg_s[hf, hp] = lax.dot_general(ki_t, qipair[hp], (((1,), (1,)), ((), ())),
                                               preferred_element_type=F32)
        for hf in range(sub):
            t = t2 * sub + hf
            r0 = pl.multiple_of(t * kt, kt)
            sc = jnp.zeros((kt, tq), F32)
            for hp in range(IDX_HEADS // 2):
                rel = lg_s[hf, hp]
                sc = sc + jnp.maximum(rel[:, :tq], 0.0) * wit[2 * hp:2 * hp + 1, :]
                sc = sc + jnp.maximum(rel[:, tq:], 0.0) * wit[2 * hp + 1:2 * hp + 2, :]
            bits = lax.bitcast_convert_type(sc, I32)
            key = jnp.where(bits < 0, bits ^ jnp.int32(0x7FFFFFFF), bits)
            key = jnp.where(key_pos(t) <= q_pos, key, jnp.int32(INT_MIN))
            key_s[pl.ds(r0, kt), :] = key
        return 0

    lax.fori_loop(0, nkt // sub, score_tile, 0)

    def count(pred_fn):
        def body(t, acc):
            r0 = pl.multiple_of(t * kt, kt)
            m = pred_fn(key_s[pl.ds(r0, kt), :], t)
            ones = jnp.where(m, 1.0, 0.0).reshape(kt // PART_ROWS, PART_ROWS, tq)
            return acc + jnp.sum(ones, axis=0)
        acc = lax.fori_loop(0, nkt, body, jnp.zeros((PART_ROWS, tq), F32))
        return jnp.sum(acc, axis=0, keepdims=True)

    def bit_step(i, u):
        bit = jnp.left_shift(jnp.int32(1), 31 - i)
        cand_u = u | bit
        cand_s = cand_u ^ jnp.int32(INT_MIN)
        cnt = count(lambda kk, t: kk >= cand_s)
        return jnp.where(cnt >= k_eff, cand_u, u)

    u_thr = lax.fori_loop(0, 32, bit_step, jnp.zeros((1, tq), I32))
    thr = u_thr ^ jnp.int32(INT_MIN)

    cnt_ge = count(lambda kk, t: kk >= thr)
    cnt_gt = count(lambda kk, t: kk > thr)
    need_eq = k_eff - cnt_gt
    has_tie = jnp.max(cnt_ge - k_eff) > 0.0

    def tie_cut():
        def pos_step(i, c):
            bit = jnp.left_shift(jnp.int32(1), seq_bits - 1 - i)
            cand = c | bit
            cnt = count(lambda kk, t: (kk == thr) & (key_pos(t) < cand))
            return jnp.where(cnt < need_eq, cand, c)
        return lax.fori_loop(0, seq_bits, pos_step, jnp.zeros((1, tq), I32))

    cut = lax.cond(has_tie, tie_cut, lambda: jnp.full((1, tq), 2 ** seq_bits, I32))

    def bias_tile(t, _):
        r0 = pl.multiple_of(t * kt, kt)
        key = key_s[pl.ds(r0, kt), :]
        sel = (key > thr) | ((key == thr) & (key_pos(t) <= cut))
        bias_s[pl.ds(r0, kt), :] = jnp.where(sel, 0.0, NEG_BIG)
        return 0

    lax.fori_loop(0, nkt, bias_tile, 0)

    q = q_ref[0]
    grp = ATTN_HEADS // ATTN_KV_HEADS
    pairs_per_kv = grp // 2
    n_unit = ATTN_KV_HEADS * pairs_per_kv
    wq = 2 * tq
    qpair = [jnp.concatenate([q[:, (2 * u) * HEAD_DIM:(2 * u + 1) * HEAD_DIM],
                              q[:, (2 * u + 1) * HEAD_DIM:(2 * u + 2) * HEAD_DIM]], axis=0)
             for u in range(n_unit)]

    def col_reduce(x, op):
        part = op(x.reshape(kt // PART_ROWS, PART_ROWS, wq), axis=0)
        return op(part, axis=0, keepdims=True)

    def attn_tile(t, carry):
        ms, ls, accs = list(carry[0]), list(carry[1]), list(carry[2])
        for hf in range(sub):
            r0 = pl.multiple_of((t * sub + hf) * kt, kt)
            bias = bias_s[pl.ds(r0, kt), :]
            bias2 = jnp.concatenate([bias, bias], axis=1)
            for u in range(n_unit):
                k_t = k_ref[0, u // pairs_per_kv, pl.ds(r0, kt), :]
                lg_s[hf, u] = lax.dot_general(k_t, qpair[u], (((1,), (1,)), ((), ())),
                                              preferred_element_type=F32) + bias2
        for hf in range(sub):
            for u in range(n_unit):
                lg = lg_s[hf, u]
                m_new = jnp.maximum(ms[u], col_reduce(lg, jnp.max))
                p = jnp.exp2(lg - m_new)
                alpha = jnp.exp2(ms[u] - m_new)
                v_t = vt_ref[0, u // pairs_per_kv, t * sub + hf]
                ls[u] = alpha * ls[u] + col_reduce(p, jnp.sum)
                accs[u] = alpha * accs[u] + jnp.dot(v_t, p.astype(BF16), preferred_element_type=F32)
                ms[u] = m_new
        return tuple(ms), tuple(ls), tuple(accs)

    init = (tuple(jnp.full((1, wq), NEG_BIG, F32) for _ in range(n_unit)),
            tuple(jnp.zeros((1, wq), F32) for _ in range(n_unit)),
            tuple(jnp.zeros((HEAD_DIM, wq), F32) for _ in range(n_unit)))
    _, ls, accs = lax.fori_loop(0, nkt // sub, attn_tile, init)
    for n in range(ATTN_KV_HEADS):
        o_ref[0, 0, n] = jnp.concatenate([accs[n * pairs_per_kv + pg] / ls[n * pairs_per_kv + pg]
                                          for pg in range(pairs_per_kv)], axis=1).astype(BF16)


def _dsa(q, qi, wit, ki, k4, vt, s0, tq, kt):
    bsz, sc, _ = q.shape
    seq = ki.shape[1]
    topk = min(TOPK_MAX, seq // 4)
    nqb = sc // tq
    grp = ATTN_HEADS // ATTN_KV_HEADS
    seq_bits = int(math.log2(seq))
    assert 2 ** seq_bits == seq
    sub = 2 if seq % (2 * kt) == 0 else 1
    n_unit = ATTN_HEADS // 2
    assert IDX_HEADS // 2 <= n_unit
    kern =functools.partial(_dsa_kernel, qb0=s0 // tq, tq=tq, kt=kt, sub=sub, topk=topk, seq_bits=seq_bits)
    o_t = pl.pallas_call(
        kern,
        grid=(bsz, nqb),
        in_specs=[pl.BlockSpec((1, tq, IDX_HEADS * IDX_DIM), lambda b, j: (b, j, 0)),
                  pl.BlockSpec((1, IDX_HEADS, tq), lambda b, j: (b, 0, j)),
                  pl.BlockSpec((1, tq, ATTN_HEADS * HEAD_DIM), lambda b, j: (b, j, 0)),
                  pl.BlockSpec((1, seq, IDX_DIM), lambda b, j: (b, 0, 0)),
                  pl.BlockSpec((1, ATTN_KV_HEADS, seq, HEAD_DIM), lambda b, j: (b, 0, 0, 0)),
                  pl.BlockSpec((1, ATTN_KV_HEADS, seq // kt, HEAD_DIM, kt), lambda b, j: (b, 0, 0, 0, 0))],
        out_specs=pl.BlockSpec((1, 1, ATTN_KV_HEADS, HEAD_DIM, grp * tq), lambda b, j: (b, j, 0, 0, 0)),
        out_shape=jax.ShapeDtypeStruct((bsz, nqb, ATTN_KV_HEADS, HEAD_DIM, grp * tq), BF16),
        scratch_shapes=[pltpu.VMEM((seq, tq), I32), pltpu.VMEM((seq, tq), F32),
                        pltpu.VMEM((sub, n_unit, kt, 2 * tq), F32)],
        compiler_params=_cparams(("arbitrary", "arbitrary")),
        name="dsa",
    )(qi, wit, q, ki, k4, vt)
    o = o_t.reshape(bsz, nqb, ATTN_KV_HEADS, HEAD_DIM, grp, tq).transpose(0, 1, 5, 2, 4, 3)
    return o.reshape(bsz, sc, ATTN_HEADS * HEAD_DIM)


def _merge_kernel(x_ref, ys_ref, ya_ref, gs_ref, ga_ref, wsu_ref, wau_ref, wout_ref, g2_ref, wq_ref,
                  h_ref, hn_ref, qp_ref):
    ms = jnp.dot(ys_ref[0], wsu_ref[...], preferred_element_type=F32)
    ma = jnp.dot(ya_ref[0], wau_ref[...], preferred_element_type=F32)
    merged = gs_ref[0].astype(F32) * ms + ga_ref[0].astype(F32) * ma
    h = x_ref[0] + jnp.dot(merged.astype(BF16), wout_ref[...], preferred_element_type=F32)
    h_ref[0] = h
    hb = _rms(h, g2_ref[...]).astype(BF16)
    bits = lax.bitcast_convert_type(hb.astype(F32), I32)
    half = bits.shape[1] // 2
    words = (bits[:, half:] & jnp.int32(-65536)) | lax.shift_right_logical(bits[:, :half], 16)
    n_seg = half // LANES
    tm = words.shape[0]
    for p in range(n_seg):
        hn_ref[0, pl.ds(p, tm, stride=n_seg), :] = words[:, p * LANES:(p + 1) * LANES]
    qp_ref[0] = jnp.dot(hb, wq_ref[...], preferred_element_type=F32).astype(BF16)


def _merge(x, ys, ya, gs, ga, wsu, wau, wo, norm2_g, wq, s0, tm):
    bsz, sc, _ = ya.shape
    d = x.shape[2]
    i0 = s0 // tm
    loc = lambda a: pl.BlockSpec((1, tm, a.shape[2]), lambda b, i: (b, i, 0))
    full = lambda a: pl.BlockSpec(a.shape, lambda b, i: (0,) * a.ndim)
    g2 = norm2_g.reshape(1, d)
    nq = wq.shape[1]
    n_seg = d // 2 // LANES
    out = lambda n: pl.BlockSpec((1, tm, n), lambda b, i: (b, i, 0))
    return pl.pallas_call(
        _merge_kernel,
        grid=(bsz, sc // tm),
        in_specs=[pl.BlockSpec((1, tm, d), lambda b, i: (b, i + i0, 0)), loc(ys), loc(ya), loc(gs), loc(ga),
                  full(wsu), full(wau), full(wo), full(g2), full(wq)],
        out_specs=[out(d), pl.BlockSpec((1, tm * n_seg, LANES), lambda b, i: (b, i, 0)), out(nq)],
        out_shape=[jax.ShapeDtypeStruct((bsz, sc, d), F32), jax.ShapeDtypeStruct((bsz, sc * n_seg, LANES), I32),
                   jax.ShapeDtypeStruct((bsz, sc, nq), BF16)],
        compiler_params=_cparams(("arbitrary", "arbitrary")),
        name="merge",
    )(x, ys, ya, gs, ga, wsu, wau, wo, g2, wq)


def _cand_layout():
    blocks = []
    blocks.append((0, 16, 16))
    for i in range(1, 8):
        blocks.append((i, 8, PEER_TOPK // (i + 1)))
    blocks.append((None, 8, 8))
    return blocks


def _top_rows(s, order, payload, k):
    big = jnp.float32(3e38)
    vals, pays = [], []
    for _ in range(k):
        m = jnp.max(s, axis=0, keepdims=True)
        o = jnp.min(jnp.where(s == m, order, big), axis=0, keepdims=True)
        hit = order == o
        pays.append(o if payload is order else jnp.min(jnp.where(hit, payload, big), axis=0, keepdims=True))
        vals.append(m)
        s = jnp.where(hit, -jnp.inf, s)
    return jnp.concatenate(vals, axis=0), jnp.concatenate(pays, axis=0)


def _route_kernel(qp_ref, k1_ref, k2_ref, e_ref, g_ref, *, tt):
    qp = qp_ref[...]
    kd = PEER_KEY_DIM
    rows_k = lax.broadcasted_iota(I32, (PEER_KEYS, tt), 0).astype(F32)
    for h in range(PEER_HEADS):
        q1 = qp[:, (2 * h) * kd:(2 * h + 1) * kd]
        q2 = qp[:, (2 * h + 1) * kd:(2 * h + 2) * kd]
        s1 = lax.dot_general(k1_ref[h], q1, (((1,), (1,)), ((), ())), preferred_element_type=F32)
        s2 = lax.dot_general(k2_ref[h], q2, (((1,), (1,)), ((), ())), preferred_element_type=F32)
        v1, i1 = _top_rows(s1, rows_k, rows_k, PEER_TOPK)
        v2, i2 = _top_rows(s2, rows_k, rows_k, PEER_TOPK)
        cs, ce, co = [], [], []
        for i, rows, valid in _cand_layout():
            r = lax.broadcasted_iota(I32, (rows, tt), 0).astype(F32)
            if i is None:
                val = v1[8:16] + v2[0:1]
                eid = i1[8:16] * PEER_KEYS + i2[0:1]
                flat = (r + 8.0) * PEER_TOPK
            else:
                val = v1[i:i + 1] + v2[0:rows]
                eid = i1[i:i + 1] * PEER_KEYS + i2[0:rows]
                flat = r + float(i * PEER_TOPK)
                if valid < rows:
                    val = jnp.where(r < float(valid), val, -jnp.inf)
            cs.append(val)
            ce.append(eid)
            co.append(flat)
        cand = jnp.concatenate(cs, axis=0)
        top_s, top_e = _top_rows(cand, jnp.concatenate(co, axis=0), jnp.concatenate(ce, axis=0), PEER_TOPK)
        p = jnp.exp(top_s - top_s[0:1])
        gates = p / jnp.sum(p, axis=0, keepdims=True)
        e_ref[h * PEER_TOPK:(h + 1) * PEER_TOPK, :] = top_e.astype(I32)
        g_ref[h * PEER_TOPK:(h + 1) * PEER_TOPK, :] = gates


def _route(qp, k1, k2, tt):
    n, nq = qp.shape
    n_sel = PEER_HEADS * PEER_TOPK
    full = lambda a: pl.BlockSpec(a.shape, lambda i: (0,) * a.ndim)
    return pl.pallas_call(
        functools.partial(_route_kernel, tt=tt),
        grid=(n // tt,),
        in_specs=[pl.BlockSpec((tt, nq), lambda i: (i, 0)), full(k1), full(k2)],
        out_specs=[pl.BlockSpec((n_sel, tt), lambda i: (0, i)), pl.BlockSpec((n_sel, tt), lambda i: (0, i))],
        out_shape=[jax.ShapeDtypeStruct((n_sel, n), I32), jax.ShapeDtypeStruct((n_sel, n), F32)],
        compiler_params=_cparams(("arbitrary",)),
        name="route",
    )(qp, k1, k2)


def _final_kernel(h_ref, p_ref, g_ref, o_ref):
    tm, d = h_ref.shape
    n_seg = d // LANES
    p = jnp.concatenate([p_ref[pl.ds(k, tm, stride=n_seg), :] for k in range(n_seg)], axis=1)
    o_ref[...] = _rms(h_ref[...] + p, g_ref[...])


def _final(h, p, g, tm):
    n, d = h.shape
    row = pl.BlockSpec((tm, d), lambda i: (i, 0))
    return pl.pallas_call(
        _final_kernel,
        grid=(n // tm,),
        in_specs=[row, pl.BlockSpec((tm * (d // LANES), LANES), lambda i: (i, 0)),
                  pl.BlockSpec((1, d), lambda i: (0, 0))],
        out_specs=row,
        out_shape=jax.ShapeDtypeStruct((n, d), F32),
        compiler_params=_cparams(("arbitrary",)),
        name="final",
    )(h, p, g.reshape(1, d))


SC_CORES_V7X = 2
SC_SUBCORES_V7X = 16
SC_LANES_V7X = 16
PEER_TOK_BATCH = 16
PEER_ROW_CHUNK = 32
PEER_RING = 4


def _pack_bf16_pairs(t):
    half = t.shape[1] // 2
    tb = t.astype(BF16)
    lo = lax.bitcast_convert_type(tb[:, :half], jnp.uint16).astype(jnp.uint32)
    hi = lax.bitcast_convert_type(tb[:, half:], jnp.uint16).astype(jnp.uint32)
    return lax.bitcast_convert_type(lo | (hi << 16), I32)


def _unpack_pair(w):
    lo = lax.bitcast_convert_type(jnp.left_shift(w, 16), F32)
    hi = lax.bitcast_convert_type(w & jnp.int32(-65536), F32)
    return lo, hi


def _peer_sc_body(hn_hbm, e_hbm, g_hbm, u_hbm, v_hbm, out_hbm,
                  idx_v, gate_v, x_v, out_v, rows, p_v, act_v, sem, *, tpw, d, n_sel):
    nl = SC_LANES_V7X
    tb = PEER_TOK_BATCH
    rc = PEER_ROW_CHUNK
    n_chunk = n_sel // rc
    jobs_per_tok = 2 * n_chunk
    half = d // 2
    n_lane_blk = half // nl
    xs = half // LANES
    os_ = d // LANES
    wid =lax.axis_index("s") * SC_CORES_V7X + lax.axis_index("c")
    base = wid * tpw
    lane = lax.iota(I32, nl)
    zero = jnp.zeros((nl,), F32)
    c_gelu = 2.0 * math.sqrt(2.0 / math.pi)

    def gather_copy(tab_hbm, job):
        tok = job // jobs_per_tok
        c = (job % jobs_per_tok) % n_chunk
        b = job % PEER_RING
        return pltpu.make_async_copy(tab_hbm.at[idx_v.at[tok, pl.ds(c * rc, rc)]], rows.at[b], sem.at[b])

    def start(job):
        j = job % jobs_per_tok

        @pl.when(j < n_chunk)
        def _():
            gather_copy(u_hbm, job).start()

        @pl.when(j >= n_chunk)
        def _():
            gather_copy(v_hbm, job).start()

    def compute_u(tok, c, b):
        def rg_body(rg, _):
            r0 = rg * 8

            def jbody(j2, accs):
                off0 = j2 * (2 * nl)
                off1 = off0 + nl
                xrow = tok * xs + off0 // LANES
                xl = off0 % LANES
                x0 = plsc.bitcast(x_v[xrow, pl.ds(xl, nl)], BF16)
                x1 = plsc.bitcast(x_v[xrow, pl.ds(xl + nl, nl)], BF16)
                new = []
                for r in range(8):
                    w0 = plsc.bitcast(rows[b, r0 + r, pl.ds(off0, nl)], BF16)
                    w1 = plsc.bitcast(rows[b, r0 + r, pl.ds(off1, nl)], BF16)
                    lo, hi = _unpack_pair(plsc.bitcast(w0 * x0 + w1 * x1, I32))
                    new.append(accs[r] + (lo + hi))
                return tuple(new)

            accs = lax.fori_loop(0, n_lane_blk // 2, jbody, (zero,) * 8)
            for r in range(8):
                p_v[c * rc + r0 + r, :] = accs[r]
            return 0

        lax.fori_loop(0, rc // 8, rg_body, 0)

    def finish_act(tok):
        def eg_body(eg, _):
            e0 = eg * nl
            ridx = e0 + lane
            s = zero
            for l in range(nl):
                s = s + plsc.load_gather(p_v, [ridx, jnp.full((nl,), l, I32)])
            inner = c_gelu * (s + 0.044715 * (s * s * s))
            gl = s / (1.0 + jnp.exp(-inner))
            a = gl * gate_v[tok, pl.ds(e0, nl)]
            bits = lax.bitcast_convert_type(a, I32)
            rnd = bits + jnp.int32(0x7FFF) + (lax.shift_right_logical(bits, 16) & 1)
            hi16 = rnd & jnp.int32(-65536)
            act_v[pl.ds(e0, nl)] = hi16 | lax.shift_right_logical(hi16, 16)
            return 0

        lax.fori_loop(0, n_sel // nl, eg_body, 0)

        def zbody(j, _):
            off = j * nl
            out_v[tok * os_ + off // LANES, pl.ds(off % LANES, nl)] = zero
            return 0

        lax.fori_loop(0, d // nl, zbody, 0, unroll=4)

    def compute_v(tok, c, b):
        def rg_body(rg, _):
            r0 = rg * nl
            splat = [plsc.bitcast(plsc.load_gather(act_v, [jnp.full((nl,), 0, I32) + (c * rc + r0 + r)]), BF16)
                     for r in range(nl)]

            def tree(parts):
                while len(parts) > 1:
                    parts = [parts[i] + parts[i + 1] for i in range(0, len(parts), 2)]
                return parts[0]

            @plsc.parallel_loop(0, n_lane_blk, unroll=2)
            def _(j):
                off = j * nl
                los, his = [], []
                for r in range(0, nl, 2):
                    w0 = plsc.bitcast(rows[b, r0 + r, pl.ds(off, nl)], BF16)
                    w1 = plsc.bitcast(rows[b, r0 + r + 1, pl.ds(off, nl)], BF16)
                    lo, hi = _unpack_pair(plsc.bitcast(w0 * splat[r] + w1 * splat[r + 1], I32))
                    los.append(lo)
                    his.append(hi)
                orow = tok * os_ + off // LANES
                ol = off % LANES
                out_v[orow, pl.ds(ol, nl)] = out_v[orow, pl.ds(ol, nl)] + tree(los)
                out_v[orow + xs, pl.ds(ol, nl)] = out_v[orow + xs, pl.ds(ol, nl)] + tree(his)

            return 0

        lax.fori_loop(0, rc // nl, rg_body, 0)

    def batch_body(bi, _):
        t0 = base + bi * tb
        pltpu.sync_copy(e_hbm.at[pl.ds(t0, tb)], idx_v)
        pltpu.sync_copy(g_hbm.at[pl.ds(t0, tb)], gate_v)
        pltpu.sync_copy(hn_hbm.at[pl.ds(t0 * xs, tb * xs)], x_v)
        for pre in range(PEER_RING - 1):
            start(pre)

        def job_body(job, _):
            @pl.when(job + (PEER_RING - 1) < tb * jobs_per_tok)
            def _():
                start(job + (PEER_RING - 1))

            j = job % jobs_per_tok
            gather_copy(u_hbm, job).wait()
            tok = job // jobs_per_tok
            b = job % PEER_RING

            @pl.when(j < n_chunk)
            def _():
                compute_u(tok, j, b)

            @pl.when(j == n_chunk - 1)
            def _():
                finish_act(tok)

            @pl.when(j >= n_chunk)
            def _():
                compute_v(tok, j - n_chunk, b)

            return 0

        lax.fori_loop(0, tb * jobs_per_tok, job_body, 0)
        pltpu.sync_copy(out_v, out_hbm.at[pl.ds(t0 * os_, tb * os_)])
        return 0

    lax.fori_loop(0, tpw // tb, batch_body, 0)


def _peer_sc(x_pk, experts, gates, u_tab, v_tab):
    d = 2 * u_tab.shape[1]
    n = x_pk.shape[0] * LANES // (d // 2)
    n_sel = experts.shape[1]
    nw = SC_CORES_V7X * SC_SUBCORES_V7X
    tpw = n // nw
    mesh = plsc.VectorSubcoreMesh(core_axis_name="c", subcore_axis_name="s",
                                  num_cores=SC_CORES_V7X, num_subcores=SC_SUBCORES_V7X)
    body = functools.partial(_peer_sc_body, tpw=tpw, d=d, n_sel=n_sel)
    call = pl.kernel(
        body,
        out_type=jax.ShapeDtypeStruct((n * d // LANES, LANES), F32),
        mesh=mesh,
        scratch_types=[pltpu.VMEM((PEER_TOK_BATCH, n_sel), I32),
                       pltpu.VMEM((PEER_TOK_BATCH, n_sel), F32),
                       pltpu.VMEM((PEER_TOK_BATCH * d // 2 // LANES, LANES), I32),
                       pltpu.VMEM((PEER_TOK_BATCH * d // LANES, LANES), F32),
                       pltpu.VMEM((PEER_RING, PEER_ROW_CHUNK, d // 2), I32),
                       pltpu.VMEM((n_sel, SC_LANES_V7X), F32),
                       pltpu.VMEM((n_sel,), I32),
                       pltpu.SemaphoreType.DMA((PEER_RING,))],
        compiler_params=pltpu.CompilerParams(needs_layout_passes=False, use_tc_tiling_on_sc=False),
        name="peer_sc",
    )
    return call(x_pk, experts, gates, u_tab, v_tab)


def kernel(x, norm1_g, w_in, a_re, a_im, log_dt, b_re, b_im, c_re, c_im, d_skip, w_glu, w_ssm_up, w_attn_up,
           w_out, norm2_g, peer_wq, peer_k1, peer_k2, peer_u, peer_v, norm_f_g):
    bsz, seq, d = x.shape
    depth = norm1_g.shape[0]
    chunks = _time_chunks(seq)
    h = x
    for layer in range(depth):
        last = layer + 1 == depth
        kv_w, main_w = _in_weights(w_in[layer], seq)
        s5p = _s5_params(a_re[layer], a_im[layer], log_dt[layer], b_re[layer], b_im[layer], c_re[layer],
                         c_im[layer], d_skip[layer], w_glu[layer], nb=bsz)
        n_state = s5p[4].shape[1]
        wsu = w_ssm_up[layer].astype(BF16)
        wau = w_attn_up[layer].astype(BF16)
        wo = w_out[layer].astype(BF16)
        wq = peer_wq[layer].astype(BF16)
        k1 = peer_k1[layer].astype(BF16)
        k2 = peer_k2[layer].astype(BF16)
        u_pk = _pack_bf16_pairs(peer_u[layer])
        v_pk = _pack_bf16_pairs(peer_v[layer])
        k4, vt, ki = _kvproj(h, norm1_g[layer], kv_w, tm=min(ROW_TILE, seq), kt=DSA_KT)
        st_re = jnp.zeros((bsz, n_state), F32)
        st_im = jnp.zeros((bsz, n_state), F32)
        outs = []
        routed = ki
        peer_outs = []
        for c, (s0, sc) in enumerate(chunks):
            tm = math.gcd(math.gcd(s0, sc), ROW_TILE)
            after =(routed, peer_outs[c - SC_LAG] if c >= SC_LAG else ki)
            u, q, qi, wit, gs, ga = _inproj(h, norm1_g[layer], main_w, after, s0=s0, sc=sc, tm=tm)
            d_ssm = u.shape[-1]
            u_tb = u.transpose(1, 0, 2).reshape(sc * bsz, d_ssm)
            y_tb, st_re, st_im = _s5(u_tb, st_re, st_im, s5p, nb=bsz, tc=64)
            ys = y_tb.reshape(sc, bsz, d_ssm).transpose(1, 0, 2)
            ya = _dsa(q, qi, wit, ki, k4, vt, s0=s0, tq=DSA_TQ, kt=DSA_KT)
            hm, x_pk, qp = _merge(h, ys, ya, gs, ga, wsu, wau, wo, norm2_g[layer], wq, s0=s0, tm=tm)
            nt = bsz * sc
            e_t, g_t = _route(qp.reshape(nt, -1), k1, k2, tt=256)
            routed = e_t
            po = _peer_sc(x_pk.reshape(-1, LANES), e_t.T, g_t.T, u_pk, v_pk)
            peer_outs.append(po)
            hm2 = hm.reshape(nt, d)
            o = _final(hm2, po, norm_f_g, tm=tm) if last else hm2 + po.reshape(nt, d)
            outs.append(o.reshape(bsz, sc, d))
        h = jnp.concatenate(outs, axis=1)
    return h
```

```python
import functools
import math

import numpy as np
import jax
import jax.numpy as jnp
from jax import lax
from jax.experimental import pallas as pl
from jax.experimental.pallas import tpu as pltpu
from jax.experimental.pallas import tpu_sc as plsc

F32 = jnp.float32
BF16 = jnp.bfloat16
I32 = jnp.int32

SSM_GROUP = 16
SSM_STATE = 64
ATTN_HEADS = 8
ATTN_KV_HEADS = 2
HEAD_DIM = 64
IDX_HEADS = 8
IDX_DIM = 32
TOPK_MAX = 256
ROPE_THETA = 10000.0
NEG_BIG = -1e30
PEER_HEADS = 8
PEER_KEYS = 128
PEER_KEY_DIM = 128
PEER_TOPK = 16
NORM_EPS = 1e-6

TIME_SPLIT_32NDS = (1, 3, 4, 4, 4, 4, 4, 3, 2, 2, 1)
SC_LAG = 3
ROW_TILE = 512
DSA_TQ = 128
DSA_KT = 256
LANES = 128
INT_MIN = -(2 ** 31)
VMEM_LIMIT = 56 * 1024 * 1024


def _time_chunks(seq):
    unit = seq // 32
    if seq % 32 == 0 and unit % DSA_TQ == 0:
        sizes = [f * unit for f in TIME_SPLIT_32NDS]
    else:
        step = min(ROW_TILE, seq)
        sizes = [step] * (seq // step)
    assert sum(sizes) == seq
    starts = np.cumsum([0] + sizes[:-1]).tolist()
    return list(zip(starts, sizes))


def _cparams(sem):
    return pltpu.CompilerParams(dimension_semantics=sem, vmem_limit_bytes=VMEM_LIMIT)


def _gelu_tanh(x):
    return 0.5 * x * (1.0 + jnp.tanh(math.sqrt(2.0 / math.pi) * (x + 0.044715 * (x * x * x))))


def _sigmoid(x):
    return 1.0 / (1.0 + jnp.exp(-x))


def _rms(x, g):
    return x * lax.rsqrt(jnp.mean(x * x, axis=-1, keepdims=True) + NORM_EPS) * g


def _rot_cols(w, hd):
    d, n = w.shape
    w3 = w.reshape(d, n // hd, hd)
    half = hd // 2
    return jnp.concatenate([-w3[..., half:], w3[..., :half]], axis=-1).reshape(d, n)


def _rope_full(seq, hd, heads):
    pos = jnp.arange(seq, dtype=F32)
    inv = ROPE_THETA ** (-jnp.arange(0, hd, 2, dtype=F32) / hd)
    ang = pos[:, None] * inv[None, :]
    c = jnp.concatenate([jnp.cos(ang), jnp.cos(ang)], axis=-1)
    s = jnp.concatenate([jnp.sin(ang), jnp.sin(ang)], axis=-1)
    return jnp.tile(c, (1, heads)), jnp.tile(s, (1, heads))


def _in_weights(w_in, seq):
    d = w_in.shape[0]
    d_ssm = d // 2
    d_q = ATTN_HEADS * HEAD_DIM
    d_kv = ATTN_KV_HEADS * HEAD_DIM
    d_qi = IDX_HEADS * IDX_DIM
    splits = (d_ssm, d_q, d_kv, d_kv, d_qi, IDX_DIM, IDX_HEADS, d, d)
    offs = np.cumsum(splits)[:-1].tolist()
    wu, wq, wk, wv, wqi, wki, wwi, wgs, wga = jnp.split(w_in, offs, axis=1)
    pad = jnp.zeros((d, 128 - IDX_DIM), F32)
    cq, sq = _rope_full(seq, HEAD_DIM, ATTN_HEADS)
    ck, sk = _rope_full(seq, HEAD_DIM, ATTN_KV_HEADS)
    cqi, sqi = _rope_full(seq, IDX_DIM, IDX_HEADS)
    cki, ski = _rope_full(seq, IDX_DIM, 1)
    tpad = jnp.zeros((seq, 128 - IDX_DIM), F32)
    kv = dict(
        w=jnp.concatenate([wk, wki, pad], axis=1).astype(BF16),
        wvt=wv.T.astype(BF16),
        wr=jnp.concatenate([_rot_cols(wk, HEAD_DIM), _rot_cols(wki, IDX_DIM), pad], axis=1).astype(BF16),
        cs=jnp.concatenate([ck, cki, tpad], axis=1), sn=jnp.concatenate([sk, ski, tpad], axis=1))
    main = dict(
        w=jnp.concatenate([wu, wq, wqi, wgs, wga], axis=1).astype(BF16),
        wr=jnp.concatenate([_rot_cols(wq, HEAD_DIM), _rot_cols(wqi, IDX_DIM)], axis=1).astype(BF16),
        cs=jnp.concatenate([cq, cqi], axis=1), sn=jnp.concatenate([sq, sqi], axis=1),
        wwit=wwi.T.astype(BF16))
    return kv, main


def _kvproj_kernel(x_ref, g_ref, w_ref, wr_ref, wvt_ref, cs_ref, sn_ref, k_ref, vt_ref, ki_ref, *, d_kv, kt):
    xb = _rms(x_ref[0], g_ref[...]).astype(BF16)

    def mm(ref, lo, n):
        return jnp.dot(xb, ref[:, lo:lo + n], preferred_element_type=F32)

    k = mm(w_ref, 0, d_kv) * cs_ref[:, :d_kv] + mm(wr_ref, 0, d_kv) * sn_ref[:, :d_kv]
    for n in range(ATTN_KV_HEADS):
        k_ref[0, n] = k[:, n * HEAD_DIM:(n + 1) * HEAD_DIM].astype(BF16)
    vt = lax.dot_general(wvt_ref[...], xb, (((1,), (1,)), ((), ())), preferred_element_type=F32)
    for n in range(ATTN_KV_HEADS):
        for j in range(vt.shape[1] // kt):
            vt_ref[0, n, j] = vt[n * HEAD_DIM:(n + 1) * HEAD_DIM, j * kt:(j + 1) * kt].astype(BF16)
    kiw = (mm(w_ref, d_kv, 128) * cs_ref[:, d_kv:d_kv + 128]
           + mm(wr_ref, d_kv, 128) * sn_ref[:, d_kv:d_kv + 128])
    ki_ref[0] = kiw[:, :IDX_DIM].astype(BF16)


def _kvproj(x, norm_g, kv, tm, kt):
    bsz, seq, d = x.shape
    d_kv = ATTN_KV_HEADS * HEAD_DIM
    full = lambda a: pl.BlockSpec(a.shape, lambda s, b: (0,) * a.ndim)
    g = norm_g.reshape(1, d)
    ncs = kv["cs"].shape[1]
    return pl.pallas_call(
        functools.partial(_kvproj_kernel, d_kv=d_kv, kt=kt),
        grid=(seq // tm, bsz),
        in_specs=[pl.BlockSpec((1, tm, d), lambda s, b: (b, s, 0)), full(g), full(kv["w"]), full(kv["wr"]),
                  full(kv["wvt"]),
                  pl.BlockSpec((tm, ncs), lambda s, b: (s, 0)), pl.BlockSpec((tm, ncs), lambda s, b: (s, 0))],
        out_specs=[pl.BlockSpec((1, ATTN_KV_HEADS, tm, HEAD_DIM), lambda s, b: (b, 0, s, 0)),
                   pl.BlockSpec((1, ATTN_KV_HEADS, tm // kt, HEAD_DIM, kt), lambda s, b: (b, 0, s, 0, 0)),
                   pl.BlockSpec((1, tm, IDX_DIM), lambda s, b: (b, s, 0))],
        out_shape=[jax.ShapeDtypeStruct((bsz, ATTN_KV_HEADS, seq, HEAD_DIM), BF16),
                   jax.ShapeDtypeStruct((bsz, ATTN_KV_HEADS, seq // kt, HEAD_DIM, kt), BF16),
                   jax.ShapeDtypeStruct((bsz, seq, IDX_DIM), BF16)],
        compiler_params=_cparams(("arbitrary", "arbitrary")),
        name="kvproj",
    )(x, g, kv["w"], kv["wr"], kv["wvt"], kv["cs"], kv["sn"])


def _inproj_kernel(x_ref, g_ref, w_ref, wr_ref, cs_ref, sn_ref, wwit_ref, after_tc_ref, after_sc_ref,
                   u_ref, q_ref, qi_ref, wit_ref, gs_ref, ga_ref, *, d_ssm, d_q, d_qi, d_model, q_scale, wi_scale):
    del after_tc_ref, after_sc_ref
    xb = _rms(x_ref[0], g_ref[...]).astype(BF16)

    def mm(ref, lo, n):
        return jnp.dot(xb, ref[:, lo:lo + n], preferred_element_type=F32)

    o = 0
    u_ref[0] = mm(w_ref, o, d_ssm).astype(BF16)
    o += d_ssm
    q = mm(w_ref, o, d_q) * cs_ref[:, :d_q] + mm(wr_ref, 0, d_q) * sn_ref[:, :d_q]
    q_ref[0] = (q * q_scale).astype(BF16)
    o += d_q
    qi = mm(w_ref, o, d_qi) * cs_ref[:, d_q:d_q + d_qi] + mm(wr_ref, d_q, d_qi) * sn_ref[:, d_q:d_q + d_qi]
    qi_ref[0] = qi.astype(BF16)
    o += d_qi
    gs_ref[0] = _sigmoid(mm(w_ref, o, d_model)).astype(BF16)
    o += d_model
    ga_ref[0] = _sigmoid(mm(w_ref, o, d_model)).astype(BF16)
    wit_ref[0] = lax.dot_general(wwit_ref[...], xb, (((1,), (1,)), ((), ())),
                                 preferred_element_type=F32) * wi_scale


def _inproj(x, norm_g, main, after, s0, sc, tm):
    bsz, _, d = x.shape
    d_ssm = d // 2
    d_q = ATTN_HEADS * HEAD_DIM
    d_qi = IDX_HEADS * IDX_DIM
    i0 = s0 // tm
    kern = functools.partial(
        _inproj_kernel, d_ssm=d_ssm, d_q=d_q, d_qi=d_qi, d_model=d,
        q_scale=HEAD_DIM ** -0.5 * math.log2(math.e), wi_scale=(IDX_HEADS ** -0.5) * (IDX_DIM ** -0.5))
    tok = lambda n: pl.BlockSpec((1, tm, n), lambda s, b: (b, s, 0))
    full = lambda a: pl.BlockSpec(a.shape, lambda s, b: (0,) * a.ndim)
    g = norm_g.reshape(1, d)
    ncs = main["cs"].shape[1]
    outs = [(d_ssm, BF16), (d_q, BF16), (d_qi, BF16)]
    return pl.pallas_call(
        kern,
        grid=(sc // tm, bsz),
        in_specs=[pl.BlockSpec((1, tm, d), lambda s, b: (b, s + i0, 0)), full(g), full(main["w"]), full(main["wr"]),
                  pl.BlockSpec((tm, ncs), lambda s, b: (s + i0, 0)),
                  pl.BlockSpec((tm, ncs), lambda s, b: (s + i0, 0)), full(main["wwit"]),
                  pl.BlockSpec(memory_space=pl.ANY), pl.BlockSpec(memory_space=pl.ANY)],
        out_specs=[tok(n) for n, _ in outs] + [pl.BlockSpec((1, IDX_HEADS, tm), lambda s, b: (b, 0, s)),
                                                tok(d), tok(d)],
        out_shape=[jax.ShapeDtypeStruct((bsz, sc, n), dt) for n, dt in outs]
        + [jax.ShapeDtypeStruct((bsz, IDX_HEADS, sc), F32),
           jax.ShapeDtypeStruct((bsz, sc, d), BF16), jax.ShapeDtypeStruct((bsz, sc, d), BF16)],
        compiler_params=_cparams(("arbitrary", "arbitrary")),
        name="inproj",
    )(x, g, main["w"], main["wr"], main["cs"], main["sn"], main["wwit"], *after)


def _s5_kernel(u_ref, sre_in, sim_in, bre_ref, bim_ref, cre_ref, cim_ref, are_ref, aim_ref, dsk_ref, wglu_ref,
               y_ref, st_re, st_im, sre, sim, *, tc, nb, lane_chunk):
    @pl.when(pl.program_id(0) == 0)
    def _():
        st_re[...] = sre_in[...]
        st_im[...] = sim_in[...]

    u = u_ref[...]
    n_half = bre_ref.shape[0]
    hin = bre_ref.shape[1]
    hst = bre_ref.shape[2]
    for h in range(n_half):
        uh = u[:, h * hin:(h + 1) * hin]
        sre[:, h * hst:(h + 1) * hst] = jnp.dot(uh, bre_ref[h], preferred_element_type=F32)
        sim[:, h * hst:(h + 1) * hst] = jnp.dot(uh, bim_ref[h], preferred_element_type=F32)

    n_state = sre.shape[1]
    for c in range(n_state // lane_chunk):
        cols = slice(c * lane_chunk, (c + 1) * lane_chunk)
        ar = are_ref[:, cols]
        ai = aim_ref[:, cols]

        def step(t, carry, cols=cols, ar=ar, ai=ai):
            sr, si = carry
            r0 = pl.multiple_of(t * nb, nb)
            nr = ar * sr - ai * si + sre[pl.ds(r0, nb), cols]
            ni = ar * si + ai * sr + sim[pl.ds(r0, nb), cols]
            sre[pl.ds(r0, nb), cols] = nr
            sim[pl.ds(r0, nb), cols] = ni
            return nr, ni

        sr, si = lax.fori_loop(0, tc, step, (st_re[:, cols], st_im[:, cols]), unroll=4)
        st_re[:, cols] = sr
        st_im[:, cols] = si

    ys = []
    for h in range(n_half):
        srh = sre[:, h * hst:(h + 1) * hst].astype(BF16)
        sih = sim[:, h * hst:(h + 1) * hst].astype(BF16)
        ys.append(jnp.dot(srh, cre_ref[h], preferred_element_type=F32)
                  - jnp.dot(sih, cim_ref[h], preferred_element_type=F32))
    y = jnp.concatenate(ys, axis=-1) + dsk_ref[...] * u.astype(F32)
    y = _gelu_tanh(y)
    gate = jnp.dot(y.astype(BF16), wglu_ref[...], preferred_element_type=F32)
    y_ref[...] = (y * _sigmoid(gate)).astype(BF16)


def _s5_params(a_re, a_im, log_dt, b_re, b_im, c_re, c_im, d_skip, w_glu, nb):
    groups = a_re.shape[0]
    d_ssm = groups * SSM_GROUP
    n_state = groups * SSM_STATE
    lam = lax.complex(a_re, a_im)
    dt = jnp.exp(log_dt)[:, None]
    a_bar = jnp.exp(lam * dt)
    b_bar = ((a_bar - 1.0) / lam)[..., None] * lax.complex(b_re, b_im)
    gh = min(groups, 256 // SSM_GROUP)
    n_half = groups // gh
    eye = jnp.eye(gh, dtype=F32)

    def bmat(bb):
        b4 = bb.reshape(n_half, gh, SSM_STATE, SSM_GROUP)
        return jnp.einsum('hgpc,gk->hgckp', b4, eye).reshape(n_half, gh * SSM_GROUP, gh * SSM_STATE)

    def cmat(cc):
        c4 = cc.reshape(n_half, gh, SSM_GROUP, SSM_STATE)
        return jnp.einsum('hgcp,gk->hgpkc', c4, eye).reshape(n_half, gh * SSM_STATE, gh * SSM_GROUP)

    return (bmat(jnp.real(b_bar)).astype(BF16), bmat(jnp.imag(b_bar)).astype(BF16),
            cmat(c_re).astype(BF16), cmat(c_im).astype(BF16),
            jnp.broadcast_to(jnp.real(a_bar).reshape(1, n_state), (nb, n_state)),
            jnp.broadcast_to(jnp.imag(a_bar).reshape(1, n_state), (nb, n_state)),
            d_skip.reshape(1, d_ssm), w_glu.astype(BF16))


def _s5(u_tb, st_re, st_im, params, nb, tc):
    rows, d_ssm = u_tb.shape
    n_state = st_re.shape[1]
    blk = tc * nb
    full = lambda a: pl.BlockSpec(a.shape, lambda i: (0,) * a.ndim)
    st_spec = pl.BlockSpec((nb, n_state), lambda i: (0, 0))
    kern = functools.partial(_s5_kernel, tc=tc, nb=nb, lane_chunk=512)
    return pl.pallas_call(
        kern,
        grid=(rows // blk,),
        in_specs=[pl.BlockSpec((blk, d_ssm), lambda i: (i, 0)), st_spec, st_spec] + [full(p) for p in params],
        out_specs=[pl.BlockSpec((blk, d_ssm), lambda i: (i, 0)), st_spec, st_spec],
        out_shape=[jax.ShapeDtypeStruct((rows, d_ssm), BF16),
                   jax.ShapeDtypeStruct((nb, n_state), F32), jax.ShapeDtypeStruct((nb, n_state), F32)],
        scratch_shapes=[pltpu.VMEM((blk, n_state), F32), pltpu.VMEM((blk, n_state), F32)],
        compiler_params=_cparams(("arbitrary",)),
        name="s5",
    )(u_tb, st_re, st_im, *params)


PART_ROWS = 32


def _dsa_kernel(qi_ref, wit_ref, q_ref, ki_ref, k_ref, vt_ref, o_ref, key_s, bias_s, lg_s,
                *, qb0, tq, kt, sub, topk, seq_bits):
    qb = pl.program_id(1) + qb0
    nkt = ((qb * tq + tq + sub * kt - 1) // (sub * kt)) * sub
    q_pos = qb * tq + lax.broadcasted_iota(I32, (1, tq), 1)
    k_eff = jnp.minimum(topk, q_pos + 1).astype(F32)

    qi = qi_ref[0]
    wit = wit_ref[0]
    qipair = [jnp.concatenate([qi[:, (2 * hp) * IDX_DIM:(2 * hp + 1) * IDX_DIM],
                               qi[:, (2 * hp + 1) * IDX_DIM:(2 * hp + 2) * IDX_DIM]], axis=0)
              for hp in range(IDX_HEADS // 2)]

    def key_pos(t):
        return t * kt + lax.broadcasted_iota(I32, (kt, tq), 0)

    def score_tile(t2, _):
        for hf in range(sub):
            r0 = pl.multiple_of((t2 * sub + hf) * kt, kt)
            ki_t = ki_ref[0, pl.ds(r0, kt), :]
            for hp in range(IDX_HEADS // 2):
                lg_s[hf, hp] = lax.dot_general(ki_t, qipair[hp], (((1,), (1,)), ((), ())),
                                               preferred_element_type=F32)
        for hf in range(sub):
            t = t2 * sub + hf
            r0 = pl.multiple_of(t * kt, kt)
            sc = jnp.zeros((kt, tq), F32)
            for hp in range(IDX_HEADS // 2):
                rel = lg_s[hf, hp]
                sc = sc + jnp.maximum(rel[:, :tq], 0.0) * wit[2 * hp:2 * hp + 1, :]
                sc = sc + jnp.maximum(rel[:, tq:], 0.0) * wit[2 * hp + 1:2 * hp + 2, :]
            bits = lax.bitcast_convert_type(sc, I32)
            key = jnp.where(bits < 0, bits ^ jnp.int32(0x7FFFFFFF), bits)
            key = jnp.where(key_pos(t) <= q_pos, key, jnp.int32(INT_MIN))
            key_s[pl.ds(r0, kt), :] = key
        return 0

    lax.fori_loop(0, nkt // sub, score_tile, 0)

    def count(pred_fn):
        def body(t, acc):
            r0 = pl.multiple_of(t * kt, kt)
            m = pred_fn(key_s[pl.ds(r0, kt), :], t)
            ones = jnp.where(m, 1.0, 0.0).reshape(kt // PART_ROWS, PART_ROWS, tq)
            return acc + jnp.sum(ones, axis=0)
        acc = lax.fori_loop(0, nkt, body, jnp.zeros((PART_ROWS, tq), F32))
        return jnp.sum(acc, axis=0, keepdims=True)

    def bit_step(i, u):
        bit = jnp.left_shift(jnp.int32(1), 31 - i)
        cand_u = u | bit
        cand_s = cand_u ^ jnp.int32(INT_MIN)
        cnt = count(lambda kk, t: kk >= cand_s)
        return jnp.where(cnt >= k_eff, cand_u, u)

    u_thr = lax.fori_loop(0, 32, bit_step, jnp.zeros((1, tq), I32))
    thr = u_thr ^ jnp.int32(INT_MIN)

    cnt_ge = count(lambda kk, t: kk >= thr)
    cnt_gt = count(lambda kk, t: kk > thr)
    need_eq = k_eff - cnt_gt
    has_tie = jnp.max(cnt_ge - k_eff) > 0.0

    def tie_cut():
        def pos_step(i, c):
            bit = jnp.left_shift(jnp.int32(1), seq_bits - 1 - i)
            cand = c | bit
            cnt = count(lambda kk, t: (kk == thr) & (key_pos(t) < cand))
            return jnp.where(cnt < need_eq, cand, c)
        return lax.fori_loop(0, seq_bits, pos_step, jnp.zeros((1, tq), I32))

    cut = lax.cond(has_tie, tie_cut, lambda: jnp.full((1, tq), 2 ** seq_bits, I32))

    def bias_tile(t, _):
        r0 = pl.multiple_of(t * kt, kt)
        key = key_s[pl.ds(r0, kt), :]
        sel = (key > thr) | ((key == thr) & (key_pos(t) <= cut))
        bias_s[pl.ds(r0, kt), :] = jnp.where(sel, 0.0, NEG_BIG)
        return 0

    lax.fori_loop(0, nkt, bias_tile, 0)

    q = q_ref[0]
    grp = ATTN_HEADS // ATTN_KV_HEADS
    pairs_per_kv = grp // 2
    n_unit = ATTN_KV_HEADS * pairs_per_kv
    wq = 2 * tq
    qpair = [jnp.concatenate([q[:, (2 * u) * HEAD_DIM:(2 * u + 1) * HEAD_DIM],
                              q[:, (2 * u + 1) * HEAD_DIM:(2 * u + 2) * HEAD_DIM]], axis=0)
             for u in range(n_unit)]

    def col_reduce(x, op):
        part = op(x.reshape(kt // PART_ROWS, PART_ROWS, wq), axis=0)
        return op(part, axis=0, keepdims=True)

    def attn_tile(t, carry):
        ms, ls, accs = list(carry[0]), list(carry[1]), list(carry[2])
        for hf in range(sub):
            r0 = pl.multiple_of((t * sub + hf) * kt, kt)
            bias = bias_s[pl.ds(r0, kt), :]
            bias2 = jnp.concatenate([bias, bias], axis=1)
            for u in range(n_unit):
                k_t = k_ref[0, u // pairs_per_kv, pl.ds(r0, kt), :]
                lg_s[hf, u] = lax.dot_general(k_t, qpair[u], (((1,), (1,)), ((), ())),
                                              preferred_element_type=F32) + bias2
        for hf in range(sub):
            for u in range(n_unit):
                lg = lg_s[hf, u]
                m_new = jnp.maximum(ms[u], col_reduce(lg, jnp.max))
                p = jnp.exp2(lg - m_new)
                alpha = jnp.exp2(ms[u] - m_new)
                v_t = vt_ref[0, u // pairs_per_kv, t * sub + hf]
                ls[u] = alpha * ls[u] + col_reduce(p, jnp.sum)
                accs[u] = alpha * accs[u] + jnp.dot(v_t, p.astype(BF16), preferred_element_type=F32)
                ms[u] = m_new
        return tuple(ms), tuple(ls), tuple(accs)

    init = (tuple(jnp.full((1, wq), NEG_BIG, F32) for _ in range(n_unit)),
            tuple(jnp.zeros((1, wq), F32) for _ in range(n_unit)),
            tuple(jnp.zeros((HEAD_DIM, wq), F32) for _ in range(n_unit)))
    _, ls, accs = lax.fori_loop(0, nkt // sub, attn_tile, init)
    for n in range(ATTN_KV_HEADS):
        o_ref[0, 0, n] = jnp.concatenate([accs[n * pairs_per_kv + pg] / ls[n * pairs_per_kv + pg]
                                          for pg in range(pairs_per_kv)], axis=1).astype(BF16)


def _dsa(q, qi, wit, ki, k4, vt, s0, tq, kt):
    bsz, sc, _ = q.shape
    seq = ki.shape[1]
    topk = min(TOPK_MAX, seq // 4)
    nqb = sc // tq
    grp = ATTN_HEADS // ATTN_KV_HEADS
    seq_bits = int(math.log2(seq))
    assert 2 ** seq_bits == seq
    sub = 2 if seq % (2 * kt) == 0 else 1
    n_unit = ATTN_HEADS // 2
    assert IDX_HEADS // 2 <= n_unit
    kern =functools.partial(_dsa_kernel, qb0=s0 // tq, tq=tq, kt=kt, sub=sub, topk=topk, seq_bits=seq_bits)
    o_t = pl.pallas_call(
        kern,
        grid=(bsz, nqb),
        in_specs=[pl.BlockSpec((1, tq, IDX_HEADS * IDX_DIM), lambda b, j: (b, j, 0)),
                  pl.BlockSpec((1, IDX_HEADS, tq), lambda b, j: (b, 0, j)),
                  pl.BlockSpec((1, tq, ATTN_HEADS * HEAD_DIM), lambda b, j: (b, j, 0)),
                  pl.BlockSpec((1, seq, IDX_DIM), lambda b, j: (b, 0, 0)),
                  pl.BlockSpec((1, ATTN_KV_HEADS, seq, HEAD_DIM), lambda b, j: (b, 0, 0, 0)),
                  pl.BlockSpec((1, ATTN_KV_HEADS, seq // kt, HEAD_DIM, kt), lambda b, j: (b, 0, 0, 0, 0))],
        out_specs=pl.BlockSpec((1, 1, ATTN_KV_HEADS, HEAD_DIM, grp * tq), lambda b, j: (b, j, 0, 0, 0)),
        out_shape=jax.ShapeDtypeStruct((bsz, nqb, ATTN_KV_HEADS, HEAD_DIM, grp * tq), BF16),
        scratch_shapes=[pltpu.VMEM((seq, tq), I32), pltpu.VMEM((seq, tq), F32),
                        pltpu.VMEM((sub, n_unit, kt, 2 * tq), F32)],
        compiler_params=_cparams(("arbitrary", "arbitrary")),
        name="dsa",
    )(qi, wit, q, ki, k4, vt)
    o = o_t.reshape(bsz, nqb, ATTN_KV_HEADS, HEAD_DIM, grp, tq).transpose(0, 1, 5, 2, 4, 3)
    return o.reshape(bsz, sc, ATTN_HEADS * HEAD_DIM)


def _merge_kernel(x_ref, ys_ref, ya_ref, gs_ref, ga_ref, wsu_ref, wau_ref, wout_ref, g2_ref, wq_ref,
                  h_ref, hn_ref, qp_ref):
    ms = jnp.dot(ys_ref[0], wsu_ref[...], preferred_element_type=F32)
    ma = jnp.dot(ya_ref[0], wau_ref[...], preferred_element_type=F32)
    merged = gs_ref[0].astype(F32) * ms + ga_ref[0].astype(F32) * ma
    h = x_ref[0] + jnp.dot(merged.astype(BF16), wout_ref[...], preferred_element_type=F32)
    h_ref[0] = h
    hb = _rms(h, g2_ref[...]).astype(BF16)
    bits = lax.bitcast_convert_type(hb.astype(F32), I32)
    half = bits.shape[1] // 2
    words = (bits[:, half:] & jnp.int32(-65536)) | lax.shift_right_logical(bits[:, :half], 16)
    n_seg = half // LANES
    tm = words.shape[0]
    for p in range(n_seg):
        hn_ref[0, pl.ds(p, tm, stride=n_seg), :] = words[:, p * LANES:(p + 1) * LANES]
    qp_ref[0] = jnp.dot(hb, wq_ref[...], preferred_element_type=F32).astype(BF16)


def _merge(x, ys, ya, gs, ga, wsu, wau, wo, norm2_g, wq, s0, tm):
    bsz, sc, _ = ya.shape
    d = x.shape[2]
    i0 = s0 // tm
    loc = lambda a: pl.BlockSpec((1, tm, a.shape[2]), lambda b, i: (b, i, 0))
    full = lambda a: pl.BlockSpec(a.shape, lambda b, i: (0,) * a.ndim)
    g2 = norm2_g.reshape(1, d)
    nq = wq.shape[1]
    n_seg = d // 2 // LANES
    out = lambda n: pl.BlockSpec((1, tm, n), lambda b, i: (b, i, 0))
    return pl.pallas_call(
        _merge_kernel,
        grid=(bsz, sc // tm),
        in_specs=[pl.BlockSpec((1, tm, d), lambda b, i: (b, i + i0, 0)), loc(ys), loc(ya), loc(gs), loc(ga),
                  full(wsu), full(wau), full(wo), full(g2), full(wq)],
        out_specs=[out(d), pl.BlockSpec((1, tm * n_seg, LANES), lambda b, i: (b, i, 0)), out(nq)],
        out_shape=[jax.ShapeDtypeStruct((bsz, sc, d), F32), jax.ShapeDtypeStruct((bsz, sc * n_seg, LANES), I32),
                   jax.ShapeDtypeStruct((bsz, sc, nq), BF16)],
        compiler_params=_cparams(("arbitrary", "arbitrary")),
        name="merge",
    )(x, ys, ya, gs, ga, wsu, wau, wo, g2, wq)


def _cand_layout():
    blocks = []
    blocks.append((0, 16, 16))
    for i in range(1, 8):
        blocks.append((i, 8, PEER_TOPK // (i + 1)))
    blocks.append((None, 8, 8))
    return blocks


def _top_rows(s, order, payload, k):
    big = jnp.float32(3e38)
    vals, pays = [], []
    for _ in range(k):
        m = jnp.max(s, axis=0, keepdims=True)
        o = jnp.min(jnp.where(s == m, order, big), axis=0, keepdims=True)
        hit = order == o
        pays.append(o if payload is order else jnp.min(jnp.where(hit, payload, big), axis=0, keepdims=True))
        vals.append(m)
        s = jnp.where(hit, -jnp.inf, s)
    return jnp.concatenate(vals, axis=0), jnp.concatenate(pays, axis=0)


def _route_kernel(qp_ref, k1_ref, k2_ref, e_ref, g_ref, *, tt):
    qp = qp_ref[...]
    kd = PEER_KEY_DIM
    rows_k = lax.broadcasted_iota(I32, (PEER_KEYS, tt), 0).astype(F32)
    for h in range(PEER_HEADS):
        q1 = qp[:, (2 * h) * kd:(2 * h + 1) * kd]
        q2 = qp[:, (2 * h + 1) * kd:(2 * h + 2) * kd]
        s1 = lax.dot_general(k1_ref[h], q1, (((1,), (1,)), ((), ())), preferred_element_type=F32)
        s2 = lax.dot_general(k2_ref[h], q2, (((1,), (1,)), ((), ())), preferred_element_type=F32)
        v1, i1 = _top_rows(s1, rows_k, rows_k, PEER_TOPK)
        v2, i2 = _top_rows(s2, rows_k, rows_k, PEER_TOPK)
        cs, ce, co = [], [], []
        for i, rows, valid in _cand_layout():
            r = lax.broadcasted_iota(I32, (rows, tt), 0).astype(F32)
            if i is None:
                val = v1[8:16] + v2[0:1]
                eid = i1[8:16] * PEER_KEYS + i2[0:1]
                flat = (r + 8.0) * PEER_TOPK
            else:
                val = v1[i:i + 1] + v2[0:rows]
                eid = i1[i:i + 1] * PEER_KEYS + i2[0:rows]
                flat = r + float(i * PEER_TOPK)
                if valid < rows:
                    val = jnp.where(r < float(valid), val, -jnp.inf)
            cs.append(val)
            ce.append(eid)
            co.append(flat)
        cand = jnp.concatenate(cs, axis=0)
        top_s, top_e = _top_rows(cand, jnp.concatenate(co, axis=0), jnp.concatenate(ce, axis=0), PEER_TOPK)
        p = jnp.exp(top_s - top_s[0:1])
        gates = p / jnp.sum(p, axis=0, keepdims=True)
        e_ref[h * PEER_TOPK:(h + 1) * PEER_TOPK, :] = top_e.astype(I32)
        g_ref[h * PEER_TOPK:(h + 1) * PEER_TOPK, :] = gates


def _route(qp, k1, k2, tt):
    n, nq = qp.shape
    n_sel = PEER_HEADS * PEER_TOPK
    full = lambda a: pl.BlockSpec(a.shape, lambda i: (0,) * a.ndim)
    return pl.pallas_call(
        functools.partial(_route_kernel, tt=tt),
        grid=(n // tt,),
        in_specs=[pl.BlockSpec((tt, nq), lambda i: (i, 0)), full(k1), full(k2)],
        out_specs=[pl.BlockSpec((n_sel, tt), lambda i: (0, i)), pl.BlockSpec((n_sel, tt), lambda i: (0, i))],
        out_shape=[jax.ShapeDtypeStruct((n_sel, n), I32), jax.ShapeDtypeStruct((n_sel, n), F32)],
        compiler_params=_cparams(("arbitrary",)),
        name="route",
    )(qp, k1, k2)


def _final_kernel(h_ref, p_ref, g_ref, o_ref):
    tm, d = h_ref.shape
    n_seg = d // LANES
    p = jnp.concatenate([p_ref[pl.ds(k, tm, stride=n_seg), :] for k in range(n_seg)], axis=1)
    o_ref[...] = _rms(h_ref[...] + p, g_ref[...])


def _final(h, p, g, tm):
    n, d = h.shape
    row = pl.BlockSpec((tm, d), lambda i: (i, 0))
    return pl.pallas_call(
        _final_kernel,
        grid=(n // tm,),
        in_specs=[row, pl.BlockSpec((tm * (d // LANES), LANES), lambda i: (i, 0)),
                  pl.BlockSpec((1, d), lambda i: (0, 0))],
        out_specs=row,
        out_shape=jax.ShapeDtypeStruct((n, d), F32),
        compiler_params=_cparams(("arbitrary",)),
        name="final",
    )(h, p, g.reshape(1, d))


SC_CORES_V7X = 2
SC_SUBCORES_V7X = 16
SC_LANES_V7X = 16
PEER_TOK_BATCH = 32
PEER_ROW_CHUNK = 32
PEER_RING = 4


def _pack_bf16_pairs(t):
    half = t.shape[1] // 2
    tb = t.astype(BF16)
    lo = lax.bitcast_convert_type(tb[:, :half], jnp.uint16).astype(jnp.uint32)
    hi = lax.bitcast_convert_type(tb[:, half:], jnp.uint16).astype(jnp.uint32)
    return lax.bitcast_convert_type(lo | (hi << 16), I32)


def _unpack_pair(w):
    lo = lax.bitcast_convert_type(jnp.left_shift(w, 16), F32)
    hi = lax.bitcast_convert_type(w & jnp.int32(-65536), F32)
    return lo, hi


def _peer_sc_body(hn_hbm, e_hbm, g_hbm, u_hbm, v_hbm, out_hbm,
                  idx_v, gate_v, x_v, out_v, rows, p_v, act_v, sem, *, tpw, d, n_sel):
    nl = SC_LANES_V7X
    tb = PEER_TOK_BATCH
    rc = PEER_ROW_CHUNK
    n_chunk = n_sel // rc
    jobs_per_tok = 2 * n_chunk
    half = d // 2
    n_lane_blk = half // nl
    xs = half // LANES
    os_ = d // LANES
    wid =lax.axis_index("s") * SC_CORES_V7X + lax.axis_index("c")
    base = wid * tpw
    lane = lax.iota(I32, nl)
    zero = jnp.zeros((nl,), F32)
    c_gelu = 2.0 * math.sqrt(2.0 / math.pi)

    def gather_copy(tab_hbm, job):
        tok = job // jobs_per_tok
        c = (job % jobs_per_tok) % n_chunk
        b = job % PEER_RING
        return pltpu.make_async_copy(tab_hbm.at[idx_v.at[tok, pl.ds(c * rc, rc)]], rows.at[b], sem.at[b])

    def start(job):
        j = job % jobs_per_tok

        @pl.when(j < n_chunk)
        def _():
            gather_copy(u_hbm, job).start()

        @pl.when(j >= n_chunk)
        def _():
            gather_copy(v_hbm, job).start()

    def compute_u(tok, c, b):
        def rg_body(rg, _):
            r0 = rg * 8

            def jbody(j2, accs):
                off0 = j2 * (2 * nl)
                off1 = off0 + nl
                xrow = tok * xs + off0 // LANES
                xl = off0 % LANES
                x0 = plsc.bitcast(x_v[xrow, pl.ds(xl, nl)], BF16)
                x1 = plsc.bitcast(x_v[xrow, pl.ds(xl + nl, nl)], BF16)
                new = []
                for r in range(8):
                    w0 = plsc.bitcast(rows[b, r0 + r, pl.ds(off0, nl)], BF16)
                    w1 = plsc.bitcast(rows[b, r0 + r, pl.ds(off1, nl)], BF16)
                    lo, hi = _unpack_pair(plsc.bitcast(w0 * x0 + w1 * x1, I32))
                    new.append(accs[r] + (lo + hi))
                return tuple(new)

            accs = lax.fori_loop(0, n_lane_blk // 2, jbody, (zero,) * 8)
            for r in range(8):
                p_v[c * rc + r0 + r, :] = accs[r]
            return 0

        lax.fori_loop(0, rc // 8, rg_body, 0)

    def finish_act(tok):
        def eg_body(eg, _):
            e0 = eg * nl
            ridx = e0 + lane
            s = zero
            for l in range(nl):
                s = s + plsc.load_gather(p_v, [ridx, jnp.full((nl,), l, I32)])
            inner = c_gelu * (s + 0.044715 * (s * s * s))
            gl = s / (1.0 + jnp.exp(-inner))
            a = gl * gate_v[tok, pl.ds(e0, nl)]
            bits = lax.bitcast_convert_type(a, I32)
            rnd = bits + jnp.int32(0x7FFF) + (lax.shift_right_logical(bits, 16) & 1)
            hi16 = rnd & jnp.int32(-65536)
            act_v[pl.ds(e0, nl)] = hi16 | lax.shift_right_logical(hi16, 16)
            return 0

        lax.fori_loop(0, n_sel // nl, eg_body, 0)

        def zbody(j, _):
            off = j * nl
            out_v[tok * os_ + off // LANES, pl.ds(off % LANES, nl)] = zero
            return 0

        lax.fori_loop(0, d // nl, zbody, 0, unroll=4)

    def compute_v(tok, c, b):
        def rg_body(rg, _):
            r0 = rg * nl
            splat = [plsc.bitcast(plsc.load_gather(act_v, [jnp.full((nl,), 0, I32) + (c * rc + r0 + r)]), BF16)
                     for r in range(nl)]

            def tree(parts):
                while len(parts) > 1:
                    parts = [parts[i] + parts[i + 1] for i in range(0, len(parts), 2)]
                return parts[0]

            @plsc.parallel_loop(0, n_lane_blk, unroll=2)
            def _(j):
                off = j * nl
                los, his = [], []
                for r in range(0, nl, 2):
                    w0 = plsc.bitcast(rows[b, r0 + r, pl.ds(off, nl)], BF16)
                    w1 = plsc.bitcast(rows[b, r0 + r + 1, pl.ds(off, nl)], BF16)
                    lo, hi = _unpack_pair(plsc.bitcast(w0 * splat[r] + w1 * splat[r + 1], I32))
                    los.append(lo)
                    his.append(hi)
                orow = tok * os_ + off // LANES
                ol = off % LANES
                out_v[orow, pl.ds(ol, nl)] = out_v[orow, pl.ds(ol, nl)] + tree(los)
                out_v[orow + xs, pl.ds(ol, nl)] = out_v[orow + xs, pl.ds(ol, nl)] + tree(his)

            return 0

        lax.fori_loop(0, rc // nl, rg_body, 0)

    def batch_body(bi, _):
        t0 = base + bi * tb
        pltpu.sync_copy(e_hbm.at[pl.ds(t0, tb)], idx_v)
        pltpu.sync_copy(g_hbm.at[pl.ds(t0, tb)], gate_v)
        pltpu.sync_copy(hn_hbm.at[pl.ds(t0 * xs, tb * xs)], x_v)
        for pre in range(PEER_RING - 1):
            start(pre)

        def job_body(job, _):
            @pl.when(job + (PEER_RING - 1) < tb * jobs_per_tok)
            def _():
                start(job + (PEER_RING - 1))

            j = job % jobs_per_tok
            gather_copy(u_hbm, job).wait()
            tok = job // jobs_per_tok
            b = job % PEER_RING

            @pl.when(j < n_chunk)
            def _():
                compute_u(tok, j, b)

            @pl.when(j == n_chunk - 1)
            def _():
                finish_act(tok)

            @pl.when(j >= n_chunk)
            def _():
                compute_v(tok, j - n_chunk, b)

            return 0

        lax.fori_loop(0, tb * jobs_per_tok, job_body, 0)
        pltpu.sync_copy(out_v, out_hbm.at[pl.ds(t0 * os_, tb * os_)])
        return 0

    lax.fori_loop(0, tpw // tb, batch_body, 0)


def _peer_sc(x_pk, experts, gates, u_tab, v_tab):
    d = 2 * u_tab.shape[1]
    n = x_pk.shape[0] * LANES // (d // 2)
    n_sel = experts.shape[1]
    nw = SC_CORES_V7X * SC_SUBCORES_V7X
    tpw = n // nw
    mesh = plsc.VectorSubcoreMesh(core_axis_name="c", subcore_axis_name="s",
                                  num_cores=SC_CORES_V7X, num_subcores=SC_SUBCORES_V7X)
    body = functools.partial(_peer_sc_body, tpw=tpw, d=d, n_sel=n_sel)
    call = pl.kernel(
        body,
        out_type=jax.ShapeDtypeStruct((n * d // LANES, LANES), F32),
        mesh=mesh,
        scratch_types=[pltpu.VMEM((PEER_TOK_BATCH, n_sel), I32),
                       pltpu.VMEM((PEER_TOK_BATCH, n_sel), F32),
                       pltpu.VMEM((PEER_TOK_BATCH * d // 2 // LANES, LANES), I32),
                       pltpu.VMEM((PEER_TOK_BATCH * d // LANES, LANES), F32),
                       pltpu.VMEM((PEER_RING, PEER_ROW_CHUNK, d // 2), I32),
                       pltpu.VMEM((n_sel, SC_LANES_V7X), F32),
                       pltpu.VMEM((n_sel,), I32),
                       pltpu.SemaphoreType.DMA((PEER_RING,))],
        compiler_params=pltpu.CompilerParams(needs_layout_passes=False, use_tc_tiling_on_sc=False),
        name="peer_sc",
    )
    return call(x_pk, experts, gates, u_tab, v_tab)


def kernel(x, norm1_g, w_in, a_re, a_im, log_dt, b_re, b_im, c_re, c_im, d_skip, w_glu, w_ssm_up, w_attn_up,
           w_out, norm2_g, peer_wq, peer_k1, peer_k2, peer_u, peer_v, norm_f_g):
    bsz, seq, d = x.shape
    depth = norm1_g.shape[0]
    chunks = _time_chunks(seq)
    h = x
    for layer in range(depth):
        last = layer + 1 == depth
        kv_w, main_w = _in_weights(w_in[layer], seq)
        s5p = _s5_params(a_re[layer], a_im[layer], log_dt[layer], b_re[layer], b_im[layer], c_re[layer],
                         c_im[layer], d_skip[layer], w_glu[layer], nb=bsz)
        n_state = s5p[4].shape[1]
        wsu = w_ssm_up[layer].astype(BF16)
        wau = w_attn_up[layer].astype(BF16)
        wo = w_out[layer].astype(BF16)
        wq = peer_wq[layer].astype(BF16)
        k1 = peer_k1[layer].astype(BF16)
        k2 = peer_k2[layer].astype(BF16)
        u_pk = _pack_bf16_pairs(peer_u[layer])
        v_pk = _pack_bf16_pairs(peer_v[layer])
        k4, vt, ki = _kvproj(h, norm1_g[layer], kv_w, tm=min(ROW_TILE, seq), kt=DSA_KT)
        st_re = jnp.zeros((bsz, n_state), F32)
        st_im = jnp.zeros((bsz, n_state), F32)
        outs = []
        routed = ki
        peer_outs = []
        for c, (s0, sc) in enumerate(chunks):
            tm = math.gcd(math.gcd(s0, sc), ROW_TILE)
            after =(routed, peer_outs[c - SC_LAG] if c >= SC_LAG else ki)
            u, q, qi, wit, gs, ga = _inproj(h, norm1_g[layer], main_w, after, s0=s0, sc=sc, tm=tm)
            d_ssm = u.shape[-1]
            u_tb = u.transpose(1, 0, 2).reshape(sc * bsz, d_ssm)
            y_tb, st_re, st_im = _s5(u_tb, st_re, st_im, s5p, nb=bsz, tc=64)
            ys = y_tb.reshape(sc, bsz, d_ssm).transpose(1, 0, 2)
            ya = _dsa(q, qi, wit, ki, k4, vt, s0=s0, tq=DSA_TQ, kt=DSA_KT)
            hm, x_pk, qp = _merge(h, ys, ya, gs, ga, wsu, wau, wo, norm2_g[layer], wq, s0=s0, tm=tm)
            nt = bsz * sc
            e_t, g_t = _route(qp.reshape(nt, -1), k1, k2, tt=256)
            routed = e_t
            po = _peer_sc(x_pk.reshape(-1, LANES), e_t.T, g_t.T, u_pk, v_pk)
            peer_outs.append(po)
            hm2 = hm.reshape(nt, d)
            o = _final(hm2, po, norm_f_g, tm=tm) if last else hm2 + po.reshape(nt, d)
            outs.append(o.reshape(bsz, sc, d))
        h = jnp.concatenate(outs, axis=1)
    return h
```

```python
import functools
import math

import numpy as np
import jax
import jax.numpy as jnp
from jax import lax
from jax.experimental import pallas as pl
from jax.experimental.pallas import tpu as pltpu
from jax.experimental.pallas import tpu_sc as plsc

F32 = jnp.float32
BF16 = jnp.bfloat16
I32 = jnp.int32

SSM_GROUP = 16
SSM_STATE = 64
ATTN_HEADS = 8
ATTN_KV_HEADS = 2
HEAD_DIM = 64
IDX_HEADS = 8
IDX_DIM = 32
TOPK_MAX = 256
ROPE_THETA = 10000.0
NEG_BIG = -1e30
PEER_HEADS = 8
PEER_KEYS = 128
PEER_KEY_DIM = 128
PEER_TOPK = 16
NORM_EPS = 1e-6

TIME_SPLIT_32NDS = (1, 3, 4, 4, 4, 4, 4, 3, 2, 2, 1)
SC_LAG = 3
ROW_TILE = 512
DSA_TQ = 128
DSA_KT = 256
LANES = 128
INT_MIN = -(2 ** 31)
VMEM_LIMIT = 56 * 1024 * 1024


def _time_chunks(seq):
    unit = seq // 32
    if seq % 32 == 0 and unit % DSA_TQ == 0:
        sizes = [f * unit for f in TIME_SPLIT_32NDS]
    else:
        step = min(ROW_TILE, seq)
        sizes = [step] * (seq // step)
    assert sum(sizes) == seq
    starts = np.cumsum([0] + sizes[:-1]).tolist()
    return list(zip(starts, sizes))


def _cparams(sem):
    return pltpu.CompilerParams(dimension_semantics=sem, vmem_limit_bytes=VMEM_LIMIT)


def _gelu_tanh(x):
    return 0.5 * x * (1.0 + jnp.tanh(math.sqrt(2.0 / math.pi) * (x + 0.044715 * (x * x * x))))


def _sigmoid(x):
    return 1.0 / (1.0 + jnp.exp(-x))


def _rms(x, g):
    return x * lax.rsqrt(jnp.mean(x * x, axis=-1, keepdims=True) + NORM_EPS) * g


def _rot_cols(w, hd):
    d, n = w.shape
    w3 = w.reshape(d, n // hd, hd)
    half = hd // 2
    return jnp.concatenate([-w3[..., half:], w3[..., :half]], axis=-1).reshape(d, n)


def _rope_full(seq, hd, heads):
    pos = jnp.arange(seq, dtype=F32)
    inv = ROPE_THETA ** (-jnp.arange(0, hd, 2, dtype=F32) / hd)
    ang = pos[:, None] * inv[None, :]
    c = jnp.concatenate([jnp.cos(ang), jnp.cos(ang)], axis=-1)
    s = jnp.concatenate([jnp.sin(ang), jnp.sin(ang)], axis=-1)
    return jnp.tile(c, (1, heads)), jnp.tile(s, (1, heads))


def _in_weights(w_in, seq):
    d = w_in.shape[0]
    d_ssm = d // 2
    d_q = ATTN_HEADS * HEAD_DIM
    d_kv = ATTN_KV_HEADS * HEAD_DIM
    d_qi = IDX_HEADS * IDX_DIM
    splits = (d_ssm, d_q, d_kv, d_kv, d_qi, IDX_DIM, IDX_HEADS, d, d)
    offs = np.cumsum(splits)[:-1].tolist()
    wu, wq, wk, wv, wqi, wki, wwi, wgs, wga = jnp.split(w_in, offs, axis=1)
    pad = jnp.zeros((d, 128 - IDX_DIM), F32)
    cq, sq = _rope_full(seq, HEAD_DIM, ATTN_HEADS)
    ck, sk = _rope_full(seq, HEAD_DIM, ATTN_KV_HEADS)
    cqi, sqi = _rope_full(seq, IDX_DIM, IDX_HEADS)
    cki, ski = _rope_full(seq, IDX_DIM, 1)
    tpad = jnp.zeros((seq, 128 - IDX_DIM), F32)
    kv = dict(
        w=jnp.concatenate([wk, wki, pad], axis=1).astype(BF16),
        wvt=wv.T.astype(BF16),
        wr=jnp.concatenate([_rot_cols(wk, HEAD_DIM), _rot_cols(wki, IDX_DIM), pad], axis=1).astype(BF16),
        cs=jnp.concatenate([ck, cki, tpad], axis=1), sn=jnp.concatenate([sk, ski, tpad], axis=1))
    main = dict(
        w=jnp.concatenate([wu, wq, wqi, wgs, wga], axis=1).astype(BF16),
        wr=jnp.concatenate([_rot_cols(wq, HEAD_DIM), _rot_cols(wqi, IDX_DIM)], axis=1).astype(BF16),
        cs=jnp.concatenate([cq, cqi], axis=1), sn=jnp.concatenate([sq, sqi], axis=1),
        wwit=wwi.T.astype(BF16))
    return kv, main


def _kvproj_kernel(x_ref, g_ref, w_ref, wr_ref, wvt_ref, cs_ref, sn_ref, after_ref, k_ref, vt_ref, ki_ref,
                   *, d_kv, kt):
    del after_ref
    xb = _rms(x_ref[0], g_ref[...]).astype(BF16)

    def mm(ref, lo, n):
        return jnp.dot(xb, ref[:, lo:lo + n], preferred_element_type=F32)

    k = mm(w_ref, 0, d_kv) * cs_ref[:, :d_kv] + mm(wr_ref, 0, d_kv) * sn_ref[:, :d_kv]
    for n in range(ATTN_KV_HEADS):
        k_ref[0, n] = k[:, n * HEAD_DIM:(n + 1) * HEAD_DIM].astype(BF16)
    vt = lax.dot_general(wvt_ref[...], xb, (((1,), (1,)), ((), ())), preferred_element_type=F32)
    for n in range(ATTN_KV_HEADS):
        for j in range(vt.shape[1] // kt):
            vt_ref[0, n, j] = vt[n * HEAD_DIM:(n + 1) * HEAD_DIM, j * kt:(j + 1) * kt].astype(BF16)
    kiw = (mm(w_ref, d_kv, 128) * cs_ref[:, d_kv:d_kv + 128]
           + mm(wr_ref, d_kv, 128) * sn_ref[:, d_kv:d_kv + 128])
    ki_ref[0] = kiw[:, :IDX_DIM].astype(BF16)


def _kvproj(x, norm_g, kv, after, seq, tm, kt):
    bsz, _, d = x.shape
    d_kv = ATTN_KV_HEADS * HEAD_DIM
    full = lambda a: pl.BlockSpec(a.shape, lambda s, b: (0,) * a.ndim)
    g = norm_g.reshape(1, d)
    ncs = kv["cs"].shape[1]
    return pl.pallas_call(
        functools.partial(_kvproj_kernel, d_kv=d_kv, kt=kt),
        grid=(seq // tm, bsz),
        in_specs=[pl.BlockSpec((1, tm, d), lambda s, b: (b, s, 0)), full(g), full(kv["w"]), full(kv["wr"]),
                  full(kv["wvt"]),
                  pl.BlockSpec((tm, ncs), lambda s, b: (s, 0)), pl.BlockSpec((tm, ncs), lambda s, b: (s, 0)),
                  pl.BlockSpec(memory_space=pl.ANY)],
        out_specs=[pl.BlockSpec((1, ATTN_KV_HEADS, tm, HEAD_DIM), lambda s, b: (b, 0, s, 0)),
                   pl.BlockSpec((1, ATTN_KV_HEADS, tm // kt, HEAD_DIM, kt), lambda s, b: (b, 0, s, 0, 0)),
                   pl.BlockSpec((1, tm, IDX_DIM), lambda s, b: (b, s, 0))],
        out_shape=[jax.ShapeDtypeStruct((bsz, ATTN_KV_HEADS, seq, HEAD_DIM), BF16),
                   jax.ShapeDtypeStruct((bsz, ATTN_KV_HEADS, seq // kt, HEAD_DIM, kt), BF16),
                   jax.ShapeDtypeStruct((bsz, seq, IDX_DIM), BF16)],
        compiler_params=_cparams(("arbitrary", "arbitrary")),
        name="kvproj",
    )(x, g, kv["w"], kv["wr"], kv["wvt"], kv["cs"], kv["sn"], after)


def _inproj_kernel(x_ref, g_ref, w_ref, wr_ref, cs_ref, sn_ref, wwit_ref, after_tc_ref, after_sc_ref,
                   u_ref, q_ref, qi_ref, wit_ref, gs_ref, ga_ref, *, d_ssm, d_q, d_qi, d_model, q_scale, wi_scale):
    del after_tc_ref, after_sc_ref
    xb = _rms(x_ref[0], g_ref[...]).astype(BF16)

    def mm(ref, lo, n):
        return jnp.dot(xb, ref[:, lo:lo + n], preferred_element_type=F32)

    o = 0
    u_ref[0] = mm(w_ref, o, d_ssm).astype(BF16)
    o += d_ssm
    q = mm(w_ref, o, d_q) * cs_ref[:, :d_q] + mm(wr_ref, 0, d_q) * sn_ref[:, :d_q]
    q_ref[0] = (q * q_scale).astype(BF16)
    o += d_q
    qi = mm(w_ref, o, d_qi) * cs_ref[:, d_q:d_q + d_qi] + mm(wr_ref, d_q, d_qi) * sn_ref[:, d_q:d_q + d_qi]
    qi_ref[0] = qi.astype(BF16)
    o += d_qi
    gs_ref[0] = _sigmoid(mm(w_ref, o, d_model)).astype(BF16)
    o += d_model
    ga_ref[0] = _sigmoid(mm(w_ref, o, d_model)).astype(BF16)
    wit_ref[0] = lax.dot_general(wwit_ref[...], xb, (((1,), (1,)), ((), ())),
                                 preferred_element_type=F32) * wi_scale


def _inproj(x, norm_g, main, after, s0, sc, tm):
    bsz, _, d = x.shape
    d_ssm = d // 2
    d_q = ATTN_HEADS * HEAD_DIM
    d_qi = IDX_HEADS * IDX_DIM
    i0 = s0 // tm
    kern = functools.partial(
        _inproj_kernel, d_ssm=d_ssm, d_q=d_q, d_qi=d_qi, d_model=d,
        q_scale=HEAD_DIM ** -0.5 * math.log2(math.e), wi_scale=(IDX_HEADS ** -0.5) * (IDX_DIM ** -0.5))
    tok = lambda n: pl.BlockSpec((1, tm, n), lambda s, b: (b, s, 0))
    full = lambda a: pl.BlockSpec(a.shape, lambda s, b: (0,) * a.ndim)
    g = norm_g.reshape(1, d)
    ncs = main["cs"].shape[1]
    outs = [(d_ssm, BF16), (d_q, BF16), (d_qi, BF16)]
    return pl.pallas_call(
        kern,
        grid=(sc // tm, bsz),
        in_specs=[pl.BlockSpec((1, tm, d), lambda s, b: (b, s + i0, 0)), full(g), full(main["w"]), full(main["wr"]),
                  pl.BlockSpec((tm, ncs), lambda s, b: (s + i0, 0)),
                  pl.BlockSpec((tm, ncs), lambda s, b: (s + i0, 0)), full(main["wwit"]),
                  pl.BlockSpec(memory_space=pl.ANY), pl.BlockSpec(memory_space=pl.ANY)],
        out_specs=[tok(n) for n, _ in outs] + [pl.BlockSpec((1, IDX_HEADS, tm), lambda s, b: (b, 0, s)),
                                                tok(d), tok(d)],
        out_shape=[jax.ShapeDtypeStruct((bsz, sc, n), dt) for n, dt in outs]
        + [jax.ShapeDtypeStruct((bsz, IDX_HEADS, sc), F32),
           jax.ShapeDtypeStruct((bsz, sc, d), BF16), jax.ShapeDtypeStruct((bsz, sc, d), BF16)],
        compiler_params=_cparams(("arbitrary", "arbitrary")),
        name="inproj",
    )(x, g, main["w"], main["wr"], main["cs"], main["sn"], main["wwit"], *after)


def _s5_kernel(u_ref, sre_in, sim_in, bre_ref, bim_ref, cre_ref, cim_ref, are_ref, aim_ref, dsk_ref, wglu_ref,
               y_ref, st_re, st_im, sre, sim, *, tc, nb, lane_chunk):
    @pl.when(pl.program_id(0) == 0)
    def _():
        st_re[...] = sre_in[...]
        st_im[...] = sim_in[...]

    u = u_ref[...]
    n_half = bre_ref.shape[0]
    hin = bre_ref.shape[1]
    hst = bre_ref.shape[2]
    for h in range(n_half):
        uh = u[:, h * hin:(h + 1) * hin]
        sre[:, h * hst:(h + 1) * hst] = jnp.dot(uh, bre_ref[h], preferred_element_type=F32)
        sim[:, h * hst:(h + 1) * hst] = jnp.dot(uh, bim_ref[h], preferred_element_type=F32)

    n_state = sre.shape[1]
    for c in range(n_state // lane_chunk):
        cols = slice(c * lane_chunk, (c + 1) * lane_chunk)
        ar = are_ref[:, cols]
        ai = aim_ref[:, cols]

        def step(t, carry, cols=cols, ar=ar, ai=ai):
            sr, si = carry
            r0 = pl.multiple_of(t * nb, nb)
            nr = ar * sr - ai * si + sre[pl.ds(r0, nb), cols]
            ni = ar * si + ai * sr + sim[pl.ds(r0, nb), cols]
            sre[pl.ds(r0, nb), cols] = nr
            sim[pl.ds(r0, nb), cols] = ni
            return nr, ni

        sr, si = lax.fori_loop(0, tc, step, (st_re[:, cols], st_im[:, cols]), unroll=4)
        st_re[:, cols] = sr
        st_im[:, cols] = si

    ys = []
    for h in range(n_half):
        srh = sre[:, h * hst:(h + 1) * hst].astype(BF16)
        sih = sim[:, h * hst:(h + 1) * hst].astype(BF16)
        ys.append(jnp.dot(srh, cre_ref[h], preferred_element_type=F32)
                  - jnp.dot(sih, cim_ref[h], preferred_element_type=F32))
    y = jnp.concatenate(ys, axis=-1) + dsk_ref[...] * u.astype(F32)
    y = _gelu_tanh(y)
    gate = jnp.dot(y.astype(BF16), wglu_ref[...], preferred_element_type=F32)
    y_ref[...] = (y * _sigmoid(gate)).astype(BF16)


def _s5_params(a_re, a_im, log_dt, b_re, b_im, c_re, c_im, d_skip, w_glu, nb):
    groups = a_re.shape[0]
    d_ssm = groups * SSM_GROUP
    n_state = groups * SSM_STATE
    lam = lax.complex(a_re, a_im)
    dt = jnp.exp(log_dt)[:, None]
    a_bar = jnp.exp(lam * dt)
    b_bar = ((a_bar - 1.0) / lam)[..., None] * lax.complex(b_re, b_im)
    gh = min(groups, 256 // SSM_GROUP)
    n_half = groups // gh
    eye = jnp.eye(gh, dtype=F32)

    def bmat(bb):
        b4 = bb.reshape(n_half, gh, SSM_STATE, SSM_GROUP)
        return jnp.einsum('hgpc,gk->hgckp', b4, eye).reshape(n_half, gh * SSM_GROUP, gh * SSM_STATE)

    def cmat(cc):
        c4 = cc.reshape(n_half, gh, SSM_GROUP, SSM_STATE)
        return jnp.einsum('hgcp,gk->hgpkc', c4, eye).reshape(n_half, gh * SSM_STATE, gh * SSM_GROUP)

    return (bmat(jnp.real(b_bar)).astype(BF16), bmat(jnp.imag(b_bar)).astype(BF16),
            cmat(c_re).astype(BF16), cmat(c_im).astype(BF16),
            jnp.broadcast_to(jnp.real(a_bar).reshape(1, n_state), (nb, n_state)),
            jnp.broadcast_to(jnp.imag(a_bar).reshape(1, n_state), (nb, n_state)),
            d_skip.reshape(1, d_ssm), w_glu.astype(BF16))


def _s5(u_tb, st_re, st_im, params, nb, tc):
    rows, d_ssm = u_tb.shape
    n_state = st_re.shape[1]
    blk = tc * nb
    full = lambda a: pl.BlockSpec(a.shape, lambda i: (0,) * a.ndim)
    st_spec = pl.BlockSpec((nb, n_state), lambda i: (0, 0))
    kern = functools.partial(_s5_kernel, tc=tc, nb=nb, lane_chunk=512)
    return pl.pallas_call(
        kern,
        grid=(rows // blk,),
        in_specs=[pl.BlockSpec((blk, d_ssm), lambda i: (i, 0)), st_spec, st_spec] + [full(p) for p in params],
        out_specs=[pl.BlockSpec((blk, d_ssm), lambda i: (i, 0)), st_spec, st_spec],
        out_shape=[jax.ShapeDtypeStruct((rows, d_ssm), BF16),
                   jax.ShapeDtypeStruct((nb, n_state), F32), jax.ShapeDtypeStruct((nb, n_state), F32)],
        scratch_shapes=[pltpu.VMEM((blk, n_state), F32), pltpu.VMEM((blk, n_state), F32)],
        compiler_params=_cparams(("arbitrary",)),
        name="s5",
    )(u_tb, st_re, st_im, *params)


PART_ROWS = 32


def _dsa_kernel(qi_ref, wit_ref, q_ref, ki_ref, k_ref, vt_ref, o_ref, key_s, bias_s, lg_s,
                *, qb0, tq, kt, sub, topk, seq_bits):
    qb = pl.program_id(1) + qb0
    nkt = ((qb * tq + tq + sub * kt - 1) // (sub * kt)) * sub
    q_pos = qb * tq + lax.broadcasted_iota(I32, (1, tq), 1)
    k_eff = jnp.minimum(topk, q_pos + 1).astype(F32)

    qi = qi_ref[0]
    wit = wit_ref[0]
    qipair = [jnp.concatenate([qi[:, (2 * hp) * IDX_DIM:(2 * hp + 1) * IDX_DIM],
                               qi[:, (2 * hp + 1) * IDX_DIM:(2 * hp + 2) * IDX_DIM]], axis=0)
              for hp in range(IDX_HEADS // 2)]

    def key_pos(t):
        return t * kt + lax.broadcasted_iota(I32, (kt, tq), 0)

    def score_tile(t2, _):
        for hf in range(sub):
            r0 = pl.multiple_of((t2 * sub + hf) * kt, kt)
            ki_t = ki_ref[0, pl.ds(r0, kt), :]
            for hp in range(IDX_HEADS // 2):
                lg_s[hf, hp] = lax.dot_general(ki_t, qipair[hp], (((1,), (1,)), ((), ())),
                                               preferred_element_type=F32)
        for hf in range(sub):
            t = t2 * sub + hf
            r0 = pl.multiple_of(t * kt, kt)
            sc = jnp.zeros((kt, tq), F32)
            for hp in range(IDX_HEADS // 2):
                rel = lg_s[hf, hp]
                sc = sc + jnp.maximum(rel[:, :tq], 0.0) * wit[2 * hp:2 * hp + 1, :]
                sc = sc + jnp.maximum(rel[:, tq:], 0.0) * wit[2 * hp + 1:2 * hp + 2, :]
            bits = lax.bitcast_convert_type(sc, I32)
            key = jnp.where(bits < 0, bits ^ jnp.int32(0x7FFFFFFF), bits)
            key = jnp.where(key_pos(t) <= q_pos, key, jnp.int32(INT_MIN))
            key_s[pl.ds(r0, kt), :] = key
        return 0

    lax.fori_loop(0, nkt // sub, score_tile, 0)

    def count(pred_fn):
        def body(t, acc):
            r0 = pl.multiple_of(t * kt, kt)
            m = pred_fn(key_s[pl.ds(r0, kt), :], t)
            ones = jnp.where(m, 1.0, 0.0).reshape(kt // PART_ROWS, PART_ROWS, tq)
            return acc + jnp.sum(ones, axis=0)
        acc = lax.fori_loop(0, nkt, body, jnp.zeros((PART_ROWS, tq), F32))
        return jnp.sum(acc, axis=0, keepdims=True)

    def bit_step(i, u):
        bit = jnp.left_shift(jnp.int32(1), 31 - i)
        cand_u = u | bit
        cand_s = cand_u ^ jnp.int32(INT_MIN)
        cnt = count(lambda kk, t: kk >= cand_s)
        return jnp.where(cnt >= k_eff, cand_u, u)

    u_thr = lax.fori_loop(0, 32, bit_step, jnp.zeros((1, tq), I32))
    thr = u_thr ^ jnp.int32(INT_MIN)

    cnt_ge = count(lambda kk, t: kk >= thr)
    cnt_gt = count(lambda kk, t: kk > thr)
    need_eq = k_eff - cnt_gt
    has_tie = jnp.max(cnt_ge - k_eff) > 0.0

    def tie_cut():
        def pos_step(i, c):
            bit = jnp.left_shift(jnp.int32(1), seq_bits - 1 - i)
            cand = c | bit
            cnt = count(lambda kk, t: (kk == thr) & (key_pos(t) < cand))
            return jnp.where(cnt < need_eq, cand, c)
        return lax.fori_loop(0, seq_bits, pos_step, jnp.zeros((1, tq), I32))

    cut = lax.cond(has_tie, tie_cut, lambda: jnp.full((1, tq), 2 ** seq_bits, I32))

    def bias_tile(t, _):
        r0 = pl.multiple_of(t * kt, kt)
        key = key_s[pl.ds(r0, kt), :]
        sel = (key > thr) | ((key == thr) & (key_pos(t) <= cut))
        bias_s[pl.ds(r0, kt), :] = jnp.where(sel, 0.0, NEG_BIG)
        return 0

    lax.fori_loop(0, nkt, bias_tile, 0)

    q = q_ref[0]
    grp = ATTN_HEADS // ATTN_KV_HEADS
    pairs_per_kv = grp // 2
    n_unit = ATTN_KV_HEADS * pairs_per_kv
    wq = 2 * tq
    qpair = [jnp.concatenate([q[:, (2 * u) * HEAD_DIM:(2 * u + 1) * HEAD_DIM],
                              q[:, (2 * u + 1) * HEAD_DIM:(2 * u + 2) * HEAD_DIM]], axis=0)
             for u in range(n_unit)]

    def col_reduce(x, op):
        part = op(x.reshape(kt // PART_ROWS, PART_ROWS, wq), axis=0)
        return op(part, axis=0, keepdims=True)

    def attn_tile(t, carry):
        ms, ls, accs = list(carry[0]), list(carry[1]), list(carry[2])
        for hf in range(sub):
            r0 = pl.multiple_of((t * sub + hf) * kt, kt)
            bias = bias_s[pl.ds(r0, kt), :]
            bias2 = jnp.concatenate([bias, bias], axis=1)
            for u in range(n_unit):
                k_t = k_ref[0, u // pairs_per_kv, pl.ds(r0, kt), :]
                lg_s[hf, u] = lax.dot_general(k_t, qpair[u], (((1,), (1,)), ((), ())),
                                              preferred_element_type=F32) + bias2
        for hf in range(sub):
            for u in range(n_unit):
                lg = lg_s[hf, u]
                m_new = jnp.maximum(ms[u], col_reduce(lg, jnp.max))
                p = jnp.exp2(lg - m_new)
                alpha = jnp.exp2(ms[u] - m_new)
                v_t = vt_ref[0, u // pairs_per_kv, t * sub + hf]
                ls[u] = alpha * ls[u] + col_reduce(p, jnp.sum)
                accs[u] = alpha * accs[u] + jnp.dot(v_t, p.astype(BF16), preferred_element_type=F32)
                ms[u] = m_new
        return tuple(ms), tuple(ls), tuple(accs)

    init = (tuple(jnp.full((1, wq), NEG_BIG, F32) for _ in range(n_unit)),
            tuple(jnp.zeros((1, wq), F32) for _ in range(n_unit)),
            tuple(jnp.zeros((HEAD_DIM, wq), F32) for _ in range(n_unit)))
    _, ls, accs = lax.fori_loop(0, nkt // sub, attn_tile, init)
    for n in range(ATTN_KV_HEADS):
        o_ref[0, 0, n] = jnp.concatenate([accs[n * pairs_per_kv + pg] / ls[n * pairs_per_kv + pg]
                                          for pg in range(pairs_per_kv)], axis=1).astype(BF16)


def _dsa(q, qi, wit, ki, k4, vt, s0, seq_total, tq, kt):
    bsz, sc, _ = q.shape
    seq = ki.shape[1]
    topk = min(TOPK_MAX, seq_total // 4)
    nqb = sc // tq
    grp = ATTN_HEADS // ATTN_KV_HEADS
    seq_bits = int(math.log2(seq))
    assert 2 ** seq_bits == seq
    sub = 2 if seq % (2 * kt) == 0 else 1
    n_unit = ATTN_HEADS // 2
    assert IDX_HEADS // 2 <= n_unit
    kern =functools.partial(_dsa_kernel, qb0=s0 // tq, tq=tq, kt=kt, sub=sub, topk=topk, seq_bits=seq_bits)
    o_t = pl.pallas_call(
        kern,
        grid=(bsz, nqb),
        in_specs=[pl.BlockSpec((1, tq, IDX_HEADS * IDX_DIM), lambda b, j: (b, j, 0)),
                  pl.BlockSpec((1, IDX_HEADS, tq), lambda b, j: (b, 0, j)),
                  pl.BlockSpec((1, tq, ATTN_HEADS * HEAD_DIM), lambda b, j: (b, j, 0)),
                  pl.BlockSpec((1, seq, IDX_DIM), lambda b, j: (b, 0, 0)),
                  pl.BlockSpec((1, ATTN_KV_HEADS, seq, HEAD_DIM), lambda b, j: (b, 0, 0, 0)),
                  pl.BlockSpec((1, ATTN_KV_HEADS, seq // kt, HEAD_DIM, kt), lambda b, j: (b, 0, 0, 0, 0))],
        out_specs=pl.BlockSpec((1, 1, ATTN_KV_HEADS, HEAD_DIM, grp * tq), lambda b, j: (b, j, 0, 0, 0)),
        out_shape=jax.ShapeDtypeStruct((bsz, nqb, ATTN_KV_HEADS, HEAD_DIM, grp * tq), BF16),
        scratch_shapes=[pltpu.VMEM((seq, tq), I32), pltpu.VMEM((seq, tq), F32),
                        pltpu.VMEM((sub, n_unit, kt, 2 * tq), F32)],
        compiler_params=_cparams(("arbitrary", "arbitrary")),
        name="dsa",
    )(qi, wit, q, ki, k4, vt)
    o = o_t.reshape(bsz, nqb, ATTN_KV_HEADS, HEAD_DIM, grp, tq).transpose(0, 1, 5, 2, 4, 3)
    return o.reshape(bsz, sc, ATTN_HEADS * HEAD_DIM)


def _merge_kernel(x_ref, ys_ref, ya_ref, gs_ref, ga_ref, wsu_ref, wau_ref, wout_ref, g2_ref, wq_ref,
                  h_ref, hn_ref, qp_ref):
    ms = jnp.dot(ys_ref[0], wsu_ref[...], preferred_element_type=F32)
    ma = jnp.dot(ya_ref[0], wau_ref[...], preferred_element_type=F32)
    merged = gs_ref[0].astype(F32) * ms + ga_ref[0].astype(F32) * ma
    h = x_ref[0] + jnp.dot(merged.astype(BF16), wout_ref[...], preferred_element_type=F32)
    h_ref[0] = h
    hb = _rms(h, g2_ref[...]).astype(BF16)
    words = _pack_words(hb)
    half = words.shape[1]
    n_seg = half // LANES
    tm = words.shape[0]
    for p in range(n_seg):
        hn_ref[0, pl.ds(p, tm, stride=n_seg), :] = words[:, p * LANES:(p + 1) * LANES]
    qp_ref[0] = jnp.dot(hb, wq_ref[...], preferred_element_type=F32).astype(BF16)


def _merge(x, ys, ya, gs, ga, wsu, wau, wo, norm2_g, wq, s0, tm):
    bsz, sc, _ = ya.shape
    d = x.shape[2]
    i0 = s0 // tm
    loc = lambda a: pl.BlockSpec((1, tm, a.shape[2]), lambda b, i: (b, i, 0))
    full = lambda a: pl.BlockSpec(a.shape, lambda b, i: (0,) * a.ndim)
    g2 = norm2_g.reshape(1, d)
    nq = wq.shape[1]
    n_seg = d // 2 // LANES
    out = lambda n: pl.BlockSpec((1, tm, n), lambda b, i: (b, i, 0))
    return pl.pallas_call(
        _merge_kernel,
        grid=(bsz, sc // tm),
        in_specs=[pl.BlockSpec((1, tm, d), lambda b, i: (b, i + i0, 0)), loc(ys), loc(ya), loc(gs), loc(ga),
                  full(wsu), full(wau), full(wo), full(g2), full(wq)],
        out_specs=[out(d), pl.BlockSpec((1, tm * n_seg, LANES), lambda b, i: (b, i, 0)), out(nq)],
        out_shape=[jax.ShapeDtypeStruct((bsz, sc, d), F32), jax.ShapeDtypeStruct((bsz, sc * n_seg, LANES), I32),
                   jax.ShapeDtypeStruct((bsz, sc, nq), BF16)],
        compiler_params=_cparams(("arbitrary", "arbitrary")),
        name="merge",
    )(x, ys, ya, gs, ga, wsu, wau, wo, g2, wq)


def _cand_layout():
    blocks = []
    blocks.append((0, 16, 16))
    for i in range(1, 8):
        blocks.append((i, 8, PEER_TOPK // (i + 1)))
    blocks.append((None, 8, 8))
    return blocks


def _top_rows(s, order, payload, k):
    big = jnp.float32(3e38)
    vals, pays = [], []
    for _ in range(k):
        m = jnp.max(s, axis=0, keepdims=True)
        o = jnp.min(jnp.where(s == m, order, big), axis=0, keepdims=True)
        hit = order == o
        pays.append(o if payload is order else jnp.min(jnp.where(hit, payload, big), axis=0, keepdims=True))
        vals.append(m)
        s = jnp.where(hit, -jnp.inf, s)
    return jnp.concatenate(vals, axis=0), jnp.concatenate(pays, axis=0)


def _route_kernel(qp_ref, k1_ref, k2_ref, e_ref, g_ref, *, tt):
    qp = qp_ref[...]
    kd = PEER_KEY_DIM
    rows_k = lax.broadcasted_iota(I32, (PEER_KEYS, tt), 0).astype(F32)
    for h in range(PEER_HEADS):
        q1 = qp[:, (2 * h) * kd:(2 * h + 1) * kd]
        q2 = qp[:, (2 * h + 1) * kd:(2 * h + 2) * kd]
        s1 = lax.dot_general(k1_ref[h], q1, (((1,), (1,)), ((), ())), preferred_element_type=F32)
        s2 = lax.dot_general(k2_ref[h], q2, (((1,), (1,)), ((), ())), preferred_element_type=F32)
        v1, i1 = _top_rows(s1, rows_k, rows_k, PEER_TOPK)
        v2, i2 = _top_rows(s2, rows_k, rows_k, PEER_TOPK)
        cs, ce, co = [], [], []
        for i, rows, valid in _cand_layout():
            r = lax.broadcasted_iota(I32, (rows, tt), 0).astype(F32)
            if i is None:
                val = v1[8:16] + v2[0:1]
                eid = i1[8:16] * PEER_KEYS + i2[0:1]
                flat = (r + 8.0) * PEER_TOPK
            else:
                val = v1[i:i + 1] + v2[0:rows]
                eid = i1[i:i + 1] * PEER_KEYS + i2[0:rows]
                flat = r + float(i * PEER_TOPK)
                if valid < rows:
                    val = jnp.where(r < float(valid), val, -jnp.inf)
            cs.append(val)
            ce.append(eid)
            co.append(flat)
        cand = jnp.concatenate(cs, axis=0)
        top_s, top_e = _top_rows(cand, jnp.concatenate(co, axis=0), jnp.concatenate(ce, axis=0), PEER_TOPK)
        p = jnp.exp(top_s - top_s[0:1])
        gates = p / jnp.sum(p, axis=0, keepdims=True)
        e_ref[h * PEER_TOPK:(h + 1) * PEER_TOPK, :] = top_e.astype(I32)
        g_ref[h * PEER_TOPK:(h + 1) * PEER_TOPK, :] = gates


def _route(qp, k1, k2, tt):
    n, nq = qp.shape
    n_sel = PEER_HEADS * PEER_TOPK
    full = lambda a: pl.BlockSpec(a.shape, lambda i: (0,) * a.ndim)
    return pl.pallas_call(
        functools.partial(_route_kernel, tt=tt),
        grid=(n // tt,),
        in_specs=[pl.BlockSpec((tt, nq), lambda i: (i, 0)), full(k1), full(k2)],
        out_specs=[pl.BlockSpec((n_sel, tt), lambda i: (0, i)), pl.BlockSpec((n_sel, tt), lambda i: (0, i))],
        out_shape=[jax.ShapeDtypeStruct((n_sel, n), I32), jax.ShapeDtypeStruct((n_sel, n), F32)],
        compiler_params=_cparams(("arbitrary",)),
        name="route",
    )(qp, k1, k2)


def _final_kernel(h_ref, p_ref, g_ref, o_ref):
    tm, d = h_ref.shape
    n_seg = d // LANES
    p = jnp.concatenate([p_ref[pl.ds(k, tm, stride=n_seg), :] for k in range(n_seg)], axis=1)
    o_ref[...] = _rms(h_ref[...] + p, g_ref[...])


def _final(h, p, g, tm):
    n, d = h.shape
    row = pl.BlockSpec((tm, d), lambda i: (i, 0))
    return pl.pallas_call(
        _final_kernel,
        grid=(n // tm,),
        in_specs=[row, pl.BlockSpec((tm * (d // LANES), LANES), lambda i: (i, 0)),
                  pl.BlockSpec((1, d), lambda i: (0, 0))],
        out_specs=row,
        out_shape=jax.ShapeDtypeStruct((n, d), F32),
        compiler_params=_cparams(("arbitrary",)),
        name="final",
    )(h, p, g.reshape(1, d))


SC_CORES_V7X = 2
SC_SUBCORES_V7X = 16
SC_LANES_V7X = 16
PEER_TOK_BATCH = 32
PEER_ROW_CHUNK = 32
PEER_RING = 4


def _pack_words(x):
    bits = lax.bitcast_convert_type(x.astype(BF16).astype(F32), I32)
    half = bits.shape[1] // 2
    return (bits[:, half:] & jnp.int32(-65536)) | lax.shift_right_logical(bits[:, :half], 16)


def _pack_kernel(t_ref, o_ref):
    o_ref[...] = _pack_words(t_ref[...])


def _pack_bf16_pairs(t, rows=1024):
    e, d = t.shape
    return pl.pallas_call(
        _pack_kernel,
        grid=(e // rows,),
        in_specs=[pl.BlockSpec((rows, d), lambda i: (i, 0))],
        out_specs=pl.BlockSpec((rows, d // 2), lambda i: (i, 0)),
        out_shape=jax.ShapeDtypeStruct((e, d // 2), I32),
        compiler_params=_cparams(("arbitrary",)),
        name="pack_table",
    )(t)


def _unpack_pair(w):
    lo = lax.bitcast_convert_type(jnp.left_shift(w, 16), F32)
    hi = lax.bitcast_convert_type(w & jnp.int32(-65536), F32)
    return lo, hi


def _peer_sc_body(hn_hbm, e_hbm, g_hbm, u_hbm, v_hbm, out_hbm,
                  idx_v, gate_v, x_v, out_v, rows, p_v, act_v, sem, *, tpw, d, n_sel):
    nl = SC_LANES_V7X
    tb = PEER_TOK_BATCH
    rc = PEER_ROW_CHUNK
    n_chunk = n_sel // rc
    jobs_per_tok = 2 * n_chunk
    half = d // 2
    n_lane_blk = half // nl
    xs = half // LANES
    os_ = d // LANES
    wid =lax.axis_index("s") * SC_CORES_V7X + lax.axis_index("c")
    base = wid * tpw
    lane = lax.iota(I32, nl)
    zero = jnp.zeros((nl,), F32)
    c_gelu = 2.0 * math.sqrt(2.0 / math.pi)

    def gather_copy(tab_hbm, job):
        tok = job // jobs_per_tok
        c = (job % jobs_per_tok) % n_chunk
        b = job % PEER_RING
        return pltpu.make_async_copy(tab_hbm.at[idx_v.at[tok, pl.ds(c * rc, rc)]], rows.at[b], sem.at[b])

    def start(job):
        j = job % jobs_per_tok

        @pl.when(j < n_chunk)
        def _():
            gather_copy(u_hbm, job).start()

        @pl.when(j >= n_chunk)
        def _():
            gather_copy(v_hbm, job).start()

    def compute_u(tok, c, b):
        def rg_body(rg, _):
            r0 = rg * 8

            def jbody(j2, accs):
                off0 = j2 * (2 * nl)
                off1 = off0 + nl
                xrow = tok * xs + off0 // LANES
                xl = off0 % LANES
                x0 = plsc.bitcast(x_v[xrow, pl.ds(xl, nl)], BF16)
                x1 = plsc.bitcast(x_v[xrow, pl.ds(xl + nl, nl)], BF16)
                new = []
                for r in range(8):
                    w0 = plsc.bitcast(rows[b, r0 + r, pl.ds(off0, nl)], BF16)
                    w1 = plsc.bitcast(rows[b, r0 + r, pl.ds(off1, nl)], BF16)
                    lo, hi = _unpack_pair(plsc.bitcast(w0 * x0 + w1 * x1, I32))
                    new.append(accs[r] + (lo + hi))
                return tuple(new)

            accs = lax.fori_loop(0, n_lane_blk // 2, jbody, (zero,) * 8)
            for r in range(8):
                p_v[c * rc + r0 + r, :] = accs[r]
            return 0

        lax.fori_loop(0, rc // 8, rg_body, 0)

    def finish_act(tok):
        def eg_body(eg, _):
            e0 = eg * nl
            ridx = e0 + lane
            s = zero
            for l in range(nl):
                s = s + plsc.load_gather(p_v, [ridx, jnp.full((nl,), l, I32)])
            inner = c_gelu * (s + 0.044715 * (s * s * s))
            gl = s / (1.0 + jnp.exp(-inner))
            a = gl * gate_v[tok, pl.ds(e0, nl)]
            bits = lax.bitcast_convert_type(a, I32)
            rnd = bits + jnp.int32(0x7FFF) + (lax.shift_right_logical(bits, 16) & 1)
            hi16 = rnd & jnp.int32(-65536)
            act_v[pl.ds(e0, nl)] = hi16 | lax.shift_right_logical(hi16, 16)
            return 0

        lax.fori_loop(0, n_sel // nl, eg_body, 0)

        def zbody(j, _):
            off = j * nl
            out_v[tok * os_ + off // LANES, pl.ds(off % LANES, nl)] = zero
            return 0

        lax.fori_loop(0, d // nl, zbody, 0, unroll=4)

    def compute_v(tok, c, b):
        def rg_body(rg, _):
            r0 = rg * nl
            splat = [plsc.bitcast(plsc.load_gather(act_v, [jnp.full((nl,), 0, I32) + (c * rc + r0 + r)]), BF16)
                     for r in range(nl)]

            def tree(parts):
                while len(parts) > 1:
                    parts = [parts[i] + parts[i + 1] for i in range(0, len(parts), 2)]
                return parts[0]

            @plsc.parallel_loop(0, n_lane_blk, unroll=2)
            def _(j):
                off = j * nl
                los, his = [], []
                for r in range(0, nl, 2):
                    w0 = plsc.bitcast(rows[b, r0 + r, pl.ds(off, nl)], BF16)
                    w1 = plsc.bitcast(rows[b, r0 + r + 1, pl.ds(off, nl)], BF16)
                    lo, hi = _unpack_pair(plsc.bitcast(w0 * splat[r] + w1 * splat[r + 1], I32))
                    los.append(lo)
                    his.append(hi)
                orow = tok * os_ + off // LANES
                ol = off % LANES
                out_v[orow, pl.ds(ol, nl)] = out_v[orow, pl.ds(ol, nl)] + tree(los)
                out_v[orow + xs, pl.ds(ol, nl)] = out_v[orow + xs, pl.ds(ol, nl)] + tree(his)

            return 0

        lax.fori_loop(0, rc // nl, rg_body, 0)

    def batch_body(bi, _):
        t0 = base + bi * tb
        pltpu.sync_copy(e_hbm.at[pl.ds(t0, tb)], idx_v)
        pltpu.sync_copy(g_hbm.at[pl.ds(t0, tb)], gate_v)
        pltpu.sync_copy(hn_hbm.at[pl.ds(t0 * xs, tb * xs)], x_v)
        for pre in range(PEER_RING - 1):
            start(pre)

        def job_body(job, _):
            @pl.when(job + (PEER_RING - 1) < tb * jobs_per_tok)
            def _():
                start(job + (PEER_RING - 1))

            j = job % jobs_per_tok
            gather_copy(u_hbm, job).wait()
            tok = job // jobs_per_tok
            b = job % PEER_RING

            @pl.when(j < n_chunk)
            def _():
                compute_u(tok, j, b)

            @pl.when(j == n_chunk - 1)
            def _():
                finish_act(tok)

            @pl.when(j >= n_chunk)
            def _():
                compute_v(tok, j - n_chunk, b)

            return 0

        lax.fori_loop(0, tb * jobs_per_tok, job_body, 0)
        pltpu.sync_copy(out_v, out_hbm.at[pl.ds(t0 * os_, tb * os_)])
        return 0

    lax.fori_loop(0, tpw // tb, batch_body, 0)


def _peer_sc(x_pk, experts, gates, u_tab, v_tab):
    d = 2 * u_tab.shape[1]
    n = x_pk.shape[0] * LANES // (d // 2)
    n_sel = experts.shape[1]
    nw = SC_CORES_V7X * SC_SUBCORES_V7X
    tpw = n // nw
    mesh = plsc.VectorSubcoreMesh(core_axis_name="c", subcore_axis_name="s",
                                  num_cores=SC_CORES_V7X, num_subcores=SC_SUBCORES_V7X)
    body = functools.partial(_peer_sc_body, tpw=tpw, d=d, n_sel=n_sel)
    call = pl.kernel(
        body,
        out_type=jax.ShapeDtypeStruct((n * d // LANES, LANES), F32),
        mesh=mesh,
        scratch_types=[pltpu.VMEM((PEER_TOK_BATCH, n_sel), I32),
                       pltpu.VMEM((PEER_TOK_BATCH, n_sel), F32),
                       pltpu.VMEM((PEER_TOK_BATCH * d // 2 // LANES, LANES), I32),
                       pltpu.VMEM((PEER_TOK_BATCH * d // LANES, LANES), F32),
                       pltpu.VMEM((PEER_RING, PEER_ROW_CHUNK, d // 2), I32),
                       pltpu.VMEM((n_sel, SC_LANES_V7X), F32),
                       pltpu.VMEM((n_sel,), I32),
                       pltpu.SemaphoreType.DMA((PEER_RING,))],
        compiler_params=pltpu.CompilerParams(needs_layout_passes=False, use_tc_tiling_on_sc=False),
        name="peer_sc",
    )
    return call(x_pk, experts, gates, u_tab, v_tab)


def kernel(x, norm1_g, w_in, a_re, a_im, log_dt, b_re, b_im, c_re, c_im, d_skip, w_glu, w_ssm_up, w_attn_up,
           w_out, norm2_g, peer_wq, peer_k1, peer_k2, peer_u, peer_v, norm_f_g):
    bsz, seq, d = x.shape
    depth = norm1_g.shape[0]
    chunks = _time_chunks(seq)
    h = x
    for layer in range(depth):
        last = layer + 1 == depth
        kv_w, main_w = _in_weights(w_in[layer], seq)
        s5p = _s5_params(a_re[layer], a_im[layer], log_dt[layer], b_re[layer], b_im[layer], c_re[layer],
                         c_im[layer], d_skip[layer], w_glu[layer], nb=bsz)
        n_state = s5p[4].shape[1]
        wsu = w_ssm_up[layer].astype(BF16)
        wau = w_attn_up[layer].astype(BF16)
        wo = w_out[layer].astype(BF16)
        wq = peer_wq[layer].astype(BF16)
        k1 = peer_k1[layer].astype(BF16)
        k2 = peer_k2[layer].astype(BF16)
        u_pk = _pack_bf16_pairs(peer_u[layer])
        v_pk = _pack_bf16_pairs(peer_v[layer])
        kv_tm = min(ROW_TILE, seq)
        head_len = 2 * DSA_KT
        split_kv = len(chunks) > 1 and chunks[0][1] <= head_len < seq
        k4, vt, ki = _kvproj(h, norm1_g[layer], kv_w, norm1_g[layer], seq=head_len if split_kv else seq,
                             tm=kv_tm, kt=DSA_KT)
        st_re = jnp.zeros((bsz, n_state), F32)
        st_im = jnp.zeros((bsz, n_state), F32)
        outs = []
        routed = ki
        peer_outs = []
        for c, (s0, sc) in enumerate(chunks):
            tm = math.gcd(math.gcd(s0, sc), ROW_TILE)
            if c == 1 and split_kv:
                k4, vt, ki = _kvproj(h, norm1_g[layer], kv_w, routed, seq=seq, tm=kv_tm, kt=DSA_KT)
            after = (routed, peer_outs[c - SC_LAG] if c >= SC_LAG else ki)
            u, q, qi, wit, gs, ga = _inproj(h, norm1_g[layer], main_w, after, s0=s0, sc=sc, tm=tm)
            d_ssm = u.shape[-1]
            u_tb = u.transpose(1, 0, 2).reshape(sc * bsz, d_ssm)
            y_tb, st_re, st_im = _s5(u_tb, st_re, st_im, s5p, nb=bsz, tc=64)
            ys = y_tb.reshape(sc, bsz, d_ssm).transpose(1, 0, 2)
            ya = _dsa(q, qi, wit, ki, k4, vt, s0=s0, seq_total=seq, tq=DSA_TQ, kt=DSA_KT)
            hm, x_pk, qp = _merge(h, ys, ya, gs, ga, wsu, wau, wo, norm2_g[layer], wq, s0=s0, tm=tm)
            nt = bsz * sc
            e_t, g_t = _route(qp.reshape(nt, -1), k1, k2, tt=256)
            routed = e_t
            po = _peer_sc(x_pk.reshape(-1, LANES), e_t.T, g_t.T, u_pk, v_pk)
            peer_outs.append(po)
            hm2 = hm.reshape(nt, d)
            o = _final(hm2, po, norm_f_g, tm=tm) if last else hm2 + po.reshape(nt, d)
            outs.append(o.reshape(bsz, sc, d))
        h = jnp.concatenate(outs, axis=1)
    return h
```

```python
import functools
import math

import numpy as np
import jax
import jax.numpy as jnp
from jax import lax
from jax.experimental import pallas as pl
from jax.experimental.pallas import tpu as pltpu
from jax.experimental.pallas import tpu_sc as plsc

F32 = jnp.float32
BF16 = jnp.bfloat16
I32 = jnp.int32

SSM_GROUP = 16
SSM_STATE = 64
ATTN_HEADS = 8
ATTN_KV_HEADS = 2
HEAD_DIM = 64
IDX_HEADS = 8
IDX_DIM = 32
TOPK_MAX = 256
ROPE_THETA = 10000.0
NEG_BIG = -1e30
PEER_HEADS = 8
PEER_KEYS = 128
PEER_KEY_DIM = 128
PEER_TOPK = 16
NORM_EPS = 1e-6

TIME_SPLIT_32NDS = (1, 1, 2, 3, 4, 4, 4, 4, 3, 3, 2, 1)
SC_LAG = 3
ROW_TILE = 512
DSA_TQ = 128
DSA_KT = 256
LANES = 128
INT_MIN = -(2 ** 31)
VMEM_LIMIT = 56 * 1024 * 1024


def _time_chunks(seq):
    unit = seq // 32
    if seq % 32 == 0 and unit % DSA_TQ == 0:
        sizes = [f * unit for f in TIME_SPLIT_32NDS]
    else:
        step = min(ROW_TILE, seq)
        sizes = [step] * (seq // step)
    assert sum(sizes) == seq
    starts = np.cumsum([0] + sizes[:-1]).tolist()
    return list(zip(starts, sizes))


def _cparams(sem):
    return pltpu.CompilerParams(dimension_semantics=sem, vmem_limit_bytes=VMEM_LIMIT)


def _gelu_tanh(x):
    return 0.5 * x * (1.0 + jnp.tanh(math.sqrt(2.0 / math.pi) * (x + 0.044715 * (x * x * x))))


def _sigmoid(x):
    return 1.0 / (1.0 + jnp.exp(-x))


def _rms(x, g):
    return x * lax.rsqrt(jnp.mean(x * x, axis=-1, keepdims=True) + NORM_EPS) * g


def _rot_cols(w, hd):
    d, n = w.shape
    w3 = w.reshape(d, n // hd, hd)
    half = hd // 2
    return jnp.concatenate([-w3[..., half:], w3[..., :half]], axis=-1).reshape(d, n)


def _rope_full(seq, hd, heads):
    pos = jnp.arange(seq, dtype=F32)
    inv = ROPE_THETA ** (-jnp.arange(0, hd, 2, dtype=F32) / hd)
    ang = pos[:, None] * inv[None, :]
    c = jnp.concatenate([jnp.cos(ang), jnp.cos(ang)], axis=-1)
    s = jnp.concatenate([jnp.sin(ang), jnp.sin(ang)], axis=-1)
    return jnp.tile(c, (1, heads)), jnp.tile(s, (1, heads))


def _in_weights(w_in, seq):
    d = w_in.shape[0]
    d_ssm = d // 2
    d_q = ATTN_HEADS * HEAD_DIM
    d_kv = ATTN_KV_HEADS * HEAD_DIM
    d_qi = IDX_HEADS * IDX_DIM
    splits = (d_ssm, d_q, d_kv, d_kv, d_qi, IDX_DIM, IDX_HEADS, d, d)
    offs = np.cumsum(splits)[:-1].tolist()
    wu, wq, wk, wv, wqi, wki, wwi, wgs, wga = jnp.split(w_in, offs, axis=1)
    pad = jnp.zeros((d, 128 - IDX_DIM), F32)
    cq, sq = _rope_full(seq, HEAD_DIM, ATTN_HEADS)
    ck, sk = _rope_full(seq, HEAD_DIM, ATTN_KV_HEADS)
    cqi, sqi = _rope_full(seq, IDX_DIM, IDX_HEADS)
    cki, ski = _rope_full(seq, IDX_DIM, 1)
    tpad = jnp.zeros((seq, 128 - IDX_DIM), F32)
    kv = dict(
        w=jnp.concatenate([wk, wki, pad], axis=1).astype(BF16),
        wvt=wv.T.astype(BF16),
        wr=jnp.concatenate([_rot_cols(wk, HEAD_DIM), _rot_cols(wki, IDX_DIM), pad], axis=1).astype(BF16),
        cs=jnp.concatenate([ck, cki, tpad], axis=1), sn=jnp.concatenate([sk, ski, tpad], axis=1))
    main = dict(
        w=jnp.concatenate([wu, wq, wqi, wgs, wga], axis=1).astype(BF16),
        wr=jnp.concatenate([_rot_cols(wq, HEAD_DIM), _rot_cols(wqi, IDX_DIM)], axis=1).astype(BF16),
        cs=jnp.concatenate([cq, cqi], axis=1), sn=jnp.concatenate([sq, sqi], axis=1),
        wwit=wwi.T.astype(BF16))
    return kv, main


def _kvproj_kernel(x_ref, g_ref, w_ref, wr_ref, wvt_ref, cs_ref, sn_ref, after_ref, k_ref, vt_ref, ki_ref,
                   *, d_kv, kt):
    del after_ref
    xb = _rms(x_ref[0], g_ref[...]).astype(BF16)

    def mm(ref, lo, n):
        return jnp.dot(xb, ref[:, lo:lo + n], preferred_element_type=F32)

    k = mm(w_ref, 0, d_kv) * cs_ref[:, :d_kv] + mm(wr_ref, 0, d_kv) * sn_ref[:, :d_kv]
    for n in range(ATTN_KV_HEADS):
        k_ref[0, n] = k[:, n * HEAD_DIM:(n + 1) * HEAD_DIM].astype(BF16)
    vt = lax.dot_general(wvt_ref[...], xb, (((1,), (1,)), ((), ())), preferred_element_type=F32)
    for n in range(ATTN_KV_HEADS):
        for j in range(vt.shape[1] // kt):
            vt_ref[0, n, j] = vt[n * HEAD_DIM:(n + 1) * HEAD_DIM, j * kt:(j + 1) * kt].astype(BF16)
    kiw = (mm(w_ref, d_kv, 128) * cs_ref[:, d_kv:d_kv + 128]
           + mm(wr_ref, d_kv, 128) * sn_ref[:, d_kv:d_kv + 128])
    ki_ref[0] = kiw[:, :IDX_DIM].astype(BF16)


def _kvproj(x, norm_g, kv, after, seq, tm, kt):
    bsz, _, d = x.shape
    d_kv = ATTN_KV_HEADS * HEAD_DIM
    full = lambda a: pl.BlockSpec(a.shape, lambda s, b: (0,) * a.ndim)
    g = norm_g.reshape(1, d)
    ncs = kv["cs"].shape[1]
    return pl.pallas_call(
        functools.partial(_kvproj_kernel, d_kv=d_kv, kt=kt),
        grid=(seq // tm, bsz),
        in_specs=[pl.BlockSpec((1, tm, d), lambda s, b: (b, s, 0)), full(g), full(kv["w"]), full(kv["wr"]),
                  full(kv["wvt"]),
                  pl.BlockSpec((tm, ncs), lambda s, b: (s, 0)), pl.BlockSpec((tm, ncs), lambda s, b: (s, 0)),
                  pl.BlockSpec(memory_space=pl.ANY)],
        out_specs=[pl.BlockSpec((1, ATTN_KV_HEADS, tm, HEAD_DIM), lambda s, b: (b, 0, s, 0)),
                   pl.BlockSpec((1, ATTN_KV_HEADS, tm // kt, HEAD_DIM, kt), lambda s, b: (b, 0, s, 0, 0)),
                   pl.BlockSpec((1, tm, IDX_DIM), lambda s, b: (b, s, 0))],
        out_shape=[jax.ShapeDtypeStruct((bsz, ATTN_KV_HEADS, seq, HEAD_DIM), BF16),
                   jax.ShapeDtypeStruct((bsz, ATTN_KV_HEADS, seq // kt, HEAD_DIM, kt), BF16),
                   jax.ShapeDtypeStruct((bsz, seq, IDX_DIM), BF16)],
        compiler_params=_cparams(("arbitrary", "arbitrary")),
        name="kvproj",
    )(x, g, kv["w"], kv["wr"], kv["wvt"], kv["cs"], kv["sn"], after)


def _inproj_kernel(x_ref, g_ref, w_ref, wr_ref, cs_ref, sn_ref, wwit_ref, after_tc_ref, after_sc_ref,
                   u_ref, q_ref, qi_ref, wit_ref, gs_ref, ga_ref, *, d_ssm, d_q, d_qi, d_model, q_scale, wi_scale):
    del after_tc_ref, after_sc_ref
    xb = _rms(x_ref[0], g_ref[...]).astype(BF16)

    def mm(ref, lo, n):
        return jnp.dot(xb, ref[:, lo:lo + n], preferred_element_type=F32)

    o = 0
    u_ref[0] = mm(w_ref, o, d_ssm).astype(BF16)
    o += d_ssm
    q = mm(w_ref, o, d_q) * cs_ref[:, :d_q] + mm(wr_ref, 0, d_q) * sn_ref[:, :d_q]
    q_ref[0] = (q * q_scale).astype(BF16)
    o += d_q
    qi = mm(w_ref, o, d_qi) * cs_ref[:, d_q:d_q + d_qi] + mm(wr_ref, d_q, d_qi) * sn_ref[:, d_q:d_q + d_qi]
    qi_ref[0] = qi.astype(BF16)
    o += d_qi
    gs_ref[0] = _sigmoid(mm(w_ref, o, d_model)).astype(BF16)
    o += d_model
    ga_ref[0] = _sigmoid(mm(w_ref, o, d_model)).astype(BF16)
    wit_ref[0] = lax.dot_general(wwit_ref[...], xb, (((1,), (1,)), ((), ())),
                                 preferred_element_type=F32) * wi_scale


def _inproj(x, norm_g, main, after, s0, sc, tm):
    bsz, _, d = x.shape
    d_ssm = d // 2
    d_q = ATTN_HEADS * HEAD_DIM
    d_qi = IDX_HEADS * IDX_DIM
    i0 = s0 // tm
    kern = functools.partial(
        _inproj_kernel, d_ssm=d_ssm, d_q=d_q, d_qi=d_qi, d_model=d,
        q_scale=HEAD_DIM ** -0.5 * math.log2(math.e), wi_scale=(IDX_HEADS ** -0.5) * (IDX_DIM ** -0.5))
    tok = lambda n: pl.BlockSpec((1, tm, n), lambda s, b: (b, s, 0))
    full = lambda a: pl.BlockSpec(a.shape, lambda s, b: (0,) * a.ndim)
    g = norm_g.reshape(1, d)
    ncs = main["cs"].shape[1]
    outs = [(d_ssm, BF16), (d_q, BF16), (d_qi, BF16)]
    return pl.pallas_call(
        kern,
        grid=(sc // tm, bsz),
        in_specs=[pl.BlockSpec((1, tm, d), lambda s, b: (b, s + i0, 0)), full(g), full(main["w"]), full(main["wr"]),
                  pl.BlockSpec((tm, ncs), lambda s, b: (s + i0, 0)),
                  pl.BlockSpec((tm, ncs), lambda s, b: (s + i0, 0)), full(main["wwit"]),
                  pl.BlockSpec(memory_space=pl.ANY), pl.BlockSpec(memory_space=pl.ANY)],
        out_specs=[tok(n) for n, _ in outs] + [pl.BlockSpec((1, IDX_HEADS, tm), lambda s, b: (b, 0, s)),
                                                tok(d), tok(d)],
        out_shape=[jax.ShapeDtypeStruct((bsz, sc, n), dt) for n, dt in outs]
        + [jax.ShapeDtypeStruct((bsz, IDX_HEADS, sc), F32),
           jax.ShapeDtypeStruct((bsz, sc, d), BF16), jax.ShapeDtypeStruct((bsz, sc, d), BF16)],
        compiler_params=_cparams(("arbitrary", "arbitrary")),
        name="inproj",
    )(x, g, main["w"], main["wr"], main["cs"], main["sn"], main["wwit"], *after)


def _s5_kernel(u_ref, sre_in, sim_in, bre_ref, bim_ref, cre_ref, cim_ref, are_ref, aim_ref, dsk_ref, wglu_ref,
               y_ref, st_re, st_im, sre, sim, *, tc, nb, lane_chunk):
    @pl.when(pl.program_id(0) == 0)
    def _():
        st_re[...] = sre_in[...]
        st_im[...] = sim_in[...]

    u = u_ref[...]
    n_half = bre_ref.shape[0]
    hin = bre_ref.shape[1]
    hst = bre_ref.shape[2]
    for h in range(n_half):
        uh = u[:, h * hin:(h + 1) * hin]
        sre[:, h * hst:(h + 1) * hst] = jnp.dot(uh, bre_ref[h], preferred_element_type=F32)
        sim[:, h * hst:(h + 1) * hst] = jnp.dot(uh, bim_ref[h], preferred_element_type=F32)

    n_state = sre.shape[1]
    for c in range(n_state // lane_chunk):
        cols = slice(c * lane_chunk, (c + 1) * lane_chunk)
        ar = are_ref[:, cols]
        ai = aim_ref[:, cols]

        def step(t, carry, cols=cols, ar=ar, ai=ai):
            sr, si = carry
            r0 = pl.multiple_of(t * nb, nb)
            nr = ar * sr - ai * si + sre[pl.ds(r0, nb), cols]
            ni = ar * si + ai * sr + sim[pl.ds(r0, nb), cols]
            sre[pl.ds(r0, nb), cols] = nr
            sim[pl.ds(r0, nb), cols] = ni
            return nr, ni

        sr, si = lax.fori_loop(0, tc, step, (st_re[:, cols], st_im[:, cols]), unroll=4)
        st_re[:, cols] = sr
        st_im[:, cols] = si

    ys = []
    for h in range(n_half):
        srh = sre[:, h * hst:(h + 1) * hst].astype(BF16)
        sih = sim[:, h * hst:(h + 1) * hst].astype(BF16)
        ys.append(jnp.dot(srh, cre_ref[h], preferred_element_type=F32)
                  - jnp.dot(sih, cim_ref[h], preferred_element_type=F32))
    y = jnp.concatenate(ys, axis=-1) + dsk_ref[...] * u.astype(F32)
    y = _gelu_tanh(y)
    gate = jnp.dot(y.astype(BF16), wglu_ref[...], preferred_element_type=F32)
    y_ref[...] = (y * _sigmoid(gate)).astype(BF16)


def _s5_params(a_re, a_im, log_dt, b_re, b_im, c_re, c_im, d_skip, w_glu, nb):
    groups = a_re.shape[0]
    d_ssm = groups * SSM_GROUP
    n_state = groups * SSM_STATE
    lam = lax.complex(a_re, a_im)
    dt = jnp.exp(log_dt)[:, None]
    a_bar = jnp.exp(lam * dt)
    b_bar = ((a_bar - 1.0) / lam)[..., None] * lax.complex(b_re, b_im)
    gh = min(groups, 256 // SSM_GROUP)
    n_half = groups // gh
    eye = jnp.eye(gh, dtype=F32)

    def bmat(bb):
        b4 = bb.reshape(n_half, gh, SSM_STATE, SSM_GROUP)
        return jnp.einsum('hgpc,gk->hgckp', b4, eye).reshape(n_half, gh * SSM_GROUP, gh * SSM_STATE)

    def cmat(cc):
        c4 = cc.reshape(n_half, gh, SSM_GROUP, SSM_STATE)
        return jnp.einsum('hgcp,gk->hgpkc', c4, eye).reshape(n_half, gh * SSM_STATE, gh * SSM_GROUP)

    return (bmat(jnp.real(b_bar)).astype(BF16), bmat(jnp.imag(b_bar)).astype(BF16),
            cmat(c_re).astype(BF16), cmat(c_im).astype(BF16),
            jnp.broadcast_to(jnp.real(a_bar).reshape(1, n_state), (nb, n_state)),
            jnp.broadcast_to(jnp.imag(a_bar).reshape(1, n_state), (nb, n_state)),
            d_skip.reshape(1, d_ssm), w_glu.astype(BF16))


def _s5(u_tb, st_re, st_im, params, nb, tc):
    rows, d_ssm = u_tb.shape
    n_state = st_re.shape[1]
    blk = tc * nb
    full = lambda a: pl.BlockSpec(a.shape, lambda i: (0,) * a.ndim)
    st_spec = pl.BlockSpec((nb, n_state), lambda i: (0, 0))
    kern = functools.partial(_s5_kernel, tc=tc, nb=nb, lane_chunk=512)
    return pl.pallas_call(
        kern,
        grid=(rows // blk,),
        in_specs=[pl.BlockSpec((blk, d_ssm), lambda i: (i, 0)), st_spec, st_spec] + [full(p) for p in params],
        out_specs=[pl.BlockSpec((blk, d_ssm), lambda i: (i, 0)), st_spec, st_spec],
        out_shape=[jax.ShapeDtypeStruct((rows, d_ssm), BF16),
                   jax.ShapeDtypeStruct((nb, n_state), F32), jax.ShapeDtypeStruct((nb, n_state), F32)],
        scratch_shapes=[pltpu.VMEM((blk, n_state), F32), pltpu.VMEM((blk, n_state), F32)],
        compiler_params=_cparams(("arbitrary",)),
        name="s5",
    )(u_tb, st_re, st_im, *params)


PART_ROWS = 32


def _dsa_kernel(qi_ref, wit_ref, q_ref, ki_ref, k_ref, vt_ref, o_ref, key_s, bias_s, lg_s,
                *, qb0, tq, kt, sub, topk, seq_bits):
    qb = pl.program_id(1) + qb0
    nkt = ((qb * tq + tq + sub * kt - 1) // (sub * kt)) * sub
    q_pos = qb * tq + lax.broadcasted_iota(I32, (1, tq), 1)
    k_eff = jnp.minimum(topk, q_pos + 1).astype(F32)

    qi = qi_ref[0]
    wit = wit_ref[0]
    qipair = [jnp.concatenate([qi[:, (2 * hp) * IDX_DIM:(2 * hp + 1) * IDX_DIM],
                               qi[:, (2 * hp + 1) * IDX_DIM:(2 * hp + 2) * IDX_DIM]], axis=0)
              for hp in range(IDX_HEADS // 2)]

    def key_pos(t):
        return t * kt + lax.broadcasted_iota(I32, (kt, tq), 0)

    def score_tile(t2, _):
        for hf in range(sub):
            r0 = pl.multiple_of((t2 * sub + hf) * kt, kt)
            ki_t = ki_ref[0, pl.ds(r0, kt), :]
            for hp in range(IDX_HEADS // 2):
                lg_s[hf, hp] = lax.dot_general(ki_t, qipair[hp], (((1,), (1,)), ((), ())),
                                               preferred_element_type=F32)
        for hf in range(sub):
            t = t2 * sub + hf
            r0 = pl.multiple_of(t * kt, kt)
            sc = jnp.zeros((kt, tq), F32)
            for hp in range(IDX_HEADS // 2):
                rel = lg_s[hf, hp]
                sc = sc + jnp.maximum(rel[:, :tq], 0.0) * wit[2 * hp:2 * hp + 1, :]
                sc = sc + jnp.maximum(rel[:, tq:], 0.0) * wit[2 * hp + 1:2 * hp + 2, :]
            bits = lax.bitcast_convert_type(sc, I32)
            key = jnp.where(bits < 0, bits ^ jnp.int32(0x7FFFFFFF), bits)
            key = jnp.where(key_pos(t) <= q_pos, key, jnp.int32(INT_MIN))
            key_s[pl.ds(r0, kt), :] = key
        return 0

    lax.fori_loop(0, nkt // sub, score_tile, 0)

    def count(pred_fn):
        def body(t, acc):
            r0 = pl.multiple_of(t * kt, kt)
            m = pred_fn(key_s[pl.ds(r0, kt), :], t)
            ones = jnp.where(m, 1.0, 0.0).reshape(kt // PART_ROWS, PART_ROWS, tq)
            return acc + jnp.sum(ones, axis=0)
        acc = lax.fori_loop(0, nkt, body, jnp.zeros((PART_ROWS, tq), F32))
        return jnp.sum(acc, axis=0, keepdims=True)

    def bit_step(i, u):
        bit = jnp.left_shift(jnp.int32(1), 31 - i)
        cand_u = u | bit
        cand_s = cand_u ^ jnp.int32(INT_MIN)
        cnt = count(lambda kk, t: kk >= cand_s)
        return jnp.where(cnt >= k_eff, cand_u, u)

    u_thr = lax.fori_loop(0, 32, bit_step, jnp.zeros((1, tq), I32))
    thr = u_thr ^ jnp.int32(INT_MIN)

    cnt_ge = count(lambda kk, t: kk >= thr)
    cnt_gt = count(lambda kk, t: kk > thr)
    need_eq = k_eff - cnt_gt
    has_tie = jnp.max(cnt_ge - k_eff) > 0.0

    def tie_cut():
        def pos_step(i, c):
            bit = jnp.left_shift(jnp.int32(1), seq_bits - 1 - i)
            cand = c | bit
            cnt = count(lambda kk, t: (kk == thr) & (key_pos(t) < cand))
            return jnp.where(cnt < need_eq, cand, c)
        return lax.fori_loop(0, seq_bits, pos_step, jnp.zeros((1, tq), I32))

    cut = lax.cond(has_tie, tie_cut, lambda: jnp.full((1, tq), 2 ** seq_bits, I32))

    def bias_tile(t, _):
        r0 = pl.multiple_of(t * kt, kt)
        key = key_s[pl.ds(r0, kt), :]
        sel = (key > thr) | ((key == thr) & (key_pos(t) <= cut))
        bias_s[pl.ds(r0, kt), :] = jnp.where(sel, 0.0, NEG_BIG)
        return 0

    lax.fori_loop(0, nkt, bias_tile, 0)

    q = q_ref[0]
    grp = ATTN_HEADS // ATTN_KV_HEADS
    pairs_per_kv = grp // 2
    n_unit = ATTN_KV_HEADS * pairs_per_kv
    wq = 2 * tq
    qpair = [jnp.concatenate([q[:, (2 * u) * HEAD_DIM:(2 * u + 1) * HEAD_DIM],
                              q[:, (2 * u + 1) * HEAD_DIM:(2 * u + 2) * HEAD_DIM]], axis=0)
             for u in range(n_unit)]

    def col_reduce(x, op):
        part = op(x.reshape(kt // PART_ROWS, PART_ROWS, wq), axis=0)
        return op(part, axis=0, keepdims=True)

    def attn_tile(t, carry):
        ms, ls, accs = list(carry[0]), list(carry[1]), list(carry[2])
        for hf in range(sub):
            r0 = pl.multiple_of((t * sub + hf) * kt, kt)
            bias = bias_s[pl.ds(r0, kt), :]
            bias2 = jnp.concatenate([bias, bias], axis=1)
            for u in range(n_unit):
                k_t = k_ref[0, u // pairs_per_kv, pl.ds(r0, kt), :]
                lg_s[hf, u] = lax.dot_general(k_t, qpair[u], (((1,), (1,)), ((), ())),
                                              preferred_element_type=F32) + bias2
        for hf in range(sub):
            for u in range(n_unit):
                lg = lg_s[hf, u]
                m_new = jnp.maximum(ms[u], col_reduce(lg, jnp.max))
                p = jnp.exp2(lg - m_new)
                alpha = jnp.exp2(ms[u] - m_new)
                v_t = vt_ref[0, u // pairs_per_kv, t * sub + hf]
                ls[u] = alpha * ls[u] + col_reduce(p, jnp.sum)
                accs[u] = alpha * accs[u] + jnp.dot(v_t, p.astype(BF16), preferred_element_type=F32)
                ms[u] = m_new
        return tuple(ms), tuple(ls), tuple(accs)

    init = (tuple(jnp.full((1, wq), NEG_BIG, F32) for _ in range(n_unit)),
            tuple(jnp.zeros((1, wq), F32) for _ in range(n_unit)),
            tuple(jnp.zeros((HEAD_DIM, wq), F32) for _ in range(n_unit)))
    _, ls, accs = lax.fori_loop(0, nkt // sub, attn_tile, init)
    for n in range(ATTN_KV_HEADS):
        o_ref[0, 0, n] = jnp.concatenate([accs[n * pairs_per_kv + pg] / ls[n * pairs_per_kv + pg]
                                          for pg in range(pairs_per_kv)], axis=1).astype(BF16)


def _dsa(q, qi, wit, ki, k4, vt, s0, seq_total, tq, kt):
    bsz, sc, _ = q.shape
    seq = ki.shape[1]
    topk = min(TOPK_MAX, seq_total // 4)
    nqb = sc // tq
    grp = ATTN_HEADS // ATTN_KV_HEADS
    seq_bits = int(math.log2(seq))
    assert 2 ** seq_bits == seq
    sub = 2 if seq % (2 * kt) == 0 else 1
    n_unit = ATTN_HEADS // 2
    assert IDX_HEADS // 2 <= n_unit
    kern =functools.partial(_dsa_kernel, qb0=s0 // tq, tq=tq, kt=kt, sub=sub, topk=topk, seq_bits=seq_bits)
    o_t = pl.pallas_call(
        kern,
        grid=(bsz, nqb),
        in_specs=[pl.BlockSpec((1, tq, IDX_HEADS * IDX_DIM), lambda b, j: (b, j, 0)),
                  pl.BlockSpec((1, IDX_HEADS, tq), lambda b, j: (b, 0, j)),
                  pl.BlockSpec((1, tq, ATTN_HEADS * HEAD_DIM), lambda b, j: (b, j, 0)),
                  pl.BlockSpec((1, seq, IDX_DIM), lambda b, j: (b, 0, 0)),
                  pl.BlockSpec((1, ATTN_KV_HEADS, seq, HEAD_DIM), lambda b, j: (b, 0, 0, 0)),
                  pl.BlockSpec((1, ATTN_KV_HEADS, seq // kt, HEAD_DIM, kt), lambda b, j: (b, 0, 0, 0, 0))],
        out_specs=pl.BlockSpec((1, 1, ATTN_KV_HEADS, HEAD_DIM, grp * tq), lambda b, j: (b, j, 0, 0, 0)),
        out_shape=jax.ShapeDtypeStruct((bsz, nqb, ATTN_KV_HEADS, HEAD_DIM, grp * tq), BF16),
        scratch_shapes=[pltpu.VMEM((seq, tq), I32), pltpu.VMEM((seq, tq), F32),
                        pltpu.VMEM((sub, n_unit, kt, 2 * tq), F32)],
        compiler_params=_cparams(("arbitrary", "arbitrary")),
        name="dsa",
    )(qi, wit, q, ki, k4, vt)
    o = o_t.reshape(bsz, nqb, ATTN_KV_HEADS, HEAD_DIM, grp, tq).transpose(0, 1, 5, 2, 4, 3)
    return o.reshape(bsz, sc, ATTN_HEADS * HEAD_DIM)


def _merge_kernel(x_ref, ys_ref, ya_ref, gs_ref, ga_ref, wsu_ref, wau_ref, wout_ref, g2_ref, wq_ref,
                  h_ref, hn_ref, qp_ref):
    ms = jnp.dot(ys_ref[0], wsu_ref[...], preferred_element_type=F32)
    ma = jnp.dot(ya_ref[0], wau_ref[...], preferred_element_type=F32)
    merged = gs_ref[0].astype(F32) * ms + ga_ref[0].astype(F32) * ma
    h = x_ref[0] + jnp.dot(merged.astype(BF16), wout_ref[...], preferred_element_type=F32)
    h_ref[0] = h
    hb = _rms(h, g2_ref[...]).astype(BF16)
    words = _pack_words(hb)
    half = words.shape[1]
    n_seg = half // LANES
    tm = words.shape[0]
    for p in range(n_seg):
        hn_ref[0, pl.ds(p, tm, stride=n_seg), :] = words[:, p * LANES:(p + 1) * LANES]
    qp_ref[0] = jnp.dot(hb, wq_ref[...], preferred_element_type=F32).astype(BF16)


def _merge(x, ys, ya, gs, ga, wsu, wau, wo, norm2_g, wq, s0, tm):
    bsz, sc, _ = ya.shape
    d = x.shape[2]
    i0 = s0 // tm
    loc = lambda a: pl.BlockSpec((1, tm, a.shape[2]), lambda b, i: (b, i, 0))
    full = lambda a: pl.BlockSpec(a.shape, lambda b, i: (0,) * a.ndim)
    g2 = norm2_g.reshape(1, d)
    nq = wq.shape[1]
    n_seg = d // 2 // LANES
    out = lambda n: pl.BlockSpec((1, tm, n), lambda b, i: (b, i, 0))
    return pl.pallas_call(
        _merge_kernel,
        grid=(bsz, sc // tm),
        in_specs=[pl.BlockSpec((1, tm, d), lambda b, i: (b, i + i0, 0)), loc(ys), loc(ya), loc(gs), loc(ga),
                  full(wsu), full(wau), full(wo), full(g2), full(wq)],
        out_specs=[out(d), pl.BlockSpec((1, tm * n_seg, LANES), lambda b, i: (b, i, 0)), out(nq)],
        out_shape=[jax.ShapeDtypeStruct((bsz, sc, d), F32), jax.ShapeDtypeStruct((bsz, sc * n_seg, LANES), I32),
                   jax.ShapeDtypeStruct((bsz, sc, nq), BF16)],
        compiler_params=_cparams(("arbitrary", "arbitrary")),
        name="merge",
    )(x, ys, ya, gs, ga, wsu, wau, wo, g2, wq)


def _cand_layout():
    blocks = []
    blocks.append((0, 16, 16))
    for i in range(1, 8):
        blocks.append((i, 8, PEER_TOPK // (i + 1)))
    blocks.append((None, 8, 8))
    return blocks


def _top_rows(s, order, payload, k):
    big = jnp.float32(3e38)
    vals, pays = [], []
    for _ in range(k):
        m = jnp.max(s, axis=0, keepdims=True)
        o = jnp.min(jnp.where(s == m, order, big), axis=0, keepdims=True)
        hit = order == o
        pays.append(o if payload is order else jnp.min(jnp.where(hit, payload, big), axis=0, keepdims=True))
        vals.append(m)
        s = jnp.where(hit, -jnp.inf, s)
    return jnp.concatenate(vals, axis=0), jnp.concatenate(pays, axis=0)


def _route_kernel(qp_ref, k1_ref, k2_ref, e_ref, g_ref, *, tt):
    qp = qp_ref[...]
    kd = PEER_KEY_DIM
    rows_k = lax.broadcasted_iota(I32, (PEER_KEYS, tt), 0).astype(F32)
    for h in range(PEER_HEADS):
        q1 = qp[:, (2 * h) * kd:(2 * h + 1) * kd]
        q2 = qp[:, (2 * h + 1) * kd:(2 * h + 2) * kd]
        s1 = lax.dot_general(k1_ref[h], q1, (((1,), (1,)), ((), ())), preferred_element_type=F32)
        s2 = lax.dot_general(k2_ref[h], q2, (((1,), (1,)), ((), ())), preferred_element_type=F32)
        v1, i1 = _top_rows(s1, rows_k, rows_k, PEER_TOPK)
        v2, i2 = _top_rows(s2, rows_k, rows_k, PEER_TOPK)
        cs, ce, co = [], [], []
        for i, rows, valid in _cand_layout():
            r = lax.broadcasted_iota(I32, (rows, tt), 0).astype(F32)
            if i is None:
                val = v1[8:16] + v2[0:1]
                eid = i1[8:16] * PEER_KEYS + i2[0:1]
                flat = (r + 8.0) * PEER_TOPK
            else:
                val = v1[i:i + 1] + v2[0:rows]
                eid = i1[i:i + 1] * PEER_KEYS + i2[0:rows]
                flat = r + float(i * PEER_TOPK)
                if valid < rows:
                    val = jnp.where(r < float(valid), val, -jnp.inf)
            cs.append(val)
            ce.append(eid)
            co.append(flat)
        cand = jnp.concatenate(cs, axis=0)
        top_s, top_e = _top_rows(cand, jnp.concatenate(co, axis=0), jnp.concatenate(ce, axis=0), PEER_TOPK)
        p = jnp.exp(top_s - top_s[0:1])
        gates = p / jnp.sum(p, axis=0, keepdims=True)
        e_ref[h * PEER_TOPK:(h + 1) * PEER_TOPK, :] = top_e.astype(I32)
        g_ref[h * PEER_TOPK:(h + 1) * PEER_TOPK, :] = gates


def _route(qp, k1, k2, tt):
    n, nq = qp.shape
    n_sel = PEER_HEADS * PEER_TOPK
    full = lambda a: pl.BlockSpec(a.shape, lambda i: (0,) * a.ndim)
    return pl.pallas_call(
        functools.partial(_route_kernel, tt=tt),
        grid=(n // tt,),
        in_specs=[pl.BlockSpec((tt, nq), lambda i: (i, 0)), full(k1), full(k2)],
        out_specs=[pl.BlockSpec((n_sel, tt), lambda i: (0, i)), pl.BlockSpec((n_sel, tt), lambda i: (0, i))],
        out_shape=[jax.ShapeDtypeStruct((n_sel, n), I32), jax.ShapeDtypeStruct((n_sel, n), F32)],
        compiler_params=_cparams(("arbitrary",)),
        name="route",
    )(qp, k1, k2)


def _final_kernel(h_ref, p_ref, g_ref, o_ref):
    tm, d = h_ref.shape
    n_seg = d // LANES
    p = jnp.concatenate([p_ref[pl.ds(k, tm, stride=n_seg), :] for k in range(n_seg)], axis=1)
    o_ref[...] = _rms(h_ref[...] + p, g_ref[...])


def _final(h, p, g, tm):
    n, d = h.shape
    row = pl.BlockSpec((tm, d), lambda i: (i, 0))
    return pl.pallas_call(
        _final_kernel,
        grid=(n // tm,),
        in_specs=[row, pl.BlockSpec((tm * (d // LANES), LANES), lambda i: (i, 0)),
                  pl.BlockSpec((1, d), lambda i: (0, 0))],
        out_specs=row,
        out_shape=jax.ShapeDtypeStruct((n, d), F32),
        compiler_params=_cparams(("arbitrary",)),
        name="final",
    )(h, p, g.reshape(1, d))


SC_CORES_V7X = 2
SC_SUBCORES_V7X = 16
SC_LANES_V7X = 16
PEER_TOK_BATCH = 32
PEER_ROW_CHUNK = 32
PEER_RING = 4


def _pack_words(x):
    bits = lax.bitcast_convert_type(x.astype(BF16).astype(F32), I32)
    half = bits.shape[1] // 2
    return (bits[:, half:] & jnp.int32(-65536)) | lax.shift_right_logical(bits[:, :half], 16)


def _pack_kernel(t_ref, o_ref):
    o_ref[...] = _pack_words(t_ref[...])


def _pack_bf16_pairs(t, rows=1024):
    e, d = t.shape
    return pl.pallas_call(
        _pack_kernel,
        grid=(e // rows,),
        in_specs=[pl.BlockSpec((rows, d), lambda i: (i, 0))],
        out_specs=pl.BlockSpec((rows, d // 2), lambda i: (i, 0)),
        out_shape=jax.ShapeDtypeStruct((e, d // 2), I32),
        compiler_params=_cparams(("arbitrary",)),
        name="pack_table",
    )(t)


def _unpack_pair(w):
    lo = lax.bitcast_convert_type(jnp.left_shift(w, 16), F32)
    hi = lax.bitcast_convert_type(w & jnp.int32(-65536), F32)
    return lo, hi


def _peer_sc_body(hn_hbm, e_hbm, g_hbm, u_hbm, v_hbm, out_hbm,
                  idx_v, gate_v, x_v, out_v, rows, p_v, act_v, sem, *, tpw, d, n_sel):
    nl = SC_LANES_V7X
    tb = PEER_TOK_BATCH
    rc = PEER_ROW_CHUNK
    n_chunk = n_sel // rc
    jobs_per_tok = 2 * n_chunk
    half = d // 2
    n_lane_blk = half // nl
    xs = half // LANES
    os_ = d // LANES
    wid =lax.axis_index("s") * SC_CORES_V7X + lax.axis_index("c")
    base = wid * tpw
    lane = lax.iota(I32, nl)
    zero = jnp.zeros((nl,), F32)
    c_gelu = 2.0 * math.sqrt(2.0 / math.pi)

    def gather_copy(tab_hbm, job):
        tok = job // jobs_per_tok
        c = (job % jobs_per_tok) % n_chunk
        b = job % PEER_RING
        return pltpu.make_async_copy(tab_hbm.at[idx_v.at[tok, pl.ds(c * rc, rc)]], rows.at[b], sem.at[b])

    def start(job):
        j = job % jobs_per_tok

        @pl.when(j < n_chunk)
        def _():
            gather_copy(u_hbm, job).start()

        @pl.when(j >= n_chunk)
        def _():
            gather_copy(v_hbm, job).start()

    def compute_u(tok, c, b):
        def rg_body(rg, _):
            r0 = rg * 8

            def jbody(j2, accs):
                off0 = j2 * (2 * nl)
                off1 = off0 + nl
                xrow = tok * xs + off0 // LANES
                xl = off0 % LANES
                x0 = plsc.bitcast(x_v[xrow, pl.ds(xl, nl)], BF16)
                x1 = plsc.bitcast(x_v[xrow, pl.ds(xl + nl, nl)], BF16)
                new = []
                for r in range(8):
                    w0 = plsc.bitcast(rows[b, r0 + r, pl.ds(off0, nl)], BF16)
                    w1 = plsc.bitcast(rows[b, r0 + r, pl.ds(off1, nl)], BF16)
                    lo, hi = _unpack_pair(plsc.bitcast(w0 * x0 + w1 * x1, I32))
                    new.append(accs[r] + (lo + hi))
                return tuple(new)

            accs = lax.fori_loop(0, n_lane_blk // 2, jbody, (zero,) * 8)
            for r in range(8):
                p_v[c * rc + r0 + r, :] = accs[r]
            return 0

        lax.fori_loop(0, rc // 8, rg_body, 0)

    def finish_act(tok):
        def eg_body(eg, _):
            e0 = eg * nl
            ridx = e0 + lane
            s = zero
            for l in range(nl):
                s = s + plsc.load_gather(p_v, [ridx, jnp.full((nl,), l, I32)])
            inner = c_gelu * (s + 0.044715 * (s * s * s))
            gl = s / (1.0 + jnp.exp(-inner))
            a = gl * gate_v[tok, pl.ds(e0, nl)]
            bits = lax.bitcast_convert_type(a, I32)
            rnd = bits + jnp.int32(0x7FFF) + (lax.shift_right_logical(bits, 16) & 1)
            hi16 = rnd & jnp.int32(-65536)
            act_v[pl.ds(e0, nl)] = hi16 | lax.shift_right_logical(hi16, 16)
            return 0

        lax.fori_loop(0, n_sel // nl, eg_body, 0)

        def zbody(j, _):
            off = j * nl
            out_v[tok * os_ + off // LANES, pl.ds(off % LANES, nl)] = zero
            return 0

        lax.fori_loop(0, d // nl, zbody, 0, unroll=4)

    def compute_v(tok, c, b):
        def rg_body(rg, _):
            r0 = rg * nl
            splat = [plsc.bitcast(plsc.load_gather(act_v, [jnp.full((nl,), 0, I32) + (c * rc + r0 + r)]), BF16)
                     for r in range(nl)]

            def tree(parts):
                while len(parts) > 1:
                    parts = [parts[i] + parts[i + 1] for i in range(0, len(parts), 2)]
                return parts[0]

            @plsc.parallel_loop(0, n_lane_blk, unroll=2)
            def _(j):
                off = j * nl
                los, his = [], []
                for r in range(0, nl, 2):
                    w0 = plsc.bitcast(rows[b, r0 + r, pl.ds(off, nl)], BF16)
                    w1 = plsc.bitcast(rows[b, r0 + r + 1, pl.ds(off, nl)], BF16)
                    lo, hi = _unpack_pair(plsc.bitcast(w0 * splat[r] + w1 * splat[r + 1], I32))
                    los.append(lo)
                    his.append(hi)
                orow = tok * os_ + off // LANES
                ol = off % LANES
                out_v[orow, pl.ds(ol, nl)] = out_v[orow, pl.ds(ol, nl)] + tree(los)
                out_v[orow + xs, pl.ds(ol, nl)] = out_v[orow + xs, pl.ds(ol, nl)] + tree(his)

            return 0

        lax.fori_loop(0, rc // nl, rg_body, 0)

    def batch_body(bi, _):
        t0 = base + bi * tb
        pltpu.sync_copy(e_hbm.at[pl.ds(t0, tb)], idx_v)
        pltpu.sync_copy(g_hbm.at[pl.ds(t0, tb)], gate_v)
        pltpu.sync_copy(hn_hbm.at[pl.ds(t0 * xs, tb * xs)], x_v)
        for pre in range(PEER_RING - 1):
            start(pre)

        def job_body(job, _):
            @pl.when(job + (PEER_RING - 1) < tb * jobs_per_tok)
            def _():
                start(job + (PEER_RING - 1))

            j = job % jobs_per_tok
            gather_copy(u_hbm, job).wait()
            tok = job // jobs_per_tok
            b = job % PEER_RING

            @pl.when(j < n_chunk)
            def _():
                compute_u(tok, j, b)

            @pl.when(j == n_chunk - 1)
            def _():
                finish_act(tok)

            @pl.when(j >= n_chunk)
            def _():
                compute_v(tok, j - n_chunk, b)

            return 0

        lax.fori_loop(0, tb * jobs_per_tok, job_body, 0)
        pltpu.sync_copy(out_v, out_hbm.at[pl.ds(t0 * os_, tb * os_)])
        return 0

    lax.fori_loop(0, tpw // tb, batch_body, 0)


def _peer_sc(x_pk, experts, gates, u_tab, v_tab):
    d = 2 * u_tab.shape[1]
    n = x_pk.shape[0] * LANES // (d // 2)
    n_sel = experts.shape[1]
    nw = SC_CORES_V7X * SC_SUBCORES_V7X
    tpw = n // nw
    mesh = plsc.VectorSubcoreMesh(core_axis_name="c", subcore_axis_name="s",
                                  num_cores=SC_CORES_V7X, num_subcores=SC_SUBCORES_V7X)
    body = functools.partial(_peer_sc_body, tpw=tpw, d=d, n_sel=n_sel)
    call = pl.kernel(
        body,
        out_type=jax.ShapeDtypeStruct((n * d // LANES, LANES), F32),
        mesh=mesh,
        scratch_types=[pltpu.VMEM((PEER_TOK_BATCH, n_sel), I32),
                       pltpu.VMEM((PEER_TOK_BATCH, n_sel), F32),
                       pltpu.VMEM((PEER_TOK_BATCH * d // 2 // LANES, LANES), I32),
                       pltpu.VMEM((PEER_TOK_BATCH * d // LANES, LANES), F32),
                       pltpu.VMEM((PEER_RING, PEER_ROW_CHUNK, d // 2), I32),
                       pltpu.VMEM((n_sel, SC_LANES_V7X), F32),
                       pltpu.VMEM((n_sel,), I32),
                       pltpu.SemaphoreType.DMA((PEER_RING,))],
        compiler_params=pltpu.CompilerParams(needs_layout_passes=False, use_tc_tiling_on_sc=False),
        name="peer_sc",
    )
    return call(x_pk, experts, gates, u_tab, v_tab)


def kernel(x, norm1_g, w_in, a_re, a_im, log_dt, b_re, b_im, c_re, c_im, d_skip, w_glu, w_ssm_up, w_attn_up,
           w_out, norm2_g, peer_wq, peer_k1, peer_k2, peer_u, peer_v, norm_f_g):
    bsz, seq, d = x.shape
    depth = norm1_g.shape[0]
    chunks = _time_chunks(seq)
    h = x
    for layer in range(depth):
        last = layer + 1 == depth
        kv_w, main_w = _in_weights(w_in[layer], seq)
        s5p = _s5_params(a_re[layer], a_im[layer], log_dt[layer], b_re[layer], b_im[layer], c_re[layer],
                         c_im[layer], d_skip[layer], w_glu[layer], nb=bsz)
        n_state = s5p[4].shape[1]
        wsu = w_ssm_up[layer].astype(BF16)
        wau = w_attn_up[layer].astype(BF16)
        wo = w_out[layer].astype(BF16)
        wq = peer_wq[layer].astype(BF16)
        k1 = peer_k1[layer].astype(BF16)
        k2 = peer_k2[layer].astype(BF16)
        u_pk = _pack_bf16_pairs(peer_u[layer])
        v_pk = _pack_bf16_pairs(peer_v[layer])
        kv_tm = min(ROW_TILE, seq)
        head_len = 2 * DSA_KT
        split_kv = len(chunks) > 1 and chunks[0][1] <= head_len < seq
        k4, vt, ki = _kvproj(h, norm1_g[layer], kv_w, norm1_g[layer], seq=head_len if split_kv else seq,
                             tm=kv_tm, kt=DSA_KT)
        st_re = jnp.zeros((bsz, n_state), F32)
        st_im = jnp.zeros((bsz, n_state), F32)
        outs = []
        routed = ki
        peer_outs = []
        for c, (s0, sc) in enumerate(chunks):
            tm = math.gcd(math.gcd(s0, sc), ROW_TILE)
            if split_kv and s0 + sc > head_len:
                k4, vt, ki = _kvproj(h, norm1_g[layer], kv_w, routed, seq=seq, tm=kv_tm, kt=DSA_KT)
                split_kv = False
            after = (routed, peer_outs[c - SC_LAG] if c >= SC_LAG else ki)
            u, q, qi, wit, gs, ga = _inproj(h, norm1_g[layer], main_w, after, s0=s0, sc=sc, tm=tm)
            d_ssm = u.shape[-1]
            u_tb = u.transpose(1, 0, 2).reshape(sc * bsz, d_ssm)
            y_tb, st_re, st_im = _s5(u_tb, st_re, st_im, s5p, nb=bsz, tc=64)
            ys = y_tb.reshape(sc, bsz, d_ssm).transpose(1, 0, 2)
            ya = _dsa(q, qi, wit, ki, k4, vt, s0=s0, seq_total=seq, tq=DSA_TQ, kt=DSA_KT)
            hm, x_pk, qp = _merge(h, ys, ya, gs, ga, wsu, wau, wo, norm2_g[layer], wq, s0=s0, tm=tm)
            nt = bsz * sc
            e_t, g_t = _route(qp.reshape(nt, -1), k1, k2, tt=256)
            routed = e_t
            po = _peer_sc(x_pk.reshape(-1, LANES), e_t.T, g_t.T, u_pk, v_pk)
            peer_outs.append(po)
            hm2 = hm.reshape(nt, d)
            o = _final(hm2, po, norm_f_g, tm=tm) if last else hm2 + po.reshape(nt, d)
            outs.append(o.reshape(bsz, sc, d))
        h = jnp.concatenate(outs, axis=1)
    return h
```

```python
import functools
import math

import numpy as np
import jax
import jax.numpy as jnp
from jax import lax
from jax.experimental import pallas as pl
from jax.experimental.pallas import tpu as pltpu
from jax.experimental.pallas import tpu_sc as plsc

F32 = jnp.float32
BF16 = jnp.bfloat16
I32 = jnp.int32

SSM_GROUP = 16
SSM_STATE = 64
ATTN_HEADS = 8
ATTN_KV_HEADS = 2
HEAD_DIM = 64
IDX_HEADS = 8
IDX_DIM = 32
TOPK_MAX = 256
ROPE_THETA = 10000.0
NEG_BIG = -1e30
PEER_HEADS = 8
PEER_KEYS = 128
PEER_KEY_DIM = 128
PEER_TOPK = 16
NORM_EPS = 1e-6

TIME_SPLIT_32NDS = (1, 3, 4, 4, 4, 4, 4, 3, 2, 2, 1)
SC_LAG = 3
ROW_TILE = 512
DSA_TQ = 128
DSA_KT = 256
LANES = 128
INT_MIN = -(2 ** 31)
VMEM_LIMIT = 56 * 1024 * 1024


def _time_chunks(seq):
    unit = seq // 32
    if seq % 32 == 0 and unit % DSA_TQ == 0:
        sizes = [f * unit for f in TIME_SPLIT_32NDS]
    else:
        step = min(ROW_TILE, seq)
        sizes = [step] * (seq // step)
    assert sum(sizes) == seq
    starts = np.cumsum([0] + sizes[:-1]).tolist()
    return list(zip(starts, sizes))


def _cparams(sem):
    return pltpu.CompilerParams(dimension_semantics=sem, vmem_limit_bytes=VMEM_LIMIT)


def _gelu_tanh(x):
    return 0.5 * x * (1.0 + jnp.tanh(math.sqrt(2.0 / math.pi) * (x + 0.044715 * (x * x * x))))


def _sigmoid(x):
    return 1.0 / (1.0 + jnp.exp(-x))


def _rms(x, g):
    return x * lax.rsqrt(jnp.mean(x * x, axis=-1, keepdims=True) + NORM_EPS) * g


def _rot_cols(w, hd):
    d, n = w.shape
    w3 = w.reshape(d, n // hd, hd)
    half = hd // 2
    return jnp.concatenate([-w3[..., half:], w3[..., :half]], axis=-1).reshape(d, n)


def _rope_full(seq, hd, heads):
    pos = jnp.arange(seq, dtype=F32)
    inv = ROPE_THETA ** (-jnp.arange(0, hd, 2, dtype=F32) / hd)
    ang = pos[:, None] * inv[None, :]
    c = jnp.concatenate([jnp.cos(ang), jnp.cos(ang)], axis=-1)
    s = jnp.concatenate([jnp.sin(ang), jnp.sin(ang)], axis=-1)
    return jnp.tile(c, (1, heads)), jnp.tile(s, (1, heads))


def _in_weights(w_in, seq):
    d = w_in.shape[0]
    d_ssm = d // 2
    d_q = ATTN_HEADS * HEAD_DIM
    d_kv = ATTN_KV_HEADS * HEAD_DIM
    d_qi = IDX_HEADS * IDX_DIM
    splits = (d_ssm, d_q, d_kv, d_kv, d_qi, IDX_DIM, IDX_HEADS, d, d)
    offs = np.cumsum(splits)[:-1].tolist()
    wu, wq, wk, wv, wqi, wki, wwi, wgs, wga = jnp.split(w_in, offs, axis=1)
    pad = jnp.zeros((d, 128 - IDX_DIM), F32)
    cq, sq = _rope_full(seq, HEAD_DIM, ATTN_HEADS)
    ck, sk = _rope_full(seq, HEAD_DIM, ATTN_KV_HEADS)
    cqi, sqi = _rope_full(seq, IDX_DIM, IDX_HEADS)
    cki, ski = _rope_full(seq, IDX_DIM, 1)
    tpad = jnp.zeros((seq, 128 - IDX_DIM), F32)
    kv = dict(
        w=jnp.concatenate([wk, wki, pad], axis=1).astype(BF16),
        wvt=wv.T.astype(BF16),
        wr=jnp.concatenate([_rot_cols(wk, HEAD_DIM), _rot_cols(wki, IDX_DIM), pad], axis=1).astype(BF16),
        cs=jnp.concatenate([ck, cki, tpad], axis=1), sn=jnp.concatenate([sk, ski, tpad], axis=1))
    main = dict(
        w=jnp.concatenate([wu, wq, wqi, wgs, wga], axis=1).astype(BF16),
        wr=jnp.concatenate([_rot_cols(wq, HEAD_DIM), _rot_cols(wqi, IDX_DIM)], axis=1).astype(BF16),
        cs=jnp.concatenate([cq, cqi], axis=1), sn=jnp.concatenate([sq, sqi], axis=1),
        wwit=wwi.T.astype(BF16))
    return kv, main


def _kvproj_kernel(x_ref, g_ref, w_ref, wr_ref, wvt_ref, cs_ref, sn_ref, after_ref, k_ref, vt_ref, ki_ref,
                   *, d_kv, kt):
    del after_ref
    xb = _rms(x_ref[0], g_ref[...]).astype(BF16)

    def mm(ref, lo, n):
        return jnp.dot(xb, ref[:, lo:lo + n], preferred_element_type=F32)

    k = mm(w_ref, 0, d_kv) * cs_ref[:, :d_kv] + mm(wr_ref, 0, d_kv) * sn_ref[:, :d_kv]
    for n in range(ATTN_KV_HEADS):
        k_ref[0, n] = k[:, n * HEAD_DIM:(n + 1) * HEAD_DIM].astype(BF16)
    vt = lax.dot_general(wvt_ref[...], xb, (((1,), (1,)), ((), ())), preferred_element_type=F32)
    for n in range(ATTN_KV_HEADS):
        for j in range(vt.shape[1] // kt):
            vt_ref[0, n, j] = vt[n * HEAD_DIM:(n + 1) * HEAD_DIM, j * kt:(j + 1) * kt].astype(BF16)
    kiw = (mm(w_ref, d_kv, 128) * cs_ref[:, d_kv:d_kv + 128]
           + mm(wr_ref, d_kv, 128) * sn_ref[:, d_kv:d_kv + 128])
    ki_ref[0] = kiw[:, :IDX_DIM].astype(BF16)


def _kvproj(x, norm_g, kv, after, seq, tm, kt):
    bsz, _, d = x.shape
    d_kv = ATTN_KV_HEADS * HEAD_DIM
    full = lambda a: pl.BlockSpec(a.shape, lambda s, b: (0,) * a.ndim)
    g = norm_g.reshape(1, d)
    ncs = kv["cs"].shape[1]
    return pl.pallas_call(
        functools.partial(_kvproj_kernel, d_kv=d_kv, kt=kt),
        grid=(seq // tm, bsz),
        in_specs=[pl.BlockSpec((1, tm, d), lambda s, b: (b, s, 0)), full(g), full(kv["w"]), full(kv["wr"]),
                  full(kv["wvt"]),
                  pl.BlockSpec((tm, ncs), lambda s, b: (s, 0)), pl.BlockSpec((tm, ncs), lambda s, b: (s, 0)),
                  pl.BlockSpec(memory_space=pl.ANY)],
        out_specs=[pl.BlockSpec((1, ATTN_KV_HEADS, tm, HEAD_DIM), lambda s, b: (b, 0, s, 0)),
                   pl.BlockSpec((1, ATTN_KV_HEADS, tm // kt, HEAD_DIM, kt), lambda s, b: (b, 0, s, 0, 0)),
                   pl.BlockSpec((1, tm, IDX_DIM), lambda s, b: (b, s, 0))],
        out_shape=[jax.ShapeDtypeStruct((bsz, ATTN_KV_HEADS, seq, HEAD_DIM), BF16),
                   jax.ShapeDtypeStruct((bsz, ATTN_KV_HEADS, seq // kt, HEAD_DIM, kt), BF16),
                   jax.ShapeDtypeStruct((bsz, seq, IDX_DIM), BF16)],
        compiler_params=_cparams(("arbitrary", "arbitrary")),
        name="kvproj",
    )(x, g, kv["w"], kv["wr"], kv["wvt"], kv["cs"], kv["sn"], after)


def _inproj_kernel(x_ref, g_ref, w_ref, wr_ref, cs_ref, sn_ref, wwit_ref, after_tc_ref, after_sc_ref,
                   u_ref, q_ref, qi_ref, wit_ref, gs_ref, ga_ref, *, d_ssm, d_q, d_qi, d_model, q_scale, wi_scale):
    del after_tc_ref, after_sc_ref
    xb = _rms(x_ref[0], g_ref[...]).astype(BF16)

    def mm(ref, lo, n):
        return jnp.dot(xb, ref[:, lo:lo + n], preferred_element_type=F32)

    o = 0
    u_ref[0] = mm(w_ref, o, d_ssm).astype(BF16)
    o += d_ssm
    q = mm(w_ref, o, d_q) * cs_ref[:, :d_q] + mm(wr_ref, 0, d_q) * sn_ref[:, :d_q]
    q_ref[0] = (q * q_scale).astype(BF16)
    o += d_q
    qi = mm(w_ref, o, d_qi) * cs_ref[:, d_q:d_q + d_qi] + mm(wr_ref, d_q, d_qi) * sn_ref[:, d_q:d_q + d_qi]
    qi_ref[0] = qi.astype(BF16)
    o += d_qi
    gs_ref[0] = _sigmoid(mm(w_ref, o, d_model)).astype(BF16)
    o += d_model
    ga_ref[0] = _sigmoid(mm(w_ref, o, d_model)).astype(BF16)
    wit_ref[0] = lax.dot_general(wwit_ref[...], xb, (((1,), (1,)), ((), ())),
                                 preferred_element_type=F32) * wi_scale


def _inproj(x, norm_g, main, after, s0, sc, tm):
    bsz, _, d = x.shape
    d_ssm = d // 2
    d_q = ATTN_HEADS * HEAD_DIM
    d_qi = IDX_HEADS * IDX_DIM
    i0 = s0 // tm
    kern = functools.partial(
        _inproj_kernel, d_ssm=d_ssm, d_q=d_q, d_qi=d_qi, d_model=d,
        q_scale=HEAD_DIM ** -0.5 * math.log2(math.e), wi_scale=(IDX_HEADS ** -0.5) * (IDX_DIM ** -0.5))
    tok = lambda n: pl.BlockSpec((1, tm, n), lambda s, b: (b, s, 0))
    full = lambda a: pl.BlockSpec(a.shape, lambda s, b: (0,) * a.ndim)
    g = norm_g.reshape(1, d)
    ncs = main["cs"].shape[1]
    outs = [(d_ssm, BF16), (d_q, BF16), (d_qi, BF16)]
    return pl.pallas_call(
        kern,
        grid=(sc // tm, bsz),
        in_specs=[pl.BlockSpec((1, tm, d), lambda s, b: (b, s + i0, 0)), full(g), full(main["w"]), full(main["wr"]),
                  pl.BlockSpec((tm, ncs), lambda s, b: (s + i0, 0)),
                  pl.BlockSpec((tm, ncs), lambda s, b: (s + i0, 0)), full(main["wwit"]),
                  pl.BlockSpec(memory_space=pl.ANY), pl.BlockSpec(memory_space=pl.ANY)],
        out_specs=[tok(n) for n, _ in outs] + [pl.BlockSpec((1, IDX_HEADS, tm), lambda s, b: (b, 0, s)),
                                                tok(d), tok(d)],
        out_shape=[jax.ShapeDtypeStruct((bsz, sc, n), dt) for n, dt in outs]
        + [jax.ShapeDtypeStruct((bsz, IDX_HEADS, sc), F32),
           jax.ShapeDtypeStruct((bsz, sc, d), BF16), jax.ShapeDtypeStruct((bsz, sc, d), BF16)],
        compiler_params=_cparams(("arbitrary", "arbitrary")),
        name="inproj",
    )(x, g, main["w"], main["wr"], main["cs"], main["sn"], main["wwit"], *after)


def _s5_kernel(u_ref, sre_in, sim_in, bre_ref, bim_ref, cre_ref, cim_ref, are_ref, aim_ref, dsk_ref, wglu_ref,
               y_ref, st_re, st_im, sre, sim, *, tc, nb, lane_chunk):
    @pl.when(pl.program_id(0) == 0)
    def _():
        st_re[...] = sre_in[...]
        st_im[...] = sim_in[...]

    u = u_ref[...]
    n_half = bre_ref.shape[0]
    hin = bre_ref.shape[1]
    hst = bre_ref.shape[2]
    for h in range(n_half):
        uh = u[:, h * hin:(h + 1) * hin]
        sre[:, h * hst:(h + 1) * hst] = jnp.dot(uh, bre_ref[h], preferred_element_type=F32)
        sim[:, h * hst:(h + 1) * hst] = jnp.dot(uh, bim_ref[h], preferred_element_type=F32)

    n_state = sre.shape[1]
    for c in range(n_state // lane_chunk):
        cols = slice(c * lane_chunk, (c + 1) * lane_chunk)
        ar = are_ref[:, cols]
        ai = aim_ref[:, cols]

        def step(t, carry, cols=cols, ar=ar, ai=ai):
            sr, si = carry
            r0 = pl.multiple_of(t * nb, nb)
            nr = ar * sr - ai * si + sre[pl.ds(r0, nb), cols]
            ni = ar * si + ai * sr + sim[pl.ds(r0, nb), cols]
            sre[pl.ds(r0, nb), cols] = nr
            sim[pl.ds(r0, nb), cols] = ni
            return nr, ni

        sr, si = lax.fori_loop(0, tc, step, (st_re[:, cols], st_im[:, cols]), unroll=4)
        st_re[:, cols] = sr
        st_im[:, cols] = si

    ys = []
    for h in range(n_half):
        srh = sre[:, h * hst:(h + 1) * hst].astype(BF16)
        sih = sim[:, h * hst:(h + 1) * hst].astype(BF16)
        ys.append(jnp.dot(srh, cre_ref[h], preferred_element_type=F32)
                  - jnp.dot(sih, cim_ref[h], preferred_element_type=F32))
    y = jnp.concatenate(ys, axis=-1) + dsk_ref[...] * u.astype(F32)
    y = _gelu_tanh(y)
    gate = jnp.dot(y.astype(BF16), wglu_ref[...], preferred_element_type=F32)
    y_ref[...] = (y * _sigmoid(gate)).astype(BF16)


def _s5_params(a_re, a_im, log_dt, b_re, b_im, c_re, c_im, d_skip, w_glu, nb):
    groups = a_re.shape[0]
    d_ssm = groups * SSM_GROUP
    n_state = groups * SSM_STATE
    lam = lax.complex(a_re, a_im)
    dt = jnp.exp(log_dt)[:, None]
    a_bar = jnp.exp(lam * dt)
    b_bar = ((a_bar - 1.0) / lam)[..., None] * lax.complex(b_re, b_im)
    gh = min(groups, 256 // SSM_GROUP)
    n_half = groups // gh
    eye = jnp.eye(gh, dtype=F32)

    def bmat(bb):
        b4 = bb.reshape(n_half, gh, SSM_STATE, SSM_GROUP)
        return jnp.einsum('hgpc,gk->hgckp', b4, eye).reshape(n_half, gh * SSM_GROUP, gh * SSM_STATE)

    def cmat(cc):
        c4 = cc.reshape(n_half, gh, SSM_GROUP, SSM_STATE)
        return jnp.einsum('hgcp,gk->hgpkc', c4, eye).reshape(n_half, gh * SSM_STATE, gh * SSM_GROUP)

    return (bmat(jnp.real(b_bar)).astype(BF16), bmat(jnp.imag(b_bar)).astype(BF16),
            cmat(c_re).astype(BF16), cmat(c_im).astype(BF16),
            jnp.broadcast_to(jnp.real(a_bar).reshape(1, n_state), (nb, n_state)),
            jnp.broadcast_to(jnp.imag(a_bar).reshape(1, n_state), (nb, n_state)),
            d_skip.reshape(1, d_ssm), w_glu.astype(BF16))


def _s5(u_tb, st_re, st_im, params, nb, tc):
    rows, d_ssm = u_tb.shape
    n_state = st_re.shape[1]
    blk = tc * nb
    full = lambda a: pl.BlockSpec(a.shape, lambda i: (0,) * a.ndim)
    st_spec = pl.BlockSpec((nb, n_state), lambda i: (0, 0))
    kern = functools.partial(_s5_kernel, tc=tc, nb=nb, lane_chunk=512)
    return pl.pallas_call(
        kern,
        grid=(rows // blk,),
        in_specs=[pl.BlockSpec((blk, d_ssm), lambda i: (i, 0)), st_spec, st_spec] + [full(p) for p in params],
        out_specs=[pl.BlockSpec((blk, d_ssm), lambda i: (i, 0)), st_spec, st_spec],
        out_shape=[jax.ShapeDtypeStruct((rows, d_ssm), BF16),
                   jax.ShapeDtypeStruct((nb, n_state), F32), jax.ShapeDtypeStruct((nb, n_state), F32)],
        scratch_shapes=[pltpu.VMEM((blk, n_state), F32), pltpu.VMEM((blk, n_state), F32)],
        compiler_params=_cparams(("arbitrary",)),
        name="s5",
    )(u_tb, st_re, st_im, *params)


PART_ROWS = 32


def _dsa_kernel(qi_ref, wit_ref, q_ref, ki_ref, k_ref, vt_ref, o_ref, key_s, bias_s, lg_s,
                *, qb0, tq, kt, sub, topk, seq_bits):
    qb = pl.program_id(1) + qb0
    nkt = ((qb * tq + tq + sub * kt - 1) // (sub * kt)) * sub
    q_pos = qb * tq + lax.broadcasted_iota(I32, (1, tq), 1)
    k_eff = jnp.minimum(topk, q_pos + 1).astype(F32)

    qi = qi_ref[0]
    wit = wit_ref[0]
    qipair = [jnp.concatenate([qi[:, (2 * hp) * IDX_DIM:(2 * hp + 1) * IDX_DIM],
                               qi[:, (2 * hp + 1) * IDX_DIM:(2 * hp + 2) * IDX_DIM]], axis=0)
              for hp in range(IDX_HEADS // 2)]

    def key_pos(t):
        return t * kt + lax.broadcasted_iota(I32, (kt, tq), 0)

    def score_tile(t2, _):
        for hf in range(sub):
            r0 = pl.multiple_of((t2 * sub + hf) * kt, kt)
            ki_t = ki_ref[0, pl.ds(r0, kt), :]
            for hp in range(IDX_HEADS // 2):
                lg_s[hf, hp] = lax.dot_general(ki_t, qipair[hp], (((1,), (1,)), ((), ())),
                                               preferred_element_type=F32)
        for hf in range(sub):
            t = t2 * sub + hf
            r0 = pl.multiple_of(t * kt, kt)
            sc = jnp.zeros((kt, tq), F32)
            for hp in range(IDX_HEADS // 2):
                rel = lg_s[hf, hp]
                sc = sc + jnp.maximum(rel[:, :tq], 0.0) * wit[2 * hp:2 * hp + 1, :]
                sc = sc + jnp.maximum(rel[:, tq:], 0.0) * wit[2 * hp + 1:2 * hp + 2, :]
            bits = lax.bitcast_convert_type(sc, I32)
            key = jnp.where(bits < 0, bits ^ jnp.int32(0x7FFFFFFF), bits)
            key = jnp.where(key_pos(t) <= q_pos, key, jnp.int32(INT_MIN))
            key_s[pl.ds(r0, kt), :] = key
        return 0

    lax.fori_loop(0, nkt // sub, score_tile, 0)

    def count(pred_fn):
        def body(t, acc):
            r0 = pl.multiple_of(t * kt, kt)
            m = pred_fn(key_s[pl.ds(r0, kt), :], t)
            ones = jnp.where(m, 1.0, 0.0).reshape(kt // PART_ROWS, PART_ROWS, tq)
            return acc + jnp.sum(ones, axis=0)
        acc = lax.fori_loop(0, nkt, body, jnp.zeros((PART_ROWS, tq), F32))
        return jnp.sum(acc, axis=0, keepdims=True)

    def bit_step(i, u):
        bit = jnp.left_shift(jnp.int32(1), 31 - i)
        cand_u = u | bit
        cand_s = cand_u ^ jnp.int32(INT_MIN)
        cnt = count(lambda kk, t: kk >= cand_s)
        return jnp.where(cnt >= k_eff, cand_u, u)

    u_thr = lax.fori_loop(0, 32, bit_step, jnp.zeros((1, tq), I32))
    thr = u_thr ^ jnp.int32(INT_MIN)

    cnt_ge = count(lambda kk, t: kk >= thr)
    cnt_gt = count(lambda kk, t: kk > thr)
    need_eq = k_eff - cnt_gt
    has_tie = jnp.max(cnt_ge - k_eff) > 0.0

    def tie_cut():
        def pos_step(i, c):
            bit = jnp.left_shift(jnp.int32(1), seq_bits - 1 - i)
            cand = c | bit
            cnt = count(lambda kk, t: (kk == thr) & (key_pos(t) < cand))
            return jnp.where(cnt < need_eq, cand, c)
        return lax.fori_loop(0, seq_bits, pos_step, jnp.zeros((1, tq), I32))

    cut = lax.cond(has_tie, tie_cut, lambda: jnp.full((1, tq), 2 ** seq_bits, I32))

    def bias_tile(t, _):
        r0 = pl.multiple_of(t * kt, kt)
        key = key_s[pl.ds(r0, kt), :]
        sel = (key > thr) | ((key == thr) & (key_pos(t) <= cut))
        bias_s[pl.ds(r0, kt), :] = jnp.where(sel, 0.0, NEG_BIG)
        return 0

    lax.fori_loop(0, nkt, bias_tile, 0)

    q = q_ref[0]
    grp = ATTN_HEADS // ATTN_KV_HEADS
    pairs_per_kv = grp // 2
    n_unit = ATTN_KV_HEADS * pairs_per_kv
    wq = 2 * tq
    qpair = [jnp.concatenate([q[:, (2 * u) * HEAD_DIM:(2 * u + 1) * HEAD_DIM],
                              q[:, (2 * u + 1) * HEAD_DIM:(2 * u + 2) * HEAD_DIM]], axis=0)
             for u in range(n_unit)]

    def col_reduce(x, op):
        part = op(x.reshape(kt // PART_ROWS, PART_ROWS, wq), axis=0)
        return op(part, axis=0, keepdims=True)

    def attn_tile(t, carry):
        ms, ls, accs = list(carry[0]), list(carry[1]), list(carry[2])
        for hf in range(sub):
            r0 = pl.multiple_of((t * sub + hf) * kt, kt)
            bias = bias_s[pl.ds(r0, kt), :]
            bias2 = jnp.concatenate([bias, bias], axis=1)
            for u in range(n_unit):
                k_t = k_ref[0, u // pairs_per_kv, pl.ds(r0, kt), :]
                lg_s[hf, u] = lax.dot_general(k_t, qpair[u], (((1,), (1,)), ((), ())),
                                              preferred_element_type=F32) + bias2
        for hf in range(sub):
            for u in range(n_unit):
                lg = lg_s[hf, u]
                m_new = jnp.maximum(ms[u], col_reduce(lg, jnp.max))
                p = jnp.exp2(lg - m_new)
                alpha = jnp.exp2(ms[u] - m_new)
                v_t = vt_ref[0, u // pairs_per_kv, t * sub + hf]
                ls[u] = alpha * ls[u] + col_reduce(p, jnp.sum)
                accs[u] = alpha * accs[u] + jnp.dot(v_t, p.astype(BF16), preferred_element_type=F32)
                ms[u] = m_new
        return tuple(ms), tuple(ls), tuple(accs)

    init = (tuple(jnp.full((1, wq), NEG_BIG, F32) for _ in range(n_unit)),
            tuple(jnp.zeros((1, wq), F32) for _ in range(n_unit)),
            tuple(jnp.zeros((HEAD_DIM, wq), F32) for _ in range(n_unit)))
    _, ls, accs = lax.fori_loop(0, nkt // sub, attn_tile, init)
    for n in range(ATTN_KV_HEADS):
        o_ref[0, 0, n] = jnp.concatenate([accs[n * pairs_per_kv + pg] / ls[n * pairs_per_kv + pg]
                                          for pg in range(pairs_per_kv)], axis=1).astype(BF16)


def _dsa(q, qi, wit, ki, k4, vt, s0, seq_total, tq, kt):
    bsz, sc, _ = q.shape
    seq = ki.shape[1]
    topk = min(TOPK_MAX, seq_total // 4)
    nqb = sc // tq
    grp = ATTN_HEADS // ATTN_KV_HEADS
    seq_bits = int(math.log2(seq))
    assert 2 ** seq_bits == seq
    sub = 2 if seq % (2 * kt) == 0 else 1
    n_unit = ATTN_HEADS // 2
    assert IDX_HEADS // 2 <= n_unit
    kern =functools.partial(_dsa_kernel, qb0=s0 // tq, tq=tq, kt=kt, sub=sub, topk=topk, seq_bits=seq_bits)
    o_t = pl.pallas_call(
        kern,
        grid=(bsz, nqb),
        in_specs=[pl.BlockSpec((1, tq, IDX_HEADS * IDX_DIM), lambda b, j: (b, j, 0)),
                  pl.BlockSpec((1, IDX_HEADS, tq), lambda b, j: (b, 0, j)),
                  pl.BlockSpec((1, tq, ATTN_HEADS * HEAD_DIM), lambda b, j: (b, j, 0)),
                  pl.BlockSpec((1, seq, IDX_DIM), lambda b, j: (b, 0, 0)),
                  pl.BlockSpec((1, ATTN_KV_HEADS, seq, HEAD_DIM), lambda b, j: (b, 0, 0, 0)),
                  pl.BlockSpec((1, ATTN_KV_HEADS, seq // kt, HEAD_DIM, kt), lambda b, j: (b, 0, 0, 0, 0))],
        out_specs=pl.BlockSpec((1, 1, ATTN_KV_HEADS, HEAD_DIM, grp * tq), lambda b, j: (b, j, 0, 0, 0)),
        out_shape=jax.ShapeDtypeStruct((bsz, nqb, ATTN_KV_HEADS, HEAD_DIM, grp * tq), BF16),
        scratch_shapes=[pltpu.VMEM((seq, tq), I32), pltpu.VMEM((seq, tq), F32),
                        pltpu.VMEM((sub, n_unit, kt, 2 * tq), F32)],
        compiler_params=_cparams(("arbitrary", "arbitrary")),
        name="dsa",
    )(qi, wit, q, ki, k4, vt)
    o = o_t.reshape(bsz, nqb, ATTN_KV_HEADS, HEAD_DIM, grp, tq).transpose(0, 1, 5, 2, 4, 3)
    return o.reshape(bsz, sc, ATTN_HEADS * HEAD_DIM)


def _merge_kernel(x_ref, ys_ref, ya_ref, gs_ref, ga_ref, wsu_ref, wau_ref, wout_ref, g2_ref, wq_ref,
                  h_ref, hn_ref, qp_ref):
    ms = jnp.dot(ys_ref[0], wsu_ref[...], preferred_element_type=F32)
    ma = jnp.dot(ya_ref[0], wau_ref[...], preferred_element_type=F32)
    merged = gs_ref[0].astype(F32) * ms + ga_ref[0].astype(F32) * ma
    h = x_ref[0] + jnp.dot(merged.astype(BF16), wout_ref[...], preferred_element_type=F32)
    h_ref[0] = h
    hb = _rms(h, g2_ref[...]).astype(BF16)
    words = _pack_words(hb)
    half = words.shape[1]
    n_seg = half // LANES
    tm = words.shape[0]
    for p in range(n_seg):
        hn_ref[0, pl.ds(p, tm, stride=n_seg), :] = words[:, p * LANES:(p + 1) * LANES]
    qp_ref[0] = jnp.dot(hb, wq_ref[...], preferred_element_type=F32).astype(BF16)


def _merge(x, ys, ya, gs, ga, wsu, wau, wo, norm2_g, wq, s0, tm):
    bsz, sc, _ = ya.shape
    d = x.shape[2]
    i0 = s0 // tm
    loc = lambda a: pl.BlockSpec((1, tm, a.shape[2]), lambda b, i: (b, i, 0))
    full = lambda a: pl.BlockSpec(a.shape, lambda b, i: (0,) * a.ndim)
    g2 = norm2_g.reshape(1, d)
    nq = wq.shape[1]
    n_seg = d // 2 // LANES
    out = lambda n: pl.BlockSpec((1, tm, n), lambda b, i: (b, i, 0))
    return pl.pallas_call(
        _merge_kernel,
        grid=(bsz, sc // tm),
        in_specs=[pl.BlockSpec((1, tm, d), lambda b, i: (b, i + i0, 0)), loc(ys), loc(ya), loc(gs), loc(ga),
                  full(wsu), full(wau), full(wo), full(g2), full(wq)],
        out_specs=[out(d), pl.BlockSpec((1, tm * n_seg, LANES), lambda b, i: (b, i, 0)), out(nq)],
        out_shape=[jax.ShapeDtypeStruct((bsz, sc, d), F32), jax.ShapeDtypeStruct((bsz, sc * n_seg, LANES), I32),
                   jax.ShapeDtypeStruct((bsz, sc, nq), BF16)],
        compiler_params=_cparams(("arbitrary", "arbitrary")),
        name="merge",
    )(x, ys, ya, gs, ga, wsu, wau, wo, g2, wq)


def _cand_layout():
    blocks = []
    blocks.append((0, 16, 16))
    for i in range(1, 8):
        blocks.append((i, 8, PEER_TOPK // (i + 1)))
    blocks.append((None, 8, 8))
    return blocks


def _top_rows(s, order, payload, k):
    big = jnp.float32(3e38)
    vals, pays = [], []
    for _ in range(k):
        m = jnp.max(s, axis=0, keepdims=True)
        o = jnp.min(jnp.where(s == m, order, big), axis=0, keepdims=True)
        hit = order == o
        pays.append(o if payload is order else jnp.min(jnp.where(hit, payload, big), axis=0, keepdims=True))
        vals.append(m)
        s = jnp.where(hit, -jnp.inf, s)
    return jnp.concatenate(vals, axis=0), jnp.concatenate(pays, axis=0)


def _route_kernel(qp_ref, k1_ref, k2_ref, e_ref, g_ref, *, tt):
    qp = qp_ref[...]
    kd = PEER_KEY_DIM
    rows_k = lax.broadcasted_iota(I32, (PEER_KEYS, tt), 0).astype(F32)
    for h in range(PEER_HEADS):
        q1 = qp[:, (2 * h) * kd:(2 * h + 1) * kd]
        q2 = qp[:, (2 * h + 1) * kd:(2 * h + 2) * kd]
        s1 = lax.dot_general(k1_ref[h], q1, (((1,), (1,)), ((), ())), preferred_element_type=F32)
        s2 = lax.dot_general(k2_ref[h], q2, (((1,), (1,)), ((), ())), preferred_element_type=F32)
        v1, i1 = _top_rows(s1, rows_k, rows_k, PEER_TOPK)
        v2, i2 = _top_rows(s2, rows_k, rows_k, PEER_TOPK)
        cs, ce, co = [], [], []
        for i, rows, valid in _cand_layout():
            r = lax.broadcasted_iota(I32, (rows, tt), 0).astype(F32)
            if i is None:
                val = v1[8:16] + v2[0:1]
                eid = i1[8:16] * PEER_KEYS + i2[0:1]
                flat = (r + 8.0) * PEER_TOPK
            else:
                val = v1[i:i + 1] + v2[0:rows]
                eid = i1[i:i + 1] * PEER_KEYS + i2[0:rows]
                flat = r + float(i * PEER_TOPK)
                if valid < rows:
                    val = jnp.where(r < float(valid), val, -jnp.inf)
            cs.append(val)
            ce.append(eid)
            co.append(flat)
        cand = jnp.concatenate(cs, axis=0)
        top_s, top_e = _top_rows(cand, jnp.concatenate(co, axis=0), jnp.concatenate(ce, axis=0), PEER_TOPK)
        p = jnp.exp(top_s - top_s[0:1])
        gates = p / jnp.sum(p, axis=0, keepdims=True)
        e_ref[h * PEER_TOPK:(h + 1) * PEER_TOPK, :] = top_e.astype(I32)
        g_ref[h * PEER_TOPK:(h + 1) * PEER_TOPK, :] = gates


def _route(qp, k1, k2, tt):
    n, nq = qp.shape
    n_sel = PEER_HEADS * PEER_TOPK
    full = lambda a: pl.BlockSpec(a.shape, lambda i: (0,) * a.ndim)
    return pl.pallas_call(
        functools.partial(_route_kernel, tt=tt),
        grid=(n // tt,),
        in_specs=[pl.BlockSpec((tt, nq), lambda i: (i, 0)), full(k1), full(k2)],
        out_specs=[pl.BlockSpec((n_sel, tt), lambda i: (0, i)), pl.BlockSpec((n_sel, tt), lambda i: (0, i))],
        out_shape=[jax.ShapeDtypeStruct((n_sel, n), I32), jax.ShapeDtypeStruct((n_sel, n), F32)],
        compiler_params=_cparams(("arbitrary",)),
        name="route",
    )(qp, k1, k2)


def _final_kernel(h_ref, p_ref, g_ref, o_ref):
    tm, d = h_ref.shape
    n_seg = d // LANES
    p = jnp.concatenate([p_ref[pl.ds(k, tm, stride=n_seg), :] for k in range(n_seg)], axis=1)
    o_ref[...] = _rms(h_ref[...] + p, g_ref[...])


def _final(h, p, g, tm):
    n, d = h.shape
    row = pl.BlockSpec((tm, d), lambda i: (i, 0))
    return pl.pallas_call(
        _final_kernel,
        grid=(n // tm,),
        in_specs=[row, pl.BlockSpec((tm * (d // LANES), LANES), lambda i: (i, 0)),
                  pl.BlockSpec((1, d), lambda i: (0, 0))],
        out_specs=row,
        out_shape=jax.ShapeDtypeStruct((n, d), F32),
        compiler_params=_cparams(("arbitrary",)),
        name="final",
    )(h, p, g.reshape(1, d))


SC_CORES_V7X = 2
SC_SUBCORES_V7X = 16
SC_LANES_V7X = 16
PEER_TOK_BATCH = 32
PEER_ROW_CHUNK = 32
PEER_RING = 4


def _pack_words(x):
    bits = lax.bitcast_convert_type(x.astype(BF16).astype(F32), I32)
    half = bits.shape[1] // 2
    return (bits[:, half:] & jnp.int32(-65536)) | lax.shift_right_logical(bits[:, :half], 16)


def _pack_kernel(t_ref, o_ref):
    o_ref[...] = _pack_words(t_ref[...])


def _pack_bf16_pairs(t, rows=1024):
    e, d = t.shape
    return pl.pallas_call(
        _pack_kernel,
        grid=(e // rows,),
        in_specs=[pl.BlockSpec((rows, d), lambda i: (i, 0))],
        out_specs=pl.BlockSpec((rows, d // 2), lambda i: (i, 0)),
        out_shape=jax.ShapeDtypeStruct((e, d // 2), I32),
        compiler_params=_cparams(("arbitrary",)),
        name="pack_table",
    )(t)


def _unpack_pair(w):
    lo = lax.bitcast_convert_type(jnp.left_shift(w, 16), F32)
    hi = lax.bitcast_convert_type(w & jnp.int32(-65536), F32)
    return lo, hi


def _peer_sc_body(hn_hbm, e_hbm, g_hbm, u_hbm, v_hbm, out_hbm,
                  idx_v, gate_v, x_v, out_v, rows, p_v, act_v, sem, *, tpw, d, n_sel):
    nl = SC_LANES_V7X
    tb = PEER_TOK_BATCH
    rc = PEER_ROW_CHUNK
    n_chunk = n_sel // rc
    jobs_per_tok = 2 * n_chunk
    half = d // 2
    n_lane_blk = half // nl
    xs = half // LANES
    os_ = d // LANES
    wid =lax.axis_index("s") * SC_CORES_V7X + lax.axis_index("c")
    base = wid * tpw
    lane = lax.iota(I32, nl)
    zero = jnp.zeros((nl,), F32)
    c_gelu = 2.0 * math.sqrt(2.0 / math.pi)

    def gather_copy(tab_hbm, job):
        tok = job // jobs_per_tok
        c = (job % jobs_per_tok) % n_chunk
        b = job % PEER_RING
        return pltpu.make_async_copy(tab_hbm.at[idx_v.at[tok, pl.ds(c * rc, rc)]], rows.at[b], sem.at[b])

    def start(job):
        j = job % jobs_per_tok

        @pl.when(j < n_chunk)
        def _():
            gather_copy(u_hbm, job).start()

        @pl.when(j >= n_chunk)
        def _():
            gather_copy(v_hbm, job).start()

    def compute_u(tok, c, b):
        def rg_body(rg, _):
            r0 = rg * 8

            def jbody(j2, accs):
                off0 = j2 * (2 * nl)
                off1 = off0 + nl
                xrow = tok * xs + off0 // LANES
                xl = off0 % LANES
                x0 = plsc.bitcast(x_v[xrow, pl.ds(xl, nl)], BF16)
                x1 = plsc.bitcast(x_v[xrow, pl.ds(xl + nl, nl)], BF16)
                new = []
                for r in range(8):
                    w0 = plsc.bitcast(rows[b, r0 + r, pl.ds(off0, nl)], BF16)
                    w1 = plsc.bitcast(rows[b, r0 + r, pl.ds(off1, nl)], BF16)
                    lo, hi = _unpack_pair(plsc.bitcast(w0 * x0 + w1 * x1, I32))
                    new.append(accs[r] + (lo + hi))
                return tuple(new)

            accs = lax.fori_loop(0, n_lane_blk // 2, jbody, (zero,) * 8)
            for r in range(8):
                p_v[c * rc + r0 + r, :] = accs[r]
            return 0

        lax.fori_loop(0, rc // 8, rg_body, 0)

    def finish_act(tok):
        def eg_body(eg, _):
            e0 = eg * nl
            ridx = e0 + lane
            s = zero
            for l in range(nl):
                s = s + plsc.load_gather(p_v, [ridx, jnp.full((nl,), l, I32)])
            inner = c_gelu * (s + 0.044715 * (s * s * s))
            gl = s / (1.0 + jnp.exp(-inner))
            a = gl * gate_v[tok, pl.ds(e0, nl)]
            bits = lax.bitcast_convert_type(a, I32)
            rnd = bits + jnp.int32(0x7FFF) + (lax.shift_right_logical(bits, 16) & 1)
            hi16 = rnd & jnp.int32(-65536)
            act_v[pl.ds(e0, nl)] = hi16 | lax.shift_right_logical(hi16, 16)
            return 0

        lax.fori_loop(0, n_sel // nl, eg_body, 0)

        def zbody(j, _):
            off = j * nl
            out_v[tok * os_ + off // LANES, pl.ds(off % LANES, nl)] = zero
            return 0

        lax.fori_loop(0, d // nl, zbody, 0, unroll=4)

    def compute_v(tok, c, b):
        def rg_body(rg, _):
            r0 = rg * nl
            splat = [plsc.bitcast(plsc.load_gather(act_v, [jnp.full((nl,), 0, I32) + (c * rc + r0 + r)]), BF16)
                     for r in range(nl)]

            def tree(parts):
                while len(parts) > 1:
                    parts = [parts[i] + parts[i + 1] for i in range(0, len(parts), 2)]
                return parts[0]

            @plsc.parallel_loop(0, n_lane_blk, unroll=2)
            def _(j):
                off = j * nl
                los, his = [], []
                for r in range(0, nl, 2):
                    w0 = plsc.bitcast(rows[b, r0 + r, pl.ds(off, nl)], BF16)
                    w1 = plsc.bitcast(rows[b, r0 + r + 1, pl.ds(off, nl)], BF16)
                    lo, hi = _unpack_pair(plsc.bitcast(w0 * splat[r] + w1 * splat[r + 1], I32))
                    los.append(lo)
                    his.append(hi)
                orow = tok * os_ + off // LANES
                ol = off % LANES
                out_v[orow, pl.ds(ol, nl)] = out_v[orow, pl.ds(ol, nl)] + tree(los)
                out_v[orow + xs, pl.ds(ol, nl)] = out_v[orow + xs, pl.ds(ol, nl)] + tree(his)

            return 0

        lax.fori_loop(0, rc // nl, rg_body, 0)

    def batch_body(bi, _):
        t0 = base + bi * tb
        pltpu.sync_copy(e_hbm.at[pl.ds(t0, tb)], idx_v)
        pltpu.sync_copy(g_hbm.at[pl.ds(t0, tb)], gate_v)
        pltpu.sync_copy(hn_hbm.at[pl.ds(t0 * xs, tb * xs)], x_v)
        for pre in range(PEER_RING - 1):
            start(pre)

        def job_body(job, _):
            @pl.when(job + (PEER_RING - 1) < tb * jobs_per_tok)
            def _():
                start(job + (PEER_RING - 1))

            j = job % jobs_per_tok
            gather_copy(u_hbm, job).wait()
            tok = job // jobs_per_tok
            b = job % PEER_RING

            @pl.when(j < n_chunk)
            def _():
                compute_u(tok, j, b)

            @pl.when(j == n_chunk - 1)
            def _():
                finish_act(tok)

            @pl.when(j >= n_chunk)
            def _():
                compute_v(tok, j - n_chunk, b)

            return 0

        lax.fori_loop(0, tb * jobs_per_tok, job_body, 0)
        pltpu.sync_copy(out_v, out_hbm.at[pl.ds(t0 * os_, tb * os_)])
        return 0

    lax.fori_loop(0, tpw // tb, batch_body, 0)


def _peer_sc(x_pk, experts, gates, u_tab, v_tab):
    d = 2 * u_tab.shape[1]
    n = x_pk.shape[0] * LANES // (d // 2)
    n_sel = experts.shape[1]
    nw = SC_CORES_V7X * SC_SUBCORES_V7X
    tpw = n // nw
    mesh = plsc.VectorSubcoreMesh(core_axis_name="c", subcore_axis_name="s",
                                  num_cores=SC_CORES_V7X, num_subcores=SC_SUBCORES_V7X)
    body = functools.partial(_peer_sc_body, tpw=tpw, d=d, n_sel=n_sel)
    call = pl.kernel(
        body,
        out_type=jax.ShapeDtypeStruct((n * d // LANES, LANES), F32),
        mesh=mesh,
        scratch_types=[pltpu.VMEM((PEER_TOK_BATCH, n_sel), I32),
                       pltpu.VMEM((PEER_TOK_BATCH, n_sel), F32),
                       pltpu.VMEM((PEER_TOK_BATCH * d // 2 // LANES, LANES), I32),
                       pltpu.VMEM((PEER_TOK_BATCH * d // LANES, LANES), F32),
                       pltpu.VMEM((PEER_RING, PEER_ROW_CHUNK, d // 2), I32),
                       pltpu.VMEM((n_sel, SC_LANES_V7X), F32),
                       pltpu.VMEM((n_sel,), I32),
                       pltpu.SemaphoreType.DMA((PEER_RING,))],
        compiler_params=pltpu.CompilerParams(needs_layout_passes=False, use_tc_tiling_on_sc=False),
        name="peer_sc",
    )
    return call(x_pk, experts, gates, u_tab, v_tab)


def kernel(x, norm1_g, w_in, a_re, a_im, log_dt, b_re, b_im, c_re, c_im, d_skip, w_glu, w_ssm_up, w_attn_up,
           w_out, norm2_g, peer_wq, peer_k1, peer_k2, peer_u, peer_v, norm_f_g):
    bsz, seq, d = x.shape
    depth = norm1_g.shape[0]
    chunks = _time_chunks(seq)
    h = x
    for layer in range(depth):
        last = layer + 1 == depth
        kv_w, main_w = _in_weights(w_in[layer], seq)
        s5p = _s5_params(a_re[layer], a_im[layer], log_dt[layer], b_re[layer], b_im[layer], c_re[layer],
                         c_im[layer], d_skip[layer], w_glu[layer], nb=bsz)
        n_state = s5p[4].shape[1]
        wsu = w_ssm_up[layer].astype(BF16)
        wau = w_attn_up[layer].astype(BF16)
        wo = w_out[layer].astype(BF16)
        wq = peer_wq[layer].astype(BF16)
        k1 = peer_k1[layer].astype(BF16)
        k2 = peer_k2[layer].astype(BF16)
        u_pk = _pack_bf16_pairs(peer_u[layer])
        v_pk = _pack_bf16_pairs(peer_v[layer])
        kv_tm = min(ROW_TILE, seq)
        head_len = 2 * DSA_KT
        split_kv = len(chunks) > 1 and chunks[0][1] <= head_len < seq
        k4, vt, ki = _kvproj(h, norm1_g[layer], kv_w, norm1_g[layer], seq=head_len if split_kv else seq,
                             tm=kv_tm, kt=DSA_KT)
        st_re = jnp.zeros((bsz, n_state), F32)
        st_im = jnp.zeros((bsz, n_state), F32)
        outs = []
        routed = ki
        peer_outs = []
        for c, (s0, sc) in enumerate(chunks):
            tm = math.gcd(math.gcd(s0, sc), ROW_TILE)
            if split_kv and s0 + sc > head_len:
                k4, vt, ki = _kvproj(h, norm1_g[layer], kv_w, routed, seq=seq, tm=kv_tm, kt=DSA_KT)
                split_kv = False
            after = (routed, peer_outs[c - SC_LAG] if c >= SC_LAG else ki)
            u, q, qi, wit, gs, ga = _inproj(h, norm1_g[layer], main_w, after, s0=s0, sc=sc, tm=tm)
            d_ssm = u.shape[-1]
            u_tb = u.transpose(1, 0, 2).reshape(sc * bsz, d_ssm)
            y_tb, st_re, st_im = _s5(u_tb, st_re, st_im, s5p, nb=bsz, tc=64)
            ys = y_tb.reshape(sc, bsz, d_ssm).transpose(1, 0, 2)
            ya = _dsa(q, qi, wit, ki, k4, vt, s0=s0, seq_total=seq, tq=DSA_TQ, kt=DSA_KT)
            hm, x_pk, qp = _merge(h, ys, ya, gs, ga, wsu, wau, wo, norm2_g[layer], wq, s0=s0, tm=tm)
            nt = bsz * sc
            e_t, g_t = _route(qp.reshape(nt, -1), k1, k2, tt=256)
            routed = e_t
            po = _peer_sc(x_pk.reshape(-1, LANES), e_t.T, g_t.T, u_pk, v_pk)
            peer_outs.append(po)
            hm2 = hm.reshape(nt, d)
            o = _final(hm2, po, norm_f_g, tm=tm) if last else hm2 + po.reshape(nt, d)
            outs.append(o.reshape(bsz, sc, d))
        h = jnp.concatenate(outs, axis=1)
    return h
```

```python
import functools
import math

import numpy as np
import jax
import jax.numpy as jnp
from jax import lax
from jax.experimental import pallas as pl
from jax.experimental.pallas import tpu as pltpu
from jax.experimental.pallas import tpu_sc as plsc

F32 = jnp.float32
BF16 = jnp.bfloat16
I32 = jnp.int32

SSM_GROUP = 16
SSM_STATE = 64
ATTN_HEADS = 8
ATTN_KV_HEADS = 2
HEAD_DIM = 64
IDX_HEADS = 8
IDX_DIM = 32
TOPK_MAX = 256
ROPE_THETA = 10000.0
NEG_BIG = -1e30
PEER_HEADS = 8
PEER_KEYS = 128
PEER_KEY_DIM = 128
PEER_TOPK = 16
NORM_EPS = 1e-6

TIME_SPLIT_32NDS = (1, 3, 4, 4, 4, 4, 4, 3, 2, 2, 1)
SC_LAG = 3
ROW_TILE = 512
DSA_TQ = 128
DSA_KT = 256
LANES = 128
INT_MIN = -(2 ** 31)
VMEM_LIMIT = 56 * 1024 * 1024


def _time_chunks(seq):
    unit = seq // 32
    if seq % 32 == 0 and unit % DSA_TQ == 0:
        sizes = [f * unit for f in TIME_SPLIT_32NDS]
    else:
        step = min(ROW_TILE, seq)
        sizes = [step] * (seq // step)
    assert sum(sizes) == seq
    starts = np.cumsum([0] + sizes[:-1]).tolist()
    return list(zip(starts, sizes))


def _cparams(sem):
    return pltpu.CompilerParams(dimension_semantics=sem, vmem_limit_bytes=VMEM_LIMIT)


def _gelu_tanh(x):
    return 0.5 * x * (1.0 + jnp.tanh(math.sqrt(2.0 / math.pi) * (x + 0.044715 * (x * x * x))))


def _sigmoid(x):
    return 1.0 / (1.0 + jnp.exp(-x))


def _rms(x, g):
    return x * lax.rsqrt(jnp.mean(x * x, axis=-1, keepdims=True) + NORM_EPS) * g


def _rot_cols(w, hd):
    d, n = w.shape
    w3 = w.reshape(d, n // hd, hd)
    half = hd // 2
    return jnp.concatenate([-w3[..., half:], w3[..., :half]], axis=-1).reshape(d, n)


def _rope_full(seq, hd, heads):
    pos = jnp.arange(seq, dtype=F32)
    inv = ROPE_THETA ** (-jnp.arange(0, hd, 2, dtype=F32) / hd)
    ang = pos[:, None] * inv[None, :]
    c = jnp.concatenate([jnp.cos(ang), jnp.cos(ang)], axis=-1)
    s = jnp.concatenate([jnp.sin(ang), jnp.sin(ang)], axis=-1)
    return jnp.tile(c, (1, heads)), jnp.tile(s, (1, heads))


def _in_weights(w_in, seq):
    d = w_in.shape[0]
    d_ssm = d // 2
    d_q = ATTN_HEADS * HEAD_DIM
    d_kv = ATTN_KV_HEADS * HEAD_DIM
    d_qi = IDX_HEADS * IDX_DIM
    splits = (d_ssm, d_q, d_kv, d_kv, d_qi, IDX_DIM, IDX_HEADS, d, d)
    offs = np.cumsum(splits)[:-1].tolist()
    wu, wq, wk, wv, wqi, wki, wwi, wgs, wga = jnp.split(w_in, offs, axis=1)
    pad = jnp.zeros((d, 128 - IDX_DIM), F32)
    cq, sq = _rope_full(seq, HEAD_DIM, ATTN_HEADS)
    ck, sk = _rope_full(seq, HEAD_DIM, ATTN_KV_HEADS)
    cqi, sqi = _rope_full(seq, IDX_DIM, IDX_HEADS)
    cki, ski = _rope_full(seq, IDX_DIM, 1)
    tpad = jnp.zeros((seq, 128 - IDX_DIM), F32)
    kv = dict(
        w=jnp.concatenate([wk, wki, pad], axis=1).astype(BF16),
        wvt=wv.T.astype(BF16),
        wr=jnp.concatenate([_rot_cols(wk, HEAD_DIM), _rot_cols(wki, IDX_DIM), pad], axis=1).astype(BF16),
        cs=jnp.concatenate([ck, cki, tpad], axis=1), sn=jnp.concatenate([sk, ski, tpad], axis=1))
    main = dict(
        w=jnp.concatenate([wu, wq, wqi, wgs, wga], axis=1).astype(BF16),
        wr=jnp.concatenate([_rot_cols(wq, HEAD_DIM), _rot_cols(wqi, IDX_DIM)], axis=1).astype(BF16),
        cs=jnp.concatenate([cq, cqi], axis=1), sn=jnp.concatenate([sq, sqi], axis=1),
        wwit=wwi.T.astype(BF16))
    return kv, main


def _kvproj_kernel(x_ref, g_ref, w_ref, wr_ref, wvt_ref, cs_ref, sn_ref, after_ref, k_ref, vt_ref, ki_ref,
                   *, d_kv, kt):
    del after_ref
    xb = _rms(x_ref[0], g_ref[...]).astype(BF16)

    def mm(ref, lo, n):
        return jnp.dot(xb, ref[:, lo:lo + n], preferred_element_type=F32)

    k = mm(w_ref, 0, d_kv) * cs_ref[:, :d_kv] + mm(wr_ref, 0, d_kv) * sn_ref[:, :d_kv]
    for n in range(ATTN_KV_HEADS):
        k_ref[0, n] = k[:, n * HEAD_DIM:(n + 1) * HEAD_DIM].astype(BF16)
    vt = lax.dot_general(wvt_ref[...], xb, (((1,), (1,)), ((), ())), preferred_element_type=F32)
    for n in range(ATTN_KV_HEADS):
        for j in range(vt.shape[1] // kt):
            vt_ref[0, n, j] = vt[n * HEAD_DIM:(n + 1) * HEAD_DIM, j * kt:(j + 1) * kt].astype(BF16)
    kiw = (mm(w_ref, d_kv, 128) * cs_ref[:, d_kv:d_kv + 128]
           + mm(wr_ref, d_kv, 128) * sn_ref[:, d_kv:d_kv + 128])
    ki_ref[0] = kiw[:, :IDX_DIM].astype(BF16)


def _kvproj(x, norm_g, kv, after, seq, tm, kt):
    bsz, _, d = x.shape
    d_kv = ATTN_KV_HEADS * HEAD_DIM
    full = lambda a: pl.BlockSpec(a.shape, lambda s, b: (0,) * a.ndim)
    g = norm_g.reshape(1, d)
    ncs = kv["cs"].shape[1]
    return pl.pallas_call(
        functools.partial(_kvproj_kernel, d_kv=d_kv, kt=kt),
        grid=(seq // tm, bsz),
        in_specs=[pl.BlockSpec((1, tm, d), lambda s, b: (b, s, 0)), full(g), full(kv["w"]), full(kv["wr"]),
                  full(kv["wvt"]),
                  pl.BlockSpec((tm, ncs), lambda s, b: (s, 0)), pl.BlockSpec((tm, ncs), lambda s, b: (s, 0)),
                  pl.BlockSpec(memory_space=pl.ANY)],
        out_specs=[pl.BlockSpec((1, ATTN_KV_HEADS, tm, HEAD_DIM), lambda s, b: (b, 0, s, 0)),
                   pl.BlockSpec((1, ATTN_KV_HEADS, tm // kt, HEAD_DIM, kt), lambda s, b: (b, 0, s, 0, 0)),
                   pl.BlockSpec((1, tm, IDX_DIM), lambda s, b: (b, s, 0))],
        out_shape=[jax.ShapeDtypeStruct((bsz, ATTN_KV_HEADS, seq, HEAD_DIM), BF16),
                   jax.ShapeDtypeStruct((bsz, ATTN_KV_HEADS, seq // kt, HEAD_DIM, kt), BF16),
                   jax.ShapeDtypeStruct((bsz, seq, IDX_DIM), BF16)],
        compiler_params=_cparams(("arbitrary", "arbitrary")),
        name="kvproj",
    )(x, g, kv["w"], kv["wr"], kv["wvt"], kv["cs"], kv["sn"], after)


def _inproj_kernel(x_ref, g_ref, w_ref, wr_ref, cs_ref, sn_ref, wwit_ref, after_tc_ref, after_sc_ref,
                   u_ref, q_ref, qi_ref, wit_ref, gs_ref, ga_ref, *, d_ssm, d_q, d_qi, d_model, q_scale, wi_scale):
    del after_tc_ref, after_sc_ref
    xb = _rms(x_ref[0], g_ref[...]).astype(BF16)

    def mm(ref, lo, n):
        return jnp.dot(xb, ref[:, lo:lo + n], preferred_element_type=F32)

    o = 0
    u_ref[0] = mm(w_ref, o, d_ssm).astype(BF16)
    o += d_ssm
    q = mm(w_ref, o, d_q) * cs_ref[:, :d_q] + mm(wr_ref, 0, d_q) * sn_ref[:, :d_q]
    q_ref[0] = (q * q_scale).astype(BF16)
    o += d_q
    qi = mm(w_ref, o, d_qi) * cs_ref[:, d_q:d_q + d_qi] + mm(wr_ref, d_q, d_qi) * sn_ref[:, d_q:d_q + d_qi]
    qi_ref[0] = qi.astype(BF16)
    o += d_qi
    gs_ref[0] = _sigmoid(mm(w_ref, o, d_model)).astype(BF16)
    o += d_model
    ga_ref[0] = _sigmoid(mm(w_ref, o, d_model)).astype(BF16)
    wit_ref[0] = lax.dot_general(wwit_ref[...], xb, (((1,), (1,)), ((), ())),
                                 preferred_element_type=F32) * wi_scale


def _inproj(x, norm_g, main, after, s0, sc, tm):
    bsz, _, d = x.shape
    d_ssm = d // 2
    d_q = ATTN_HEADS * HEAD_DIM
    d_qi = IDX_HEADS * IDX_DIM
    i0 = s0 // tm
    kern = functools.partial(
        _inproj_kernel, d_ssm=d_ssm, d_q=d_q, d_qi=d_qi, d_model=d,
        q_scale=HEAD_DIM ** -0.5 * math.log2(math.e), wi_scale=(IDX_HEADS ** -0.5) * (IDX_DIM ** -0.5))
    tok = lambda n: pl.BlockSpec((1, tm, n), lambda s, b: (b, s, 0))
    full = lambda a: pl.BlockSpec(a.shape, lambda s, b: (0,) * a.ndim)
    g = norm_g.reshape(1, d)
    ncs = main["cs"].shape[1]
    outs = [(d_ssm, BF16), (d_q, BF16), (d_qi, BF16)]
    return pl.pallas_call(
        kern,
        grid=(sc // tm, bsz),
        in_specs=[pl.BlockSpec((1, tm, d), lambda s, b: (b, s + i0, 0)), full(g), full(main["w"]), full(main["wr"]),
                  pl.BlockSpec((tm, ncs), lambda s, b: (s + i0, 0)),
                  pl.BlockSpec((tm, ncs), lambda s, b: (s + i0, 0)), full(main["wwit"]),
                  pl.BlockSpec(memory_space=pl.ANY), pl.BlockSpec(memory_space=pl.ANY)],
        out_specs=[tok(n) for n, _ in outs] + [pl.BlockSpec((1, IDX_HEADS, tm), lambda s, b: (b, 0, s)),
                                                tok(d), tok(d)],
        out_shape=[jax.ShapeDtypeStruct((bsz, sc, n), dt) for n, dt in outs]
        + [jax.ShapeDtypeStruct((bsz, IDX_HEADS, sc), F32),
           jax.ShapeDtypeStruct((bsz, sc, d), BF16), jax.ShapeDtypeStruct((bsz, sc, d), BF16)],
        compiler_params=_cparams(("arbitrary", "arbitrary")),
        name="inproj",
    )(x, g, main["w"], main["wr"], main["cs"], main["sn"], main["wwit"], *after)


def _s5_kernel(u_ref, sre_in, sim_in, bre_ref, bim_ref, cre_ref, cim_ref, are_ref, aim_ref, dsk_ref, wglu_ref,
               y_ref, st_re, st_im, sre, sim, *, tc, nb, lane_chunk):
    @pl.when(pl.program_id(0) == 0)
    def _():
        st_re[...] = sre_in[...]
        st_im[...] = sim_in[...]

    u = u_ref[...]
    n_half = bre_ref.shape[0]
    hin = bre_ref.shape[1]
    hst = bre_ref.shape[2]
    for h in range(n_half):
        uh = u[:, h * hin:(h + 1) * hin]
        sre[:, h * hst:(h + 1) * hst] = jnp.dot(uh, bre_ref[h], preferred_element_type=F32)
        sim[:, h * hst:(h + 1) * hst] = jnp.dot(uh, bim_ref[h], preferred_element_type=F32)

    n_state = sre.shape[1]
    for c in range(n_state // lane_chunk):
        cols = slice(c * lane_chunk, (c + 1) * lane_chunk)
        ar = are_ref[:, cols]
        ai = aim_ref[:, cols]

        def step(t, carry, cols=cols, ar=ar, ai=ai):
            sr, si = carry
            r0 = pl.multiple_of(t * nb, nb)
            nr = ar * sr - ai * si + sre[pl.ds(r0, nb), cols]
            ni = ar * si + ai * sr + sim[pl.ds(r0, nb), cols]
            sre[pl.ds(r0, nb), cols] = nr
            sim[pl.ds(r0, nb), cols] = ni
            return nr, ni

        sr, si = lax.fori_loop(0, tc, step, (st_re[:, cols], st_im[:, cols]), unroll=4)
        st_re[:, cols] = sr
        st_im[:, cols] = si

    ys = []
    for h in range(n_half):
        srh = sre[:, h * hst:(h + 1) * hst].astype(BF16)
        sih = sim[:, h * hst:(h + 1) * hst].astype(BF16)
        ys.append(jnp.dot(srh, cre_ref[h], preferred_element_type=F32)
                  - jnp.dot(sih, cim_ref[h], preferred_element_type=F32))
    y = jnp.concatenate(ys, axis=-1) + dsk_ref[...] * u.astype(F32)
    y = _gelu_tanh(y)
    gate = jnp.dot(y.astype(BF16), wglu_ref[...], preferred_element_type=F32)
    y_ref[...] = (y * _sigmoid(gate)).astype(BF16)


def _s5_params(a_re, a_im, log_dt, b_re, b_im, c_re, c_im, d_skip, w_glu, nb):
    groups = a_re.shape[0]
    d_ssm = groups * SSM_GROUP
    n_state = groups * SSM_STATE
    lam = lax.complex(a_re, a_im)
    dt = jnp.exp(log_dt)[:, None]
    a_bar = jnp.exp(lam * dt)
    b_bar = ((a_bar - 1.0) / lam)[..., None] * lax.complex(b_re, b_im)
    gh = min(groups, 256 // SSM_GROUP)
    n_half = groups // gh
    eye = jnp.eye(gh, dtype=F32)

    def bmat(bb):
        b4 = bb.reshape(n_half, gh, SSM_STATE, SSM_GROUP)
        return jnp.einsum('hgpc,gk->hgckp', b4, eye).reshape(n_half, gh * SSM_GROUP, gh * SSM_STATE)

    def cmat(cc):
        c4 = cc.reshape(n_half, gh, SSM_GROUP, SSM_STATE)
        return jnp.einsum('hgcp,gk->hgpkc', c4, eye).reshape(n_half, gh * SSM_STATE, gh * SSM_GROUP)

    return (bmat(jnp.real(b_bar)).astype(BF16), bmat(jnp.imag(b_bar)).astype(BF16),
            cmat(c_re).astype(BF16), cmat(c_im).astype(BF16),
            jnp.broadcast_to(jnp.real(a_bar).reshape(1, n_state), (nb, n_state)),
            jnp.broadcast_to(jnp.imag(a_bar).reshape(1, n_state), (nb, n_state)),
            d_skip.reshape(1, d_ssm), w_glu.astype(BF16))


def _s5(u_tb, st_re, st_im, params, nb, tc):
    rows, d_ssm = u_tb.shape
    n_state = st_re.shape[1]
    blk = tc * nb
    full = lambda a: pl.BlockSpec(a.shape, lambda i: (0,) * a.ndim)
    st_spec = pl.BlockSpec((nb, n_state), lambda i: (0, 0))
    kern = functools.partial(_s5_kernel, tc=tc, nb=nb, lane_chunk=512)
    return pl.pallas_call(
        kern,
        grid=(rows // blk,),
        in_specs=[pl.BlockSpec((blk, d_ssm), lambda i: (i, 0)), st_spec, st_spec] + [full(p) for p in params],
        out_specs=[pl.BlockSpec((blk, d_ssm), lambda i: (i, 0)), st_spec, st_spec],
        out_shape=[jax.ShapeDtypeStruct((rows, d_ssm), BF16),
                   jax.ShapeDtypeStruct((nb, n_state), F32), jax.ShapeDtypeStruct((nb, n_state), F32)],
        scratch_shapes=[pltpu.VMEM((blk, n_state), F32), pltpu.VMEM((blk, n_state), F32)],
        compiler_params=_cparams(("arbitrary",)),
        name="s5",
    )(u_tb, st_re, st_im, *params)


PART_ROWS = 32


def _dsa_kernel(qi_ref, wit_ref, q_ref, ki_ref, k_ref, vt_ref, o_ref, key_s, bias_s, lg_s,
                *, qb0, tq, kt, sub, topk, seq_bits):
    qb = pl.program_id(1) + qb0
    nkt = ((qb * tq + tq + sub * kt - 1) // (sub * kt)) * sub
    q_pos = qb * tq + lax.broadcasted_iota(I32, (1, tq), 1)
    k_eff = jnp.minimum(topk, q_pos + 1).astype(F32)

    qi = qi_ref[0]
    wit = wit_ref[0]
    qipair = [jnp.concatenate([qi[:, (2 * hp) * IDX_DIM:(2 * hp + 1) * IDX_DIM],
                               qi[:, (2 * hp + 1) * IDX_DIM:(2 * hp + 2) * IDX_DIM]], axis=0)
              for hp in range(IDX_HEADS // 2)]

    def key_pos(t):
        return t * kt + lax.broadcasted_iota(I32, (kt, tq), 0)

    def score_tile(t2, _):
        for hf in range(sub):
            r0 = pl.multiple_of((t2 * sub + hf) * kt, kt)
            ki_t = ki_ref[0, pl.ds(r0, kt), :]
            for hp in range(IDX_HEADS // 2):
                lg_s[hf, hp] = lax.dot_general(ki_t, qipair[hp], (((1,), (1,)), ((), ())),
                                               preferred_element_type=F32)
        for hf in range(sub):
            t = t2 * sub + hf
            r0 = pl.multiple_of(t * kt, kt)
            sc = jnp.zeros((kt, tq), F32)
            for hp in range(IDX_HEADS // 2):
                rel = lg_s[hf, hp]
                sc = sc + jnp.maximum(rel[:, :tq], 0.0) * wit[2 * hp:2 * hp + 1, :]
                sc = sc + jnp.maximum(rel[:, tq:], 0.0) * wit[2 * hp + 1:2 * hp + 2, :]
            bits = lax.bitcast_convert_type(sc, I32)
            key = jnp.where(bits < 0, bits ^ jnp.int32(0x7FFFFFFF), bits)
            key = jnp.where(key_pos(t) <= q_pos, key, jnp.int32(INT_MIN))
            key_s[pl.ds(r0, kt), :] = key
        return 0

    lax.fori_loop(0, nkt // sub, score_tile, 0)

    def count(pred_fn):
        def body(t, acc):
            r0 = pl.multiple_of(t * kt, kt)
            m = pred_fn(key_s[pl.ds(r0, kt), :], t)
            ones = jnp.where(m, 1.0, 0.0).reshape(kt // PART_ROWS, PART_ROWS, tq)
            return acc + jnp.sum(ones, axis=0)
        acc = lax.fori_loop(0, nkt, body, jnp.zeros((PART_ROWS, tq), F32))
        return jnp.sum(acc, axis=0, keepdims=True)

    def bit_step(i, u):
        bit = jnp.left_shift(jnp.int32(1), 31 - i)
        cand_u = u | bit
        cand_s = cand_u ^ jnp.int32(INT_MIN)
        cnt = count(lambda kk, t: kk >= cand_s)
        return jnp.where(cnt >= k_eff, cand_u, u)

    u_thr = lax.fori_loop(0, 32, bit_step, jnp.zeros((1, tq), I32))
    thr = u_thr ^ jnp.int32(INT_MIN)

    cnt_ge = count(lambda kk, t: kk >= thr)
    cnt_gt = count(lambda kk, t: kk > thr)
    need_eq = k_eff - cnt_gt
    has_tie = jnp.max(cnt_ge - k_eff) > 0.0

    def tie_cut():
        def pos_step(i, c):
            bit = jnp.left_shift(jnp.int32(1), seq_bits - 1 - i)
            cand = c | bit
            cnt = count(lambda kk, t: (kk == thr) & (key_pos(t) < cand))
            return jnp.where(cnt < need_eq, cand, c)
        return lax.fori_loop(0, seq_bits, pos_step, jnp.zeros((1, tq), I32))

    cut = lax.cond(has_tie, tie_cut, lambda: jnp.full((1, tq), 2 ** seq_bits, I32))

    def bias_tile(t, _):
        r0 = pl.multiple_of(t * kt, kt)
        key = key_s[pl.ds(r0, kt), :]
        sel = (key > thr) | ((key == thr) & (key_pos(t) <= cut))
        bias_s[pl.ds(r0, kt), :] = jnp.where(sel, 0.0, NEG_BIG)
        return 0

    lax.fori_loop(0, nkt, bias_tile, 0)

    q = q_ref[0]
    grp = ATTN_HEADS // ATTN_KV_HEADS
    pairs_per_kv = grp // 2
    n_unit = ATTN_KV_HEADS * pairs_per_kv
    wq = 2 * tq
    qpair = [jnp.concatenate([q[:, (2 * u) * HEAD_DIM:(2 * u + 1) * HEAD_DIM],
                              q[:, (2 * u + 1) * HEAD_DIM:(2 * u + 2) * HEAD_DIM]], axis=0)
             for u in range(n_unit)]

    def col_reduce(x, op):
        part = op(x.reshape(kt // PART_ROWS, PART_ROWS, wq), axis=0)
        return op(part, axis=0, keepdims=True)

    def attn_tile(t, carry):
        ms, ls, accs = list(carry[0]), list(carry[1]), list(carry[2])
        for hf in range(sub):
            r0 = pl.multiple_of((t * sub + hf) * kt, kt)
            bias = bias_s[pl.ds(r0, kt), :]
            bias2 = jnp.concatenate([bias, bias], axis=1)
            for u in range(n_unit):
                k_t = k_ref[0, u // pairs_per_kv, pl.ds(r0, kt), :]
                lg_s[hf, u] = lax.dot_general(k_t, qpair[u], (((1,), (1,)), ((), ())),
                                              preferred_element_type=F32) + bias2
        for hf in range(sub):
            for u in range(n_unit):
                lg = lg_s[hf, u]
                m_new = jnp.maximum(ms[u], col_reduce(lg, jnp.max))
                p = jnp.exp2(lg - m_new)
                alpha = jnp.exp2(ms[u] - m_new)
                v_t = vt_ref[0, u // pairs_per_kv, t * sub + hf]
                ls[u] = alpha * ls[u] + col_reduce(p, jnp.sum)
                accs[u] = alpha * accs[u] + jnp.dot(v_t, p.astype(BF16), preferred_element_type=F32)
                ms[u] = m_new
        return tuple(ms), tuple(ls), tuple(accs)

    init = (tuple(jnp.full((1, wq), NEG_BIG, F32) for _ in range(n_unit)),
            tuple(jnp.zeros((1, wq), F32) for _ in range(n_unit)),
            tuple(jnp.zeros((HEAD_DIM, wq), F32) for _ in range(n_unit)))
    _, ls, accs = lax.fori_loop(0, nkt // sub, attn_tile, init)
    for n in range(ATTN_KV_HEADS):
        o_ref[0, 0, n] = jnp.concatenate([accs[n * pairs_per_kv + pg] / ls[n * pairs_per_kv + pg]
                                          for pg in range(pairs_per_kv)], axis=1).astype(BF16)


def _dsa(q, qi, wit, ki, k4, vt, s0, seq_total, tq, kt):
    bsz, sc, _ = q.shape
    seq = ki.shape[1]
    topk = min(TOPK_MAX, seq_total // 4)
    nqb = sc // tq
    grp = ATTN_HEADS // ATTN_KV_HEADS
    seq_bits = int(math.log2(seq))
    assert 2 ** seq_bits == seq
    sub = 2 if seq % (2 * kt) == 0 else 1
    n_unit = ATTN_HEADS // 2
    assert IDX_HEADS // 2 <= n_unit
    kern =functools.partial(_dsa_kernel, qb0=s0 // tq, tq=tq, kt=kt, sub=sub, topk=topk, seq_bits=seq_bits)
    o_t = pl.pallas_call(
        kern,
        grid=(bsz, nqb),
        in_specs=[pl.BlockSpec((1, tq, IDX_HEADS * IDX_DIM), lambda b, j: (b, j, 0)),
                  pl.BlockSpec((1, IDX_HEADS, tq), lambda b, j: (b, 0, j)),
                  pl.BlockSpec((1, tq, ATTN_HEADS * HEAD_DIM), lambda b, j: (b, j, 0)),
                  pl.BlockSpec((1, seq, IDX_DIM), lambda b, j: (b, 0, 0)),
                  pl.BlockSpec((1, ATTN_KV_HEADS, seq, HEAD_DIM), lambda b, j: (b, 0, 0, 0)),
                  pl.BlockSpec((1, ATTN_KV_HEADS, seq // kt, HEAD_DIM, kt), lambda b, j: (b, 0, 0, 0, 0))],
        out_specs=pl.BlockSpec((1, 1, ATTN_KV_HEADS, HEAD_DIM, grp * tq), lambda b, j: (b, j, 0, 0, 0)),
        out_shape=jax.ShapeDtypeStruct((bsz, nqb, ATTN_KV_HEADS, HEAD_DIM, grp * tq), BF16),
        scratch_shapes=[pltpu.VMEM((seq, tq), I32), pltpu.VMEM((seq, tq), F32),
                        pltpu.VMEM((sub, n_unit, kt, 2 * tq), F32)],
        compiler_params=_cparams(("arbitrary", "arbitrary")),
        name="dsa",
    )(qi, wit, q, ki, k4, vt)
    o = o_t.reshape(bsz, nqb, ATTN_KV_HEADS, HEAD_DIM, grp, tq).transpose(0, 1, 5, 2, 4, 3)
    return o.reshape(bsz, sc, ATTN_HEADS * HEAD_DIM)


def _merge_kernel(x_ref, ys_ref, ya_ref, gs_ref, ga_ref, wsu_ref, wau_ref, wout_ref, g2_ref, wq_ref,
                  h_ref, hn_ref, qp_ref):
    ms = jnp.dot(ys_ref[0], wsu_ref[...], preferred_element_type=F32)
    ma = jnp.dot(ya_ref[0], wau_ref[...], preferred_element_type=F32)
    merged = gs_ref[0].astype(F32) * ms + ga_ref[0].astype(F32) * ma
    h = x_ref[0] + jnp.dot(merged.astype(BF16), wout_ref[...], preferred_element_type=F32)
    h_ref[0] = h
    hb = _rms(h, g2_ref[...]).astype(BF16)
    words = _pack_words(hb)
    half = words.shape[1]
    n_seg = half // LANES
    tm = words.shape[0]
    for p in range(n_seg):
        hn_ref[0, pl.ds(p, tm, stride=n_seg), :] = words[:, p * LANES:(p + 1) * LANES]
    qp_ref[0] = jnp.dot(hb, wq_ref[...], preferred_element_type=F32).astype(BF16)


def _merge(x, ys, ya, gs, ga, wsu, wau, wo, norm2_g, wq, s0, tm):
    bsz, sc, _ = ya.shape
    d = x.shape[2]
    i0 = s0 // tm
    loc = lambda a: pl.BlockSpec((1, tm, a.shape[2]), lambda b, i: (b, i, 0))
    full = lambda a: pl.BlockSpec(a.shape, lambda b, i: (0,) * a.ndim)
    g2 = norm2_g.reshape(1, d)
    nq = wq.shape[1]
    n_seg = d // 2 // LANES
    out = lambda n: pl.BlockSpec((1, tm, n), lambda b, i: (b, i, 0))
    return pl.pallas_call(
        _merge_kernel,
        grid=(bsz, sc // tm),
        in_specs=[pl.BlockSpec((1, tm, d), lambda b, i: (b, i + i0, 0)), loc(ys), loc(ya), loc(gs), loc(ga),
                  full(wsu), full(wau), full(wo), full(g2), full(wq)],
        out_specs=[out(d), pl.BlockSpec((1, tm * n_seg, LANES), lambda b, i: (b, i, 0)), out(nq)],
        out_shape=[jax.ShapeDtypeStruct((bsz, sc, d), F32), jax.ShapeDtypeStruct((bsz, sc * n_seg, LANES), I32),
                   jax.ShapeDtypeStruct((bsz, sc, nq), BF16)],
        compiler_params=_cparams(("arbitrary", "arbitrary")),
        name="merge",
    )(x, ys, ya, gs, ga, wsu, wau, wo, g2, wq)


def _cand_layout():
    blocks = []
    blocks.append((0, 16, 16))
    for i in range(1, 8):
        blocks.append((i, 8, PEER_TOPK // (i + 1)))
    blocks.append((None, 8, 8))
    return blocks


def _top_rows(s, order, payload, k):
    big = jnp.float32(3e38)
    vals, pays = [], []
    for _ in range(k):
        m = jnp.max(s, axis=0, keepdims=True)
        o = jnp.min(jnp.where(s == m, order, big), axis=0, keepdims=True)
        hit = order == o
        pays.append(o if payload is order else jnp.min(jnp.where(hit, payload, big), axis=0, keepdims=True))
        vals.append(m)
        s = jnp.where(hit, -jnp.inf, s)
    return jnp.concatenate(vals, axis=0), jnp.concatenate(pays, axis=0)


def _route_kernel(qp_ref, k1_ref, k2_ref, after_ref, e_ref, g_ref, *, tt):
    del after_ref
    qp = qp_ref[...]
    kd = PEER_KEY_DIM
    rows_k = lax.broadcasted_iota(I32, (PEER_KEYS, tt), 0).astype(F32)
    for h in range(PEER_HEADS):
        q1 = qp[:, (2 * h) * kd:(2 * h + 1) * kd]
        q2 = qp[:, (2 * h + 1) * kd:(2 * h + 2) * kd]
        s1 = lax.dot_general(k1_ref[h], q1, (((1,), (1,)), ((), ())), preferred_element_type=F32)
        s2 = lax.dot_general(k2_ref[h], q2, (((1,), (1,)), ((), ())), preferred_element_type=F32)
        v1, i1 = _top_rows(s1, rows_k, rows_k, PEER_TOPK)
        v2, i2 = _top_rows(s2, rows_k, rows_k, PEER_TOPK)
        cs, ce, co = [], [], []
        for i, rows, valid in _cand_layout():
            r = lax.broadcasted_iota(I32, (rows, tt), 0).astype(F32)
            if i is None:
                val = v1[8:16] + v2[0:1]
                eid = i1[8:16] * PEER_KEYS + i2[0:1]
                flat = (r + 8.0) * PEER_TOPK
            else:
                val = v1[i:i + 1] + v2[0:rows]
                eid = i1[i:i + 1] * PEER_KEYS + i2[0:rows]
                flat = r + float(i * PEER_TOPK)
                if valid < rows:
                    val = jnp.where(r < float(valid), val, -jnp.inf)
            cs.append(val)
            ce.append(eid)
            co.append(flat)
        cand = jnp.concatenate(cs, axis=0)
        top_s, top_e = _top_rows(cand, jnp.concatenate(co, axis=0), jnp.concatenate(ce, axis=0), PEER_TOPK)
        p = jnp.exp(top_s - top_s[0:1])
        gates = p / jnp.sum(p, axis=0, keepdims=True)
        e_ref[h * PEER_TOPK:(h + 1) * PEER_TOPK, :] = top_e.astype(I32)
        g_ref[h * PEER_TOPK:(h + 1) * PEER_TOPK, :] = gates


def _route(qp, k1, k2, after, tt):
    n, nq = qp.shape
    n_sel = PEER_HEADS * PEER_TOPK
    full = lambda a: pl.BlockSpec(a.shape, lambda i: (0,) * a.ndim)
    return pl.pallas_call(
        functools.partial(_route_kernel, tt=tt),
        grid=(n // tt,),
        in_specs=[pl.BlockSpec((tt, nq), lambda i: (i, 0)), full(k1), full(k2),
                  pl.BlockSpec(memory_space=pl.ANY)],
        out_specs=[pl.BlockSpec((n_sel, tt), lambda i: (0, i)), pl.BlockSpec((n_sel, tt), lambda i: (0, i))],
        out_shape=[jax.ShapeDtypeStruct((n_sel, n), I32), jax.ShapeDtypeStruct((n_sel, n), F32)],
        compiler_params=_cparams(("arbitrary",)),
        name="route",
    )(qp, k1, k2, after)


def _final_kernel(h_ref, p_ref, g_ref, o_ref):
    tm, d = h_ref.shape
    n_seg = d // LANES
    p = jnp.concatenate([p_ref[pl.ds(k, tm, stride=n_seg), :] for k in range(n_seg)], axis=1)
    o_ref[...] = _rms(h_ref[...] + p, g_ref[...])


def _final(h, p, g, tm):
    n, d = h.shape
    row = pl.BlockSpec((tm, d), lambda i: (i, 0))
    return pl.pallas_call(
        _final_kernel,
        grid=(n // tm,),
        in_specs=[row, pl.BlockSpec((tm * (d // LANES), LANES), lambda i: (i, 0)),
                  pl.BlockSpec((1, d), lambda i: (0, 0))],
        out_specs=row,
        out_shape=jax.ShapeDtypeStruct((n, d), F32),
        compiler_params=_cparams(("arbitrary",)),
        name="final",
    )(h, p, g.reshape(1, d))


SC_CORES_V7X = 2
SC_SUBCORES_V7X = 16
SC_LANES_V7X = 16
PEER_TOK_BATCH = 32
PEER_ROW_CHUNK = 32
PEER_RING = 4


def _pack_words(x):
    bits = lax.bitcast_convert_type(x.astype(BF16).astype(F32), I32)
    half = bits.shape[1] // 2
    return (bits[:, half:] & jnp.int32(-65536)) | lax.shift_right_logical(bits[:, :half], 16)


def _pack_kernel(t_ref, o_ref):
    o_ref[...] = _pack_words(t_ref[...])


def _pack_bf16_pairs(t, rows=1024):
    e, d = t.shape
    return pl.pallas_call(
        _pack_kernel,
        grid=(e // rows,),
        in_specs=[pl.BlockSpec((rows, d), lambda i: (i, 0))],
        out_specs=pl.BlockSpec((rows, d // 2), lambda i: (i, 0)),
        out_shape=jax.ShapeDtypeStruct((e, d // 2), I32),
        compiler_params=_cparams(("arbitrary",)),
        name="pack_table",
    )(t)


def _unpack_pair(w):
    lo = lax.bitcast_convert_type(jnp.left_shift(w, 16), F32)
    hi = lax.bitcast_convert_type(w & jnp.int32(-65536), F32)
    return lo, hi


def _peer_sc_body(hn_hbm, e_hbm, g_hbm, u_hbm, v_hbm, out_hbm,
                  idx_v, gate_v, x_v, out_v, rows, p_v, act_v, sem, *, tpw, d, n_sel):
    nl = SC_LANES_V7X
    tb = PEER_TOK_BATCH
    rc = PEER_ROW_CHUNK
    n_chunk = n_sel // rc
    jobs_per_tok = 2 * n_chunk
    half = d // 2
    n_lane_blk = half // nl
    xs = half // LANES
    os_ = d // LANES
    wid =lax.axis_index("s") * SC_CORES_V7X + lax.axis_index("c")
    base = wid * tpw
    lane = lax.iota(I32, nl)
    zero = jnp.zeros((nl,), F32)
    c_gelu = 2.0 * math.sqrt(2.0 / math.pi)

    def gather_copy(tab_hbm, job):
        tok = job // jobs_per_tok
        c = (job % jobs_per_tok) % n_chunk
        b = job % PEER_RING
        return pltpu.make_async_copy(tab_hbm.at[idx_v.at[tok, pl.ds(c * rc, rc)]], rows.at[b], sem.at[b])

    def start(job):
        j = job % jobs_per_tok

        @pl.when(j < n_chunk)
        def _():
            gather_copy(u_hbm, job).start()

        @pl.when(j >= n_chunk)
        def _():
            gather_copy(v_hbm, job).start()

    def compute_u(tok, c, b):
        def rg_body(rg, _):
            r0 = rg * 8

            def jbody(j2, accs):
                off0 = j2 * (2 * nl)
                off1 = off0 + nl
                xrow = tok * xs + off0 // LANES
                xl = off0 % LANES
                x0 = plsc.bitcast(x_v[xrow, pl.ds(xl, nl)], BF16)
                x1 = plsc.bitcast(x_v[xrow, pl.ds(xl + nl, nl)], BF16)
                new = []
                for r in range(8):
                    w0 = plsc.bitcast(rows[b, r0 + r, pl.ds(off0, nl)], BF16)
                    w1 = plsc.bitcast(rows[b, r0 + r, pl.ds(off1, nl)], BF16)
                    lo, hi = _unpack_pair(plsc.bitcast(w0 * x0 + w1 * x1, I32))
                    new.append(accs[r] + (lo + hi))
                return tuple(new)

            accs = lax.fori_loop(0, n_lane_blk // 2, jbody, (zero,) * 8)
            for r in range(8):
                p_v[c * rc + r0 + r, :] = accs[r]
            return 0

        lax.fori_loop(0, rc // 8, rg_body, 0)

    def finish_act(tok):
        def eg_body(eg, _):
            e0 = eg * nl
            ridx = e0 + lane
            s = zero
            for l in range(nl):
                s = s + plsc.load_gather(p_v, [ridx, jnp.full((nl,), l, I32)])
            inner = c_gelu * (s + 0.044715 * (s * s * s))
            gl = s / (1.0 + jnp.exp(-inner))
            a = gl * gate_v[tok, pl.ds(e0, nl)]
            bits = lax.bitcast_convert_type(a, I32)
            rnd = bits + jnp.int32(0x7FFF) + (lax.shift_right_logical(bits, 16) & 1)
            hi16 = rnd & jnp.int32(-65536)
            act_v[pl.ds(e0, nl)] = hi16 | lax.shift_right_logical(hi16, 16)
            return 0

        lax.fori_loop(0, n_sel // nl, eg_body, 0)

        def zbody(j, _):
            off = j * nl
            out_v[tok * os_ + off // LANES, pl.ds(off % LANES, nl)] = zero
            return 0

        lax.fori_loop(0, d // nl, zbody, 0, unroll=4)

    def compute_v(tok, c, b):
        def rg_body(rg, _):
            r0 = rg * nl
            splat = [plsc.bitcast(plsc.load_gather(act_v, [jnp.full((nl,), 0, I32) + (c * rc + r0 + r)]), BF16)
                     for r in range(nl)]

            def tree(parts):
                while len(parts) > 1:
                    parts = [parts[i] + parts[i + 1] for i in range(0, len(parts), 2)]
                return parts[0]

            @plsc.parallel_loop(0, n_lane_blk, unroll=2)
            def _(j):
                off = j * nl
                los, his = [], []
                for r in range(0, nl, 2):
                    w0 = plsc.bitcast(rows[b, r0 + r, pl.ds(off, nl)], BF16)
                    w1 = plsc.bitcast(rows[b, r0 + r + 1, pl.ds(off, nl)], BF16)
                    lo, hi = _unpack_pair(plsc.bitcast(w0 * splat[r] + w1 * splat[r + 1], I32))
                    los.append(lo)
                    his.append(hi)
                orow = tok * os_ + off // LANES
                ol = off % LANES
                out_v[orow, pl.ds(ol, nl)] = out_v[orow, pl.ds(ol, nl)] + tree(los)
                out_v[orow + xs, pl.ds(ol, nl)] = out_v[orow + xs, pl.ds(ol, nl)] + tree(his)

            return 0

        lax.fori_loop(0, rc // nl, rg_body, 0)

    def batch_body(bi, _):
        t0 = base + bi * tb
        pltpu.sync_copy(e_hbm.at[pl.ds(t0, tb)], idx_v)
        pltpu.sync_copy(g_hbm.at[pl.ds(t0, tb)], gate_v)
        pltpu.sync_copy(hn_hbm.at[pl.ds(t0 * xs, tb * xs)], x_v)
        for pre in range(PEER_RING - 1):
            start(pre)

        def job_body(job, _):
            @pl.when(job + (PEER_RING - 1) < tb * jobs_per_tok)
            def _():
                start(job + (PEER_RING - 1))

            j = job % jobs_per_tok
            gather_copy(u_hbm, job).wait()
            tok = job // jobs_per_tok
            b = job % PEER_RING

            @pl.when(j < n_chunk)
            def _():
                compute_u(tok, j, b)

            @pl.when(j == n_chunk - 1)
            def _():
                finish_act(tok)

            @pl.when(j >= n_chunk)
            def _():
                compute_v(tok, j - n_chunk, b)

            return 0

        lax.fori_loop(0, tb * jobs_per_tok, job_body, 0)
        pltpu.sync_copy(out_v, out_hbm.at[pl.ds(t0 * os_, tb * os_)])
        return 0

    lax.fori_loop(0, tpw // tb, batch_body, 0)


def _peer_sc(x_pk, experts, gates, u_tab, v_tab):
    d = 2 * u_tab.shape[1]
    n = x_pk.shape[0] * LANES // (d // 2)
    n_sel = experts.shape[1]
    nw = SC_CORES_V7X * SC_SUBCORES_V7X
    tpw = n // nw
    mesh = plsc.VectorSubcoreMesh(core_axis_name="c", subcore_axis_name="s",
                                  num_cores=SC_CORES_V7X, num_subcores=SC_SUBCORES_V7X)
    body = functools.partial(_peer_sc_body, tpw=tpw, d=d, n_sel=n_sel)
    call = pl.kernel(
        body,
        out_type=jax.ShapeDtypeStruct((n * d // LANES, LANES), F32),
        mesh=mesh,
        scratch_types=[pltpu.VMEM((PEER_TOK_BATCH, n_sel), I32),
                       pltpu.VMEM((PEER_TOK_BATCH, n_sel), F32),
                       pltpu.VMEM((PEER_TOK_BATCH * d // 2 // LANES, LANES), I32),
                       pltpu.VMEM((PEER_TOK_BATCH * d // LANES, LANES), F32),
                       pltpu.VMEM((PEER_RING, PEER_ROW_CHUNK, d // 2), I32),
                       pltpu.VMEM((n_sel, SC_LANES_V7X), F32),
                       pltpu.VMEM((n_sel,), I32),
                       pltpu.SemaphoreType.DMA((PEER_RING,))],
        compiler_params=pltpu.CompilerParams(needs_layout_passes=False, use_tc_tiling_on_sc=False),
        name="peer_sc",
    )
    return call(x_pk, experts, gates, u_tab, v_tab)


def kernel(x, norm1_g, w_in, a_re, a_im, log_dt, b_re, b_im, c_re, c_im, d_skip, w_glu, w_ssm_up, w_attn_up,
           w_out, norm2_g, peer_wq, peer_k1, peer_k2, peer_u, peer_v, norm_f_g):
    bsz, seq, d = x.shape
    depth = norm1_g.shape[0]
    chunks = _time_chunks(seq)
    h = x
    for layer in range(depth):
        last = layer + 1 == depth
        kv_w, main_w = _in_weights(w_in[layer], seq)
        s5p = _s5_params(a_re[layer], a_im[layer], log_dt[layer], b_re[layer], b_im[layer], c_re[layer],
                         c_im[layer], d_skip[layer], w_glu[layer], nb=bsz)
        n_state = s5p[4].shape[1]
        wsu = w_ssm_up[layer].astype(BF16)
        wau = w_attn_up[layer].astype(BF16)
        wo = w_out[layer].astype(BF16)
        wq = peer_wq[layer].astype(BF16)
        k1 = peer_k1[layer].astype(BF16)
        k2 = peer_k2[layer].astype(BF16)
        u_pk = _pack_bf16_pairs(peer_u[layer])
        v_pk = _pack_bf16_pairs(peer_v[layer])
        kv_tm = min(ROW_TILE, seq)
        head_len = 2 * DSA_KT
        split_kv = len(chunks) > 1 and chunks[0][1] <= head_len < seq
        k4, vt, ki = _kvproj(h, norm1_g[layer], kv_w, norm1_g[layer], seq=head_len if split_kv else seq,
                             tm=kv_tm, kt=DSA_KT)
        st_re = jnp.zeros((bsz, n_state), F32)
        st_im = jnp.zeros((bsz, n_state), F32)
        outs = []
        routed = ki
        peer_outs = []
        for c, (s0, sc) in enumerate(chunks):
            tm = math.gcd(math.gcd(s0, sc), ROW_TILE)
            if split_kv and s0 + sc > head_len:
                k4, vt, ki = _kvproj(h, norm1_g[layer], kv_w, routed, seq=seq, tm=kv_tm, kt=DSA_KT)
                split_kv = False
            after = (routed, peer_outs[c - SC_LAG] if c >= SC_LAG else ki)
            u, q, qi, wit, gs, ga = _inproj(h, norm1_g[layer], main_w, after, s0=s0, sc=sc, tm=tm)
            d_ssm = u.shape[-1]
            u_tb = u.transpose(1, 0, 2).reshape(sc * bsz, d_ssm)
            y_tb, st_re, st_im = _s5(u_tb, st_re, st_im, s5p, nb=bsz, tc=64)
            ys = y_tb.reshape(sc, bsz, d_ssm).transpose(1, 0, 2)
            ya = _dsa(q, qi, wit, ki, k4, vt, s0=s0, seq_total=seq, tq=DSA_TQ, kt=DSA_KT)
            hm, x_pk, qp = _merge(h, ys, ya, gs, ga, wsu, wau, wo, norm2_g[layer], wq, s0=s0, tm=tm)
            nt = bsz * sc
            tail_dep = peer_outs[-1] if (c + 1 == len(chunks) and peer_outs) else k1
            e_t, g_t = _route(qp.reshape(nt, -1), k1, k2, tail_dep, tt=256)
            routed = e_t
            po = _peer_sc(x_pk.reshape(-1, LANES), e_t.T, g_t.T, u_pk, v_pk)
            peer_outs.append(po)
            hm2 = hm.reshape(nt, d)
            o = _final(hm2, po, norm_f_g, tm=tm) if last else hm2 + po.reshape(nt, d)
            outs.append(o.reshape(bsz, sc, d))
        h = jnp.concatenate(outs, axis=1)
    return h
```

```python
import functools
import math

import numpy as np
import jax
import jax.numpy as jnp
from jax import lax
from jax.experimental import pallas as pl
from jax.experimental.pallas import tpu as pltpu
from jax.experimental.pallas import tpu_sc as plsc

F32 = jnp.float32
BF16 = jnp.bfloat16
I32 = jnp.int32

SSM_GROUP = 16
SSM_STATE = 64
ATTN_HEADS = 8
ATTN_KV_HEADS = 2
HEAD_DIM = 64
IDX_HEADS = 8
IDX_DIM = 32
TOPK_MAX = 256
ROPE_THETA = 10000.0
NEG_BIG = -1e30
PEER_HEADS = 8
PEER_KEYS = 128
PEER_KEY_DIM = 128
PEER_TOPK = 16
NORM_EPS = 1e-6

TIME_SPLIT_32NDS = (1, 3, 4, 4, 4, 4, 4, 3, 2, 2, 1)
SC_LAG = 3
ROW_TILE = 512
S5_STEPS = 64
ROUTE_TOKENS = 256
DSA_TQ = 128
DSA_KT = 256
LANES = 128
INT_MIN = -(2 ** 31)
VMEM_LIMIT = 56 * 1024 * 1024


def _time_chunks(seq):
    unit = seq // 32
    if seq % 32 == 0 and unit % DSA_TQ == 0:
        sizes = [f * unit for f in TIME_SPLIT_32NDS]
    else:
        step = min(ROW_TILE, seq)
        sizes = [step] * (seq // step)
    assert sum(sizes) == seq
    starts = np.cumsum([0] + sizes[:-1]).tolist()
    return list(zip(starts, sizes))


def _cparams(sem):
    return pltpu.CompilerParams(dimension_semantics=sem, vmem_limit_bytes=VMEM_LIMIT)


def _gelu_tanh(x):
    return 0.5 * x * (1.0 + jnp.tanh(math.sqrt(2.0 / math.pi) * (x + 0.044715 * (x * x * x))))


def _sigmoid(x):
    return 1.0 / (1.0 + jnp.exp(-x))


def _rms(x, g):
    return x * lax.rsqrt(jnp.mean(x * x, axis=-1, keepdims=True) + NORM_EPS) * g


def _rot_cols(w, hd):
    d, n = w.shape
    w3 = w.reshape(d, n // hd, hd)
    half = hd // 2
    return jnp.concatenate([-w3[..., half:], w3[..., :half]], axis=-1).reshape(d, n)


def _rope_full(seq, hd, heads):
    pos = jnp.arange(seq, dtype=F32)
    inv = ROPE_THETA ** (-jnp.arange(0, hd, 2, dtype=F32) / hd)
    ang = pos[:, None] * inv[None, :]
    c = jnp.concatenate([jnp.cos(ang), jnp.cos(ang)], axis=-1)
    s = jnp.concatenate([jnp.sin(ang), jnp.sin(ang)], axis=-1)
    return jnp.tile(c, (1, heads)), jnp.tile(s, (1, heads))


def _in_weights(w_in, seq):
    d = w_in.shape[0]
    d_ssm = d // 2
    d_q = ATTN_HEADS * HEAD_DIM
    d_kv = ATTN_KV_HEADS * HEAD_DIM
    d_qi = IDX_HEADS * IDX_DIM
    splits = (d_ssm, d_q, d_kv, d_kv, d_qi, IDX_DIM, IDX_HEADS, d, d)
    offs = np.cumsum(splits)[:-1].tolist()
    wu, wq, wk, wv, wqi, wki, wwi, wgs, wga = jnp.split(w_in, offs, axis=1)
    pad = jnp.zeros((d, 128 - IDX_DIM), F32)
    cq, sq = _rope_full(seq, HEAD_DIM, ATTN_HEADS)
    ck, sk = _rope_full(seq, HEAD_DIM, ATTN_KV_HEADS)
    cqi, sqi = _rope_full(seq, IDX_DIM, IDX_HEADS)
    cki, ski = _rope_full(seq, IDX_DIM, 1)
    tpad = jnp.zeros((seq, 128 - IDX_DIM), F32)
    kv = dict(
        w=jnp.concatenate([wk, wki, pad], axis=1).astype(BF16),
        wvt=wv.T.astype(BF16),
        wr=jnp.concatenate([_rot_cols(wk, HEAD_DIM), _rot_cols(wki, IDX_DIM), pad], axis=1).astype(BF16),
        cs=jnp.concatenate([ck, cki, tpad], axis=1), sn=jnp.concatenate([sk, ski, tpad], axis=1))
    main = dict(
        w=jnp.concatenate([wu, wq, wqi, wgs, wga], axis=1).astype(BF16),
        wr=jnp.concatenate([_rot_cols(wq, HEAD_DIM), _rot_cols(wqi, IDX_DIM)], axis=1).astype(BF16),
        cs=jnp.concatenate([cq, cqi], axis=1), sn=jnp.concatenate([sq, sqi], axis=1),
        wwit=wwi.T.astype(BF16))
    return kv, main


def _kvproj_kernel(x_ref, g_ref, w_ref, wr_ref, wvt_ref, cs_ref, sn_ref, after_ref, k_ref, vt_ref, ki_ref,
                   *, d_kv, kt):
    del after_ref
    xb = _rms(x_ref[0], g_ref[...]).astype(BF16)

    def mm(ref, lo, n):
        return jnp.dot(xb, ref[:, lo:lo + n], preferred_element_type=F32)

    k = mm(w_ref, 0, d_kv) * cs_ref[:, :d_kv] + mm(wr_ref, 0, d_kv) * sn_ref[:, :d_kv]
    for n in range(ATTN_KV_HEADS):
        k_ref[0, n] = k[:, n * HEAD_DIM:(n + 1) * HEAD_DIM].astype(BF16)
    vt = lax.dot_general(wvt_ref[...], xb, (((1,), (1,)), ((), ())), preferred_element_type=F32)
    for n in range(ATTN_KV_HEADS):
        for j in range(vt.shape[1] // kt):
            vt_ref[0, n, j] = vt[n * HEAD_DIM:(n + 1) * HEAD_DIM, j * kt:(j + 1) * kt].astype(BF16)
    kiw = (mm(w_ref, d_kv, 128) * cs_ref[:, d_kv:d_kv + 128]
           + mm(wr_ref, d_kv, 128) * sn_ref[:, d_kv:d_kv + 128])
    ki_ref[0] = kiw[:, :IDX_DIM].astype(BF16)


def _kvproj(x, norm_g, kv, after, seq, tm, kt):
    bsz, _, d = x.shape
    d_kv = ATTN_KV_HEADS * HEAD_DIM
    full = lambda a: pl.BlockSpec(a.shape, lambda s, b: (0,) * a.ndim)
    g = norm_g.reshape(1, d)
    ncs = kv["cs"].shape[1]
    return pl.pallas_call(
        functools.partial(_kvproj_kernel, d_kv=d_kv, kt=kt),
        grid=(seq // tm, bsz),
        in_specs=[pl.BlockSpec((1, tm, d), lambda s, b: (b, s, 0)), full(g), full(kv["w"]), full(kv["wr"]),
                  full(kv["wvt"]),
                  pl.BlockSpec((tm, ncs), lambda s, b: (s, 0)), pl.BlockSpec((tm, ncs), lambda s, b: (s, 0)),
                  pl.BlockSpec(memory_space=pl.ANY)],
        out_specs=[pl.BlockSpec((1, ATTN_KV_HEADS, tm, HEAD_DIM), lambda s, b: (b, 0, s, 0)),
                   pl.BlockSpec((1, ATTN_KV_HEADS, tm // kt, HEAD_DIM, kt), lambda s, b: (b, 0, s, 0, 0)),
                   pl.BlockSpec((1, tm, IDX_DIM), lambda s, b: (b, s, 0))],
        out_shape=[jax.ShapeDtypeStruct((bsz, ATTN_KV_HEADS, seq, HEAD_DIM), BF16),
                   jax.ShapeDtypeStruct((bsz, ATTN_KV_HEADS, seq // kt, HEAD_DIM, kt), BF16),
                   jax.ShapeDtypeStruct((bsz, seq, IDX_DIM), BF16)],
        compiler_params=_cparams(("arbitrary", "arbitrary")),
        name="kvproj",
    )(x, g, kv["w"], kv["wr"], kv["wvt"], kv["cs"], kv["sn"], after)


def _inproj_kernel(x_ref, g_ref, w_ref, wr_ref, cs_ref, sn_ref, wwit_ref, after_tc_ref, after_sc_ref,
                   u_ref, q_ref, qi_ref, wit_ref, gs_ref, ga_ref, *, d_ssm, d_q, d_qi, d_model, q_scale, wi_scale):
    del after_tc_ref, after_sc_ref
    xb = _rms(x_ref[0], g_ref[...]).astype(BF16)

    def mm(ref, lo, n):
        return jnp.dot(xb, ref[:, lo:lo + n], preferred_element_type=F32)

    o = 0
    u_ref[0] = mm(w_ref, o, d_ssm).astype(BF16)
    o += d_ssm
    q = mm(w_ref, o, d_q) * cs_ref[:, :d_q] + mm(wr_ref, 0, d_q) * sn_ref[:, :d_q]
    q_ref[0] = (q * q_scale).astype(BF16)
    o += d_q
    qi = mm(w_ref, o, d_qi) * cs_ref[:, d_q:d_q + d_qi] + mm(wr_ref, d_q, d_qi) * sn_ref[:, d_q:d_q + d_qi]
    qi_ref[0] = qi.astype(BF16)
    o += d_qi
    gs_ref[0] = _sigmoid(mm(w_ref, o, d_model)).astype(BF16)
    o += d_model
    ga_ref[0] = _sigmoid(mm(w_ref, o, d_model)).astype(BF16)
    wit_ref[0] = lax.dot_general(wwit_ref[...], xb, (((1,), (1,)), ((), ())),
                                 preferred_element_type=F32) * wi_scale


def _inproj(x, norm_g, main, after, s0, sc, tm):
    bsz, _, d = x.shape
    d_ssm = d // 2
    d_q = ATTN_HEADS * HEAD_DIM
    d_qi = IDX_HEADS * IDX_DIM
    i0 = s0 // tm
    kern = functools.partial(
        _inproj_kernel, d_ssm=d_ssm, d_q=d_q, d_qi=d_qi, d_model=d,
        q_scale=HEAD_DIM ** -0.5 * math.log2(math.e), wi_scale=(IDX_HEADS ** -0.5) * (IDX_DIM ** -0.5))
    tok = lambda n: pl.BlockSpec((1, tm, n), lambda s, b: (b, s, 0))
    full = lambda a: pl.BlockSpec(a.shape, lambda s, b: (0,) * a.ndim)
    g = norm_g.reshape(1, d)
    ncs = main["cs"].shape[1]
    outs = [(d_ssm, BF16), (d_q, BF16), (d_qi, BF16)]
    return pl.pallas_call(
        kern,
        grid=(sc // tm, bsz),
        in_specs=[pl.BlockSpec((1, tm, d), lambda s, b: (b, s + i0, 0)), full(g), full(main["w"]), full(main["wr"]),
                  pl.BlockSpec((tm, ncs), lambda s, b: (s + i0, 0)),
                  pl.BlockSpec((tm, ncs), lambda s, b: (s + i0, 0)), full(main["wwit"]),
                  pl.BlockSpec(memory_space=pl.ANY), pl.BlockSpec(memory_space=pl.ANY)],
        out_specs=[tok(n) for n, _ in outs] + [pl.BlockSpec((1, IDX_HEADS, tm), lambda s, b: (b, 0, s)),
                                                tok(d), tok(d)],
        out_shape=[jax.ShapeDtypeStruct((bsz, sc, n), dt) for n, dt in outs]
        + [jax.ShapeDtypeStruct((bsz, IDX_HEADS, sc), F32),
           jax.ShapeDtypeStruct((bsz, sc, d), BF16), jax.ShapeDtypeStruct((bsz, sc, d), BF16)],
        compiler_params=_cparams(("arbitrary", "arbitrary")),
        name="inproj",
    )(x, g, main["w"], main["wr"], main["cs"], main["sn"], main["wwit"], *after)


def _s5_kernel(u_ref, sre_in, sim_in, bre_ref, bim_ref, cre_ref, cim_ref, are_ref, aim_ref, dsk_ref, wglu_ref,
               y_ref, st_re, st_im, sre, sim, *, tc, nb, lane_chunk):
    @pl.when(pl.program_id(0) == 0)
    def _():
        st_re[...] = sre_in[...]
        st_im[...] = sim_in[...]

    u = u_ref[...]
    n_half = bre_ref.shape[0]
    hin = bre_ref.shape[1]
    hst = bre_ref.shape[2]
    for h in range(n_half):
        uh = u[:, h * hin:(h + 1) * hin]
        sre[:, h * hst:(h + 1) * hst] = jnp.dot(uh, bre_ref[h], preferred_element_type=F32)
        sim[:, h * hst:(h + 1) * hst] = jnp.dot(uh, bim_ref[h], preferred_element_type=F32)

    n_state = sre.shape[1]
    for c in range(n_state // lane_chunk):
        cols = slice(c * lane_chunk, (c + 1) * lane_chunk)
        ar = are_ref[:, cols]
        ai = aim_ref[:, cols]

        def step(t, carry, cols=cols, ar=ar, ai=ai):
            sr, si = carry
            r0 = pl.multiple_of(t * nb, nb)
            nr = ar * sr - ai * si + sre[pl.ds(r0, nb), cols]
            ni = ar * si + ai * sr + sim[pl.ds(r0, nb), cols]
            sre[pl.ds(r0, nb), cols] = nr
            sim[pl.ds(r0, nb), cols] = ni
            return nr, ni

        sr, si = lax.fori_loop(0, tc, step, (st_re[:, cols], st_im[:, cols]), unroll=4)
        st_re[:, cols] = sr
        st_im[:, cols] = si

    ys = []
    for h in range(n_half):
        srh = sre[:, h * hst:(h + 1) * hst].astype(BF16)
        sih = sim[:, h * hst:(h + 1) * hst].astype(BF16)
        ys.append(jnp.dot(srh, cre_ref[h], preferred_element_type=F32)
                  - jnp.dot(sih, cim_ref[h], preferred_element_type=F32))
    y = jnp.concatenate(ys, axis=-1) + dsk_ref[...] * u.astype(F32)
    y = _gelu_tanh(y)
    gate = jnp.dot(y.astype(BF16), wglu_ref[...], preferred_element_type=F32)
    y_ref[...] = (y * _sigmoid(gate)).astype(BF16)


def _s5_params(a_re, a_im, log_dt, b_re, b_im, c_re, c_im, d_skip, w_glu, nb):
    groups = a_re.shape[0]
    d_ssm = groups * SSM_GROUP
    n_state = groups * SSM_STATE
    lam = lax.complex(a_re, a_im)
    dt = jnp.exp(log_dt)[:, None]
    a_bar = jnp.exp(lam * dt)
    b_bar = ((a_bar - 1.0) / lam)[..., None] * lax.complex(b_re, b_im)
    gh = min(groups, 256 // SSM_GROUP)
    n_half = groups // gh
    eye = jnp.eye(gh, dtype=F32)

    def bmat(bb):
        b4 = bb.reshape(n_half, gh, SSM_STATE, SSM_GROUP)
        return jnp.einsum('hgpc,gk->hgckp', b4, eye).reshape(n_half, gh * SSM_GROUP, gh * SSM_STATE)

    def cmat(cc):
        c4 = cc.reshape(n_half, gh, SSM_GROUP, SSM_STATE)
        return jnp.einsum('hgcp,gk->hgpkc', c4, eye).reshape(n_half, gh * SSM_STATE, gh * SSM_GROUP)

    return (bmat(jnp.real(b_bar)).astype(BF16), bmat(jnp.imag(b_bar)).astype(BF16),
            cmat(c_re).astype(BF16), cmat(c_im).astype(BF16),
            jnp.broadcast_to(jnp.real(a_bar).reshape(1, n_state), (nb, n_state)),
            jnp.broadcast_to(jnp.imag(a_bar).reshape(1, n_state), (nb, n_state)),
            d_skip.reshape(1, d_ssm), w_glu.astype(BF16))


def _s5(u_tb, st_re, st_im, params, nb, tc):
    rows, d_ssm = u_tb.shape
    n_state = st_re.shape[1]
    blk = tc * nb
    full = lambda a: pl.BlockSpec(a.shape, lambda i: (0,) * a.ndim)
    st_spec = pl.BlockSpec((nb, n_state), lambda i: (0, 0))
    kern = functools.partial(_s5_kernel, tc=tc, nb=nb, lane_chunk=512)
    return pl.pallas_call(
        kern,
        grid=(rows // blk,),
        in_specs=[pl.BlockSpec((blk, d_ssm), lambda i: (i, 0)), st_spec, st_spec] + [full(p) for p in params],
        out_specs=[pl.BlockSpec((blk, d_ssm), lambda i: (i, 0)), st_spec, st_spec],
        out_shape=[jax.ShapeDtypeStruct((rows, d_ssm), BF16),
                   jax.ShapeDtypeStruct((nb, n_state), F32), jax.ShapeDtypeStruct((nb, n_state), F32)],
        scratch_shapes=[pltpu.VMEM((blk, n_state), F32), pltpu.VMEM((blk, n_state), F32)],
        compiler_params=_cparams(("arbitrary",)),
        name="s5",
    )(u_tb, st_re, st_im, *params)


PART_ROWS = 32


def _dsa_kernel(qi_ref, wit_ref, q_ref, ki_ref, k_ref, vt_ref, o_ref, key_s, bias_s, lg_s,
                *, qb0, tq, kt, sub, topk, seq_bits):
    qb = pl.program_id(1) + qb0
    nkt = ((qb * tq + tq + sub * kt - 1) // (sub * kt)) * sub
    q_pos = qb * tq + lax.broadcasted_iota(I32, (1, tq), 1)
    k_eff = jnp.minimum(topk, q_pos + 1).astype(F32)

    qi = qi_ref[0]
    wit = wit_ref[0]
    qipair = [jnp.concatenate([qi[:, (2 * hp) * IDX_DIM:(2 * hp + 1) * IDX_DIM],
                               qi[:, (2 * hp + 1) * IDX_DIM:(2 * hp + 2) * IDX_DIM]], axis=0)
              for hp in range(IDX_HEADS // 2)]

    def key_pos(t):
        return t * kt + lax.broadcasted_iota(I32, (kt, tq), 0)

    def score_tile(t2, _):
        for hf in range(sub):
            r0 = pl.multiple_of((t2 * sub + hf) * kt, kt)
            ki_t = ki_ref[0, pl.ds(r0, kt), :]
            for hp in range(IDX_HEADS // 2):
                lg_s[hf, hp] = lax.dot_general(ki_t, qipair[hp], (((1,), (1,)), ((), ())),
                                               preferred_element_type=F32)
        for hf in range(sub):
            t = t2 * sub + hf
            r0 = pl.multiple_of(t * kt, kt)
            sc = jnp.zeros((kt, tq), F32)
            for hp in range(IDX_HEADS // 2):
                rel = lg_s[hf, hp]
                sc = sc + jnp.maximum(rel[:, :tq], 0.0) * wit[2 * hp:2 * hp + 1, :]
                sc = sc + jnp.maximum(rel[:, tq:], 0.0) * wit[2 * hp + 1:2 * hp + 2, :]
            bits = lax.bitcast_convert_type(sc, I32)
            key = jnp.where(bits < 0, bits ^ jnp.int32(0x7FFFFFFF), bits)
            key = jnp.where(key_pos(t) <= q_pos, key, jnp.int32(INT_MIN))
            key_s[pl.ds(r0, kt), :] = key
        return 0

    lax.fori_loop(0, nkt // sub, score_tile, 0)

    def count(pred_fn):
        def body(t, acc):
            r0 = pl.multiple_of(t * kt, kt)
            m = pred_fn(key_s[pl.ds(r0, kt), :], t)
            ones = jnp.where(m, 1.0, 0.0).reshape(kt // PART_ROWS, PART_ROWS, tq)
            return acc + jnp.sum(ones, axis=0)
        acc = lax.fori_loop(0, nkt, body, jnp.zeros((PART_ROWS, tq), F32))
        return jnp.sum(acc, axis=0, keepdims=True)

    def bit_step(i, u):
        bit = jnp.left_shift(jnp.int32(1), 31 - i)
        cand_u = u | bit
        cand_s = cand_u ^ jnp.int32(INT_MIN)
        cnt = count(lambda kk, t: kk >= cand_s)
        return jnp.where(cnt >= k_eff, cand_u, u)

    u_thr = lax.fori_loop(0, 32, bit_step, jnp.zeros((1, tq), I32))
    thr = u_thr ^ jnp.int32(INT_MIN)

    cnt_ge = count(lambda kk, t: kk >= thr)
    cnt_gt = count(lambda kk, t: kk > thr)
    need_eq = k_eff - cnt_gt
    has_tie = jnp.max(cnt_ge - k_eff) > 0.0

    def tie_cut():
        def pos_step(i, c):
            bit = jnp.left_shift(jnp.int32(1), seq_bits - 1 - i)
            cand = c | bit
            cnt = count(lambda kk, t: (kk == thr) & (key_pos(t) < cand))
            return jnp.where(cnt < need_eq, cand, c)
        return lax.fori_loop(0, seq_bits, pos_step, jnp.zeros((1, tq), I32))

    cut = lax.cond(has_tie, tie_cut, lambda: jnp.full((1, tq), 2 ** seq_bits, I32))

    def bias_tile(t, _):
        r0 = pl.multiple_of(t * kt, kt)
        key = key_s[pl.ds(r0, kt), :]
        sel = (key > thr) | ((key == thr) & (key_pos(t) <= cut))
        bias_s[pl.ds(r0, kt), :] = jnp.where(sel, 0.0, NEG_BIG)
        return 0

    lax.fori_loop(0, nkt, bias_tile, 0)

    q = q_ref[0]
    grp = ATTN_HEADS // ATTN_KV_HEADS
    pairs_per_kv = grp // 2
    n_unit = ATTN_KV_HEADS * pairs_per_kv
    wq = 2 * tq
    qpair = [jnp.concatenate([q[:, (2 * u) * HEAD_DIM:(2 * u + 1) * HEAD_DIM],
                              q[:, (2 * u + 1) * HEAD_DIM:(2 * u + 2) * HEAD_DIM]], axis=0)
             for u in range(n_unit)]

    def col_reduce(x, op):
        part = op(x.reshape(kt // PART_ROWS, PART_ROWS, wq), axis=0)
        return op(part, axis=0, keepdims=True)

    def attn_tile(t, carry):
        ms, ls, accs = list(carry[0]), list(carry[1]), list(carry[2])
        for hf in range(sub):
            r0 = pl.multiple_of((t * sub + hf) * kt, kt)
            bias = bias_s[pl.ds(r0, kt), :]
            bias2 = jnp.concatenate([bias, bias], axis=1)
            for u in range(n_unit):
                k_t = k_ref[0, u // pairs_per_kv, pl.ds(r0, kt), :]
                lg_s[hf, u] = lax.dot_general(k_t, qpair[u], (((1,), (1,)), ((), ())),
                                              preferred_element_type=F32) + bias2
        for hf in range(sub):
            for u in range(n_unit):
                lg = lg_s[hf, u]
                m_new = jnp.maximum(ms[u], col_reduce(lg, jnp.max))
                p = jnp.exp2(lg - m_new)
                alpha = jnp.exp2(ms[u] - m_new)
                v_t = vt_ref[0, u // pairs_per_kv, t * sub + hf]
                ls[u] = alpha * ls[u] + col_reduce(p, jnp.sum)
                accs[u] = alpha * accs[u] + jnp.dot(v_t, p.astype(BF16), preferred_element_type=F32)
                ms[u] = m_new
        return tuple(ms), tuple(ls), tuple(accs)

    init = (tuple(jnp.full((1, wq), NEG_BIG, F32) for _ in range(n_unit)),
            tuple(jnp.zeros((1, wq), F32) for _ in range(n_unit)),
            tuple(jnp.zeros((HEAD_DIM, wq), F32) for _ in range(n_unit)))
    _, ls, accs = lax.fori_loop(0, nkt // sub, attn_tile, init)
    for n in range(ATTN_KV_HEADS):
        o_ref[0, 0, n] = jnp.concatenate([accs[n * pairs_per_kv + pg] / ls[n * pairs_per_kv + pg]
                                          for pg in range(pairs_per_kv)], axis=1).astype(BF16)


def _dsa(q, qi, wit, ki, k4, vt, s0, seq_total, tq, kt):
    bsz, sc, _ = q.shape
    seq = ki.shape[1]
    topk = min(TOPK_MAX, seq_total // 4)
    nqb = sc // tq
    grp = ATTN_HEADS // ATTN_KV_HEADS
    seq_bits = int(math.log2(seq))
    assert 2 ** seq_bits == seq
    sub = 2 if seq % (2 * kt) == 0 else 1
    n_unit = ATTN_HEADS // 2
    assert IDX_HEADS // 2 <= n_unit
    kern =functools.partial(_dsa_kernel, qb0=s0 // tq, tq=tq, kt=kt, sub=sub, topk=topk, seq_bits=seq_bits)
    o_t = pl.pallas_call(
        kern,
        grid=(bsz, nqb),
        in_specs=[pl.BlockSpec((1, tq, IDX_HEADS * IDX_DIM), lambda b, j: (b, j, 0)),
                  pl.BlockSpec((1, IDX_HEADS, tq), lambda b, j: (b, 0, j)),
                  pl.BlockSpec((1, tq, ATTN_HEADS * HEAD_DIM), lambda b, j: (b, j, 0)),
                  pl.BlockSpec((1, seq, IDX_DIM), lambda b, j: (b, 0, 0)),
                  pl.BlockSpec((1, ATTN_KV_HEADS, seq, HEAD_DIM), lambda b, j: (b, 0, 0, 0)),
                  pl.BlockSpec((1, ATTN_KV_HEADS, seq // kt, HEAD_DIM, kt), lambda b, j: (b, 0, 0, 0, 0))],
        out_specs=pl.BlockSpec((1, 1, ATTN_KV_HEADS, HEAD_DIM, grp * tq), lambda b, j: (b, j, 0, 0, 0)),
        out_shape=jax.ShapeDtypeStruct((bsz, nqb, ATTN_KV_HEADS, HEAD_DIM, grp * tq), BF16),
        scratch_shapes=[pltpu.VMEM((seq, tq), I32), pltpu.VMEM((seq, tq), F32),
                        pltpu.VMEM((sub, n_unit, kt, 2 * tq), F32)],
        compiler_params=_cparams(("arbitrary", "arbitrary")),
        name="dsa",
    )(qi, wit, q, ki, k4, vt)
    o = o_t.reshape(bsz, nqb, ATTN_KV_HEADS, HEAD_DIM, grp, tq).transpose(0, 1, 5, 2, 4, 3)
    return o.reshape(bsz, sc, ATTN_HEADS * HEAD_DIM)


def _merge_kernel(x_ref, ys_ref, ya_ref, gs_ref, ga_ref, wsu_ref, wau_ref, wout_ref, g2_ref, wq_ref,
                  h_ref, hn_ref, qp_ref):
    ms = jnp.dot(ys_ref[0], wsu_ref[...], preferred_element_type=F32)
    ma = jnp.dot(ya_ref[0], wau_ref[...], preferred_element_type=F32)
    merged = gs_ref[0].astype(F32) * ms + ga_ref[0].astype(F32) * ma
    h = x_ref[0] + jnp.dot(merged.astype(BF16), wout_ref[...], preferred_element_type=F32)
    h_ref[0] = h
    hb = _rms(h, g2_ref[...]).astype(BF16)
    words = _pack_words(hb)
    half = words.shape[1]
    n_seg = half // LANES
    tm = words.shape[0]
    for p in range(n_seg):
        hn_ref[0, pl.ds(p, tm, stride=n_seg), :] = words[:, p * LANES:(p + 1) * LANES]
    qp_ref[0] = jnp.dot(hb, wq_ref[...], preferred_element_type=F32).astype(BF16)


def _merge(x, ys, ya, gs, ga, wsu, wau, wo, norm2_g, wq, s0, tm):
    bsz, sc, _ = ya.shape
    d = x.shape[2]
    i0 = s0 // tm
    loc = lambda a: pl.BlockSpec((1, tm, a.shape[2]), lambda b, i: (b, i, 0))
    full = lambda a: pl.BlockSpec(a.shape, lambda b, i: (0,) * a.ndim)
    g2 = norm2_g.reshape(1, d)
    nq = wq.shape[1]
    n_seg = d // 2 // LANES
    out = lambda n: pl.BlockSpec((1, tm, n), lambda b, i: (b, i, 0))
    return pl.pallas_call(
        _merge_kernel,
        grid=(bsz, sc // tm),
        in_specs=[pl.BlockSpec((1, tm, d), lambda b, i: (b, i + i0, 0)), loc(ys), loc(ya), loc(gs), loc(ga),
                  full(wsu), full(wau), full(wo), full(g2), full(wq)],
        out_specs=[out(d), pl.BlockSpec((1, tm * n_seg, LANES), lambda b, i: (b, i, 0)), out(nq)],
        out_shape=[jax.ShapeDtypeStruct((bsz, sc, d), F32), jax.ShapeDtypeStruct((bsz, sc * n_seg, LANES), I32),
                   jax.ShapeDtypeStruct((bsz, sc, nq), BF16)],
        compiler_params=_cparams(("arbitrary", "arbitrary")),
        name="merge",
    )(x, ys, ya, gs, ga, wsu, wau, wo, g2, wq)


def _cand_layout():
    blocks = []
    blocks.append((0, 16, 16))
    for i in range(1, 8):
        blocks.append((i, 8, PEER_TOPK // (i + 1)))
    blocks.append((None, 8, 8))
    return blocks


def _top_rows(s, order, payload, k):
    big = jnp.float32(3e38)
    vals, pays = [], []
    for _ in range(k):
        m = jnp.max(s, axis=0, keepdims=True)
        o = jnp.min(jnp.where(s == m, order, big), axis=0, keepdims=True)
        hit = order == o
        pays.append(o if payload is order else jnp.min(jnp.where(hit, payload, big), axis=0, keepdims=True))
        vals.append(m)
        s = jnp.where(hit, -jnp.inf, s)
    return jnp.concatenate(vals, axis=0), jnp.concatenate(pays, axis=0)


def _route_kernel(qp_ref, k1_ref, k2_ref, e_ref, g_ref, *, tt):
    qp = qp_ref[...]
    kd = PEER_KEY_DIM
    rows_k = lax.broadcasted_iota(I32, (PEER_KEYS, tt), 0).astype(F32)
    for h in range(PEER_HEADS):
        q1 = qp[:, (2 * h) * kd:(2 * h + 1) * kd]
        q2 = qp[:, (2 * h + 1) * kd:(2 * h + 2) * kd]
        s1 = lax.dot_general(k1_ref[h], q1, (((1,), (1,)), ((), ())), preferred_element_type=F32)
        s2 = lax.dot_general(k2_ref[h], q2, (((1,), (1,)), ((), ())), preferred_element_type=F32)
        v1, i1 = _top_rows(s1, rows_k, rows_k, PEER_TOPK)
        v2, i2 = _top_rows(s2, rows_k, rows_k, PEER_TOPK)
        cs, ce, co = [], [], []
        for i, rows, valid in _cand_layout():
            r = lax.broadcasted_iota(I32, (rows, tt), 0).astype(F32)
            if i is None:
                val = v1[8:16] + v2[0:1]
                eid = i1[8:16] * PEER_KEYS + i2[0:1]
                flat = (r + 8.0) * PEER_TOPK
            else:
                val = v1[i:i + 1] + v2[0:rows]
                eid = i1[i:i + 1] * PEER_KEYS + i2[0:rows]
                flat = r + float(i * PEER_TOPK)
                if valid < rows:
                    val = jnp.where(r < float(valid), val, -jnp.inf)
            cs.append(val)
            ce.append(eid)
            co.append(flat)
        cand = jnp.concatenate(cs, axis=0)
        top_s, top_e = _top_rows(cand, jnp.concatenate(co, axis=0), jnp.concatenate(ce, axis=0), PEER_TOPK)
        p = jnp.exp(top_s - top_s[0:1])
        gates = p / jnp.sum(p, axis=0, keepdims=True)
        e_ref[h * PEER_TOPK:(h + 1) * PEER_TOPK, :] = top_e.astype(I32)
        g_ref[h * PEER_TOPK:(h + 1) * PEER_TOPK, :] = gates


def _route(qp, k1, k2, tt):
    n, nq = qp.shape
    n_sel = PEER_HEADS * PEER_TOPK
    full = lambda a: pl.BlockSpec(a.shape, lambda i: (0,) * a.ndim)
    return pl.pallas_call(
        functools.partial(_route_kernel, tt=tt),
        grid=(n // tt,),
        in_specs=[pl.BlockSpec((tt, nq), lambda i: (i, 0)), full(k1), full(k2)],
        out_specs=[pl.BlockSpec((n_sel, tt), lambda i: (0, i)), pl.BlockSpec((n_sel, tt), lambda i: (0, i))],
        out_shape=[jax.ShapeDtypeStruct((n_sel, n), I32), jax.ShapeDtypeStruct((n_sel, n), F32)],
        compiler_params=_cparams(("arbitrary",)),
        name="route",
    )(qp, k1, k2)


def _final_kernel(h_ref, p_ref, g_ref, o_ref):
    tm, d = h_ref.shape
    n_seg = d // LANES
    p = jnp.concatenate([p_ref[pl.ds(k, tm, stride=n_seg), :] for k in range(n_seg)], axis=1)
    o_ref[...] = _rms(h_ref[...] + p, g_ref[...])


def _final(h, p, g, tm):
    n, d = h.shape
    row = pl.BlockSpec((tm, d), lambda i: (i, 0))
    return pl.pallas_call(
        _final_kernel,
        grid=(n // tm,),
        in_specs=[row, pl.BlockSpec((tm * (d // LANES), LANES), lambda i: (i, 0)),
                  pl.BlockSpec((1, d), lambda i: (0, 0))],
        out_specs=row,
        out_shape=jax.ShapeDtypeStruct((n, d), F32),
        compiler_params=_cparams(("arbitrary",)),
        name="final",
    )(h, p, g.reshape(1, d))


SC_CORES_V7X = 2
SC_SUBCORES_V7X = 16
SC_LANES_V7X = 16
PEER_TOK_BATCH = 32
PEER_ROW_CHUNK = 32
PEER_RING = 4


def _pack_words(x):
    bits = lax.bitcast_convert_type(x.astype(BF16).astype(F32), I32)
    half = bits.shape[1] // 2
    return (bits[:, half:] & jnp.int32(-65536)) | lax.shift_right_logical(bits[:, :half], 16)


def _pack_kernel(t_ref, o_ref):
    o_ref[...] = _pack_words(t_ref[...])


def _pack_bf16_pairs(t, rows=1024):
    e, d = t.shape
    return pl.pallas_call(
        _pack_kernel,
        grid=(e // rows,),
        in_specs=[pl.BlockSpec((rows, d), lambda i: (i, 0))],
        out_specs=pl.BlockSpec((rows, d // 2), lambda i: (i, 0)),
        out_shape=jax.ShapeDtypeStruct((e, d // 2), I32),
        compiler_params=_cparams(("arbitrary",)),
        name="pack_table",
    )(t)


def _unpack_pair(w):
    lo = lax.bitcast_convert_type(jnp.left_shift(w, 16), F32)
    hi = lax.bitcast_convert_type(w & jnp.int32(-65536), F32)
    return lo, hi


def _peer_sc_body(hn_hbm, e_hbm, g_hbm, u_hbm, v_hbm, out_hbm,
                  idx_v, gate_v, x_v, out_v, rows, p_v, act_v, sem, *, tpw, d, n_sel):
    nl = SC_LANES_V7X
    tb = PEER_TOK_BATCH
    rc = PEER_ROW_CHUNK
    n_chunk = n_sel // rc
    jobs_per_tok = 2 * n_chunk
    half = d // 2
    n_lane_blk = half // nl
    xs = half // LANES
    os_ = d // LANES
    wid =lax.axis_index("s") * SC_CORES_V7X + lax.axis_index("c")
    base = wid * tpw
    lane = lax.iota(I32, nl)
    zero = jnp.zeros((nl,), F32)
    c_gelu = 2.0 * math.sqrt(2.0 / math.pi)

    def gather_copy(tab_hbm, job):
        tok = job // jobs_per_tok
        c = (job % jobs_per_tok) % n_chunk
        b = job % PEER_RING
        return pltpu.make_async_copy(tab_hbm.at[idx_v.at[tok, pl.ds(c * rc, rc)]], rows.at[b], sem.at[b])

    def start(job):
        j = job % jobs_per_tok

        @pl.when(j < n_chunk)
        def _():
            gather_copy(u_hbm, job).start()

        @pl.when(j >= n_chunk)
        def _():
            gather_copy(v_hbm, job).start()

    def compute_u(tok, c, b):
        def rg_body(rg, _):
            r0 = rg * 8

            def jbody(j2, accs):
                off0 = j2 * (2 * nl)
                off1 = off0 + nl
                xrow = tok * xs + off0 // LANES
                xl = off0 % LANES
                x0 = plsc.bitcast(x_v[xrow, pl.ds(xl, nl)], BF16)
                x1 = plsc.bitcast(x_v[xrow, pl.ds(xl + nl, nl)], BF16)
                new = []
                for r in range(8):
                    w0 = plsc.bitcast(rows[b, r0 + r, pl.ds(off0, nl)], BF16)
                    w1 = plsc.bitcast(rows[b, r0 + r, pl.ds(off1, nl)], BF16)
                    lo, hi = _unpack_pair(plsc.bitcast(w0 * x0 + w1 * x1, I32))
                    new.append(accs[r] + (lo + hi))
                return tuple(new)

            accs = lax.fori_loop(0, n_lane_blk // 2, jbody, (zero,) * 8)
            for r in range(8):
                p_v[c * rc + r0 + r, :] = accs[r]
            return 0

        lax.fori_loop(0, rc // 8, rg_body, 0)

    def finish_act(tok):
        def eg_body(eg, _):
            e0 = eg * nl
            ridx = e0 + lane
            s = zero
            for l in range(nl):
                s = s + plsc.load_gather(p_v, [ridx, jnp.full((nl,), l, I32)])
            inner = c_gelu * (s + 0.044715 * (s * s * s))
            gl = s / (1.0 + jnp.exp(-inner))
            a = gl * gate_v[tok, pl.ds(e0, nl)]
            bits = lax.bitcast_convert_type(a, I32)
            rnd = bits + jnp.int32(0x7FFF) + (lax.shift_right_logical(bits, 16) & 1)
            hi16 = rnd & jnp.int32(-65536)
            act_v[pl.ds(e0, nl)] = hi16 | lax.shift_right_logical(hi16, 16)
            return 0

        lax.fori_loop(0, n_sel // nl, eg_body, 0)

        def zbody(j, _):
            off = j * nl
            out_v[tok * os_ + off // LANES, pl.ds(off % LANES, nl)] = zero
            return 0

        lax.fori_loop(0, d // nl, zbody, 0, unroll=4)

    def compute_v(tok, c, b):
        def rg_body(rg, _):
            r0 = rg * nl
            splat = [plsc.bitcast(plsc.load_gather(act_v, [jnp.full((nl,), 0, I32) + (c * rc + r0 + r)]), BF16)
                     for r in range(nl)]

            def tree(parts):
                while len(parts) > 1:
                    parts = [parts[i] + parts[i + 1] for i in range(0, len(parts), 2)]
                return parts[0]

            @plsc.parallel_loop(0, n_lane_blk, unroll=2)
            def _(j):
                off = j * nl
                los, his = [], []
                for r in range(0, nl, 2):
                    w0 = plsc.bitcast(rows[b, r0 + r, pl.ds(off, nl)], BF16)
                    w1 = plsc.bitcast(rows[b, r0 + r + 1, pl.ds(off, nl)], BF16)
                    lo, hi = _unpack_pair(plsc.bitcast(w0 * splat[r] + w1 * splat[r + 1], I32))
                    los.append(lo)
                    his.append(hi)
                orow = tok * os_ + off // LANES
                ol = off % LANES
                out_v[orow, pl.ds(ol, nl)] = out_v[orow, pl.ds(ol, nl)] + tree(los)
                out_v[orow + xs, pl.ds(ol, nl)] = out_v[orow + xs, pl.ds(ol, nl)] + tree(his)

            return 0

        lax.fori_loop(0, rc // nl, rg_body, 0)

    def batch_body(bi, _):
        t0 = base + bi * tb
        pltpu.sync_copy(e_hbm.at[pl.ds(t0, tb)], idx_v)
        pltpu.sync_copy(g_hbm.at[pl.ds(t0, tb)], gate_v)
        pltpu.sync_copy(hn_hbm.at[pl.ds(t0 * xs, tb * xs)], x_v)
        for pre in range(PEER_RING - 1):
            start(pre)

        def job_body(job, _):
            @pl.when(job + (PEER_RING - 1) < tb * jobs_per_tok)
            def _():
                start(job + (PEER_RING - 1))

            j = job % jobs_per_tok
            gather_copy(u_hbm, job).wait()
            tok = job // jobs_per_tok
            b = job % PEER_RING

            @pl.when(j < n_chunk)
            def _():
                compute_u(tok, j, b)

            @pl.when(j == n_chunk - 1)
            def _():
                finish_act(tok)

            @pl.when(j >= n_chunk)
            def _():
                compute_v(tok, j - n_chunk, b)

            return 0

        lax.fori_loop(0, tb * jobs_per_tok, job_body, 0)
        pltpu.sync_copy(out_v, out_hbm.at[pl.ds(t0 * os_, tb * os_)])
        return 0

    lax.fori_loop(0, tpw // tb, batch_body, 0)


def _peer_sc(x_pk, experts, gates, u_tab, v_tab):
    d = 2 * u_tab.shape[1]
    n = x_pk.shape[0] * LANES // (d // 2)
    n_sel = experts.shape[1]
    nw = SC_CORES_V7X * SC_SUBCORES_V7X
    tpw = n // nw
    mesh = plsc.VectorSubcoreMesh(core_axis_name="c", subcore_axis_name="s",
                                  num_cores=SC_CORES_V7X, num_subcores=SC_SUBCORES_V7X)
    body = functools.partial(_peer_sc_body, tpw=tpw, d=d, n_sel=n_sel)
    call = pl.kernel(
        body,
        out_type=jax.ShapeDtypeStruct((n * d // LANES, LANES), F32),
        mesh=mesh,
        scratch_types=[pltpu.VMEM((PEER_TOK_BATCH, n_sel), I32),
                       pltpu.VMEM((PEER_TOK_BATCH, n_sel), F32),
                       pltpu.VMEM((PEER_TOK_BATCH * d // 2 // LANES, LANES), I32),
                       pltpu.VMEM((PEER_TOK_BATCH * d // LANES, LANES), F32),
                       pltpu.VMEM((PEER_RING, PEER_ROW_CHUNK, d // 2), I32),
                       pltpu.VMEM((n_sel, SC_LANES_V7X), F32),
                       pltpu.VMEM((n_sel,), I32),
                       pltpu.SemaphoreType.DMA((PEER_RING,))],
        compiler_params=pltpu.CompilerParams(needs_layout_passes=False, use_tc_tiling_on_sc=False),
        name="peer_sc",
    )
    return call(x_pk, experts, gates, u_tab, v_tab)


def kernel(x, norm1_g, w_in, a_re, a_im, log_dt, b_re, b_im, c_re, c_im, d_skip, w_glu, w_ssm_up, w_attn_up,
           w_out, norm2_g, peer_wq, peer_k1, peer_k2, peer_u, peer_v, norm_f_g):
    bsz, seq, d = x.shape
    depth = norm1_g.shape[0]
    chunks = _time_chunks(seq)
    h = x
    for layer in range(depth):
        last = layer + 1 == depth
        kv_w, main_w = _in_weights(w_in[layer], seq)
        s5p = _s5_params(a_re[layer], a_im[layer], log_dt[layer], b_re[layer], b_im[layer], c_re[layer],
                         c_im[layer], d_skip[layer], w_glu[layer], nb=bsz)
        n_state = s5p[4].shape[1]
        wsu = w_ssm_up[layer].astype(BF16)
        wau = w_attn_up[layer].astype(BF16)
        wo = w_out[layer].astype(BF16)
        wq = peer_wq[layer].astype(BF16)
        k1 = peer_k1[layer].astype(BF16)
        k2 = peer_k2[layer].astype(BF16)
        u_pk = _pack_bf16_pairs(peer_u[layer])
        v_pk = _pack_bf16_pairs(peer_v[layer])
        kv_tm = min(ROW_TILE, seq)
        head_len = 2 * DSA_KT
        split_kv = len(chunks) > 1 and chunks[0][1] <= head_len < seq
        k4, vt, ki = _kvproj(h, norm1_g[layer], kv_w, norm1_g[layer], seq=head_len if split_kv else seq,
                             tm=kv_tm, kt=DSA_KT)
        st_re = jnp.zeros((bsz, n_state), F32)
        st_im = jnp.zeros((bsz, n_state), F32)
        outs = []
        routed = ki
        peer_outs = []
        for c, (s0, sc) in enumerate(chunks):
            tm = math.gcd(math.gcd(s0, sc), ROW_TILE)
            if split_kv and s0 + sc > head_len:
                k4, vt, ki = _kvproj(h, norm1_g[layer], kv_w, routed, seq=seq, tm=kv_tm, kt=DSA_KT)
                split_kv = False
            after = (routed, peer_outs[c - SC_LAG] if c >= SC_LAG else ki)
            u, q, qi, wit, gs, ga = _inproj(h, norm1_g[layer], main_w, after, s0=s0, sc=sc, tm=tm)
            d_ssm = u.shape[-1]
            u_tb = u.transpose(1, 0, 2).reshape(sc * bsz, d_ssm)
            y_tb, st_re, st_im = _s5(u_tb, st_re, st_im, s5p, nb=bsz, tc=S5_STEPS)
            ys = y_tb.reshape(sc, bsz, d_ssm).transpose(1, 0, 2)
            ya = _dsa(q, qi, wit, ki, k4, vt, s0=s0, seq_total=seq, tq=DSA_TQ, kt=DSA_KT)
            hm, x_pk, qp = _merge(h, ys, ya, gs, ga, wsu, wau, wo, norm2_g[layer], wq, s0=s0, tm=tm)
            nt = bsz * sc
            e_t, g_t = _route(qp.reshape(nt, -1), k1, k2, tt=ROUTE_TOKENS)
            routed = e_t
            po = _peer_sc(x_pk.reshape(-1, LANES), e_t.T, g_t.T, u_pk, v_pk)
            peer_outs.append(po)
            hm2 = hm.reshape(nt, d)
            o = _final(hm2, po, norm_f_g, tm=tm) if last else hm2 + po.reshape(nt, d)
            outs.append(o.reshape(bsz, sc, d))
        h = jnp.concatenate(outs, axis=1)
    return h
```

```python
import functools
import math

import numpy as np
import jax
import jax.numpy as jnp
from jax import lax
from jax.experimental import pallas as pl
from jax.experimental.pallas import tpu as pltpu
from jax.experimental.pallas import tpu_sc as plsc

F32 = jnp.float32
BF16 = jnp.bfloat16
I32 = jnp.int32

SSM_GROUP = 16
SSM_STATE = 64
ATTN_HEADS = 8
ATTN_KV_HEADS = 2
HEAD_DIM = 64
IDX_HEADS = 8
IDX_DIM = 32
TOPK_MAX = 256
ROPE_THETA = 10000.0
NEG_BIG = -1e30
PEER_HEADS = 8
PEER_KEYS = 128
PEER_KEY_DIM = 128
PEER_TOPK = 16
NORM_EPS = 1e-6

TIME_SPLIT_32NDS = (1, 3, 4, 4, 4, 4, 4, 3, 2, 2, 1)
SC_LAG = 4
ROW_TILE = 512
S5_STEPS = 64
ROUTE_TOKENS = 256
DSA_TQ = 128
DSA_KT = 256
LANES = 128
INT_MIN = -(2 ** 31)
VMEM_LIMIT = 56 * 1024 * 1024


def _time_chunks(seq):
    unit = seq // 32
    if seq % 32 == 0 and unit % DSA_TQ == 0:
        sizes = [f * unit for f in TIME_SPLIT_32NDS]
    else:
        step = min(ROW_TILE, seq)
        sizes = [step] * (seq // step)
    assert sum(sizes) == seq
    starts = np.cumsum([0] + sizes[:-1]).tolist()
    return list(zip(starts, sizes))


def _cparams(sem):
    return pltpu.CompilerParams(dimension_semantics=sem, vmem_limit_bytes=VMEM_LIMIT)


def _gelu_tanh(x):
    return 0.5 * x * (1.0 + jnp.tanh(math.sqrt(2.0 / math.pi) * (x + 0.044715 * (x * x * x))))


def _sigmoid(x):
    return 1.0 / (1.0 + jnp.exp(-x))


def _rms(x, g):
    return x * lax.rsqrt(jnp.mean(x * x, axis=-1, keepdims=True) + NORM_EPS) * g


def _rot_cols(w, hd):
    d, n = w.shape
    w3 = w.reshape(d, n // hd, hd)
    half = hd // 2
    return jnp.concatenate([-w3[..., half:], w3[..., :half]], axis=-1).reshape(d, n)


def _rope_full(seq, hd, heads):
    pos = jnp.arange(seq, dtype=F32)
    inv = ROPE_THETA ** (-jnp.arange(0, hd, 2, dtype=F32) / hd)
    ang = pos[:, None] * inv[None, :]
    c = jnp.concatenate([jnp.cos(ang), jnp.cos(ang)], axis=-1)
    s = jnp.concatenate([jnp.sin(ang), jnp.sin(ang)], axis=-1)
    return jnp.tile(c, (1, heads)), jnp.tile(s, (1, heads))


def _in_weights(w_in, seq):
    d = w_in.shape[0]
    d_ssm = d // 2
    d_q = ATTN_HEADS * HEAD_DIM
    d_kv = ATTN_KV_HEADS * HEAD_DIM
    d_qi = IDX_HEADS * IDX_DIM
    splits = (d_ssm, d_q, d_kv, d_kv, d_qi, IDX_DIM, IDX_HEADS, d, d)
    offs = np.cumsum(splits)[:-1].tolist()
    wu, wq, wk, wv, wqi, wki, wwi, wgs, wga = jnp.split(w_in, offs, axis=1)
    pad = jnp.zeros((d, 128 - IDX_DIM), F32)
    cq, sq = _rope_full(seq, HEAD_DIM, ATTN_HEADS)
    ck, sk = _rope_full(seq, HEAD_DIM, ATTN_KV_HEADS)
    cqi, sqi = _rope_full(seq, IDX_DIM, IDX_HEADS)
    cki, ski = _rope_full(seq, IDX_DIM, 1)
    tpad = jnp.zeros((seq, 128 - IDX_DIM), F32)
    kv = dict(
        w=jnp.concatenate([wk, wki, pad], axis=1).astype(BF16),
        wvt=wv.T.astype(BF16),
        wr=jnp.concatenate([_rot_cols(wk, HEAD_DIM), _rot_cols(wki, IDX_DIM), pad], axis=1).astype(BF16),
        cs=jnp.concatenate([ck, cki, tpad], axis=1), sn=jnp.concatenate([sk, ski, tpad], axis=1))
    main = dict(
        w=jnp.concatenate([wu, wq, wqi, wgs, wga], axis=1).astype(BF16),
        wr=jnp.concatenate([_rot_cols(wq, HEAD_DIM), _rot_cols(wqi, IDX_DIM)], axis=1).astype(BF16),
        cs=jnp.concatenate([cq, cqi], axis=1), sn=jnp.concatenate([sq, sqi], axis=1),
        wwit=wwi.T.astype(BF16))
    return kv, main


def _kvproj_kernel(x_ref, g_ref, w_ref, wr_ref, wvt_ref, cs_ref, sn_ref, after_ref, k_ref, vt_ref, ki_ref,
                   *, d_kv, kt):
    del after_ref
    xb = _rms(x_ref[0], g_ref[...]).astype(BF16)

    def mm(ref, lo, n):
        return jnp.dot(xb, ref[:, lo:lo + n], preferred_element_type=F32)

    k = mm(w_ref, 0, d_kv) * cs_ref[:, :d_kv] + mm(wr_ref, 0, d_kv) * sn_ref[:, :d_kv]
    for n in range(ATTN_KV_HEADS):
        k_ref[0, n] = k[:, n * HEAD_DIM:(n + 1) * HEAD_DIM].astype(BF16)
    vt = lax.dot_general(wvt_ref[...], xb, (((1,), (1,)), ((), ())), preferred_element_type=F32)
    for n in range(ATTN_KV_HEADS):
        for j in range(vt.shape[1] // kt):
            vt_ref[0, n, j] = vt[n * HEAD_DIM:(n + 1) * HEAD_DIM, j * kt:(j + 1) * kt].astype(BF16)
    kiw = (mm(w_ref, d_kv, 128) * cs_ref[:, d_kv:d_kv + 128]
           + mm(wr_ref, d_kv, 128) * sn_ref[:, d_kv:d_kv + 128])
    ki_ref[0] = kiw[:, :IDX_DIM].astype(BF16)


def _kvproj(x, norm_g, kv, after, seq, tm, kt):
    bsz, _, d = x.shape
    d_kv = ATTN_KV_HEADS * HEAD_DIM
    full = lambda a: pl.BlockSpec(a.shape, lambda s, b: (0,) * a.ndim)
    g = norm_g.reshape(1, d)
    ncs = kv["cs"].shape[1]
    return pl.pallas_call(
        functools.partial(_kvproj_kernel, d_kv=d_kv, kt=kt),
        grid=(seq // tm, bsz),
        in_specs=[pl.BlockSpec((1, tm, d), lambda s, b: (b, s, 0)), full(g), full(kv["w"]), full(kv["wr"]),
                  full(kv["wvt"]),
                  pl.BlockSpec((tm, ncs), lambda s, b: (s, 0)), pl.BlockSpec((tm, ncs), lambda s, b: (s, 0)),
                  pl.BlockSpec(memory_space=pl.ANY)],
        out_specs=[pl.BlockSpec((1, ATTN_KV_HEADS, tm, HEAD_DIM), lambda s, b: (b, 0, s, 0)),
                   pl.BlockSpec((1, ATTN_KV_HEADS, tm // kt, HEAD_DIM, kt), lambda s, b: (b, 0, s, 0, 0)),
                   pl.BlockSpec((1, tm, IDX_DIM), lambda s, b: (b, s, 0))],
        out_shape=[jax.ShapeDtypeStruct((bsz, ATTN_KV_HEADS, seq, HEAD_DIM), BF16),
                   jax.ShapeDtypeStruct((bsz, ATTN_KV_HEADS, seq // kt, HEAD_DIM, kt), BF16),
                   jax.ShapeDtypeStruct((bsz, seq, IDX_DIM), BF16)],
        compiler_params=_cparams(("arbitrary", "arbitrary")),
        name="kvproj",
    )(x, g, kv["w"], kv["wr"], kv["wvt"], kv["cs"], kv["sn"], after)


def _inproj_kernel(x_ref, g_ref, w_ref, wr_ref, cs_ref, sn_ref, wwit_ref, after_tc_ref, after_sc_ref,
                   u_ref, q_ref, qi_ref, wit_ref, gs_ref, ga_ref, *, d_ssm, d_q, d_qi, d_model, q_scale, wi_scale):
    del after_tc_ref, after_sc_ref
    xb = _rms(x_ref[0], g_ref[...]).astype(BF16)

    def mm(ref, lo, n):
        return jnp.dot(xb, ref[:, lo:lo + n], preferred_element_type=F32)

    o = 0
    u_ref[0] = mm(w_ref, o, d_ssm).astype(BF16)
    o += d_ssm
    q = mm(w_ref, o, d_q) * cs_ref[:, :d_q] + mm(wr_ref, 0, d_q) * sn_ref[:, :d_q]
    q_ref[0] = (q * q_scale).astype(BF16)
    o += d_q
    qi = mm(w_ref, o, d_qi) * cs_ref[:, d_q:d_q + d_qi] + mm(wr_ref, d_q, d_qi) * sn_ref[:, d_q:d_q + d_qi]
    qi_ref[0] = qi.astype(BF16)
    o += d_qi
    gs_ref[0] = _sigmoid(mm(w_ref, o, d_model)).astype(BF16)
    o += d_model
    ga_ref[0] = _sigmoid(mm(w_ref, o, d_model)).astype(BF16)
    wit_ref[0] = lax.dot_general(wwit_ref[...], xb, (((1,), (1,)), ((), ())),
                                 preferred_element_type=F32) * wi_scale


def _inproj(x, norm_g, main, after, s0, sc, tm):
    bsz, _, d = x.shape
    d_ssm = d // 2
    d_q = ATTN_HEADS * HEAD_DIM
    d_qi = IDX_HEADS * IDX_DIM
    i0 = s0 // tm
    kern = functools.partial(
        _inproj_kernel, d_ssm=d_ssm, d_q=d_q, d_qi=d_qi, d_model=d,
        q_scale=HEAD_DIM ** -0.5 * math.log2(math.e), wi_scale=(IDX_HEADS ** -0.5) * (IDX_DIM ** -0.5))
    tok = lambda n: pl.BlockSpec((1, tm, n), lambda s, b: (b, s, 0))
    full = lambda a: pl.BlockSpec(a.shape, lambda s, b: (0,) * a.ndim)
    g = norm_g.reshape(1, d)
    ncs = main["cs"].shape[1]
    outs = [(d_ssm, BF16), (d_q, BF16), (d_qi, BF16)]
    return pl.pallas_call(
        kern,
        grid=(sc // tm, bsz),
        in_specs=[pl.BlockSpec((1, tm, d), lambda s, b: (b, s + i0, 0)), full(g), full(main["w"]), full(main["wr"]),
                  pl.BlockSpec((tm, ncs), lambda s, b: (s + i0, 0)),
                  pl.BlockSpec((tm, ncs), lambda s, b: (s + i0, 0)), full(main["wwit"]),
                  pl.BlockSpec(memory_space=pl.ANY), pl.BlockSpec(memory_space=pl.ANY)],
        out_specs=[tok(n) for n, _ in outs] + [pl.BlockSpec((1, IDX_HEADS, tm), lambda s, b: (b, 0, s)),
                                                tok(d), tok(d)],
        out_shape=[jax.ShapeDtypeStruct((bsz, sc, n), dt) for n, dt in outs]
        + [jax.ShapeDtypeStruct((bsz, IDX_HEADS, sc), F32),
           jax.ShapeDtypeStruct((bsz, sc, d), BF16), jax.ShapeDtypeStruct((bsz, sc, d), BF16)],
        compiler_params=_cparams(("arbitrary", "arbitrary")),
        name="inproj",
    )(x, g, main["w"], main["wr"], main["cs"], main["sn"], main["wwit"], *after)


def _s5_kernel(u_ref, sre_in, sim_in, bre_ref, bim_ref, cre_ref, cim_ref, are_ref, aim_ref, dsk_ref, wglu_ref,
               y_ref, st_re, st_im, sre, sim, *, tc, nb, lane_chunk):
    @pl.when(pl.program_id(0) == 0)
    def _():
        st_re[...] = sre_in[...]
        st_im[...] = sim_in[...]

    u = u_ref[...]
    n_half = bre_ref.shape[0]
    hin = bre_ref.shape[1]
    hst = bre_ref.shape[2]
    for h in range(n_half):
        uh = u[:, h * hin:(h + 1) * hin]
        sre[:, h * hst:(h + 1) * hst] = jnp.dot(uh, bre_ref[h], preferred_element_type=F32)
        sim[:, h * hst:(h + 1) * hst] = jnp.dot(uh, bim_ref[h], preferred_element_type=F32)

    n_state = sre.shape[1]
    for c in range(n_state // lane_chunk):
        cols = slice(c * lane_chunk, (c + 1) * lane_chunk)
        ar = are_ref[:, cols]
        ai = aim_ref[:, cols]

        def step(t, carry, cols=cols, ar=ar, ai=ai):
            sr, si = carry
            r0 = pl.multiple_of(t * nb, nb)
            nr = ar * sr - ai * si + sre[pl.ds(r0, nb), cols]
            ni = ar * si + ai * sr + sim[pl.ds(r0, nb), cols]
            sre[pl.ds(r0, nb), cols] = nr
            sim[pl.ds(r0, nb), cols] = ni
            return nr, ni

        sr, si = lax.fori_loop(0, tc, step, (st_re[:, cols], st_im[:, cols]), unroll=4)
        st_re[:, cols] = sr
        st_im[:, cols] = si

    ys = []
    for h in range(n_half):
        srh = sre[:, h * hst:(h + 1) * hst].astype(BF16)
        sih = sim[:, h * hst:(h + 1) * hst].astype(BF16)
        ys.append(jnp.dot(srh, cre_ref[h], preferred_element_type=F32)
                  - jnp.dot(sih, cim_ref[h], preferred_element_type=F32))
    y = jnp.concatenate(ys, axis=-1) + dsk_ref[...] * u.astype(F32)
    y = _gelu_tanh(y)
    gate = jnp.dot(y.astype(BF16), wglu_ref[...], preferred_element_type=F32)
    y_ref[...] = (y * _sigmoid(gate)).astype(BF16)


def _s5_params(a_re, a_im, log_dt, b_re, b_im, c_re, c_im, d_skip, w_glu, nb):
    groups = a_re.shape[0]
    d_ssm = groups * SSM_GROUP
    n_state = groups * SSM_STATE
    lam = lax.complex(a_re, a_im)
    dt = jnp.exp(log_dt)[:, None]
    a_bar = jnp.exp(lam * dt)
    b_bar = ((a_bar - 1.0) / lam)[..., None] * lax.complex(b_re, b_im)
    gh = min(groups, 256 // SSM_GROUP)
    n_half = groups // gh
    eye = jnp.eye(gh, dtype=F32)

    def bmat(bb):
        b4 = bb.reshape(n_half, gh, SSM_STATE, SSM_GROUP)
        return jnp.einsum('hgpc,gk->hgckp', b4, eye).reshape(n_half, gh * SSM_GROUP, gh * SSM_STATE)

    def cmat(cc):
        c4 = cc.reshape(n_half, gh, SSM_GROUP, SSM_STATE)
        return jnp.einsum('hgcp,gk->hgpkc', c4, eye).reshape(n_half, gh * SSM_STATE, gh * SSM_GROUP)

    return (bmat(jnp.real(b_bar)).astype(BF16), bmat(jnp.imag(b_bar)).astype(BF16),
            cmat(c_re).astype(BF16), cmat(c_im).astype(BF16),
            jnp.broadcast_to(jnp.real(a_bar).reshape(1, n_state), (nb, n_state)),
            jnp.broadcast_to(jnp.imag(a_bar).reshape(1, n_state), (nb, n_state)),
            d_skip.reshape(1, d_ssm), w_glu.astype(BF16))


def _s5(u_tb, st_re, st_im, params, nb, tc):
    rows, d_ssm = u_tb.shape
    n_state = st_re.shape[1]
    blk = tc * nb
    full = lambda a: pl.BlockSpec(a.shape, lambda i: (0,) * a.ndim)
    st_spec = pl.BlockSpec((nb, n_state), lambda i: (0, 0))
    kern = functools.partial(_s5_kernel, tc=tc, nb=nb, lane_chunk=512)
    return pl.pallas_call(
        kern,
        grid=(rows // blk,),
        in_specs=[pl.BlockSpec((blk, d_ssm), lambda i: (i, 0)), st_spec, st_spec] + [full(p) for p in params],
        out_specs=[pl.BlockSpec((blk, d_ssm), lambda i: (i, 0)), st_spec, st_spec],
        out_shape=[jax.ShapeDtypeStruct((rows, d_ssm), BF16),
                   jax.ShapeDtypeStruct((nb, n_state), F32), jax.ShapeDtypeStruct((nb, n_state), F32)],
        scratch_shapes=[pltpu.VMEM((blk, n_state), F32), pltpu.VMEM((blk, n_state), F32)],
        compiler_params=_cparams(("arbitrary",)),
        name="s5",
    )(u_tb, st_re, st_im, *params)


PART_ROWS = 32


def _dsa_kernel(qi_ref, wit_ref, q_ref, ki_ref, k_ref, vt_ref, o_ref, key_s, bias_s, lg_s,
                *, qb0, tq, kt, sub, topk, seq_bits):
    qb = pl.program_id(1) + qb0
    nkt = ((qb * tq + tq + sub * kt - 1) // (sub * kt)) * sub
    q_pos = qb * tq + lax.broadcasted_iota(I32, (1, tq), 1)
    k_eff = jnp.minimum(topk, q_pos + 1).astype(F32)

    qi = qi_ref[0]
    wit = wit_ref[0]
    qipair = [jnp.concatenate([qi[:, (2 * hp) * IDX_DIM:(2 * hp + 1) * IDX_DIM],
                               qi[:, (2 * hp + 1) * IDX_DIM:(2 * hp + 2) * IDX_DIM]], axis=0)
              for hp in range(IDX_HEADS // 2)]

    def key_pos(t):
        return t * kt + lax.broadcasted_iota(I32, (kt, tq), 0)

    def score_tile(t2, _):
        for hf in range(sub):
            r0 = pl.multiple_of((t2 * sub + hf) * kt, kt)
            ki_t = ki_ref[0, pl.ds(r0, kt), :]
            for hp in range(IDX_HEADS // 2):
                lg_s[hf, hp] = lax.dot_general(ki_t, qipair[hp], (((1,), (1,)), ((), ())),
                                               preferred_element_type=F32)
        for hf in range(sub):
            t = t2 * sub + hf
            r0 = pl.multiple_of(t * kt, kt)
            sc = jnp.zeros((kt, tq), F32)
            for hp in range(IDX_HEADS // 2):
                rel = lg_s[hf, hp]
                sc = sc + jnp.maximum(rel[:, :tq], 0.0) * wit[2 * hp:2 * hp + 1, :]
                sc = sc + jnp.maximum(rel[:, tq:], 0.0) * wit[2 * hp + 1:2 * hp + 2, :]
            bits = lax.bitcast_convert_type(sc, I32)
            key = jnp.where(bits < 0, bits ^ jnp.int32(0x7FFFFFFF), bits)
            key = jnp.where(key_pos(t) <= q_pos, key, jnp.int32(INT_MIN))
            key_s[pl.ds(r0, kt), :] = key
        return 0

    lax.fori_loop(0, nkt // sub, score_tile, 0)

    def count(pred_fn):
        def body(t, acc):
            r0 = pl.multiple_of(t * kt, kt)
            m = pred_fn(key_s[pl.ds(r0, kt), :], t)
            ones = jnp.where(m, 1.0, 0.0).reshape(kt // PART_ROWS, PART_ROWS, tq)
            return acc + jnp.sum(ones, axis=0)
        acc = lax.fori_loop(0, nkt, body, jnp.zeros((PART_ROWS, tq), F32))
        return jnp.sum(acc, axis=0, keepdims=True)

    def bit_step(i, u):
        bit = jnp.left_shift(jnp.int32(1), 31 - i)
        cand_u = u | bit
        cand_s = cand_u ^ jnp.int32(INT_MIN)
        cnt = count(lambda kk, t: kk >= cand_s)
        return jnp.where(cnt >= k_eff, cand_u, u)

    u_thr = lax.fori_loop(0, 32, bit_step, jnp.zeros((1, tq), I32))
    thr = u_thr ^ jnp.int32(INT_MIN)

    cnt_ge = count(lambda kk, t: kk >= thr)
    cnt_gt = count(lambda kk, t: kk > thr)
    need_eq = k_eff - cnt_gt
    has_tie = jnp.max(cnt_ge - k_eff) > 0.0

    def tie_cut():
        def pos_step(i, c):
            bit = jnp.left_shift(jnp.int32(1), seq_bits - 1 - i)
            cand = c | bit
            cnt = count(lambda kk, t: (kk == thr) & (key_pos(t) < cand))
            return jnp.where(cnt < need_eq, cand, c)
        return lax.fori_loop(0, seq_bits, pos_step, jnp.zeros((1, tq), I32))

    cut = lax.cond(has_tie, tie_cut, lambda: jnp.full((1, tq), 2 ** seq_bits, I32))

    def bias_tile(t, _):
        r0 = pl.multiple_of(t * kt, kt)
        key = key_s[pl.ds(r0, kt), :]
        sel = (key > thr) | ((key == thr) & (key_pos(t) <= cut))
        bias_s[pl.ds(r0, kt), :] = jnp.where(sel, 0.0, NEG_BIG)
        return 0

    lax.fori_loop(0, nkt, bias_tile, 0)

    q = q_ref[0]
    grp = ATTN_HEADS // ATTN_KV_HEADS
    pairs_per_kv = grp // 2
    n_unit = ATTN_KV_HEADS * pairs_per_kv
    wq = 2 * tq
    qpair = [jnp.concatenate([q[:, (2 * u) * HEAD_DIM:(2 * u + 1) * HEAD_DIM],
                              q[:, (2 * u + 1) * HEAD_DIM:(2 * u + 2) * HEAD_DIM]], axis=0)
             for u in range(n_unit)]

    def col_reduce(x, op):
        part = op(x.reshape(kt // PART_ROWS, PART_ROWS, wq), axis=0)
        return op(part, axis=0, keepdims=True)

    def attn_tile(t, carry):
        ms, ls, accs = list(carry[0]), list(carry[1]), list(carry[2])
        for hf in range(sub):
            r0 = pl.multiple_of((t * sub + hf) * kt, kt)
            bias = bias_s[pl.ds(r0, kt), :]
            bias2 = jnp.concatenate([bias, bias], axis=1)
            for u in range(n_unit):
                k_t = k_ref[0, u // pairs_per_kv, pl.ds(r0, kt), :]
                lg_s[hf, u] = lax.dot_general(k_t, qpair[u], (((1,), (1,)), ((), ())),
                                              preferred_element_type=F32) + bias2
        for hf in range(sub):
            for u in range(n_unit):
                lg = lg_s[hf, u]
                m_new = jnp.maximum(ms[u], col_reduce(lg, jnp.max))
                p = jnp.exp2(lg - m_new)
                alpha = jnp.exp2(ms[u] - m_new)
                v_t = vt_ref[0, u // pairs_per_kv, t * sub + hf]
                ls[u] = alpha * ls[u] + col_reduce(p, jnp.sum)
                accs[u] = alpha * accs[u] + jnp.dot(v_t, p.astype(BF16), preferred_element_type=F32)
                ms[u] = m_new
        return tuple(ms), tuple(ls), tuple(accs)

    init = (tuple(jnp.full((1, wq), NEG_BIG, F32) for _ in range(n_unit)),
            tuple(jnp.zeros((1, wq), F32) for _ in range(n_unit)),
            tuple(jnp.zeros((HEAD_DIM, wq), F32) for _ in range(n_unit)))
    _, ls, accs = lax.fori_loop(0, nkt // sub, attn_tile, init)
    for n in range(ATTN_KV_HEADS):
        o_ref[0, 0, n] = jnp.concatenate([accs[n * pairs_per_kv + pg] / ls[n * pairs_per_kv + pg]
                                          for pg in range(pairs_per_kv)], axis=1).astype(BF16)


def _dsa(q, qi, wit, ki, k4, vt, s0, seq_total, tq, kt):
    bsz, sc, _ = q.shape
    seq = ki.shape[1]
    topk = min(TOPK_MAX, seq_total // 4)
    nqb = sc // tq
    grp = ATTN_HEADS // ATTN_KV_HEADS
    seq_bits = int(math.log2(seq))
    assert 2 ** seq_bits == seq
    sub = 2 if seq % (2 * kt) == 0 else 1
    n_unit = ATTN_HEADS // 2
    assert IDX_HEADS // 2 <= n_unit
    kern =functools.partial(_dsa_kernel, qb0=s0 // tq, tq=tq, kt=kt, sub=sub, topk=topk, seq_bits=seq_bits)
    o_t = pl.pallas_call(
        kern,
        grid=(bsz, nqb),
        in_specs=[pl.BlockSpec((1, tq, IDX_HEADS * IDX_DIM), lambda b, j: (b, j, 0)),
                  pl.BlockSpec((1, IDX_HEADS, tq), lambda b, j: (b, 0, j)),
                  pl.BlockSpec((1, tq, ATTN_HEADS * HEAD_DIM), lambda b, j: (b, j, 0)),
                  pl.BlockSpec((1, seq, IDX_DIM), lambda b, j: (b, 0, 0)),
                  pl.BlockSpec((1, ATTN_KV_HEADS, seq, HEAD_DIM), lambda b, j: (b, 0, 0, 0)),
                  pl.BlockSpec((1, ATTN_KV_HEADS, seq // kt, HEAD_DIM, kt), lambda b, j: (b, 0, 0, 0, 0))],
        out_specs=pl.BlockSpec((1, 1, ATTN_KV_HEADS, HEAD_DIM, grp * tq), lambda b, j: (b, j, 0, 0, 0)),
        out_shape=jax.ShapeDtypeStruct((bsz, nqb, ATTN_KV_HEADS, HEAD_DIM, grp * tq), BF16),
        scratch_shapes=[pltpu.VMEM((seq, tq), I32), pltpu.VMEM((seq, tq), F32),
                        pltpu.VMEM((sub, n_unit, kt, 2 * tq), F32)],
        compiler_params=_cparams(("arbitrary", "arbitrary")),
        name="dsa",
    )(qi, wit, q, ki, k4, vt)
    o = o_t.reshape(bsz, nqb, ATTN_KV_HEADS, HEAD_DIM, grp, tq).transpose(0, 1, 5, 2, 4, 3)
    return o.reshape(bsz, sc, ATTN_HEADS * HEAD_DIM)


def _merge_kernel(x_ref, ys_ref, ya_ref, gs_ref, ga_ref, wsu_ref, wau_ref, wout_ref, g2_ref, wq_ref,
                  h_ref, hn_ref, qp_ref):
    ms = jnp.dot(ys_ref[0], wsu_ref[...], preferred_element_type=F32)
    ma = jnp.dot(ya_ref[0], wau_ref[...], preferred_element_type=F32)
    merged = gs_ref[0].astype(F32) * ms + ga_ref[0].astype(F32) * ma
    h = x_ref[0] + jnp.dot(merged.astype(BF16), wout_ref[...], preferred_element_type=F32)
    h_ref[0] = h
    hb = _rms(h, g2_ref[...]).astype(BF16)
    words = _pack_words(hb)
    half = words.shape[1]
    n_seg = half // LANES
    tm = words.shape[0]
    for p in range(n_seg):
        hn_ref[0, pl.ds(p, tm, stride=n_seg), :] = words[:, p * LANES:(p + 1) * LANES]
    qp_ref[0] = jnp.dot(hb, wq_ref[...], preferred_element_type=F32).astype(BF16)


def _merge(x, ys, ya, gs, ga, wsu, wau, wo, norm2_g, wq, s0, tm):
    bsz, sc, _ = ya.shape
    d = x.shape[2]
    i0 = s0 // tm
    loc = lambda a: pl.BlockSpec((1, tm, a.shape[2]), lambda b, i: (b, i, 0))
    full = lambda a: pl.BlockSpec(a.shape, lambda b, i: (0,) * a.ndim)
    g2 = norm2_g.reshape(1, d)
    nq = wq.shape[1]
    n_seg = d // 2 // LANES
    out = lambda n: pl.BlockSpec((1, tm, n), lambda b, i: (b, i, 0))
    return pl.pallas_call(
        _merge_kernel,
        grid=(bsz, sc // tm),
        in_specs=[pl.BlockSpec((1, tm, d), lambda b, i: (b, i + i0, 0)), loc(ys), loc(ya), loc(gs), loc(ga),
                  full(wsu), full(wau), full(wo), full(g2), full(wq)],
        out_specs=[out(d), pl.BlockSpec((1, tm * n_seg, LANES), lambda b, i: (b, i, 0)), out(nq)],
        out_shape=[jax.ShapeDtypeStruct((bsz, sc, d), F32), jax.ShapeDtypeStruct((bsz, sc * n_seg, LANES), I32),
                   jax.ShapeDtypeStruct((bsz, sc, nq), BF16)],
        compiler_params=_cparams(("arbitrary", "arbitrary")),
        name="merge",
    )(x, ys, ya, gs, ga, wsu, wau, wo, g2, wq)


def _cand_layout():
    blocks = []
    blocks.append((0, 16, 16))
    for i in range(1, 8):
        blocks.append((i, 8, PEER_TOPK // (i + 1)))
    blocks.append((None, 8, 8))
    return blocks


def _top_rows(s, order, payload, k):
    big = jnp.float32(3e38)
    vals, pays = [], []
    for _ in range(k):
        m = jnp.max(s, axis=0, keepdims=True)
        o = jnp.min(jnp.where(s == m, order, big), axis=0, keepdims=True)
        hit = order == o
        pays.append(o if payload is order else jnp.min(jnp.where(hit, payload, big), axis=0, keepdims=True))
        vals.append(m)
        s = jnp.where(hit, -jnp.inf, s)
    return jnp.concatenate(vals, axis=0), jnp.concatenate(pays, axis=0)


def _route_kernel(qp_ref, k1_ref, k2_ref, e_ref, g_ref, *, tt):
    qp = qp_ref[...]
    kd = PEER_KEY_DIM
    rows_k = lax.broadcasted_iota(I32, (PEER_KEYS, tt), 0).astype(F32)
    for h in range(PEER_HEADS):
        q1 = qp[:, (2 * h) * kd:(2 * h + 1) * kd]
        q2 = qp[:, (2 * h + 1) * kd:(2 * h + 2) * kd]
        s1 = lax.dot_general(k1_ref[h], q1, (((1,), (1,)), ((), ())), preferred_element_type=F32)
        s2 = lax.dot_general(k2_ref[h], q2, (((1,), (1,)), ((), ())), preferred_element_type=F32)
        v1, i1 = _top_rows(s1, rows_k, rows_k, PEER_TOPK)
        v2, i2 = _top_rows(s2, rows_k, rows_k, PEER_TOPK)
        cs, ce, co = [], [], []
        for i, rows, valid in _cand_layout():
            r = lax.broadcasted_iota(I32, (rows, tt), 0).astype(F32)
            if i is None:
                val = v1[8:16] + v2[0:1]
                eid = i1[8:16] * PEER_KEYS + i2[0:1]
                flat = (r + 8.0) * PEER_TOPK
            else:
                val = v1[i:i + 1] + v2[0:rows]
                eid = i1[i:i + 1] * PEER_KEYS + i2[0:rows]
                flat = r + float(i * PEER_TOPK)
                if valid < rows:
                    val = jnp.where(r < float(valid), val, -jnp.inf)
            cs.append(val)
            ce.append(eid)
            co.append(flat)
        cand = jnp.concatenate(cs, axis=0)
        top_s, top_e = _top_rows(cand, jnp.concatenate(co, axis=0), jnp.concatenate(ce, axis=0), PEER_TOPK)
        p = jnp.exp(top_s - top_s[0:1])
        gates = p / jnp.sum(p, axis=0, keepdims=True)
        e_ref[h * PEER_TOPK:(h + 1) * PEER_TOPK, :] = top_e.astype(I32)
        g_ref[h * PEER_TOPK:(h + 1) * PEER_TOPK, :] = gates


def _route(qp, k1, k2, tt):
    n, nq = qp.shape
    n_sel = PEER_HEADS * PEER_TOPK
    full = lambda a: pl.BlockSpec(a.shape, lambda i: (0,) * a.ndim)
    return pl.pallas_call(
        functools.partial(_route_kernel, tt=tt),
        grid=(n // tt,),
        in_specs=[pl.BlockSpec((tt, nq), lambda i: (i, 0)), full(k1), full(k2)],
        out_specs=[pl.BlockSpec((n_sel, tt), lambda i: (0, i)), pl.BlockSpec((n_sel, tt), lambda i: (0, i))],
        out_shape=[jax.ShapeDtypeStruct((n_sel, n), I32), jax.ShapeDtypeStruct((n_sel, n), F32)],
        compiler_params=_cparams(("arbitrary",)),
        name="route",
    )(qp, k1, k2)


def _final_kernel(h_ref, p_ref, g_ref, o_ref):
    tm, d = h_ref.shape
    n_seg = d // LANES
    p = jnp.concatenate([p_ref[pl.ds(k, tm, stride=n_seg), :] for k in range(n_seg)], axis=1)
    o_ref[...] = _rms(h_ref[...] + p, g_ref[...])


def _final(h, p, g, tm):
    n, d = h.shape
    row = pl.BlockSpec((tm, d), lambda i: (i, 0))
    return pl.pallas_call(
        _final_kernel,
        grid=(n // tm,),
        in_specs=[row, pl.BlockSpec((tm * (d // LANES), LANES), lambda i: (i, 0)),
                  pl.BlockSpec((1, d), lambda i: (0, 0))],
        out_specs=row,
        out_shape=jax.ShapeDtypeStruct((n, d), F32),
        compiler_params=_cparams(("arbitrary",)),
        name="final",
    )(h, p, g.reshape(1, d))


SC_CORES_V7X = 2
SC_SUBCORES_V7X = 16
SC_LANES_V7X = 16
PEER_TOK_BATCH = 32
PEER_ROW_CHUNK = 32
PEER_RING = 4


def _pack_words(x):
    bits = lax.bitcast_convert_type(x.astype(BF16).astype(F32), I32)
    half = bits.shape[1] // 2
    return (bits[:, half:] & jnp.int32(-65536)) | lax.shift_right_logical(bits[:, :half], 16)


def _pack_kernel(t_ref, o_ref):
    o_ref[...] = _pack_words(t_ref[...])


def _pack_bf16_pairs(t, rows=1024):
    e, d = t.shape
    return pl.pallas_call(
        _pack_kernel,
        grid=(e // rows,),
        in_specs=[pl.BlockSpec((rows, d), lambda i: (i, 0))],
        out_specs=pl.BlockSpec((rows, d // 2), lambda i: (i, 0)),
        out_shape=jax.ShapeDtypeStruct((e, d // 2), I32),
        compiler_params=_cparams(("arbitrary",)),
        name="pack_table",
    )(t)


def _unpack_pair(w):
    lo = lax.bitcast_convert_type(jnp.left_shift(w, 16), F32)
    hi = lax.bitcast_convert_type(w & jnp.int32(-65536), F32)
    return lo, hi


def _peer_sc_body(hn_hbm, e_hbm, g_hbm, u_hbm, v_hbm, out_hbm,
                  idx_v, gate_v, x_v, out_v, rows, p_v, act_v, sem, *, tpw, d, n_sel):
    nl = SC_LANES_V7X
    tb = PEER_TOK_BATCH
    rc = PEER_ROW_CHUNK
    n_chunk = n_sel // rc
    jobs_per_tok = 2 * n_chunk
    half = d // 2
    n_lane_blk = half // nl
    xs = half // LANES
    os_ = d // LANES
    wid =lax.axis_index("s") * SC_CORES_V7X + lax.axis_index("c")
    base = wid * tpw
    lane = lax.iota(I32, nl)
    zero = jnp.zeros((nl,), F32)
    c_gelu = 2.0 * math.sqrt(2.0 / math.pi)

    def gather_copy(tab_hbm, job):
        tok = job // jobs_per_tok
        c = (job % jobs_per_tok) % n_chunk
        b = job % PEER_RING
        return pltpu.make_async_copy(tab_hbm.at[idx_v.at[tok, pl.ds(c * rc, rc)]], rows.at[b], sem.at[b])

    def start(job):
        j = job % jobs_per_tok

        @pl.when(j < n_chunk)
        def _():
            gather_copy(u_hbm, job).start()

        @pl.when(j >= n_chunk)
        def _():
            gather_copy(v_hbm, job).start()

    def compute_u(tok, c, b):
        def rg_body(rg, _):
            r0 = rg * 8

            def jbody(j2, accs):
                off0 = j2 * (2 * nl)
                off1 = off0 + nl
                xrow = tok * xs + off0 // LANES
                xl = off0 % LANES
                x0 = plsc.bitcast(x_v[xrow, pl.ds(xl, nl)], BF16)
                x1 = plsc.bitcast(x_v[xrow, pl.ds(xl + nl, nl)], BF16)
                new = []
                for r in range(8):
                    w0 = plsc.bitcast(rows[b, r0 + r, pl.ds(off0, nl)], BF16)
                    w1 = plsc.bitcast(rows[b, r0 + r, pl.ds(off1, nl)], BF16)
                    lo, hi = _unpack_pair(plsc.bitcast(w0 * x0 + w1 * x1, I32))
                    new.append(accs[r] + (lo + hi))
                return tuple(new)

            accs = lax.fori_loop(0, n_lane_blk // 2, jbody, (zero,) * 8)
            for r in range(8):
                p_v[c * rc + r0 + r, :] = accs[r]
            return 0

        lax.fori_loop(0, rc // 8, rg_body, 0)

    def finish_act(tok):
        def eg_body(eg, _):
            e0 = eg * nl
            ridx = e0 + lane
            s = zero
            for l in range(nl):
                s = s + plsc.load_gather(p_v, [ridx, jnp.full((nl,), l, I32)])
            inner = c_gelu * (s + 0.044715 * (s * s * s))
            gl = s / (1.0 + jnp.exp(-inner))
            a = gl * gate_v[tok, pl.ds(e0, nl)]
            bits = lax.bitcast_convert_type(a, I32)
            rnd = bits + jnp.int32(0x7FFF) + (lax.shift_right_logical(bits, 16) & 1)
            hi16 = rnd & jnp.int32(-65536)
            act_v[pl.ds(e0, nl)] = hi16 | lax.shift_right_logical(hi16, 16)
            return 0

        lax.fori_loop(0, n_sel // nl, eg_body, 0)

        def zbody(j, _):
            off = j * nl
            out_v[tok * os_ + off // LANES, pl.ds(off % LANES, nl)] = zero
            return 0

        lax.fori_loop(0, d // nl, zbody, 0, unroll=4)

    def compute_v(tok, c, b):
        def rg_body(rg, _):
            r0 = rg * nl
            splat = [plsc.bitcast(plsc.load_gather(act_v, [jnp.full((nl,), 0, I32) + (c * rc + r0 + r)]), BF16)
                     for r in range(nl)]

            def tree(parts):
                while len(parts) > 1:
                    parts = [parts[i] + parts[i + 1] for i in range(0, len(parts), 2)]
                return parts[0]

            @plsc.parallel_loop(0, n_lane_blk, unroll=2)
            def _(j):
                off = j * nl
                los, his = [], []
                for r in range(0, nl, 2):
                    w0 = plsc.bitcast(rows[b, r0 + r, pl.ds(off, nl)], BF16)
                    w1 = plsc.bitcast(rows[b, r0 + r + 1, pl.ds(off, nl)], BF16)
                    lo, hi = _unpack_pair(plsc.bitcast(w0 * splat[r] + w1 * splat[r + 1], I32))
                    los.append(lo)
                    his.append(hi)
                orow = tok * os_ + off // LANES
                ol = off % LANES
                out_v[orow, pl.ds(ol, nl)] = out_v[orow, pl.ds(ol, nl)] + tree(los)
                out_v[orow + xs, pl.ds(ol, nl)] = out_v[orow + xs, pl.ds(ol, nl)] + tree(his)

            return 0

        lax.fori_loop(0, rc // nl, rg_body, 0)

    def batch_body(bi, _):
        t0 = base + bi * tb
        pltpu.sync_copy(e_hbm.at[pl.ds(t0, tb)], idx_v)
        pltpu.sync_copy(g_hbm.at[pl.ds(t0, tb)], gate_v)
        pltpu.sync_copy(hn_hbm.at[pl.ds(t0 * xs, tb * xs)], x_v)
        for pre in range(PEER_RING - 1):
            start(pre)

        def job_body(job, _):
            @pl.when(job + (PEER_RING - 1) < tb * jobs_per_tok)
            def _():
                start(job + (PEER_RING - 1))

            j = job % jobs_per_tok
            gather_copy(u_hbm, job).wait()
            tok = job // jobs_per_tok
            b = job % PEER_RING

            @pl.when(j < n_chunk)
            def _():
                compute_u(tok, j, b)

            @pl.when(j == n_chunk - 1)
            def _():
                finish_act(tok)

            @pl.when(j >= n_chunk)
            def _():
                compute_v(tok, j - n_chunk, b)

            return 0

        lax.fori_loop(0, tb * jobs_per_tok, job_body, 0)
        pltpu.sync_copy(out_v, out_hbm.at[pl.ds(t0 * os_, tb * os_)])
        return 0

    lax.fori_loop(0, tpw // tb, batch_body, 0)


def _peer_sc(x_pk, experts, gates, u_tab, v_tab):
    d = 2 * u_tab.shape[1]
    n = x_pk.shape[0] * LANES // (d // 2)
    n_sel = experts.shape[1]
    nw = SC_CORES_V7X * SC_SUBCORES_V7X
    tpw = n // nw
    mesh = plsc.VectorSubcoreMesh(core_axis_name="c", subcore_axis_name="s",
                                  num_cores=SC_CORES_V7X, num_subcores=SC_SUBCORES_V7X)
    body = functools.partial(_peer_sc_body, tpw=tpw, d=d, n_sel=n_sel)
    call = pl.kernel(
        body,
        out_type=jax.ShapeDtypeStruct((n * d // LANES, LANES), F32),
        mesh=mesh,
        scratch_types=[pltpu.VMEM((PEER_TOK_BATCH, n_sel), I32),
                       pltpu.VMEM((PEER_TOK_BATCH, n_sel), F32),
                       pltpu.VMEM((PEER_TOK_BATCH * d // 2 // LANES, LANES), I32),
                       pltpu.VMEM((PEER_TOK_BATCH * d // LANES, LANES), F32),
                       pltpu.VMEM((PEER_RING, PEER_ROW_CHUNK, d // 2), I32),
                       pltpu.VMEM((n_sel, SC_LANES_V7X), F32),
                       pltpu.VMEM((n_sel,), I32),
                       pltpu.SemaphoreType.DMA((PEER_RING,))],
        compiler_params=pltpu.CompilerParams(needs_layout_passes=False, use_tc_tiling_on_sc=False),
        name="peer_sc",
    )
    return call(x_pk, experts, gates, u_tab, v_tab)


def kernel(x, norm1_g, w_in, a_re, a_im, log_dt, b_re, b_im, c_re, c_im, d_skip, w_glu, w_ssm_up, w_attn_up,
           w_out, norm2_g, peer_wq, peer_k1, peer_k2, peer_u, peer_v, norm_f_g):
    bsz, seq, d = x.shape
    depth = norm1_g.shape[0]
    chunks = _time_chunks(seq)
    h = x
    for layer in range(depth):
        last = layer + 1 == depth
        kv_w, main_w = _in_weights(w_in[layer], seq)
        s5p = _s5_params(a_re[layer], a_im[layer], log_dt[layer], b_re[layer], b_im[layer], c_re[layer],
                         c_im[layer], d_skip[layer], w_glu[layer], nb=bsz)
        n_state = s5p[4].shape[1]
        wsu = w_ssm_up[layer].astype(BF16)
        wau = w_attn_up[layer].astype(BF16)
        wo = w_out[layer].astype(BF16)
        wq = peer_wq[layer].astype(BF16)
        k1 = peer_k1[layer].astype(BF16)
        k2 = peer_k2[layer].astype(BF16)
        u_pk = _pack_bf16_pairs(peer_u[layer])
        v_pk = _pack_bf16_pairs(peer_v[layer])
        kv_tm = min(ROW_TILE, seq)
        head_len = 2 * DSA_KT
        split_kv = len(chunks) > 1 and chunks[0][1] <= head_len < seq
        k4, vt, ki = _kvproj(h, norm1_g[layer], kv_w, norm1_g[layer], seq=head_len if split_kv else seq,
                             tm=kv_tm, kt=DSA_KT)
        st_re = jnp.zeros((bsz, n_state), F32)
        st_im = jnp.zeros((bsz, n_state), F32)
        outs = []
        routed = ki
        peer_outs = []
        for c, (s0, sc) in enumerate(chunks):
            tm = math.gcd(math.gcd(s0, sc), ROW_TILE)
            if split_kv and s0 + sc > head_len:
                k4, vt, ki = _kvproj(h, norm1_g[layer], kv_w, routed, seq=seq, tm=kv_tm, kt=DSA_KT)
                split_kv = False
            after = (routed, peer_outs[c - SC_LAG] if c >= SC_LAG else ki)
            u, q, qi, wit, gs, ga = _inproj(h, norm1_g[layer], main_w, after, s0=s0, sc=sc, tm=tm)
            d_ssm = u.shape[-1]
            u_tb = u.transpose(1, 0, 2).reshape(sc * bsz, d_ssm)
            y_tb, st_re, st_im = _s5(u_tb, st_re, st_im, s5p, nb=bsz, tc=S5_STEPS)
            ys = y_tb.reshape(sc, bsz, d_ssm).transpose(1, 0, 2)
            ya = _dsa(q, qi, wit, ki, k4, vt, s0=s0, seq_total=seq, tq=DSA_TQ, kt=DSA_KT)
            hm, x_pk, qp = _merge(h, ys, ya, gs, ga, wsu, wau, wo, norm2_g[layer], wq, s0=s0, tm=tm)
            nt = bsz * sc
            e_t, g_t = _route(qp.reshape(nt, -1), k1, k2, tt=ROUTE_TOKENS)
            routed = e_t
            po = _peer_sc(x_pk.reshape(-1, LANES), e_t.T, g_t.T, u_pk, v_pk)
            peer_outs.append(po)
            hm2 = hm.reshape(nt, d)
            o = _final(hm2, po, norm_f_g, tm=tm) if last else hm2 + po.reshape(nt, d)
            outs.append(o.reshape(bsz, sc, d))
        h = jnp.concatenate(outs, axis=1)
    return h
```

```python
import functools
import math

import numpy as np
import jax
import jax.numpy as jnp
from jax import lax
from jax.experimental import pallas as pl
from jax.experimental.pallas import tpu as pltpu
from jax.experimental.pallas import tpu_sc as plsc

F32 = jnp.float32
BF16 = jnp.bfloat16
I32 = jnp.int32

SSM_GROUP = 16
SSM_STATE = 64
ATTN_HEADS = 8
ATTN_KV_HEADS = 2
HEAD_DIM = 64
IDX_HEADS = 8
IDX_DIM = 32
TOPK_MAX = 256
ROPE_THETA = 10000.0
NEG_BIG = -1e30
PEER_HEADS = 8
PEER_KEYS = 128
PEER_KEY_DIM = 128
PEER_TOPK = 16
NORM_EPS = 1e-6

TIME_SPLIT_32NDS = (1, 3, 4, 4, 4, 4, 4, 3, 2, 2, 1)
SC_LAG = 2
ROW_TILE = 512
S5_STEPS = 64
ROUTE_TOKENS = 256
DSA_TQ = 128
DSA_KT = 256
LANES = 128
INT_MIN = -(2 ** 31)
VMEM_LIMIT = 56 * 1024 * 1024


def _time_chunks(seq):
    unit = seq // 32
    if seq % 32 == 0 and unit % DSA_TQ == 0:
        sizes = [f * unit for f in TIME_SPLIT_32NDS]
    else:
        step = min(ROW_TILE, seq)
        sizes = [step] * (seq // step)
    assert sum(sizes) == seq
    starts = np.cumsum([0] + sizes[:-1]).tolist()
    return list(zip(starts, sizes))


def _cparams(sem):
    return pltpu.CompilerParams(dimension_semantics=sem, vmem_limit_bytes=VMEM_LIMIT)


def _gelu_tanh(x):
    return 0.5 * x * (1.0 + jnp.tanh(math.sqrt(2.0 / math.pi) * (x + 0.044715 * (x * x * x))))


def _sigmoid(x):
    return 1.0 / (1.0 + jnp.exp(-x))


def _rms(x, g):
    return x * lax.rsqrt(jnp.mean(x * x, axis=-1, keepdims=True) + NORM_EPS) * g


def _rot_cols(w, hd):
    d, n = w.shape
    w3 = w.reshape(d, n // hd, hd)
    half = hd // 2
    return jnp.concatenate([-w3[..., half:], w3[..., :half]], axis=-1).reshape(d, n)


def _rope_full(seq, hd, heads):
    pos = jnp.arange(seq, dtype=F32)
    inv = ROPE_THETA ** (-jnp.arange(0, hd, 2, dtype=F32) / hd)
    ang = pos[:, None] * inv[None, :]
    c = jnp.concatenate([jnp.cos(ang), jnp.cos(ang)], axis=-1)
    s = jnp.concatenate([jnp.sin(ang), jnp.sin(ang)], axis=-1)
    return jnp.tile(c, (1, heads)), jnp.tile(s, (1, heads))


def _in_weights(w_in, seq):
    d = w_in.shape[0]
    d_ssm = d // 2
    d_q = ATTN_HEADS * HEAD_DIM
    d_kv = ATTN_KV_HEADS * HEAD_DIM
    d_qi = IDX_HEADS * IDX_DIM
    splits = (d_ssm, d_q, d_kv, d_kv, d_qi, IDX_DIM, IDX_HEADS, d, d)
    offs = np.cumsum(splits)[:-1].tolist()
    wu, wq, wk, wv, wqi, wki, wwi, wgs, wga = jnp.split(w_in, offs, axis=1)
    pad = jnp.zeros((d, 128 - IDX_DIM), F32)
    cq, sq = _rope_full(seq, HEAD_DIM, ATTN_HEADS)
    ck, sk = _rope_full(seq, HEAD_DIM, ATTN_KV_HEADS)
    cqi, sqi = _rope_full(seq, IDX_DIM, IDX_HEADS)
    cki, ski = _rope_full(seq, IDX_DIM, 1)
    tpad = jnp.zeros((seq, 128 - IDX_DIM), F32)
    kv = dict(
        w=jnp.concatenate([wk, wki, pad], axis=1).astype(BF16),
        wvt=wv.T.astype(BF16),
        wr=jnp.concatenate([_rot_cols(wk, HEAD_DIM), _rot_cols(wki, IDX_DIM), pad], axis=1).astype(BF16),
        cs=jnp.concatenate([ck, cki, tpad], axis=1), sn=jnp.concatenate([sk, ski, tpad], axis=1))
    main = dict(
        w=jnp.concatenate([wu, wq, wqi, wgs, wga], axis=1).astype(BF16),
        wr=jnp.concatenate([_rot_cols(wq, HEAD_DIM), _rot_cols(wqi, IDX_DIM)], axis=1).astype(BF16),
        cs=jnp.concatenate([cq, cqi], axis=1), sn=jnp.concatenate([sq, sqi], axis=1),
        wwit=wwi.T.astype(BF16))
    return kv, main


def _kvproj_kernel(x_ref, g_ref, w_ref, wr_ref, wvt_ref, cs_ref, sn_ref, after_ref, k_ref, vt_ref, ki_ref,
                   *, d_kv, kt):
    del after_ref
    xb = _rms(x_ref[0], g_ref[...]).astype(BF16)

    def mm(ref, lo, n):
        return jnp.dot(xb, ref[:, lo:lo + n], preferred_element_type=F32)

    k = mm(w_ref, 0, d_kv) * cs_ref[:, :d_kv] + mm(wr_ref, 0, d_kv) * sn_ref[:, :d_kv]
    for n in range(ATTN_KV_HEADS):
        k_ref[0, n] = k[:, n * HEAD_DIM:(n + 1) * HEAD_DIM].astype(BF16)
    vt = lax.dot_general(wvt_ref[...], xb, (((1,), (1,)), ((), ())), preferred_element_type=F32)
    for n in range(ATTN_KV_HEADS):
        for j in range(vt.shape[1] // kt):
            vt_ref[0, n, j] = vt[n * HEAD_DIM:(n + 1) * HEAD_DIM, j * kt:(j + 1) * kt].astype(BF16)
    kiw = (mm(w_ref, d_kv, 128) * cs_ref[:, d_kv:d_kv + 128]
           + mm(wr_ref, d_kv, 128) * sn_ref[:, d_kv:d_kv + 128])
    ki_ref[0] = kiw[:, :IDX_DIM].astype(BF16)


def _kvproj(x, norm_g, kv, after, seq, tm, kt):
    bsz, _, d = x.shape
    d_kv = ATTN_KV_HEADS * HEAD_DIM
    full = lambda a: pl.BlockSpec(a.shape, lambda s, b: (0,) * a.ndim)
    g = norm_g.reshape(1, d)
    ncs = kv["cs"].shape[1]
    return pl.pallas_call(
        functools.partial(_kvproj_kernel, d_kv=d_kv, kt=kt),
        grid=(seq // tm, bsz),
        in_specs=[pl.BlockSpec((1, tm, d), lambda s, b: (b, s, 0)), full(g), full(kv["w"]), full(kv["wr"]),
                  full(kv["wvt"]),
                  pl.BlockSpec((tm, ncs), lambda s, b: (s, 0)), pl.BlockSpec((tm, ncs), lambda s, b: (s, 0)),
                  pl.BlockSpec(memory_space=pl.ANY)],
        out_specs=[pl.BlockSpec((1, ATTN_KV_HEADS, tm, HEAD_DIM), lambda s, b: (b, 0, s, 0)),
                   pl.BlockSpec((1, ATTN_KV_HEADS, tm // kt, HEAD_DIM, kt), lambda s, b: (b, 0, s, 0, 0)),
                   pl.BlockSpec((1, tm, IDX_DIM), lambda s, b: (b, s, 0))],
        out_shape=[jax.ShapeDtypeStruct((bsz, ATTN_KV_HEADS, seq, HEAD_DIM), BF16),
                   jax.ShapeDtypeStruct((bsz, ATTN_KV_HEADS, seq // kt, HEAD_DIM, kt), BF16),
                   jax.ShapeDtypeStruct((bsz, seq, IDX_DIM), BF16)],
        compiler_params=_cparams(("arbitrary", "arbitrary")),
        name="kvproj",
    )(x, g, kv["w"], kv["wr"], kv["wvt"], kv["cs"], kv["sn"], after)


def _inproj_kernel(x_ref, g_ref, w_ref, wr_ref, cs_ref, sn_ref, wwit_ref, after_tc_ref, after_sc_ref,
                   u_ref, q_ref, qi_ref, wit_ref, gs_ref, ga_ref, *, d_ssm, d_q, d_qi, d_model, q_scale, wi_scale):
    del after_tc_ref, after_sc_ref
    xb = _rms(x_ref[0], g_ref[...]).astype(BF16)

    def mm(ref, lo, n):
        return jnp.dot(xb, ref[:, lo:lo + n], preferred_element_type=F32)

    o = 0
    u_ref[0] = mm(w_ref, o, d_ssm).astype(BF16)
    o += d_ssm
    q = mm(w_ref, o, d_q) * cs_ref[:, :d_q] + mm(wr_ref, 0, d_q) * sn_ref[:, :d_q]
    q_ref[0] = (q * q_scale).astype(BF16)
    o += d_q
    qi = mm(w_ref, o, d_qi) * cs_ref[:, d_q:d_q + d_qi] + mm(wr_ref, d_q, d_qi) * sn_ref[:, d_q:d_q + d_qi]
    qi_ref[0] = qi.astype(BF16)
    o += d_qi
    gs_ref[0] = _sigmoid(mm(w_ref, o, d_model)).astype(BF16)
    o += d_model
    ga_ref[0] = _sigmoid(mm(w_ref, o, d_model)).astype(BF16)
    wit_ref[0] = lax.dot_general(wwit_ref[...], xb, (((1,), (1,)), ((), ())),
                                 preferred_element_type=F32) * wi_scale


def _inproj(x, norm_g, main, after, s0, sc, tm):
    bsz, _, d = x.shape
    d_ssm = d // 2
    d_q = ATTN_HEADS * HEAD_DIM
    d_qi = IDX_HEADS * IDX_DIM
    i0 = s0 // tm
    kern = functools.partial(
        _inproj_kernel, d_ssm=d_ssm, d_q=d_q, d_qi=d_qi, d_model=d,
        q_scale=HEAD_DIM ** -0.5 * math.log2(math.e), wi_scale=(IDX_HEADS ** -0.5) * (IDX_DIM ** -0.5))
    tok = lambda n: pl.BlockSpec((1, tm, n), lambda s, b: (b, s, 0))
    full = lambda a: pl.BlockSpec(a.shape, lambda s, b: (0,) * a.ndim)
    g = norm_g.reshape(1, d)
    ncs = main["cs"].shape[1]
    outs = [(d_ssm, BF16), (d_q, BF16), (d_qi, BF16)]
    return pl.pallas_call(
        kern,
        grid=(sc // tm, bsz),
        in_specs=[pl.BlockSpec((1, tm, d), lambda s, b: (b, s + i0, 0)), full(g), full(main["w"]), full(main["wr"]),
                  pl.BlockSpec((tm, ncs), lambda s, b: (s + i0, 0)),
                  pl.BlockSpec((tm, ncs), lambda s, b: (s + i0, 0)), full(main["wwit"]),
                  pl.BlockSpec(memory_space=pl.ANY), pl.BlockSpec(memory_space=pl.ANY)],
        out_specs=[tok(n) for n, _ in outs] + [pl.BlockSpec((1, IDX_HEADS, tm), lambda s, b: (b, 0, s)),
                                                tok(d), tok(d)],
        out_shape=[jax.ShapeDtypeStruct((bsz, sc, n), dt) for n, dt in outs]
        + [jax.ShapeDtypeStruct((bsz, IDX_HEADS, sc), F32),
           jax.ShapeDtypeStruct((bsz, sc, d), BF16), jax.ShapeDtypeStruct((bsz, sc, d), BF16)],
        compiler_params=_cparams(("arbitrary", "arbitrary")),
        name="inproj",
    )(x, g, main["w"], main["wr"], main["cs"], main["sn"], main["wwit"], *after)


def _s5_kernel(u_ref, sre_in, sim_in, bre_ref, bim_ref, cre_ref, cim_ref, are_ref, aim_ref, dsk_ref, wglu_ref,
               y_ref, st_re, st_im, sre, sim, *, tc, nb, lane_chunk):
    @pl.when(pl.program_id(0) == 0)
    def _():
        st_re[...] = sre_in[...]
        st_im[...] = sim_in[...]

    u = u_ref[...]
    n_half = bre_ref.shape[0]
    hin = bre_ref.shape[1]
    hst = bre_ref.shape[2]
    for h in range(n_half):
        uh = u[:, h * hin:(h + 1) * hin]
        sre[:, h * hst:(h + 1) * hst] = jnp.dot(uh, bre_ref[h], preferred_element_type=F32)
        sim[:, h * hst:(h + 1) * hst] = jnp.dot(uh, bim_ref[h], preferred_element_type=F32)

    n_state = sre.shape[1]
    for c in range(n_state // lane_chunk):
        cols = slice(c * lane_chunk, (c + 1) * lane_chunk)
        ar = are_ref[:, cols]
        ai = aim_ref[:, cols]

        def step(t, carry, cols=cols, ar=ar, ai=ai):
            sr, si = carry
            r0 = pl.multiple_of(t * nb, nb)
            nr = ar * sr - ai * si + sre[pl.ds(r0, nb), cols]
            ni = ar * si + ai * sr + sim[pl.ds(r0, nb), cols]
            sre[pl.ds(r0, nb), cols] = nr
            sim[pl.ds(r0, nb), cols] = ni
            return nr, ni

        sr, si = lax.fori_loop(0, tc, step, (st_re[:, cols], st_im[:, cols]), unroll=4)
        st_re[:, cols] = sr
        st_im[:, cols] = si

    ys = []
    for h in range(n_half):
        srh = sre[:, h * hst:(h + 1) * hst].astype(BF16)
        sih = sim[:, h * hst:(h + 1) * hst].astype(BF16)
        ys.append(jnp.dot(srh, cre_ref[h], preferred_element_type=F32)
                  - jnp.dot(sih, cim_ref[h], preferred_element_type=F32))
    y = jnp.concatenate(ys, axis=-1) + dsk_ref[...] * u.astype(F32)
    y = _gelu_tanh(y)
    gate = jnp.dot(y.astype(BF16), wglu_ref[...], preferred_element_type=F32)
    y_ref[...] = (y * _sigmoid(gate)).astype(BF16)


def _s5_params(a_re, a_im, log_dt, b_re, b_im, c_re, c_im, d_skip, w_glu, nb):
    groups = a_re.shape[0]
    d_ssm = groups * SSM_GROUP
    n_state = groups * SSM_STATE
    lam = lax.complex(a_re, a_im)
    dt = jnp.exp(log_dt)[:, None]
    a_bar = jnp.exp(lam * dt)
    b_bar = ((a_bar - 1.0) / lam)[..., None] * lax.complex(b_re, b_im)
    gh = min(groups, 256 // SSM_GROUP)
    n_half = groups // gh
    eye = jnp.eye(gh, dtype=F32)

    def bmat(bb):
        b4 = bb.reshape(n_half, gh, SSM_STATE, SSM_GROUP)
        return jnp.einsum('hgpc,gk->hgckp', b4, eye).reshape(n_half, gh * SSM_GROUP, gh * SSM_STATE)

    def cmat(cc):
        c4 = cc.reshape(n_half, gh, SSM_GROUP, SSM_STATE)
        return jnp.einsum('hgcp,gk->hgpkc', c4, eye).reshape(n_half, gh * SSM_STATE, gh * SSM_GROUP)

    return (bmat(jnp.real(b_bar)).astype(BF16), bmat(jnp.imag(b_bar)).astype(BF16),
            cmat(c_re).astype(BF16), cmat(c_im).astype(BF16),
            jnp.broadcast_to(jnp.real(a_bar).reshape(1, n_state), (nb, n_state)),
            jnp.broadcast_to(jnp.imag(a_bar).reshape(1, n_state), (nb, n_state)),
            d_skip.reshape(1, d_ssm), w_glu.astype(BF16))


def _s5(u_tb, st_re, st_im, params, nb, tc):
    rows, d_ssm = u_tb.shape
    n_state = st_re.shape[1]
    blk = tc * nb
    full = lambda a: pl.BlockSpec(a.shape, lambda i: (0,) * a.ndim)
    st_spec = pl.BlockSpec((nb, n_state), lambda i: (0, 0))
    kern = functools.partial(_s5_kernel, tc=tc, nb=nb, lane_chunk=512)
    return pl.pallas_call(
        kern,
        grid=(rows // blk,),
        in_specs=[pl.BlockSpec((blk, d_ssm), lambda i: (i, 0)), st_spec, st_spec] + [full(p) for p in params],
        out_specs=[pl.BlockSpec((blk, d_ssm), lambda i: (i, 0)), st_spec, st_spec],
        out_shape=[jax.ShapeDtypeStruct((rows, d_ssm), BF16),
                   jax.ShapeDtypeStruct((nb, n_state), F32), jax.ShapeDtypeStruct((nb, n_state), F32)],
        scratch_shapes=[pltpu.VMEM((blk, n_state), F32), pltpu.VMEM((blk, n_state), F32)],
        compiler_params=_cparams(("arbitrary",)),
        name="s5",
    )(u_tb, st_re, st_im, *params)


PART_ROWS = 32


def _dsa_kernel(qi_ref, wit_ref, q_ref, ki_ref, k_ref, vt_ref, o_ref, key_s, bias_s, lg_s,
                *, qb0, tq, kt, sub, topk, seq_bits):
    qb = pl.program_id(1) + qb0
    nkt = ((qb * tq + tq + sub * kt - 1) // (sub * kt)) * sub
    q_pos = qb * tq + lax.broadcasted_iota(I32, (1, tq), 1)
    k_eff = jnp.minimum(topk, q_pos + 1).astype(F32)

    qi = qi_ref[0]
    wit = wit_ref[0]
    qipair = [jnp.concatenate([qi[:, (2 * hp) * IDX_DIM:(2 * hp + 1) * IDX_DIM],
                               qi[:, (2 * hp + 1) * IDX_DIM:(2 * hp + 2) * IDX_DIM]], axis=0)
              for hp in range(IDX_HEADS // 2)]

    def key_pos(t):
        return t * kt + lax.broadcasted_iota(I32, (kt, tq), 0)

    def score_tile(t2, _):
        for hf in range(sub):
            r0 = pl.multiple_of((t2 * sub + hf) * kt, kt)
            ki_t = ki_ref[0, pl.ds(r0, kt), :]
            for hp in range(IDX_HEADS // 2):
                lg_s[hf, hp] = lax.dot_general(ki_t, qipair[hp], (((1,), (1,)), ((), ())),
                                               preferred_element_type=F32)
        for hf in range(sub):
            t = t2 * sub + hf
            r0 = pl.multiple_of(t * kt, kt)
            sc = jnp.zeros((kt, tq), F32)
            for hp in range(IDX_HEADS // 2):
                rel = lg_s[hf, hp]
                sc = sc + jnp.maximum(rel[:, :tq], 0.0) * wit[2 * hp:2 * hp + 1, :]
                sc = sc + jnp.maximum(rel[:, tq:], 0.0) * wit[2 * hp + 1:2 * hp + 2, :]
            bits = lax.bitcast_convert_type(sc, I32)
            key = jnp.where(bits < 0, bits ^ jnp.int32(0x7FFFFFFF), bits)
            key = jnp.where(key_pos(t) <= q_pos, key, jnp.int32(INT_MIN))
            key_s[pl.ds(r0, kt), :] = key
        return 0

    lax.fori_loop(0, nkt // sub, score_tile, 0)

    def count(pred_fn):
        def body(t, acc):
            r0 = pl.multiple_of(t * kt, kt)
            m = pred_fn(key_s[pl.ds(r0, kt), :], t)
            ones = jnp.where(m, 1.0, 0.0).reshape(kt // PART_ROWS, PART_ROWS, tq)
            return acc + jnp.sum(ones, axis=0)
        acc = lax.fori_loop(0, nkt, body, jnp.zeros((PART_ROWS, tq), F32))
        return jnp.sum(acc, axis=0, keepdims=True)

    def bit_step(i, u):
        bit = jnp.left_shift(jnp.int32(1), 31 - i)
        cand_u = u | bit
        cand_s = cand_u ^ jnp.int32(INT_MIN)
        cnt = count(lambda kk, t: kk >= cand_s)
        return jnp.where(cnt >= k_eff, cand_u, u)

    u_thr = lax.fori_loop(0, 32, bit_step, jnp.zeros((1, tq), I32))
    thr = u_thr ^ jnp.int32(INT_MIN)

    cnt_ge = count(lambda kk, t: kk >= thr)
    cnt_gt = count(lambda kk, t: kk > thr)
    need_eq = k_eff - cnt_gt
    has_tie = jnp.max(cnt_ge - k_eff) > 0.0

    def tie_cut():
        def pos_step(i, c):
            bit = jnp.left_shift(jnp.int32(1), seq_bits - 1 - i)
            cand = c | bit
            cnt = count(lambda kk, t: (kk == thr) & (key_pos(t) < cand))
            return jnp.where(cnt < need_eq, cand, c)
        return lax.fori_loop(0, seq_bits, pos_step, jnp.zeros((1, tq), I32))

    cut = lax.cond(has_tie, tie_cut, lambda: jnp.full((1, tq), 2 ** seq_bits, I32))

    def bias_tile(t, _):
        r0 = pl.multiple_of(t * kt, kt)
        key = key_s[pl.ds(r0, kt), :]
        sel = (key > thr) | ((key == thr) & (key_pos(t) <= cut))
        bias_s[pl.ds(r0, kt), :] = jnp.where(sel, 0.0, NEG_BIG)
        return 0

    lax.fori_loop(0, nkt, bias_tile, 0)

    q = q_ref[0]
    grp = ATTN_HEADS // ATTN_KV_HEADS
    pairs_per_kv = grp // 2
    n_unit = ATTN_KV_HEADS * pairs_per_kv
    wq = 2 * tq
    qpair = [jnp.concatenate([q[:, (2 * u) * HEAD_DIM:(2 * u + 1) * HEAD_DIM],
                              q[:, (2 * u + 1) * HEAD_DIM:(2 * u + 2) * HEAD_DIM]], axis=0)
             for u in range(n_unit)]

    def col_reduce(x, op):
        part = op(x.reshape(kt // PART_ROWS, PART_ROWS, wq), axis=0)
        return op(part, axis=0, keepdims=True)

    def attn_tile(t, carry):
        ms, ls, accs = list(carry[0]), list(carry[1]), list(carry[2])
        for hf in range(sub):
            r0 = pl.multiple_of((t * sub + hf) * kt, kt)
            bias = bias_s[pl.ds(r0, kt), :]
            bias2 = jnp.concatenate([bias, bias], axis=1)
            for u in range(n_unit):
                k_t = k_ref[0, u // pairs_per_kv, pl.ds(r0, kt), :]
                lg_s[hf, u] = lax.dot_general(k_t, qpair[u], (((1,), (1,)), ((), ())),
                                              preferred_element_type=F32) + bias2
        for hf in range(sub):
            for u in range(n_unit):
                lg = lg_s[hf, u]
                m_new = jnp.maximum(ms[u], col_reduce(lg, jnp.max))
                p = jnp.exp2(lg - m_new)
                alpha = jnp.exp2(ms[u] - m_new)
                v_t = vt_ref[0, u // pairs_per_kv, t * sub + hf]
                ls[u] = alpha * ls[u] + col_reduce(p, jnp.sum)
                accs[u] = alpha * accs[u] + jnp.dot(v_t, p.astype(BF16), preferred_element_type=F32)
                ms[u] = m_new
        return tuple(ms), tuple(ls), tuple(accs)

    init = (tuple(jnp.full((1, wq), NEG_BIG, F32) for _ in range(n_unit)),
            tuple(jnp.zeros((1, wq), F32) for _ in range(n_unit)),
            tuple(jnp.zeros((HEAD_DIM, wq), F32) for _ in range(n_unit)))
    _, ls, accs = lax.fori_loop(0, nkt // sub, attn_tile, init)
    for n in range(ATTN_KV_HEADS):
        o_ref[0, 0, n] = jnp.concatenate([accs[n * pairs_per_kv + pg] / ls[n * pairs_per_kv + pg]
                                          for pg in range(pairs_per_kv)], axis=1).astype(BF16)


def _dsa(q, qi, wit, ki, k4, vt, s0, seq_total, tq, kt):
    bsz, sc, _ = q.shape
    seq = ki.shape[1]
    topk = min(TOPK_MAX, seq_total // 4)
    nqb = sc // tq
    grp = ATTN_HEADS // ATTN_KV_HEADS
    seq_bits = int(math.log2(seq))
    assert 2 ** seq_bits == seq
    sub = 2 if seq % (2 * kt) == 0 else 1
    n_unit = ATTN_HEADS // 2
    assert IDX_HEADS // 2 <= n_unit
    kern =functools.partial(_dsa_kernel, qb0=s0 // tq, tq=tq, kt=kt, sub=sub, topk=topk, seq_bits=seq_bits)
    o_t = pl.pallas_call(
        kern,
        grid=(bsz, nqb),
        in_specs=[pl.BlockSpec((1, tq, IDX_HEADS * IDX_DIM), lambda b, j: (b, j, 0)),
                  pl.BlockSpec((1, IDX_HEADS, tq), lambda b, j: (b, 0, j)),
                  pl.BlockSpec((1, tq, ATTN_HEADS * HEAD_DIM), lambda b, j: (b, j, 0)),
                  pl.BlockSpec((1, seq, IDX_DIM), lambda b, j: (b, 0, 0)),
                  pl.BlockSpec((1, ATTN_KV_HEADS, seq, HEAD_DIM), lambda b, j: (b, 0, 0, 0)),
                  pl.BlockSpec((1, ATTN_KV_HEADS, seq // kt, HEAD_DIM, kt), lambda b, j: (b, 0, 0, 0, 0))],
        out_specs=pl.BlockSpec((1, 1, ATTN_KV_HEADS, HEAD_DIM, grp * tq), lambda b, j: (b, j, 0, 0, 0)),
        out_shape=jax.ShapeDtypeStruct((bsz, nqb, ATTN_KV_HEADS, HEAD_DIM, grp * tq), BF16),
        scratch_shapes=[pltpu.VMEM((seq, tq), I32), pltpu.VMEM((seq, tq), F32),
                        pltpu.VMEM((sub, n_unit, kt, 2 * tq), F32)],
        compiler_params=_cparams(("arbitrary", "arbitrary")),
        name="dsa",
    )(qi, wit, q, ki, k4, vt)
    o = o_t.reshape(bsz, nqb, ATTN_KV_HEADS, HEAD_DIM, grp, tq).transpose(0, 1, 5, 2, 4, 3)
    return o.reshape(bsz, sc, ATTN_HEADS * HEAD_DIM)


def _merge_kernel(x_ref, ys_ref, ya_ref, gs_ref, ga_ref, wsu_ref, wau_ref, wout_ref, g2_ref, wq_ref,
                  h_ref, hn_ref, qp_ref):
    ms = jnp.dot(ys_ref[0], wsu_ref[...], preferred_element_type=F32)
    ma = jnp.dot(ya_ref[0], wau_ref[...], preferred_element_type=F32)
    merged = gs_ref[0].astype(F32) * ms + ga_ref[0].astype(F32) * ma
    h = x_ref[0] + jnp.dot(merged.astype(BF16), wout_ref[...], preferred_element_type=F32)
    h_ref[0] = h
    hb = _rms(h, g2_ref[...]).astype(BF16)
    words = _pack_words(hb)
    half = words.shape[1]
    n_seg = half // LANES
    tm = words.shape[0]
    for p in range(n_seg):
        hn_ref[0, pl.ds(p, tm, stride=n_seg), :] = words[:, p * LANES:(p + 1) * LANES]
    qp_ref[0] = jnp.dot(hb, wq_ref[...], preferred_element_type=F32).astype(BF16)


def _merge(x, ys, ya, gs, ga, wsu, wau, wo, norm2_g, wq, s0, tm):
    bsz, sc, _ = ya.shape
    d = x.shape[2]
    i0 = s0 // tm
    loc = lambda a: pl.BlockSpec((1, tm, a.shape[2]), lambda b, i: (b, i, 0))
    full = lambda a: pl.BlockSpec(a.shape, lambda b, i: (0,) * a.ndim)
    g2 = norm2_g.reshape(1, d)
    nq = wq.shape[1]
    n_seg = d // 2 // LANES
    out = lambda n: pl.BlockSpec((1, tm, n), lambda b, i: (b, i, 0))
    return pl.pallas_call(
        _merge_kernel,
        grid=(bsz, sc // tm),
        in_specs=[pl.BlockSpec((1, tm, d), lambda b, i: (b, i + i0, 0)), loc(ys), loc(ya), loc(gs), loc(ga),
                  full(wsu), full(wau), full(wo), full(g2), full(wq)],
        out_specs=[out(d), pl.BlockSpec((1, tm * n_seg, LANES), lambda b, i: (b, i, 0)), out(nq)],
        out_shape=[jax.ShapeDtypeStruct((bsz, sc, d), F32), jax.ShapeDtypeStruct((bsz, sc * n_seg, LANES), I32),
                   jax.ShapeDtypeStruct((bsz, sc, nq), BF16)],
        compiler_params=_cparams(("arbitrary", "arbitrary")),
        name="merge",
    )(x, ys, ya, gs, ga, wsu, wau, wo, g2, wq)


def _cand_layout():
    blocks = []
    blocks.append((0, 16, 16))
    for i in range(1, 8):
        blocks.append((i, 8, PEER_TOPK // (i + 1)))
    blocks.append((None, 8, 8))
    return blocks


def _top_rows(s, order, payload, k):
    big = jnp.float32(3e38)
    vals, pays = [], []
    for _ in range(k):
        m = jnp.max(s, axis=0, keepdims=True)
        o = jnp.min(jnp.where(s == m, order, big), axis=0, keepdims=True)
        hit = order == o
        pays.append(o if payload is order else jnp.min(jnp.where(hit, payload, big), axis=0, keepdims=True))
        vals.append(m)
        s = jnp.where(hit, -jnp.inf, s)
    return jnp.concatenate(vals, axis=0), jnp.concatenate(pays, axis=0)


def _route_kernel(qp_ref, k1_ref, k2_ref, e_ref, g_ref, *, tt):
    qp = qp_ref[...]
    kd = PEER_KEY_DIM
    rows_k = lax.broadcasted_iota(I32, (PEER_KEYS, tt), 0).astype(F32)
    for h in range(PEER_HEADS):
        q1 = qp[:, (2 * h) * kd:(2 * h + 1) * kd]
        q2 = qp[:, (2 * h + 1) * kd:(2 * h + 2) * kd]
        s1 = lax.dot_general(k1_ref[h], q1, (((1,), (1,)), ((), ())), preferred_element_type=F32)
        s2 = lax.dot_general(k2_ref[h], q2, (((1,), (1,)), ((), ())), preferred_element_type=F32)
        v1, i1 = _top_rows(s1, rows_k, rows_k, PEER_TOPK)
        v2, i2 = _top_rows(s2, rows_k, rows_k, PEER_TOPK)
        cs, ce, co = [], [], []
        for i, rows, valid in _cand_layout():
            r = lax.broadcasted_iota(I32, (rows, tt), 0).astype(F32)
            if i is None:
                val = v1[8:16] + v2[0:1]
                eid = i1[8:16] * PEER_KEYS + i2[0:1]
                flat = (r + 8.0) * PEER_TOPK
            else:
                val = v1[i:i + 1] + v2[0:rows]
                eid = i1[i:i + 1] * PEER_KEYS + i2[0:rows]
                flat = r + float(i * PEER_TOPK)
                if valid < rows:
                    val = jnp.where(r < float(valid), val, -jnp.inf)
            cs.append(val)
            ce.append(eid)
            co.append(flat)
        cand = jnp.concatenate(cs, axis=0)
        top_s, top_e = _top_rows(cand, jnp.concatenate(co, axis=0), jnp.concatenate(ce, axis=0), PEER_TOPK)
        p = jnp.exp(top_s - top_s[0:1])
        gates = p / jnp.sum(p, axis=0, keepdims=True)
        e_ref[h * PEER_TOPK:(h + 1) * PEER_TOPK, :] = top_e.astype(I32)
        g_ref[h * PEER_TOPK:(h + 1) * PEER_TOPK, :] = gates


def _route(qp, k1, k2, tt):
    n, nq = qp.shape
    n_sel = PEER_HEADS * PEER_TOPK
    full = lambda a: pl.BlockSpec(a.shape, lambda i: (0,) * a.ndim)
    return pl.pallas_call(
        functools.partial(_route_kernel, tt=tt),
        grid=(n // tt,),
        in_specs=[pl.BlockSpec((tt, nq), lambda i: (i, 0)), full(k1), full(k2)],
        out_specs=[pl.BlockSpec((n_sel, tt), lambda i: (0, i)), pl.BlockSpec((n_sel, tt), lambda i: (0, i))],
        out_shape=[jax.ShapeDtypeStruct((n_sel, n), I32), jax.ShapeDtypeStruct((n_sel, n), F32)],
        compiler_params=_cparams(("arbitrary",)),
        name="route",
    )(qp, k1, k2)


def _final_kernel(h_ref, p_ref, g_ref, o_ref):
    tm, d = h_ref.shape
    n_seg = d // LANES
    p = jnp.concatenate([p_ref[pl.ds(k, tm, stride=n_seg), :] for k in range(n_seg)], axis=1)
    o_ref[...] = _rms(h_ref[...] + p, g_ref[...])


def _final(h, p, g, tm):
    n, d = h.shape
    row = pl.BlockSpec((tm, d), lambda i: (i, 0))
    return pl.pallas_call(
        _final_kernel,
        grid=(n // tm,),
        in_specs=[row, pl.BlockSpec((tm * (d // LANES), LANES), lambda i: (i, 0)),
                  pl.BlockSpec((1, d), lambda i: (0, 0))],
        out_specs=row,
        out_shape=jax.ShapeDtypeStruct((n, d), F32),
        compiler_params=_cparams(("arbitrary",)),
        name="final",
    )(h, p, g.reshape(1, d))


SC_CORES_V7X = 2
SC_SUBCORES_V7X = 16
SC_LANES_V7X = 16
PEER_TOK_BATCH = 32
PEER_ROW_CHUNK = 32
PEER_RING = 4


def _pack_words(x):
    bits = lax.bitcast_convert_type(x.astype(BF16).astype(F32), I32)
    half = bits.shape[1] // 2
    return (bits[:, half:] & jnp.int32(-65536)) | lax.shift_right_logical(bits[:, :half], 16)


def _pack_kernel(t_ref, o_ref):
    o_ref[...] = _pack_words(t_ref[...])


def _pack_bf16_pairs(t, rows=1024):
    e, d = t.shape
    return pl.pallas_call(
        _pack_kernel,
        grid=(e // rows,),
        in_specs=[pl.BlockSpec((rows, d), lambda i: (i, 0))],
        out_specs=pl.BlockSpec((rows, d // 2), lambda i: (i, 0)),
        out_shape=jax.ShapeDtypeStruct((e, d // 2), I32),
        compiler_params=_cparams(("arbitrary",)),
        name="pack_table",
    )(t)


def _unpack_pair(w):
    lo = lax.bitcast_convert_type(jnp.left_shift(w, 16), F32)
    hi = lax.bitcast_convert_type(w & jnp.int32(-65536), F32)
    return lo, hi


def _peer_sc_body(hn_hbm, e_hbm, g_hbm, u_hbm, v_hbm, out_hbm,
                  idx_v, gate_v, x_v, out_v, rows, p_v, act_v, sem, *, tpw, d, n_sel):
    nl = SC_LANES_V7X
    tb = PEER_TOK_BATCH
    rc = PEER_ROW_CHUNK
    n_chunk = n_sel // rc
    jobs_per_tok = 2 * n_chunk
    half = d // 2
    n_lane_blk = half // nl
    xs = half // LANES
    os_ = d // LANES
    wid =lax.axis_index("s") * SC_CORES_V7X + lax.axis_index("c")
    base = wid * tpw
    lane = lax.iota(I32, nl)
    zero = jnp.zeros((nl,), F32)
    c_gelu = 2.0 * math.sqrt(2.0 / math.pi)

    def gather_copy(tab_hbm, job):
        tok = job // jobs_per_tok
        c = (job % jobs_per_tok) % n_chunk
        b = job % PEER_RING
        return pltpu.make_async_copy(tab_hbm.at[idx_v.at[tok, pl.ds(c * rc, rc)]], rows.at[b], sem.at[b])

    def start(job):
        j = job % jobs_per_tok

        @pl.when(j < n_chunk)
        def _():
            gather_copy(u_hbm, job).start()

        @pl.when(j >= n_chunk)
        def _():
            gather_copy(v_hbm, job).start()

    def compute_u(tok, c, b):
        def rg_body(rg, _):
            r0 = rg * 8

            def jbody(j2, accs):
                off0 = j2 * (2 * nl)
                off1 = off0 + nl
                xrow = tok * xs + off0 // LANES
                xl = off0 % LANES
                x0 = plsc.bitcast(x_v[xrow, pl.ds(xl, nl)], BF16)
                x1 = plsc.bitcast(x_v[xrow, pl.ds(xl + nl, nl)], BF16)
                new = []
                for r in range(8):
                    w0 = plsc.bitcast(rows[b, r0 + r, pl.ds(off0, nl)], BF16)
                    w1 = plsc.bitcast(rows[b, r0 + r, pl.ds(off1, nl)], BF16)
                    lo, hi = _unpack_pair(plsc.bitcast(w0 * x0 + w1 * x1, I32))
                    new.append(accs[r] + (lo + hi))
                return tuple(new)

            accs = lax.fori_loop(0, n_lane_blk // 2, jbody, (zero,) * 8)
            for r in range(8):
                p_v[c * rc + r0 + r, :] = accs[r]
            return 0

        lax.fori_loop(0, rc // 8, rg_body, 0)

    def finish_act(tok):
        def eg_body(eg, _):
            e0 = eg * nl
            ridx = e0 + lane
            s = zero
            for l in range(nl):
                s = s + plsc.load_gather(p_v, [ridx, jnp.full((nl,), l, I32)])
            inner = c_gelu * (s + 0.044715 * (s * s * s))
            gl = s / (1.0 + jnp.exp(-inner))
            a = gl * gate_v[tok, pl.ds(e0, nl)]
            bits = lax.bitcast_convert_type(a, I32)
            rnd = bits + jnp.int32(0x7FFF) + (lax.shift_right_logical(bits, 16) & 1)
            hi16 = rnd & jnp.int32(-65536)
            act_v[pl.ds(e0, nl)] = hi16 | lax.shift_right_logical(hi16, 16)
            return 0

        lax.fori_loop(0, n_sel // nl, eg_body, 0)

        def zbody(j, _):
            off = j * nl
            out_v[tok * os_ + off // LANES, pl.ds(off % LANES, nl)] = zero
            return 0

        lax.fori_loop(0, d // nl, zbody, 0, unroll=4)

    def compute_v(tok, c, b):
        def rg_body(rg, _):
            r0 = rg * nl
            splat = [plsc.bitcast(plsc.load_gather(act_v, [jnp.full((nl,), 0, I32) + (c * rc + r0 + r)]), BF16)
                     for r in range(nl)]

            def tree(parts):
                while len(parts) > 1:
                    parts = [parts[i] + parts[i + 1] for i in range(0, len(parts), 2)]
                return parts[0]

            @plsc.parallel_loop(0, n_lane_blk, unroll=2)
            def _(j):
                off = j * nl
                los, his = [], []
                for r in range(0, nl, 2):
                    w0 = plsc.bitcast(rows[b, r0 + r, pl.ds(off, nl)], BF16)
                    w1 = plsc.bitcast(rows[b, r0 + r + 1, pl.ds(off, nl)], BF16)
                    lo, hi = _unpack_pair(plsc.bitcast(w0 * splat[r] + w1 * splat[r + 1], I32))
                    los.append(lo)
                    his.append(hi)
                orow = tok * os_ + off // LANES
                ol = off % LANES
                out_v[orow, pl.ds(ol, nl)] = out_v[orow, pl.ds(ol, nl)] + tree(los)
                out_v[orow + xs, pl.ds(ol, nl)] = out_v[orow + xs, pl.ds(ol, nl)] + tree(his)

            return 0

        lax.fori_loop(0, rc // nl, rg_body, 0)

    def batch_body(bi, _):
        t0 = base + bi * tb
        pltpu.sync_copy(e_hbm.at[pl.ds(t0, tb)], idx_v)
        pltpu.sync_copy(g_hbm.at[pl.ds(t0, tb)], gate_v)
        pltpu.sync_copy(hn_hbm.at[pl.ds(t0 * xs, tb * xs)], x_v)
        for pre in range(PEER_RING - 1):
            start(pre)

        def job_body(job, _):
            @pl.when(job + (PEER_RING - 1) < tb * jobs_per_tok)
            def _():
                start(job + (PEER_RING - 1))

            j = job % jobs_per_tok
            gather_copy(u_hbm, job).wait()
            tok = job // jobs_per_tok
            b = job % PEER_RING

            @pl.when(j < n_chunk)
            def _():
                compute_u(tok, j, b)

            @pl.when(j == n_chunk - 1)
            def _():
                finish_act(tok)

            @pl.when(j >= n_chunk)
            def _():
                compute_v(tok, j - n_chunk, b)

            return 0

        lax.fori_loop(0, tb * jobs_per_tok, job_body, 0)
        pltpu.sync_copy(out_v, out_hbm.at[pl.ds(t0 * os_, tb * os_)])
        return 0

    lax.fori_loop(0, tpw // tb, batch_body, 0)


def _peer_sc(x_pk, experts, gates, u_tab, v_tab):
    d = 2 * u_tab.shape[1]
    n = x_pk.shape[0] * LANES // (d // 2)
    n_sel = experts.shape[1]
    nw = SC_CORES_V7X * SC_SUBCORES_V7X
    tpw = n // nw
    mesh = plsc.VectorSubcoreMesh(core_axis_name="c", subcore_axis_name="s",
                                  num_cores=SC_CORES_V7X, num_subcores=SC_SUBCORES_V7X)
    body = functools.partial(_peer_sc_body, tpw=tpw, d=d, n_sel=n_sel)
    call = pl.kernel(
        body,
        out_type=jax.ShapeDtypeStruct((n * d // LANES, LANES), F32),
        mesh=mesh,
        scratch_types=[pltpu.VMEM((PEER_TOK_BATCH, n_sel), I32),
                       pltpu.VMEM((PEER_TOK_BATCH, n_sel), F32),
                       pltpu.VMEM((PEER_TOK_BATCH * d // 2 // LANES, LANES), I32),
                       pltpu.VMEM((PEER_TOK_BATCH * d // LANES, LANES), F32),
                       pltpu.VMEM((PEER_RING, PEER_ROW_CHUNK, d // 2), I32),
                       pltpu.VMEM((n_sel, SC_LANES_V7X), F32),
                       pltpu.VMEM((n_sel,), I32),
                       pltpu.SemaphoreType.DMA((PEER_RING,))],
        compiler_params=pltpu.CompilerParams(needs_layout_passes=False, use_tc_tiling_on_sc=False),
        name="peer_sc",
    )
    return call(x_pk, experts, gates, u_tab, v_tab)


def kernel(x, norm1_g, w_in, a_re, a_im, log_dt, b_re, b_im, c_re, c_im, d_skip, w_glu, w_ssm_up, w_attn_up,
           w_out, norm2_g, peer_wq, peer_k1, peer_k2, peer_u, peer_v, norm_f_g):
    bsz, seq, d = x.shape
    depth = norm1_g.shape[0]
    chunks = _time_chunks(seq)
    h = x
    for layer in range(depth):
        last = layer + 1 == depth
        kv_w, main_w = _in_weights(w_in[layer], seq)
        s5p = _s5_params(a_re[layer], a_im[layer], log_dt[layer], b_re[layer], b_im[layer], c_re[layer],
                         c_im[layer], d_skip[layer], w_glu[layer], nb=bsz)
        n_state = s5p[4].shape[1]
        wsu = w_ssm_up[layer].astype(BF16)
        wau = w_attn_up[layer].astype(BF16)
        wo = w_out[layer].astype(BF16)
        wq = peer_wq[layer].astype(BF16)
        k1 = peer_k1[layer].astype(BF16)
        k2 = peer_k2[layer].astype(BF16)
        u_pk = _pack_bf16_pairs(peer_u[layer])
        v_pk = _pack_bf16_pairs(peer_v[layer])
        kv_tm = min(ROW_TILE, seq)
        head_len = 2 * DSA_KT
        split_kv = len(chunks) > 1 and chunks[0][1] <= head_len < seq
        k4, vt, ki = _kvproj(h, norm1_g[layer], kv_w, norm1_g[layer], seq=head_len if split_kv else seq,
                             tm=kv_tm, kt=DSA_KT)
        st_re = jnp.zeros((bsz, n_state), F32)
        st_im = jnp.zeros((bsz, n_state), F32)
        outs = []
        routed = ki
        peer_outs = []
        for c, (s0, sc) in enumerate(chunks):
            tm = math.gcd(math.gcd(s0, sc), ROW_TILE)
            if split_kv and s0 + sc > head_len:
                k4, vt, ki = _kvproj(h, norm1_g[layer], kv_w, routed, seq=seq, tm=kv_tm, kt=DSA_KT)
                split_kv = False
            after = (routed, peer_outs[c - SC_LAG] if c >= SC_LAG else ki)
            u, q, qi, wit, gs, ga = _inproj(h, norm1_g[layer], main_w, after, s0=s0, sc=sc, tm=tm)
            d_ssm = u.shape[-1]
            u_tb = u.transpose(1, 0, 2).reshape(sc * bsz, d_ssm)
            y_tb, st_re, st_im = _s5(u_tb, st_re, st_im, s5p, nb=bsz, tc=S5_STEPS)
            ys = y_tb.reshape(sc, bsz, d_ssm).transpose(1, 0, 2)
            ya = _dsa(q, qi, wit, ki, k4, vt, s0=s0, seq_total=seq, tq=DSA_TQ, kt=DSA_KT)
            hm, x_pk, qp = _merge(h, ys, ya, gs, ga, wsu, wau, wo, norm2_g[layer], wq, s0=s0, tm=tm)
            nt = bsz * sc
            e_t, g_t = _route(qp.reshape(nt, -1), k1, k2, tt=ROUTE_TOKENS)
            routed = e_t
            po = _peer_sc(x_pk.reshape(-1, LANES), e_t.T, g_t.T, u_pk, v_pk)
            peer_outs.append(po)
            hm2 = hm.reshape(nt, d)
            o = _final(hm2, po, norm_f_g, tm=tm) if last else hm2 + po.reshape(nt, d)
            outs.append(o.reshape(bsz, sc, d))
        h = jnp.concatenate(outs, axis=1)
    return h
```

```python
import functools
import math

import numpy as np
import jax
import jax.numpy as jnp
from jax import lax
from jax.experimental import pallas as pl
from jax.experimental.pallas import tpu as pltpu
from jax.experimental.pallas import tpu_sc as plsc

F32 = jnp.float32
BF16 = jnp.bfloat16
I32 = jnp.int32

SSM_GROUP = 16
SSM_STATE = 64
ATTN_HEADS = 8
ATTN_KV_HEADS = 2
HEAD_DIM = 64
IDX_HEADS = 8
IDX_DIM = 32
TOPK_MAX = 256
ROPE_THETA = 10000.0
NEG_BIG = -1e30
PEER_HEADS = 8
PEER_KEYS = 128
PEER_KEY_DIM = 128
PEER_TOPK = 16
NORM_EPS = 1e-6

TIME_SPLIT_32NDS = (1, 3, 4, 4, 4, 4, 4, 3, 2, 2, 1)
SC_LAG = 3
ROW_TILE = 512
S5_STEPS = 64
ROUTE_TOKENS = 256
DSA_TQ = 128
DSA_KT = 256
LANES = 128
INT_MIN = -(2 ** 31)
VMEM_LIMIT = 56 * 1024 * 1024


def _time_chunks(seq):
    unit = seq // 32
    if seq % 32 == 0 and unit % DSA_TQ == 0:
        sizes = [f * unit for f in TIME_SPLIT_32NDS]
    else:
        step = min(ROW_TILE, seq)
        sizes = [step] * (seq // step)
    assert sum(sizes) == seq
    starts = np.cumsum([0] + sizes[:-1]).tolist()
    return list(zip(starts, sizes))


def _cparams(sem):
    return pltpu.CompilerParams(dimension_semantics=sem, vmem_limit_bytes=VMEM_LIMIT)


def _gelu_tanh(x):
    return 0.5 * x * (1.0 + jnp.tanh(math.sqrt(2.0 / math.pi) * (x + 0.044715 * (x * x * x))))


def _sigmoid(x):
    return 1.0 / (1.0 + jnp.exp(-x))


def _rms(x, g):
    return x * lax.rsqrt(jnp.mean(x * x, axis=-1, keepdims=True) + NORM_EPS) * g


def _rot_cols(w, hd):
    d, n = w.shape
    w3 = w.reshape(d, n // hd, hd)
    half = hd // 2
    return jnp.concatenate([-w3[..., half:], w3[..., :half]], axis=-1).reshape(d, n)


def _rope_full(seq, hd, heads):
    pos = jnp.arange(seq, dtype=F32)
    inv = ROPE_THETA ** (-jnp.arange(0, hd, 2, dtype=F32) / hd)
    ang = pos[:, None] * inv[None, :]
    c = jnp.concatenate([jnp.cos(ang), jnp.cos(ang)], axis=-1)
    s = jnp.concatenate([jnp.sin(ang), jnp.sin(ang)], axis=-1)
    return jnp.tile(c, (1, heads)), jnp.tile(s, (1, heads))


def _in_weights(w_in, seq):
    d = w_in.shape[0]
    d_ssm = d // 2
    d_q = ATTN_HEADS * HEAD_DIM
    d_kv = ATTN_KV_HEADS * HEAD_DIM
    d_qi = IDX_HEADS * IDX_DIM
    splits = (d_ssm, d_q, d_kv, d_kv, d_qi, IDX_DIM, IDX_HEADS, d, d)
    offs = np.cumsum(splits)[:-1].tolist()
    wu, wq, wk, wv, wqi, wki, wwi, wgs, wga = jnp.split(w_in, offs, axis=1)
    pad = jnp.zeros((d, 128 - IDX_DIM), F32)
    cq, sq = _rope_full(seq, HEAD_DIM, ATTN_HEADS)
    ck, sk = _rope_full(seq, HEAD_DIM, ATTN_KV_HEADS)
    cqi, sqi = _rope_full(seq, IDX_DIM, IDX_HEADS)
    cki, ski = _rope_full(seq, IDX_DIM, 1)
    tpad = jnp.zeros((seq, 128 - IDX_DIM), F32)
    kv = dict(
        w=jnp.concatenate([wk, wki, pad], axis=1).astype(BF16),
        wvt=wv.T.astype(BF16),
        wr=jnp.concatenate([_rot_cols(wk, HEAD_DIM), _rot_cols(wki, IDX_DIM), pad], axis=1).astype(BF16),
        cs=jnp.concatenate([ck, cki, tpad], axis=1), sn=jnp.concatenate([sk, ski, tpad], axis=1))
    main = dict(
        w=jnp.concatenate([wu, wq, wqi, wgs, wga], axis=1).astype(BF16),
        wr=jnp.concatenate([_rot_cols(wq, HEAD_DIM), _rot_cols(wqi, IDX_DIM)], axis=1).astype(BF16),
        cs=jnp.concatenate([cq, cqi], axis=1), sn=jnp.concatenate([sq, sqi], axis=1),
        wwit=wwi.T.astype(BF16))
    return kv, main


def _kvproj_kernel(x_ref, g_ref, w_ref, wr_ref, wvt_ref, cs_ref, sn_ref, after_ref, k_ref, vt_ref, ki_ref,
                   *, d_kv, kt):
    del after_ref
    xb = _rms(x_ref[0], g_ref[...]).astype(BF16)

    def mm(ref, lo, n):
        return jnp.dot(xb, ref[:, lo:lo + n], preferred_element_type=F32)

    k = mm(w_ref, 0, d_kv) * cs_ref[:, :d_kv] + mm(wr_ref, 0, d_kv) * sn_ref[:, :d_kv]
    for n in range(ATTN_KV_HEADS):
        k_ref[0, n] = k[:, n * HEAD_DIM:(n + 1) * HEAD_DIM].astype(BF16)
    vt = lax.dot_general(wvt_ref[...], xb, (((1,), (1,)), ((), ())), preferred_element_type=F32)
    for n in range(ATTN_KV_HEADS):
        for j in range(vt.shape[1] // kt):
            vt_ref[0, n, j] = vt[n * HEAD_DIM:(n + 1) * HEAD_DIM, j * kt:(j + 1) * kt].astype(BF16)
    kiw = (mm(w_ref, d_kv, 128) * cs_ref[:, d_kv:d_kv + 128]
           + mm(wr_ref, d_kv, 128) * sn_ref[:, d_kv:d_kv + 128])
    ki_ref[0] = kiw[:, :IDX_DIM].astype(BF16)


def _kvproj(x, norm_g, kv, after, seq, tm, kt):
    bsz, _, d = x.shape
    d_kv = ATTN_KV_HEADS * HEAD_DIM
    full = lambda a: pl.BlockSpec(a.shape, lambda s, b: (0,) * a.ndim)
    g = norm_g.reshape(1, d)
    ncs = kv["cs"].shape[1]
    return pl.pallas_call(
        functools.partial(_kvproj_kernel, d_kv=d_kv, kt=kt),
        grid=(seq // tm, bsz),
        in_specs=[pl.BlockSpec((1, tm, d), lambda s, b: (b, s, 0)), full(g), full(kv["w"]), full(kv["wr"]),
                  full(kv["wvt"]),
                  pl.BlockSpec((tm, ncs), lambda s, b: (s, 0)), pl.BlockSpec((tm, ncs), lambda s, b: (s, 0)),
                  pl.BlockSpec(memory_space=pl.ANY)],
        out_specs=[pl.BlockSpec((1, ATTN_KV_HEADS, tm, HEAD_DIM), lambda s, b: (b, 0, s, 0)),
                   pl.BlockSpec((1, ATTN_KV_HEADS, tm // kt, HEAD_DIM, kt), lambda s, b: (b, 0, s, 0, 0)),
                   pl.BlockSpec((1, tm, IDX_DIM), lambda s, b: (b, s, 0))],
        out_shape=[jax.ShapeDtypeStruct((bsz, ATTN_KV_HEADS, seq, HEAD_DIM), BF16),
                   jax.ShapeDtypeStruct((bsz, ATTN_KV_HEADS, seq // kt, HEAD_DIM, kt), BF16),
                   jax.ShapeDtypeStruct((bsz, seq, IDX_DIM), BF16)],
        compiler_params=_cparams(("arbitrary", "arbitrary")),
        name="kvproj",
    )(x, g, kv["w"], kv["wr"], kv["wvt"], kv["cs"], kv["sn"], after)


def _inproj_kernel(x_ref, g_ref, w_ref, wr_ref, cs_ref, sn_ref, wwit_ref, after_tc_ref, after_sc_ref,
                   u_ref, q_ref, qi_ref, wit_ref, gs_ref, ga_ref, *, d_ssm, d_q, d_qi, d_model, q_scale, wi_scale):
    del after_tc_ref, after_sc_ref
    xb = _rms(x_ref[0], g_ref[...]).astype(BF16)

    def mm(ref, lo, n):
        return jnp.dot(xb, ref[:, lo:lo + n], preferred_element_type=F32)

    o = 0
    u_ref[0] = mm(w_ref, o, d_ssm).astype(BF16)
    o += d_ssm
    q = mm(w_ref, o, d_q) * cs_ref[:, :d_q] + mm(wr_ref, 0, d_q) * sn_ref[:, :d_q]
    q_ref[0] = (q * q_scale).astype(BF16)
    o += d_q
    qi = mm(w_ref, o, d_qi) * cs_ref[:, d_q:d_q + d_qi] + mm(wr_ref, d_q, d_qi) * sn_ref[:, d_q:d_q + d_qi]
    qi_ref[0] = qi.astype(BF16)
    o += d_qi
    gs_ref[0] = _sigmoid(mm(w_ref, o, d_model)).astype(BF16)
    o += d_model
    ga_ref[0] = _sigmoid(mm(w_ref, o, d_model)).astype(BF16)
    wit_ref[0] = lax.dot_general(wwit_ref[...], xb, (((1,), (1,)), ((), ())),
                                 preferred_element_type=F32) * wi_scale


def _inproj(x, norm_g, main, after, s0, sc, tm):
    bsz, _, d = x.shape
    d_ssm = d // 2
    d_q = ATTN_HEADS * HEAD_DIM
    d_qi = IDX_HEADS * IDX_DIM
    i0 = s0 // tm
    kern = functools.partial(
        _inproj_kernel, d_ssm=d_ssm, d_q=d_q, d_qi=d_qi, d_model=d,
        q_scale=HEAD_DIM ** -0.5 * math.log2(math.e), wi_scale=(IDX_HEADS ** -0.5) * (IDX_DIM ** -0.5))
    tok = lambda n: pl.BlockSpec((1, tm, n), lambda s, b: (b, s, 0))
    full = lambda a: pl.BlockSpec(a.shape, lambda s, b: (0,) * a.ndim)
    g = norm_g.reshape(1, d)
    ncs = main["cs"].shape[1]
    outs = [(d_ssm, BF16), (d_q, BF16), (d_qi, BF16)]
    return pl.pallas_call(
        kern,
        grid=(sc // tm, bsz),
        in_specs=[pl.BlockSpec((1, tm, d), lambda s, b: (b, s + i0, 0)), full(g), full(main["w"]), full(main["wr"]),
                  pl.BlockSpec((tm, ncs), lambda s, b: (s + i0, 0)),
                  pl.BlockSpec((tm, ncs), lambda s, b: (s + i0, 0)), full(main["wwit"]),
                  pl.BlockSpec(memory_space=pl.ANY), pl.BlockSpec(memory_space=pl.ANY)],
        out_specs=[tok(n) for n, _ in outs] + [pl.BlockSpec((1, IDX_HEADS, tm), lambda s, b: (b, 0, s)),
                                                tok(d), tok(d)],
        out_shape=[jax.ShapeDtypeStruct((bsz, sc, n), dt) for n, dt in outs]
        + [jax.ShapeDtypeStruct((bsz, IDX_HEADS, sc), F32),
           jax.ShapeDtypeStruct((bsz, sc, d), BF16), jax.ShapeDtypeStruct((bsz, sc, d), BF16)],
        compiler_params=_cparams(("arbitrary", "arbitrary")),
        name="inproj",
    )(x, g, main["w"], main["wr"], main["cs"], main["sn"], main["wwit"], *after)


def _s5_kernel(u_ref, sre_in, sim_in, bre_ref, bim_ref, cre_ref, cim_ref, are_ref, aim_ref, dsk_ref, wglu_ref,
               y_ref, st_re, st_im, sre, sim, *, tc, nb, lane_chunk):
    @pl.when(pl.program_id(0) == 0)
    def _():
        st_re[...] = sre_in[...]
        st_im[...] = sim_in[...]

    u = u_ref[...]
    n_half = bre_ref.shape[0]
    hin = bre_ref.shape[1]
    hst = bre_ref.shape[2]
    for h in range(n_half):
        uh = u[:, h * hin:(h + 1) * hin]
        sre[:, h * hst:(h + 1) * hst] = jnp.dot(uh, bre_ref[h], preferred_element_type=F32)
        sim[:, h * hst:(h + 1) * hst] = jnp.dot(uh, bim_ref[h], preferred_element_type=F32)

    n_state = sre.shape[1]
    for c in range(n_state // lane_chunk):
        cols = slice(c * lane_chunk, (c + 1) * lane_chunk)
        ar = are_ref[:, cols]
        ai = aim_ref[:, cols]

        def step(t, carry, cols=cols, ar=ar, ai=ai):
            sr, si = carry
            r0 = pl.multiple_of(t * nb, nb)
            nr = ar * sr - ai * si + sre[pl.ds(r0, nb), cols]
            ni = ar * si + ai * sr + sim[pl.ds(r0, nb), cols]
            sre[pl.ds(r0, nb), cols] = nr
            sim[pl.ds(r0, nb), cols] = ni
            return nr, ni

        sr, si = lax.fori_loop(0, tc, step, (st_re[:, cols], st_im[:, cols]), unroll=4)
        st_re[:, cols] = sr
        st_im[:, cols] = si

    ys = []
    for h in range(n_half):
        srh = sre[:, h * hst:(h + 1) * hst].astype(BF16)
        sih = sim[:, h * hst:(h + 1) * hst].astype(BF16)
        ys.append(jnp.dot(srh, cre_ref[h], preferred_element_type=F32)
                  - jnp.dot(sih, cim_ref[h], preferred_element_type=F32))
    y = jnp.concatenate(ys, axis=-1) + dsk_ref[...] * u.astype(F32)
    y = _gelu_tanh(y)
    gate = jnp.dot(y.astype(BF16), wglu_ref[...], preferred_element_type=F32)
    y_ref[...] = (y * _sigmoid(gate)).astype(BF16)


def _s5_params(a_re, a_im, log_dt, b_re, b_im, c_re, c_im, d_skip, w_glu, nb):
    groups = a_re.shape[0]
    d_ssm = groups * SSM_GROUP
    n_state = groups * SSM_STATE
    lam = lax.complex(a_re, a_im)
    dt = jnp.exp(log_dt)[:, None]
    a_bar = jnp.exp(lam * dt)
    b_bar = ((a_bar - 1.0) / lam)[..., None] * lax.complex(b_re, b_im)
    gh = min(groups, 256 // SSM_GROUP)
    n_half = groups // gh
    eye = jnp.eye(gh, dtype=F32)

    def bmat(bb):
        b4 = bb.reshape(n_half, gh, SSM_STATE, SSM_GROUP)
        return jnp.einsum('hgpc,gk->hgckp', b4, eye).reshape(n_half, gh * SSM_GROUP, gh * SSM_STATE)

    def cmat(cc):
        c4 = cc.reshape(n_half, gh, SSM_GROUP, SSM_STATE)
        return jnp.einsum('hgcp,gk->hgpkc', c4, eye).reshape(n_half, gh * SSM_STATE, gh * SSM_GROUP)

    return (bmat(jnp.real(b_bar)).astype(BF16), bmat(jnp.imag(b_bar)).astype(BF16),
            cmat(c_re).astype(BF16), cmat(c_im).astype(BF16),
            jnp.broadcast_to(jnp.real(a_bar).reshape(1, n_state), (nb, n_state)),
            jnp.broadcast_to(jnp.imag(a_bar).reshape(1, n_state), (nb, n_state)),
            d_skip.reshape(1, d_ssm), w_glu.astype(BF16))


def _s5(u_tb, st_re, st_im, params, nb, tc):
    rows, d_ssm = u_tb.shape
    n_state = st_re.shape[1]
    blk = tc * nb
    full = lambda a: pl.BlockSpec(a.shape, lambda i: (0,) * a.ndim)
    st_spec = pl.BlockSpec((nb, n_state), lambda i: (0, 0))
    kern = functools.partial(_s5_kernel, tc=tc, nb=nb, lane_chunk=512)
    return pl.pallas_call(
        kern,
        grid=(rows // blk,),
        in_specs=[pl.BlockSpec((blk, d_ssm), lambda i: (i, 0)), st_spec, st_spec] + [full(p) for p in params],
        out_specs=[pl.BlockSpec((blk, d_ssm), lambda i: (i, 0)), st_spec, st_spec],
        out_shape=[jax.ShapeDtypeStruct((rows, d_ssm), BF16),
                   jax.ShapeDtypeStruct((nb, n_state), F32), jax.ShapeDtypeStruct((nb, n_state), F32)],
        scratch_shapes=[pltpu.VMEM((blk, n_state), F32), pltpu.VMEM((blk, n_state), F32)],
        compiler_params=_cparams(("arbitrary",)),
        name="s5",
    )(u_tb, st_re, st_im, *params)


PART_ROWS = 32


def _dsa_kernel(qi_ref, wit_ref, q_ref, ki_ref, k_ref, vt_ref, o_ref, key_s, bias_s, lg_s,
                *, qb0, tq, kt, sub, topk, seq_bits):
    qb = pl.program_id(1) + qb0
    nkt = ((qb * tq + tq + sub * kt - 1) // (sub * kt)) * sub
    q_pos = qb * tq + lax.broadcasted_iota(I32, (1, tq), 1)
    k_eff = jnp.minimum(topk, q_pos + 1).astype(F32)

    qi = qi_ref[0]
    wit = wit_ref[0]
    qipair = [jnp.concatenate([qi[:, (2 * hp) * IDX_DIM:(2 * hp + 1) * IDX_DIM],
                               qi[:, (2 * hp + 1) * IDX_DIM:(2 * hp + 2) * IDX_DIM]], axis=0)
              for hp in range(IDX_HEADS // 2)]

    def key_pos(t):
        return t * kt + lax.broadcasted_iota(I32, (kt, tq), 0)

    def score_tile(t2, _):
        for hf in range(sub):
            r0 = pl.multiple_of((t2 * sub + hf) * kt, kt)
            ki_t = ki_ref[0, pl.ds(r0, kt), :]
            for hp in range(IDX_HEADS // 2):
                lg_s[hf, hp] = lax.dot_general(ki_t, qipair[hp], (((1,), (1,)), ((), ())),
                                               preferred_element_type=F32)
        for hf in range(sub):
            t = t2 * sub + hf
            r0 = pl.multiple_of(t * kt, kt)
            sc = jnp.zeros((kt, tq), F32)
            for hp in range(IDX_HEADS // 2):
                rel = lg_s[hf, hp]
                sc = sc + jnp.maximum(rel[:, :tq], 0.0) * wit[2 * hp:2 * hp + 1, :]
                sc = sc + jnp.maximum(rel[:, tq:], 0.0) * wit[2 * hp + 1:2 * hp + 2, :]
            bits = lax.bitcast_convert_type(sc, I32)
            key = jnp.where(bits < 0, bits ^ jnp.int32(0x7FFFFFFF), bits)
            key = jnp.where(key_pos(t) <= q_pos, key, jnp.int32(INT_MIN))
            key_s[pl.ds(r0, kt), :] = key
        return 0

    lax.fori_loop(0, nkt // sub, score_tile, 0)

    def count(pred_fn):
        def body(t2, acc):
            for hf in range(sub):
                t = t2 * sub + hf
                r0 = pl.multiple_of(t * kt, kt)
                m = pred_fn(key_s[pl.ds(r0, kt), :], t)
                ones = jnp.where(m, 1.0, 0.0).reshape(kt // PART_ROWS, PART_ROWS, tq)
                acc = acc + jnp.sum(ones, axis=0)
            return acc
        acc = lax.fori_loop(0, nkt // sub, body, jnp.zeros((PART_ROWS, tq), F32))
        return jnp.sum(acc, axis=0, keepdims=True)

    def bit_step(i, carry):
        u, cnt_u = carry
        bit = jnp.left_shift(jnp.int32(1), 31 - i)
        cand_u = u | bit
        cand_s = cand_u ^ jnp.int32(INT_MIN)
        cnt = count(lambda kk, t: kk >= cand_s)
        ok = cnt >= k_eff
        return jnp.where(ok, cand_u, u), jnp.where(ok, cnt, cnt_u)

    all_keys = (nkt * kt).astype(F32)
    u_thr, cnt_ge = lax.fori_loop(0, 32, bit_step, (jnp.zeros((1, tq), I32), jnp.zeros((1, tq), F32) + all_keys))
    thr = u_thr ^ jnp.int32(INT_MIN)
    has_tie = jnp.max(cnt_ge - k_eff) > 0.0

    def tie_cut():
        need_eq = k_eff - count(lambda kk, t: kk > thr)

        def pos_step(i, c):
            bit = jnp.left_shift(jnp.int32(1), seq_bits - 1 - i)
            cand = c | bit
            cnt = count(lambda kk, t: (kk == thr) & (key_pos(t) < cand))
            return jnp.where(cnt < need_eq, cand, c)
        return lax.fori_loop(0, seq_bits, pos_step, jnp.zeros((1, tq), I32))

    cut = lax.cond(has_tie, tie_cut, lambda: jnp.full((1, tq), 2 ** seq_bits, I32))

    def bias_tile(t, _):
        r0 = pl.multiple_of(t * kt, kt)
        key = key_s[pl.ds(r0, kt), :]
        sel = (key > thr) | ((key == thr) & (key_pos(t) <= cut))
        bias_s[pl.ds(r0, kt), :] = jnp.where(sel, 0.0, NEG_BIG)
        return 0

    lax.fori_loop(0, nkt, bias_tile, 0)

    q = q_ref[0]
    grp = ATTN_HEADS // ATTN_KV_HEADS
    pairs_per_kv = grp // 2
    n_unit = ATTN_KV_HEADS * pairs_per_kv
    wq = 2 * tq
    qpair = [jnp.concatenate([q[:, (2 * u) * HEAD_DIM:(2 * u + 1) * HEAD_DIM],
                              q[:, (2 * u + 1) * HEAD_DIM:(2 * u + 2) * HEAD_DIM]], axis=0)
             for u in range(n_unit)]

    def col_reduce(x, op):
        part = op(x.reshape(kt // PART_ROWS, PART_ROWS, wq), axis=0)
        return op(part, axis=0, keepdims=True)

    def attn_tile(t, carry):
        ms, ls, accs = list(carry[0]), list(carry[1]), list(carry[2])
        for hf in range(sub):
            r0 = pl.multiple_of((t * sub + hf) * kt, kt)
            bias = bias_s[pl.ds(r0, kt), :]
            bias2 = jnp.concatenate([bias, bias], axis=1)
            for u in range(n_unit):
                k_t = k_ref[0, u // pairs_per_kv, pl.ds(r0, kt), :]
                lg_s[hf, u] = lax.dot_general(k_t, qpair[u], (((1,), (1,)), ((), ())),
                                              preferred_element_type=F32) + bias2
        for hf in range(sub):
            for u in range(n_unit):
                lg = lg_s[hf, u]
                m_new = jnp.maximum(ms[u], col_reduce(lg, jnp.max))
                p = jnp.exp2(lg - m_new)
                alpha = jnp.exp2(ms[u] - m_new)
                v_t = vt_ref[0, u // pairs_per_kv, t * sub + hf]
                ls[u] = alpha * ls[u] + col_reduce(p, jnp.sum)
                accs[u] = alpha * accs[u] + jnp.dot(v_t, p.astype(BF16), preferred_element_type=F32)
                ms[u] = m_new
        return tuple(ms), tuple(ls), tuple(accs)

    init = (tuple(jnp.full((1, wq), NEG_BIG, F32) for _ in range(n_unit)),
            tuple(jnp.zeros((1, wq), F32) for _ in range(n_unit)),
            tuple(jnp.zeros((HEAD_DIM, wq), F32) for _ in range(n_unit)))
    _, ls, accs = lax.fori_loop(0, nkt // sub, attn_tile, init)
    for n in range(ATTN_KV_HEADS):
        o_ref[0, 0, n] = jnp.concatenate([accs[n * pairs_per_kv + pg] / ls[n * pairs_per_kv + pg]
                                          for pg in range(pairs_per_kv)], axis=1).astype(BF16)


def _dsa(q, qi, wit, ki, k4, vt, s0, seq_total, tq, kt):
    bsz, sc, _ = q.shape
    seq = ki.shape[1]
    topk = min(TOPK_MAX, seq_total // 4)
    nqb = sc // tq
    grp = ATTN_HEADS // ATTN_KV_HEADS
    seq_bits = int(math.log2(seq))
    assert 2 ** seq_bits == seq
    sub = 2 if seq % (2 * kt) == 0 else 1
    n_unit = ATTN_HEADS // 2
    assert IDX_HEADS // 2 <= n_unit
    kern =functools.partial(_dsa_kernel, qb0=s0 // tq, tq=tq, kt=kt, sub=sub, topk=topk, seq_bits=seq_bits)
    o_t = pl.pallas_call(
        kern,
        grid=(bsz, nqb),
        in_specs=[pl.BlockSpec((1, tq, IDX_HEADS * IDX_DIM), lambda b, j: (b, j, 0)),
                  pl.BlockSpec((1, IDX_HEADS, tq), lambda b, j: (b, 0, j)),
                  pl.BlockSpec((1, tq, ATTN_HEADS * HEAD_DIM), lambda b, j: (b, j, 0)),
                  pl.BlockSpec((1, seq, IDX_DIM), lambda b, j: (b, 0, 0)),
                  pl.BlockSpec((1, ATTN_KV_HEADS, seq, HEAD_DIM), lambda b, j: (b, 0, 0, 0)),
                  pl.BlockSpec((1, ATTN_KV_HEADS, seq // kt, HEAD_DIM, kt), lambda b, j: (b, 0, 0, 0, 0))],
        out_specs=pl.BlockSpec((1, 1, ATTN_KV_HEADS, HEAD_DIM, grp * tq), lambda b, j: (b, j, 0, 0, 0)),
        out_shape=jax.ShapeDtypeStruct((bsz, nqb, ATTN_KV_HEADS, HEAD_DIM, grp * tq), BF16),
        scratch_shapes=[pltpu.VMEM((seq, tq), I32), pltpu.VMEM((seq, tq), F32),
                        pltpu.VMEM((sub, n_unit, kt, 2 * tq), F32)],
        compiler_params=_cparams(("arbitrary", "arbitrary")),
        name="dsa",
    )(qi, wit, q, ki, k4, vt)
    o = o_t.reshape(bsz, nqb, ATTN_KV_HEADS, HEAD_DIM, grp, tq).transpose(0, 1, 5, 2, 4, 3)
    return o.reshape(bsz, sc, ATTN_HEADS * HEAD_DIM)


def _merge_kernel(x_ref, ys_ref, ya_ref, gs_ref, ga_ref, wsu_ref, wau_ref, wout_ref, g2_ref, wq_ref,
                  h_ref, hn_ref, qp_ref):
    ms = jnp.dot(ys_ref[0], wsu_ref[...], preferred_element_type=F32)
    ma = jnp.dot(ya_ref[0], wau_ref[...], preferred_element_type=F32)
    merged = gs_ref[0].astype(F32) * ms + ga_ref[0].astype(F32) * ma
    h = x_ref[0] + jnp.dot(merged.astype(BF16), wout_ref[...], preferred_element_type=F32)
    h_ref[0] = h
    hb = _rms(h, g2_ref[...]).astype(BF16)
    words = _pack_words(hb)
    half = words.shape[1]
    n_seg = half // LANES
    tm = words.shape[0]
    for p in range(n_seg):
        hn_ref[0, pl.ds(p, tm, stride=n_seg), :] = words[:, p * LANES:(p + 1) * LANES]
    qp_ref[0] = jnp.dot(hb, wq_ref[...], preferred_element_type=F32).astype(BF16)


def _merge(x, ys, ya, gs, ga, wsu, wau, wo, norm2_g, wq, s0, tm):
    bsz, sc, _ = ya.shape
    d = x.shape[2]
    i0 = s0 // tm
    loc = lambda a: pl.BlockSpec((1, tm, a.shape[2]), lambda b, i: (b, i, 0))
    full = lambda a: pl.BlockSpec(a.shape, lambda b, i: (0,) * a.ndim)
    g2 = norm2_g.reshape(1, d)
    nq = wq.shape[1]
    n_seg = d // 2 // LANES
    out = lambda n: pl.BlockSpec((1, tm, n), lambda b, i: (b, i, 0))
    return pl.pallas_call(
        _merge_kernel,
        grid=(bsz, sc // tm),
        in_specs=[pl.BlockSpec((1, tm, d), lambda b, i: (b, i + i0, 0)), loc(ys), loc(ya), loc(gs), loc(ga),
                  full(wsu), full(wau), full(wo), full(g2), full(wq)],
        out_specs=[out(d), pl.BlockSpec((1, tm * n_seg, LANES), lambda b, i: (b, i, 0)), out(nq)],
        out_shape=[jax.ShapeDtypeStruct((bsz, sc, d), F32), jax.ShapeDtypeStruct((bsz, sc * n_seg, LANES), I32),
                   jax.ShapeDtypeStruct((bsz, sc, nq), BF16)],
        compiler_params=_cparams(("arbitrary", "arbitrary")),
        name="merge",
    )(x, ys, ya, gs, ga, wsu, wau, wo, g2, wq)


def _cand_layout():
    blocks = []
    blocks.append((0, 16, 16))
    for i in range(1, 8):
        blocks.append((i, 8, PEER_TOPK // (i + 1)))
    blocks.append((None, 8, 8))
    return blocks


def _top_rows(s, order, payload, k):
    big = jnp.float32(3e38)
    vals, pays = [], []
    for _ in range(k):
        m = jnp.max(s, axis=0, keepdims=True)
        o = jnp.min(jnp.where(s == m, order, big), axis=0, keepdims=True)
        hit = order == o
        pays.append(o if payload is order else jnp.min(jnp.where(hit, payload, big), axis=0, keepdims=True))
        vals.append(m)
        s = jnp.where(hit, -jnp.inf, s)
    return jnp.concatenate(vals, axis=0), jnp.concatenate(pays, axis=0)


def _route_kernel(qp_ref, k1_ref, k2_ref, e_ref, g_ref, *, tt):
    qp = qp_ref[...]
    kd = PEER_KEY_DIM
    rows_k = lax.broadcasted_iota(I32, (PEER_KEYS, tt), 0).astype(F32)
    for h in range(PEER_HEADS):
        q1 = qp[:, (2 * h) * kd:(2 * h + 1) * kd]
        q2 = qp[:, (2 * h + 1) * kd:(2 * h + 2) * kd]
        s1 = lax.dot_general(k1_ref[h], q1, (((1,), (1,)), ((), ())), preferred_element_type=F32)
        s2 = lax.dot_general(k2_ref[h], q2, (((1,), (1,)), ((), ())), preferred_element_type=F32)
        v1, i1 = _top_rows(s1, rows_k, rows_k, PEER_TOPK)
        v2, i2 = _top_rows(s2, rows_k, rows_k, PEER_TOPK)
        cs, ce, co = [], [], []
        for i, rows, valid in _cand_layout():
            r = lax.broadcasted_iota(I32, (rows, tt), 0).astype(F32)
            if i is None:
                val = v1[8:16] + v2[0:1]
                eid = i1[8:16] * PEER_KEYS + i2[0:1]
                flat = (r + 8.0) * PEER_TOPK
            else:
                val = v1[i:i + 1] + v2[0:rows]
                eid = i1[i:i + 1] * PEER_KEYS + i2[0:rows]
                flat = r + float(i * PEER_TOPK)
                if valid < rows:
                    val = jnp.where(r < float(valid), val, -jnp.inf)
            cs.append(val)
            ce.append(eid)
            co.append(flat)
        cand = jnp.concatenate(cs, axis=0)
        top_s, top_e = _top_rows(cand, jnp.concatenate(co, axis=0), jnp.concatenate(ce, axis=0), PEER_TOPK)
        p = jnp.exp(top_s - top_s[0:1])
        gates = p / jnp.sum(p, axis=0, keepdims=True)
        e_ref[h * PEER_TOPK:(h + 1) * PEER_TOPK, :] = top_e.astype(I32)
        g_ref[h * PEER_TOPK:(h + 1) * PEER_TOPK, :] = gates


def _route(qp, k1, k2, tt):
    n, nq = qp.shape
    n_sel = PEER_HEADS * PEER_TOPK
    full = lambda a: pl.BlockSpec(a.shape, lambda i: (0,) * a.ndim)
    return pl.pallas_call(
        functools.partial(_route_kernel, tt=tt),
        grid=(n // tt,),
        in_specs=[pl.BlockSpec((tt, nq), lambda i: (i, 0)), full(k1), full(k2)],
        out_specs=[pl.BlockSpec((n_sel, tt), lambda i: (0, i)), pl.BlockSpec((n_sel, tt), lambda i: (0, i))],
        out_shape=[jax.ShapeDtypeStruct((n_sel, n), I32), jax.ShapeDtypeStruct((n_sel, n), F32)],
        compiler_params=_cparams(("arbitrary",)),
        name="route",
    )(qp, k1, k2)


def _final_kernel(h_ref, p_ref, g_ref, o_ref):
    tm, d = h_ref.shape
    n_seg = d // LANES
    p = jnp.concatenate([p_ref[pl.ds(k, tm, stride=n_seg), :] for k in range(n_seg)], axis=1)
    o_ref[...] = _rms(h_ref[...] + p, g_ref[...])


def _final(h, p, g, tm):
    n, d = h.shape
    row = pl.BlockSpec((tm, d), lambda i: (i, 0))
    return pl.pallas_call(
        _final_kernel,
        grid=(n // tm,),
        in_specs=[row, pl.BlockSpec((tm * (d // LANES), LANES), lambda i: (i, 0)),
                  pl.BlockSpec((1, d), lambda i: (0, 0))],
        out_specs=row,
        out_shape=jax.ShapeDtypeStruct((n, d), F32),
        compiler_params=_cparams(("arbitrary",)),
        name="final",
    )(h, p, g.reshape(1, d))


SC_CORES_V7X = 2
SC_SUBCORES_V7X = 16
SC_LANES_V7X = 16
PEER_TOK_BATCH = 32
PEER_ROW_CHUNK = 32
PEER_RING = 4


def _pack_words(x):
    bits = lax.bitcast_convert_type(x.astype(BF16).astype(F32), I32)
    half = bits.shape[1] // 2
    return (bits[:, half:] & jnp.int32(-65536)) | lax.shift_right_logical(bits[:, :half], 16)


def _pack_kernel(t_ref, o_ref):
    o_ref[...] = _pack_words(t_ref[...])


def _pack_bf16_pairs(t, rows=1024):
    e, d = t.shape
    return pl.pallas_call(
        _pack_kernel,
        grid=(e // rows,),
        in_specs=[pl.BlockSpec((rows, d), lambda i: (i, 0))],
        out_specs=pl.BlockSpec((rows, d // 2), lambda i: (i, 0)),
        out_shape=jax.ShapeDtypeStruct((e, d // 2), I32),
        compiler_params=_cparams(("arbitrary",)),
        name="pack_table",
    )(t)


def _unpack_pair(w):
    lo = lax.bitcast_convert_type(jnp.left_shift(w, 16), F32)
    hi = lax.bitcast_convert_type(w & jnp.int32(-65536), F32)
    return lo, hi


def _peer_sc_body(hn_hbm, e_hbm, g_hbm, u_hbm, v_hbm, out_hbm,
                  idx_v, gate_v, x_v, out_v, rows, p_v, act_v, sem, *, tpw, d, n_sel):
    nl = SC_LANES_V7X
    tb = PEER_TOK_BATCH
    rc = PEER_ROW_CHUNK
    n_chunk = n_sel // rc
    jobs_per_tok = 2 * n_chunk
    half = d // 2
    n_lane_blk = half // nl
    xs = half // LANES
    os_ = d // LANES
    wid =lax.axis_index("s") * SC_CORES_V7X + lax.axis_index("c")
    base = wid * tpw
    lane = lax.iota(I32, nl)
    zero = jnp.zeros((nl,), F32)
    c_gelu = 2.0 * math.sqrt(2.0 / math.pi)

    def gather_copy(tab_hbm, job):
        tok = job // jobs_per_tok
        c = (job % jobs_per_tok) % n_chunk
        b = job % PEER_RING
        return pltpu.make_async_copy(tab_hbm.at[idx_v.at[tok, pl.ds(c * rc, rc)]], rows.at[b], sem.at[b])

    def start(job):
        j = job % jobs_per_tok

        @pl.when(j < n_chunk)
        def _():
            gather_copy(u_hbm, job).start()

        @pl.when(j >= n_chunk)
        def _():
            gather_copy(v_hbm, job).start()

    def compute_u(tok, c, b):
        def rg_body(rg, _):
            r0 = rg * 8

            def jbody(j2, accs):
                off0 = j2 * (2 * nl)
                off1 = off0 + nl
                xrow = tok * xs + off0 // LANES
                xl = off0 % LANES
                x0 = plsc.bitcast(x_v[xrow, pl.ds(xl, nl)], BF16)
                x1 = plsc.bitcast(x_v[xrow, pl.ds(xl + nl, nl)], BF16)
                new = []
                for r in range(8):
                    w0 = plsc.bitcast(rows[b, r0 + r, pl.ds(off0, nl)], BF16)
                    w1 = plsc.bitcast(rows[b, r0 + r, pl.ds(off1, nl)], BF16)
                    lo, hi = _unpack_pair(plsc.bitcast(w0 * x0 + w1 * x1, I32))
                    new.append(accs[r] + (lo + hi))
                return tuple(new)

            accs = lax.fori_loop(0, n_lane_blk // 2, jbody, (zero,) * 8)
            for r in range(8):
                p_v[c * rc + r0 + r, :] = accs[r]
            return 0

        lax.fori_loop(0, rc // 8, rg_body, 0)

    def finish_act(tok):
        def eg_body(eg, _):
            e0 = eg * nl
            ridx = e0 + lane
            s = zero
            for l in range(nl):
                s = s + plsc.load_gather(p_v, [ridx, jnp.full((nl,), l, I32)])
            inner = c_gelu * (s + 0.044715 * (s * s * s))
            gl = s / (1.0 + jnp.exp(-inner))
            a = gl * gate_v[tok, pl.ds(e0, nl)]
            bits = lax.bitcast_convert_type(a, I32)
            rnd = bits + jnp.int32(0x7FFF) + (lax.shift_right_logical(bits, 16) & 1)
            hi16 = rnd & jnp.int32(-65536)
            act_v[pl.ds(e0, nl)] = hi16 | lax.shift_right_logical(hi16, 16)
            return 0

        lax.fori_loop(0, n_sel // nl, eg_body, 0)

        def zbody(j, _):
            off = j * nl
            out_v[tok * os_ + off // LANES, pl.ds(off % LANES, nl)] = zero
            return 0

        lax.fori_loop(0, d // nl, zbody, 0, unroll=4)

    def compute_v(tok, c, b):
        def rg_body(rg, _):
            r0 = rg * nl
            splat = [plsc.bitcast(plsc.load_gather(act_v, [jnp.full((nl,), 0, I32) + (c * rc + r0 + r)]), BF16)
                     for r in range(nl)]

            def tree(parts):
                while len(parts) > 1:
                    parts = [parts[i] + parts[i + 1] for i in range(0, len(parts), 2)]
                return parts[0]

            @plsc.parallel_loop(0, n_lane_blk, unroll=2)
            def _(j):
                off = j * nl
                los, his = [], []
                for r in range(0, nl, 2):
                    w0 = plsc.bitcast(rows[b, r0 + r, pl.ds(off, nl)], BF16)
                    w1 = plsc.bitcast(rows[b, r0 + r + 1, pl.ds(off, nl)], BF16)
                    lo, hi = _unpack_pair(plsc.bitcast(w0 * splat[r] + w1 * splat[r + 1], I32))
                    los.append(lo)
                    his.append(hi)
                orow = tok * os_ + off // LANES
                ol = off % LANES
                out_v[orow, pl.ds(ol, nl)] = out_v[orow, pl.ds(ol, nl)] + tree(los)
                out_v[orow + xs, pl.ds(ol, nl)] = out_v[orow + xs, pl.ds(ol, nl)] + tree(his)

            return 0

        lax.fori_loop(0, rc // nl, rg_body, 0)

    def batch_body(bi, _):
        t0 = base + bi * tb
        pltpu.sync_copy(e_hbm.at[pl.ds(t0, tb)], idx_v)
        pltpu.sync_copy(g_hbm.at[pl.ds(t0, tb)], gate_v)
        pltpu.sync_copy(hn_hbm.at[pl.ds(t0 * xs, tb * xs)], x_v)
        for pre in range(PEER_RING - 1):
            start(pre)

        def job_body(job, _):
            @pl.when(job + (PEER_RING - 1) < tb * jobs_per_tok)
            def _():
                start(job + (PEER_RING - 1))

            j = job % jobs_per_tok
            gather_copy(u_hbm, job).wait()
            tok = job // jobs_per_tok
            b = job % PEER_RING

            @pl.when(j < n_chunk)
            def _():
                compute_u(tok, j, b)

            @pl.when(j == n_chunk - 1)
            def _():
                finish_act(tok)

            @pl.when(j >= n_chunk)
            def _():
                compute_v(tok, j - n_chunk, b)

            return 0

        lax.fori_loop(0, tb * jobs_per_tok, job_body, 0)
        pltpu.sync_copy(out_v, out_hbm.at[pl.ds(t0 * os_, tb * os_)])
        return 0

    lax.fori_loop(0, tpw // tb, batch_body, 0)


def _peer_sc(x_pk, experts, gates, u_tab, v_tab):
    d = 2 * u_tab.shape[1]
    n = x_pk.shape[0] * LANES // (d // 2)
    n_sel = experts.shape[1]
    nw = SC_CORES_V7X * SC_SUBCORES_V7X
    tpw = n // nw
    mesh = plsc.VectorSubcoreMesh(core_axis_name="c", subcore_axis_name="s",
                                  num_cores=SC_CORES_V7X, num_subcores=SC_SUBCORES_V7X)
    body = functools.partial(_peer_sc_body, tpw=tpw, d=d, n_sel=n_sel)
    call = pl.kernel(
        body,
        out_type=jax.ShapeDtypeStruct((n * d // LANES, LANES), F32),
        mesh=mesh,
        scratch_types=[pltpu.VMEM((PEER_TOK_BATCH, n_sel), I32),
                       pltpu.VMEM((PEER_TOK_BATCH, n_sel), F32),
                       pltpu.VMEM((PEER_TOK_BATCH * d // 2 // LANES, LANES), I32),
                       pltpu.VMEM((PEER_TOK_BATCH * d // LANES, LANES), F32),
                       pltpu.VMEM((PEER_RING, PEER_ROW_CHUNK, d // 2), I32),
                       pltpu.VMEM((n_sel, SC_LANES_V7X), F32),
                       pltpu.VMEM((n_sel,), I32),
                       pltpu.SemaphoreType.DMA((PEER_RING,))],
        compiler_params=pltpu.CompilerParams(needs_layout_passes=False, use_tc_tiling_on_sc=False),
        name="peer_sc",
    )
    return call(x_pk, experts, gates, u_tab, v_tab)


def kernel(x, norm1_g, w_in, a_re, a_im, log_dt, b_re, b_im, c_re, c_im, d_skip, w_glu, w_ssm_up, w_attn_up,
           w_out, norm2_g, peer_wq, peer_k1, peer_k2, peer_u, peer_v, norm_f_g):
    bsz, seq, d = x.shape
    depth = norm1_g.shape[0]
    chunks = _time_chunks(seq)
    h = x
    for layer in range(depth):
        last = layer + 1 == depth
        kv_w, main_w = _in_weights(w_in[layer], seq)
        s5p = _s5_params(a_re[layer], a_im[layer], log_dt[layer], b_re[layer], b_im[layer], c_re[layer],
                         c_im[layer], d_skip[layer], w_glu[layer], nb=bsz)
        n_state = s5p[4].shape[1]
        wsu = w_ssm_up[layer].astype(BF16)
        wau = w_attn_up[layer].astype(BF16)
        wo = w_out[layer].astype(BF16)
        wq = peer_wq[layer].astype(BF16)
        k1 = peer_k1[layer].astype(BF16)
        k2 = peer_k2[layer].astype(BF16)
        u_pk = _pack_bf16_pairs(peer_u[layer])
        v_pk = _pack_bf16_pairs(peer_v[layer])
        kv_tm = min(ROW_TILE, seq)
        head_len = 2 * DSA_KT
        split_kv = len(chunks) > 1 and chunks[0][1] <= head_len < seq
        k4, vt, ki = _kvproj(h, norm1_g[layer], kv_w, norm1_g[layer], seq=head_len if split_kv else seq,
                             tm=kv_tm, kt=DSA_KT)
        st_re = jnp.zeros((bsz, n_state), F32)
        st_im = jnp.zeros((bsz, n_state), F32)
        outs = []
        routed = ki
        peer_outs = []
        for c, (s0, sc) in enumerate(chunks):
            tm = math.gcd(math.gcd(s0, sc), ROW_TILE)
            if split_kv and s0 + sc > head_len:
                k4, vt, ki = _kvproj(h, norm1_g[layer], kv_w, routed, seq=seq, tm=kv_tm, kt=DSA_KT)
                split_kv = False
            after = (routed, peer_outs[c - SC_LAG] if c >= SC_LAG else ki)
            u, q, qi, wit, gs, ga = _inproj(h, norm1_g[layer], main_w, after, s0=s0, sc=sc, tm=tm)
            d_ssm = u.shape[-1]
            u_tb = u.transpose(1, 0, 2).reshape(sc * bsz, d_ssm)
            y_tb, st_re, st_im = _s5(u_tb, st_re, st_im, s5p, nb=bsz, tc=S5_STEPS)
            ys = y_tb.reshape(sc, bsz, d_ssm).transpose(1, 0, 2)
            ya = _dsa(q, qi, wit, ki, k4, vt, s0=s0, seq_total=seq, tq=DSA_TQ, kt=DSA_KT)
            hm, x_pk, qp = _merge(h, ys, ya, gs, ga, wsu, wau, wo, norm2_g[layer], wq, s0=s0, tm=tm)
            nt = bsz * sc
            e_t, g_t = _route(qp.reshape(nt, -1), k1, k2, tt=ROUTE_TOKENS)
            routed = e_t
            po = _peer_sc(x_pk.reshape(-1, LANES), e_t.T, g_t.T, u_pk, v_pk)
            peer_outs.append(po)
            hm2 = hm.reshape(nt, d)
            o = _final(hm2, po, norm_f_g, tm=tm) if last else hm2 + po.reshape(nt, d)
            outs.append(o.reshape(bsz, sc, d))
        h = jnp.concatenate(outs, axis=1)
    return h
```

```python
import functools
import math

import numpy as np
import jax
import jax.numpy as jnp
from jax import lax
from jax.experimental import pallas as pl
from jax.experimental.pallas import tpu as pltpu
from jax.experimental.pallas import tpu_sc as plsc

F32 = jnp.float32
BF16 = jnp.bfloat16
I32 = jnp.int32

SSM_GROUP = 16
SSM_STATE = 64
ATTN_HEADS = 8
ATTN_KV_HEADS = 2
HEAD_DIM = 64
IDX_HEADS = 8
IDX_DIM = 32
TOPK_MAX = 256
ROPE_THETA = 10000.0
NEG_BIG = -1e30
PEER_HEADS = 8
PEER_KEYS = 128
PEER_KEY_DIM = 128
PEER_TOPK = 16
NORM_EPS = 1e-6

TIME_SPLIT_32NDS = (1, 3, 4, 4, 4, 4, 4, 4, 2, 1, 1)
SC_LAG = 3
ROW_TILE = 512
S5_STEPS = 64
ROUTE_TOKENS = 256
DSA_TQ = 128
DSA_KT = 256
LANES = 128
INT_MIN = -(2 ** 31)
VMEM_LIMIT = 56 * 1024 * 1024


def _time_chunks(seq):
    unit = seq // 32
    if seq % 32 == 0 and unit % DSA_TQ == 0:
        sizes = [f * unit for f in TIME_SPLIT_32NDS]
    else:
        step = min(ROW_TILE, seq)
        sizes = [step] * (seq // step)
    assert sum(sizes) == seq
    starts = np.cumsum([0] + sizes[:-1]).tolist()
    return list(zip(starts, sizes))


def _cparams(sem):
    return pltpu.CompilerParams(dimension_semantics=sem, vmem_limit_bytes=VMEM_LIMIT)


def _gelu_tanh(x):
    return 0.5 * x * (1.0 + jnp.tanh(math.sqrt(2.0 / math.pi) * (x + 0.044715 * (x * x * x))))


def _sigmoid(x):
    return 1.0 / (1.0 + jnp.exp(-x))


def _rms(x, g):
    return x * lax.rsqrt(jnp.mean(x * x, axis=-1, keepdims=True) + NORM_EPS) * g


def _rot_cols(w, hd):
    d, n = w.shape
    w3 = w.reshape(d, n // hd, hd)
    half = hd // 2
    return jnp.concatenate([-w3[..., half:], w3[..., :half]], axis=-1).reshape(d, n)


def _rope_full(seq, hd, heads):
    pos = jnp.arange(seq, dtype=F32)
    inv = ROPE_THETA ** (-jnp.arange(0, hd, 2, dtype=F32) / hd)
    ang = pos[:, None] * inv[None, :]
    c = jnp.concatenate([jnp.cos(ang), jnp.cos(ang)], axis=-1)
    s = jnp.concatenate([jnp.sin(ang), jnp.sin(ang)], axis=-1)
    return jnp.tile(c, (1, heads)), jnp.tile(s, (1, heads))


def _in_weights(w_in, seq):
    d = w_in.shape[0]
    d_ssm = d // 2
    d_q = ATTN_HEADS * HEAD_DIM
    d_kv = ATTN_KV_HEADS * HEAD_DIM
    d_qi = IDX_HEADS * IDX_DIM
    splits = (d_ssm, d_q, d_kv, d_kv, d_qi, IDX_DIM, IDX_HEADS, d, d)
    offs = np.cumsum(splits)[:-1].tolist()
    wu, wq, wk, wv, wqi, wki, wwi, wgs, wga = jnp.split(w_in, offs, axis=1)
    pad = jnp.zeros((d, 128 - IDX_DIM), F32)
    cq, sq = _rope_full(seq, HEAD_DIM, ATTN_HEADS)
    ck, sk = _rope_full(seq, HEAD_DIM, ATTN_KV_HEADS)
    cqi, sqi = _rope_full(seq, IDX_DIM, IDX_HEADS)
    cki, ski = _rope_full(seq, IDX_DIM, 1)
    tpad = jnp.zeros((seq, 128 - IDX_DIM), F32)
    kv = dict(
        w=jnp.concatenate([wk, wki, pad], axis=1).astype(BF16),
        wvt=wv.T.astype(BF16),
        wr=jnp.concatenate([_rot_cols(wk, HEAD_DIM), _rot_cols(wki, IDX_DIM), pad], axis=1).astype(BF16),
        cs=jnp.concatenate([ck, cki, tpad], axis=1), sn=jnp.concatenate([sk, ski, tpad], axis=1))
    main = dict(
        w=jnp.concatenate([wu, wq, wqi, wgs, wga], axis=1).astype(BF16),
        wr=jnp.concatenate([_rot_cols(wq, HEAD_DIM), _rot_cols(wqi, IDX_DIM)], axis=1).astype(BF16),
        cs=jnp.concatenate([cq, cqi], axis=1), sn=jnp.concatenate([sq, sqi], axis=1),
        wwit=wwi.T.astype(BF16))
    return kv, main


def _kvproj_kernel(x_ref, g_ref, w_ref, wr_ref, wvt_ref, cs_ref, sn_ref, after_ref, k_ref, vt_ref, ki_ref,
                   *, d_kv, kt):
    del after_ref
    xb = _rms(x_ref[0], g_ref[...]).astype(BF16)

    def mm(ref, lo, n):
        return jnp.dot(xb, ref[:, lo:lo + n], preferred_element_type=F32)

    k = mm(w_ref, 0, d_kv) * cs_ref[:, :d_kv] + mm(wr_ref, 0, d_kv) * sn_ref[:, :d_kv]
    for n in range(ATTN_KV_HEADS):
        k_ref[0, n] = k[:, n * HEAD_DIM:(n + 1) * HEAD_DIM].astype(BF16)
    vt = lax.dot_general(wvt_ref[...], xb, (((1,), (1,)), ((), ())), preferred_element_type=F32)
    for n in range(ATTN_KV_HEADS):
        for j in range(vt.shape[1] // kt):
            vt_ref[0, n, j] = vt[n * HEAD_DIM:(n + 1) * HEAD_DIM, j * kt:(j + 1) * kt].astype(BF16)
    kiw = (mm(w_ref, d_kv, 128) * cs_ref[:, d_kv:d_kv + 128]
           + mm(wr_ref, d_kv, 128) * sn_ref[:, d_kv:d_kv + 128])
    ki_ref[0] = kiw[:, :IDX_DIM].astype(BF16)


def _kvproj(x, norm_g, kv, after, seq, tm, kt):
    bsz, _, d = x.shape
    d_kv = ATTN_KV_HEADS * HEAD_DIM
    full = lambda a: pl.BlockSpec(a.shape, lambda s, b: (0,) * a.ndim)
    g = norm_g.reshape(1, d)
    ncs = kv["cs"].shape[1]
    return pl.pallas_call(
        functools.partial(_kvproj_kernel, d_kv=d_kv, kt=kt),
        grid=(seq // tm, bsz),
        in_specs=[pl.BlockSpec((1, tm, d), lambda s, b: (b, s, 0)), full(g), full(kv["w"]), full(kv["wr"]),
                  full(kv["wvt"]),
                  pl.BlockSpec((tm, ncs), lambda s, b: (s, 0)), pl.BlockSpec((tm, ncs), lambda s, b: (s, 0)),
                  pl.BlockSpec(memory_space=pl.ANY)],
        out_specs=[pl.BlockSpec((1, ATTN_KV_HEADS, tm, HEAD_DIM), lambda s, b: (b, 0, s, 0)),
                   pl.BlockSpec((1, ATTN_KV_HEADS, tm // kt, HEAD_DIM, kt), lambda s, b: (b, 0, s, 0, 0)),
                   pl.BlockSpec((1, tm, IDX_DIM), lambda s, b: (b, s, 0))],
        out_shape=[jax.ShapeDtypeStruct((bsz, ATTN_KV_HEADS, seq, HEAD_DIM), BF16),
                   jax.ShapeDtypeStruct((bsz, ATTN_KV_HEADS, seq // kt, HEAD_DIM, kt), BF16),
                   jax.ShapeDtypeStruct((bsz, seq, IDX_DIM), BF16)],
        compiler_params=_cparams(("arbitrary", "arbitrary")),
        name="kvproj",
    )(x, g, kv["w"], kv["wr"], kv["wvt"], kv["cs"], kv["sn"], after)


def _inproj_kernel(x_ref, g_ref, w_ref, wr_ref, cs_ref, sn_ref, wwit_ref, after_tc_ref, after_sc_ref,
                   u_ref, q_ref, qi_ref, wit_ref, gs_ref, ga_ref, *, d_ssm, d_q, d_qi, d_model, q_scale, wi_scale):
    del after_tc_ref, after_sc_ref
    xb = _rms(x_ref[0], g_ref[...]).astype(BF16)

    def mm(ref, lo, n):
        return jnp.dot(xb, ref[:, lo:lo + n], preferred_element_type=F32)

    o = 0
    u_ref[0] = mm(w_ref, o, d_ssm).astype(BF16)
    o += d_ssm
    q = mm(w_ref, o, d_q) * cs_ref[:, :d_q] + mm(wr_ref, 0, d_q) * sn_ref[:, :d_q]
    q_ref[0] = (q * q_scale).astype(BF16)
    o += d_q
    qi = mm(w_ref, o, d_qi) * cs_ref[:, d_q:d_q + d_qi] + mm(wr_ref, d_q, d_qi) * sn_ref[:, d_q:d_q + d_qi]
    qi_ref[0] = qi.astype(BF16)
    o += d_qi
    gs_ref[0] = _sigmoid(mm(w_ref, o, d_model)).astype(BF16)
    o += d_model
    ga_ref[0] = _sigmoid(mm(w_ref, o, d_model)).astype(BF16)
    wit_ref[0] = lax.dot_general(wwit_ref[...], xb, (((1,), (1,)), ((), ())),
                                 preferred_element_type=F32) * wi_scale


def _inproj(x, norm_g, main, after, s0, sc, tm):
    bsz, _, d = x.shape
    d_ssm = d // 2
    d_q = ATTN_HEADS * HEAD_DIM
    d_qi = IDX_HEADS * IDX_DIM
    i0 = s0 // tm
    kern = functools.partial(
        _inproj_kernel, d_ssm=d_ssm, d_q=d_q, d_qi=d_qi, d_model=d,
        q_scale=HEAD_DIM ** -0.5 * math.log2(math.e), wi_scale=(IDX_HEADS ** -0.5) * (IDX_DIM ** -0.5))
    tok = lambda n: pl.BlockSpec((1, tm, n), lambda s, b: (b, s, 0))
    full = lambda a: pl.BlockSpec(a.shape, lambda s, b: (0,) * a.ndim)
    g = norm_g.reshape(1, d)
    ncs = main["cs"].shape[1]
    outs = [(d_ssm, BF16), (d_q, BF16), (d_qi, BF16)]
    return pl.pallas_call(
        kern,
        grid=(sc // tm, bsz),
        in_specs=[pl.BlockSpec((1, tm, d), lambda s, b: (b, s + i0, 0)), full(g), full(main["w"]), full(main["wr"]),
                  pl.BlockSpec((tm, ncs), lambda s, b: (s + i0, 0)),
                  pl.BlockSpec((tm, ncs), lambda s, b: (s + i0, 0)), full(main["wwit"]),
                  pl.BlockSpec(memory_space=pl.ANY), pl.BlockSpec(memory_space=pl.ANY)],
        out_specs=[tok(n) for n, _ in outs] + [pl.BlockSpec((1, IDX_HEADS, tm), lambda s, b: (b, 0, s)),
                                                tok(d), tok(d)],
        out_shape=[jax.ShapeDtypeStruct((bsz, sc, n), dt) for n, dt in outs]
        + [jax.ShapeDtypeStruct((bsz, IDX_HEADS, sc), F32),
           jax.ShapeDtypeStruct((bsz, sc, d), BF16), jax.ShapeDtypeStruct((bsz, sc, d), BF16)],
        compiler_params=_cparams(("arbitrary", "arbitrary")),
        name="inproj",
    )(x, g, main["w"], main["wr"], main["cs"], main["sn"], main["wwit"], *after)


def _s5_kernel(u_ref, sre_in, sim_in, bre_ref, bim_ref, cre_ref, cim_ref, are_ref, aim_ref, dsk_ref, wglu_ref,
               y_ref, st_re, st_im, sre, sim, *, tc, nb, lane_chunk):
    @pl.when(pl.program_id(0) == 0)
    def _():
        st_re[...] = sre_in[...]
        st_im[...] = sim_in[...]

    u = u_ref[...]
    n_half = bre_ref.shape[0]
    hin = bre_ref.shape[1]
    hst = bre_ref.shape[2]
    for h in range(n_half):
        uh = u[:, h * hin:(h + 1) * hin]
        sre[:, h * hst:(h + 1) * hst] = jnp.dot(uh, bre_ref[h], preferred_element_type=F32)
        sim[:, h * hst:(h + 1) * hst] = jnp.dot(uh, bim_ref[h], preferred_element_type=F32)

    n_state = sre.shape[1]
    for c in range(n_state // lane_chunk):
        cols = slice(c * lane_chunk, (c + 1) * lane_chunk)
        ar = are_ref[:, cols]
        ai = aim_ref[:, cols]

        def step(t, carry, cols=cols, ar=ar, ai=ai):
            sr, si = carry
            r0 = pl.multiple_of(t * nb, nb)
            nr = ar * sr - ai * si + sre[pl.ds(r0, nb), cols]
            ni = ar * si + ai * sr + sim[pl.ds(r0, nb), cols]
            sre[pl.ds(r0, nb), cols] = nr
            sim[pl.ds(r0, nb), cols] = ni
            return nr, ni

        sr, si = lax.fori_loop(0, tc, step, (st_re[:, cols], st_im[:, cols]), unroll=4)
        st_re[:, cols] = sr
        st_im[:, cols] = si

    ys = []
    for h in range(n_half):
        srh = sre[:, h * hst:(h + 1) * hst].astype(BF16)
        sih = sim[:, h * hst:(h + 1) * hst].astype(BF16)
        ys.append(jnp.dot(srh, cre_ref[h], preferred_element_type=F32)
                  - jnp.dot(sih, cim_ref[h], preferred_element_type=F32))
    y = jnp.concatenate(ys, axis=-1) + dsk_ref[...] * u.astype(F32)
    y = _gelu_tanh(y)
    gate = jnp.dot(y.astype(BF16), wglu_ref[...], preferred_element_type=F32)
    y_ref[...] = (y * _sigmoid(gate)).astype(BF16)


def _s5_params(a_re, a_im, log_dt, b_re, b_im, c_re, c_im, d_skip, w_glu, nb):
    groups = a_re.shape[0]
    d_ssm = groups * SSM_GROUP
    n_state = groups * SSM_STATE
    lam = lax.complex(a_re, a_im)
    dt = jnp.exp(log_dt)[:, None]
    a_bar = jnp.exp(lam * dt)
    b_bar = ((a_bar - 1.0) / lam)[..., None] * lax.complex(b_re, b_im)
    gh = min(groups, 256 // SSM_GROUP)
    n_half = groups // gh
    eye = jnp.eye(gh, dtype=F32)

    def bmat(bb):
        b4 = bb.reshape(n_half, gh, SSM_STATE, SSM_GROUP)
        return jnp.einsum('hgpc,gk->hgckp', b4, eye).reshape(n_half, gh * SSM_GROUP, gh * SSM_STATE)

    def cmat(cc):
        c4 = cc.reshape(n_half, gh, SSM_GROUP, SSM_STATE)
        return jnp.einsum('hgcp,gk->hgpkc', c4, eye).reshape(n_half, gh * SSM_STATE, gh * SSM_GROUP)

    return (bmat(jnp.real(b_bar)).astype(BF16), bmat(jnp.imag(b_bar)).astype(BF16),
            cmat(c_re).astype(BF16), cmat(c_im).astype(BF16),
            jnp.broadcast_to(jnp.real(a_bar).reshape(1, n_state), (nb, n_state)),
            jnp.broadcast_to(jnp.imag(a_bar).reshape(1, n_state), (nb, n_state)),
            d_skip.reshape(1, d_ssm), w_glu.astype(BF16))


def _s5(u_tb, st_re, st_im, params, nb, tc):
    rows, d_ssm = u_tb.shape
    n_state = st_re.shape[1]
    blk = tc * nb
    full = lambda a: pl.BlockSpec(a.shape, lambda i: (0,) * a.ndim)
    st_spec = pl.BlockSpec((nb, n_state), lambda i: (0, 0))
    kern = functools.partial(_s5_kernel, tc=tc, nb=nb, lane_chunk=512)
    return pl.pallas_call(
        kern,
        grid=(rows // blk,),
        in_specs=[pl.BlockSpec((blk, d_ssm), lambda i: (i, 0)), st_spec, st_spec] + [full(p) for p in params],
        out_specs=[pl.BlockSpec((blk, d_ssm), lambda i: (i, 0)), st_spec, st_spec],
        out_shape=[jax.ShapeDtypeStruct((rows, d_ssm), BF16),
                   jax.ShapeDtypeStruct((nb, n_state), F32), jax.ShapeDtypeStruct((nb, n_state), F32)],
        scratch_shapes=[pltpu.VMEM((blk, n_state), F32), pltpu.VMEM((blk, n_state), F32)],
        compiler_params=_cparams(("arbitrary",)),
        name="s5",
    )(u_tb, st_re, st_im, *params)


PART_ROWS = 32


def _dsa_kernel(qi_ref, wit_ref, q_ref, ki_ref, k_ref, vt_ref, o_ref, key_s, bias_s, lg_s,
                *, qb0, tq, kt, sub, topk, seq_bits):
    qb = pl.program_id(1) + qb0
    nkt = ((qb * tq + tq + sub * kt - 1) // (sub * kt)) * sub
    q_pos = qb * tq + lax.broadcasted_iota(I32, (1, tq), 1)
    k_eff = jnp.minimum(topk, q_pos + 1).astype(F32)

    qi = qi_ref[0]
    wit = wit_ref[0]
    qipair = [jnp.concatenate([qi[:, (2 * hp) * IDX_DIM:(2 * hp + 1) * IDX_DIM],
                               qi[:, (2 * hp + 1) * IDX_DIM:(2 * hp + 2) * IDX_DIM]], axis=0)
              for hp in range(IDX_HEADS // 2)]

    def key_pos(t):
        return t * kt + lax.broadcasted_iota(I32, (kt, tq), 0)

    def score_tile(t2, _):
        for hf in range(sub):
            r0 = pl.multiple_of((t2 * sub + hf) * kt, kt)
            ki_t = ki_ref[0, pl.ds(r0, kt), :]
            for hp in range(IDX_HEADS // 2):
                lg_s[hf, hp] = lax.dot_general(ki_t, qipair[hp], (((1,), (1,)), ((), ())),
                                               preferred_element_type=F32)
        for hf in range(sub):
            t = t2 * sub + hf
            r0 = pl.multiple_of(t * kt, kt)
            sc = jnp.zeros((kt, tq), F32)
            for hp in range(IDX_HEADS // 2):
                rel = lg_s[hf, hp]
                sc = sc + jnp.maximum(rel[:, :tq], 0.0) * wit[2 * hp:2 * hp + 1, :]
                sc = sc + jnp.maximum(rel[:, tq:], 0.0) * wit[2 * hp + 1:2 * hp + 2, :]
            bits = lax.bitcast_convert_type(sc, I32)
            key = jnp.where(bits < 0, bits ^ jnp.int32(0x7FFFFFFF), bits)
            key = jnp.where(key_pos(t) <= q_pos, key, jnp.int32(INT_MIN))
            key_s[pl.ds(r0, kt), :] = key
        return 0

    lax.fori_loop(0, nkt // sub, score_tile, 0)

    def count(pred_fn):
        def body(t2, acc):
            for hf in range(sub):
                t = t2 * sub + hf
                r0 = pl.multiple_of(t * kt, kt)
                m = pred_fn(key_s[pl.ds(r0, kt), :], t)
                ones = jnp.where(m, 1.0, 0.0).reshape(kt // PART_ROWS, PART_ROWS, tq)
                acc = acc + jnp.sum(ones, axis=0)
            return acc
        acc = lax.fori_loop(0, nkt // sub, body, jnp.zeros((PART_ROWS, tq), F32))
        return jnp.sum(acc, axis=0, keepdims=True)

    def bit_step(i, carry):
        u, cnt_u = carry
        bit = jnp.left_shift(jnp.int32(1), 31 - i)
        cand_u = u | bit
        cand_s = cand_u ^ jnp.int32(INT_MIN)
        cnt = count(lambda kk, t: kk >= cand_s)
        ok = cnt >= k_eff
        return jnp.where(ok, cand_u, u), jnp.where(ok, cnt, cnt_u)

    all_keys = (nkt * kt).astype(F32)
    u_thr, cnt_ge = lax.fori_loop(0, 32, bit_step, (jnp.zeros((1, tq), I32), jnp.zeros((1, tq), F32) + all_keys))
    thr = u_thr ^ jnp.int32(INT_MIN)
    has_tie = jnp.max(cnt_ge - k_eff) > 0.0

    def tie_cut():
        need_eq = k_eff - count(lambda kk, t: kk > thr)

        def pos_step(i, c):
            bit = jnp.left_shift(jnp.int32(1), seq_bits - 1 - i)
            cand = c | bit
            cnt = count(lambda kk, t: (kk == thr) & (key_pos(t) < cand))
            return jnp.where(cnt < need_eq, cand, c)
        return lax.fori_loop(0, seq_bits, pos_step, jnp.zeros((1, tq), I32))

    cut = lax.cond(has_tie, tie_cut, lambda: jnp.full((1, tq), 2 ** seq_bits, I32))

    def bias_tile(t, _):
        r0 = pl.multiple_of(t * kt, kt)
        key = key_s[pl.ds(r0, kt), :]
        sel = (key > thr) | ((key == thr) & (key_pos(t) <= cut))
        bias_s[pl.ds(r0, kt), :] = jnp.where(sel, 0.0, NEG_BIG)
        return 0

    lax.fori_loop(0, nkt, bias_tile, 0)

    q = q_ref[0]
    grp = ATTN_HEADS // ATTN_KV_HEADS
    pairs_per_kv = grp // 2
    n_unit = ATTN_KV_HEADS * pairs_per_kv
    wq = 2 * tq
    qpair = [jnp.concatenate([q[:, (2 * u) * HEAD_DIM:(2 * u + 1) * HEAD_DIM],
                              q[:, (2 * u + 1) * HEAD_DIM:(2 * u + 2) * HEAD_DIM]], axis=0)
             for u in range(n_unit)]

    def col_reduce(x, op):
        part = op(x.reshape(kt // PART_ROWS, PART_ROWS, wq), axis=0)
        return op(part, axis=0, keepdims=True)

    def attn_tile(t, carry):
        ms, ls, accs = list(carry[0]), list(carry[1]), list(carry[2])
        for hf in range(sub):
            r0 = pl.multiple_of((t * sub + hf) * kt, kt)
            bias = bias_s[pl.ds(r0, kt), :]
            bias2 = jnp.concatenate([bias, bias], axis=1)
            for u in range(n_unit):
                k_t = k_ref[0, u // pairs_per_kv, pl.ds(r0, kt), :]
                lg_s[hf, u] = lax.dot_general(k_t, qpair[u], (((1,), (1,)), ((), ())),
                                              preferred_element_type=F32) + bias2
        for hf in range(sub):
            for u in range(n_unit):
                lg = lg_s[hf, u]
                m_new = jnp.maximum(ms[u], col_reduce(lg, jnp.max))
                p = jnp.exp2(lg - m_new)
                alpha = jnp.exp2(ms[u] - m_new)
                v_t = vt_ref[0, u // pairs_per_kv, t * sub + hf]
                ls[u] = alpha * ls[u] + col_reduce(p, jnp.sum)
                accs[u] = alpha * accs[u] + jnp.dot(v_t, p.astype(BF16), preferred_element_type=F32)
                ms[u] = m_new
        return tuple(ms), tuple(ls), tuple(accs)

    init = (tuple(jnp.full((1, wq), NEG_BIG, F32) for _ in range(n_unit)),
            tuple(jnp.zeros((1, wq), F32) for _ in range(n_unit)),
            tuple(jnp.zeros((HEAD_DIM, wq), F32) for _ in range(n_unit)))
    _, ls, accs = lax.fori_loop(0, nkt // sub, attn_tile, init)
    for n in range(ATTN_KV_HEADS):
        o_ref[0, 0, n] = jnp.concatenate([accs[n * pairs_per_kv + pg] / ls[n * pairs_per_kv + pg]
                                          for pg in range(pairs_per_kv)], axis=1).astype(BF16)


def _dsa(q, qi, wit, ki, k4, vt, s0, seq_total, tq, kt):
    bsz, sc, _ = q.shape
    seq = ki.shape[1]
    topk = min(TOPK_MAX, seq_total // 4)
    nqb = sc // tq
    grp = ATTN_HEADS // ATTN_KV_HEADS
    seq_bits = int(math.log2(seq))
    assert 2 ** seq_bits == seq
    sub = 2 if seq % (2 * kt) == 0 else 1
    n_unit = ATTN_HEADS // 2
    assert IDX_HEADS // 2 <= n_unit
    kern =functools.partial(_dsa_kernel, qb0=s0 // tq, tq=tq, kt=kt, sub=sub, topk=topk, seq_bits=seq_bits)
    o_t = pl.pallas_call(
        kern,
        grid=(bsz, nqb),
        in_specs=[pl.BlockSpec((1, tq, IDX_HEADS * IDX_DIM), lambda b, j: (b, j, 0)),
                  pl.BlockSpec((1, IDX_HEADS, tq), lambda b, j: (b, 0, j)),
                  pl.BlockSpec((1, tq, ATTN_HEADS * HEAD_DIM), lambda b, j: (b, j, 0)),
                  pl.BlockSpec((1, seq, IDX_DIM), lambda b, j: (b, 0, 0)),
                  pl.BlockSpec((1, ATTN_KV_HEADS, seq, HEAD_DIM), lambda b, j: (b, 0, 0, 0)),
                  pl.BlockSpec((1, ATTN_KV_HEADS, seq // kt, HEAD_DIM, kt), lambda b, j: (b, 0, 0, 0, 0))],
        out_specs=pl.BlockSpec((1, 1, ATTN_KV_HEADS, HEAD_DIM, grp * tq), lambda b, j: (b, j, 0, 0, 0)),
        out_shape=jax.ShapeDtypeStruct((bsz, nqb, ATTN_KV_HEADS, HEAD_DIM, grp * tq), BF16),
        scratch_shapes=[pltpu.VMEM((seq, tq), I32), pltpu.VMEM((seq, tq), F32),
                        pltpu.VMEM((sub, n_unit, kt, 2 * tq), F32)],
        compiler_params=_cparams(("arbitrary", "arbitrary")),
        name="dsa",
    )(qi, wit, q, ki, k4, vt)
    o = o_t.reshape(bsz, nqb, ATTN_KV_HEADS, HEAD_DIM, grp, tq).transpose(0, 1, 5, 2, 4, 3)
    return o.reshape(bsz, sc, ATTN_HEADS * HEAD_DIM)


def _merge_kernel(x_ref, ys_ref, ya_ref, gs_ref, ga_ref, wsu_ref, wau_ref, wout_ref, g2_ref, wq_ref,
                  h_ref, hn_ref, qp_ref):
    ms = jnp.dot(ys_ref[0], wsu_ref[...], preferred_element_type=F32)
    ma = jnp.dot(ya_ref[0], wau_ref[...], preferred_element_type=F32)
    merged = gs_ref[0].astype(F32) * ms + ga_ref[0].astype(F32) * ma
    h = x_ref[0] + jnp.dot(merged.astype(BF16), wout_ref[...], preferred_element_type=F32)
    h_ref[0] = h
    hb = _rms(h, g2_ref[...]).astype(BF16)
    words = _pack_words(hb)
    half = words.shape[1]
    n_seg = half // LANES
    tm = words.shape[0]
    for p in range(n_seg):
        hn_ref[0, pl.ds(p, tm, stride=n_seg), :] = words[:, p * LANES:(p + 1) * LANES]
    qp_ref[0] = jnp.dot(hb, wq_ref[...], preferred_element_type=F32).astype(BF16)


def _merge(x, ys, ya, gs, ga, wsu, wau, wo, norm2_g, wq, s0, tm):
    bsz, sc, _ = ya.shape
    d = x.shape[2]
    i0 = s0 // tm
    loc = lambda a: pl.BlockSpec((1, tm, a.shape[2]), lambda b, i: (b, i, 0))
    full = lambda a: pl.BlockSpec(a.shape, lambda b, i: (0,) * a.ndim)
    g2 = norm2_g.reshape(1, d)
    nq = wq.shape[1]
    n_seg = d // 2 // LANES
    out = lambda n: pl.BlockSpec((1, tm, n), lambda b, i: (b, i, 0))
    return pl.pallas_call(
        _merge_kernel,
        grid=(bsz, sc // tm),
        in_specs=[pl.BlockSpec((1, tm, d), lambda b, i: (b, i + i0, 0)), loc(ys), loc(ya), loc(gs), loc(ga),
                  full(wsu), full(wau), full(wo), full(g2), full(wq)],
        out_specs=[out(d), pl.BlockSpec((1, tm * n_seg, LANES), lambda b, i: (b, i, 0)), out(nq)],
        out_shape=[jax.ShapeDtypeStruct((bsz, sc, d), F32), jax.ShapeDtypeStruct((bsz, sc * n_seg, LANES), I32),
                   jax.ShapeDtypeStruct((bsz, sc, nq), BF16)],
        compiler_params=_cparams(("arbitrary", "arbitrary")),
        name="merge",
    )(x, ys, ya, gs, ga, wsu, wau, wo, g2, wq)


def _cand_layout():
    blocks = []
    blocks.append((0, 16, 16))
    for i in range(1, 8):
        blocks.append((i, 8, PEER_TOPK // (i + 1)))
    blocks.append((None, 8, 8))
    return blocks


def _top_rows(s, order, payload, k):
    big = jnp.float32(3e38)
    vals, pays = [], []
    for _ in range(k):
        m = jnp.max(s, axis=0, keepdims=True)
        o = jnp.min(jnp.where(s == m, order, big), axis=0, keepdims=True)
        hit = order == o
        pays.append(o if payload is order else jnp.min(jnp.where(hit, payload, big), axis=0, keepdims=True))
        vals.append(m)
        s = jnp.where(hit, -jnp.inf, s)
    return jnp.concatenate(vals, axis=0), jnp.concatenate(pays, axis=0)


def _route_kernel(qp_ref, k1_ref, k2_ref, e_ref, g_ref, *, tt):
    qp = qp_ref[...]
    kd = PEER_KEY_DIM
    rows_k = lax.broadcasted_iota(I32, (PEER_KEYS, tt), 0).astype(F32)
    for h in range(PEER_HEADS):
        q1 = qp[:, (2 * h) * kd:(2 * h + 1) * kd]
        q2 = qp[:, (2 * h + 1) * kd:(2 * h + 2) * kd]
        s1 = lax.dot_general(k1_ref[h], q1, (((1,), (1,)), ((), ())), preferred_element_type=F32)
        s2 = lax.dot_general(k2_ref[h], q2, (((1,), (1,)), ((), ())), preferred_element_type=F32)
        v1, i1 = _top_rows(s1, rows_k, rows_k, PEER_TOPK)
        v2, i2 = _top_rows(s2, rows_k, rows_k, PEER_TOPK)
        cs, ce, co = [], [], []
        for i, rows, valid in _cand_layout():
            r = lax.broadcasted_iota(I32, (rows, tt), 0).astype(F32)
            if i is None:
                val = v1[8:16] + v2[0:1]
                eid = i1[8:16] * PEER_KEYS + i2[0:1]
                flat = (r + 8.0) * PEER_TOPK
            else:
                val = v1[i:i + 1] + v2[0:rows]
                eid = i1[i:i + 1] * PEER_KEYS + i2[0:rows]
                flat = r + float(i * PEER_TOPK)
                if valid < rows:
                    val = jnp.where(r < float(valid), val, -jnp.inf)
            cs.append(val)
            ce.append(eid)
            co.append(flat)
        cand = jnp.concatenate(cs, axis=0)
        top_s, top_e = _top_rows(cand, jnp.concatenate(co, axis=0), jnp.concatenate(ce, axis=0), PEER_TOPK)
        p = jnp.exp(top_s - top_s[0:1])
        gates = p / jnp.sum(p, axis=0, keepdims=True)
        e_ref[h * PEER_TOPK:(h + 1) * PEER_TOPK, :] = top_e.astype(I32)
        g_ref[h * PEER_TOPK:(h + 1) * PEER_TOPK, :] = gates


def _route(qp, k1, k2, tt):
    n, nq = qp.shape
    n_sel = PEER_HEADS * PEER_TOPK
    full = lambda a: pl.BlockSpec(a.shape, lambda i: (0,) * a.ndim)
    return pl.pallas_call(
        functools.partial(_route_kernel, tt=tt),
        grid=(n // tt,),
        in_specs=[pl.BlockSpec((tt, nq), lambda i: (i, 0)), full(k1), full(k2)],
        out_specs=[pl.BlockSpec((n_sel, tt), lambda i: (0, i)), pl.BlockSpec((n_sel, tt), lambda i: (0, i))],
        out_shape=[jax.ShapeDtypeStruct((n_sel, n), I32), jax.ShapeDtypeStruct((n_sel, n), F32)],
        compiler_params=_cparams(("arbitrary",)),
        name="route",
    )(qp, k1, k2)


def _final_kernel(h_ref, p_ref, g_ref, o_ref):
    tm, d = h_ref.shape
    n_seg = d // LANES
    p = jnp.concatenate([p_ref[pl.ds(k, tm, stride=n_seg), :] for k in range(n_seg)], axis=1)
    o_ref[...] = _rms(h_ref[...] + p, g_ref[...])


def _final(h, p, g, tm):
    n, d = h.shape
    row = pl.BlockSpec((tm, d), lambda i: (i, 0))
    return pl.pallas_call(
        _final_kernel,
        grid=(n // tm,),
        in_specs=[row, pl.BlockSpec((tm * (d // LANES), LANES), lambda i: (i, 0)),
                  pl.BlockSpec((1, d), lambda i: (0, 0))],
        out_specs=row,
        out_shape=jax.ShapeDtypeStruct((n, d), F32),
        compiler_params=_cparams(("arbitrary",)),
        name="final",
    )(h, p, g.reshape(1, d))


SC_CORES_V7X = 2
SC_SUBCORES_V7X = 16
SC_LANES_V7X = 16
PEER_TOK_BATCH = 32
PEER_ROW_CHUNK = 32
PEER_RING = 4


def _pack_words(x):
    bits = lax.bitcast_convert_type(x.astype(BF16).astype(F32), I32)
    half = bits.shape[1] // 2
    return (bits[:, half:] & jnp.int32(-65536)) | lax.shift_right_logical(bits[:, :half], 16)


def _pack_kernel(t_ref, o_ref):
    o_ref[...] = _pack_words(t_ref[...])


def _pack_bf16_pairs(t, rows=1024):
    e, d = t.shape
    return pl.pallas_call(
        _pack_kernel,
        grid=(e // rows,),
        in_specs=[pl.BlockSpec((rows, d), lambda i: (i, 0))],
        out_specs=pl.BlockSpec((rows, d // 2), lambda i: (i, 0)),
        out_shape=jax.ShapeDtypeStruct((e, d // 2), I32),
        compiler_params=_cparams(("arbitrary",)),
        name="pack_table",
    )(t)


def _unpack_pair(w):
    lo = lax.bitcast_convert_type(jnp.left_shift(w, 16), F32)
    hi = lax.bitcast_convert_type(w & jnp.int32(-65536), F32)
    return lo, hi


def _peer_sc_body(hn_hbm, e_hbm, g_hbm, u_hbm, v_hbm, out_hbm,
                  idx_v, gate_v, x_v, out_v, rows, p_v, act_v, sem, *, tpw, d, n_sel):
    nl = SC_LANES_V7X
    tb = PEER_TOK_BATCH
    rc = PEER_ROW_CHUNK
    n_chunk = n_sel // rc
    jobs_per_tok = 2 * n_chunk
    half = d // 2
    n_lane_blk = half // nl
    xs = half // LANES
    os_ = d // LANES
    wid =lax.axis_index("s") * SC_CORES_V7X + lax.axis_index("c")
    base = wid * tpw
    lane = lax.iota(I32, nl)
    zero = jnp.zeros((nl,), F32)
    c_gelu = 2.0 * math.sqrt(2.0 / math.pi)

    def gather_copy(tab_hbm, job):
        tok = job // jobs_per_tok
        c = (job % jobs_per_tok) % n_chunk
        b = job % PEER_RING
        return pltpu.make_async_copy(tab_hbm.at[idx_v.at[tok, pl.ds(c * rc, rc)]], rows.at[b], sem.at[b])

    def start(job):
        j = job % jobs_per_tok

        @pl.when(j < n_chunk)
        def _():
            gather_copy(u_hbm, job).start()

        @pl.when(j >= n_chunk)
        def _():
            gather_copy(v_hbm, job).start()

    def compute_u(tok, c, b):
        def rg_body(rg, _):
            r0 = rg * 8

            def jbody(j2, accs):
                off0 = j2 * (2 * nl)
                off1 = off0 + nl
                xrow = tok * xs + off0 // LANES
                xl = off0 % LANES
                x0 = plsc.bitcast(x_v[xrow, pl.ds(xl, nl)], BF16)
                x1 = plsc.bitcast(x_v[xrow, pl.ds(xl + nl, nl)], BF16)
                new = []
                for r in range(8):
                    w0 = plsc.bitcast(rows[b, r0 + r, pl.ds(off0, nl)], BF16)
                    w1 = plsc.bitcast(rows[b, r0 + r, pl.ds(off1, nl)], BF16)
                    lo, hi = _unpack_pair(plsc.bitcast(w0 * x0 + w1 * x1, I32))
                    new.append(accs[r] + (lo + hi))
                return tuple(new)

            accs = lax.fori_loop(0, n_lane_blk // 2, jbody, (zero,) * 8)
            for r in range(8):
                p_v[c * rc + r0 + r, :] = accs[r]
            return 0

        lax.fori_loop(0, rc // 8, rg_body, 0)

    def finish_act(tok):
        def eg_body(eg, _):
            e0 = eg * nl
            ridx = e0 + lane
            s = zero
            for l in range(nl):
                s = s + plsc.load_gather(p_v, [ridx, jnp.full((nl,), l, I32)])
            inner = c_gelu * (s + 0.044715 * (s * s * s))
            gl = s / (1.0 + jnp.exp(-inner))
            a = gl * gate_v[tok, pl.ds(e0, nl)]
            bits = lax.bitcast_convert_type(a, I32)
            rnd = bits + jnp.int32(0x7FFF) + (lax.shift_right_logical(bits, 16) & 1)
            hi16 = rnd & jnp.int32(-65536)
            act_v[pl.ds(e0, nl)] = hi16 | lax.shift_right_logical(hi16, 16)
            return 0

        lax.fori_loop(0, n_sel // nl, eg_body, 0)

        def zbody(j, _):
            off = j * nl
            out_v[tok * os_ + off // LANES, pl.ds(off % LANES, nl)] = zero
            return 0

        lax.fori_loop(0, d // nl, zbody, 0, unroll=4)

    def compute_v(tok, c, b):
        def rg_body(rg, _):
            r0 = rg * nl
            splat = [plsc.bitcast(plsc.load_gather(act_v, [jnp.full((nl,), 0, I32) + (c * rc + r0 + r)]), BF16)
                     for r in range(nl)]

            def tree(parts):
                while len(parts) > 1:
                    parts = [parts[i] + parts[i + 1] for i in range(0, len(parts), 2)]
                return parts[0]

            @plsc.parallel_loop(0, n_lane_blk, unroll=2)
            def _(j):
                off = j * nl
                los, his = [], []
                for r in range(0, nl, 2):
                    w0 = plsc.bitcast(rows[b, r0 + r, pl.ds(off, nl)], BF16)
                    w1 = plsc.bitcast(rows[b, r0 + r + 1, pl.ds(off, nl)], BF16)
                    lo, hi = _unpack_pair(plsc.bitcast(w0 * splat[r] + w1 * splat[r + 1], I32))
                    los.append(lo)
                    his.append(hi)
                orow = tok * os_ + off // LANES
                ol = off % LANES
                out_v[orow, pl.ds(ol, nl)] = out_v[orow, pl.ds(ol, nl)] + tree(los)
                out_v[orow + xs, pl.ds(ol, nl)] = out_v[orow + xs, pl.ds(ol, nl)] + tree(his)

            return 0

        lax.fori_loop(0, rc // nl, rg_body, 0)

    def batch_body(bi, _):
        t0 = base + bi * tb
        pltpu.sync_copy(e_hbm.at[pl.ds(t0, tb)], idx_v)
        pltpu.sync_copy(g_hbm.at[pl.ds(t0, tb)], gate_v)
        pltpu.sync_copy(hn_hbm.at[pl.ds(t0 * xs, tb * xs)], x_v)
        for pre in range(PEER_RING - 1):
            start(pre)

        def job_body(job, _):
            @pl.when(job + (PEER_RING - 1) < tb * jobs_per_tok)
            def _():
                start(job + (PEER_RING - 1))

            j = job % jobs_per_tok
            gather_copy(u_hbm, job).wait()
            tok = job // jobs_per_tok
            b = job % PEER_RING

            @pl.when(j < n_chunk)
            def _():
                compute_u(tok, j, b)

            @pl.when(j == n_chunk - 1)
            def _():
                finish_act(tok)

            @pl.when(j >= n_chunk)
            def _():
                compute_v(tok, j - n_chunk, b)

            return 0

        lax.fori_loop(0, tb * jobs_per_tok, job_body, 0)
        pltpu.sync_copy(out_v, out_hbm.at[pl.ds(t0 * os_, tb * os_)])
        return 0

    lax.fori_loop(0, tpw // tb, batch_body, 0)


def _peer_sc(x_pk, experts, gates, u_tab, v_tab):
    d = 2 * u_tab.shape[1]
    n = x_pk.shape[0] * LANES // (d // 2)
    n_sel = experts.shape[1]
    nw = SC_CORES_V7X * SC_SUBCORES_V7X
    tpw = n // nw
    mesh = plsc.VectorSubcoreMesh(core_axis_name="c", subcore_axis_name="s",
                                  num_cores=SC_CORES_V7X, num_subcores=SC_SUBCORES_V7X)
    body = functools.partial(_peer_sc_body, tpw=tpw, d=d, n_sel=n_sel)
    call = pl.kernel(
        body,
        out_type=jax.ShapeDtypeStruct((n * d // LANES, LANES), F32),
        mesh=mesh,
        scratch_types=[pltpu.VMEM((PEER_TOK_BATCH, n_sel), I32),
                       pltpu.VMEM((PEER_TOK_BATCH, n_sel), F32),
                       pltpu.VMEM((PEER_TOK_BATCH * d // 2 // LANES, LANES), I32),
                       pltpu.VMEM((PEER_TOK_BATCH * d // LANES, LANES), F32),
                       pltpu.VMEM((PEER_RING, PEER_ROW_CHUNK, d // 2), I32),
                       pltpu.VMEM((n_sel, SC_LANES_V7X), F32),
                       pltpu.VMEM((n_sel,), I32),
                       pltpu.SemaphoreType.DMA((PEER_RING,))],
        compiler_params=pltpu.CompilerParams(needs_layout_passes=False, use_tc_tiling_on_sc=False),
        name="peer_sc",
    )
    return call(x_pk, experts, gates, u_tab, v_tab)


def kernel(x, norm1_g, w_in, a_re, a_im, log_dt, b_re, b_im, c_re, c_im, d_skip, w_glu, w_ssm_up, w_attn_up,
           w_out, norm2_g, peer_wq, peer_k1, peer_k2, peer_u, peer_v, norm_f_g):
    bsz, seq, d = x.shape
    depth = norm1_g.shape[0]
    chunks = _time_chunks(seq)
    h = x
    for layer in range(depth):
        last = layer + 1 == depth
        kv_w, main_w = _in_weights(w_in[layer], seq)
        s5p = _s5_params(a_re[layer], a_im[layer], log_dt[layer], b_re[layer], b_im[layer], c_re[layer],
                         c_im[layer], d_skip[layer], w_glu[layer], nb=bsz)
        n_state = s5p[4].shape[1]
        wsu = w_ssm_up[layer].astype(BF16)
        wau = w_attn_up[layer].astype(BF16)
        wo = w_out[layer].astype(BF16)
        wq = peer_wq[layer].astype(BF16)
        k1 = peer_k1[layer].astype(BF16)
        k2 = peer_k2[layer].astype(BF16)
        u_pk = _pack_bf16_pairs(peer_u[layer])
        v_pk = _pack_bf16_pairs(peer_v[layer])
        kv_tm = min(ROW_TILE, seq)
        head_len = 2 * DSA_KT
        split_kv = len(chunks) > 1 and chunks[0][1] <= head_len < seq
        k4, vt, ki = _kvproj(h, norm1_g[layer], kv_w, norm1_g[layer], seq=head_len if split_kv else seq,
                             tm=kv_tm, kt=DSA_KT)
        st_re = jnp.zeros((bsz, n_state), F32)
        st_im = jnp.zeros((bsz, n_state), F32)
        outs = []
        routed = ki
        peer_outs = []
        for c, (s0, sc) in enumerate(chunks):
            tm = math.gcd(math.gcd(s0, sc), ROW_TILE)
            if split_kv and s0 + sc > head_len:
                k4, vt, ki = _kvproj(h, norm1_g[layer], kv_w, routed, seq=seq, tm=kv_tm, kt=DSA_KT)
                split_kv = False
            after = (routed, peer_outs[c - SC_LAG] if c >= SC_LAG else ki)
            u, q, qi, wit, gs, ga = _inproj(h, norm1_g[layer], main_w, after, s0=s0, sc=sc, tm=tm)
            d_ssm = u.shape[-1]
            u_tb = u.transpose(1, 0, 2).reshape(sc * bsz, d_ssm)
            y_tb, st_re, st_im = _s5(u_tb, st_re, st_im, s5p, nb=bsz, tc=S5_STEPS)
            ys = y_tb.reshape(sc, bsz, d_ssm).transpose(1, 0, 2)
            ya = _dsa(q, qi, wit, ki, k4, vt, s0=s0, seq_total=seq, tq=DSA_TQ, kt=DSA_KT)
            hm, x_pk, qp = _merge(h, ys, ya, gs, ga, wsu, wau, wo, norm2_g[layer], wq, s0=s0, tm=tm)
            nt = bsz * sc
            e_t, g_t = _route(qp.reshape(nt, -1), k1, k2, tt=ROUTE_TOKENS)
            routed = e_t
            po = _peer_sc(x_pk.reshape(-1, LANES), e_t.T, g_t.T, u_pk, v_pk)
            peer_outs.append(po)
            hm2 = hm.reshape(nt, d)
            o = _final(hm2, po, norm_f_g, tm=tm) if last else hm2 + po.reshape(nt, d)
            outs.append(o.reshape(bsz, sc, d))
        h = jnp.concatenate(outs, axis=1)
    return h
```

```python
import functools
import math

import numpy as np
import jax
import jax.numpy as jnp
from jax import lax
from jax.experimental import pallas as pl
from jax.experimental.pallas import tpu as pltpu
from jax.experimental.pallas import tpu_sc as plsc

F32 = jnp.float32
BF16 = jnp.bfloat16
I32 = jnp.int32

SSM_GROUP = 16
SSM_STATE = 64
ATTN_HEADS = 8
ATTN_KV_HEADS = 2
HEAD_DIM = 64
IDX_HEADS = 8
IDX_DIM = 32
TOPK_MAX = 256
ROPE_THETA = 10000.0
NEG_BIG = -1e30
PEER_HEADS = 8
PEER_KEYS = 128
PEER_KEY_DIM = 128
PEER_TOPK = 16
NORM_EPS = 1e-6

TIME_SPLIT_32NDS = (1, 3, 4, 4, 4, 4, 4, 3, 3, 1, 1)
SC_LAG = 3
ROW_TILE = 512
S5_STEPS = 64
ROUTE_TOKENS = 256
DSA_TQ = 128
DSA_KT = 256
LANES = 128
INT_MIN = -(2 ** 31)
VMEM_LIMIT = 56 * 1024 * 1024


def _time_chunks(seq):
    unit = seq // 32
    if seq % 32 == 0 and unit % DSA_TQ == 0:
        sizes = [f * unit for f in TIME_SPLIT_32NDS]
    else:
        step = min(ROW_TILE, seq)
        sizes = [step] * (seq // step)
    assert sum(sizes) == seq
    starts = np.cumsum([0] + sizes[:-1]).tolist()
    return list(zip(starts, sizes))


def _cparams(sem):
    return pltpu.CompilerParams(dimension_semantics=sem, vmem_limit_bytes=VMEM_LIMIT)


def _gelu_tanh(x):
    return 0.5 * x * (1.0 + jnp.tanh(math.sqrt(2.0 / math.pi) * (x + 0.044715 * (x * x * x))))


def _sigmoid(x):
    return 1.0 / (1.0 + jnp.exp(-x))


def _rms(x, g):
    return x * lax.rsqrt(jnp.mean(x * x, axis=-1, keepdims=True) + NORM_EPS) * g


def _rot_cols(w, hd):
    d, n = w.shape
    w3 = w.reshape(d, n // hd, hd)
    half = hd // 2
    return jnp.concatenate([-w3[..., half:], w3[..., :half]], axis=-1).reshape(d, n)


def _rope_full(seq, hd, heads):
    pos = jnp.arange(seq, dtype=F32)
    inv = ROPE_THETA ** (-jnp.arange(0, hd, 2, dtype=F32) / hd)
    ang = pos[:, None] * inv[None, :]
    c = jnp.concatenate([jnp.cos(ang), jnp.cos(ang)], axis=-1)
    s = jnp.concatenate([jnp.sin(ang), jnp.sin(ang)], axis=-1)
    return jnp.tile(c, (1, heads)), jnp.tile(s, (1, heads))


def _in_weights(w_in, seq):
    d = w_in.shape[0]
    d_ssm = d // 2
    d_q = ATTN_HEADS * HEAD_DIM
    d_kv = ATTN_KV_HEADS * HEAD_DIM
    d_qi = IDX_HEADS * IDX_DIM
    splits = (d_ssm, d_q, d_kv, d_kv, d_qi, IDX_DIM, IDX_HEADS, d, d)
    offs = np.cumsum(splits)[:-1].tolist()
    wu, wq, wk, wv, wqi, wki, wwi, wgs, wga = jnp.split(w_in, offs, axis=1)
    pad = jnp.zeros((d, 128 - IDX_DIM), F32)
    cq, sq = _rope_full(seq, HEAD_DIM, ATTN_HEADS)
    ck, sk = _rope_full(seq, HEAD_DIM, ATTN_KV_HEADS)
    cqi, sqi = _rope_full(seq, IDX_DIM, IDX_HEADS)
    cki, ski = _rope_full(seq, IDX_DIM, 1)
    tpad = jnp.zeros((seq, 128 - IDX_DIM), F32)
    kv = dict(
        w=jnp.concatenate([wk, wki, pad], axis=1).astype(BF16),
        wvt=wv.T.astype(BF16),
        wr=jnp.concatenate([_rot_cols(wk, HEAD_DIM), _rot_cols(wki, IDX_DIM), pad], axis=1).astype(BF16),
        cs=jnp.concatenate([ck, cki, tpad], axis=1), sn=jnp.concatenate([sk, ski, tpad], axis=1))
    main = dict(
        w=jnp.concatenate([wu, wq, wqi, wgs, wga], axis=1).astype(BF16),
        wr=jnp.concatenate([_rot_cols(wq, HEAD_DIM), _rot_cols(wqi, IDX_DIM)], axis=1).astype(BF16),
        cs=jnp.concatenate([cq, cqi], axis=1), sn=jnp.concatenate([sq, sqi], axis=1),
        wwit=wwi.T.astype(BF16))
    return kv, main


def _kvproj_kernel(x_ref, g_ref, w_ref, wr_ref, wvt_ref, cs_ref, sn_ref, after_ref, k_ref, vt_ref, ki_ref,
                   *, d_kv, kt):
    del after_ref
    xb = _rms(x_ref[0], g_ref[...]).astype(BF16)

    def mm(ref, lo, n):
        return jnp.dot(xb, ref[:, lo:lo + n], preferred_element_type=F32)

    k = mm(w_ref, 0, d_kv) * cs_ref[:, :d_kv] + mm(wr_ref, 0, d_kv) * sn_ref[:, :d_kv]
    for n in range(ATTN_KV_HEADS):
        k_ref[0, n] = k[:, n * HEAD_DIM:(n + 1) * HEAD_DIM].astype(BF16)
    vt = lax.dot_general(wvt_ref[...], xb, (((1,), (1,)), ((), ())), preferred_element_type=F32)
    for n in range(ATTN_KV_HEADS):
        for j in range(vt.shape[1] // kt):
            vt_ref[0, n, j] = vt[n * HEAD_DIM:(n + 1) * HEAD_DIM, j * kt:(j + 1) * kt].astype(BF16)
    kiw = (mm(w_ref, d_kv, 128) * cs_ref[:, d_kv:d_kv + 128]
           + mm(wr_ref, d_kv, 128) * sn_ref[:, d_kv:d_kv + 128])
    ki_ref[0] = kiw[:, :IDX_DIM].astype(BF16)


def _kvproj(x, norm_g, kv, after, seq, tm, kt):
    bsz, _, d = x.shape
    d_kv = ATTN_KV_HEADS * HEAD_DIM
    full = lambda a: pl.BlockSpec(a.shape, lambda s, b: (0,) * a.ndim)
    g = norm_g.reshape(1, d)
    ncs = kv["cs"].shape[1]
    return pl.pallas_call(
        functools.partial(_kvproj_kernel, d_kv=d_kv, kt=kt),
        grid=(seq // tm, bsz),
        in_specs=[pl.BlockSpec((1, tm, d), lambda s, b: (b, s, 0)), full(g), full(kv["w"]), full(kv["wr"]),
                  full(kv["wvt"]),
                  pl.BlockSpec((tm, ncs), lambda s, b: (s, 0)), pl.BlockSpec((tm, ncs), lambda s, b: (s, 0)),
                  pl.BlockSpec(memory_space=pl.ANY)],
        out_specs=[pl.BlockSpec((1, ATTN_KV_HEADS, tm, HEAD_DIM), lambda s, b: (b, 0, s, 0)),
                   pl.BlockSpec((1, ATTN_KV_HEADS, tm // kt, HEAD_DIM, kt), lambda s, b: (b, 0, s, 0, 0)),
                   pl.BlockSpec((1, tm, IDX_DIM), lambda s, b: (b, s, 0))],
        out_shape=[jax.ShapeDtypeStruct((bsz, ATTN_KV_HEADS, seq, HEAD_DIM), BF16),
                   jax.ShapeDtypeStruct((bsz, ATTN_KV_HEADS, seq // kt, HEAD_DIM, kt), BF16),
                   jax.ShapeDtypeStruct((bsz, seq, IDX_DIM), BF16)],
        compiler_params=_cparams(("arbitrary", "arbitrary")),
        name="kvproj",
    )(x, g, kv["w"], kv["wr"], kv["wvt"], kv["cs"], kv["sn"], after)


def _inproj_kernel(x_ref, g_ref, w_ref, wr_ref, cs_ref, sn_ref, wwit_ref, after_tc_ref, after_sc_ref,
                   u_ref, q_ref, qi_ref, wit_ref, gs_ref, ga_ref, *, d_ssm, d_q, d_qi, d_model, q_scale, wi_scale):
    del after_tc_ref, after_sc_ref
    xb = _rms(x_ref[0], g_ref[...]).astype(BF16)

    def mm(ref, lo, n):
        return jnp.dot(xb, ref[:, lo:lo + n], preferred_element_type=F32)

    o = 0
    u_ref[0] = mm(w_ref, o, d_ssm).astype(BF16)
    o += d_ssm
    q = mm(w_ref, o, d_q) * cs_ref[:, :d_q] + mm(wr_ref, 0, d_q) * sn_ref[:, :d_q]
    q_ref[0] = (q * q_scale).astype(BF16)
    o += d_q
    qi = mm(w_ref, o, d_qi) * cs_ref[:, d_q:d_q + d_qi] + mm(wr_ref, d_q, d_qi) * sn_ref[:, d_q:d_q + d_qi]
    qi_ref[0] = qi.astype(BF16)
    o += d_qi
    gs_ref[0] = _sigmoid(mm(w_ref, o, d_model)).astype(BF16)
    o += d_model
    ga_ref[0] = _sigmoid(mm(w_ref, o, d_model)).astype(BF16)
    wit_ref[0] = lax.dot_general(wwit_ref[...], xb, (((1,), (1,)), ((), ())),
                                 preferred_element_type=F32) * wi_scale


def _inproj(x, norm_g, main, after, s0, sc, tm):
    bsz, _, d = x.shape
    d_ssm = d // 2
    d_q = ATTN_HEADS * HEAD_DIM
    d_qi = IDX_HEADS * IDX_DIM
    i0 = s0 // tm
    kern = functools.partial(
        _inproj_kernel, d_ssm=d_ssm, d_q=d_q, d_qi=d_qi, d_model=d,
        q_scale=HEAD_DIM ** -0.5 * math.log2(math.e), wi_scale=(IDX_HEADS ** -0.5) * (IDX_DIM ** -0.5))
    tok = lambda n: pl.BlockSpec((1, tm, n), lambda s, b: (b, s, 0))
    full = lambda a: pl.BlockSpec(a.shape, lambda s, b: (0,) * a.ndim)
    g = norm_g.reshape(1, d)
    ncs = main["cs"].shape[1]
    outs = [(d_ssm, BF16), (d_q, BF16), (d_qi, BF16)]
    return pl.pallas_call(
        kern,
        grid=(sc // tm, bsz),
        in_specs=[pl.BlockSpec((1, tm, d), lambda s, b: (b, s + i0, 0)), full(g), full(main["w"]), full(main["wr"]),
                  pl.BlockSpec((tm, ncs), lambda s, b: (s + i0, 0)),
                  pl.BlockSpec((tm, ncs), lambda s, b: (s + i0, 0)), full(main["wwit"]),
                  pl.BlockSpec(memory_space=pl.ANY), pl.BlockSpec(memory_space=pl.ANY)],
        out_specs=[tok(n) for n, _ in outs] + [pl.BlockSpec((1, IDX_HEADS, tm), lambda s, b: (b, 0, s)),
                                                tok(d), tok(d)],
        out_shape=[jax.ShapeDtypeStruct((bsz, sc, n), dt) for n, dt in outs]
        + [jax.ShapeDtypeStruct((bsz, IDX_HEADS, sc), F32),
           jax.ShapeDtypeStruct((bsz, sc, d), BF16), jax.ShapeDtypeStruct((bsz, sc, d), BF16)],
        compiler_params=_cparams(("arbitrary", "arbitrary")),
        name="inproj",
    )(x, g, main["w"], main["wr"], main["cs"], main["sn"], main["wwit"], *after)


def _s5_kernel(u_ref, sre_in, sim_in, bre_ref, bim_ref, cre_ref, cim_ref, are_ref, aim_ref, dsk_ref, wglu_ref,
               y_ref, st_re, st_im, sre, sim, *, tc, nb, lane_chunk):
    @pl.when(pl.program_id(0) == 0)
    def _():
        st_re[...] = sre_in[...]
        st_im[...] = sim_in[...]

    u = u_ref[...]
    n_half = bre_ref.shape[0]
    hin = bre_ref.shape[1]
    hst = bre_ref.shape[2]
    for h in range(n_half):
        uh = u[:, h * hin:(h + 1) * hin]
        sre[:, h * hst:(h + 1) * hst] = jnp.dot(uh, bre_ref[h], preferred_element_type=F32)
        sim[:, h * hst:(h + 1) * hst] = jnp.dot(uh, bim_ref[h], preferred_element_type=F32)

    n_state = sre.shape[1]
    for c in range(n_state // lane_chunk):
        cols = slice(c * lane_chunk, (c + 1) * lane_chunk)
        ar = are_ref[:, cols]
        ai = aim_ref[:, cols]

        def step(t, carry, cols=cols, ar=ar, ai=ai):
            sr, si = carry
            r0 = pl.multiple_of(t * nb, nb)
            nr = ar * sr - ai * si + sre[pl.ds(r0, nb), cols]
            ni = ar * si + ai * sr + sim[pl.ds(r0, nb), cols]
            sre[pl.ds(r0, nb), cols] = nr
            sim[pl.ds(r0, nb), cols] = ni
            return nr, ni

        sr, si = lax.fori_loop(0, tc, step, (st_re[:, cols], st_im[:, cols]), unroll=4)
        st_re[:, cols] = sr
        st_im[:, cols] = si

    ys = []
    for h in range(n_half):
        srh = sre[:, h * hst:(h + 1) * hst].astype(BF16)
        sih = sim[:, h * hst:(h + 1) * hst].astype(BF16)
        ys.append(jnp.dot(srh, cre_ref[h], preferred_element_type=F32)
                  - jnp.dot(sih, cim_ref[h], preferred_element_type=F32))
    y = jnp.concatenate(ys, axis=-1) + dsk_ref[...] * u.astype(F32)
    y = _gelu_tanh(y)
    gate = jnp.dot(y.astype(BF16), wglu_ref[...], preferred_element_type=F32)
    y_ref[...] = (y * _sigmoid(gate)).astype(BF16)


def _s5_params(a_re, a_im, log_dt, b_re, b_im, c_re, c_im, d_skip, w_glu, nb):
    groups = a_re.shape[0]
    d_ssm = groups * SSM_GROUP
    n_state = groups * SSM_STATE
    lam = lax.complex(a_re, a_im)
    dt = jnp.exp(log_dt)[:, None]
    a_bar = jnp.exp(lam * dt)
    b_bar = ((a_bar - 1.0) / lam)[..., None] * lax.complex(b_re, b_im)
    gh = min(groups, 256 // SSM_GROUP)
    n_half = groups // gh
    eye = jnp.eye(gh, dtype=F32)

    def bmat(bb):
        b4 = bb.reshape(n_half, gh, SSM_STATE, SSM_GROUP)
        return jnp.einsum('hgpc,gk->hgckp', b4, eye).reshape(n_half, gh * SSM_GROUP, gh * SSM_STATE)

    def cmat(cc):
        c4 = cc.reshape(n_half, gh, SSM_GROUP, SSM_STATE)
        return jnp.einsum('hgcp,gk->hgpkc', c4, eye).reshape(n_half, gh * SSM_STATE, gh * SSM_GROUP)

    return (bmat(jnp.real(b_bar)).astype(BF16), bmat(jnp.imag(b_bar)).astype(BF16),
            cmat(c_re).astype(BF16), cmat(c_im).astype(BF16),
            jnp.broadcast_to(jnp.real(a_bar).reshape(1, n_state), (nb, n_state)),
            jnp.broadcast_to(jnp.imag(a_bar).reshape(1, n_state), (nb, n_state)),
            d_skip.reshape(1, d_ssm), w_glu.astype(BF16))


def _s5(u_tb, st_re, st_im, params, nb, tc):
    rows, d_ssm = u_tb.shape
    n_state = st_re.shape[1]
    blk = tc * nb
    full = lambda a: pl.BlockSpec(a.shape, lambda i: (0,) * a.ndim)
    st_spec = pl.BlockSpec((nb, n_state), lambda i: (0, 0))
    kern = functools.partial(_s5_kernel, tc=tc, nb=nb, lane_chunk=512)
    return pl.pallas_call(
        kern,
        grid=(rows // blk,),
        in_specs=[pl.BlockSpec((blk, d_ssm), lambda i: (i, 0)), st_spec, st_spec] + [full(p) for p in params],
        out_specs=[pl.BlockSpec((blk, d_ssm), lambda i: (i, 0)), st_spec, st_spec],
        out_shape=[jax.ShapeDtypeStruct((rows, d_ssm), BF16),
                   jax.ShapeDtypeStruct((nb, n_state), F32), jax.ShapeDtypeStruct((nb, n_state), F32)],
        scratch_shapes=[pltpu.VMEM((blk, n_state), F32), pltpu.VMEM((blk, n_state), F32)],
        compiler_params=_cparams(("arbitrary",)),
        name="s5",
    )(u_tb, st_re, st_im, *params)


PART_ROWS = 32


def _dsa_kernel(qi_ref, wit_ref, q_ref, ki_ref, k_ref, vt_ref, o_ref, key_s, bias_s, lg_s,
                *, qb0, tq, kt, sub, topk, seq_bits):
    qb = pl.program_id(1) + qb0
    nkt = ((qb * tq + tq + sub * kt - 1) // (sub * kt)) * sub
    q_pos = qb * tq + lax.broadcasted_iota(I32, (1, tq), 1)
    k_eff = jnp.minimum(topk, q_pos + 1).astype(F32)

    qi = qi_ref[0]
    wit = wit_ref[0]
    qipair = [jnp.concatenate([qi[:, (2 * hp) * IDX_DIM:(2 * hp + 1) * IDX_DIM],
                               qi[:, (2 * hp + 1) * IDX_DIM:(2 * hp + 2) * IDX_DIM]], axis=0)
              for hp in range(IDX_HEADS // 2)]

    def key_pos(t):
        return t * kt + lax.broadcasted_iota(I32, (kt, tq), 0)

    def score_tile(t2, _):
        for hf in range(sub):
            r0 = pl.multiple_of((t2 * sub + hf) * kt, kt)
            ki_t = ki_ref[0, pl.ds(r0, kt), :]
            for hp in range(IDX_HEADS // 2):
                lg_s[hf, hp] = lax.dot_general(ki_t, qipair[hp], (((1,), (1,)), ((), ())),
                                               preferred_element_type=F32)
        for hf in range(sub):
            t = t2 * sub + hf
            r0 = pl.multiple_of(t * kt, kt)
            sc = jnp.zeros((kt, tq), F32)
            for hp in range(IDX_HEADS // 2):
                rel = lg_s[hf, hp]
                sc = sc + jnp.maximum(rel[:, :tq], 0.0) * wit[2 * hp:2 * hp + 1, :]
                sc = sc + jnp.maximum(rel[:, tq:], 0.0) * wit[2 * hp + 1:2 * hp + 2, :]
            bits = lax.bitcast_convert_type(sc, I32)
            key = jnp.where(bits < 0, bits ^ jnp.int32(0x7FFFFFFF), bits)
            key = jnp.where(key_pos(t) <= q_pos, key, jnp.int32(INT_MIN))
            key_s[pl.ds(r0, kt), :] = key
        return 0

    lax.fori_loop(0, nkt // sub, score_tile, 0)

    def count(pred_fn):
        def body(t2, acc):
            for hf in range(sub):
                t = t2 * sub + hf
                r0 = pl.multiple_of(t * kt, kt)
                m = pred_fn(key_s[pl.ds(r0, kt), :], t)
                ones = jnp.where(m, 1.0, 0.0).reshape(kt // PART_ROWS, PART_ROWS, tq)
                acc = acc + jnp.sum(ones, axis=0)
            return acc
        acc = lax.fori_loop(0, nkt // sub, body, jnp.zeros((PART_ROWS, tq), F32))
        return jnp.sum(acc, axis=0, keepdims=True)

    def bit_step(i, carry):
        u, cnt_u = carry
        bit = jnp.left_shift(jnp.int32(1), 31 - i)
        cand_u = u | bit
        cand_s = cand_u ^ jnp.int32(INT_MIN)
        cnt = count(lambda kk, t: kk >= cand_s)
        ok = cnt >= k_eff
        return jnp.where(ok, cand_u, u), jnp.where(ok, cnt, cnt_u)

    all_keys = (nkt * kt).astype(F32)
    u_thr, cnt_ge = lax.fori_loop(0, 32, bit_step, (jnp.zeros((1, tq), I32), jnp.zeros((1, tq), F32) + all_keys))
    thr = u_thr ^ jnp.int32(INT_MIN)
    has_tie = jnp.max(cnt_ge - k_eff) > 0.0

    def tie_cut():
        need_eq = k_eff - count(lambda kk, t: kk > thr)

        def pos_step(i, c):
            bit = jnp.left_shift(jnp.int32(1), seq_bits - 1 - i)
            cand = c | bit
            cnt = count(lambda kk, t: (kk == thr) & (key_pos(t) < cand))
            return jnp.where(cnt < need_eq, cand, c)
        return lax.fori_loop(0, seq_bits, pos_step, jnp.zeros((1, tq), I32))

    cut = lax.cond(has_tie, tie_cut, lambda: jnp.full((1, tq), 2 ** seq_bits, I32))

    def bias_tile(t, _):
        r0 = pl.multiple_of(t * kt, kt)
        key = key_s[pl.ds(r0, kt), :]
        sel = (key > thr) | ((key == thr) & (key_pos(t) <= cut))
        bias_s[pl.ds(r0, kt), :] = jnp.where(sel, 0.0, NEG_BIG)
        return 0

    lax.fori_loop(0, nkt, bias_tile, 0)

    q = q_ref[0]
    grp = ATTN_HEADS // ATTN_KV_HEADS
    pairs_per_kv = grp // 2
    n_unit = ATTN_KV_HEADS * pairs_per_kv
    wq = 2 * tq
    qpair = [jnp.concatenate([q[:, (2 * u) * HEAD_DIM:(2 * u + 1) * HEAD_DIM],
                              q[:, (2 * u + 1) * HEAD_DIM:(2 * u + 2) * HEAD_DIM]], axis=0)
             for u in range(n_unit)]

    def col_reduce(x, op):
        part = op(x.reshape(kt // PART_ROWS, PART_ROWS, wq), axis=0)
        return op(part, axis=0, keepdims=True)

    def attn_tile(t, carry):
        ms, ls, accs = list(carry[0]), list(carry[1]), list(carry[2])
        for hf in range(sub):
            r0 = pl.multiple_of((t * sub + hf) * kt, kt)
            bias = bias_s[pl.ds(r0, kt), :]
            bias2 = jnp.concatenate([bias, bias], axis=1)
            for u in range(n_unit):
                k_t = k_ref[0, u // pairs_per_kv, pl.ds(r0, kt), :]
                lg_s[hf, u] = lax.dot_general(k_t, qpair[u], (((1,), (1,)), ((), ())),
                                              preferred_element_type=F32) + bias2
        for hf in range(sub):
            for u in range(n_unit):
                lg = lg_s[hf, u]
                m_new = jnp.maximum(ms[u], col_reduce(lg, jnp.max))
                p = jnp.exp2(lg - m_new)
                alpha = jnp.exp2(ms[u] - m_new)
                v_t = vt_ref[0, u // pairs_per_kv, t * sub + hf]
                ls[u] = alpha * ls[u] + col_reduce(p, jnp.sum)
                accs[u] = alpha * accs[u] + jnp.dot(v_t, p.astype(BF16), preferred_element_type=F32)
                ms[u] = m_new
        return tuple(ms), tuple(ls), tuple(accs)

    init = (tuple(jnp.full((1, wq), NEG_BIG, F32) for _ in range(n_unit)),
            tuple(jnp.zeros((1, wq), F32) for _ in range(n_unit)),
            tuple(jnp.zeros((HEAD_DIM, wq), F32) for _ in range(n_unit)))
    _, ls, accs = lax.fori_loop(0, nkt // sub, attn_tile, init)
    for n in range(ATTN_KV_HEADS):
        o_ref[0, 0, n] = jnp.concatenate([accs[n * pairs_per_kv + pg] / ls[n * pairs_per_kv + pg]
                                          for pg in range(pairs_per_kv)], axis=1).astype(BF16)


def _dsa(q, qi, wit, ki, k4, vt, s0, seq_total, tq, kt):
    bsz, sc, _ = q.shape
    seq = ki.shape[1]
    topk = min(TOPK_MAX, seq_total // 4)
    nqb = sc // tq
    grp = ATTN_HEADS // ATTN_KV_HEADS
    seq_bits = int(math.log2(seq))
    assert 2 ** seq_bits == seq
    sub = 2 if seq % (2 * kt) == 0 else 1
    n_unit = ATTN_HEADS // 2
    assert IDX_HEADS // 2 <= n_unit
    kern =functools.partial(_dsa_kernel, qb0=s0 // tq, tq=tq, kt=kt, sub=sub, topk=topk, seq_bits=seq_bits)
    o_t = pl.pallas_call(
        kern,
        grid=(bsz, nqb),
        in_specs=[pl.BlockSpec((1, tq, IDX_HEADS * IDX_DIM), lambda b, j: (b, j, 0)),
                  pl.BlockSpec((1, IDX_HEADS, tq), lambda b, j: (b, 0, j)),
                  pl.BlockSpec((1, tq, ATTN_HEADS * HEAD_DIM), lambda b, j: (b, j, 0)),
                  pl.BlockSpec((1, seq, IDX_DIM), lambda b, j: (b, 0, 0)),
                  pl.BlockSpec((1, ATTN_KV_HEADS, seq, HEAD_DIM), lambda b, j: (b, 0, 0, 0)),
                  pl.BlockSpec((1, ATTN_KV_HEADS, seq // kt, HEAD_DIM, kt), lambda b, j: (b, 0, 0, 0, 0))],
        out_specs=pl.BlockSpec((1, 1, ATTN_KV_HEADS, HEAD_DIM, grp * tq), lambda b, j: (b, j, 0, 0, 0)),
        out_shape=jax.ShapeDtypeStruct((bsz, nqb, ATTN_KV_HEADS, HEAD_DIM, grp * tq), BF16),
        scratch_shapes=[pltpu.VMEM((seq, tq), I32), pltpu.VMEM((seq, tq), F32),
                        pltpu.VMEM((sub, n_unit, kt, 2 * tq), F32)],
        compiler_params=_cparams(("arbitrary", "arbitrary")),
        name="dsa",
    )(qi, wit, q, ki, k4, vt)
    o = o_t.reshape(bsz, nqb, ATTN_KV_HEADS, HEAD_DIM, grp, tq).transpose(0, 1, 5, 2, 4, 3)
    return o.reshape(bsz, sc, ATTN_HEADS * HEAD_DIM)


def _merge_kernel(x_ref, ys_ref, ya_ref, gs_ref, ga_ref, wsu_ref, wau_ref, wout_ref, g2_ref, wq_ref,
                  h_ref, hn_ref, qp_ref):
    ms = jnp.dot(ys_ref[0], wsu_ref[...], preferred_element_type=F32)
    ma = jnp.dot(ya_ref[0], wau_ref[...], preferred_element_type=F32)
    merged = gs_ref[0].astype(F32) * ms + ga_ref[0].astype(F32) * ma
    h = x_ref[0] + jnp.dot(merged.astype(BF16), wout_ref[...], preferred_element_type=F32)
    h_ref[0] = h
    hb = _rms(h, g2_ref[...]).astype(BF16)
    words = _pack_words(hb)
    half = words.shape[1]
    n_seg = half // LANES
    tm = words.shape[0]
    for p in range(n_seg):
        hn_ref[0, pl.ds(p, tm, stride=n_seg), :] = words[:, p * LANES:(p + 1) * LANES]
    qp_ref[0] = jnp.dot(hb, wq_ref[...], preferred_element_type=F32).astype(BF16)


def _merge(x, ys, ya, gs, ga, wsu, wau, wo, norm2_g, wq, s0, tm):
    bsz, sc, _ = ya.shape
    d = x.shape[2]
    i0 = s0 // tm
    loc = lambda a: pl.BlockSpec((1, tm, a.shape[2]), lambda b, i: (b, i, 0))
    full = lambda a: pl.BlockSpec(a.shape, lambda b, i: (0,) * a.ndim)
    g2 = norm2_g.reshape(1, d)
    nq = wq.shape[1]
    n_seg = d // 2 // LANES
    out = lambda n: pl.BlockSpec((1, tm, n), lambda b, i: (b, i, 0))
    return pl.pallas_call(
        _merge_kernel,
        grid=(bsz, sc // tm),
        in_specs=[pl.BlockSpec((1, tm, d), lambda b, i: (b, i + i0, 0)), loc(ys), loc(ya), loc(gs), loc(ga),
                  full(wsu), full(wau), full(wo), full(g2), full(wq)],
        out_specs=[out(d), pl.BlockSpec((1, tm * n_seg, LANES), lambda b, i: (b, i, 0)), out(nq)],
        out_shape=[jax.ShapeDtypeStruct((bsz, sc, d), F32), jax.ShapeDtypeStruct((bsz, sc * n_seg, LANES), I32),
                   jax.ShapeDtypeStruct((bsz, sc, nq), BF16)],
        compiler_params=_cparams(("arbitrary", "arbitrary")),
        name="merge",
    )(x, ys, ya, gs, ga, wsu, wau, wo, g2, wq)


def _cand_layout():
    blocks = []
    blocks.append((0, 16, 16))
    for i in range(1, 8):
        blocks.append((i, 8, PEER_TOPK // (i + 1)))
    blocks.append((None, 8, 8))
    return blocks


def _top_rows(s, order, payload, k):
    big = jnp.float32(3e38)
    vals, pays = [], []
    for _ in range(k):
        m = jnp.max(s, axis=0, keepdims=True)
        o = jnp.min(jnp.where(s == m, order, big), axis=0, keepdims=True)
        hit = order == o
        pays.append(o if payload is order else jnp.min(jnp.where(hit, payload, big), axis=0, keepdims=True))
        vals.append(m)
        s = jnp.where(hit, -jnp.inf, s)
    return jnp.concatenate(vals, axis=0), jnp.concatenate(pays, axis=0)


def _route_kernel(qp_ref, k1_ref, k2_ref, e_ref, g_ref, *, tt):
    qp = qp_ref[...]
    kd = PEER_KEY_DIM
    rows_k = lax.broadcasted_iota(I32, (PEER_KEYS, tt), 0).astype(F32)
    for h in range(PEER_HEADS):
        q1 = qp[:, (2 * h) * kd:(2 * h + 1) * kd]
        q2 = qp[:, (2 * h + 1) * kd:(2 * h + 2) * kd]
        s1 = lax.dot_general(k1_ref[h], q1, (((1,), (1,)), ((), ())), preferred_element_type=F32)
        s2 = lax.dot_general(k2_ref[h], q2, (((1,), (1,)), ((), ())), preferred_element_type=F32)
        v1, i1 = _top_rows(s1, rows_k, rows_k, PEER_TOPK)
        v2, i2 = _top_rows(s2, rows_k, rows_k, PEER_TOPK)
        cs, ce, co = [], [], []
        for i, rows, valid in _cand_layout():
            r = lax.broadcasted_iota(I32, (rows, tt), 0).astype(F32)
            if i is None:
                val = v1[8:16] + v2[0:1]
                eid = i1[8:16] * PEER_KEYS + i2[0:1]
                flat = (r + 8.0) * PEER_TOPK
            else:
                val = v1[i:i + 1] + v2[0:rows]
                eid = i1[i:i + 1] * PEER_KEYS + i2[0:rows]
                flat = r + float(i * PEER_TOPK)
                if valid < rows:
                    val = jnp.where(r < float(valid), val, -jnp.inf)
            cs.append(val)
            ce.append(eid)
            co.append(flat)
        cand = jnp.concatenate(cs, axis=0)
        top_s, top_e = _top_rows(cand, jnp.concatenate(co, axis=0), jnp.concatenate(ce, axis=0), PEER_TOPK)
        p = jnp.exp(top_s - top_s[0:1])
        gates = p / jnp.sum(p, axis=0, keepdims=True)
        e_ref[h * PEER_TOPK:(h + 1) * PEER_TOPK, :] = top_e.astype(I32)
        g_ref[h * PEER_TOPK:(h + 1) * PEER_TOPK, :] = gates


def _route(qp, k1, k2, tt):
    n, nq = qp.shape
    n_sel = PEER_HEADS * PEER_TOPK
    full = lambda a: pl.BlockSpec(a.shape, lambda i: (0,) * a.ndim)
    return pl.pallas_call(
        functools.partial(_route_kernel, tt=tt),
        grid=(n // tt,),
        in_specs=[pl.BlockSpec((tt, nq), lambda i: (i, 0)), full(k1), full(k2)],
        out_specs=[pl.BlockSpec((n_sel, tt), lambda i: (0, i)), pl.BlockSpec((n_sel, tt), lambda i: (0, i))],
        out_shape=[jax.ShapeDtypeStruct((n_sel, n), I32), jax.ShapeDtypeStruct((n_sel, n), F32)],
        compiler_params=_cparams(("arbitrary",)),
        name="route",
    )(qp, k1, k2)


def _final_kernel(h_ref, p_ref, g_ref, o_ref):
    tm, d = h_ref.shape
    n_seg = d // LANES
    p = jnp.concatenate([p_ref[pl.ds(k, tm, stride=n_seg), :] for k in range(n_seg)], axis=1)
    o_ref[...] = _rms(h_ref[...] + p, g_ref[...])


def _final(h, p, g, tm):
    n, d = h.shape
    row = pl.BlockSpec((tm, d), lambda i: (i, 0))
    return pl.pallas_call(
        _final_kernel,
        grid=(n // tm,),
        in_specs=[row, pl.BlockSpec((tm * (d // LANES), LANES), lambda i: (i, 0)),
                  pl.BlockSpec((1, d), lambda i: (0, 0))],
        out_specs=row,
        out_shape=jax.ShapeDtypeStruct((n, d), F32),
        compiler_params=_cparams(("arbitrary",)),
        name="final",
    )(h, p, g.reshape(1, d))


SC_CORES_V7X = 2
SC_SUBCORES_V7X = 16
SC_LANES_V7X = 16
PEER_TOK_BATCH = 32
PEER_ROW_CHUNK = 32
PEER_RING = 4


def _pack_words(x):
    bits = lax.bitcast_convert_type(x.astype(BF16).astype(F32), I32)
    half = bits.shape[1] // 2
    return (bits[:, half:] & jnp.int32(-65536)) | lax.shift_right_logical(bits[:, :half], 16)


def _pack_kernel(t_ref, o_ref):
    o_ref[...] = _pack_words(t_ref[...])


def _pack_bf16_pairs(t, rows=1024):
    e, d = t.shape
    return pl.pallas_call(
        _pack_kernel,
        grid=(e // rows,),
        in_specs=[pl.BlockSpec((rows, d), lambda i: (i, 0))],
        out_specs=pl.BlockSpec((rows, d // 2), lambda i: (i, 0)),
        out_shape=jax.ShapeDtypeStruct((e, d // 2), I32),
        compiler_params=_cparams(("arbitrary",)),
        name="pack_table",
    )(t)


def _unpack_pair(w):
    lo = lax.bitcast_convert_type(jnp.left_shift(w, 16), F32)
    hi = lax.bitcast_convert_type(w & jnp.int32(-65536), F32)
    return lo, hi


def _peer_sc_body(hn_hbm, e_hbm, g_hbm, u_hbm, v_hbm, out_hbm,
                  idx_v, gate_v, x_v, out_v, rows, p_v, act_v, sem, *, tpw, d, n_sel):
    nl = SC_LANES_V7X
    tb = PEER_TOK_BATCH
    rc = PEER_ROW_CHUNK
    n_chunk = n_sel // rc
    jobs_per_tok = 2 * n_chunk
    half = d // 2
    n_lane_blk = half // nl
    xs = half // LANES
    os_ = d // LANES
    wid =lax.axis_index("s") * SC_CORES_V7X + lax.axis_index("c")
    base = wid * tpw
    lane = lax.iota(I32, nl)
    zero = jnp.zeros((nl,), F32)
    c_gelu = 2.0 * math.sqrt(2.0 / math.pi)

    def gather_copy(tab_hbm, job):
        tok = job // jobs_per_tok
        c = (job % jobs_per_tok) % n_chunk
        b = job % PEER_RING
        return pltpu.make_async_copy(tab_hbm.at[idx_v.at[tok, pl.ds(c * rc, rc)]], rows.at[b], sem.at[b])

    def start(job):
        j = job % jobs_per_tok

        @pl.when(j < n_chunk)
        def _():
            gather_copy(u_hbm, job).start()

        @pl.when(j >= n_chunk)
        def _():
            gather_copy(v_hbm, job).start()

    def compute_u(tok, c, b):
        def rg_body(rg, _):
            r0 = rg * 8

            def jbody(j2, accs):
                off0 = j2 * (2 * nl)
                off1 = off0 + nl
                xrow = tok * xs + off0 // LANES
                xl = off0 % LANES
                x0 = plsc.bitcast(x_v[xrow, pl.ds(xl, nl)], BF16)
                x1 = plsc.bitcast(x_v[xrow, pl.ds(xl + nl, nl)], BF16)
                new = []
                for r in range(8):
                    w0 = plsc.bitcast(rows[b, r0 + r, pl.ds(off0, nl)], BF16)
                    w1 = plsc.bitcast(rows[b, r0 + r, pl.ds(off1, nl)], BF16)
                    lo, hi = _unpack_pair(plsc.bitcast(w0 * x0 + w1 * x1, I32))
                    new.append(accs[r] + (lo + hi))
                return tuple(new)

            accs = lax.fori_loop(0, n_lane_blk // 2, jbody, (zero,) * 8)
            for r in range(8):
                p_v[c * rc + r0 + r, :] = accs[r]
            return 0

        lax.fori_loop(0, rc // 8, rg_body, 0)

    def finish_act(tok):
        def eg_body(eg, _):
            e0 = eg * nl
            ridx = e0 + lane
            s = zero
            for l in range(nl):
                s = s + plsc.load_gather(p_v, [ridx, jnp.full((nl,), l, I32)])
            inner = c_gelu * (s + 0.044715 * (s * s * s))
            gl = s / (1.0 + jnp.exp(-inner))
            a = gl * gate_v[tok, pl.ds(e0, nl)]
            bits = lax.bitcast_convert_type(a, I32)
            rnd = bits + jnp.int32(0x7FFF) + (lax.shift_right_logical(bits, 16) & 1)
            hi16 = rnd & jnp.int32(-65536)
            act_v[pl.ds(e0, nl)] = hi16 | lax.shift_right_logical(hi16, 16)
            return 0

        lax.fori_loop(0, n_sel // nl, eg_body, 0)

        def zbody(j, _):
            off = j * nl
            out_v[tok * os_ + off // LANES, pl.ds(off % LANES, nl)] = zero
            return 0

        lax.fori_loop(0, d // nl, zbody, 0, unroll=4)

    def compute_v(tok, c, b):
        def rg_body(rg, _):
            r0 = rg * nl
            splat = [plsc.bitcast(plsc.load_gather(act_v, [jnp.full((nl,), 0, I32) + (c * rc + r0 + r)]), BF16)
                     for r in range(nl)]

            def tree(parts):
                while len(parts) > 1:
                    parts = [parts[i] + parts[i + 1] for i in range(0, len(parts), 2)]
                return parts[0]

            @plsc.parallel_loop(0, n_lane_blk, unroll=2)
            def _(j):
                off = j * nl
                los, his = [], []
                for r in range(0, nl, 2):
                    w0 = plsc.bitcast(rows[b, r0 + r, pl.ds(off, nl)], BF16)
                    w1 = plsc.bitcast(rows[b, r0 + r + 1, pl.ds(off, nl)], BF16)
                    lo, hi = _unpack_pair(plsc.bitcast(w0 * splat[r] + w1 * splat[r + 1], I32))
                    los.append(lo)
                    his.append(hi)
                orow = tok * os_ + off // LANES
                ol = off % LANES
                out_v[orow, pl.ds(ol, nl)] = out_v[orow, pl.ds(ol, nl)] + tree(los)
                out_v[orow + xs, pl.ds(ol, nl)] = out_v[orow + xs, pl.ds(ol, nl)] + tree(his)

            return 0

        lax.fori_loop(0, rc // nl, rg_body, 0)

    def batch_body(bi, _):
        t0 = base + bi * tb
        pltpu.sync_copy(e_hbm.at[pl.ds(t0, tb)], idx_v)
        pltpu.sync_copy(g_hbm.at[pl.ds(t0, tb)], gate_v)
        pltpu.sync_copy(hn_hbm.at[pl.ds(t0 * xs, tb * xs)], x_v)
        for pre in range(PEER_RING - 1):
            start(pre)

        def job_body(job, _):
            @pl.when(job + (PEER_RING - 1) < tb * jobs_per_tok)
            def _():
                start(job + (PEER_RING - 1))

            j = job % jobs_per_tok
            gather_copy(u_hbm, job).wait()
            tok = job // jobs_per_tok
            b = job % PEER_RING

            @pl.when(j < n_chunk)
            def _():
                compute_u(tok, j, b)

            @pl.when(j == n_chunk - 1)
            def _():
                finish_act(tok)

            @pl.when(j >= n_chunk)
            def _():
                compute_v(tok, j - n_chunk, b)

            return 0

        lax.fori_loop(0, tb * jobs_per_tok, job_body, 0)
        pltpu.sync_copy(out_v, out_hbm.at[pl.ds(t0 * os_, tb * os_)])
        return 0

    lax.fori_loop(0, tpw // tb, batch_body, 0)


def _peer_sc(x_pk, experts, gates, u_tab, v_tab):
    d = 2 * u_tab.shape[1]
    n = x_pk.shape[0] * LANES // (d // 2)
    n_sel = experts.shape[1]
    nw = SC_CORES_V7X * SC_SUBCORES_V7X
    tpw = n // nw
    mesh = plsc.VectorSubcoreMesh(core_axis_name="c", subcore_axis_name="s",
                                  num_cores=SC_CORES_V7X, num_subcores=SC_SUBCORES_V7X)
    body = functools.partial(_peer_sc_body, tpw=tpw, d=d, n_sel=n_sel)
    call = pl.kernel(
        body,
        out_type=jax.ShapeDtypeStruct((n * d // LANES, LANES), F32),
        mesh=mesh,
        scratch_types=[pltpu.VMEM((PEER_TOK_BATCH, n_sel), I32),
                       pltpu.VMEM((PEER_TOK_BATCH, n_sel), F32),
                       pltpu.VMEM((PEER_TOK_BATCH * d // 2 // LANES, LANES), I32),
                       pltpu.VMEM((PEER_TOK_BATCH * d // LANES, LANES), F32),
                       pltpu.VMEM((PEER_RING, PEER_ROW_CHUNK, d // 2), I32),
                       pltpu.VMEM((n_sel, SC_LANES_V7X), F32),
                       pltpu.VMEM((n_sel,), I32),
                       pltpu.SemaphoreType.DMA((PEER_RING,))],
        compiler_params=pltpu.CompilerParams(needs_layout_passes=False, use_tc_tiling_on_sc=False),
        name="peer_sc",
    )
    return call(x_pk, experts, gates, u_tab, v_tab)


def kernel(x, norm1_g, w_in, a_re, a_im, log_dt, b_re, b_im, c_re, c_im, d_skip, w_glu, w_ssm_up, w_attn_up,
           w_out, norm2_g, peer_wq, peer_k1, peer_k2, peer_u, peer_v, norm_f_g):
    bsz, seq, d = x.shape
    depth = norm1_g.shape[0]
    chunks = _time_chunks(seq)
    h = x
    for layer in range(depth):
        last = layer + 1 == depth
        kv_w, main_w = _in_weights(w_in[layer], seq)
        s5p = _s5_params(a_re[layer], a_im[layer], log_dt[layer], b_re[layer], b_im[layer], c_re[layer],
                         c_im[layer], d_skip[layer], w_glu[layer], nb=bsz)
        n_state = s5p[4].shape[1]
        wsu = w_ssm_up[layer].astype(BF16)
        wau = w_attn_up[layer].astype(BF16)
        wo = w_out[layer].astype(BF16)
        wq = peer_wq[layer].astype(BF16)
        k1 = peer_k1[layer].astype(BF16)
        k2 = peer_k2[layer].astype(BF16)
        u_pk = _pack_bf16_pairs(peer_u[layer])
        v_pk = _pack_bf16_pairs(peer_v[layer])
        kv_tm = min(ROW_TILE, seq)
        head_len = 2 * DSA_KT
        split_kv = len(chunks) > 1 and chunks[0][1] <= head_len < seq
        k4, vt, ki = _kvproj(h, norm1_g[layer], kv_w, norm1_g[layer], seq=head_len if split_kv else seq,
                             tm=kv_tm, kt=DSA_KT)
        st_re = jnp.zeros((bsz, n_state), F32)
        st_im = jnp.zeros((bsz, n_state), F32)
        outs = []
        routed = ki
        peer_outs = []
        for c, (s0, sc) in enumerate(chunks):
            tm = math.gcd(math.gcd(s0, sc), ROW_TILE)
            if split_kv and s0 + sc > head_len:
                k4, vt, ki = _kvproj(h, norm1_g[layer], kv_w, routed, seq=seq, tm=kv_tm, kt=DSA_KT)
                split_kv = False
            after = (routed, peer_outs[c - SC_LAG] if c >= SC_LAG else ki)
            u, q, qi, wit, gs, ga = _inproj(h, norm1_g[layer], main_w, after, s0=s0, sc=sc, tm=tm)
            d_ssm = u.shape[-1]
            u_tb = u.transpose(1, 0, 2).reshape(sc * bsz, d_ssm)
            y_tb, st_re, st_im = _s5(u_tb, st_re, st_im, s5p, nb=bsz, tc=S5_STEPS)
            ys = y_tb.reshape(sc, bsz, d_ssm).transpose(1, 0, 2)
            ya = _dsa(q, qi, wit, ki, k4, vt, s0=s0, seq_total=seq, tq=DSA_TQ, kt=DSA_KT)
            hm, x_pk, qp = _merge(h, ys, ya, gs, ga, wsu, wau, wo, norm2_g[layer], wq, s0=s0, tm=tm)
            nt = bsz * sc
            e_t, g_t = _route(qp.reshape(nt, -1), k1, k2, tt=ROUTE_TOKENS)
            routed = e_t
            po = _peer_sc(x_pk.reshape(-1, LANES), e_t.T, g_t.T, u_pk, v_pk)
            peer_outs.append(po)
            hm2 = hm.reshape(nt, d)
            o = _final(hm2, po, norm_f_g, tm=tm) if last else hm2 + po.reshape(nt, d)
            outs.append(o.reshape(bsz, sc, d))
        h = jnp.concatenate(outs, axis=1)
    return h
```

```python
import functools
import math

import numpy as np
import jax
import jax.numpy as jnp
from jax import lax
from jax.experimental import pallas as pl
from jax.experimental.pallas import tpu as pltpu
from jax.experimental.pallas import tpu_sc as plsc

F32 = jnp.float32
BF16 = jnp.bfloat16
I32 = jnp.int32

SSM_GROUP = 16
SSM_STATE = 64
ATTN_HEADS = 8
ATTN_KV_HEADS = 2
HEAD_DIM = 64
IDX_HEADS = 8
IDX_DIM = 32
TOPK_MAX = 256
ROPE_THETA = 10000.0
NEG_BIG = -1e30
PEER_HEADS = 8
PEER_KEYS = 128
PEER_KEY_DIM = 128
PEER_TOPK = 16
NORM_EPS = 1e-6

TIME_SPLIT_32NDS = (1, 3, 4, 4, 4, 4, 4, 3, 2, 2, 1)
SC_LAG = 3
ROW_TILE = 512
S5_STEPS = 64
ROUTE_TOKENS = 256
DSA_TQ = 128
DSA_KT = 256
LANES = 128
INT_MIN = -(2 ** 31)
VMEM_LIMIT = 56 * 1024 * 1024


def _time_chunks(seq):
    unit = seq // 32
    if seq % 32 == 0 and unit % DSA_TQ == 0:
        sizes = [f * unit for f in TIME_SPLIT_32NDS]
    else:
        step = min(ROW_TILE, seq)
        sizes = [step] * (seq // step)
    assert sum(sizes) == seq
    starts = np.cumsum([0] + sizes[:-1]).tolist()
    return list(zip(starts, sizes))


def _cparams(sem):
    return pltpu.CompilerParams(dimension_semantics=sem, vmem_limit_bytes=VMEM_LIMIT)


def _gelu_tanh(x):
    return 0.5 * x * (1.0 + jnp.tanh(math.sqrt(2.0 / math.pi) * (x + 0.044715 * (x * x * x))))


def _sigmoid(x):
    return 1.0 / (1.0 + jnp.exp(-x))


def _rms(x, g):
    return x * lax.rsqrt(jnp.mean(x * x, axis=-1, keepdims=True) + NORM_EPS) * g


def _rot_cols(w, hd):
    d, n = w.shape
    w3 = w.reshape(d, n // hd, hd)
    half = hd // 2
    return jnp.concatenate([-w3[..., half:], w3[..., :half]], axis=-1).reshape(d, n)


def _rope_full(seq, hd, heads):
    pos = jnp.arange(seq, dtype=F32)
    inv = ROPE_THETA ** (-jnp.arange(0, hd, 2, dtype=F32) / hd)
    ang = pos[:, None] * inv[None, :]
    c = jnp.concatenate([jnp.cos(ang), jnp.cos(ang)], axis=-1)
    s = jnp.concatenate([jnp.sin(ang), jnp.sin(ang)], axis=-1)
    return jnp.tile(c, (1, heads)), jnp.tile(s, (1, heads))


def _in_weights(w_in, seq):
    d = w_in.shape[0]
    d_ssm = d // 2
    d_q = ATTN_HEADS * HEAD_DIM
    d_kv = ATTN_KV_HEADS * HEAD_DIM
    d_qi = IDX_HEADS * IDX_DIM
    splits = (d_ssm, d_q, d_kv, d_kv, d_qi, IDX_DIM, IDX_HEADS, d, d)
    offs = np.cumsum(splits)[:-1].tolist()
    wu, wq, wk, wv, wqi, wki, wwi, wgs, wga = jnp.split(w_in, offs, axis=1)
    pad = jnp.zeros((d, 128 - IDX_DIM), F32)
    cq, sq = _rope_full(seq, HEAD_DIM, ATTN_HEADS)
    ck, sk = _rope_full(seq, HEAD_DIM, ATTN_KV_HEADS)
    cqi, sqi = _rope_full(seq, IDX_DIM, IDX_HEADS)
    cki, ski = _rope_full(seq, IDX_DIM, 1)
    tpad = jnp.zeros((seq, 128 - IDX_DIM), F32)
    kv = dict(
        w=jnp.concatenate([wk, wki, pad], axis=1).astype(BF16),
        wvt=wv.T.astype(BF16),
        wr=jnp.concatenate([_rot_cols(wk, HEAD_DIM), _rot_cols(wki, IDX_DIM), pad], axis=1).astype(BF16),
        cs=jnp.concatenate([ck, cki, tpad], axis=1), sn=jnp.concatenate([sk, ski, tpad], axis=1))
    main = dict(
        w=jnp.concatenate([wu, wq, wqi, wgs, wga], axis=1).astype(BF16),
        wr=jnp.concatenate([_rot_cols(wq, HEAD_DIM), _rot_cols(wqi, IDX_DIM)], axis=1).astype(BF16),
        cs=jnp.concatenate([cq, cqi], axis=1), sn=jnp.concatenate([sq, sqi], axis=1),
        wwit=wwi.T.astype(BF16))
    return kv, main


def _kvproj_kernel(x_ref, g_ref, w_ref, wr_ref, wvt_ref, cs_ref, sn_ref, after_ref, k_ref, vt_ref, ki_ref,
                   *, d_kv, kt):
    del after_ref
    xb = _rms(x_ref[0], g_ref[...]).astype(BF16)

    def mm(ref, lo, n):
        return jnp.dot(xb, ref[:, lo:lo + n], preferred_element_type=F32)

    k = mm(w_ref, 0, d_kv) * cs_ref[:, :d_kv] + mm(wr_ref, 0, d_kv) * sn_ref[:, :d_kv]
    for n in range(ATTN_KV_HEADS):
        k_ref[0, n] = k[:, n * HEAD_DIM:(n + 1) * HEAD_DIM].astype(BF16)
    vt = lax.dot_general(wvt_ref[...], xb, (((1,), (1,)), ((), ())), preferred_element_type=F32)
    for n in range(ATTN_KV_HEADS):
        for j in range(vt.shape[1] // kt):
            vt_ref[0, n, j] = vt[n * HEAD_DIM:(n + 1) * HEAD_DIM, j * kt:(j + 1) * kt].astype(BF16)
    kiw = (mm(w_ref, d_kv, 128) * cs_ref[:, d_kv:d_kv + 128]
           + mm(wr_ref, d_kv, 128) * sn_ref[:, d_kv:d_kv + 128])
    ki_ref[0] = kiw[:, :IDX_DIM].astype(BF16)


def _kvproj(x, norm_g, kv, after, seq, tm, kt):
    bsz, _, d = x.shape
    d_kv = ATTN_KV_HEADS * HEAD_DIM
    full = lambda a: pl.BlockSpec(a.shape, lambda s, b: (0,) * a.ndim)
    g = norm_g.reshape(1, d)
    ncs = kv["cs"].shape[1]
    return pl.pallas_call(
        functools.partial(_kvproj_kernel, d_kv=d_kv, kt=kt),
        grid=(seq // tm, bsz),
        in_specs=[pl.BlockSpec((1, tm, d), lambda s, b: (b, s, 0)), full(g), full(kv["w"]), full(kv["wr"]),
                  full(kv["wvt"]),
                  pl.BlockSpec((tm, ncs), lambda s, b: (s, 0)), pl.BlockSpec((tm, ncs), lambda s, b: (s, 0)),
                  pl.BlockSpec(memory_space=pl.ANY)],
        out_specs=[pl.BlockSpec((1, ATTN_KV_HEADS, tm, HEAD_DIM), lambda s, b: (b, 0, s, 0)),
                   pl.BlockSpec((1, ATTN_KV_HEADS, tm // kt, HEAD_DIM, kt), lambda s, b: (b, 0, s, 0, 0)),
                   pl.BlockSpec((1, tm, IDX_DIM), lambda s, b: (b, s, 0))],
        out_shape=[jax.ShapeDtypeStruct((bsz, ATTN_KV_HEADS, seq, HEAD_DIM), BF16),
                   jax.ShapeDtypeStruct((bsz, ATTN_KV_HEADS, seq // kt, HEAD_DIM, kt), BF16),
                   jax.ShapeDtypeStruct((bsz, seq, IDX_DIM), BF16)],
        compiler_params=_cparams(("arbitrary", "arbitrary")),
        name="kvproj",
    )(x, g, kv["w"], kv["wr"], kv["wvt"], kv["cs"], kv["sn"], after)


def _inproj_kernel(x_ref, g_ref, w_ref, wr_ref, cs_ref, sn_ref, wwit_ref, after_tc_ref, after_sc_ref,
                   u_ref, q_ref, qi_ref, wit_ref, gs_ref, ga_ref, *, d_ssm, d_q, d_qi, d_model, q_scale, wi_scale):
    del after_tc_ref, after_sc_ref
    xb = _rms(x_ref[0], g_ref[...]).astype(BF16)

    def mm(ref, lo, n):
        return jnp.dot(xb, ref[:, lo:lo + n], preferred_element_type=F32)

    o = 0
    u_ref[0] = mm(w_ref, o, d_ssm).astype(BF16)
    o += d_ssm
    q = mm(w_ref, o, d_q) * cs_ref[:, :d_q] + mm(wr_ref, 0, d_q) * sn_ref[:, :d_q]
    q_ref[0] = (q * q_scale).astype(BF16)
    o += d_q
    qi = mm(w_ref, o, d_qi) * cs_ref[:, d_q:d_q + d_qi] + mm(wr_ref, d_q, d_qi) * sn_ref[:, d_q:d_q + d_qi]
    qi_ref[0] = qi.astype(BF16)
    o += d_qi
    gs_ref[0] = _sigmoid(mm(w_ref, o, d_model)).astype(BF16)
    o += d_model
    ga_ref[0] = _sigmoid(mm(w_ref, o, d_model)).astype(BF16)
    wit_ref[0] = lax.dot_general(wwit_ref[...], xb, (((1,), (1,)), ((), ())),
                                 preferred_element_type=F32) * wi_scale


def _inproj(x, norm_g, main, after, s0, sc, tm):
    bsz, _, d = x.shape
    d_ssm = d // 2
    d_q = ATTN_HEADS * HEAD_DIM
    d_qi = IDX_HEADS * IDX_DIM
    i0 = s0 // tm
    kern = functools.partial(
        _inproj_kernel, d_ssm=d_ssm, d_q=d_q, d_qi=d_qi, d_model=d,
        q_scale=HEAD_DIM ** -0.5 * math.log2(math.e), wi_scale=(IDX_HEADS ** -0.5) * (IDX_DIM ** -0.5))
    tok = lambda n: pl.BlockSpec((1, tm, n), lambda s, b: (b, s, 0))
    full = lambda a: pl.BlockSpec(a.shape, lambda s, b: (0,) * a.ndim)
    g = norm_g.reshape(1, d)
    ncs = main["cs"].shape[1]
    outs = [(d_ssm, BF16), (d_q, BF16), (d_qi, BF16)]
    return pl.pallas_call(
        kern,
        grid=(sc // tm, bsz),
        in_specs=[pl.BlockSpec((1, tm, d), lambda s, b: (b, s + i0, 0)), full(g), full(main["w"]), full(main["wr"]),
                  pl.BlockSpec((tm, ncs), lambda s, b: (s + i0, 0)),
                  pl.BlockSpec((tm, ncs), lambda s, b: (s + i0, 0)), full(main["wwit"]),
                  pl.BlockSpec(memory_space=pl.ANY), pl.BlockSpec(memory_space=pl.ANY)],
        out_specs=[tok(n) for n, _ in outs] + [pl.BlockSpec((1, IDX_HEADS, tm), lambda s, b: (b, 0, s)),
                                                tok(d), tok(d)],
        out_shape=[jax.ShapeDtypeStruct((bsz, sc, n), dt) for n, dt in outs]
        + [jax.ShapeDtypeStruct((bsz, IDX_HEADS, sc), F32),
           jax.ShapeDtypeStruct((bsz, sc, d), BF16), jax.ShapeDtypeStruct((bsz, sc, d), BF16)],
        compiler_params=_cparams(("arbitrary", "arbitrary")),
        name="inproj",
    )(x, g, main["w"], main["wr"], main["cs"], main["sn"], main["wwit"], *after)


def _s5_kernel(u_ref, sre_in, sim_in, bre_ref, bim_ref, cre_ref, cim_ref, are_ref, aim_ref, dsk_ref, wglu_ref,
               y_ref, st_re, st_im, sre, sim, *, tc, nb, lane_chunk):
    @pl.when(pl.program_id(0) == 0)
    def _():
        st_re[...] = sre_in[...]
        st_im[...] = sim_in[...]

    u = u_ref[...]
    n_half = bre_ref.shape[0]
    hin = bre_ref.shape[1]
    hst = bre_ref.shape[2]
    for h in range(n_half):
        uh = u[:, h * hin:(h + 1) * hin]
        sre[:, h * hst:(h + 1) * hst] = jnp.dot(uh, bre_ref[h], preferred_element_type=F32)
        sim[:, h * hst:(h + 1) * hst] = jnp.dot(uh, bim_ref[h], preferred_element_type=F32)

    n_state = sre.shape[1]
    for c in range(n_state // lane_chunk):
        cols = slice(c * lane_chunk, (c + 1) * lane_chunk)
        ar = are_ref[:, cols]
        ai = aim_ref[:, cols]

        def step(t, carry, cols=cols, ar=ar, ai=ai):
            sr, si = carry
            r0 = pl.multiple_of(t * nb, nb)
            nr = ar * sr - ai * si + sre[pl.ds(r0, nb), cols]
            ni = ar * si + ai * sr + sim[pl.ds(r0, nb), cols]
            sre[pl.ds(r0, nb), cols] = nr
            sim[pl.ds(r0, nb), cols] = ni
            return nr, ni

        sr, si = lax.fori_loop(0, tc, step, (st_re[:, cols], st_im[:, cols]), unroll=4)
        st_re[:, cols] = sr
        st_im[:, cols] = si

    ys = []
    for h in range(n_half):
        srh = sre[:, h * hst:(h + 1) * hst].astype(BF16)
        sih = sim[:, h * hst:(h + 1) * hst].astype(BF16)
        ys.append(jnp.dot(srh, cre_ref[h], preferred_element_type=F32)
                  - jnp.dot(sih, cim_ref[h], preferred_element_type=F32))
    y = jnp.concatenate(ys, axis=-1) + dsk_ref[...] * u.astype(F32)
    y = _gelu_tanh(y)
    gate = jnp.dot(y.astype(BF16), wglu_ref[...], preferred_element_type=F32)
    y_ref[...] = (y * _sigmoid(gate)).astype(BF16)


def _s5_params(a_re, a_im, log_dt, b_re, b_im, c_re, c_im, d_skip, w_glu, nb):
    groups = a_re.shape[0]
    d_ssm = groups * SSM_GROUP
    n_state = groups * SSM_STATE
    lam = lax.complex(a_re, a_im)
    dt = jnp.exp(log_dt)[:, None]
    a_bar = jnp.exp(lam * dt)
    b_bar = ((a_bar - 1.0) / lam)[..., None] * lax.complex(b_re, b_im)
    gh = min(groups, 256 // SSM_GROUP)
    n_half = groups // gh
    eye = jnp.eye(gh, dtype=F32)

    def bmat(bb):
        b4 = bb.reshape(n_half, gh, SSM_STATE, SSM_GROUP)
        return jnp.einsum('hgpc,gk->hgckp', b4, eye).reshape(n_half, gh * SSM_GROUP, gh * SSM_STATE)

    def cmat(cc):
        c4 = cc.reshape(n_half, gh, SSM_GROUP, SSM_STATE)
        return jnp.einsum('hgcp,gk->hgpkc', c4, eye).reshape(n_half, gh * SSM_STATE, gh * SSM_GROUP)

    return (bmat(jnp.real(b_bar)).astype(BF16), bmat(jnp.imag(b_bar)).astype(BF16),
            cmat(c_re).astype(BF16), cmat(c_im).astype(BF16),
            jnp.broadcast_to(jnp.real(a_bar).reshape(1, n_state), (nb, n_state)),
            jnp.broadcast_to(jnp.imag(a_bar).reshape(1, n_state), (nb, n_state)),
            d_skip.reshape(1, d_ssm), w_glu.astype(BF16))


def _s5(u_tb, st_re, st_im, params, nb, tc):
    rows, d_ssm = u_tb.shape
    n_state = st_re.shape[1]
    blk = tc * nb
    full = lambda a: pl.BlockSpec(a.shape, lambda i: (0,) * a.ndim)
    st_spec = pl.BlockSpec((nb, n_state), lambda i: (0, 0))
    kern = functools.partial(_s5_kernel, tc=tc, nb=nb, lane_chunk=512)
    return pl.pallas_call(
        kern,
        grid=(rows // blk,),
        in_specs=[pl.BlockSpec((blk, d_ssm), lambda i: (i, 0)), st_spec, st_spec] + [full(p) for p in params],
        out_specs=[pl.BlockSpec((blk, d_ssm), lambda i: (i, 0)), st_spec, st_spec],
        out_shape=[jax.ShapeDtypeStruct((rows, d_ssm), BF16),
                   jax.ShapeDtypeStruct((nb, n_state), F32), jax.ShapeDtypeStruct((nb, n_state), F32)],
        scratch_shapes=[pltpu.VMEM((blk, n_state), F32), pltpu.VMEM((blk, n_state), F32)],
        compiler_params=_cparams(("arbitrary",)),
        name="s5",
    )(u_tb, st_re, st_im, *params)


PART_ROWS = 32


def _dsa_kernel(qi_ref, wit_ref, q_ref, ki_ref, k_ref, vt_ref, o_ref, key_s, bias_s, lg_s, p_s,
                *, qb0, tq, kt, sub, topk, seq_bits):
    qb = pl.program_id(1) + qb0
    nkt = ((qb * tq + tq + sub * kt - 1) // (sub * kt)) * sub
    q_pos = qb * tq + lax.broadcasted_iota(I32, (1, tq), 1)
    k_eff = jnp.minimum(topk, q_pos + 1).astype(F32)

    qi = qi_ref[0]
    wit = wit_ref[0]
    qipair = [jnp.concatenate([qi[:, (2 * hp) * IDX_DIM:(2 * hp + 1) * IDX_DIM],
                               qi[:, (2 * hp + 1) * IDX_DIM:(2 * hp + 2) * IDX_DIM]], axis=0)
              for hp in range(IDX_HEADS // 2)]

    def key_pos(t):
        return t * kt + lax.broadcasted_iota(I32, (kt, tq), 0)

    def score_tile(t2, _):
        for hf in range(sub):
            r0 = pl.multiple_of((t2 * sub + hf) * kt, kt)
            ki_t = ki_ref[0, pl.ds(r0, kt), :]
            for hp in range(IDX_HEADS // 2):
                lg_s[hf, hp] = lax.dot_general(ki_t, qipair[hp], (((1,), (1,)), ((), ())),
                                               preferred_element_type=F32)
        for hf in range(sub):
            t = t2 * sub + hf
            r0 = pl.multiple_of(t * kt, kt)
            sc = jnp.zeros((kt, tq), F32)
            for hp in range(IDX_HEADS // 2):
                rel = lg_s[hf, hp]
                sc = sc + jnp.maximum(rel[:, :tq], 0.0) * wit[2 * hp:2 * hp + 1, :]
                sc = sc + jnp.maximum(rel[:, tq:], 0.0) * wit[2 * hp + 1:2 * hp + 2, :]
            bits = lax.bitcast_convert_type(sc, I32)
            key = jnp.where(bits < 0, bits ^ jnp.int32(0x7FFFFFFF), bits)
            key = jnp.where(key_pos(t) <= q_pos, key, jnp.int32(INT_MIN))
            key_s[pl.ds(r0, kt), :] = key
        return 0

    lax.fori_loop(0, nkt // sub, score_tile, 0)

    def count(pred_fn):
        def body(t2, acc):
            for hf in range(sub):
                t = t2 * sub + hf
                r0 = pl.multiple_of(t * kt, kt)
                m = pred_fn(key_s[pl.ds(r0, kt), :], t)
                ones = jnp.where(m, 1.0, 0.0).reshape(kt // PART_ROWS, PART_ROWS, tq)
                acc = acc + jnp.sum(ones, axis=0)
            return acc
        acc = lax.fori_loop(0, nkt // sub, body, jnp.zeros((PART_ROWS, tq), F32))
        return jnp.sum(acc, axis=0, keepdims=True)

    def bit_step(i, carry):
        u, cnt_u = carry
        bit = jnp.left_shift(jnp.int32(1), 31 - i)
        cand_u = u | bit
        cand_s = cand_u ^ jnp.int32(INT_MIN)
        cnt = count(lambda kk, t: kk >= cand_s)
        ok = cnt >= k_eff
        return jnp.where(ok, cand_u, u), jnp.where(ok, cnt, cnt_u)

    all_keys = (nkt * kt).astype(F32)
    u_thr, cnt_ge = lax.fori_loop(0, 32, bit_step, (jnp.zeros((1, tq), I32), jnp.zeros((1, tq), F32) + all_keys))
    thr = u_thr ^ jnp.int32(INT_MIN)
    has_tie = jnp.max(cnt_ge - k_eff) > 0.0

    def tie_cut():
        need_eq = k_eff - count(lambda kk, t: kk > thr)

        def pos_step(i, c):
            bit = jnp.left_shift(jnp.int32(1), seq_bits - 1 - i)
            cand = c | bit
            cnt = count(lambda kk, t: (kk == thr) & (key_pos(t) < cand))
            return jnp.where(cnt < need_eq, cand, c)
        return lax.fori_loop(0, seq_bits, pos_step, jnp.zeros((1, tq), I32))

    cut = lax.cond(has_tie, tie_cut, lambda: jnp.full((1, tq), 2 ** seq_bits, I32))

    def bias_tile(t, _):
        r0 = pl.multiple_of(t * kt, kt)
        key = key_s[pl.ds(r0, kt), :]
        sel = (key > thr) | ((key == thr) & (key_pos(t) <= cut))
        bias_s[pl.ds(r0, kt), :] = jnp.where(sel, 0.0, NEG_BIG)
        return 0

    lax.fori_loop(0, nkt, bias_tile, 0)

    q = q_ref[0]
    grp = ATTN_HEADS // ATTN_KV_HEADS
    pairs_per_kv = grp // 2
    n_unit = ATTN_KV_HEADS * pairs_per_kv
    wq = 2 * tq
    qpair = [jnp.concatenate([q[:, (2 * u) * HEAD_DIM:(2 * u + 1) * HEAD_DIM],
                              q[:, (2 * u + 1) * HEAD_DIM:(2 * u + 2) * HEAD_DIM]], axis=0)
             for u in range(n_unit)]

    def attn_tile(t, carry):
        ms, ls, accs = list(carry[0]), list(carry[1]), list(carry[2])
        for hf in range(sub):
            r0 = pl.multiple_of((t * sub + hf) * kt, kt)
            bias = bias_s[pl.ds(r0, kt), :]
            bias2 = jnp.concatenate([bias, bias], axis=1)
            for u in range(n_unit):
                k_t = k_ref[0, u // pairs_per_kv, pl.ds(r0, kt), :]
                lg_s[hf, u] = lax.dot_general(k_t, qpair[u], (((1,), (1,)), ((), ())),
                                              preferred_element_type=F32) + bias2
        for hf in range(sub):
            for u in range(n_unit):
                pieces = [slice(r * PART_ROWS, (r + 1) * PART_ROWS) for r in range(kt // PART_ROWS)]
                part = lg_s[hf, u, pieces[0], :]
                for rs in pieces[1:]:
                    part = jnp.maximum(part, lg_s[hf, u, rs, :])
                m_new = jnp.maximum(ms[u], jnp.max(part, axis=0, keepdims=True))
                alpha = jnp.exp2(ms[u] - m_new)
                psum = jnp.zeros((PART_ROWS, wq), F32)
                for rs in pieces:
                    p = jnp.exp2(lg_s[hf, u, rs, :] - m_new)
                    psum = psum + p
                    p_s[hf, u, rs, :] = p.astype(BF16)
                v_t = vt_ref[0, u // pairs_per_kv, t * sub + hf]
                ls[u] = alpha * ls[u] + jnp.sum(psum, axis=0, keepdims=True)
                accs[u] = alpha * accs[u] + jnp.dot(v_t, p_s[hf, u], preferred_element_type=F32)
                ms[u] = m_new
        return tuple(ms), tuple(ls), tuple(accs)

    init = (tuple(jnp.full((1, wq), NEG_BIG, F32) for _ in range(n_unit)),
            tuple(jnp.zeros((1, wq), F32) for _ in range(n_unit)),
            tuple(jnp.zeros((HEAD_DIM, wq), F32) for _ in range(n_unit)))
    _, ls, accs = lax.fori_loop(0, nkt // sub, attn_tile, init)
    for n in range(ATTN_KV_HEADS):
        o_ref[0, 0, n] = jnp.concatenate([accs[n * pairs_per_kv + pg] / ls[n * pairs_per_kv + pg]
                                          for pg in range(pairs_per_kv)], axis=1).astype(BF16)


def _dsa(q, qi, wit, ki, k4, vt, s0, seq_total, tq, kt):
    bsz, sc, _ = q.shape
    seq = ki.shape[1]
    topk = min(TOPK_MAX, seq_total // 4)
    nqb = sc // tq
    grp = ATTN_HEADS // ATTN_KV_HEADS
    seq_bits = int(math.log2(seq))
    assert 2 ** seq_bits == seq
    sub = 2 if seq % (2 * kt) == 0 else 1
    n_unit = ATTN_HEADS // 2
    assert IDX_HEADS // 2 <= n_unit
    kern =functools.partial(_dsa_kernel, qb0=s0 // tq, tq=tq, kt=kt, sub=sub, topk=topk, seq_bits=seq_bits)
    o_t = pl.pallas_call(
        kern,
        grid=(bsz, nqb),
        in_specs=[pl.BlockSpec((1, tq, IDX_HEADS * IDX_DIM), lambda b, j: (b, j, 0)),
                  pl.BlockSpec((1, IDX_HEADS, tq), lambda b, j: (b, 0, j)),
                  pl.BlockSpec((1, tq, ATTN_HEADS * HEAD_DIM), lambda b, j: (b, j, 0)),
                  pl.BlockSpec((1, seq, IDX_DIM), lambda b, j: (b, 0, 0)),
                  pl.BlockSpec((1, ATTN_KV_HEADS, seq, HEAD_DIM), lambda b, j: (b, 0, 0, 0)),
                  pl.BlockSpec((1, ATTN_KV_HEADS, seq // kt, HEAD_DIM, kt), lambda b, j: (b, 0, 0, 0, 0))],
        out_specs=pl.BlockSpec((1, 1, ATTN_KV_HEADS, HEAD_DIM, grp * tq), lambda b, j: (b, j, 0, 0, 0)),
        out_shape=jax.ShapeDtypeStruct((bsz, nqb, ATTN_KV_HEADS, HEAD_DIM, grp * tq), BF16),
        scratch_shapes=[pltpu.VMEM((seq, tq), I32), pltpu.VMEM((seq, tq), F32),
                        pltpu.VMEM((sub, n_unit, kt, 2 * tq), F32),
                        pltpu.VMEM((sub, n_unit, kt, 2 * tq), BF16)],
        compiler_params=_cparams(("arbitrary", "arbitrary")),
        name="dsa",
    )(qi, wit, q, ki, k4, vt)
    o = o_t.reshape(bsz, nqb, ATTN_KV_HEADS, HEAD_DIM, grp, tq).transpose(0, 1, 5, 2, 4, 3)
    return o.reshape(bsz, sc, ATTN_HEADS * HEAD_DIM)


def _merge_kernel(x_ref, ys_ref, ya_ref, gs_ref, ga_ref, wsu_ref, wau_ref, wout_ref, g2_ref, wq_ref,
                  h_ref, hn_ref, qp_ref):
    ms = jnp.dot(ys_ref[0], wsu_ref[...], preferred_element_type=F32)
    ma = jnp.dot(ya_ref[0], wau_ref[...], preferred_element_type=F32)
    merged = gs_ref[0].astype(F32) * ms + ga_ref[0].astype(F32) * ma
    h = x_ref[0] + jnp.dot(merged.astype(BF16), wout_ref[...], preferred_element_type=F32)
    h_ref[0] = h
    hb = _rms(h, g2_ref[...]).astype(BF16)
    words = _pack_words(hb)
    half = words.shape[1]
    n_seg = half // LANES
    tm = words.shape[0]
    for p in range(n_seg):
        hn_ref[0, pl.ds(p, tm, stride=n_seg), :] = words[:, p * LANES:(p + 1) * LANES]
    qp_ref[0] = jnp.dot(hb, wq_ref[...], preferred_element_type=F32).astype(BF16)


def _merge(x, ys, ya, gs, ga, wsu, wau, wo, norm2_g, wq, s0, tm):
    bsz, sc, _ = ya.shape
    d = x.shape[2]
    i0 = s0 // tm
    loc = lambda a: pl.BlockSpec((1, tm, a.shape[2]), lambda b, i: (b, i, 0))
    full = lambda a: pl.BlockSpec(a.shape, lambda b, i: (0,) * a.ndim)
    g2 = norm2_g.reshape(1, d)
    nq = wq.shape[1]
    n_seg = d // 2 // LANES
    out = lambda n: pl.BlockSpec((1, tm, n), lambda b, i: (b, i, 0))
    return pl.pallas_call(
        _merge_kernel,
        grid=(bsz, sc // tm),
        in_specs=[pl.BlockSpec((1, tm, d), lambda b, i: (b, i + i0, 0)), loc(ys), loc(ya), loc(gs), loc(ga),
                  full(wsu), full(wau), full(wo), full(g2), full(wq)],
        out_specs=[out(d), pl.BlockSpec((1, tm * n_seg, LANES), lambda b, i: (b, i, 0)), out(nq)],
        out_shape=[jax.ShapeDtypeStruct((bsz, sc, d), F32), jax.ShapeDtypeStruct((bsz, sc * n_seg, LANES), I32),
                   jax.ShapeDtypeStruct((bsz, sc, nq), BF16)],
        compiler_params=_cparams(("arbitrary", "arbitrary")),
        name="merge",
    )(x, ys, ya, gs, ga, wsu, wau, wo, g2, wq)


def _cand_layout():
    blocks = []
    blocks.append((0, 16, 16))
    for i in range(1, 8):
        blocks.append((i, 8, PEER_TOPK // (i + 1)))
    blocks.append((None, 8, 8))
    return blocks


def _top_rows(s, order, payload, k):
    big = jnp.float32(3e38)
    vals, pays = [], []
    for _ in range(k):
        m = jnp.max(s, axis=0, keepdims=True)
        o = jnp.min(jnp.where(s == m, order, big), axis=0, keepdims=True)
        hit = order == o
        pays.append(o if payload is order else jnp.min(jnp.where(hit, payload, big), axis=0, keepdims=True))
        vals.append(m)
        s = jnp.where(hit, -jnp.inf, s)
    return jnp.concatenate(vals, axis=0), jnp.concatenate(pays, axis=0)


def _route_kernel(qp_ref, k1_ref, k2_ref, e_ref, g_ref, *, tt):
    qp = qp_ref[...]
    kd = PEER_KEY_DIM
    rows_k = lax.broadcasted_iota(I32, (PEER_KEYS, tt), 0).astype(F32)
    for h in range(PEER_HEADS):
        q1 = qp[:, (2 * h) * kd:(2 * h + 1) * kd]
        q2 = qp[:, (2 * h + 1) * kd:(2 * h + 2) * kd]
        s1 = lax.dot_general(k1_ref[h], q1, (((1,), (1,)), ((), ())), preferred_element_type=F32)
        s2 = lax.dot_general(k2_ref[h], q2, (((1,), (1,)), ((), ())), preferred_element_type=F32)
        v1, i1 = _top_rows(s1, rows_k, rows_k, PEER_TOPK)
        v2, i2 = _top_rows(s2, rows_k, rows_k, PEER_TOPK)
        cs, ce, co = [], [], []
        for i, rows, valid in _cand_layout():
            r = lax.broadcasted_iota(I32, (rows, tt), 0).astype(F32)
            if i is None:
                val = v1[8:16] + v2[0:1]
                eid = i1[8:16] * PEER_KEYS + i2[0:1]
                flat = (r + 8.0) * PEER_TOPK
            else:
                val = v1[i:i + 1] + v2[0:rows]
                eid = i1[i:i + 1] * PEER_KEYS + i2[0:rows]
                flat = r + float(i * PEER_TOPK)
                if valid < rows:
                    val = jnp.where(r < float(valid), val, -jnp.inf)
            cs.append(val)
            ce.append(eid)
            co.append(flat)
        cand = jnp.concatenate(cs, axis=0)
        top_s, top_e = _top_rows(cand, jnp.concatenate(co, axis=0), jnp.concatenate(ce, axis=0), PEER_TOPK)
        p = jnp.exp(top_s - top_s[0:1])
        gates = p / jnp.sum(p, axis=0, keepdims=True)
        e_ref[h * PEER_TOPK:(h + 1) * PEER_TOPK, :] = top_e.astype(I32)
        g_ref[h * PEER_TOPK:(h + 1) * PEER_TOPK, :] = gates


def _route(qp, k1, k2, tt):
    n, nq = qp.shape
    n_sel = PEER_HEADS * PEER_TOPK
    full = lambda a: pl.BlockSpec(a.shape, lambda i: (0,) * a.ndim)
    return pl.pallas_call(
        functools.partial(_route_kernel, tt=tt),
        grid=(n // tt,),
        in_specs=[pl.BlockSpec((tt, nq), lambda i: (i, 0)), full(k1), full(k2)],
        out_specs=[pl.BlockSpec((n_sel, tt), lambda i: (0, i)), pl.BlockSpec((n_sel, tt), lambda i: (0, i))],
        out_shape=[jax.ShapeDtypeStruct((n_sel, n), I32), jax.ShapeDtypeStruct((n_sel, n), F32)],
        compiler_params=_cparams(("arbitrary",)),
        name="route",
    )(qp, k1, k2)


def _final_kernel(h_ref, p_ref, g_ref, o_ref):
    tm, d = h_ref.shape
    n_seg = d // LANES
    p = jnp.concatenate([p_ref[pl.ds(k, tm, stride=n_seg), :] for k in range(n_seg)], axis=1)
    o_ref[...] = _rms(h_ref[...] + p, g_ref[...])


def _final(h, p, g, tm):
    n, d = h.shape
    row = pl.BlockSpec((tm, d), lambda i: (i, 0))
    return pl.pallas_call(
        _final_kernel,
        grid=(n // tm,),
        in_specs=[row, pl.BlockSpec((tm * (d // LANES), LANES), lambda i: (i, 0)),
                  pl.BlockSpec((1, d), lambda i: (0, 0))],
        out_specs=row,
        out_shape=jax.ShapeDtypeStruct((n, d), F32),
        compiler_params=_cparams(("arbitrary",)),
        name="final",
    )(h, p, g.reshape(1, d))


SC_CORES_V7X = 2
SC_SUBCORES_V7X = 16
SC_LANES_V7X = 16
PEER_TOK_BATCH = 32
PEER_ROW_CHUNK = 32
PEER_RING = 4


def _pack_words(x):
    bits = lax.bitcast_convert_type(x.astype(BF16).astype(F32), I32)
    half = bits.shape[1] // 2
    return (bits[:, half:] & jnp.int32(-65536)) | lax.shift_right_logical(bits[:, :half], 16)


def _pack_kernel(t_ref, o_ref):
    o_ref[...] = _pack_words(t_ref[...])


def _pack_bf16_pairs(t, rows=1024):
    e, d = t.shape
    return pl.pallas_call(
        _pack_kernel,
        grid=(e // rows,),
        in_specs=[pl.BlockSpec((rows, d), lambda i: (i, 0))],
        out_specs=pl.BlockSpec((rows, d // 2), lambda i: (i, 0)),
        out_shape=jax.ShapeDtypeStruct((e, d // 2), I32),
        compiler_params=_cparams(("arbitrary",)),
        name="pack_table",
    )(t)


def _unpack_pair(w):
    lo = lax.bitcast_convert_type(jnp.left_shift(w, 16), F32)
    hi = lax.bitcast_convert_type(w & jnp.int32(-65536), F32)
    return lo, hi


def _peer_sc_body(hn_hbm, e_hbm, g_hbm, u_hbm, v_hbm, out_hbm,
                  idx_v, gate_v, x_v, out_v, rows, p_v, act_v, sem, *, tpw, d, n_sel):
    nl = SC_LANES_V7X
    tb = PEER_TOK_BATCH
    rc = PEER_ROW_CHUNK
    n_chunk = n_sel // rc
    jobs_per_tok = 2 * n_chunk
    half = d // 2
    n_lane_blk = half // nl
    xs = half // LANES
    os_ = d // LANES
    wid =lax.axis_index("s") * SC_CORES_V7X + lax.axis_index("c")
    base = wid * tpw
    lane = lax.iota(I32, nl)
    zero = jnp.zeros((nl,), F32)
    c_gelu = 2.0 * math.sqrt(2.0 / math.pi)

    def gather_copy(tab_hbm, job):
        tok = job // jobs_per_tok
        c = (job % jobs_per_tok) % n_chunk
        b = job % PEER_RING
        return pltpu.make_async_copy(tab_hbm.at[idx_v.at[tok, pl.ds(c * rc, rc)]], rows.at[b], sem.at[b])

    def start(job):
        j = job % jobs_per_tok

        @pl.when(j < n_chunk)
        def _():
            gather_copy(u_hbm, job).start()

        @pl.when(j >= n_chunk)
        def _():
            gather_copy(v_hbm, job).start()

    def compute_u(tok, c, b):
        def rg_body(rg, _):
            r0 = rg * 8

            def jbody(j2, accs):
                off0 = j2 * (2 * nl)
                off1 = off0 + nl
                xrow = tok * xs + off0 // LANES
                xl = off0 % LANES
                x0 = plsc.bitcast(x_v[xrow, pl.ds(xl, nl)], BF16)
                x1 = plsc.bitcast(x_v[xrow, pl.ds(xl + nl, nl)], BF16)
                new = []
                for r in range(8):
                    w0 = plsc.bitcast(rows[b, r0 + r, pl.ds(off0, nl)], BF16)
                    w1 = plsc.bitcast(rows[b, r0 + r, pl.ds(off1, nl)], BF16)
                    lo, hi = _unpack_pair(plsc.bitcast(w0 * x0 + w1 * x1, I32))
                    new.append(accs[r] + (lo + hi))
                return tuple(new)

            accs = lax.fori_loop(0, n_lane_blk // 2, jbody, (zero,) * 8)
            for r in range(8):
                p_v[c * rc + r0 + r, :] = accs[r]
            return 0

        lax.fori_loop(0, rc // 8, rg_body, 0)

    def finish_act(tok):
        def eg_body(eg, _):
            e0 = eg * nl
            ridx = e0 + lane
            s = zero
            for l in range(nl):
                s = s + plsc.load_gather(p_v, [ridx, jnp.full((nl,), l, I32)])
            inner = c_gelu * (s + 0.044715 * (s * s * s))
            gl = s / (1.0 + jnp.exp(-inner))
            a = gl * gate_v[tok, pl.ds(e0, nl)]
            bits = lax.bitcast_convert_type(a, I32)
            rnd = bits + jnp.int32(0x7FFF) + (lax.shift_right_logical(bits, 16) & 1)
            hi16 = rnd & jnp.int32(-65536)
            act_v[pl.ds(e0, nl)] = hi16 | lax.shift_right_logical(hi16, 16)
            return 0

        lax.fori_loop(0, n_sel // nl, eg_body, 0)

        def zbody(j, _):
            off = j * nl
            out_v[tok * os_ + off // LANES, pl.ds(off % LANES, nl)] = zero
            return 0

        lax.fori_loop(0, d // nl, zbody, 0, unroll=4)

    def compute_v(tok, c, b):
        def rg_body(rg, _):
            r0 = rg * nl
            splat = [plsc.bitcast(plsc.load_gather(act_v, [jnp.full((nl,), 0, I32) + (c * rc + r0 + r)]), BF16)
                     for r in range(nl)]

            def tree(parts):
                while len(parts) > 1:
                    parts = [parts[i] + parts[i + 1] for i in range(0, len(parts), 2)]
                return parts[0]

            @plsc.parallel_loop(0, n_lane_blk, unroll=2)
            def _(j):
                off = j * nl
                los, his = [], []
                for r in range(0, nl, 2):
                    w0 = plsc.bitcast(rows[b, r0 + r, pl.ds(off, nl)], BF16)
                    w1 = plsc.bitcast(rows[b, r0 + r + 1, pl.ds(off, nl)], BF16)
                    lo, hi = _unpack_pair(plsc.bitcast(w0 * splat[r] + w1 * splat[r + 1], I32))
                    los.append(lo)
                    his.append(hi)
                orow = tok * os_ + off // LANES
                ol = off % LANES
                out_v[orow, pl.ds(ol, nl)] = out_v[orow, pl.ds(ol, nl)] + tree(los)
                out_v[orow + xs, pl.ds(ol, nl)] = out_v[orow + xs, pl.ds(ol, nl)] + tree(his)

            return 0

        lax.fori_loop(0, rc // nl, rg_body, 0)

    def batch_body(bi, _):
        t0 = base + bi * tb
        pltpu.sync_copy(e_hbm.at[pl.ds(t0, tb)], idx_v)
        pltpu.sync_copy(g_hbm.at[pl.ds(t0, tb)], gate_v)
        pltpu.sync_copy(hn_hbm.at[pl.ds(t0 * xs, tb * xs)], x_v)
        for pre in range(PEER_RING - 1):
            start(pre)

        def job_body(job, _):
            @pl.when(job + (PEER_RING - 1) < tb * jobs_per_tok)
            def _():
                start(job + (PEER_RING - 1))

            j = job % jobs_per_tok
            gather_copy(u_hbm, job).wait()
            tok = job // jobs_per_tok
            b = job % PEER_RING

            @pl.when(j < n_chunk)
            def _():
                compute_u(tok, j, b)

            @pl.when(j == n_chunk - 1)
            def _():
                finish_act(tok)

            @pl.when(j >= n_chunk)
            def _():
                compute_v(tok, j - n_chunk, b)

            return 0

        lax.fori_loop(0, tb * jobs_per_tok, job_body, 0)
        pltpu.sync_copy(out_v, out_hbm.at[pl.ds(t0 * os_, tb * os_)])
        return 0

    lax.fori_loop(0, tpw // tb, batch_body, 0)


def _peer_sc(x_pk, experts, gates, u_tab, v_tab):
    d = 2 * u_tab.shape[1]
    n = x_pk.shape[0] * LANES // (d // 2)
    n_sel = experts.shape[1]
    nw = SC_CORES_V7X * SC_SUBCORES_V7X
    tpw = n // nw
    mesh = plsc.VectorSubcoreMesh(core_axis_name="c", subcore_axis_name="s",
                                  num_cores=SC_CORES_V7X, num_subcores=SC_SUBCORES_V7X)
    body = functools.partial(_peer_sc_body, tpw=tpw, d=d, n_sel=n_sel)
    call = pl.kernel(
        body,
        out_type=jax.ShapeDtypeStruct((n * d // LANES, LANES), F32),
        mesh=mesh,
        scratch_types=[pltpu.VMEM((PEER_TOK_BATCH, n_sel), I32),
                       pltpu.VMEM((PEER_TOK_BATCH, n_sel), F32),
                       pltpu.VMEM((PEER_TOK_BATCH * d // 2 // LANES, LANES), I32),
                       pltpu.VMEM((PEER_TOK_BATCH * d // LANES, LANES), F32),
                       pltpu.VMEM((PEER_RING, PEER_ROW_CHUNK, d // 2), I32),
                       pltpu.VMEM((n_sel, SC_LANES_V7X), F32),
                       pltpu.VMEM((n_sel,), I32),
                       pltpu.SemaphoreType.DMA((PEER_RING,))],
        compiler_params=pltpu.CompilerParams(needs_layout_passes=False, use_tc_tiling_on_sc=False),
        name="peer_sc",
    )
    return call(x_pk, experts, gates, u_tab, v_tab)


def kernel(x, norm1_g, w_in, a_re, a_im, log_dt, b_re, b_im, c_re, c_im, d_skip, w_glu, w_ssm_up, w_attn_up,
           w_out, norm2_g, peer_wq, peer_k1, peer_k2, peer_u, peer_v, norm_f_g):
    bsz, seq, d = x.shape
    depth = norm1_g.shape[0]
    chunks = _time_chunks(seq)
    h = x
    for layer in range(depth):
        last = layer + 1 == depth
        kv_w, main_w = _in_weights(w_in[layer], seq)
        s5p = _s5_params(a_re[layer], a_im[layer], log_dt[layer], b_re[layer], b_im[layer], c_re[layer],
                         c_im[layer], d_skip[layer], w_glu[layer], nb=bsz)
        n_state = s5p[4].shape[1]
        wsu = w_ssm_up[layer].astype(BF16)
        wau = w_attn_up[layer].astype(BF16)
        wo = w_out[layer].astype(BF16)
        wq = peer_wq[layer].astype(BF16)
        k1 = peer_k1[layer].astype(BF16)
        k2 = peer_k2[layer].astype(BF16)
        u_pk = _pack_bf16_pairs(peer_u[layer])
        v_pk = _pack_bf16_pairs(peer_v[layer])
        kv_tm = min(ROW_TILE, seq)
        head_len = 2 * DSA_KT
        split_kv = len(chunks) > 1 and chunks[0][1] <= head_len < seq
        k4, vt, ki = _kvproj(h, norm1_g[layer], kv_w, norm1_g[layer], seq=head_len if split_kv else seq,
                             tm=kv_tm, kt=DSA_KT)
        st_re = jnp.zeros((bsz, n_state), F32)
        st_im = jnp.zeros((bsz, n_state), F32)
        outs = []
        routed = ki
        peer_outs = []
        for c, (s0, sc) in enumerate(chunks):
            tm = math.gcd(math.gcd(s0, sc), ROW_TILE)
            if split_kv and s0 + sc > head_len:
                k4, vt, ki = _kvproj(h, norm1_g[layer], kv_w, routed, seq=seq, tm=kv_tm, kt=DSA_KT)
                split_kv = False
            after = (routed, peer_outs[c - SC_LAG] if c >= SC_LAG else ki)
            u, q, qi, wit, gs, ga = _inproj(h, norm1_g[layer], main_w, after, s0=s0, sc=sc, tm=tm)
            d_ssm = u.shape[-1]
            u_tb = u.transpose(1, 0, 2).reshape(sc * bsz, d_ssm)
            y_tb, st_re, st_im = _s5(u_tb, st_re, st_im, s5p, nb=bsz, tc=S5_STEPS)
            ys = y_tb.reshape(sc, bsz, d_ssm).transpose(1, 0, 2)
            ya = _dsa(q, qi, wit, ki, k4, vt, s0=s0, seq_total=seq, tq=DSA_TQ, kt=DSA_KT)
            hm, x_pk, qp = _merge(h, ys, ya, gs, ga, wsu, wau, wo, norm2_g[layer], wq, s0=s0, tm=tm)
            nt = bsz * sc
            e_t, g_t = _route(qp.reshape(nt, -1), k1, k2, tt=ROUTE_TOKENS)
            routed = e_t
            po = _peer_sc(x_pk.reshape(-1, LANES), e_t.T, g_t.T, u_pk, v_pk)
            peer_outs.append(po)
            hm2 = hm.reshape(nt, d)
            o = _final(hm2, po, norm_f_g, tm=tm) if last else hm2 + po.reshape(nt, d)
            outs.append(o.reshape(bsz, sc, d))
        h = jnp.concatenate(outs, axis=1)
    return h
```

```python
import functools
import math

import numpy as np
import jax
import jax.numpy as jnp
from jax import lax
from jax.experimental import pallas as pl
from jax.experimental.pallas import tpu as pltpu
from jax.experimental.pallas import tpu_sc as plsc

F32 = jnp.float32
BF16 = jnp.bfloat16
I32 = jnp.int32

SSM_GROUP = 16
SSM_STATE = 64
ATTN_HEADS = 8
ATTN_KV_HEADS = 2
HEAD_DIM = 64
IDX_HEADS = 8
IDX_DIM = 32
TOPK_MAX = 256
ROPE_THETA = 10000.0
NEG_BIG = -1e30
PEER_HEADS = 8
PEER_KEYS = 128
PEER_KEY_DIM = 128
PEER_TOPK = 16
NORM_EPS = 1e-6

TIME_SPLIT_32NDS = (1, 3, 4, 4, 4, 4, 4, 3, 2, 2, 1)
SC_LAG = 3
ROW_TILE = 512
S5_STEPS = 64
ROUTE_TOKENS = 256
DSA_TQ = 128
DSA_KT = 256
LANES = 128
INT_MIN = -(2 ** 31)
VMEM_LIMIT = 56 * 1024 * 1024


def _time_chunks(seq):
    unit = seq // 32
    if seq % 32 == 0 and unit % DSA_TQ == 0:
        sizes = [f * unit for f in TIME_SPLIT_32NDS]
    else:
        step = min(ROW_TILE, seq)
        sizes = [step] * (seq // step)
    assert sum(sizes) == seq
    starts = np.cumsum([0] + sizes[:-1]).tolist()
    return list(zip(starts, sizes))


def _cparams(sem):
    return pltpu.CompilerParams(dimension_semantics=sem, vmem_limit_bytes=VMEM_LIMIT)


def _gelu_tanh(x):
    return 0.5 * x * (1.0 + jnp.tanh(math.sqrt(2.0 / math.pi) * (x + 0.044715 * (x * x * x))))


def _sigmoid(x):
    return 1.0 / (1.0 + jnp.exp(-x))


def _rms(x, g):
    return x * lax.rsqrt(jnp.mean(x * x, axis=-1, keepdims=True) + NORM_EPS) * g


def _rot_cols(w, hd):
    d, n = w.shape
    w3 = w.reshape(d, n // hd, hd)
    half = hd // 2
    return jnp.concatenate([-w3[..., half:], w3[..., :half]], axis=-1).reshape(d, n)


def _rope_full(seq, hd, heads):
    pos = jnp.arange(seq, dtype=F32)
    inv = ROPE_THETA ** (-jnp.arange(0, hd, 2, dtype=F32) / hd)
    ang = pos[:, None] * inv[None, :]
    c = jnp.concatenate([jnp.cos(ang), jnp.cos(ang)], axis=-1)
    s = jnp.concatenate([jnp.sin(ang), jnp.sin(ang)], axis=-1)
    return jnp.tile(c, (1, heads)), jnp.tile(s, (1, heads))


def _in_weights(w_in, seq):
    d = w_in.shape[0]
    d_ssm = d // 2
    d_q = ATTN_HEADS * HEAD_DIM
    d_kv = ATTN_KV_HEADS * HEAD_DIM
    d_qi = IDX_HEADS * IDX_DIM
    splits = (d_ssm, d_q, d_kv, d_kv, d_qi, IDX_DIM, IDX_HEADS, d, d)
    offs = np.cumsum(splits)[:-1].tolist()
    wu, wq, wk, wv, wqi, wki, wwi, wgs, wga = jnp.split(w_in, offs, axis=1)
    pad = jnp.zeros((d, 128 - IDX_DIM), F32)
    cq, sq = _rope_full(seq, HEAD_DIM, ATTN_HEADS)
    ck, sk = _rope_full(seq, HEAD_DIM, ATTN_KV_HEADS)
    cqi, sqi = _rope_full(seq, IDX_DIM, IDX_HEADS)
    cki, ski = _rope_full(seq, IDX_DIM, 1)
    tpad = jnp.zeros((seq, 128 - IDX_DIM), F32)
    kv = dict(
        w=jnp.concatenate([wk, wki, pad], axis=1).astype(BF16),
        wvt=wv.T.astype(BF16),
        wr=jnp.concatenate([_rot_cols(wk, HEAD_DIM), _rot_cols(wki, IDX_DIM), pad], axis=1).astype(BF16),
        cs=jnp.concatenate([ck, cki, tpad], axis=1), sn=jnp.concatenate([sk, ski, tpad], axis=1))
    main = dict(
        w=jnp.concatenate([wu, wq, wqi, wgs, wga], axis=1).astype(BF16),
        wr=jnp.concatenate([_rot_cols(wq, HEAD_DIM), _rot_cols(wqi, IDX_DIM)], axis=1).astype(BF16),
        cs=jnp.concatenate([cq, cqi], axis=1), sn=jnp.concatenate([sq, sqi], axis=1),
        wwit=wwi.T.astype(BF16))
    return kv, main


def _kvproj_kernel(x_ref, g_ref, w_ref, wr_ref, wvt_ref, cs_ref, sn_ref, after_ref, k_ref, vt_ref, ki_ref,
                   *, d_kv, kt):
    del after_ref
    xb = _rms(x_ref[0], g_ref[...]).astype(BF16)

    def mm(ref, lo, n):
        return jnp.dot(xb, ref[:, lo:lo + n], preferred_element_type=F32)

    k = mm(w_ref, 0, d_kv) * cs_ref[:, :d_kv] + mm(wr_ref, 0, d_kv) * sn_ref[:, :d_kv]
    for n in range(ATTN_KV_HEADS):
        k_ref[0, n] = k[:, n * HEAD_DIM:(n + 1) * HEAD_DIM].astype(BF16)
    vt = lax.dot_general(wvt_ref[...], xb, (((1,), (1,)), ((), ())), preferred_element_type=F32)
    for n in range(ATTN_KV_HEADS):
        for j in range(vt.shape[1] // kt):
            vt_ref[0, n, j] = vt[n * HEAD_DIM:(n + 1) * HEAD_DIM, j * kt:(j + 1) * kt].astype(BF16)
    kiw = (mm(w_ref, d_kv, 128) * cs_ref[:, d_kv:d_kv + 128]
           + mm(wr_ref, d_kv, 128) * sn_ref[:, d_kv:d_kv + 128])
    ki_ref[0] = kiw[:, :IDX_DIM].astype(BF16)


def _kvproj(x, norm_g, kv, after, seq, tm, kt):
    bsz, _, d = x.shape
    d_kv = ATTN_KV_HEADS * HEAD_DIM
    full = lambda a: pl.BlockSpec(a.shape, lambda s, b: (0,) * a.ndim)
    g = norm_g.reshape(1, d)
    ncs = kv["cs"].shape[1]
    return pl.pallas_call(
        functools.partial(_kvproj_kernel, d_kv=d_kv, kt=kt),
        grid=(seq // tm, bsz),
        in_specs=[pl.BlockSpec((1, tm, d), lambda s, b: (b, s, 0)), full(g), full(kv["w"]), full(kv["wr"]),
                  full(kv["wvt"]),
                  pl.BlockSpec((tm, ncs), lambda s, b: (s, 0)), pl.BlockSpec((tm, ncs), lambda s, b: (s, 0)),
                  pl.BlockSpec(memory_space=pl.ANY)],
        out_specs=[pl.BlockSpec((1, ATTN_KV_HEADS, tm, HEAD_DIM), lambda s, b: (b, 0, s, 0)),
                   pl.BlockSpec((1, ATTN_KV_HEADS, tm // kt, HEAD_DIM, kt), lambda s, b: (b, 0, s, 0, 0)),
                   pl.BlockSpec((1, tm, IDX_DIM), lambda s, b: (b, s, 0))],
        out_shape=[jax.ShapeDtypeStruct((bsz, ATTN_KV_HEADS, seq, HEAD_DIM), BF16),
                   jax.ShapeDtypeStruct((bsz, ATTN_KV_HEADS, seq // kt, HEAD_DIM, kt), BF16),
                   jax.ShapeDtypeStruct((bsz, seq, IDX_DIM), BF16)],
        compiler_params=_cparams(("arbitrary", "arbitrary")),
        name="kvproj",
    )(x, g, kv["w"], kv["wr"], kv["wvt"], kv["cs"], kv["sn"], after)


def _inproj_kernel(x_ref, g_ref, w_ref, wr_ref, cs_ref, sn_ref, wwit_ref, after_tc_ref, after_sc_ref,
                   u_ref, q_ref, qi_ref, wit_ref, gs_ref, ga_ref, *, d_ssm, d_q, d_qi, d_model, q_scale, wi_scale):
    del after_tc_ref, after_sc_ref
    xb = _rms(x_ref[0], g_ref[...]).astype(BF16)

    def mm(ref, lo, n):
        return jnp.dot(xb, ref[:, lo:lo + n], preferred_element_type=F32)

    o = 0
    u_ref[0] = mm(w_ref, o, d_ssm).astype(BF16)
    o += d_ssm
    q = mm(w_ref, o, d_q) * cs_ref[:, :d_q] + mm(wr_ref, 0, d_q) * sn_ref[:, :d_q]
    q_ref[0] = (q * q_scale).astype(BF16)
    o += d_q
    qi = mm(w_ref, o, d_qi) * cs_ref[:, d_q:d_q + d_qi] + mm(wr_ref, d_q, d_qi) * sn_ref[:, d_q:d_q + d_qi]
    qi_ref[0] = qi.astype(BF16)
    o += d_qi
    gs_ref[0] = _sigmoid(mm(w_ref, o, d_model)).astype(BF16)
    o += d_model
    ga_ref[0] = _sigmoid(mm(w_ref, o, d_model)).astype(BF16)
    wit_ref[0] = lax.dot_general(wwit_ref[...], xb, (((1,), (1,)), ((), ())),
                                 preferred_element_type=F32) * wi_scale


def _inproj(x, norm_g, main, after, s0, sc, tm):
    bsz, _, d = x.shape
    d_ssm = d // 2
    d_q = ATTN_HEADS * HEAD_DIM
    d_qi = IDX_HEADS * IDX_DIM
    i0 = s0 // tm
    kern = functools.partial(
        _inproj_kernel, d_ssm=d_ssm, d_q=d_q, d_qi=d_qi, d_model=d,
        q_scale=HEAD_DIM ** -0.5 * math.log2(math.e), wi_scale=(IDX_HEADS ** -0.5) * (IDX_DIM ** -0.5))
    tok = lambda n: pl.BlockSpec((1, tm, n), lambda s, b: (b, s, 0))
    full = lambda a: pl.BlockSpec(a.shape, lambda s, b: (0,) * a.ndim)
    g = norm_g.reshape(1, d)
    ncs = main["cs"].shape[1]
    outs = [(d_ssm, BF16), (d_q, BF16), (d_qi, BF16)]
    return pl.pallas_call(
        kern,
        grid=(sc // tm, bsz),
        in_specs=[pl.BlockSpec((1, tm, d), lambda s, b: (b, s + i0, 0)), full(g), full(main["w"]), full(main["wr"]),
                  pl.BlockSpec((tm, ncs), lambda s, b: (s + i0, 0)),
                  pl.BlockSpec((tm, ncs), lambda s, b: (s + i0, 0)), full(main["wwit"]),
                  pl.BlockSpec(memory_space=pl.ANY), pl.BlockSpec(memory_space=pl.ANY)],
        out_specs=[tok(n) for n, _ in outs] + [pl.BlockSpec((1, IDX_HEADS, tm), lambda s, b: (b, 0, s)),
                                                tok(d), tok(d)],
        out_shape=[jax.ShapeDtypeStruct((bsz, sc, n), dt) for n, dt in outs]
        + [jax.ShapeDtypeStruct((bsz, IDX_HEADS, sc), F32),
           jax.ShapeDtypeStruct((bsz, sc, d), BF16), jax.ShapeDtypeStruct((bsz, sc, d), BF16)],
        compiler_params=_cparams(("arbitrary", "arbitrary")),
        name="inproj",
    )(x, g, main["w"], main["wr"], main["cs"], main["sn"], main["wwit"], *after)


def _s5_kernel(u_ref, sre_in, sim_in, bre_ref, bim_ref, cre_ref, cim_ref, are_ref, aim_ref, dsk_ref, wglu_ref,
               y_ref, st_re, st_im, sre, sim, *, tc, nb, lane_chunk):
    @pl.when(pl.program_id(0) == 0)
    def _():
        st_re[...] = sre_in[...]
        st_im[...] = sim_in[...]

    u = u_ref[...]
    n_half = bre_ref.shape[0]
    hin = bre_ref.shape[1]
    hst = bre_ref.shape[2]
    for h in range(n_half):
        uh = u[:, h * hin:(h + 1) * hin]
        sre[:, h * hst:(h + 1) * hst] = jnp.dot(uh, bre_ref[h], preferred_element_type=F32)
        sim[:, h * hst:(h + 1) * hst] = jnp.dot(uh, bim_ref[h], preferred_element_type=F32)

    n_state = sre.shape[1]
    for c in range(n_state // lane_chunk):
        cols = slice(c * lane_chunk, (c + 1) * lane_chunk)
        ar = are_ref[:, cols]
        ai = aim_ref[:, cols]

        def step(t, carry, cols=cols, ar=ar, ai=ai):
            sr, si = carry
            r0 = pl.multiple_of(t * nb, nb)
            nr = ar * sr - ai * si + sre[pl.ds(r0, nb), cols]
            ni = ar * si + ai * sr + sim[pl.ds(r0, nb), cols]
            sre[pl.ds(r0, nb), cols] = nr
            sim[pl.ds(r0, nb), cols] = ni
            return nr, ni

        sr, si = lax.fori_loop(0, tc, step, (st_re[:, cols], st_im[:, cols]), unroll=4)
        st_re[:, cols] = sr
        st_im[:, cols] = si

    ys = []
    for h in range(n_half):
        srh = sre[:, h * hst:(h + 1) * hst].astype(BF16)
        sih = sim[:, h * hst:(h + 1) * hst].astype(BF16)
        ys.append(jnp.dot(srh, cre_ref[h], preferred_element_type=F32)
                  - jnp.dot(sih, cim_ref[h], preferred_element_type=F32))
    y = jnp.concatenate(ys, axis=-1) + dsk_ref[...] * u.astype(F32)
    y = _gelu_tanh(y)
    gate = jnp.dot(y.astype(BF16), wglu_ref[...], preferred_element_type=F32)
    y_ref[...] = (y * _sigmoid(gate)).astype(BF16)


def _s5_params(a_re, a_im, log_dt, b_re, b_im, c_re, c_im, d_skip, w_glu, nb):
    groups = a_re.shape[0]
    d_ssm = groups * SSM_GROUP
    n_state = groups * SSM_STATE
    lam = lax.complex(a_re, a_im)
    dt = jnp.exp(log_dt)[:, None]
    a_bar = jnp.exp(lam * dt)
    b_bar = ((a_bar - 1.0) / lam)[..., None] * lax.complex(b_re, b_im)
    gh = min(groups, 256 // SSM_GROUP)
    n_half = groups // gh
    eye = jnp.eye(gh, dtype=F32)

    def bmat(bb):
        b4 = bb.reshape(n_half, gh, SSM_STATE, SSM_GROUP)
        return jnp.einsum('hgpc,gk->hgckp', b4, eye).reshape(n_half, gh * SSM_GROUP, gh * SSM_STATE)

    def cmat(cc):
        c4 = cc.reshape(n_half, gh, SSM_GROUP, SSM_STATE)
        return jnp.einsum('hgcp,gk->hgpkc', c4, eye).reshape(n_half, gh * SSM_STATE, gh * SSM_GROUP)

    return (bmat(jnp.real(b_bar)).astype(BF16), bmat(jnp.imag(b_bar)).astype(BF16),
            cmat(c_re).astype(BF16), cmat(c_im).astype(BF16),
            jnp.broadcast_to(jnp.real(a_bar).reshape(1, n_state), (nb, n_state)),
            jnp.broadcast_to(jnp.imag(a_bar).reshape(1, n_state), (nb, n_state)),
            d_skip.reshape(1, d_ssm), w_glu.astype(BF16))


def _s5(u_tb, st_re, st_im, params, nb, tc):
    rows, d_ssm = u_tb.shape
    n_state = st_re.shape[1]
    blk = tc * nb
    full = lambda a: pl.BlockSpec(a.shape, lambda i: (0,) * a.ndim)
    st_spec = pl.BlockSpec((nb, n_state), lambda i: (0, 0))
    kern = functools.partial(_s5_kernel, tc=tc, nb=nb, lane_chunk=512)
    return pl.pallas_call(
        kern,
        grid=(rows // blk,),
        in_specs=[pl.BlockSpec((blk, d_ssm), lambda i: (i, 0)), st_spec, st_spec] + [full(p) for p in params],
        out_specs=[pl.BlockSpec((blk, d_ssm), lambda i: (i, 0)), st_spec, st_spec],
        out_shape=[jax.ShapeDtypeStruct((rows, d_ssm), BF16),
                   jax.ShapeDtypeStruct((nb, n_state), F32), jax.ShapeDtypeStruct((nb, n_state), F32)],
        scratch_shapes=[pltpu.VMEM((blk, n_state), F32), pltpu.VMEM((blk, n_state), F32)],
        compiler_params=_cparams(("arbitrary",)),
        name="s5",
    )(u_tb, st_re, st_im, *params)


PART_ROWS = 32


def _dsa_kernel(qi_ref, wit_ref, q_ref, ki_ref, k_ref, vt_ref, o_ref, key_s, bias_s, lg_s,
                *, qb0, tq, kt, sub, topk, seq_bits):
    qb = pl.program_id(1) + qb0
    nkt = ((qb * tq + tq + sub * kt - 1) // (sub * kt)) * sub
    q_pos = qb * tq + lax.broadcasted_iota(I32, (1, tq), 1)
    k_eff = jnp.minimum(topk, q_pos + 1).astype(F32)

    qi = qi_ref[0]
    wit = wit_ref[0]
    qipair = [jnp.concatenate([qi[:, (2 * hp) * IDX_DIM:(2 * hp + 1) * IDX_DIM],
                               qi[:, (2 * hp + 1) * IDX_DIM:(2 * hp + 2) * IDX_DIM]], axis=0)
              for hp in range(IDX_HEADS // 2)]

    def key_pos(t):
        return t * kt + lax.broadcasted_iota(I32, (kt, tq), 0)

    def score_tile(t2, _):
        for hf in range(sub):
            r0 = pl.multiple_of((t2 * sub + hf) * kt, kt)
            ki_t = ki_ref[0, pl.ds(r0, kt), :]
            for hp in range(IDX_HEADS // 2):
                lg_s[hf, hp] = lax.dot_general(ki_t, qipair[hp], (((1,), (1,)), ((), ())),
                                               preferred_element_type=F32)
        for hf in range(sub):
            t = t2 * sub + hf
            r0 = pl.multiple_of(t * kt, kt)
            sc = jnp.zeros((kt, tq), F32)
            for hp in range(IDX_HEADS // 2):
                rel = lg_s[hf, hp]
                sc = sc + jnp.maximum(rel[:, :tq], 0.0) * wit[2 * hp:2 * hp + 1, :]
                sc = sc + jnp.maximum(rel[:, tq:], 0.0) * wit[2 * hp + 1:2 * hp + 2, :]
            bits = lax.bitcast_convert_type(sc, I32)
            key = jnp.where(bits < 0, bits ^ jnp.int32(0x7FFFFFFF), bits)
            key = jnp.where(key_pos(t) <= q_pos, key, jnp.int32(INT_MIN))
            key_s[pl.ds(r0, kt), :] = key
        return 0

    lax.fori_loop(0, nkt // sub, score_tile, 0)

    def count(pred_fn):
        def body(t2, acc):
            for hf in range(sub):
                t = t2 * sub + hf
                r0 = pl.multiple_of(t * kt, kt)
                m = pred_fn(key_s[pl.ds(r0, kt), :], t)
                ones = jnp.where(m, 1.0, 0.0).reshape(kt // PART_ROWS, PART_ROWS, tq)
                acc = acc + jnp.sum(ones, axis=0)
            return acc
        acc = lax.fori_loop(0, nkt // sub, body, jnp.zeros((PART_ROWS, tq), F32))
        return jnp.sum(acc, axis=0, keepdims=True)

    def bit_step(i, carry):
        u, cnt_u = carry
        bit = jnp.left_shift(jnp.int32(1), 31 - i)
        cand_u = u | bit
        cand_s = cand_u ^ jnp.int32(INT_MIN)
        cnt = count(lambda kk, t: kk >= cand_s)
        ok = cnt >= k_eff
        return jnp.where(ok, cand_u, u), jnp.where(ok, cnt, cnt_u)

    all_keys = (nkt * kt).astype(F32)
    u_thr, cnt_ge = lax.fori_loop(0, 32, bit_step, (jnp.zeros((1, tq), I32), jnp.zeros((1, tq), F32) + all_keys))
    thr = u_thr ^ jnp.int32(INT_MIN)
    has_tie = jnp.max(cnt_ge - k_eff) > 0.0

    def tie_cut():
        need_eq = k_eff - count(lambda kk, t: kk > thr)

        def pos_step(i, c):
            bit = jnp.left_shift(jnp.int32(1), seq_bits - 1 - i)
            cand = c | bit
            cnt = count(lambda kk, t: (kk == thr) & (key_pos(t) < cand))
            return jnp.where(cnt < need_eq, cand, c)
        return lax.fori_loop(0, seq_bits, pos_step, jnp.zeros((1, tq), I32))

    cut = lax.cond(has_tie, tie_cut, lambda: jnp.full((1, tq), 2 ** seq_bits, I32))

    def bias_tile(t, _):
        r0 = pl.multiple_of(t * kt, kt)
        key = key_s[pl.ds(r0, kt), :]
        sel = (key > thr) | ((key == thr) & (key_pos(t) <= cut))
        bias_s[pl.ds(r0, kt), :] = jnp.where(sel, 0.0, NEG_BIG)
        return 0

    lax.fori_loop(0, nkt, bias_tile, 0)

    q = q_ref[0]
    grp = ATTN_HEADS // ATTN_KV_HEADS
    pairs_per_kv = grp // 2
    n_unit = ATTN_KV_HEADS * pairs_per_kv
    wq = 2 * tq
    qpair = [jnp.concatenate([q[:, (2 * u) * HEAD_DIM:(2 * u + 1) * HEAD_DIM],
                              q[:, (2 * u + 1) * HEAD_DIM:(2 * u + 2) * HEAD_DIM]], axis=0)
             for u in range(n_unit)]

    def col_reduce(x, op):
        part = op(x.reshape(kt // PART_ROWS, PART_ROWS, wq), axis=0)
        return op(part, axis=0, keepdims=True)

    def attn_tile(t, carry):
        ms, ls, accs = list(carry[0]), list(carry[1]), list(carry[2])
        for hf in range(sub):
            r0 = pl.multiple_of((t * sub + hf) * kt, kt)
            bias = bias_s[pl.ds(r0, kt), :]
            bias2 = jnp.concatenate([bias, bias], axis=1)
            for u in range(n_unit):
                k_t = k_ref[0, u // pairs_per_kv, pl.ds(r0, kt), :]
                lg_s[hf, u] = lax.dot_general(k_t, qpair[u], (((1,), (1,)), ((), ())),
                                              preferred_element_type=F32) + bias2
        for hf in range(sub):
            for u in range(n_unit):
                lg = lg_s[hf, u]
                m_new = jnp.maximum(ms[u], col_reduce(lg, jnp.max))
                p = jnp.exp2(lg - m_new)
                alpha = jnp.exp2(ms[u] - m_new)
                v_t = vt_ref[0, u // pairs_per_kv, t * sub + hf]
                ls[u] = alpha * ls[u] + col_reduce(p, jnp.sum)
                accs[u] = alpha * accs[u] + jnp.dot(v_t, p.astype(BF16), preferred_element_type=F32)
                ms[u] = m_new
        return tuple(ms), tuple(ls), tuple(accs)

    init = (tuple(jnp.full((1, wq), NEG_BIG, F32) for _ in range(n_unit)),
            tuple(jnp.zeros((1, wq), F32) for _ in range(n_unit)),
            tuple(jnp.zeros((HEAD_DIM, wq), F32) for _ in range(n_unit)))
    _, ls, accs = lax.fori_loop(0, nkt // sub, attn_tile, init)
    for n in range(ATTN_KV_HEADS):
        o_ref[0, 0, n] = jnp.concatenate([accs[n * pairs_per_kv + pg] / ls[n * pairs_per_kv + pg]
                                          for pg in range(pairs_per_kv)], axis=1).astype(BF16)


def _dsa(q, qi, wit, ki, k4, vt, s0, seq_total, tq, kt):
    bsz, sc, _ = q.shape
    seq = ki.shape[1]
    topk = min(TOPK_MAX, seq_total // 4)
    nqb = sc // tq
    grp = ATTN_HEADS // ATTN_KV_HEADS
    seq_bits = int(math.log2(seq))
    assert 2 ** seq_bits == seq
    sub = 2 if seq % (2 * kt) == 0 else 1
    n_unit = ATTN_HEADS // 2
    assert IDX_HEADS // 2 <= n_unit
    kern =functools.partial(_dsa_kernel, qb0=s0 // tq, tq=tq, kt=kt, sub=sub, topk=topk, seq_bits=seq_bits)
    o_t = pl.pallas_call(
        kern,
        grid=(bsz, nqb),
        in_specs=[pl.BlockSpec((1, tq, IDX_HEADS * IDX_DIM), lambda b, j: (b, j, 0)),
                  pl.BlockSpec((1, IDX_HEADS, tq), lambda b, j: (b, 0, j)),
                  pl.BlockSpec((1, tq, ATTN_HEADS * HEAD_DIM), lambda b, j: (b, j, 0)),
                  pl.BlockSpec((1, seq, IDX_DIM), lambda b, j: (b, 0, 0)),
                  pl.BlockSpec((1, ATTN_KV_HEADS, seq, HEAD_DIM), lambda b, j: (b, 0, 0, 0)),
                  pl.BlockSpec((1, ATTN_KV_HEADS, seq // kt, HEAD_DIM, kt), lambda b, j: (b, 0, 0, 0, 0))],
        out_specs=pl.BlockSpec((1, 1, ATTN_KV_HEADS, HEAD_DIM, grp * tq), lambda b, j: (b, j, 0, 0, 0)),
        out_shape=jax.ShapeDtypeStruct((bsz, nqb, ATTN_KV_HEADS, HEAD_DIM, grp * tq), BF16),
        scratch_shapes=[pltpu.VMEM((seq, tq), I32), pltpu.VMEM((seq, tq), F32),
                        pltpu.VMEM((sub, n_unit, kt, 2 * tq), F32)],
        compiler_params=_cparams(("arbitrary", "arbitrary")),
        name="dsa",
    )(qi, wit, q, ki, k4, vt)
    o = o_t.reshape(bsz, nqb, ATTN_KV_HEADS, HEAD_DIM, grp, tq).transpose(0, 1, 5, 2, 4, 3)
    return o.reshape(bsz, sc, ATTN_HEADS * HEAD_DIM)


def _merge_kernel(x_ref, ys_ref, ya_ref, gs_ref, ga_ref, wsu_ref, wau_ref, wout_ref, g2_ref, wq_ref,
                  h_ref, hn_ref, qp_ref):
    ms = jnp.dot(ys_ref[0], wsu_ref[...], preferred_element_type=F32)
    ma = jnp.dot(ya_ref[0], wau_ref[...], preferred_element_type=F32)
    merged = gs_ref[0].astype(F32) * ms + ga_ref[0].astype(F32) * ma
    h = x_ref[0] + jnp.dot(merged.astype(BF16), wout_ref[...], preferred_element_type=F32)
    h_ref[0] = h
    hb = _rms(h, g2_ref[...]).astype(BF16)
    words = _pack_words(hb)
    half = words.shape[1]
    n_seg = half // LANES
    tm = words.shape[0]
    for p in range(n_seg):
        hn_ref[0, pl.ds(p, tm, stride=n_seg), :] = words[:, p * LANES:(p + 1) * LANES]
    qp_ref[0] = jnp.dot(hb, wq_ref[...], preferred_element_type=F32).astype(BF16)


def _merge(x, ys, ya, gs, ga, wsu, wau, wo, norm2_g, wq, s0, tm):
    bsz, sc, _ = ya.shape
    d = x.shape[2]
    i0 = s0 // tm
    loc = lambda a: pl.BlockSpec((1, tm, a.shape[2]), lambda b, i: (b, i, 0))
    full = lambda a: pl.BlockSpec(a.shape, lambda b, i: (0,) * a.ndim)
    g2 = norm2_g.reshape(1, d)
    nq = wq.shape[1]
    n_seg = d // 2 // LANES
    out = lambda n: pl.BlockSpec((1, tm, n), lambda b, i: (b, i, 0))
    return pl.pallas_call(
        _merge_kernel,
        grid=(bsz, sc // tm),
        in_specs=[pl.BlockSpec((1, tm, d), lambda b, i: (b, i + i0, 0)), loc(ys), loc(ya), loc(gs), loc(ga),
                  full(wsu), full(wau), full(wo), full(g2), full(wq)],
        out_specs=[out(d), pl.BlockSpec((1, tm * n_seg, LANES), lambda b, i: (b, i, 0)), out(nq)],
        out_shape=[jax.ShapeDtypeStruct((bsz, sc, d), F32), jax.ShapeDtypeStruct((bsz, sc * n_seg, LANES), I32),
                   jax.ShapeDtypeStruct((bsz, sc, nq), BF16)],
        compiler_params=_cparams(("arbitrary", "arbitrary")),
        name="merge",
    )(x, ys, ya, gs, ga, wsu, wau, wo, g2, wq)


def _cand_layout():
    blocks = []
    blocks.append((0, 16, 16))
    for i in range(1, 8):
        blocks.append((i, 8, PEER_TOPK // (i + 1)))
    blocks.append((None, 8, 8))
    return blocks


def _top_rows(s, order, payload, k):
    big = jnp.float32(3e38)
    vals, pays = [], []
    for _ in range(k):
        m = jnp.max(s, axis=0, keepdims=True)
        o = jnp.min(jnp.where(s == m, order, big), axis=0, keepdims=True)
        hit = order == o
        pays.append(o if payload is order else jnp.min(jnp.where(hit, payload, big), axis=0, keepdims=True))
        vals.append(m)
        s = jnp.where(hit, -jnp.inf, s)
    return jnp.concatenate(vals, axis=0), jnp.concatenate(pays, axis=0)


def _route_kernel(qp_ref, k1_ref, k2_ref, e_ref, g_ref, *, tt):
    qp = qp_ref[...]
    kd = PEER_KEY_DIM
    rows_k = lax.broadcasted_iota(I32, (PEER_KEYS, tt), 0).astype(F32)
    for h in range(PEER_HEADS):
        q1 = qp[:, (2 * h) * kd:(2 * h + 1) * kd]
        q2 = qp[:, (2 * h + 1) * kd:(2 * h + 2) * kd]
        s1 = lax.dot_general(k1_ref[h], q1, (((1,), (1,)), ((), ())), preferred_element_type=F32)
        s2 = lax.dot_general(k2_ref[h], q2, (((1,), (1,)), ((), ())), preferred_element_type=F32)
        v1, i1 = _top_rows(s1, rows_k, rows_k, PEER_TOPK)
        v2, i2 = _top_rows(s2, rows_k, rows_k, PEER_TOPK)
        cs, ce, co = [], [], []
        for i, rows, valid in _cand_layout():
            r = lax.broadcasted_iota(I32, (rows, tt), 0).astype(F32)
            if i is None:
                val = v1[8:16] + v2[0:1]
                eid = i1[8:16] * PEER_KEYS + i2[0:1]
                flat = (r + 8.0) * PEER_TOPK
            else:
                val = v1[i:i + 1] + v2[0:rows]
                eid = i1[i:i + 1] * PEER_KEYS + i2[0:rows]
                flat = r + float(i * PEER_TOPK)
                if valid < rows:
                    val = jnp.where(r < float(valid), val, -jnp.inf)
            cs.append(val)
            ce.append(eid)
            co.append(flat)
        cand = jnp.concatenate(cs, axis=0)
        n_exp = float(PEER_KEYS * PEER_KEYS)
        combo = jnp.concatenate(co, axis=0) * n_exp + jnp.concatenate(ce, axis=0)
        top_s, top_c = _top_rows(cand, combo, combo, PEER_TOPK)
        top_e = top_c - jnp.floor(top_c * (1.0 / n_exp)) * n_exp
        p = jnp.exp(top_s - top_s[0:1])
        gates = p / jnp.sum(p, axis=0, keepdims=True)
        e_ref[h * PEER_TOPK:(h + 1) * PEER_TOPK, :] = top_e.astype(I32)
        g_ref[h * PEER_TOPK:(h + 1) * PEER_TOPK, :] = gates


def _route(qp, k1, k2, tt):
    n, nq = qp.shape
    n_sel = PEER_HEADS * PEER_TOPK
    full = lambda a: pl.BlockSpec(a.shape, lambda i: (0,) * a.ndim)
    return pl.pallas_call(
        functools.partial(_route_kernel, tt=tt),
        grid=(n // tt,),
        in_specs=[pl.BlockSpec((tt, nq), lambda i: (i, 0)), full(k1), full(k2)],
        out_specs=[pl.BlockSpec((n_sel, tt), lambda i: (0, i)), pl.BlockSpec((n_sel, tt), lambda i: (0, i))],
        out_shape=[jax.ShapeDtypeStruct((n_sel, n), I32), jax.ShapeDtypeStruct((n_sel, n), F32)],
        compiler_params=_cparams(("arbitrary",)),
        name="route",
    )(qp, k1, k2)


def _final_kernel(h_ref, p_ref, g_ref, o_ref):
    tm, d = h_ref.shape
    n_seg = d // LANES
    p = jnp.concatenate([p_ref[pl.ds(k, tm, stride=n_seg), :] for k in range(n_seg)], axis=1)
    o_ref[...] = _rms(h_ref[...] + p, g_ref[...])


def _final(h, p, g, tm):
    n, d = h.shape
    row = pl.BlockSpec((tm, d), lambda i: (i, 0))
    return pl.pallas_call(
        _final_kernel,
        grid=(n // tm,),
        in_specs=[row, pl.BlockSpec((tm * (d // LANES), LANES), lambda i: (i, 0)),
                  pl.BlockSpec((1, d), lambda i: (0, 0))],
        out_specs=row,
        out_shape=jax.ShapeDtypeStruct((n, d), F32),
        compiler_params=_cparams(("arbitrary",)),
        name="final",
    )(h, p, g.reshape(1, d))


SC_CORES_V7X = 2
SC_SUBCORES_V7X = 16
SC_LANES_V7X = 16
PEER_TOK_BATCH = 32
PEER_ROW_CHUNK = 32
PEER_RING = 4


def _pack_words(x):
    bits = lax.bitcast_convert_type(x.astype(BF16).astype(F32), I32)
    half = bits.shape[1] // 2
    return (bits[:, half:] & jnp.int32(-65536)) | lax.shift_right_logical(bits[:, :half], 16)


def _pack_kernel(t_ref, o_ref):
    o_ref[...] = _pack_words(t_ref[...])


def _pack_bf16_pairs(t, rows=1024):
    e, d = t.shape
    return pl.pallas_call(
        _pack_kernel,
        grid=(e // rows,),
        in_specs=[pl.BlockSpec((rows, d), lambda i: (i, 0))],
        out_specs=pl.BlockSpec((rows, d // 2), lambda i: (i, 0)),
        out_shape=jax.ShapeDtypeStruct((e, d // 2), I32),
        compiler_params=_cparams(("arbitrary",)),
        name="pack_table",
    )(t)


def _unpack_pair(w):
    lo = lax.bitcast_convert_type(jnp.left_shift(w, 16), F32)
    hi = lax.bitcast_convert_type(w & jnp.int32(-65536), F32)
    return lo, hi


def _peer_sc_body(hn_hbm, e_hbm, g_hbm, u_hbm, v_hbm, out_hbm,
                  idx_v, gate_v, x_v, out_v, rows, p_v, act_v, sem, *, tpw, d, n_sel):
    nl = SC_LANES_V7X
    tb = PEER_TOK_BATCH
    rc = PEER_ROW_CHUNK
    n_chunk = n_sel // rc
    jobs_per_tok = 2 * n_chunk
    half = d // 2
    n_lane_blk = half // nl
    xs = half // LANES
    os_ = d // LANES
    wid =lax.axis_index("s") * SC_CORES_V7X + lax.axis_index("c")
    base = wid * tpw
    lane = lax.iota(I32, nl)
    zero = jnp.zeros((nl,), F32)
    c_gelu = 2.0 * math.sqrt(2.0 / math.pi)

    def gather_copy(tab_hbm, job):
        tok = job // jobs_per_tok
        c = (job % jobs_per_tok) % n_chunk
        b = job % PEER_RING
        return pltpu.make_async_copy(tab_hbm.at[idx_v.at[tok, pl.ds(c * rc, rc)]], rows.at[b], sem.at[b])

    def start(job):
        j = job % jobs_per_tok

        @pl.when(j < n_chunk)
        def _():
            gather_copy(u_hbm, job).start()

        @pl.when(j >= n_chunk)
        def _():
            gather_copy(v_hbm, job).start()

    def compute_u(tok, c, b):
        def rg_body(rg, _):
            r0 = rg * 8

            def jbody(j2, accs):
                off0 = j2 * (2 * nl)
                off1 = off0 + nl
                xrow = tok * xs + off0 // LANES
                xl = off0 % LANES
                x0 = plsc.bitcast(x_v[xrow, pl.ds(xl, nl)], BF16)
                x1 = plsc.bitcast(x_v[xrow, pl.ds(xl + nl, nl)], BF16)
                new = []
                for r in range(8):
                    w0 = plsc.bitcast(rows[b, r0 + r, pl.ds(off0, nl)], BF16)
                    w1 = plsc.bitcast(rows[b, r0 + r, pl.ds(off1, nl)], BF16)
                    lo, hi = _unpack_pair(plsc.bitcast(w0 * x0 + w1 * x1, I32))
                    new.append(accs[r] + (lo + hi))
                return tuple(new)

            accs = lax.fori_loop(0, n_lane_blk // 2, jbody, (zero,) * 8)
            for r in range(8):
                p_v[c * rc + r0 + r, :] = accs[r]
            return 0

        lax.fori_loop(0, rc // 8, rg_body, 0)

    def finish_act(tok):
        def eg_body(eg, _):
            e0 = eg * nl
            ridx = e0 + lane
            s = zero
            for l in range(nl):
                s = s + plsc.load_gather(p_v, [ridx, jnp.full((nl,), l, I32)])
            inner = c_gelu * (s + 0.044715 * (s * s * s))
            gl = s / (1.0 + jnp.exp(-inner))
            a = gl * gate_v[tok, pl.ds(e0, nl)]
            bits = lax.bitcast_convert_type(a, I32)
            rnd = bits + jnp.int32(0x7FFF) + (lax.shift_right_logical(bits, 16) & 1)
            hi16 = rnd & jnp.int32(-65536)
            act_v[pl.ds(e0, nl)] = hi16 | lax.shift_right_logical(hi16, 16)
            return 0

        lax.fori_loop(0, n_sel // nl, eg_body, 0)

        def zbody(j, _):
            off = j * nl
            out_v[tok * os_ + off // LANES, pl.ds(off % LANES, nl)] = zero
            return 0

        lax.fori_loop(0, d // nl, zbody, 0, unroll=4)

    def compute_v(tok, c, b):
        def rg_body(rg, _):
            r0 = rg * nl
            splat = [plsc.bitcast(plsc.load_gather(act_v, [jnp.full((nl,), 0, I32) + (c * rc + r0 + r)]), BF16)
                     for r in range(nl)]

            def tree(parts):
                while len(parts) > 1:
                    parts = [parts[i] + parts[i + 1] for i in range(0, len(parts), 2)]
                return parts[0]

            @plsc.parallel_loop(0, n_lane_blk, unroll=2)
            def _(j):
                off = j * nl
                los, his = [], []
                for r in range(0, nl, 2):
                    w0 = plsc.bitcast(rows[b, r0 + r, pl.ds(off, nl)], BF16)
                    w1 = plsc.bitcast(rows[b, r0 + r + 1, pl.ds(off, nl)], BF16)
                    lo, hi = _unpack_pair(plsc.bitcast(w0 * splat[r] + w1 * splat[r + 1], I32))
                    los.append(lo)
                    his.append(hi)
                orow = tok * os_ + off // LANES
                ol = off % LANES
                out_v[orow, pl.ds(ol, nl)] = out_v[orow, pl.ds(ol, nl)] + tree(los)
                out_v[orow + xs, pl.ds(ol, nl)] = out_v[orow + xs, pl.ds(ol, nl)] + tree(his)

            return 0

        lax.fori_loop(0, rc // nl, rg_body, 0)

    def batch_body(bi, _):
        t0 = base + bi * tb
        pltpu.sync_copy(e_hbm.at[pl.ds(t0, tb)], idx_v)
        pltpu.sync_copy(g_hbm.at[pl.ds(t0, tb)], gate_v)
        pltpu.sync_copy(hn_hbm.at[pl.ds(t0 * xs, tb * xs)], x_v)
        for pre in range(PEER_RING - 1):
            start(pre)

        def job_body(job, _):
            @pl.when(job + (PEER_RING - 1) < tb * jobs_per_tok)
            def _():
                start(job + (PEER_RING - 1))

            j = job % jobs_per_tok
            gather_copy(u_hbm, job).wait()
            tok = job // jobs_per_tok
            b = job % PEER_RING

            @pl.when(j < n_chunk)
            def _():
                compute_u(tok, j, b)

            @pl.when(j == n_chunk - 1)
            def _():
                finish_act(tok)

            @pl.when(j >= n_chunk)
            def _():
                compute_v(tok, j - n_chunk, b)

            return 0

        lax.fori_loop(0, tb * jobs_per_tok, job_body, 0)
        pltpu.sync_copy(out_v, out_hbm.at[pl.ds(t0 * os_, tb * os_)])
        return 0

    lax.fori_loop(0, tpw // tb, batch_body, 0)


def _peer_sc(x_pk, experts, gates, u_tab, v_tab):
    d = 2 * u_tab.shape[1]
    n = x_pk.shape[0] * LANES // (d // 2)
    n_sel = experts.shape[1]
    nw = SC_CORES_V7X * SC_SUBCORES_V7X
    tpw = n // nw
    mesh = plsc.VectorSubcoreMesh(core_axis_name="c", subcore_axis_name="s",
                                  num_cores=SC_CORES_V7X, num_subcores=SC_SUBCORES_V7X)
    body = functools.partial(_peer_sc_body, tpw=tpw, d=d, n_sel=n_sel)
    call = pl.kernel(
        body,
        out_type=jax.ShapeDtypeStruct((n * d // LANES, LANES), F32),
        mesh=mesh,
        scratch_types=[pltpu.VMEM((PEER_TOK_BATCH, n_sel), I32),
                       pltpu.VMEM((PEER_TOK_BATCH, n_sel), F32),
                       pltpu.VMEM((PEER_TOK_BATCH * d // 2 // LANES, LANES), I32),
                       pltpu.VMEM((PEER_TOK_BATCH * d // LANES, LANES), F32),
                       pltpu.VMEM((PEER_RING, PEER_ROW_CHUNK, d // 2), I32),
                       pltpu.VMEM((n_sel, SC_LANES_V7X), F32),
                       pltpu.VMEM((n_sel,), I32),
                       pltpu.SemaphoreType.DMA((PEER_RING,))],
        compiler_params=pltpu.CompilerParams(needs_layout_passes=False, use_tc_tiling_on_sc=False),
        name="peer_sc",
    )
    return call(x_pk, experts, gates, u_tab, v_tab)


def kernel(x, norm1_g, w_in, a_re, a_im, log_dt, b_re, b_im, c_re, c_im, d_skip, w_glu, w_ssm_up, w_attn_up,
           w_out, norm2_g, peer_wq, peer_k1, peer_k2, peer_u, peer_v, norm_f_g):
    bsz, seq, d = x.shape
    depth = norm1_g.shape[0]
    chunks = _time_chunks(seq)
    h = x
    for layer in range(depth):
        last = layer + 1 == depth
        kv_w, main_w = _in_weights(w_in[layer], seq)
        s5p = _s5_params(a_re[layer], a_im[layer], log_dt[layer], b_re[layer], b_im[layer], c_re[layer],
                         c_im[layer], d_skip[layer], w_glu[layer], nb=bsz)
        n_state = s5p[4].shape[1]
        wsu = w_ssm_up[layer].astype(BF16)
        wau = w_attn_up[layer].astype(BF16)
        wo = w_out[layer].astype(BF16)
        wq = peer_wq[layer].astype(BF16)
        k1 = peer_k1[layer].astype(BF16)
        k2 = peer_k2[layer].astype(BF16)
        u_pk = _pack_bf16_pairs(peer_u[layer])
        v_pk = _pack_bf16_pairs(peer_v[layer])
        kv_tm = min(ROW_TILE, seq)
        head_len = 2 * DSA_KT
        split_kv = len(chunks) > 1 and chunks[0][1] <= head_len < seq
        k4, vt, ki = _kvproj(h, norm1_g[layer], kv_w, norm1_g[layer], seq=head_len if split_kv else seq,
                             tm=kv_tm, kt=DSA_KT)
        st_re = jnp.zeros((bsz, n_state), F32)
        st_im = jnp.zeros((bsz, n_state), F32)
        outs = []
        routed = ki
        peer_outs = []
        for c, (s0, sc) in enumerate(chunks):
            tm = math.gcd(math.gcd(s0, sc), ROW_TILE)
            if split_kv and s0 + sc > head_len:
                k4, vt, ki = _kvproj(h, norm1_g[layer], kv_w, routed, seq=seq, tm=kv_tm, kt=DSA_KT)
                split_kv = False
            after = (routed, peer_outs[c - SC_LAG] if c >= SC_LAG else ki)
            u, q, qi, wit, gs, ga = _inproj(h, norm1_g[layer], main_w, after, s0=s0, sc=sc, tm=tm)
            d_ssm = u.shape[-1]
            u_tb = u.transpose(1, 0, 2).reshape(sc * bsz, d_ssm)
            y_tb, st_re, st_im = _s5(u_tb, st_re, st_im, s5p, nb=bsz, tc=S5_STEPS)
            ys = y_tb.reshape(sc, bsz, d_ssm).transpose(1, 0, 2)
            ya = _dsa(q, qi, wit, ki, k4, vt, s0=s0, seq_total=seq, tq=DSA_TQ, kt=DSA_KT)
            hm, x_pk, qp = _merge(h, ys, ya, gs, ga, wsu, wau, wo, norm2_g[layer], wq, s0=s0, tm=tm)
            nt = bsz * sc
            e_t, g_t = _route(qp.reshape(nt, -1), k1, k2, tt=ROUTE_TOKENS)
            routed = e_t
            po = _peer_sc(x_pk.reshape(-1, LANES), e_t.T, g_t.T, u_pk, v_pk)
            peer_outs.append(po)
            hm2 = hm.reshape(nt, d)
            o = _final(hm2, po, norm_f_g, tm=tm) if last else hm2 + po.reshape(nt, d)
            outs.append(o.reshape(bsz, sc, d))
        h = jnp.concatenate(outs, axis=1)
    return h
```
